```python
import jax, jax.numpy as jnp
from jax import lax
import numpy as np

D_MODEL = 2048
BATCH = 8
SEQ = 4096
DEPTH = 1

HEAD_DIM = 128
D_ATTN = D_MODEL // 2
N_ATTN_HEADS = D_ATTN // HEAD_DIM
D_GMLP = D_MODEL - D_ATTN
N_GMLP_HEADS = D_GMLP // HEAD_DIM
D_MIX = D_ATTN + D_GMLP
CHUNK = 128
Q_BLOCK = 128
D_FF = 4 * D_MODEL
D_IN_PROJ = 3 * D_ATTN + N_ATTN_HEADS + 2 * D_GMLP
EPS = 1e-6

kernel_name = "hymba_fox_gmlp_hybrid_block"


def rmsnorm(x, g):
    xf = x.astype(jnp.float32)
    y = xf * lax.rsqrt(jnp.mean(xf * xf, axis=-1, keepdims=True) + EPS)
    return (y * g.astype(jnp.float32)).astype(x.dtype)


def layernorm(x, g, b):
    xf = x.astype(jnp.float32)
    mu = jnp.mean(xf, axis=-1, keepdims=True)
    xc = xf - mu
    y = xc * lax.rsqrt(jnp.mean(xc * xc, axis=-1, keepdims=True) + EPS)
    return (y * g.astype(jnp.float32) + b.astype(jnp.float32)).astype(x.dtype)


def forgetting_attention(q, k, v, log_f):
    B, S, H, D = q.shape
    nb = S // Q_BLOCK
    scale = 1.0 / np.sqrt(D).astype(np.float32)
    F = jnp.cumsum(log_f, axis=1).transpose(0, 2, 1)
    q_blocks = q.reshape(B, nb, Q_BLOCK, H, D).transpose(1, 0, 2, 3, 4)
    F_blocks = F.reshape(B, H, nb, Q_BLOCK).transpose(2, 0, 1, 3)
    k_pos = jnp.arange(S)

    def one_block(args):
        qi, Fq, i = args
        s = jnp.einsum('bqhd,bkhd->bhqk', qi, k, preferred_element_type=jnp.float32) * scale
        s = s + Fq[..., :, None] - F[:, :, None, :]
        q_pos = i * Q_BLOCK + jnp.arange(Q_BLOCK)
        causal = k_pos[None, :] <= q_pos[:, None]
        s = jnp.where(causal[None, None], s, -jnp.inf)
        p = jax.nn.softmax(s, axis=-1)
        return jnp.einsum('bhqk,bkhd->bqhd', p.astype(v.dtype), v)

    out = lax.map(one_block, (q_blocks, F_blocks, jnp.arange(nb)))
    return out.transpose(1, 0, 2, 3, 4).reshape(B, S, H * D)


def chunked_spatial_gating(zu, zv, ln_g, ln_b, w_s, b_s):
    B, S, _ = zu.shape
    nc = S // CHUNK
    u = jax.nn.gelu(zu)
    v = layernorm(jax.nn.gelu(zv), ln_g, ln_b)
    v = v.reshape(B, nc, CHUNK, N_GMLP_HEADS, HEAD_DIM)
    w_causal = jnp.tril(w_s)
    mix = jnp.einsum('hts,bcshd->bcthd', w_causal.astype(v.dtype), v)
    mix = mix + b_s.T[None, None, :, :, None]
    out = u.reshape(B, nc, CHUNK, N_GMLP_HEADS, HEAD_DIM) * mix
    return out.reshape(B, S, D_GMLP)


def _fwd_setup_inputs(seed: int = 0) -> dict:
    key = jax.random.key(seed)
    ks = jax.random.split(key, 20)
    L = DEPTH
    nrm = jax.random.normal
    x = nrm(ks[0], (BATCH, SEQ, D_MODEL), jnp.float32)
    norm_mix_g = 1.0 + 0.02 * nrm(ks[1], (L, D_MODEL), jnp.float32)
    w_qkv = nrm(ks[2], (L, D_MODEL, 3 * D_ATTN), jnp.float32) * D_MODEL ** -0.5
    w_f = nrm(ks[3], (L, D_MODEL, N_ATTN_HEADS), jnp.float32) * 0.1 * D_MODEL ** -0.5
    w_g = nrm(ks[4], (L, D_MODEL, 2 * D_GMLP), jnp.float32) * D_MODEL ** -0.5
    w_in = jnp.concatenate([w_qkv, w_f, w_g], axis=-1)
    b_f = jax.random.uniform(ks[5], (L, N_ATTN_HEADS), jnp.float32, 1.0, 5.0)
    gmlp_ln_g = 1.0 + 0.02 * nrm(ks[6], (L, D_GMLP), jnp.float32)
    gmlp_ln_b = 0.02 * nrm(ks[7], (L, D_GMLP), jnp.float32)
    w_s = nrm(ks[8], (L, N_GMLP_HEADS, CHUNK, CHUNK), jnp.float32) * CHUNK ** -0.5
    b_s = 1.0 + 0.1 * nrm(ks[9], (L, N_GMLP_HEADS, CHUNK), jnp.float32)
    attn_out_g = 1.0 + 0.02 * nrm(ks[10], (L, D_ATTN), jnp.float32)
    gmlp_out_g = 1.0 + 0.02 * nrm(ks[11], (L, D_GMLP), jnp.float32)
    w_out = nrm(ks[12], (L, D_MIX, D_MODEL), jnp.float32) * D_MIX ** -0.5
    norm_ffn_g = 1.0 + 0.02 * nrm(ks[13], (L, D_MODEL), jnp.float32)
    w_ff1 = nrm(ks[14], (L, D_MODEL, D_FF), jnp.float32) * D_MODEL ** -0.5
    w_ff2 = nrm(ks[15], (L, D_FF, D_MODEL), jnp.float32) * D_FF ** -0.5
    norm_final_g = 1.0 + 0.02 * nrm(ks[16], (D_MODEL,), jnp.float32)
    return {"x": x, "norm_mix_g": norm_mix_g, "w_in": w_in, "b_f": b_f,
            "gmlp_ln_g": gmlp_ln_g, "gmlp_ln_b": gmlp_ln_b, "w_s": w_s, "b_s": b_s,
            "attn_out_g": attn_out_g, "gmlp_out_g": gmlp_out_g, "w_out": w_out,
            "norm_ffn_g": norm_ffn_g, "w_ff1": w_ff1, "w_ff2": w_ff2,
            "norm_final_g": norm_final_g}


def _fwd_reference(x, norm_mix_g, w_in, b_f, gmlp_ln_g, gmlp_ln_b, w_s, b_s,
              attn_out_g, gmlp_out_g, w_out, norm_ffn_g, w_ff1, w_ff2, norm_final_g):
    B, S, _ = x.shape
    for l in range(DEPTH):
        h = rmsnorm(x, norm_mix_g[l])
        z = jnp.einsum('bsd,de->bse', h, w_in[l])
        o1 = D_ATTN; o2 = 2 * D_ATTN; o3 = 3 * D_ATTN; o4 = o3 + N_ATTN_HEADS
        q = z[..., :o1].reshape(B, S, N_ATTN_HEADS, HEAD_DIM)
        k = z[..., o1:o2].reshape(B, S, N_ATTN_HEADS, HEAD_DIM)
        v = z[..., o2:o3].reshape(B, S, N_ATTN_HEADS, HEAD_DIM)
        log_f = jax.nn.log_sigmoid(z[..., o3:o4].astype(jnp.float32) + b_f[l].astype(jnp.float32))
        zu = z[..., o4:o4 + D_GMLP]
        zv = z[..., o4 + D_GMLP:]
        attn = forgetting_attention(q, k, v, log_f)
        gm = chunked_spatial_gating(zu, zv, gmlp_ln_g[l], gmlp_ln_b[l], w_s[l], b_s[l])
        merged = jnp.concatenate([rmsnorm(attn, attn_out_g[l]),
                                  rmsnorm(gm, gmlp_out_g[l])], axis=-1)
        x = x + jnp.einsum('bse,ed->bsd', merged, w_out[l])
        h2 = rmsnorm(x, norm_ffn_g[l])
        a = jax.nn.relu(jnp.einsum('bsd,df->bsf', h2, w_ff1[l]))
        x = x + jnp.einsum('bsf,fd->bsd', a * a, w_ff2[l])
    return rmsnorm(x, norm_final_g)


import jax as _jax
import jax.numpy as _jnp

TWIN_FORMAT = 'train_step'
FWD_PARAMS = ['x', 'norm_mix_g', 'w_in', 'b_f', 'gmlp_ln_g', 'gmlp_ln_b', 'w_s', 'b_s', 'attn_out_g', 'gmlp_out_g', 'w_out', 'norm_ffn_g', 'w_ff1', 'w_ff2', 'norm_final_g']
TWIN_WEIGHTS = ['norm_mix_g', 'w_in', 'b_f', 'gmlp_ln_g', 'gmlp_ln_b', 'w_s', 'b_s', 'attn_out_g', 'gmlp_out_g', 'w_out', 'norm_ffn_g', 'w_ff1', 'w_ff2', 'norm_final_g']
TWIN_DIFF_INPUT = 'x'
TWIN_INPUTS = ['x', 'norm_mix_g', 'w_in', 'b_f', 'gmlp_ln_g', 'gmlp_ln_b', 'w_s', 'b_s', 'attn_out_g', 'gmlp_out_g', 'w_out', 'norm_ffn_g', 'w_ff1', 'w_ff2', 'norm_final_g', 'loss_target', 'm_norm_mix_g', 'm_w_in', 'm_b_f', 'm_gmlp_ln_g', 'm_gmlp_ln_b', 'm_w_s', 'm_b_s', 'm_attn_out_g', 'm_gmlp_out_g', 'm_w_out', 'm_norm_ffn_g', 'm_w_ff1', 'm_w_ff2', 'm_norm_final_g', 'v_norm_mix_g', 'v_w_in', 'v_b_f', 'v_gmlp_ln_g', 'v_gmlp_ln_b', 'v_w_s', 'v_b_s', 'v_attn_out_g', 'v_gmlp_out_g', 'v_w_out', 'v_norm_ffn_g', 'v_w_ff1', 'v_w_ff2', 'v_norm_final_g']
TWIN_OUTPUTS = ['loss', 'grad_x', 'grad_norm_mix_g', 'grad_w_in', 'grad_b_f', 'grad_gmlp_ln_g', 'grad_gmlp_ln_b', 'grad_w_s', 'grad_b_s', 'grad_attn_out_g', 'grad_gmlp_out_g', 'grad_w_out', 'grad_norm_ffn_g', 'grad_w_ff1', 'grad_w_ff2', 'grad_norm_final_g', 'delta_norm_mix_g', 'delta_w_in', 'delta_b_f', 'delta_gmlp_ln_g', 'delta_gmlp_ln_b', 'delta_w_s', 'delta_b_s', 'delta_attn_out_g', 'delta_gmlp_out_g', 'delta_w_out', 'delta_norm_ffn_g', 'delta_w_ff1', 'delta_w_ff2', 'delta_norm_final_g', 'new_m_norm_mix_g', 'new_m_w_in', 'new_m_b_f', 'new_m_gmlp_ln_g', 'new_m_gmlp_ln_b', 'new_m_w_s', 'new_m_b_s', 'new_m_attn_out_g', 'new_m_gmlp_out_g', 'new_m_w_out', 'new_m_norm_ffn_g', 'new_m_w_ff1', 'new_m_w_ff2', 'new_m_norm_final_g', 'new_v_norm_mix_g', 'new_v_w_in', 'new_v_b_f', 'new_v_gmlp_ln_g', 'new_v_gmlp_ln_b', 'new_v_w_s', 'new_v_b_s', 'new_v_attn_out_g', 'new_v_gmlp_out_g', 'new_v_w_out', 'new_v_norm_ffn_g', 'new_v_w_ff1', 'new_v_w_ff2', 'new_v_norm_final_g']
TWIN_LEAF_KINDS = {'loss': 'loss', 'grad_x': 'grad_x', 'grad_norm_mix_g': 'grad_w', 'grad_w_in': 'grad_w', 'grad_b_f': 'grad_w', 'grad_gmlp_ln_g': 'grad_w', 'grad_gmlp_ln_b': 'grad_w', 'grad_w_s': 'grad_w', 'grad_b_s': 'grad_w', 'grad_attn_out_g': 'grad_w', 'grad_gmlp_out_g': 'grad_w', 'grad_w_out': 'grad_w', 'grad_norm_ffn_g': 'grad_w', 'grad_w_ff1': 'grad_w', 'grad_w_ff2': 'grad_w', 'grad_norm_final_g': 'grad_w', 'delta_norm_mix_g': 'delta_w', 'delta_w_in': 'delta_w', 'delta_b_f': 'delta_w', 'delta_gmlp_ln_g': 'delta_w', 'delta_gmlp_ln_b': 'delta_w', 'delta_w_s': 'delta_w', 'delta_b_s': 'delta_w', 'delta_attn_out_g': 'delta_w', 'delta_gmlp_out_g': 'delta_w', 'delta_w_out': 'delta_w', 'delta_norm_ffn_g': 'delta_w', 'delta_w_ff1': 'delta_w', 'delta_w_ff2': 'delta_w', 'delta_norm_final_g': 'delta_w', 'new_m_norm_mix_g': 'new_m', 'new_m_w_in': 'new_m', 'new_m_b_f': 'new_m', 'new_m_gmlp_ln_g': 'new_m', 'new_m_gmlp_ln_b': 'new_m', 'new_m_w_s': 'new_m', 'new_m_b_s': 'new_m', 'new_m_attn_out_g': 'new_m', 'new_m_gmlp_out_g': 'new_m', 'new_m_w_out': 'new_m', 'new_m_norm_ffn_g': 'new_m', 'new_m_w_ff1': 'new_m', 'new_m_w_ff2': 'new_m', 'new_m_norm_final_g': 'new_m', 'new_v_norm_mix_g': 'new_v', 'new_v_w_in': 'new_v', 'new_v_b_f': 'new_v', 'new_v_gmlp_ln_g': 'new_v', 'new_v_gmlp_ln_b': 'new_v', 'new_v_w_s': 'new_v', 'new_v_b_s': 'new_v', 'new_v_attn_out_g': 'new_v', 'new_v_gmlp_out_g': 'new_v', 'new_v_w_out': 'new_v', 'new_v_norm_ffn_g': 'new_v', 'new_v_w_ff1': 'new_v', 'new_v_w_ff2': 'new_v', 'new_v_norm_final_g': 'new_v'}


def _forward(args):
    return _fwd_reference(*[args[k] for k in FWD_PARAMS])


def _output_shape():
    def fwd():
        inp = _fwd_setup_inputs(0)
        return _fwd_reference(*[inp[k] for k in FWD_PARAMS])
    out = _jax.eval_shape(fwd)
    return out.shape, out.dtype

N_MICROBATCH = 1
ADAM_LR = 0.001
ADAM_B1 = 0.9
ADAM_B2 = 0.999
ADAM_EPS = 1e-08
ADAM_WD = 0.01
ADAM_STEP = 10
PER_EXAMPLE_BATCH_AXIS = {'x': 0, 'loss_target': 0}
SHARED_INPUTS = []
_WEIGHT_DTYPES = {'norm_mix_g': _jnp.float32, 'w_in': _jnp.float32, 'b_f': _jnp.float32, 'gmlp_ln_g': _jnp.float32, 'gmlp_ln_b': _jnp.float32, 'w_s': _jnp.float32, 'b_s': _jnp.float32, 'attn_out_g': _jnp.float32, 'gmlp_out_g': _jnp.float32, 'w_out': _jnp.float32, 'norm_ffn_g': _jnp.float32, 'w_ff1': _jnp.float32, 'w_ff2': _jnp.float32, 'norm_final_g': _jnp.float32}
MOMENT_SCALE = {'norm_mix_g': 9.500079e-02, 'w_in': 5.943159e-02, 'b_f': 1.213585e+00, 'gmlp_ln_g': 3.843958e-02, 'gmlp_ln_b': 3.818028e-02, 'w_s': 3.731534e-02, 'b_s': 5.445319e-02, 'attn_out_g': 7.465345e-02, 'gmlp_out_g': 7.489515e-02, 'w_out': 7.023333e-02, 'norm_ffn_g': 6.701537e-02, 'w_ff1': 3.397012e-02, 'w_ff2': 6.873847e-02, 'norm_final_g': 1.613985e+01}


def _to_microbatches(a, axis):
    t = _jnp.moveaxis(a, axis, 0)
    t = t.reshape((N_MICROBATCH, t.shape[0] // N_MICROBATCH) + t.shape[1:])
    return _jnp.moveaxis(t, 1, axis + 1)


def setup_inputs(seed: int = 0) -> dict:
    inp = _fwd_setup_inputs(seed)
    key = _jax.random.fold_in(_jax.random.key(seed), 7919)
    shape, _ = _output_shape()
    out = dict(inp)
    out["loss_target"] = _jax.random.normal(_jax.random.fold_in(key, 0), shape, _jnp.float32)
    for i, name in enumerate(TWIN_WEIGHTS):
        w = inp[name].astype(_jnp.float32)
        if MOMENT_SCALE is None:
            s = _jnp.sqrt(_jnp.mean(_jnp.square(w)) + 1e-30)
        else:
            s = MOMENT_SCALE[name]
        km, kv = _jax.random.split(_jax.random.fold_in(key, i + 1))
        out[name] = w
        out["m_" + name] = s * _jax.random.normal(km, w.shape, _jnp.float32)
        out["v_" + name] = (s * s) * _jax.random.uniform(kv, w.shape, _jnp.float32, 0.5, 1.5)
    if N_MICROBATCH > 1:
        for name, axis in PER_EXAMPLE_BATCH_AXIS.items():
            out[name] = _to_microbatches(out[name], axis)
    return {'x': out['x'], 'norm_mix_g': out['norm_mix_g'], 'w_in': out['w_in'], 'b_f': out['b_f'], 'gmlp_ln_g': out['gmlp_ln_g'], 'gmlp_ln_b': out['gmlp_ln_b'], 'w_s': out['w_s'], 'b_s': out['b_s'], 'attn_out_g': out['attn_out_g'], 'gmlp_out_g': out['gmlp_out_g'], 'w_out': out['w_out'], 'norm_ffn_g': out['norm_ffn_g'], 'w_ff1': out['w_ff1'], 'w_ff2': out['w_ff2'], 'norm_final_g': out['norm_final_g'], 'loss_target': out['loss_target'], 'm_norm_mix_g': out['m_norm_mix_g'], 'm_w_in': out['m_w_in'], 'm_b_f': out['m_b_f'], 'm_gmlp_ln_g': out['m_gmlp_ln_g'], 'm_gmlp_ln_b': out['m_gmlp_ln_b'], 'm_w_s': out['m_w_s'], 'm_b_s': out['m_b_s'], 'm_attn_out_g': out['m_attn_out_g'], 'm_gmlp_out_g': out['m_gmlp_out_g'], 'm_w_out': out['m_w_out'], 'm_norm_ffn_g': out['m_norm_ffn_g'], 'm_w_ff1': out['m_w_ff1'], 'm_w_ff2': out['m_w_ff2'], 'm_norm_final_g': out['m_norm_final_g'], 'v_norm_mix_g': out['v_norm_mix_g'], 'v_w_in': out['v_w_in'], 'v_b_f': out['v_b_f'], 'v_gmlp_ln_g': out['v_gmlp_ln_g'], 'v_gmlp_ln_b': out['v_gmlp_ln_b'], 'v_w_s': out['v_w_s'], 'v_b_s': out['v_b_s'], 'v_attn_out_g': out['v_attn_out_g'], 'v_gmlp_out_g': out['v_gmlp_out_g'], 'v_w_out': out['v_w_out'], 'v_norm_ffn_g': out['v_norm_ffn_g'], 'v_w_ff1': out['v_w_ff1'], 'v_w_ff2': out['v_w_ff2'], 'v_norm_final_g': out['v_norm_final_g']}


def _loss(weights, diff, rest, loss_target):
    with _jax.named_scope("forward"):
        args = {**rest, TWIN_DIFF_INPUT: diff, **{k: w.astype(_WEIGHT_DTYPES[k]) for k, w in weights.items()}}
        y = _forward(args)
    with _jax.named_scope("loss_head"):
        err = _jnp.square(y.astype(_jnp.float32) - loss_target)
        return 0.5 * _jnp.sum(_jnp.mean(err, axis=-1)) if err.ndim else 0.5 * err


def _adamw(w, g, m, v):
    m = ADAM_B1 * m + (1.0 - ADAM_B1) * g
    v = ADAM_B2 * v + (1.0 - ADAM_B2) * _jnp.square(g)
    m_hat = m / (1.0 - ADAM_B1 ** ADAM_STEP)
    v_hat = v / (1.0 - ADAM_B2 ** ADAM_STEP)
    delta = -ADAM_LR * (m_hat / (_jnp.sqrt(v_hat) + ADAM_EPS) + ADAM_WD * w)
    return delta, m, v


def reference(x, norm_mix_g, w_in, b_f, gmlp_ln_g, gmlp_ln_b, w_s, b_s, attn_out_g, gmlp_out_g, w_out, norm_ffn_g, w_ff1, w_ff2, norm_final_g, loss_target, m_norm_mix_g, m_w_in, m_b_f, m_gmlp_ln_g, m_gmlp_ln_b, m_w_s, m_b_s, m_attn_out_g, m_gmlp_out_g, m_w_out, m_norm_ffn_g, m_w_ff1, m_w_ff2, m_norm_final_g, v_norm_mix_g, v_w_in, v_b_f, v_gmlp_ln_g, v_gmlp_ln_b, v_w_s, v_b_s, v_attn_out_g, v_gmlp_out_g, v_w_out, v_norm_ffn_g, v_w_ff1, v_w_ff2, v_norm_final_g):
    given = dict(x=x, norm_mix_g=norm_mix_g, w_in=w_in, b_f=b_f, gmlp_ln_g=gmlp_ln_g, gmlp_ln_b=gmlp_ln_b, w_s=w_s, b_s=b_s, attn_out_g=attn_out_g, gmlp_out_g=gmlp_out_g, w_out=w_out, norm_ffn_g=norm_ffn_g, w_ff1=w_ff1, w_ff2=w_ff2, norm_final_g=norm_final_g, loss_target=loss_target, m_norm_mix_g=m_norm_mix_g, m_w_in=m_w_in, m_b_f=m_b_f, m_gmlp_ln_g=m_gmlp_ln_g, m_gmlp_ln_b=m_gmlp_ln_b, m_w_s=m_w_s, m_b_s=m_b_s, m_attn_out_g=m_attn_out_g, m_gmlp_out_g=m_gmlp_out_g, m_w_out=m_w_out, m_norm_ffn_g=m_norm_ffn_g, m_w_ff1=m_w_ff1, m_w_ff2=m_w_ff2, m_norm_final_g=m_norm_final_g, v_norm_mix_g=v_norm_mix_g, v_w_in=v_w_in, v_b_f=v_b_f, v_gmlp_ln_g=v_gmlp_ln_g, v_gmlp_ln_b=v_gmlp_ln_b, v_w_s=v_w_s, v_b_s=v_b_s, v_attn_out_g=v_attn_out_g, v_gmlp_out_g=v_gmlp_out_g, v_w_out=v_w_out, v_norm_ffn_g=v_norm_ffn_g, v_w_ff1=v_w_ff1, v_w_ff2=v_w_ff2, v_norm_final_g=v_norm_final_g)
    weights = {n: given[n] for n in TWIN_WEIGHTS}
    shared = {n: given[n] for n in SHARED_INPUTS}
    per_example = {n: given[n] for n in ['x']}
    grad_fn = _jax.value_and_grad(_loss, argnums=(0, 1))

    def one_microbatch(ex, loss_target):
        ex = dict(ex)
        diff = ex.pop(TWIN_DIFF_INPUT)
        return grad_fn(weights, diff, {**shared, **ex}, loss_target)

    if N_MICROBATCH == 1:
        loss, (grad_w, grad_x) = one_microbatch(per_example, given["loss_target"])
    else:
        def body(carry, xs):
            loss_sum, grad_sum = carry
            l_k, (gw_k, gx_k) = one_microbatch(xs[0], xs[1])
            with _jax.named_scope("update"):
                return (loss_sum + l_k, _jax.tree.map(_jnp.add, grad_sum, gw_k)), gx_k

        init = (_jnp.zeros((), _jnp.float32), _jax.tree.map(_jnp.zeros_like, weights))
        (loss, grad_w), grad_x = _jax.lax.scan(body, init, (per_example, given["loss_target"]))
    with _jax.named_scope("update"):
        delta_w, new_m, new_v = {}, {}, {}
        for n in TWIN_WEIGHTS:
            delta_w[n], new_m[n], new_v[n] = _adamw(weights[n], grad_w[n], given["m_" + n], given["v_" + n])
    return (loss, grad_x, *[grad_w[n] for n in TWIN_WEIGHTS], *[delta_w[n] for n in TWIN_WEIGHTS],
            *[new_m[n] for n in TWIN_WEIGHTS], *[new_v[n] for n in TWIN_WEIGHTS])
```

```python
import functools
import math

import numpy as np
import jax
import jax.numpy as jnp
from jax import lax
from jax.experimental import pallas as pl
from jax.experimental.pallas import tpu as pltpu

HEAD_DIM = 128
CHUNK = 128
EPS = 1e-6
LANES = 128
N_CHIPS = 4
N_DEV = 8
VMEM_LIMIT_BYTES = 56 * 1024 * 1024

ADAM_LR = 0.001
ADAM_B1 = 0.9
ADAM_B2 = 0.999
ADAM_EPS = 1e-08
ADAM_WD = 0.01
ADAM_STEP = 10

BF16 = jnp.bfloat16
F32 = jnp.float32
MESH = pl.DeviceIdType.MESH
ANY = pl.BlockSpec(memory_space=pl.ANY)
NEG_BIG = -1e30


def _params(*sem):
    return pltpu.CompilerParams(dimension_semantics=tuple(sem), vmem_limit_bytes=VMEM_LIMIT_BYTES)


def _tile(n, pref, unit):
    t = (min(pref, n) // unit) * unit
    while t >= unit:
        if n % t == 0:
            return t
        t -= unit
    return n


def _matmul(a, b, *, name, out_dtype, trans_a=False, trans_b=False, tm=1024, tn=1024, tk=512,
            square_lhs=False, relu=False, residual=None, scale2_by=None,
            b_sharded=False, out_sharded=False):
    m, k = (a.shape[1], a.shape[0]) if trans_a else a.shape
    if b_sharded:
        if trans_b:
            n, ks = b.shape[1], b.shape[2]
            assert N_CHIPS * ks == k
        else:
            ns = b.shape[2]
            n = N_CHIPS * ns
            assert b.shape[1] == k
    else:
        n = b.shape[0] if trans_b else b.shape[1]
        assert (b.shape[1] if trans_b else b.shape[0]) == k
    tm, tn, tk = _tile(m, tm, 128), _tile(n, tn, 128), _tile(k, tk, 128)
    nk = k // tk

    if trans_a:
        a_spec = pl.BlockSpec((tk, tm), lambda i, j, kk: (kk, i))
    else:
        a_spec = pl.BlockSpec((tm, tk), lambda i, j, kk: (i, kk))
    if b_sharded and trans_b:
        per = ks // tk
        assert per * tk == ks
        b_spec = pl.BlockSpec((None, tn, tk), lambda i, j, kk: (kk // per, j, kk % per))
    elif b_sharded:
        per = ns // tn
        assert per * tn == ns
        b_spec = pl.BlockSpec((None, tk, tn), lambda i, j, kk: (j // per, kk, j % per))
    elif trans_b:
        b_spec = pl.BlockSpec((tn, tk), lambda i, j, kk: (j, kk))
    else:
        b_spec = pl.BlockSpec((tk, tn), lambda i, j, kk: (kk, j))
    if out_sharded:
        ns_out = n // N_CHIPS
        per_o = ns_out // tn
        assert per_o * tn == ns_out
        out_shape = jax.ShapeDtypeStruct((N_CHIPS, m, ns_out), out_dtype)
        o_spec = pl.BlockSpec((None, tm, tn), lambda i, j, kk: (j // per_o, i, j % per_o))
    else:
        out_shape = jax.ShapeDtypeStruct((m, n), out_dtype)
        o_spec = pl.BlockSpec((tm, tn), lambda i, j, kk: (i, j))
    mn_spec = pl.BlockSpec((tm, tn), lambda i, j, kk: (i, j))

    operands, in_specs = [a, b], [a_spec, b_spec]
    if scale2_by is not None:
        operands.append(scale2_by)
        in_specs.append(mn_spec)
    if residual is not None:
        operands.append(residual)
        in_specs.append(mn_spec)
    dims = (((0 if trans_a else 1,), (1 if trans_b else 0,)), ((), ()))

    def body(*refs):
        a_ref, b_ref = refs[0], refs[1]
        pos = 2
        scale_ref = res_ref = None
        if scale2_by is not None:
            scale_ref = refs[pos]
            pos += 1
        if residual is not None:
            res_ref = refs[pos]
            pos += 1
        o_ref, acc_ref = refs[pos], refs[pos + 1]
        kk = pl.program_id(2)

        @pl.when(kk == 0)
        def _():
            acc_ref[...] = jnp.zeros_like(acc_ref)

        av = a_ref[...]
        if square_lhs:
            av = av.astype(F32)
            av = av * av
        acc_ref[...] += lax.dot_general(av.astype(BF16), b_ref[...].astype(BF16), dims,
                                        preferred_element_type=F32)

        @pl.when(kk == nk - 1)
        def _():
            r = acc_ref[...]
            if relu:
                r = jnp.maximum(r, 0.0)
            if scale_ref is not None:
                r = r * (2.0 * scale_ref[...].astype(F32))
            if res_ref is not None:
                r = r + res_ref[...].astype(F32)
            o_ref[...] = r.astype(out_dtype)

    return pl.pallas_call(
        body, name=name, out_shape=out_shape, grid=(m // tm, n // tn, nk),
        in_specs=in_specs, out_specs=o_spec,
        scratch_shapes=[pltpu.VMEM((tm, tn), F32)],
        compiler_params=_params("parallel", "parallel", "arbitrary"),
    )(*operands)


def _rmsnorm_fwd(x, g, *, name, tr=512):
    s, d = x.shape
    tr = _tile(s, tr, 8)

    def body(x_ref, g_ref, o_ref):
        xv = x_ref[...]
        r = lax.rsqrt(jnp.mean(xv * xv, axis=-1, keepdims=True) + EPS)
        o_ref[...] = ((xv * r) * g_ref[...]).astype(BF16)

    return pl.pallas_call(
        body, name=name, out_shape=jax.ShapeDtypeStruct((s, d), BF16), grid=(s // tr,),
        in_specs=[pl.BlockSpec((tr, d), lambda i: (i, 0)), pl.BlockSpec((1, d), lambda i: (0, 0))],
        out_specs=pl.BlockSpec((tr, d), lambda i: (i, 0)),
        compiler_params=_params("parallel"),
    )(x, g)


def _rms_bwd_rows(dy, xv, g):
    d = xv.shape[-1]
    r = lax.rsqrt(jnp.mean(xv * xv, axis=-1, keepdims=True) + EPS)
    gdy = dy * g
    dot = jnp.sum(gdy * xv, axis=-1, keepdims=True)
    dx = gdy * r - xv * (r * r * r) * (dot / d)
    return dx, dy * (xv * r)


def _rmsnorm_bwd(dy, x, res, g, *, name, tr=256):
    s, d = x.shape
    tr = _tile(s, tr, 8)

    def body(dy_ref, x_ref, res_ref, g_ref, dx_ref, dg_ref):
        @pl.when(pl.program_id(0) == 0)
        def _():
            dg_ref[...] = jnp.zeros_like(dg_ref)

        dx, dg_rows = _rms_bwd_rows(dy_ref[...].astype(F32), x_ref[...], g_ref[...])
        dx_ref[...] = res_ref[...] + dx
        dg_ref[...] += jnp.sum(dg_rows, axis=0, keepdims=True)

    row = pl.BlockSpec((tr, d), lambda i: (i, 0))
    vec = pl.BlockSpec((1, d), lambda i: (0, 0))
    return pl.pallas_call(
        body, name=name,
        out_shape=(jax.ShapeDtypeStruct((s, d), F32), jax.ShapeDtypeStruct((1, d), F32)),
        grid=(s // tr,), in_specs=[row, row, row, vec], out_specs=(row, vec),
        compiler_params=_params("arbitrary"),
    )(dy, x, res, g)


def _loss_and_final_bwd(x2, target, g, *, name, tr=256):
    s, d = x2.shape
    tr = _tile(s, tr, 8)

    def body(x_ref, t_ref, g_ref, dx_ref, dg_ref, loss_ref):
        @pl.when(pl.program_id(0) == 0)
        def _():
            dg_ref[...] = jnp.zeros_like(dg_ref)
            loss_ref[...] = jnp.zeros_like(loss_ref)

        xv, gv = x_ref[...], g_ref[...]
        r = lax.rsqrt(jnp.mean(xv * xv, axis=-1, keepdims=True) + EPS)
        err = (xv * r) * gv - t_ref[...]
        row_loss = jnp.mean(err * err, axis=-1, keepdims=True)
        loss_ref[...] += 0.5 * jnp.sum(row_loss, axis=0, keepdims=True)
        dx, dg_rows = _rms_bwd_rows(err / d, xv, gv)
        dx_ref[...] = dx
        dg_ref[...] += jnp.sum(dg_rows, axis=0, keepdims=True)

    row = pl.BlockSpec((tr, d), lambda i: (i, 0))
    vec = pl.BlockSpec((1, d), lambda i: (0, 0))
    one = pl.BlockSpec((1, 1), lambda i: (0, 0))
    return pl.pallas_call(
        body, name=name,
        out_shape=(jax.ShapeDtypeStruct((s, d), F32), jax.ShapeDtypeStruct((1, d), F32),
                   jax.ShapeDtypeStruct((1, 1), F32)),
        grid=(s // tr,), in_specs=[row, row, vec], out_specs=(row, vec, one),
        compiler_params=_params("arbitrary"),
    )(x2, target, g)


def _tri_ones(n, lower):
    r = lax.broadcasted_iota(jnp.int32, (n, n), 0)
    c = lax.broadcasted_iota(jnp.int32, (n, n), 1)
    return jnp.where((c <= r) if lower else (c >= r), 1.0, 0.0).astype(F32)


def _forget_fwd(h, w_f, b_f, *, name, tr=256):
    s, d = h.shape
    tr = _tile(s, tr, 8)

    def body(h_ref, w_ref, b_ref, zb_ref, f_ref, carry):
        @pl.when(pl.program_id(0) == 0)
        def _():
            carry[...] = jnp.zeros_like(carry)

        zb = jnp.dot(h_ref[...], w_ref[...], preferred_element_type=F32) + b_ref[...]
        zb_ref[...] = zb
        log_f = jnp.minimum(zb, 0.0) - jnp.log(1.0 + jnp.exp(-jnp.abs(zb)))
        run = jnp.dot(_tri_ones(tr, True), log_f, preferred_element_type=F32,
                      precision=lax.Precision.HIGHEST) + carry[...]
        f_ref[...] = run
        carry[...] = run[tr - 1:tr, :]

    row = pl.BlockSpec((tr, LANES), lambda i: (i, 0))
    return pl.pallas_call(
        body, name=name,
        out_shape=(jax.ShapeDtypeStruct((s, LANES), F32), jax.ShapeDtypeStruct((s, LANES), F32)),
        grid=(s // tr,),
        in_specs=[pl.BlockSpec((tr, d), lambda i: (i, 0)), pl.BlockSpec((d, LANES), lambda i: (0, 0)),
                  pl.BlockSpec((1, LANES), lambda i: (0, 0))],
        out_specs=(row, row), scratch_shapes=[pltpu.VMEM((1, LANES), F32)],
        compiler_params=_params("arbitrary"),
    )(h, w_f, b_f)


def _forget_bwd(d_f, zb, *, name, tr=256):
    s = zb.shape[0]
    tr = _tile(s, tr, 8)
    nb = s // tr

    def body(df_ref, zb_ref, dz_ref, db_ref, carry):
        @pl.when(pl.program_id(0) == 0)
        def _():
            carry[...] = jnp.zeros_like(carry)
            db_ref[...] = jnp.zeros_like(db_ref)

        run = jnp.dot(_tri_ones(tr, False), df_ref[...], preferred_element_type=F32,
                      precision=lax.Precision.HIGHEST) + carry[...]
        carry[...] = run[0:1, :]
        dz = run / (1.0 + jnp.exp(zb_ref[...]))
        dz_ref[...] = dz.astype(BF16)
        db_ref[...] += jnp.sum(dz, axis=0, keepdims=True)

    row = pl.BlockSpec((tr, LANES), lambda i: (nb - 1 - i, 0))
    return pl.pallas_call(
        body, name=name,
        out_shape=(jax.ShapeDtypeStruct((s, LANES), BF16), jax.ShapeDtypeStruct((1, LANES), F32)),
        grid=(nb,), in_specs=[row, row], out_specs=(row, pl.BlockSpec((1, LANES), lambda i: (0, 0))),
        scratch_shapes=[pltpu.VMEM((1, LANES), F32)],
        compiler_params=_params("arbitrary"),
    )(d_f, zb)


def _pairs(nblk, by_kv):
    if by_kv:
        pr = [(i, j) for j in range(nblk) for i in range(j, nblk)]
    else:
        pr = [(i, j) for i in range(nblk) for j in range(i + 1)]
    return (jnp.asarray(np.array([p[0] for p in pr], np.int32)), jnp.asarray(np.array([p[1] for p in pr], np.int32)))


def _causal_mask(t):
    r = lax.broadcasted_iota(jnp.int32, (t, t), 0)
    c = lax.broadcasted_iota(jnp.int32, (t, t), 1)
    return c <= r


def _attn_fwd(z, f_col, f_row, n_heads, *, name, tb=512):
    s = z.shape[0]
    tb = _tile(s, tb, 128)
    nblk = s // tb
    qi, kj = _pairs(nblk, by_kv=False)
    scale = 1.0 / math.sqrt(HEAD_DIM)

    def body(qi_ref, kj_ref, q_ref, k_ref, v_ref, fq_ref, fk_ref, o_ref, lse_ref, m_sc, l_sc, acc_sc):
        p = pl.program_id(1)
        i, j = qi_ref[p], kj_ref[p]

        @pl.when(j == 0)
        def _():
            m_sc[...] = jnp.full_like(m_sc, NEG_BIG)
            l_sc[...] = jnp.zeros_like(l_sc)
            acc_sc[...] = jnp.zeros_like(acc_sc)

        def update(masked):
            sc = lax.dot_general(q_ref[...], k_ref[...], (((1,), (1,)), ((), ())), preferred_element_type=F32)
            sc = sc * scale + (fq_ref[...] - fk_ref[...])
            if masked:
                sc = jnp.where(_causal_mask(tb), sc, NEG_BIG)
            m_new = jnp.maximum(m_sc[...], jnp.max(sc, axis=-1, keepdims=True))
            alpha = jnp.exp(m_sc[...] - m_new)
            pv = jnp.exp(sc - m_new)
            l_sc[...] = alpha * l_sc[...] + jnp.sum(pv, axis=-1, keepdims=True)
            acc_sc[...] = alpha * acc_sc[...] + jnp.dot(pv.astype(BF16), v_ref[...], preferred_element_type=F32)
            m_sc[...] = m_new

        @pl.when(j < i)
        def _():
            update(False)

        @pl.when(j == i)
        def _():
            update(True)
            o_ref[...] = (acc_sc[...] / l_sc[...]).astype(BF16)
            lse_ref[...] = m_sc[...] + jnp.log(l_sc[...])

    h = n_heads
    grid_spec = pltpu.PrefetchScalarGridSpec(
        num_scalar_prefetch=2, grid=(h, int(qi.shape[0])),
        in_specs=[
            pl.BlockSpec((tb, HEAD_DIM), lambda hh, p, qi_r, kj_r: (qi_r[p], hh)),
            pl.BlockSpec((tb, HEAD_DIM), lambda hh, p, qi_r, kj_r: (kj_r[p], h + hh)),
            pl.BlockSpec((tb, HEAD_DIM), lambda hh, p, qi_r, kj_r: (kj_r[p], 2 * h + hh)),
            pl.BlockSpec((None, tb, 1), lambda hh, p, qi_r, kj_r: (hh, qi_r[p], 0)),
            pl.BlockSpec((None, 1, tb), lambda hh, p, qi_r, kj_r: (hh, 0, kj_r[p])),
        ],
        out_specs=(
            pl.BlockSpec((tb, HEAD_DIM), lambda hh, p, qi_r, kj_r: (qi_r[p], hh)),
            pl.BlockSpec((None, tb, 1), lambda hh, p, qi_r, kj_r: (hh, qi_r[p], 0)),
        ),
        scratch_shapes=[pltpu.VMEM((tb, 1), F32), pltpu.VMEM((tb, 1), F32), pltpu.VMEM((tb, HEAD_DIM), F32)],
    )
    return pl.pallas_call(
        body, name=name, grid_spec=grid_spec,
        out_shape=(jax.ShapeDtypeStruct((s, h * HEAD_DIM), BF16), jax.ShapeDtypeStruct((h, s, 1), F32)),
        compiler_params=_params("parallel", "arbitrary"),
    )(qi, kj, z, z, z, f_col, f_row)


def _attn_bwd(z, o, d_o, lse, f_col, f_row, n_heads, *, name, tb=512):
    s = z.shape[0]
    tb = _tile(s, tb, 128)
    nblk = s // tb
    qi, kj = _pairs(nblk, by_kv=True)
    scale = 1.0 / math.sqrt(HEAD_DIM)
    h = n_heads

    def body(qi_ref, kj_ref, q_ref, k_ref, v_ref, o_ref, do_ref, lse_ref, fq_ref, fk_ref,
             dq_ref, dk_ref, dv_ref, df_ref, dfq_ref, dq_sc, dk_sc, dv_sc, df_sc, dfq_sc):
        p = pl.program_id(1)
        i, j = qi_ref[p], kj_ref[p]

        @pl.when(p == 0)
        def _():
            dq_sc[...] = jnp.zeros_like(dq_sc)
            dfq_sc[...] = jnp.zeros_like(dfq_sc)

        @pl.when(i == j)
        def _():
            dk_sc[...] = jnp.zeros_like(dk_sc)
            dv_sc[...] = jnp.zeros_like(dv_sc)
            df_sc[...] = jnp.zeros_like(df_sc)

        def update(masked):
            q, k, v, do = q_ref[...], k_ref[...], v_ref[...], do_ref[...]
            delta = jnp.sum(do.astype(F32) * o_ref[...].astype(F32), axis=-1, keepdims=True)
            sc = lax.dot_general(q, k, (((1,), (1,)), ((), ())), preferred_element_type=F32)
            sc = sc * scale + (fq_ref[...] - fk_ref[...])
            pv = jnp.exp(sc - lse_ref[...])
            if masked:
                pv = jnp.where(_causal_mask(tb), pv, 0.0)
            dp = lax.dot_general(do, v, (((1,), (1,)), ((), ())), preferred_element_type=F32)
            ds = pv * (dp - delta)
            ds_b = ds.astype(BF16)
            dv_sc[...] += lax.dot_general(pv.astype(BF16), do, (((0,), (0,)), ((), ())), preferred_element_type=F32)
            dk_sc[...] += lax.dot_general(ds_b, q, (((0,), (0,)), ((), ())), preferred_element_type=F32)
            rows = pl.ds(pl.multiple_of(i * tb, tb), tb)
            dq_sc[rows, :] += jnp.dot(ds_b, k, preferred_element_type=F32)
            df_sc[...] -= jnp.sum(ds, axis=0, keepdims=True)
            dfq_sc[rows, :] += jnp.sum(ds, axis=1, keepdims=True)

        @pl.when(i > j)
        def _():
            update(False)

        @pl.when(i == j)
        def _():
            update(True)

        @pl.when(i == nblk - 1)
        def _():
            dk_ref[...] = (dk_sc[...] * scale).astype(BF16)
            dv_ref[...] = dv_sc[...].astype(BF16)
            df_ref[...] = df_sc[...]

        @pl.when(p == pl.num_programs(1) - 1)
        def _():
            dq_ref[...] = (dq_sc[...] * scale).astype(BF16)
            dfq_ref[...] = dfq_sc[...]

    qblk = lambda off: pl.BlockSpec((tb, HEAD_DIM), lambda hh, p, qi_r, kj_r: (qi_r[p], off + hh))
    kblk = lambda off: pl.BlockSpec((tb, HEAD_DIM), lambda hh, p, qi_r, kj_r: (kj_r[p], off + hh))
    qcol = pl.BlockSpec((None, tb, 1), lambda hh, p, qi_r, kj_r: (hh, qi_r[p], 0))
    krow = pl.BlockSpec((None, 1, tb), lambda hh, p, qi_r, kj_r: (hh, 0, kj_r[p]))
    grid_spec = pltpu.PrefetchScalarGridSpec(
        num_scalar_prefetch=2, grid=(h, int(qi.shape[0])),
        in_specs=[qblk(0), kblk(h), kblk(2 * h), qblk(0), qblk(0), qcol, qcol, krow],
        out_specs=(
            pl.BlockSpec((s, HEAD_DIM), lambda hh, p, qi_r, kj_r: (0, hh)),
            kblk(0), kblk(0), krow,
            pl.BlockSpec((None, s, 1), lambda hh, p, qi_r, kj_r: (hh, 0, 0)),
        ),
        scratch_shapes=[pltpu.VMEM((s, HEAD_DIM), F32), pltpu.VMEM((tb, HEAD_DIM), F32),
                        pltpu.VMEM((tb, HEAD_DIM), F32), pltpu.VMEM((1, tb), F32), pltpu.VMEM((s, 1), F32)],
    )
    act = jax.ShapeDtypeStruct((s, h * HEAD_DIM), BF16)
    return pl.pallas_call(
        body, name=name, grid_spec=grid_spec,
        out_shape=(act, act, act, jax.ShapeDtypeStruct((h, 1, s), F32), jax.ShapeDtypeStruct((h, s, 1), F32)),
        compiler_params=_params("parallel", "arbitrary"),
    )(qi, kj, z, z, z, o, d_o, lse, f_col, f_row)


GELU_C = math.sqrt(2.0 / math.pi)
GELU_A = 0.044715


def _gelu(x):
    return 0.5 * x * (1.0 + jnp.tanh(GELU_C * (x + GELU_A * (x * x * x))))


def _gelu_and_grad(x):
    t = jnp.tanh(GELU_C * (x + GELU_A * (x * x * x)))
    y = 0.5 * x * (1.0 + t)
    dy = 0.5 * (1.0 + t) + 0.5 * x * (1.0 - t * t) * (GELU_C * (1.0 + 3.0 * GELU_A * (x * x)))
    return y, dy


def _layernorm_parts(g):
    mu = jnp.mean(g, axis=-1, keepdims=True)
    xc = g - mu
    rs = lax.rsqrt(jnp.mean(xc * xc, axis=-1, keepdims=True) + EPS)
    return xc * rs, rs


def _spatial_mix(w_ref, bcol_ref, vv_b, n_heads, n_chunks):
    tril = _causal_mask(CHUNK)
    cols = []
    for hh in range(n_heads):
        wc = jnp.where(tril, w_ref[hh], 0.0).astype(BF16)
        lanes = slice(hh * HEAD_DIM, (hh + 1) * HEAD_DIM)
        rows = [jnp.dot(wc, vv_b[c * CHUNK:(c + 1) * CHUNK, lanes], preferred_element_type=F32)
                + bcol_ref[:, hh:hh + 1] for c in range(n_chunks)]
        cols.append(jnp.concatenate(rows, axis=0))
    return jnp.concatenate(cols, axis=1)


def _mix_fwd(z, o, ln_g, ln_b, w_s, b_col, attn_g, gm_g, n_heads, *, name, tr=256):
    s = z.shape[0]
    dg = n_heads * HEAD_DIM
    tr = _tile(s, tr, CHUNK)
    n_chunks = tr // CHUNK

    def body(zu_ref, zv_ref, o_ref, lg_ref, lb_ref, w_ref, bcol_ref, ag_ref, gg_ref, out_ref):
        u = _gelu(zu_ref[...].astype(F32))
        xhat, _ = _layernorm_parts(_gelu(zv_ref[...].astype(F32)))
        vv = xhat * lg_ref[...] + lb_ref[...]
        gm = u * _spatial_mix(w_ref, bcol_ref, vv.astype(BF16), n_heads, n_chunks)
        rg = lax.rsqrt(jnp.mean(gm * gm, axis=-1, keepdims=True) + EPS)
        ov = o_ref[...].astype(F32)
        ra = lax.rsqrt(jnp.mean(ov * ov, axis=-1, keepdims=True) + EPS)
        out_ref[:, :dg] = ((ov * ra) * ag_ref[...]).astype(BF16)
        out_ref[:, dg:] = ((gm * rg) * gg_ref[...]).astype(BF16)

    vec = pl.BlockSpec((1, dg), lambda i: (0, 0))
    return pl.pallas_call(
        body, name=name, out_shape=jax.ShapeDtypeStruct((s, 2 * dg), BF16), grid=(s // tr,),
        in_specs=[pl.BlockSpec((tr, dg), lambda i: (i, 3)), pl.BlockSpec((tr, dg), lambda i: (i, 4)),
                  pl.BlockSpec((tr, dg), lambda i: (i, 0)), vec, vec,
                  pl.BlockSpec((n_heads, CHUNK, CHUNK), lambda i: (0, 0, 0)),
                  pl.BlockSpec((CHUNK, n_heads), lambda i: (0, 0)), vec, vec],
        out_specs=pl.BlockSpec((tr, 2 * dg), lambda i: (i, 0)),
        compiler_params=_params("parallel"),
    )(z, z, o, ln_g, ln_b, w_s, b_col, attn_g, gm_g)


def _mix_bwd(z, o, d_merged, ln_g, ln_b, w_s, b_col, attn_g, gm_g, n_heads, *, name, tr=256):
    s = z.shape[0]
    dg = n_heads * HEAD_DIM
    tr = _tile(s, tr, CHUNK)
    n_chunks = tr // CHUNK

    def body(zu_ref, zv_ref, o_ref, dm_ref, lg_ref, lb_ref, w_ref, bcol_ref, ag_ref, gg_ref,
             do_ref, dzu_ref, dzv_ref, dw_ref, dbcol_ref, dlg_ref, dlb_ref, dag_ref, dgg_ref):
        @pl.when(pl.program_id(0) == 0)
        def _():
            for ref in (dw_ref, dbcol_ref, dlg_ref, dlb_ref, dag_ref, dgg_ref):
                ref[...] = jnp.zeros_like(ref)

        d_o, dag_rows = _rms_bwd_rows(dm_ref[:, :dg], o_ref[...].astype(F32), ag_ref[...])
        do_ref[...] = d_o.astype(BF16)
        dag_ref[...] += jnp.sum(dag_rows, axis=0, keepdims=True)

        u, du_dz = _gelu_and_grad(zu_ref[...].astype(F32))
        gv, dgv_dz = _gelu_and_grad(zv_ref[...].astype(F32))
        xhat, rs = _layernorm_parts(gv)
        lg = lg_ref[...]
        vv_b = (xhat * lg + lb_ref[...]).astype(BF16)
        mix = _spatial_mix(w_ref, bcol_ref, vv_b, n_heads, n_chunks)
        gm = u * mix
        d_gm, dgg_rows = _rms_bwd_rows(dm_ref[:, dg:], gm, gg_ref[...])
        dgg_ref[...] += jnp.sum(dgg_rows, axis=0, keepdims=True)
        dzu_ref[...] = ((d_gm * mix) * du_dz).astype(BF16)
        d_mix = d_gm * u
        d_mix_b = d_mix.astype(BF16)

        tril = _causal_mask(CHUNK)
        lane = lax.broadcasted_iota(jnp.int32, (CHUNK, n_heads), 1)
        cols = []
        db = jnp.zeros((CHUNK, n_heads), F32)
        for hh in range(n_heads):
            wc = jnp.where(tril, w_ref[hh], 0.0).astype(BF16)
            lanes = slice(hh * HEAD_DIM, (hh + 1) * HEAD_DIM)
            dw = jnp.zeros((CHUNK, CHUNK), F32)
            dmix_sum = jnp.zeros((CHUNK, HEAD_DIM), F32)
            rows = []
            for c in range(n_chunks):
                rws = slice(c * CHUNK, (c + 1) * CHUNK)
                dmb = d_mix_b[rws, lanes]
                dw += lax.dot_general(dmb, vv_b[rws, lanes], (((1,), (1,)), ((), ())), preferred_element_type=F32)
                rows.append(lax.dot_general(wc, dmb, (((0,), (0,)), ((), ())), preferred_element_type=F32))
                dmix_sum += d_mix[rws, lanes]
            dw_ref[hh] += jnp.where(tril, dw, 0.0)
            db += jnp.where(lane == hh, jnp.sum(dmix_sum, axis=-1, keepdims=True), 0.0)
            cols.append(jnp.concatenate(rows, axis=0))
        dbcol_ref[...] += db
        d_vv = jnp.concatenate(cols, axis=1)

        dlg_ref[...] += jnp.sum(d_vv * xhat, axis=0, keepdims=True)
        dlb_ref[...] += jnp.sum(d_vv, axis=0, keepdims=True)
        d_xhat = d_vv * lg
        d_gv = rs * (d_xhat - jnp.mean(d_xhat, axis=-1, keepdims=True)
                     - xhat * jnp.mean(d_xhat * xhat, axis=-1, keepdims=True))
        dzv_ref[...] = (d_gv * dgv_dz).astype(BF16)

    vec = pl.BlockSpec((1, dg), lambda i: (0, 0))
    wspec = pl.BlockSpec((n_heads, CHUNK, CHUNK), lambda i: (0, 0, 0))
    bspec = pl.BlockSpec((CHUNK, n_heads), lambda i: (0, 0))
    rowb = pl.BlockSpec((tr, dg), lambda i: (i, 0))
    act = jax.ShapeDtypeStruct((s, dg), BF16)
    vshape = jax.ShapeDtypeStruct((1, dg), F32)
    return pl.pallas_call(
        body, name=name,
        out_shape=(act, act, act, jax.ShapeDtypeStruct((n_heads, CHUNK, CHUNK), F32),
                   jax.ShapeDtypeStruct((CHUNK, n_heads), F32), vshape, vshape, vshape, vshape),
        grid=(s // tr,),
        in_specs=[pl.BlockSpec((tr, dg), lambda i: (i, 3)), pl.BlockSpec((tr, dg), lambda i: (i, 4)),
                  rowb, pl.BlockSpec((tr, 2 * dg), lambda i: (i, 0)), vec, vec, wspec, bspec, vec, vec],
        out_specs=(rowb, rowb, rowb, wspec, bspec, vec, vec, vec, vec),
        compiler_params=_params("arbitrary"),
    )(z, z, o, d_merged, ln_g, ln_b, w_s, b_col, attn_g, gm_g)


def _place():
    x, y, c = lax.axis_index("x"), lax.axis_index("y"), lax.axis_index("c")
    other_chips = [(1 - x, y), (x, 1 - y), (1 - x, 1 - y)]
    return x, y, c, other_chips


def _remote(src, dst, send_sem, recv_sem, to):
    return pltpu.make_async_remote_copy(src_ref=src, dst_ref=dst, send_sem=send_sem, recv_sem=recv_sem,
                                        device_id=to, device_id_type=MESH)


def _allgather_weights(shards, *, name):
    n = len(shards)

    def body(*refs):
        ins, outs = refs[:n], refs[n:2 * n]
        send_sems, recv_sems, local_sems = refs[2 * n:]
        x, y, c, chips = _place()
        me = 2 * x + y
        sibling = (x, y, 1 - c)
        local, sent = [], []
        for t in range(n):
            half = ins[t].shape[0] // 2
            own = pltpu.make_async_copy(ins[t], outs[t].at[me], local_sems.at[t])
            own.start()
            local.append(own)
            for k, (cx, cy) in enumerate(chips):
                rows = pl.ds(c * half, half)
                cp = _remote(ins[t].at[rows, :], outs[t].at[me, rows, :],
                             send_sems.at[6 * t + k], recv_sems.at[6 * t + k], (cx, cy, c))
                cp.start()
                sent.append(cp)
        for t in range(n):
            half = ins[t].shape[0] // 2
            for k, (cx, cy) in enumerate(chips):
                blk = outs[t].at[2 * cx + cy, pl.ds(c * half, half), :]
                _remote(blk, blk, send_sems.at[6 * t + k], recv_sems.at[6 * t + k], sibling).wait_recv()
                fwd = _remote(blk, blk, send_sems.at[6 * t + 3 + k], recv_sems.at[6 * t + 3 + k], sibling)
                fwd.start()
                sent.append(fwd)
        for t in range(n):
            half = ins[t].shape[0] // 2
            for k, (cx, cy) in enumerate(chips):
                blk = outs[t].at[2 * cx + cy, pl.ds((1 - c) * half, half), :]
                _remote(blk, blk, send_sems.at[6 * t + 3 + k], recv_sems.at[6 * t + 3 + k], sibling).wait_recv()
        for cp in sent:
            cp.wait_send()
        for cp in local:
            cp.wait()

    return pl.pallas_call(
        body, name=name,
        out_shape=tuple(jax.ShapeDtypeStruct((N_CHIPS,) + a.shape, a.dtype) for a in shards),
        in_specs=[ANY] * n, out_specs=tuple([ANY] * n),
        scratch_shapes=[pltpu.SemaphoreType.DMA((6 * n,)), pltpu.SemaphoreType.DMA((6 * n,)),
                        pltpu.SemaphoreType.DMA((n,))],
    )(*shards)


def _swap_halves(grads, *, name):
    n = len(grads)

    def body(*refs):
        ins, outs = refs[:n], refs[n:2 * n]
        send_sems, recv_sems = refs[2 * n:]
        x, y, c, _ = _place()
        copies = []
        for t in range(n):
            half = ins[t].shape[1] // 2
            cp = _remote(ins[t].at[:, pl.ds((1 - c) * half, half), :], outs[t],
                         send_sems.at[t], recv_sems.at[t], (x, y, 1 - c))
            cp.start()
            copies.append(cp)
        for cp in copies:
            cp.wait()

    return pl.pallas_call(
        body, name=name,
        out_shape=tuple(jax.ShapeDtypeStruct((a.shape[0], a.shape[1] // 2, a.shape[2]), a.dtype) for a in grads),
        in_specs=[ANY] * n, out_specs=tuple([ANY] * n),
        scratch_shapes=[pltpu.SemaphoreType.DMA((n,)), pltpu.SemaphoreType.DMA((n,))],
    )(*grads)


def _add_halves(grad, received, core, *, name, tr=256):
    ns, half, cols = received.shape
    tr = _tile(half, tr, 16)
    per = half // tr

    def body(core_ref, g_ref, r_ref, o_ref):
        o_ref[...] = (g_ref[...].astype(F32) + r_ref[...].astype(F32)).astype(BF16)

    grid_spec = pltpu.PrefetchScalarGridSpec(
        num_scalar_prefetch=1, grid=(ns, per),
        in_specs=[pl.BlockSpec((None, tr, cols), lambda s, i, core_r: (s, core_r[0] * per + i, 0)),
                  pl.BlockSpec((None, tr, cols), lambda s, i, core_r: (s, i, 0))],
        out_specs=pl.BlockSpec((None, tr, cols), lambda s, i, core_r: (s, i, 0)),
    )
    return pl.pallas_call(
        body, name=name, grid_spec=grid_spec, out_shape=jax.ShapeDtypeStruct(received.shape, BF16),
        compiler_params=_params("parallel", "parallel"),
    )(core, grad, received)


def _send_partials(parts, *, name):
    n = len(parts)

    def body(*refs):
        ins, outs = refs[:n], refs[n:2 * n]
        send_sems, recv_sems, local_sems = refs[2 * n:]
        x, y, c, chips = _place()
        me = 2 * x + y
        local, sent = [], []
        for t in range(n):
            own = pltpu.make_async_copy(ins[t].at[me], outs[t].at[me], local_sems.at[t])
            own.start()
            local.append(own)
            for k, (cx, cy) in enumerate(chips):
                cp = _remote(ins[t].at[2 * cx + cy], outs[t].at[me],
                             send_sems.at[3 * t + k], recv_sems.at[3 * t + k], (cx, cy, c))
                cp.start()
                sent.append(cp)
        for t in range(n):
            for k, (cx, cy) in enumerate(chips):
                slot = outs[t].at[2 * cx + cy]
                _remote(slot, slot, send_sems.at[3 * t + k], recv_sems.at[3 * t + k], (cx, cy, c)).wait_recv()
        for cp in sent:
            cp.wait_send()
        for cp in local:
            cp.wait()

    return pl.pallas_call(
        body, name=name,
        out_shape=tuple(jax.ShapeDtypeStruct(a.shape, a.dtype) for a in parts),
        in_specs=[ANY] * n, out_specs=tuple([ANY] * n),
        scratch_shapes=[pltpu.SemaphoreType.DMA((3 * n,)), pltpu.SemaphoreType.DMA((3 * n,)),
                        pltpu.SemaphoreType.DMA((n,))],
    )(*parts)


def _sum_chips(slots, *, name, tr=256):
    ns, half, cols = slots.shape
    tr = _tile(half, tr, 16)

    def body(s_ref, o_ref):
        acc = s_ref[0].astype(F32)
        for k in range(1, ns):
            acc = acc + s_ref[k].astype(F32)
        o_ref[...] = acc

    return pl.pallas_call(
        body, name=name, out_shape=jax.ShapeDtypeStruct((half, cols), F32), grid=(half // tr,),
        in_specs=[pl.BlockSpec((ns, tr, cols), lambda i: (0, i, 0))],
        out_specs=pl.BlockSpec((tr, cols), lambda i: (i, 0)),
        compiler_params=_params("parallel"),
    )(slots)


def _join_halves(halves, *, name):
    n = len(halves)

    def body(*refs):
        ins, outs = refs[:n], refs[n:2 * n]
        send_sems, recv_sems, local_sems = refs[2 * n:]
        x, y, c, _ = _place()
        local, sent = [], []
        for t in range(n):
            half = ins[t].shape[0]
            mine = outs[t].at[pl.ds(c * half, half), :]
            own = pltpu.make_async_copy(ins[t], mine, local_sems.at[t])
            own.start()
            local.append(own)
            cp = _remote(ins[t], mine, send_sems.at[t], recv_sems.at[t], (x, y, 1 - c))
            cp.start()
            sent.append(cp)
        for t in range(n):
            half = ins[t].shape[0]
            theirs = outs[t].at[pl.ds((1 - c) * half, half), :]
            _remote(theirs, theirs, send_sems.at[t], recv_sems.at[t], (x, y, 1 - c)).wait_recv()
        for cp in sent:
            cp.wait_send()
        for cp in local:
            cp.wait()

    return pl.pallas_call(
        body, name=name,
        out_shape=tuple(jax.ShapeDtypeStruct((2 * a.shape[0], a.shape[1]), a.dtype) for a in halves),
        in_specs=[ANY] * n, out_specs=tuple([ANY] * n),
        scratch_shapes=[pltpu.SemaphoreType.DMA((n,)), pltpu.SemaphoreType.DMA((n,)), pltpu.SemaphoreType.DMA((n,))],
    )(*halves)


def _allgather_small(buf, *, name):
    rows = buf.shape[0]

    def body(x_ref, out_ref, send_sems, recv_sems, local_sem):
        x, y, c, chips = _place()
        sibling = (x, y, 1 - c)

        def slot(px, py, pc):
            return out_ref.at[4 * px + 2 * py + pc]

        def copy(k, block, to, src=None):
            return _remote(slot(*block) if src is None else src, slot(*block), send_sems.at[k], recv_sems.at[k], to)

        mine = pltpu.make_async_copy(x_ref, slot(x, y, c), local_sem)
        mine.start()
        first = [copy(0, (x, y, c), sibling, src=x_ref)]
        first += [copy(1 + k, (x, y, c), (*chip, c), src=x_ref) for k, chip in enumerate(chips)]
        for cp in first:
            cp.start()
        passed = [copy(4 + k, (*chip, c), sibling) for k, chip in enumerate(chips)]
        for k, chip in enumerate(chips):
            copy(1 + k, (*chip, c), (x, y, c)).wait_recv()
            passed[k].start()
        copy(0, (x, y, 1 - c), (x, y, c)).wait_recv()
        for k, chip in enumerate(chips):
            copy(4 + k, (*chip, 1 - c), (x, y, c)).wait_recv()
        for cp in first + passed:
            cp.wait_send()
        mine.wait()

    return pl.pallas_call(
        body, name=name, out_shape=jax.ShapeDtypeStruct((N_DEV, rows, LANES), buf.dtype),
        in_specs=[pl.BlockSpec(memory_space=pltpu.VMEM)], out_specs=pl.BlockSpec(memory_space=pltpu.VMEM),
        scratch_shapes=[pltpu.SemaphoreType.DMA((7,)), pltpu.SemaphoreType.DMA((7,)), pltpu.SemaphoreType.DMA],
    )(buf)


def _adamw_math(w, g, m, v):
    m = ADAM_B1 * m + (1.0 - ADAM_B1) * g
    v = ADAM_B2 * v + (1.0 - ADAM_B2) * (g * g)
    m_hat = m / (1.0 - ADAM_B1 ** ADAM_STEP)
    v_hat = v / (1.0 - ADAM_B2 ** ADAM_STEP)
    delta = -ADAM_LR * (m_hat / (jnp.sqrt(v_hat) + ADAM_EPS) + ADAM_WD * w)
    return delta, m, v


def _adamw(w, g, m, v, *, name, tr=256):
    rows, cols = w.shape
    tr = _tile(rows, tr, 8)

    def body(w_ref, g_ref, m_ref, v_ref, d_ref, mo_ref, vo_ref):
        d_ref[...], mo_ref[...], vo_ref[...] = _adamw_math(w_ref[...], g_ref[...], m_ref[...], v_ref[...])

    blk = pl.BlockSpec((tr, cols), lambda i: (i, 0))
    shape = jax.ShapeDtypeStruct((rows, cols), F32)
    return pl.pallas_call(
        body, name=name, out_shape=(shape, shape, shape), grid=(rows // tr,),
        in_specs=[blk] * 4, out_specs=(blk, blk, blk), compiler_params=_params("parallel"),
    )(w, g, m, v)


def _adamw_small(gathered, w, m, v, *, name):
    nd = gathered.shape[0]

    def body(gs_ref, w_ref, m_ref, v_ref, g_ref, d_ref, mo_ref, vo_ref):
        g = gs_ref[0]
        for k in range(1, nd):
            g = g + gs_ref[k]
        g_ref[...] = g
        d_ref[...], mo_ref[...], vo_ref[...] = _adamw_math(w_ref[...], g, m_ref[...], v_ref[...])

    shape = jax.ShapeDtypeStruct(w.shape, F32)
    return pl.pallas_call(body, name=name, out_shape=(shape, shape, shape, shape),
                          compiler_params=pltpu.CompilerParams(vmem_limit_bytes=VMEM_LIMIT_BYTES))(gathered, w, m, v)


def _pack(parts):
    flat = jnp.concatenate([p.reshape(-1).astype(F32) for p in parts])
    rows = -(-flat.shape[0] // (8 * LANES)) * 8
    return jnp.pad(flat, (0, rows * LANES - flat.shape[0])).reshape(rows, LANES)


def _unpack(buf, shapes):
    flat = buf.reshape(-1)
    out, pos = [], 0
    for shp in shapes:
        size = int(np.prod(shp))
        out.append(flat[pos:pos + size].reshape(shp))
        pos += size
    return out


def kernel(x, norm_mix_g, w_in, b_f, gmlp_ln_g, gmlp_ln_b, w_s, b_s, attn_out_g, gmlp_out_g, w_out, norm_ffn_g, w_ff1, w_ff2, norm_final_g, loss_target, m_norm_mix_g, m_w_in, m_b_f, m_gmlp_ln_g, m_gmlp_ln_b, m_w_s, m_b_s, m_attn_out_g, m_gmlp_out_g, m_w_out, m_norm_ffn_g, m_w_ff1, m_w_ff2, m_norm_final_g, v_norm_mix_g, v_w_in, v_b_f, v_gmlp_ln_g, v_gmlp_ln_b, v_w_s, v_b_s, v_attn_out_g, v_gmlp_out_g, v_w_out, v_norm_ffn_g, v_w_ff1, v_w_ff2, v_norm_final_g):
    seq, d_model = x.shape[1], x.shape[2]
    d_attn = d_model // 2
    n_heads = d_attn // HEAD_DIM
    qkv = 3 * d_attn
    shard_cols = w_in.shape[2]
    assert N_CHIPS * shard_cols == qkv + n_heads + 2 * d_attn
    xs = x.reshape(seq, d_model)
    target = loss_target.reshape(seq, d_model)

    g_in, g_out, g_ff1, g_ff2 = _allgather_weights(
        [w_in[0].astype(BF16), w_out[0].astype(BF16), w_ff1[0].astype(BF16), w_ff2[0].astype(BF16)],
        name="allgather_weights")
    w_in_full = jnp.transpose(g_in, (1, 0, 2)).reshape(d_model, N_CHIPS * shard_cols)
    w_main = jnp.concatenate([w_in_full[:, :qkv], w_in_full[:, qkv + n_heads:]], axis=1)
    w_f = jnp.pad(w_in_full[:, qkv:qkv + n_heads], ((0, 0), (0, LANES - n_heads)))
    w_out_full = g_out.reshape(2 * d_attn, d_model)
    w_ff2_full = g_ff2.reshape(N_CHIPS * g_ff2.shape[1], d_model)
    b_f_pad = jnp.pad(b_f, ((0, 0), (0, LANES - n_heads)))
    b_col = b_s[0].T

    h = _rmsnorm_fwd(xs, norm_mix_g, name="norm_mix")
    z = _matmul(h, w_main, name="in_proj", out_dtype=BF16)
    zb, f_cum = _forget_fwd(h, w_f, b_f_pad, name="forget_fwd")
    f_heads = f_cum[:, :n_heads].T
    f_col, f_row = f_heads[:, :, None], f_heads[:, None, :]
    o, lse = _attn_fwd(z, f_col, f_row, n_heads, name="attn_fwd")
    merged = _mix_fwd(z, o, gmlp_ln_g, gmlp_ln_b, w_s[0], b_col, attn_out_g, gmlp_out_g, n_heads, name="mix_fwd")
    x1 = _matmul(merged, w_out_full, name="out_proj", out_dtype=F32, residual=xs)
    h2 = _rmsnorm_fwd(x1, norm_ffn_g, name="norm_ffn")
    a = _matmul(h2, g_ff1, name="ff1", out_dtype=BF16, relu=True, b_sharded=True)
    x2 = _matmul(a, w_ff2_full, name="ff2", out_dtype=F32, square_lhs=True, residual=x1)
    dx2, dg_final, loss = _loss_and_final_bwd(x2, target, norm_final_g.reshape(1, d_model), name="loss_head")

    da = _matmul(dx2, w_ff2_full, name="ff2_dlhs", out_dtype=BF16, trans_b=True, scale2_by=a)
    dw_ff2 = _matmul(a, dx2, name="ff2_dw", out_dtype=BF16, trans_a=True, square_lhs=True)
    dh2 = _matmul(da, g_ff1, name="ff1_dlhs", out_dtype=F32, trans_b=True, b_sharded=True)
    dw_ff1 = _matmul(h2, da, name="ff1_dw", out_dtype=BF16, trans_a=True, out_sharded=True)
    dx1, dg_ffn = _rmsnorm_bwd(dh2, x1, dx2, norm_ffn_g, name="norm_ffn_bwd")
    d_merged = _matmul(dx1, w_out_full, name="out_proj_dlhs", out_dtype=F32, trans_b=True)
    dw_out = _matmul(merged, dx1, name="out_proj_dw", out_dtype=BF16, trans_a=True)
    d_o, dzu, dzv, dw_s, db_col, dlg, dlb, dag, dgg = _mix_bwd(
        z, o, d_merged, gmlp_ln_g, gmlp_ln_b, w_s[0], b_col, attn_out_g, gmlp_out_g, n_heads, name="mix_bwd")
    dq, dk, dv, d_f_key, d_f_query = _attn_bwd(z, o, d_o, lse, f_col, f_row, n_heads, name="attn_bwd")
    d_f = d_f_key.reshape(n_heads, seq) + d_f_query.reshape(n_heads, seq)
    d_f_pad = jnp.pad(d_f.T, ((0, 0), (0, LANES - n_heads)))
    dzf, db_f = _forget_bwd(d_f_pad, zb, name="forget_bwd")
    dz = jnp.concatenate([dq, dk, dv, dzu, dzv], axis=1)
    dh_gate = _matmul(dzf, w_f, name="gate_dlhs", out_dtype=F32, trans_b=True)
    dh = _matmul(dz, w_main, name="in_proj_dlhs", out_dtype=F32, trans_b=True, residual=dh_gate)
    dw_main = _matmul(h, dz, name="in_proj_dw", out_dtype=BF16, trans_a=True)
    dw_f = _matmul(h, dzf, name="gate_dw", out_dtype=BF16, trans_a=True)
    grad_x, dg_mix = _rmsnorm_bwd(dh, xs, dx1, norm_mix_g, name="norm_mix_bwd")

    dw_in_full = jnp.concatenate([dw_main[:, :qkv], dw_f[:, :n_heads], dw_main[:, qkv:]], axis=1)
    dw_in = jnp.transpose(dw_in_full.reshape(d_model, N_CHIPS, shard_cols), (1, 0, 2))
    local = [dw_in, dw_out.reshape(N_CHIPS, -1, d_model), dw_ff1, dw_ff2.reshape(N_CHIPS, -1, d_model)]
    names = ["w_in", "w_out", "w_ff1", "w_ff2"]
    from_sibling = _swap_halves(local, name="grads_swap_halves")
    core = lax.axis_index("c").astype(jnp.int32).reshape(1)
    pair_sums = [_add_halves(g, r, core, name="grads_pair_sum_" + nm) for g, r, nm in zip(local, from_sibling, names)]
    slots = _send_partials(pair_sums, name="grads_to_chips")
    halves = [_sum_chips(s, name="grads_chip_sum_" + nm) for s, nm in zip(slots, names)]
    big_g = _join_halves(halves, name="grads_join_halves")
    big = {}
    for nm, g, w, m, v in zip(names, big_g, (w_in, w_out, w_ff1, w_ff2), (m_w_in, m_w_out, m_w_ff1, m_w_ff2),
                              (v_w_in, v_w_out, v_w_ff1, v_w_ff2)):
        d, mo, vo = _adamw(w[0], g, m[0], v[0], name="adamw_" + nm)
        big[nm] = tuple(t[None] for t in (g, d, mo, vo))

    small_w = [norm_mix_g, b_f, gmlp_ln_g, gmlp_ln_b, w_s, b_s, attn_out_g, gmlp_out_g, norm_ffn_g, norm_final_g]
    small_m = [m_norm_mix_g, m_b_f, m_gmlp_ln_g, m_gmlp_ln_b, m_w_s, m_b_s, m_attn_out_g, m_gmlp_out_g, m_norm_ffn_g, m_norm_final_g]
    small_v = [v_norm_mix_g, v_b_f, v_gmlp_ln_g, v_gmlp_ln_b, v_w_s, v_b_s, v_attn_out_g, v_gmlp_out_g, v_norm_ffn_g, v_norm_final_g]
    small_g = [dg_mix, db_f[:, :n_heads], dlg, dlb, dw_s, db_col.T, dag, dgg, dg_ffn, dg_final]
    shapes = [w.shape for w in small_w]
    gathered = _allgather_small(_pack(small_g), name="allgather_small_grads")
    packed = _adamw_small(gathered, _pack(small_w), _pack(small_m), _pack(small_v), name="adamw_small")
    sg, sd, sm, sv = (_unpack(p, shapes) for p in packed)
    small_names = ["norm_mix_g", "b_f", "gmlp_ln_g", "gmlp_ln_b", "w_s", "b_s", "attn_out_g", "gmlp_out_g", "norm_ffn_g", "norm_final_g"]
    small = {nm: (sg[i], sd[i], sm[i], sv[i]) for i, nm in enumerate(small_names)}

    order = ["norm_mix_g", "w_in", "b_f", "gmlp_ln_g", "gmlp_ln_b", "w_s", "b_s", "attn_out_g", "gmlp_out_g", "w_out",
             "norm_ffn_g", "w_ff1", "w_ff2", "norm_final_g"]
    result = {**small, **big}
    total_loss = lax.psum(loss[0, 0], ("x", "y", "c"))
    outs = [total_loss, grad_x.reshape(x.shape)]
    for part in range(4):
        outs += [result[nm][part] for nm in order]
    return tuple(outs)
```

```python
import functools
import math

import numpy as np
import jax
import jax.numpy as jnp
from jax import lax
from jax.experimental import pallas as pl
from jax.experimental.pallas import tpu as pltpu

HEAD_DIM = 128
CHUNK = 128
EPS = 1e-6
LANES = 128
N_CHIPS = 4
N_DEV = 8
VMEM_LIMIT_BYTES = 56 * 1024 * 1024

ADAM_LR = 0.001
ADAM_B1 = 0.9
ADAM_B2 = 0.999
ADAM_EPS = 1e-08
ADAM_WD = 0.01
ADAM_STEP = 10

BF16 = jnp.bfloat16
F32 = jnp.float32
MESH = pl.DeviceIdType.MESH
ANY = pl.BlockSpec(memory_space=pl.ANY)
NEG_BIG = -1e30


def _params(*sem):
    return pltpu.CompilerParams(dimension_semantics=tuple(sem), vmem_limit_bytes=VMEM_LIMIT_BYTES)


def _tile(n, pref, unit):
    t = (min(pref, n) // unit) * unit
    while t >= unit:
        if n % t == 0:
            return t
        t -= unit
    return n


def _matmul(a, b, *, name, out_dtype, trans_a=False, trans_b=False, tm=1024, tn=1024, tk=512,
            square_lhs=False, relu=False, residual=None, scale2_by=None,
            b_sharded=False, out_sharded=False):
    m, k = (a.shape[1], a.shape[0]) if trans_a else a.shape
    if b_sharded:
        if trans_b:
            n, ks = b.shape[1], b.shape[2]
            assert N_CHIPS * ks == k
        else:
            ns = b.shape[2]
            n = N_CHIPS * ns
            assert b.shape[1] == k
    else:
        n = b.shape[0] if trans_b else b.shape[1]
        assert (b.shape[1] if trans_b else b.shape[0]) == k
    tm, tn, tk = _tile(m, tm, 128), _tile(n, tn, 128), _tile(k, tk, 128)
    nk = k // tk

    if trans_a:
        a_spec = pl.BlockSpec((tk, tm), lambda i, j, kk: (kk, i))
    else:
        a_spec = pl.BlockSpec((tm, tk), lambda i, j, kk: (i, kk))
    if b_sharded and trans_b:
        per = ks // tk
        assert per * tk == ks
        b_spec = pl.BlockSpec((None, tn, tk), lambda i, j, kk: (kk // per, j, kk % per))
    elif b_sharded:
        per = ns // tn
        assert per * tn == ns
        b_spec = pl.BlockSpec((None, tk, tn), lambda i, j, kk: (j // per, kk, j % per))
    elif trans_b:
        b_spec = pl.BlockSpec((tn, tk), lambda i, j, kk: (j, kk))
    else:
        b_spec = pl.BlockSpec((tk, tn), lambda i, j, kk: (kk, j))
    if out_sharded:
        ns_out = n // N_CHIPS
        per_o = ns_out // tn
        assert per_o * tn == ns_out
        out_shape = jax.ShapeDtypeStruct((N_CHIPS, m, ns_out), out_dtype)
        o_spec = pl.BlockSpec((None, tm, tn), lambda i, j, kk: (j // per_o, i, j % per_o))
    else:
        out_shape = jax.ShapeDtypeStruct((m, n), out_dtype)
        o_spec = pl.BlockSpec((tm, tn), lambda i, j, kk: (i, j))
    mn_spec = pl.BlockSpec((tm, tn), lambda i, j, kk: (i, j))

    operands, in_specs = [a, b], [a_spec, b_spec]
    if scale2_by is not None:
        operands.append(scale2_by)
        in_specs.append(mn_spec)
    if residual is not None:
        operands.append(residual)
        in_specs.append(mn_spec)
    dims = (((0 if trans_a else 1,), (1 if trans_b else 0,)), ((), ()))

    def body(*refs):
        a_ref, b_ref = refs[0], refs[1]
        pos = 2
        scale_ref = res_ref = None
        if scale2_by is not None:
            scale_ref = refs[pos]
            pos += 1
        if residual is not None:
            res_ref = refs[pos]
            pos += 1
        o_ref, acc_ref = refs[pos], refs[pos + 1]
        kk = pl.program_id(2)

        @pl.when(kk == 0)
        def _():
            acc_ref[...] = jnp.zeros_like(acc_ref)

        av = a_ref[...]
        if square_lhs:
            av = av.astype(F32)
            av = av * av
        acc_ref[...] += lax.dot_general(av.astype(BF16), b_ref[...].astype(BF16), dims,
                                        preferred_element_type=F32)

        @pl.when(kk == nk - 1)
        def _():
            r = acc_ref[...]
            if relu:
                r = jnp.maximum(r, 0.0)
            if scale_ref is not None:
                r = r * (2.0 * scale_ref[...].astype(F32))
            if res_ref is not None:
                r = r + res_ref[...].astype(F32)
            o_ref[...] = r.astype(out_dtype)

    return pl.pallas_call(
        body, name=name, out_shape=out_shape, grid=(m // tm, n // tn, nk),
        in_specs=in_specs, out_specs=o_spec,
        scratch_shapes=[pltpu.VMEM((tm, tn), F32)],
        compiler_params=_params("parallel", "parallel", "arbitrary"),
    )(*operands)


def _rmsnorm_fwd(x, g, *, name, tr=512):
    s, d = x.shape
    tr = _tile(s, tr, 8)

    def body(x_ref, g_ref, o_ref):
        xv = x_ref[...]
        r = lax.rsqrt(jnp.mean(xv * xv, axis=-1, keepdims=True) + EPS)
        o_ref[...] = ((xv * r) * g_ref[...]).astype(BF16)

    return pl.pallas_call(
        body, name=name, out_shape=jax.ShapeDtypeStruct((s, d), BF16), grid=(s // tr,),
        in_specs=[pl.BlockSpec((tr, d), lambda i: (i, 0)), pl.BlockSpec((1, d), lambda i: (0, 0))],
        out_specs=pl.BlockSpec((tr, d), lambda i: (i, 0)),
        compiler_params=_params("parallel"),
    )(x, g)


def _rms_bwd_rows(dy, xv, g):
    d = xv.shape[-1]
    r = lax.rsqrt(jnp.mean(xv * xv, axis=-1, keepdims=True) + EPS)
    gdy = dy * g
    dot = jnp.sum(gdy * xv, axis=-1, keepdims=True)
    dx = gdy * r - xv * (r * r * r) * (dot / d)
    return dx, dy * (xv * r)


def _rmsnorm_bwd(dy, x, res, g, *, name, tr=256):
    s, d = x.shape
    tr = _tile(s, tr, 8)

    def body(dy_ref, x_ref, res_ref, g_ref, dx_ref, dg_ref):
        @pl.when(pl.program_id(0) == 0)
        def _():
            dg_ref[...] = jnp.zeros_like(dg_ref)

        dx, dg_rows = _rms_bwd_rows(dy_ref[...].astype(F32), x_ref[...], g_ref[...])
        dx_ref[...] = res_ref[...] + dx
        dg_ref[...] += jnp.sum(dg_rows, axis=0, keepdims=True)

    row = pl.BlockSpec((tr, d), lambda i: (i, 0))
    vec = pl.BlockSpec((1, d), lambda i: (0, 0))
    return pl.pallas_call(
        body, name=name,
        out_shape=(jax.ShapeDtypeStruct((s, d), F32), jax.ShapeDtypeStruct((1, d), F32)),
        grid=(s // tr,), in_specs=[row, row, row, vec], out_specs=(row, vec),
        compiler_params=_params("arbitrary"),
    )(dy, x, res, g)


def _loss_and_final_bwd(x2, target, g, *, name, tr=256):
    s, d = x2.shape
    tr = _tile(s, tr, 8)

    def body(x_ref, t_ref, g_ref, dx_ref, dg_ref, loss_ref):
        @pl.when(pl.program_id(0) == 0)
        def _():
            dg_ref[...] = jnp.zeros_like(dg_ref)
            loss_ref[...] = jnp.zeros_like(loss_ref)

        xv, gv = x_ref[...], g_ref[...]
        r = lax.rsqrt(jnp.mean(xv * xv, axis=-1, keepdims=True) + EPS)
        err = (xv * r) * gv - t_ref[...]
        row_loss = jnp.mean(err * err, axis=-1, keepdims=True)
        loss_ref[...] += 0.5 * jnp.sum(row_loss, axis=0, keepdims=True)
        dx, dg_rows = _rms_bwd_rows(err / d, xv, gv)
        dx_ref[...] = dx
        dg_ref[...] += jnp.sum(dg_rows, axis=0, keepdims=True)

    row = pl.BlockSpec((tr, d), lambda i: (i, 0))
    vec = pl.BlockSpec((1, d), lambda i: (0, 0))
    one = pl.BlockSpec((1, 1), lambda i: (0, 0))
    return pl.pallas_call(
        body, name=name,
        out_shape=(jax.ShapeDtypeStruct((s, d), F32), jax.ShapeDtypeStruct((1, d), F32),
                   jax.ShapeDtypeStruct((1, 1), F32)),
        grid=(s // tr,), in_specs=[row, row, vec], out_specs=(row, vec, one),
        compiler_params=_params("arbitrary"),
    )(x2, target, g)


def _tri_ones(n, lower):
    r = lax.broadcasted_iota(jnp.int32, (n, n), 0)
    c = lax.broadcasted_iota(jnp.int32, (n, n), 1)
    return jnp.where((c <= r) if lower else (c >= r), 1.0, 0.0).astype(F32)


def _forget_fwd(h, w_f, b_f, *, name, tr=256):
    s, d = h.shape
    tr = _tile(s, tr, 8)

    def body(h_ref, w_ref, b_ref, zb_ref, f_ref, carry):
        @pl.when(pl.program_id(0) == 0)
        def _():
            carry[...] = jnp.zeros_like(carry)

        zb = jnp.dot(h_ref[...], w_ref[...], preferred_element_type=F32) + b_ref[...]
        zb_ref[...] = zb
        log_f = jnp.minimum(zb, 0.0) - jnp.log(1.0 + jnp.exp(-jnp.abs(zb)))
        run = jnp.dot(_tri_ones(tr, True), log_f, preferred_element_type=F32,
                      precision=lax.Precision.HIGHEST) + carry[...]
        f_ref[...] = run
        carry[...] = run[tr - 1:tr, :]

    row = pl.BlockSpec((tr, LANES), lambda i: (i, 0))
    return pl.pallas_call(
        body, name=name,
        out_shape=(jax.ShapeDtypeStruct((s, LANES), F32), jax.ShapeDtypeStruct((s, LANES), F32)),
        grid=(s // tr,),
        in_specs=[pl.BlockSpec((tr, d), lambda i: (i, 0)), pl.BlockSpec((d, LANES), lambda i: (0, 0)),
                  pl.BlockSpec((1, LANES), lambda i: (0, 0))],
        out_specs=(row, row), scratch_shapes=[pltpu.VMEM((1, LANES), F32)],
        compiler_params=_params("arbitrary"),
    )(h, w_f, b_f)


def _forget_bwd(d_f, zb, *, name, tr=256):
    s = zb.shape[0]
    tr = _tile(s, tr, 8)
    nb = s // tr

    def body(df_ref, zb_ref, dz_ref, db_ref, carry):
        @pl.when(pl.program_id(0) == 0)
        def _():
            carry[...] = jnp.zeros_like(carry)
            db_ref[...] = jnp.zeros_like(db_ref)

        run = jnp.dot(_tri_ones(tr, False), df_ref[...], preferred_element_type=F32,
                      precision=lax.Precision.HIGHEST) + carry[...]
        carry[...] = run[0:1, :]
        dz = run / (1.0 + jnp.exp(zb_ref[...]))
        dz_ref[...] = dz.astype(BF16)
        db_ref[...] += jnp.sum(dz, axis=0, keepdims=True)

    row = pl.BlockSpec((tr, LANES), lambda i: (nb - 1 - i, 0))
    return pl.pallas_call(
        body, name=name,
        out_shape=(jax.ShapeDtypeStruct((s, LANES), BF16), jax.ShapeDtypeStruct((1, LANES), F32)),
        grid=(nb,), in_specs=[row, row], out_specs=(row, pl.BlockSpec((1, LANES), lambda i: (0, 0))),
        scratch_shapes=[pltpu.VMEM((1, LANES), F32)],
        compiler_params=_params("arbitrary"),
    )(d_f, zb)


def _pairs(nblk, by_kv):
    if by_kv:
        pr = [(i, j) for j in range(nblk) for i in range(j, nblk)]
    else:
        pr = [(i, j) for i in range(nblk) for j in range(i + 1)]
    return (jnp.asarray(np.array([p[0] for p in pr], np.int32)), jnp.asarray(np.array([p[1] for p in pr], np.int32)))


def _causal_mask(t):
    r = lax.broadcasted_iota(jnp.int32, (t, t), 0)
    c = lax.broadcasted_iota(jnp.int32, (t, t), 1)
    return c <= r


def _attn_fwd(z, f_col, f_row, n_heads, *, name, tb=512):
    s = z.shape[0]
    tb = _tile(s, tb, 128)
    nblk = s // tb
    qi, kj = _pairs(nblk, by_kv=False)
    scale = 1.0 / math.sqrt(HEAD_DIM)

    def body(qi_ref, kj_ref, q_ref, k_ref, v_ref, fq_ref, fk_ref, o_ref, lse_ref, m_sc, l_sc, acc_sc):
        p = pl.program_id(1)
        i, j = qi_ref[p], kj_ref[p]

        @pl.when(j == 0)
        def _():
            m_sc[...] = jnp.full_like(m_sc, NEG_BIG)
            l_sc[...] = jnp.zeros_like(l_sc)
            acc_sc[...] = jnp.zeros_like(acc_sc)

        def update(masked):
            sc = lax.dot_general(q_ref[...], k_ref[...], (((1,), (1,)), ((), ())), preferred_element_type=F32)
            sc = sc * scale + (fq_ref[...] - fk_ref[...])
            if masked:
                sc = jnp.where(_causal_mask(tb), sc, NEG_BIG)
            m_new = jnp.maximum(m_sc[...], jnp.max(sc, axis=-1, keepdims=True))
            alpha = jnp.exp(m_sc[...] - m_new)
            pv = jnp.exp(sc - m_new)
            l_sc[...] = alpha * l_sc[...] + jnp.sum(pv, axis=-1, keepdims=True)
            acc_sc[...] = alpha * acc_sc[...] + jnp.dot(pv.astype(BF16), v_ref[...], preferred_element_type=F32)
            m_sc[...] = m_new

        @pl.when(j < i)
        def _():
            update(False)

        @pl.when(j == i)
        def _():
            update(True)
            o_ref[...] = (acc_sc[...] / l_sc[...]).astype(BF16)
            lse_ref[...] = m_sc[...] + jnp.log(l_sc[...])

    h = n_heads
    grid_spec = pltpu.PrefetchScalarGridSpec(
        num_scalar_prefetch=2, grid=(h, int(qi.shape[0])),
        in_specs=[
            pl.BlockSpec((tb, HEAD_DIM), lambda hh, p, qi_r, kj_r: (qi_r[p], hh)),
            pl.BlockSpec((tb, HEAD_DIM), lambda hh, p, qi_r, kj_r: (kj_r[p], h + hh)),
            pl.BlockSpec((tb, HEAD_DIM), lambda hh, p, qi_r, kj_r: (kj_r[p], 2 * h + hh)),
            pl.BlockSpec((None, tb, 1), lambda hh, p, qi_r, kj_r: (hh, qi_r[p], 0)),
            pl.BlockSpec((None, 1, tb), lambda hh, p, qi_r, kj_r: (hh, 0, kj_r[p])),
        ],
        out_specs=(
            pl.BlockSpec((tb, HEAD_DIM), lambda hh, p, qi_r, kj_r: (qi_r[p], hh)),
            pl.BlockSpec((None, tb, 1), lambda hh, p, qi_r, kj_r: (hh, qi_r[p], 0)),
        ),
        scratch_shapes=[pltpu.VMEM((tb, 1), F32), pltpu.VMEM((tb, 1), F32), pltpu.VMEM((tb, HEAD_DIM), F32)],
    )
    return pl.pallas_call(
        body, name=name, grid_spec=grid_spec,
        out_shape=(jax.ShapeDtypeStruct((s, h * HEAD_DIM), BF16), jax.ShapeDtypeStruct((h, s, 1), F32)),
        compiler_params=_params("parallel", "arbitrary"),
    )(qi, kj, z, z, z, f_col, f_row)


def _attn_bwd(z, o, d_o, lse, f_col, f_row, n_heads, *, name, tb=512):
    s = z.shape[0]
    tb = _tile(s, tb, 128)
    nblk = s // tb
    qi, kj = _pairs(nblk, by_kv=True)
    scale = 1.0 / math.sqrt(HEAD_DIM)
    h = n_heads

    def body(qi_ref, kj_ref, q_ref, k_ref, v_ref, o_ref, do_ref, lse_ref, fq_ref, fk_ref,
             dq_ref, dk_ref, dv_ref, df_ref, dfq_ref, dq_sc, dk_sc, dv_sc, df_sc, dfq_sc):
        p = pl.program_id(1)
        i, j = qi_ref[p], kj_ref[p]

        @pl.when(p == 0)
        def _():
            dq_sc[...] = jnp.zeros_like(dq_sc)
            dfq_sc[...] = jnp.zeros_like(dfq_sc)

        @pl.when(i == j)
        def _():
            dk_sc[...] = jnp.zeros_like(dk_sc)
            dv_sc[...] = jnp.zeros_like(dv_sc)
            df_sc[...] = jnp.zeros_like(df_sc)

        def update(masked):
            q, k, v, do = q_ref[...], k_ref[...], v_ref[...], do_ref[...]
            delta = jnp.sum(do.astype(F32) * o_ref[...].astype(F32), axis=-1, keepdims=True)
            sc = lax.dot_general(q, k, (((1,), (1,)), ((), ())), preferred_element_type=F32)
            sc = sc * scale + (fq_ref[...] - fk_ref[...])
            pv = jnp.exp(sc - lse_ref[...])
            if masked:
                pv = jnp.where(_causal_mask(tb), pv, 0.0)
            dp = lax.dot_general(do, v, (((1,), (1,)), ((), ())), preferred_element_type=F32)
            ds = pv * (dp - delta)
            ds_b = ds.astype(BF16)
            dv_sc[...] += lax.dot_general(pv.astype(BF16), do, (((0,), (0,)), ((), ())), preferred_element_type=F32)
            dk_sc[...] += lax.dot_general(ds_b, q, (((0,), (0,)), ((), ())), preferred_element_type=F32)
            rows = pl.ds(pl.multiple_of(i * tb, tb), tb)
            dq_sc[rows, :] += jnp.dot(ds_b, k, preferred_element_type=F32)
            df_sc[...] -= jnp.sum(ds, axis=0, keepdims=True)
            dfq_sc[rows, :] += jnp.sum(ds, axis=1, keepdims=True)

        @pl.when(i > j)
        def _():
            update(False)

        @pl.when(i == j)
        def _():
            update(True)

        @pl.when(i == nblk - 1)
        def _():
            dk_ref[...] = (dk_sc[...] * scale).astype(BF16)
            dv_ref[...] = dv_sc[...].astype(BF16)
            df_ref[...] = df_sc[...]

        @pl.when(p == pl.num_programs(1) - 1)
        def _():
            dq_ref[...] = (dq_sc[...] * scale).astype(BF16)
            dfq_ref[...] = dfq_sc[...]

    qblk = lambda off: pl.BlockSpec((tb, HEAD_DIM), lambda hh, p, qi_r, kj_r: (qi_r[p], off + hh))
    kblk = lambda off: pl.BlockSpec((tb, HEAD_DIM), lambda hh, p, qi_r, kj_r: (kj_r[p], off + hh))
    qcol = pl.BlockSpec((None, tb, 1), lambda hh, p, qi_r, kj_r: (hh, qi_r[p], 0))
    krow = pl.BlockSpec((None, 1, tb), lambda hh, p, qi_r, kj_r: (hh, 0, kj_r[p]))
    grid_spec = pltpu.PrefetchScalarGridSpec(
        num_scalar_prefetch=2, grid=(h, int(qi.shape[0])),
        in_specs=[qblk(0), kblk(h), kblk(2 * h), qblk(0), qblk(0), qcol, qcol, krow],
        out_specs=(
            pl.BlockSpec((s, HEAD_DIM), lambda hh, p, qi_r, kj_r: (0, hh)),
            kblk(0), kblk(0), krow,
            pl.BlockSpec((None, s, 1), lambda hh, p, qi_r, kj_r: (hh, 0, 0)),
        ),
        scratch_shapes=[pltpu.VMEM((s, HEAD_DIM), F32), pltpu.VMEM((tb, HEAD_DIM), F32),
                        pltpu.VMEM((tb, HEAD_DIM), F32), pltpu.VMEM((1, tb), F32), pltpu.VMEM((s, 1), F32)],
    )
    act = jax.ShapeDtypeStruct((s, h * HEAD_DIM), BF16)
    return pl.pallas_call(
        body, name=name, grid_spec=grid_spec,
        out_shape=(act, act, act, jax.ShapeDtypeStruct((h, 1, s), F32), jax.ShapeDtypeStruct((h, s, 1), F32)),
        compiler_params=_params("parallel", "arbitrary"),
    )(qi, kj, z, z, z, o, d_o, lse, f_col, f_row)


GELU_C = math.sqrt(2.0 / math.pi)
GELU_A = 0.044715


def _gelu(x):
    return 0.5 * x * (1.0 + jnp.tanh(GELU_C * (x + GELU_A * (x * x * x))))


def _gelu_and_grad(x):
    t = jnp.tanh(GELU_C * (x + GELU_A * (x * x * x)))
    y = 0.5 * x * (1.0 + t)
    dy = 0.5 * (1.0 + t) + 0.5 * x * (1.0 - t * t) * (GELU_C * (1.0 + 3.0 * GELU_A * (x * x)))
    return y, dy


def _layernorm_parts(g):
    mu = jnp.mean(g, axis=-1, keepdims=True)
    xc = g - mu
    rs = lax.rsqrt(jnp.mean(xc * xc, axis=-1, keepdims=True) + EPS)
    return xc * rs, rs


def _spatial_mix(w_ref, bcol_ref, vv_b, n_heads, n_chunks):
    tril = _causal_mask(CHUNK)
    cols = []
    for hh in range(n_heads):
        wc = jnp.where(tril, w_ref[hh], 0.0).astype(BF16)
        lanes = slice(hh * HEAD_DIM, (hh + 1) * HEAD_DIM)
        rows = [jnp.dot(wc, vv_b[c * CHUNK:(c + 1) * CHUNK, lanes], preferred_element_type=F32)
                + bcol_ref[:, hh:hh + 1] for c in range(n_chunks)]
        cols.append(jnp.concatenate(rows, axis=0))
    return jnp.concatenate(cols, axis=1)


def _mix_fwd(z, o, ln_g, ln_b, w_s, b_col, attn_g, gm_g, n_heads, *, name, tr=256):
    s = z.shape[0]
    dg = n_heads * HEAD_DIM
    tr = _tile(s, tr, CHUNK)
    n_chunks = tr // CHUNK

    def body(zu_ref, zv_ref, o_ref, lg_ref, lb_ref, w_ref, bcol_ref, ag_ref, gg_ref, out_ref):
        u = _gelu(zu_ref[...].astype(F32))
        xhat, _ = _layernorm_parts(_gelu(zv_ref[...].astype(F32)))
        vv = xhat * lg_ref[...] + lb_ref[...]
        gm = u * _spatial_mix(w_ref, bcol_ref, vv.astype(BF16), n_heads, n_chunks)
        rg = lax.rsqrt(jnp.mean(gm * gm, axis=-1, keepdims=True) + EPS)
        ov = o_ref[...].astype(F32)
        ra = lax.rsqrt(jnp.mean(ov * ov, axis=-1, keepdims=True) + EPS)
        out_ref[:, :dg] = ((ov * ra) * ag_ref[...]).astype(BF16)
        out_ref[:, dg:] = ((gm * rg) * gg_ref[...]).astype(BF16)

    vec = pl.BlockSpec((1, dg), lambda i: (0, 0))
    return pl.pallas_call(
        body, name=name, out_shape=jax.ShapeDtypeStruct((s, 2 * dg), BF16), grid=(s // tr,),
        in_specs=[pl.BlockSpec((tr, dg), lambda i: (i, 3)), pl.BlockSpec((tr, dg), lambda i: (i, 4)),
                  pl.BlockSpec((tr, dg), lambda i: (i, 0)), vec, vec,
                  pl.BlockSpec((n_heads, CHUNK, CHUNK), lambda i: (0, 0, 0)),
                  pl.BlockSpec((CHUNK, n_heads), lambda i: (0, 0)), vec, vec],
        out_specs=pl.BlockSpec((tr, 2 * dg), lambda i: (i, 0)),
        compiler_params=_params("parallel"),
    )(z, z, o, ln_g, ln_b, w_s, b_col, attn_g, gm_g)


def _mix_bwd(z, o, d_merged, ln_g, ln_b, w_s, b_col, attn_g, gm_g, n_heads, *, name, tr=256):
    s = z.shape[0]
    dg = n_heads * HEAD_DIM
    tr = _tile(s, tr, CHUNK)
    n_chunks = tr // CHUNK

    def body(zu_ref, zv_ref, o_ref, dm_ref, lg_ref, lb_ref, w_ref, bcol_ref, ag_ref, gg_ref,
             do_ref, dzu_ref, dzv_ref, dw_ref, dbcol_ref, dlg_ref, dlb_ref, dag_ref, dgg_ref):
        @pl.when(pl.program_id(0) == 0)
        def _():
            for ref in (dw_ref, dbcol_ref, dlg_ref, dlb_ref, dag_ref, dgg_ref):
                ref[...] = jnp.zeros_like(ref)

        d_o, dag_rows = _rms_bwd_rows(dm_ref[:, :dg], o_ref[...].astype(F32), ag_ref[...])
        do_ref[...] = d_o.astype(BF16)
        dag_ref[...] += jnp.sum(dag_rows, axis=0, keepdims=True)

        u, du_dz = _gelu_and_grad(zu_ref[...].astype(F32))
        gv, dgv_dz = _gelu_and_grad(zv_ref[...].astype(F32))
        xhat, rs = _layernorm_parts(gv)
        lg = lg_ref[...]
        vv_b = (xhat * lg + lb_ref[...]).astype(BF16)
        mix = _spatial_mix(w_ref, bcol_ref, vv_b, n_heads, n_chunks)
        gm = u * mix
        d_gm, dgg_rows = _rms_bwd_rows(dm_ref[:, dg:], gm, gg_ref[...])
        dgg_ref[...] += jnp.sum(dgg_rows, axis=0, keepdims=True)
        dzu_ref[...] = ((d_gm * mix) * du_dz).astype(BF16)
        d_mix = d_gm * u
        d_mix_b = d_mix.astype(BF16)

        tril = _causal_mask(CHUNK)
        lane = lax.broadcasted_iota(jnp.int32, (CHUNK, n_heads), 1)
        cols = []
        db = jnp.zeros((CHUNK, n_heads), F32)
        for hh in range(n_heads):
            wc = jnp.where(tril, w_ref[hh], 0.0).astype(BF16)
            lanes = slice(hh * HEAD_DIM, (hh + 1) * HEAD_DIM)
            dw = jnp.zeros((CHUNK, CHUNK), F32)
            dmix_sum = jnp.zeros((CHUNK, HEAD_DIM), F32)
            rows = []
            for c in range(n_chunks):
                rws = slice(c * CHUNK, (c + 1) * CHUNK)
                dmb = d_mix_b[rws, lanes]
                dw += lax.dot_general(dmb, vv_b[rws, lanes], (((1,), (1,)), ((), ())), preferred_element_type=F32)
                rows.append(lax.dot_general(wc, dmb, (((0,), (0,)), ((), ())), preferred_element_type=F32))
                dmix_sum += d_mix[rws, lanes]
            dw_ref[hh] += jnp.where(tril, dw, 0.0)
            db += jnp.where(lane == hh, jnp.sum(dmix_sum, axis=-1, keepdims=True), 0.0)
            cols.append(jnp.concatenate(rows, axis=0))
        dbcol_ref[...] += db
        d_vv = jnp.concatenate(cols, axis=1)

        dlg_ref[...] += jnp.sum(d_vv * xhat, axis=0, keepdims=True)
        dlb_ref[...] += jnp.sum(d_vv, axis=0, keepdims=True)
        d_xhat = d_vv * lg
        d_gv = rs * (d_xhat - jnp.mean(d_xhat, axis=-1, keepdims=True)
                     - xhat * jnp.mean(d_xhat * xhat, axis=-1, keepdims=True))
        dzv_ref[...] = (d_gv * dgv_dz).astype(BF16)

    vec = pl.BlockSpec((1, dg), lambda i: (0, 0))
    wspec = pl.BlockSpec((n_heads, CHUNK, CHUNK), lambda i: (0, 0, 0))
    bspec = pl.BlockSpec((CHUNK, n_heads), lambda i: (0, 0))
    rowb = pl.BlockSpec((tr, dg), lambda i: (i, 0))
    act = jax.ShapeDtypeStruct((s, dg), BF16)
    vshape = jax.ShapeDtypeStruct((1, dg), F32)
    return pl.pallas_call(
        body, name=name,
        out_shape=(act, act, act, jax.ShapeDtypeStruct((n_heads, CHUNK, CHUNK), F32),
                   jax.ShapeDtypeStruct((CHUNK, n_heads), F32), vshape, vshape, vshape, vshape),
        grid=(s // tr,),
        in_specs=[pl.BlockSpec((tr, dg), lambda i: (i, 3)), pl.BlockSpec((tr, dg), lambda i: (i, 4)),
                  rowb, pl.BlockSpec((tr, 2 * dg), lambda i: (i, 0)), vec, vec, wspec, bspec, vec, vec],
        out_specs=(rowb, rowb, rowb, wspec, bspec, vec, vec, vec, vec),
        compiler_params=_params("arbitrary"),
    )(z, z, o, d_merged, ln_g, ln_b, w_s, b_col, attn_g, gm_g)


def _place():
    x, y, c = lax.axis_index("x"), lax.axis_index("y"), lax.axis_index("c")
    other_chips = [(1 - x, y), (x, 1 - y), (1 - x, 1 - y)]
    return x, y, c, other_chips


def _remote(src, dst, send_sem, recv_sem, to):
    return pltpu.make_async_remote_copy(src_ref=src, dst_ref=dst, send_sem=send_sem, recv_sem=recv_sem,
                                        device_id=to, device_id_type=MESH)


def _cast_into_slot(w, place, *, name, tr=256):
    rows, cols = w.shape
    tr = _tile(rows, tr, 16)

    def body(place_ref, w_ref, o_ref):
        o_ref[...] = w_ref[...].astype(BF16)

    grid_spec = pltpu.PrefetchScalarGridSpec(
        num_scalar_prefetch=1, grid=(rows // tr,),
        in_specs=[pl.BlockSpec((tr, cols), lambda i, pr: (i, 0))],
        out_specs=pl.BlockSpec((None, tr, cols), lambda i, pr: (pr[0], i, 0)),
    )
    return pl.pallas_call(
        body, name=name, grid_spec=grid_spec, out_shape=jax.ShapeDtypeStruct((N_CHIPS, rows, cols), BF16),
        compiler_params=_params("parallel"),
    )(place, w)


def _allgather_weights(bufs, *, name):
    n = len(bufs)

    def body(*refs):
        outs = refs[n:2 * n]
        send_sems, recv_sems = refs[2 * n:]
        x, y, c, chips = _place()
        me = 2 * x + y
        sibling = (x, y, 1 - c)
        sent = []
        for t in range(n):
            half = outs[t].shape[1] // 2
            mine = outs[t].at[me, pl.ds(c * half, half), :]
            for k, (cx, cy) in enumerate(chips):
                cp = _remote(mine, mine, send_sems.at[6 * t + k], recv_sems.at[6 * t + k], (cx, cy, c))
                cp.start()
                sent.append(cp)
        for t in range(n):
            half = outs[t].shape[1] // 2
            for k, (cx, cy) in enumerate(chips):
                blk = outs[t].at[2 * cx + cy, pl.ds(c * half, half), :]
                _remote(blk, blk, send_sems.at[6 * t + k], recv_sems.at[6 * t + k], sibling).wait_recv()
                fwd = _remote(blk, blk, send_sems.at[6 * t + 3 + k], recv_sems.at[6 * t + 3 + k], sibling)
                fwd.start()
                sent.append(fwd)
        for t in range(n):
            half = outs[t].shape[1] // 2
            for k, (cx, cy) in enumerate(chips):
                blk = outs[t].at[2 * cx + cy, pl.ds((1 - c) * half, half), :]
                _remote(blk, blk, send_sems.at[6 * t + 3 + k], recv_sems.at[6 * t + 3 + k], sibling).wait_recv()
        for cp in sent:
            cp.wait_send()

    return pl.pallas_call(
        body, name=name,
        out_shape=tuple(jax.ShapeDtypeStruct(a.shape, a.dtype) for a in bufs),
        in_specs=[ANY] * n, out_specs=tuple([ANY] * n), input_output_aliases={t: t for t in range(n)},
        scratch_shapes=[pltpu.SemaphoreType.DMA((6 * n,)), pltpu.SemaphoreType.DMA((6 * n,))],
    )(*bufs)


def _swap_halves(grads, *, name):
    n = len(grads)

    def body(*refs):
        ins, outs = refs[:n], refs[n:2 * n]
        send_sems, recv_sems = refs[2 * n:]
        x, y, c, _ = _place()
        copies = []
        for t in range(n):
            half = ins[t].shape[1] // 2
            cp = _remote(ins[t].at[:, pl.ds((1 - c) * half, half), :], outs[t],
                         send_sems.at[t], recv_sems.at[t], (x, y, 1 - c))
            cp.start()
            copies.append(cp)
        for cp in copies:
            cp.wait()

    return pl.pallas_call(
        body, name=name,
        out_shape=tuple(jax.ShapeDtypeStruct((a.shape[0], a.shape[1] // 2, a.shape[2]), a.dtype) for a in grads),
        in_specs=[ANY] * n, out_specs=tuple([ANY] * n),
        scratch_shapes=[pltpu.SemaphoreType.DMA((n,)), pltpu.SemaphoreType.DMA((n,))],
    )(*grads)


def _add_halves(grad, received, place, *, name, tr=256):
    ns, half, cols = received.shape
    tr = _tile(half, tr, 16)
    per = half // tr

    def body(place_ref, g_ref, r_ref, o_ref):
        o_ref[...] = (g_ref[...].astype(F32) + r_ref[...].astype(F32)).astype(BF16)

    grid_spec = pltpu.PrefetchScalarGridSpec(
        num_scalar_prefetch=1, grid=(ns, per),
        in_specs=[pl.BlockSpec((None, tr, cols), lambda s, i, pr: (s, pr[1] * per + i, 0)),
                  pl.BlockSpec((None, tr, cols), lambda s, i, pr: (s, i, 0))],
        out_specs=pl.BlockSpec((None, tr, cols), lambda s, i, pr: (s, i, 0)),
    )
    return pl.pallas_call(
        body, name=name, grid_spec=grid_spec, out_shape=jax.ShapeDtypeStruct(received.shape, BF16),
        compiler_params=_params("parallel", "parallel"),
    )(place, grad, received)


def _send_partials(parts, *, name):
    n = len(parts)

    def body(*refs):
        ins, outs = refs[:n], refs[n:2 * n]
        send_sems, recv_sems = refs[2 * n:]
        x, y, c, chips = _place()
        me = 2 * x + y
        sent = []
        for t in range(n):
            for k, (cx, cy) in enumerate(chips):
                cp = _remote(ins[t].at[2 * cx + cy], outs[t].at[me],
                             send_sems.at[3 * t + k], recv_sems.at[3 * t + k], (cx, cy, c))
                cp.start()
                sent.append(cp)
        for t in range(n):
            for k, (cx, cy) in enumerate(chips):
                slot = outs[t].at[2 * cx + cy]
                _remote(slot, slot, send_sems.at[3 * t + k], recv_sems.at[3 * t + k], (cx, cy, c)).wait_recv()
        for cp in sent:
            cp.wait_send()

    return pl.pallas_call(
        body, name=name,
        out_shape=tuple(jax.ShapeDtypeStruct(a.shape, a.dtype) for a in parts),
        in_specs=[ANY] * n, out_specs=tuple([ANY] * n),
        scratch_shapes=[pltpu.SemaphoreType.DMA((3 * n,)), pltpu.SemaphoreType.DMA((3 * n,))],
    )(*parts)


def _sum_chips(parts, slots, place, *, name, tr=256):
    ns, half, cols = slots.shape
    tr = _tile(half, tr, 16)
    per = half // tr

    def body(place_ref, p_ref, s_ref, o_ref):
        acc = p_ref[...].astype(F32)
        for k in range(ns):
            acc = acc + jnp.where(place_ref[0] == k, 0.0, s_ref[k].astype(F32))
        o_ref[...] = acc

    grid_spec = pltpu.PrefetchScalarGridSpec(
        num_scalar_prefetch=1, grid=(per,),
        in_specs=[pl.BlockSpec((None, tr, cols), lambda i, pr: (pr[0], i, 0)),
                  pl.BlockSpec((ns, tr, cols), lambda i, pr: (0, i, 0))],
        out_specs=pl.BlockSpec((tr, cols), lambda i, pr: (pr[1] * per + i, 0)),
    )
    return pl.pallas_call(
        body, name=name, grid_spec=grid_spec, out_shape=jax.ShapeDtypeStruct((2 * half, cols), F32),
        compiler_params=_params("parallel"),
    )(place, parts, slots)


def _join_halves(bufs, *, name):
    n = len(bufs)

    def body(*refs):
        outs = refs[n:2 * n]
        send_sems, recv_sems = refs[2 * n:]
        x, y, c, _ = _place()
        sent = []
        for t in range(n):
            half = outs[t].shape[0] // 2
            mine = outs[t].at[pl.ds(c * half, half), :]
            cp = _remote(mine, mine, send_sems.at[t], recv_sems.at[t], (x, y, 1 - c))
            cp.start()
            sent.append(cp)
        for t in range(n):
            half = outs[t].shape[0] // 2
            theirs = outs[t].at[pl.ds((1 - c) * half, half), :]
            _remote(theirs, theirs, send_sems.at[t], recv_sems.at[t], (x, y, 1 - c)).wait_recv()
        for cp in sent:
            cp.wait_send()

    return pl.pallas_call(
        body, name=name,
        out_shape=tuple(jax.ShapeDtypeStruct(a.shape, a.dtype) for a in bufs),
        in_specs=[ANY] * n, out_specs=tuple([ANY] * n), input_output_aliases={t: t for t in range(n)},
        scratch_shapes=[pltpu.SemaphoreType.DMA((n,)), pltpu.SemaphoreType.DMA((n,))],
    )(*bufs)


def _allgather_small(buf, *, name):
    rows = buf.shape[0]

    def body(x_ref, out_ref, send_sems, recv_sems, local_sem):
        x, y, c, chips = _place()
        sibling = (x, y, 1 - c)

        def slot(px, py, pc):
            return out_ref.at[4 * px + 2 * py + pc]

        def copy(k, block, to, src=None):
            return _remote(slot(*block) if src is None else src, slot(*block), send_sems.at[k], recv_sems.at[k], to)

        mine = pltpu.make_async_copy(x_ref, slot(x, y, c), local_sem)
        mine.start()
        first = [copy(0, (x, y, c), sibling, src=x_ref)]
        first += [copy(1 + k, (x, y, c), (*chip, c), src=x_ref) for k, chip in enumerate(chips)]
        for cp in first:
            cp.start()
        passed = [copy(4 + k, (*chip, c), sibling) for k, chip in enumerate(chips)]
        for k, chip in enumerate(chips):
            copy(1 + k, (*chip, c), (x, y, c)).wait_recv()
            passed[k].start()
        copy(0, (x, y, 1 - c), (x, y, c)).wait_recv()
        for k, chip in enumerate(chips):
            copy(4 + k, (*chip, 1 - c), (x, y, c)).wait_recv()
        for cp in first + passed:
            cp.wait_send()
        mine.wait()

    return pl.pallas_call(
        body, name=name, out_shape=jax.ShapeDtypeStruct((N_DEV, rows, LANES), buf.dtype),
        in_specs=[pl.BlockSpec(memory_space=pltpu.VMEM)], out_specs=pl.BlockSpec(memory_space=pltpu.VMEM),
        scratch_shapes=[pltpu.SemaphoreType.DMA((7,)), pltpu.SemaphoreType.DMA((7,)), pltpu.SemaphoreType.DMA],
    )(buf)


def _adamw_math(w, g, m, v):
    m = ADAM_B1 * m + (1.0 - ADAM_B1) * g
    v = ADAM_B2 * v + (1.0 - ADAM_B2) * (g * g)
    m_hat = m / (1.0 - ADAM_B1 ** ADAM_STEP)
    v_hat = v / (1.0 - ADAM_B2 ** ADAM_STEP)
    delta = -ADAM_LR * (m_hat / (jnp.sqrt(v_hat) + ADAM_EPS) + ADAM_WD * w)
    return delta, m, v


def _adamw(w, g, m, v, *, name, tr=256):
    rows, cols = w.shape
    tr = _tile(rows, tr, 8)

    def body(w_ref, g_ref, m_ref, v_ref, d_ref, mo_ref, vo_ref):
        d_ref[...], mo_ref[...], vo_ref[...] = _adamw_math(w_ref[...], g_ref[...], m_ref[...], v_ref[...])

    blk = pl.BlockSpec((tr, cols), lambda i: (i, 0))
    shape = jax.ShapeDtypeStruct((rows, cols), F32)
    return pl.pallas_call(
        body, name=name, out_shape=(shape, shape, shape), grid=(rows // tr,),
        in_specs=[blk] * 4, out_specs=(blk, blk, blk), compiler_params=_params("parallel"),
    )(w, g, m, v)


def _adamw_small(gathered, w, m, v, *, name):
    nd = gathered.shape[0]

    def body(gs_ref, w_ref, m_ref, v_ref, g_ref, d_ref, mo_ref, vo_ref):
        g = gs_ref[0]
        for k in range(1, nd):
            g = g + gs_ref[k]
        g_ref[...] = g
        d_ref[...], mo_ref[...], vo_ref[...] = _adamw_math(w_ref[...], g, m_ref[...], v_ref[...])

    shape = jax.ShapeDtypeStruct(w.shape, F32)
    return pl.pallas_call(body, name=name, out_shape=(shape, shape, shape, shape),
                          compiler_params=pltpu.CompilerParams(vmem_limit_bytes=VMEM_LIMIT_BYTES))(gathered, w, m, v)


def _pack(parts):
    flat = jnp.concatenate([p.reshape(-1).astype(F32) for p in parts])
    rows = -(-flat.shape[0] // (8 * LANES)) * 8
    return jnp.pad(flat, (0, rows * LANES - flat.shape[0])).reshape(rows, LANES)


def _unpack(buf, shapes):
    flat = buf.reshape(-1)
    out, pos = [], 0
    for shp in shapes:
        size = int(np.prod(shp))
        out.append(flat[pos:pos + size].reshape(shp))
        pos += size
    return out


def _shard_cols(g, lo, hi):
    cs = g.shape[2]
    pieces = []
    for j in range(g.shape[0]):
        a, b = max(lo, j * cs), min(hi, (j + 1) * cs)
        if a < b:
            pieces.append(g[j][:, a - j * cs:b - j * cs])
    return pieces


def _cols_from_segments(segments, lo, hi):
    pieces = []
    for first, last, src, at in segments:
        a, b = max(lo, first), min(hi, last)
        if a < b:
            pieces.append(src[:, at + a - first:at + b - first])
    return jnp.concatenate(pieces, axis=1)


def kernel(x, norm_mix_g, w_in, b_f, gmlp_ln_g, gmlp_ln_b, w_s, b_s, attn_out_g, gmlp_out_g, w_out, norm_ffn_g, w_ff1, w_ff2, norm_final_g, loss_target, m_norm_mix_g, m_w_in, m_b_f, m_gmlp_ln_g, m_gmlp_ln_b, m_w_s, m_b_s, m_attn_out_g, m_gmlp_out_g, m_w_out, m_norm_ffn_g, m_w_ff1, m_w_ff2, m_norm_final_g, v_norm_mix_g, v_w_in, v_b_f, v_gmlp_ln_g, v_gmlp_ln_b, v_w_s, v_b_s, v_attn_out_g, v_gmlp_out_g, v_w_out, v_norm_ffn_g, v_w_ff1, v_w_ff2, v_norm_final_g):
    seq, d_model = x.shape[1], x.shape[2]
    d_attn = d_model // 2
    n_heads = d_attn // HEAD_DIM
    qkv = 3 * d_attn
    shard_cols = w_in.shape[2]
    assert N_CHIPS * shard_cols == qkv + n_heads + 2 * d_attn
    xs = x.reshape(seq, d_model)
    target = loss_target.reshape(seq, d_model)

    place = jnp.stack([2 * lax.axis_index("x") + lax.axis_index("y"), lax.axis_index("c")]).astype(jnp.int32)
    names = ["w_in", "w_out", "w_ff1", "w_ff2"]
    g_in, g_out, g_ff1, g_ff2 = _allgather_weights(
        [_cast_into_slot(w[0], place, name="cast_" + nm) for w, nm in zip((w_in, w_out, w_ff1, w_ff2), names)],
        name="allgather_weights")
    n_cols = N_CHIPS * shard_cols
    w_main = jnp.concatenate(_shard_cols(g_in, 0, qkv) + _shard_cols(g_in, qkv + n_heads, n_cols), axis=1)
    w_f = jnp.pad(jnp.concatenate(_shard_cols(g_in, qkv, qkv + n_heads), axis=1), ((0, 0), (0, LANES - n_heads)))
    w_out_full = g_out.reshape(2 * d_attn, d_model)
    w_ff2_full = g_ff2.reshape(N_CHIPS * g_ff2.shape[1], d_model)
    b_f_pad = jnp.pad(b_f, ((0, 0), (0, LANES - n_heads)))
    b_col = b_s[0].T

    h = _rmsnorm_fwd(xs, norm_mix_g, name="norm_mix")
    z = _matmul(h, w_main, name="in_proj", out_dtype=BF16)
    zb, f_cum = _forget_fwd(h, w_f, b_f_pad, name="forget_fwd")
    f_heads = f_cum[:, :n_heads].T
    f_col, f_row = f_heads[:, :, None], f_heads[:, None, :]
    o, lse = _attn_fwd(z, f_col, f_row, n_heads, name="attn_fwd")
    merged = _mix_fwd(z, o, gmlp_ln_g, gmlp_ln_b, w_s[0], b_col, attn_out_g, gmlp_out_g, n_heads, name="mix_fwd")
    x1 = _matmul(merged, w_out_full, name="out_proj", out_dtype=F32, residual=xs)
    h2 = _rmsnorm_fwd(x1, norm_ffn_g, name="norm_ffn")
    a = _matmul(h2, g_ff1, name="ff1", out_dtype=BF16, relu=True, b_sharded=True)
    x2 = _matmul(a, w_ff2_full, name="ff2", out_dtype=F32, square_lhs=True, residual=x1)
    dx2, dg_final, loss = _loss_and_final_bwd(x2, target, norm_final_g.reshape(1, d_model), name="loss_head")

    da = _matmul(dx2, w_ff2_full, name="ff2_dlhs", out_dtype=BF16, trans_b=True, scale2_by=a)
    dw_ff2 = _matmul(a, dx2, name="ff2_dw", out_dtype=BF16, trans_a=True, square_lhs=True)
    dh2 = _matmul(da, g_ff1, name="ff1_dlhs", out_dtype=F32, trans_b=True, b_sharded=True)
    dw_ff1 = _matmul(h2, da, name="ff1_dw", out_dtype=BF16, trans_a=True, out_sharded=True)
    dx1, dg_ffn = _rmsnorm_bwd(dh2, x1, dx2, norm_ffn_g, name="norm_ffn_bwd")
    d_merged = _matmul(dx1, w_out_full, name="out_proj_dlhs", out_dtype=F32, trans_b=True)
    dw_out = _matmul(merged, dx1, name="out_proj_dw", out_dtype=BF16, trans_a=True)
    d_o, dzu, dzv, dw_s, db_col, dlg, dlb, dag, dgg = _mix_bwd(
        z, o, d_merged, gmlp_ln_g, gmlp_ln_b, w_s[0], b_col, attn_out_g, gmlp_out_g, n_heads, name="mix_bwd")
    dq, dk, dv, d_f_key, d_f_query = _attn_bwd(z, o, d_o, lse, f_col, f_row, n_heads, name="attn_bwd")
    d_f = d_f_key.reshape(n_heads, seq) + d_f_query.reshape(n_heads, seq)
    d_f_pad = jnp.pad(d_f.T, ((0, 0), (0, LANES - n_heads)))
    dzf, db_f = _forget_bwd(d_f_pad, zb, name="forget_bwd")
    dz = jnp.concatenate([dq, dk, dv, dzu, dzv], axis=1)
    dh_gate = _matmul(dzf, w_f, name="gate_dlhs", out_dtype=F32, trans_b=True)
    dh = _matmul(dz, w_main, name="in_proj_dlhs", out_dtype=F32, trans_b=True, residual=dh_gate)
    dw_main = _matmul(h, dz, name="in_proj_dw", out_dtype=BF16, trans_a=True)
    dw_f = _matmul(h, dzf, name="gate_dw", out_dtype=BF16, trans_a=True)
    grad_x, dg_mix = _rmsnorm_bwd(dh, xs, dx1, norm_mix_g, name="norm_mix_bwd")

    segments = [(0, qkv, dw_main, 0), (qkv, qkv + n_heads, dw_f, 0), (qkv + n_heads, n_cols, dw_main, qkv)]
    dw_in = jnp.stack([_cols_from_segments(segments, j * shard_cols, (j + 1) * shard_cols) for j in range(N_CHIPS)])
    local = [dw_in, dw_out.reshape(N_CHIPS, -1, d_model), dw_ff1, dw_ff2.reshape(N_CHIPS, -1, d_model)]
    from_sibling = _swap_halves(local, name="grads_swap_halves")
    pair_sums = [_add_halves(g, r, place, name="grads_pair_sum_" + nm) for g, r, nm in zip(local, from_sibling, names)]
    slots = _send_partials(pair_sums, name="grads_to_chips")
    reduced = [_sum_chips(p, s, place, name="grads_chip_sum_" + nm) for p, s, nm in zip(pair_sums, slots, names)]
    big_g = _join_halves(reduced, name="grads_join_halves")
    big = {}
    for nm, g, w, m, v in zip(names, big_g, (w_in, w_out, w_ff1, w_ff2), (m_w_in, m_w_out, m_w_ff1, m_w_ff2),
                              (v_w_in, v_w_out, v_w_ff1, v_w_ff2)):
        d, mo, vo = _adamw(w[0], g, m[0], v[0], name="adamw_" + nm)
        big[nm] = tuple(t[None] for t in (g, d, mo, vo))

    small_w = [norm_mix_g, b_f, gmlp_ln_g, gmlp_ln_b, w_s, b_s, attn_out_g, gmlp_out_g, norm_ffn_g, norm_final_g]
    small_m = [m_norm_mix_g, m_b_f, m_gmlp_ln_g, m_gmlp_ln_b, m_w_s, m_b_s, m_attn_out_g, m_gmlp_out_g, m_norm_ffn_g, m_norm_final_g]
    small_v = [v_norm_mix_g, v_b_f, v_gmlp_ln_g, v_gmlp_ln_b, v_w_s, v_b_s, v_attn_out_g, v_gmlp_out_g, v_norm_ffn_g, v_norm_final_g]
    small_g = [dg_mix, db_f[:, :n_heads], dlg, dlb, dw_s, db_col.T, dag, dgg, dg_ffn, dg_final]
    shapes = [w.shape for w in small_w]
    gathered = _allgather_small(_pack(small_g), name="allgather_small_grads")
    packed = _adamw_small(gathered, _pack(small_w), _pack(small_m), _pack(small_v), name="adamw_small")
    sg, sd, sm, sv = (_unpack(p, shapes) for p in packed)
    small_names = ["norm_mix_g", "b_f", "gmlp_ln_g", "gmlp_ln_b", "w_s", "b_s", "attn_out_g", "gmlp_out_g", "norm_ffn_g", "norm_final_g"]
    small = {nm: (sg[i], sd[i], sm[i], sv[i]) for i, nm in enumerate(small_names)}

    order = ["norm_mix_g", "w_in", "b_f", "gmlp_ln_g", "gmlp_ln_b", "w_s", "b_s", "attn_out_g", "gmlp_out_g", "w_out",
             "norm_ffn_g", "w_ff1", "w_ff2", "norm_final_g"]
    result = {**small, **big}
    total_loss = lax.psum(loss[0, 0], ("x", "y", "c"))
    outs = [total_loss, grad_x.reshape(x.shape)]
    for part in range(4):
        outs += [result[nm][part] for nm in order]
    return tuple(outs)
```

```python
import functools
import math

import numpy as np
import jax
import jax.numpy as jnp
from jax import lax
from jax.experimental import pallas as pl
from jax.experimental.pallas import tpu as pltpu

HEAD_DIM = 128
CHUNK = 128
EPS = 1e-6
LANES = 128
N_CHIPS = 4
N_DEV = 8
VMEM_LIMIT_BYTES = 56 * 1024 * 1024

ADAM_LR = 0.001
ADAM_B1 = 0.9
ADAM_B2 = 0.999
ADAM_EPS = 1e-08
ADAM_WD = 0.01
ADAM_STEP = 10

BF16 = jnp.bfloat16
F32 = jnp.float32
MESH = pl.DeviceIdType.MESH
ANY = pl.BlockSpec(memory_space=pl.ANY)
NEG_BIG = -1e30


def _params(*sem):
    return pltpu.CompilerParams(dimension_semantics=tuple(sem), vmem_limit_bytes=VMEM_LIMIT_BYTES)


def _tile(n, pref, unit):
    t = (min(pref, n) // unit) * unit
    while t >= unit:
        if n % t == 0:
            return t
        t -= unit
    return n


def _matmul(a, b, *, name, out_dtype, trans_a=False, trans_b=False, tm=1024, tn=1024, tk=2048,
            square_lhs=False, relu=False, residual=None, scale2_by=None,
            b_sharded=False, out_sharded=False):
    m, k = (a.shape[1], a.shape[0]) if trans_a else a.shape
    if b_sharded:
        if trans_b:
            n, ks = b.shape[1], b.shape[2]
            assert N_CHIPS * ks == k
        else:
            ns = b.shape[2]
            n = N_CHIPS * ns
            assert b.shape[1] == k
    else:
        n = b.shape[0] if trans_b else b.shape[1]
        assert (b.shape[1] if trans_b else b.shape[0]) == k
    tm, tn, tk = _tile(m, tm, 128), _tile(n, tn, 128), _tile(k, tk, 128)
    nk = k // tk

    if trans_a:
        a_spec = pl.BlockSpec((tk, tm), lambda i, j, kk: (kk, i))
    else:
        a_spec = pl.BlockSpec((tm, tk), lambda i, j, kk: (i, kk))
    if b_sharded and trans_b:
        per = ks // tk
        assert per * tk == ks
        b_spec = pl.BlockSpec((None, tn, tk), lambda i, j, kk: (kk // per, j, kk % per))
    elif b_sharded:
        per = ns // tn
        assert per * tn == ns
        b_spec = pl.BlockSpec((None, tk, tn), lambda i, j, kk: (j // per, kk, j % per))
    elif trans_b:
        b_spec = pl.BlockSpec((tn, tk), lambda i, j, kk: (j, kk))
    else:
        b_spec = pl.BlockSpec((tk, tn), lambda i, j, kk: (kk, j))
    if out_sharded:
        ns_out = n // N_CHIPS
        per_o = ns_out // tn
        assert per_o * tn == ns_out
        out_shape = jax.ShapeDtypeStruct((N_CHIPS, m, ns_out), out_dtype)
        o_spec = pl.BlockSpec((None, tm, tn), lambda i, j, kk: (j // per_o, i, j % per_o))
    else:
        out_shape = jax.ShapeDtypeStruct((m, n), out_dtype)
        o_spec = pl.BlockSpec((tm, tn), lambda i, j, kk: (i, j))
    mn_spec = pl.BlockSpec((tm, tn), lambda i, j, kk: (i, j))

    operands, in_specs = [a, b], [a_spec, b_spec]
    if scale2_by is not None:
        operands.append(scale2_by)
        in_specs.append(mn_spec)
    if residual is not None:
        operands.append(residual)
        in_specs.append(mn_spec)
    dims = (((0 if trans_a else 1,), (1 if trans_b else 0,)), ((), ()))

    def body(*refs):
        a_ref, b_ref = refs[0], refs[1]
        pos = 2
        scale_ref = res_ref = None
        if scale2_by is not None:
            scale_ref = refs[pos]
            pos += 1
        if residual is not None:
            res_ref = refs[pos]
            pos += 1
        o_ref = refs[pos]
        kk = pl.program_id(2)

        av = a_ref[...]
        if square_lhs:
            av = av.astype(F32)
            av = av * av
        part = lax.dot_general(av.astype(BF16), b_ref[...].astype(BF16), dims, preferred_element_type=F32)

        def finish(r):
            if relu:
                r = jnp.maximum(r, 0.0)
            if scale_ref is not None:
                r = r * (2.0 * scale_ref[...].astype(F32))
            if res_ref is not None:
                r = r + res_ref[...].astype(F32)
            o_ref[...] = r.astype(out_dtype)

        if nk == 1:
            finish(part)
        else:
            acc_ref = refs[pos + 1]

            @pl.when(kk == 0)
            def _():
                acc_ref[...] = part

            @pl.when(jnp.logical_and(kk > 0, kk < nk - 1))
            def _():
                acc_ref[...] += part

            @pl.when(kk == nk - 1)
            def _():
                finish(acc_ref[...] + part)

    return pl.pallas_call(
        body, name=name, out_shape=out_shape, grid=(m // tm, n // tn, nk),
        in_specs=in_specs, out_specs=o_spec,
        scratch_shapes=[pltpu.VMEM((tm, tn), F32)] if nk > 1 else [],
        compiler_params=_params("parallel", "parallel", "arbitrary"),
    )(*operands)


def _rmsnorm_fwd(x, g, *, name, tr=512):
    s, d = x.shape
    tr = _tile(s, tr, 8)

    def body(x_ref, g_ref, o_ref):
        xv = x_ref[...]
        r = lax.rsqrt(jnp.mean(xv * xv, axis=-1, keepdims=True) + EPS)
        o_ref[...] = ((xv * r) * g_ref[...]).astype(BF16)

    return pl.pallas_call(
        body, name=name, out_shape=jax.ShapeDtypeStruct((s, d), BF16), grid=(s // tr,),
        in_specs=[pl.BlockSpec((tr, d), lambda i: (i, 0)), pl.BlockSpec((1, d), lambda i: (0, 0))],
        out_specs=pl.BlockSpec((tr, d), lambda i: (i, 0)),
        compiler_params=_params("parallel"),
    )(x, g)


def _rms_bwd_rows(dy, xv, g):
    d = xv.shape[-1]
    r = lax.rsqrt(jnp.mean(xv * xv, axis=-1, keepdims=True) + EPS)
    gdy = dy * g
    dot = jnp.sum(gdy * xv, axis=-1, keepdims=True)
    dx = gdy * r - xv * (r * r * r) * (dot / d)
    return dx, dy * (xv * r)


def _rmsnorm_bwd(dy, x, res, g, *, name, tr=256):
    s, d = x.shape
    tr = _tile(s, tr, 8)

    def body(dy_ref, x_ref, res_ref, g_ref, dx_ref, dg_ref):
        @pl.when(pl.program_id(0) == 0)
        def _():
            dg_ref[...] = jnp.zeros_like(dg_ref)

        dx, dg_rows = _rms_bwd_rows(dy_ref[...].astype(F32), x_ref[...], g_ref[...])
        dx_ref[...] = res_ref[...] + dx
        dg_ref[...] += jnp.sum(dg_rows, axis=0, keepdims=True)

    row = pl.BlockSpec((tr, d), lambda i: (i, 0))
    vec = pl.BlockSpec((1, d), lambda i: (0, 0))
    return pl.pallas_call(
        body, name=name,
        out_shape=(jax.ShapeDtypeStruct((s, d), F32), jax.ShapeDtypeStruct((1, d), F32)),
        grid=(s // tr,), in_specs=[row, row, row, vec], out_specs=(row, vec),
        compiler_params=_params("arbitrary"),
    )(dy, x, res, g)


def _loss_and_final_bwd(x2, target, g, *, name, tr=256):
    s, d = x2.shape
    tr = _tile(s, tr, 8)

    def body(x_ref, t_ref, g_ref, dx_ref, dg_ref, loss_ref):
        @pl.when(pl.program_id(0) == 0)
        def _():
            dg_ref[...] = jnp.zeros_like(dg_ref)
            loss_ref[...] = jnp.zeros_like(loss_ref)

        xv, gv = x_ref[...], g_ref[...]
        r = lax.rsqrt(jnp.mean(xv * xv, axis=-1, keepdims=True) + EPS)
        err = (xv * r) * gv - t_ref[...]
        row_loss = jnp.mean(err * err, axis=-1, keepdims=True)
        loss_ref[...] += 0.5 * jnp.sum(row_loss, axis=0, keepdims=True)
        dx, dg_rows = _rms_bwd_rows(err / d, xv, gv)
        dx_ref[...] = dx
        dg_ref[...] += jnp.sum(dg_rows, axis=0, keepdims=True)

    row = pl.BlockSpec((tr, d), lambda i: (i, 0))
    vec = pl.BlockSpec((1, d), lambda i: (0, 0))
    one = pl.BlockSpec((1, 1), lambda i: (0, 0))
    return pl.pallas_call(
        body, name=name,
        out_shape=(jax.ShapeDtypeStruct((s, d), F32), jax.ShapeDtypeStruct((1, d), F32),
                   jax.ShapeDtypeStruct((1, 1), F32)),
        grid=(s // tr,), in_specs=[row, row, vec], out_specs=(row, vec, one),
        compiler_params=_params("arbitrary"),
    )(x2, target, g)


def _tri_ones(n, lower):
    r = lax.broadcasted_iota(jnp.int32, (n, n), 0)
    c = lax.broadcasted_iota(jnp.int32, (n, n), 1)
    return jnp.where((c <= r) if lower else (c >= r), 1.0, 0.0).astype(F32)


def _forget_fwd(h, w_f, b_f, *, name, tr=256):
    s, d = h.shape
    tr = _tile(s, tr, 8)

    def body(h_ref, w_ref, b_ref, zb_ref, f_ref, carry):
        @pl.when(pl.program_id(0) == 0)
        def _():
            carry[...] = jnp.zeros_like(carry)

        zb = jnp.dot(h_ref[...], w_ref[...], preferred_element_type=F32) + b_ref[...]
        zb_ref[...] = zb
        log_f = jnp.minimum(zb, 0.0) - jnp.log(1.0 + jnp.exp(-jnp.abs(zb)))
        run = jnp.dot(_tri_ones(tr, True), log_f, preferred_element_type=F32,
                      precision=lax.Precision.HIGHEST) + carry[...]
        f_ref[...] = run
        carry[...] = run[tr - 1:tr, :]

    row = pl.BlockSpec((tr, LANES), lambda i: (i, 0))
    return pl.pallas_call(
        body, name=name,
        out_shape=(jax.ShapeDtypeStruct((s, LANES), F32), jax.ShapeDtypeStruct((s, LANES), F32)),
        grid=(s // tr,),
        in_specs=[pl.BlockSpec((tr, d), lambda i: (i, 0)), pl.BlockSpec((d, LANES), lambda i: (0, 0)),
                  pl.BlockSpec((1, LANES), lambda i: (0, 0))],
        out_specs=(row, row), scratch_shapes=[pltpu.VMEM((1, LANES), F32)],
        compiler_params=_params("arbitrary"),
    )(h, w_f, b_f)


def _forget_bwd(d_f, zb, *, name, tr=256):
    s = zb.shape[0]
    tr = _tile(s, tr, 8)
    nb = s // tr

    def body(df_ref, zb_ref, dz_ref, db_ref, carry):
        @pl.when(pl.program_id(0) == 0)
        def _():
            carry[...] = jnp.zeros_like(carry)
            db_ref[...] = jnp.zeros_like(db_ref)

        run = jnp.dot(_tri_ones(tr, False), df_ref[...], preferred_element_type=F32,
                      precision=lax.Precision.HIGHEST) + carry[...]
        carry[...] = run[0:1, :]
        dz = run / (1.0 + jnp.exp(zb_ref[...]))
        dz_ref[...] = dz.astype(BF16)
        db_ref[...] += jnp.sum(dz, axis=0, keepdims=True)

    row = pl.BlockSpec((tr, LANES), lambda i: (nb - 1 - i, 0))
    return pl.pallas_call(
        body, name=name,
        out_shape=(jax.ShapeDtypeStruct((s, LANES), BF16), jax.ShapeDtypeStruct((1, LANES), F32)),
        grid=(nb,), in_specs=[row, row], out_specs=(row, pl.BlockSpec((1, LANES), lambda i: (0, 0))),
        scratch_shapes=[pltpu.VMEM((1, LANES), F32)],
        compiler_params=_params("arbitrary"),
    )(d_f, zb)


def _pairs(nblk, by_kv):
    if by_kv:
        pr = [(i, j) for j in range(nblk) for i in range(j, nblk)]
    else:
        pr = [(i, j) for i in range(nblk) for j in range(i + 1)]
    return (jnp.asarray(np.array([p[0] for p in pr], np.int32)), jnp.asarray(np.array([p[1] for p in pr], np.int32)))


def _causal_mask(t):
    r = lax.broadcasted_iota(jnp.int32, (t, t), 0)
    c = lax.broadcasted_iota(jnp.int32, (t, t), 1)
    return c <= r


def _attn_fwd(z, f_col, f_row, n_heads, *, name, tb=512):
    s = z.shape[0]
    tb = _tile(s, tb, 128)
    nblk = s // tb
    qi, kj = _pairs(nblk, by_kv=False)
    scale = 1.0 / math.sqrt(HEAD_DIM)

    def body(qi_ref, kj_ref, q_ref, k_ref, v_ref, fq_ref, fk_ref, o_ref, lse_ref, m_sc, l_sc, acc_sc):
        p = pl.program_id(1)
        i, j = qi_ref[p], kj_ref[p]

        @pl.when(j == 0)
        def _():
            m_sc[...] = jnp.full_like(m_sc, NEG_BIG)
            l_sc[...] = jnp.zeros_like(l_sc)
            acc_sc[...] = jnp.zeros_like(acc_sc)

        def update(masked):
            sc = lax.dot_general(q_ref[...], k_ref[...], (((1,), (1,)), ((), ())), preferred_element_type=F32)
            sc = sc * scale + (fq_ref[...] - fk_ref[...])
            if masked:
                sc = jnp.where(_causal_mask(tb), sc, NEG_BIG)
            m_new = jnp.maximum(m_sc[...], jnp.max(sc, axis=-1, keepdims=True))
            alpha = jnp.exp(m_sc[...] - m_new)
            pv = jnp.exp(sc - m_new)
            l_sc[...] = alpha * l_sc[...] + jnp.sum(pv, axis=-1, keepdims=True)
            acc_sc[...] = alpha * acc_sc[...] + jnp.dot(pv.astype(BF16), v_ref[...], preferred_element_type=F32)
            m_sc[...] = m_new

        @pl.when(j < i)
        def _():
            update(False)

        @pl.when(j == i)
        def _():
            update(True)
            o_ref[...] = (acc_sc[...] / l_sc[...]).astype(BF16)
            lse_ref[...] = m_sc[...] + jnp.log(l_sc[...])

    h = n_heads
    grid_spec = pltpu.PrefetchScalarGridSpec(
        num_scalar_prefetch=2, grid=(h, int(qi.shape[0])),
        in_specs=[
            pl.BlockSpec((tb, HEAD_DIM), lambda hh, p, qi_r, kj_r: (qi_r[p], hh)),
            pl.BlockSpec((tb, HEAD_DIM), lambda hh, p, qi_r, kj_r: (kj_r[p], h + hh)),
            pl.BlockSpec((tb, HEAD_DIM), lambda hh, p, qi_r, kj_r: (kj_r[p], 2 * h + hh)),
            pl.BlockSpec((None, tb, 1), lambda hh, p, qi_r, kj_r: (hh, qi_r[p], 0)),
            pl.BlockSpec((None, 1, tb), lambda hh, p, qi_r, kj_r: (hh, 0, kj_r[p])),
        ],
        out_specs=(
            pl.BlockSpec((tb, HEAD_DIM), lambda hh, p, qi_r, kj_r: (qi_r[p], hh)),
            pl.BlockSpec((None, tb, 1), lambda hh, p, qi_r, kj_r: (hh, qi_r[p], 0)),
        ),
        scratch_shapes=[pltpu.VMEM((tb, 1), F32), pltpu.VMEM((tb, 1), F32), pltpu.VMEM((tb, HEAD_DIM), F32)],
    )
    return pl.pallas_call(
        body, name=name, grid_spec=grid_spec,
        out_shape=(jax.ShapeDtypeStruct((s, h * HEAD_DIM), BF16), jax.ShapeDtypeStruct((h, s, 1), F32)),
        compiler_params=_params("parallel", "arbitrary"),
    )(qi, kj, z, z, z, f_col, f_row)


def _attn_bwd(z, o, d_o, lse, f_col, f_row, n_heads, *, name, tb=512):
    s = z.shape[0]
    tb = _tile(s, tb, 128)
    nblk = s // tb
    qi, kj = _pairs(nblk, by_kv=True)
    scale = 1.0 / math.sqrt(HEAD_DIM)
    h = n_heads

    def body(qi_ref, kj_ref, q_ref, k_ref, v_ref, o_ref, do_ref, lse_ref, fq_ref, fk_ref,
             dq_ref, dk_ref, dv_ref, df_ref, dfq_ref, dq_sc, dk_sc, dv_sc, df_sc, dfq_sc):
        p = pl.program_id(1)
        i, j = qi_ref[p], kj_ref[p]

        @pl.when(p == 0)
        def _():
            dq_sc[...] = jnp.zeros_like(dq_sc)
            dfq_sc[...] = jnp.zeros_like(dfq_sc)

        @pl.when(i == j)
        def _():
            dk_sc[...] = jnp.zeros_like(dk_sc)
            dv_sc[...] = jnp.zeros_like(dv_sc)
            df_sc[...] = jnp.zeros_like(df_sc)

        def update(masked):
            q, k, v, do = q_ref[...], k_ref[...], v_ref[...], do_ref[...]
            delta = jnp.sum(do.astype(F32) * o_ref[...].astype(F32), axis=-1, keepdims=True)
            sc = lax.dot_general(q, k, (((1,), (1,)), ((), ())), preferred_element_type=F32)
            sc = sc * scale + (fq_ref[...] - fk_ref[...])
            pv = jnp.exp(sc - lse_ref[...])
            if masked:
                pv = jnp.where(_causal_mask(tb), pv, 0.0)
            dp = lax.dot_general(do, v, (((1,), (1,)), ((), ())), preferred_element_type=F32)
            ds = pv * (dp - delta)
            ds_b = ds.astype(BF16)
            dv_sc[...] += lax.dot_general(pv.astype(BF16), do, (((0,), (0,)), ((), ())), preferred_element_type=F32)
            dk_sc[...] += lax.dot_general(ds_b, q, (((0,), (0,)), ((), ())), preferred_element_type=F32)
            rows = pl.ds(pl.multiple_of(i * tb, tb), tb)
            dq_sc[rows, :] += jnp.dot(ds_b, k, preferred_element_type=F32)
            df_sc[...] -= jnp.sum(ds, axis=0, keepdims=True)
            dfq_sc[rows, :] += jnp.sum(ds, axis=1, keepdims=True)

        @pl.when(i > j)
        def _():
            update(False)

        @pl.when(i == j)
        def _():
            update(True)

        @pl.when(i == nblk - 1)
        def _():
            dk_ref[...] = (dk_sc[...] * scale).astype(BF16)
            dv_ref[...] = dv_sc[...].astype(BF16)
            df_ref[...] = df_sc[...]

        @pl.when(p == pl.num_programs(1) - 1)
        def _():
            dq_ref[...] = (dq_sc[...] * scale).astype(BF16)
            dfq_ref[...] = dfq_sc[...]

    qblk = lambda off: pl.BlockSpec((tb, HEAD_DIM), lambda hh, p, qi_r, kj_r: (qi_r[p], off + hh))
    kblk = lambda off: pl.BlockSpec((tb, HEAD_DIM), lambda hh, p, qi_r, kj_r: (kj_r[p], off + hh))
    qcol = pl.BlockSpec((None, tb, 1), lambda hh, p, qi_r, kj_r: (hh, qi_r[p], 0))
    krow = pl.BlockSpec((None, 1, tb), lambda hh, p, qi_r, kj_r: (hh, 0, kj_r[p]))
    grid_spec = pltpu.PrefetchScalarGridSpec(
        num_scalar_prefetch=2, grid=(h, int(qi.shape[0])),
        in_specs=[qblk(0), kblk(h), kblk(2 * h), qblk(0), qblk(0), qcol, qcol, krow],
        out_specs=(
            pl.BlockSpec((s, HEAD_DIM), lambda hh, p, qi_r, kj_r: (0, hh)),
            kblk(0), kblk(0), krow,
            pl.BlockSpec((None, s, 1), lambda hh, p, qi_r, kj_r: (hh, 0, 0)),
        ),
        scratch_shapes=[pltpu.VMEM((s, HEAD_DIM), F32), pltpu.VMEM((tb, HEAD_DIM), F32),
                        pltpu.VMEM((tb, HEAD_DIM), F32), pltpu.VMEM((1, tb), F32), pltpu.VMEM((s, 1), F32)],
    )
    act = jax.ShapeDtypeStruct((s, h * HEAD_DIM), BF16)
    return pl.pallas_call(
        body, name=name, grid_spec=grid_spec,
        out_shape=(act, act, act, jax.ShapeDtypeStruct((h, 1, s), F32), jax.ShapeDtypeStruct((h, s, 1), F32)),
        compiler_params=_params("parallel", "arbitrary"),
    )(qi, kj, z, z, z, o, d_o, lse, f_col, f_row)


GELU_C = math.sqrt(2.0 / math.pi)
GELU_A = 0.044715


def _gelu(x):
    return 0.5 * x * (1.0 + jnp.tanh(GELU_C * (x + GELU_A * (x * x * x))))


def _gelu_and_grad(x):
    t = jnp.tanh(GELU_C * (x + GELU_A * (x * x * x)))
    y = 0.5 * x * (1.0 + t)
    dy = 0.5 * (1.0 + t) + 0.5 * x * (1.0 - t * t) * (GELU_C * (1.0 + 3.0 * GELU_A * (x * x)))
    return y, dy


def _layernorm_parts(g):
    mu = jnp.mean(g, axis=-1, keepdims=True)
    xc = g - mu
    rs = lax.rsqrt(jnp.mean(xc * xc, axis=-1, keepdims=True) + EPS)
    return xc * rs, rs


def _spatial_mix(w_ref, bcol_ref, vv_b, n_heads, n_chunks):
    tril = _causal_mask(CHUNK)
    cols = []
    for hh in range(n_heads):
        wc = jnp.where(tril, w_ref[hh], 0.0).astype(BF16)
        lanes = slice(hh * HEAD_DIM, (hh + 1) * HEAD_DIM)
        rows = [jnp.dot(wc, vv_b[c * CHUNK:(c + 1) * CHUNK, lanes], preferred_element_type=F32)
                + bcol_ref[:, hh:hh + 1] for c in range(n_chunks)]
        cols.append(jnp.concatenate(rows, axis=0))
    return jnp.concatenate(cols, axis=1)


def _mix_fwd(z, o, ln_g, ln_b, w_s, b_col, attn_g, gm_g, n_heads, *, name, tr=256):
    s = z.shape[0]
    dg = n_heads * HEAD_DIM
    tr = _tile(s, tr, CHUNK)
    n_chunks = tr // CHUNK

    def body(zu_ref, zv_ref, o_ref, lg_ref, lb_ref, w_ref, bcol_ref, ag_ref, gg_ref, out_ref):
        u = _gelu(zu_ref[...].astype(F32))
        xhat, _ = _layernorm_parts(_gelu(zv_ref[...].astype(F32)))
        vv = xhat * lg_ref[...] + lb_ref[...]
        gm = u * _spatial_mix(w_ref, bcol_ref, vv.astype(BF16), n_heads, n_chunks)
        rg = lax.rsqrt(jnp.mean(gm * gm, axis=-1, keepdims=True) + EPS)
        ov = o_ref[...].astype(F32)
        ra = lax.rsqrt(jnp.mean(ov * ov, axis=-1, keepdims=True) + EPS)
        out_ref[:, :dg] = ((ov * ra) * ag_ref[...]).astype(BF16)
        out_ref[:, dg:] = ((gm * rg) * gg_ref[...]).astype(BF16)

    vec = pl.BlockSpec((1, dg), lambda i: (0, 0))
    return pl.pallas_call(
        body, name=name, out_shape=jax.ShapeDtypeStruct((s, 2 * dg), BF16), grid=(s // tr,),
        in_specs=[pl.BlockSpec((tr, dg), lambda i: (i, 3)), pl.BlockSpec((tr, dg), lambda i: (i, 4)),
                  pl.BlockSpec((tr, dg), lambda i: (i, 0)), vec, vec,
                  pl.BlockSpec((n_heads, CHUNK, CHUNK), lambda i: (0, 0, 0)),
                  pl.BlockSpec((CHUNK, n_heads), lambda i: (0, 0)), vec, vec],
        out_specs=pl.BlockSpec((tr, 2 * dg), lambda i: (i, 0)),
        compiler_params=_params("parallel"),
    )(z, z, o, ln_g, ln_b, w_s, b_col, attn_g, gm_g)


def _mix_bwd(z, o, d_merged, ln_g, ln_b, w_s, b_col, attn_g, gm_g, n_heads, *, name, tr=256):
    s = z.shape[0]
    dg = n_heads * HEAD_DIM
    tr = _tile(s, tr, CHUNK)
    n_chunks = tr // CHUNK

    def body(zu_ref, zv_ref, o_ref, dm_ref, lg_ref, lb_ref, w_ref, bcol_ref, ag_ref, gg_ref,
             do_ref, dzu_ref, dzv_ref, dw_ref, dbcol_ref, dlg_ref, dlb_ref, dag_ref, dgg_ref):
        @pl.when(pl.program_id(0) == 0)
        def _():
            for ref in (dw_ref, dbcol_ref, dlg_ref, dlb_ref, dag_ref, dgg_ref):
                ref[...] = jnp.zeros_like(ref)

        d_o, dag_rows = _rms_bwd_rows(dm_ref[:, :dg], o_ref[...].astype(F32), ag_ref[...])
        do_ref[...] = d_o.astype(BF16)
        dag_ref[...] += jnp.sum(dag_rows, axis=0, keepdims=True)

        u, du_dz = _gelu_and_grad(zu_ref[...].astype(F32))
        gv, dgv_dz = _gelu_and_grad(zv_ref[...].astype(F32))
        xhat, rs = _layernorm_parts(gv)
        lg = lg_ref[...]
        vv_b = (xhat * lg + lb_ref[...]).astype(BF16)
        mix = _spatial_mix(w_ref, bcol_ref, vv_b, n_heads, n_chunks)
        gm = u * mix
        d_gm, dgg_rows = _rms_bwd_rows(dm_ref[:, dg:], gm, gg_ref[...])
        dgg_ref[...] += jnp.sum(dgg_rows, axis=0, keepdims=True)
        dzu_ref[...] = ((d_gm * mix) * du_dz).astype(BF16)
        d_mix = d_gm * u
        d_mix_b = d_mix.astype(BF16)

        tril = _causal_mask(CHUNK)
        lane = lax.broadcasted_iota(jnp.int32, (CHUNK, n_heads), 1)
        cols = []
        db = jnp.zeros((CHUNK, n_heads), F32)
        for hh in range(n_heads):
            wc = jnp.where(tril, w_ref[hh], 0.0).astype(BF16)
            lanes = slice(hh * HEAD_DIM, (hh + 1) * HEAD_DIM)
            dw = jnp.zeros((CHUNK, CHUNK), F32)
            dmix_sum = jnp.zeros((CHUNK, HEAD_DIM), F32)
            rows = []
            for c in range(n_chunks):
                rws = slice(c * CHUNK, (c + 1) * CHUNK)
                dmb = d_mix_b[rws, lanes]
                dw += lax.dot_general(dmb, vv_b[rws, lanes], (((1,), (1,)), ((), ())), preferred_element_type=F32)
                rows.append(lax.dot_general(wc, dmb, (((0,), (0,)), ((), ())), preferred_element_type=F32))
                dmix_sum += d_mix[rws, lanes]
            dw_ref[hh] += jnp.where(tril, dw, 0.0)
            db += jnp.where(lane == hh, jnp.sum(dmix_sum, axis=-1, keepdims=True), 0.0)
            cols.append(jnp.concatenate(rows, axis=0))
        dbcol_ref[...] += db
        d_vv = jnp.concatenate(cols, axis=1)

        dlg_ref[...] += jnp.sum(d_vv * xhat, axis=0, keepdims=True)
        dlb_ref[...] += jnp.sum(d_vv, axis=0, keepdims=True)
        d_xhat = d_vv * lg
        d_gv = rs * (d_xhat - jnp.mean(d_xhat, axis=-1, keepdims=True)
                     - xhat * jnp.mean(d_xhat * xhat, axis=-1, keepdims=True))
        dzv_ref[...] = (d_gv * dgv_dz).astype(BF16)

    vec = pl.BlockSpec((1, dg), lambda i: (0, 0))
    wspec = pl.BlockSpec((n_heads, CHUNK, CHUNK), lambda i: (0, 0, 0))
    bspec = pl.BlockSpec((CHUNK, n_heads), lambda i: (0, 0))
    rowb = pl.BlockSpec((tr, dg), lambda i: (i, 0))
    act = jax.ShapeDtypeStruct((s, dg), BF16)
    vshape = jax.ShapeDtypeStruct((1, dg), F32)
    return pl.pallas_call(
        body, name=name,
        out_shape=(act, act, act, jax.ShapeDtypeStruct((n_heads, CHUNK, CHUNK), F32),
                   jax.ShapeDtypeStruct((CHUNK, n_heads), F32), vshape, vshape, vshape, vshape),
        grid=(s // tr,),
        in_specs=[pl.BlockSpec((tr, dg), lambda i: (i, 3)), pl.BlockSpec((tr, dg), lambda i: (i, 4)),
                  rowb, pl.BlockSpec((tr, 2 * dg), lambda i: (i, 0)), vec, vec, wspec, bspec, vec, vec],
        out_specs=(rowb, rowb, rowb, wspec, bspec, vec, vec, vec, vec),
        compiler_params=_params("arbitrary"),
    )(z, z, o, d_merged, ln_g, ln_b, w_s, b_col, attn_g, gm_g)


def _place():
    x, y, c = lax.axis_index("x"), lax.axis_index("y"), lax.axis_index("c")
    other_chips = [(1 - x, y), (x, 1 - y), (1 - x, 1 - y)]
    return x, y, c, other_chips


def _remote(src, dst, send_sem, recv_sem, to):
    return pltpu.make_async_remote_copy(src_ref=src, dst_ref=dst, send_sem=send_sem, recv_sem=recv_sem,
                                        device_id=to, device_id_type=MESH)


def _cast_into_slot(w, place, *, name, tr=256):
    rows, cols = w.shape
    tr = _tile(rows, tr, 16)

    def body(place_ref, w_ref, o_ref):
        o_ref[...] = w_ref[...].astype(BF16)

    grid_spec = pltpu.PrefetchScalarGridSpec(
        num_scalar_prefetch=1, grid=(rows // tr,),
        in_specs=[pl.BlockSpec((tr, cols), lambda i, pr: (i, 0))],
        out_specs=pl.BlockSpec((None, tr, cols), lambda i, pr: (pr[0], i, 0)),
    )
    return pl.pallas_call(
        body, name=name, grid_spec=grid_spec, out_shape=jax.ShapeDtypeStruct((N_CHIPS, rows, cols), BF16),
        compiler_params=_params("parallel"),
    )(place, w)


def _allgather_weights(bufs, *, name):
    n = len(bufs)

    def body(*refs):
        outs = refs[n:2 * n]
        send_sems, recv_sems = refs[2 * n:]
        x, y, c, chips = _place()
        me = 2 * x + y
        sibling = (x, y, 1 - c)
        sent = []
        for t in range(n):
            half = outs[t].shape[1] // 2
            mine = outs[t].at[me, pl.ds(c * half, half), :]
            for k, (cx, cy) in enumerate(chips):
                cp = _remote(mine, mine, send_sems.at[6 * t + k], recv_sems.at[6 * t + k], (cx, cy, c))
                cp.start()
                sent.append(cp)
        for t in range(n):
            half = outs[t].shape[1] // 2
            for k, (cx, cy) in enumerate(chips):
                blk = outs[t].at[2 * cx + cy, pl.ds(c * half, half), :]
                _remote(blk, blk, send_sems.at[6 * t + k], recv_sems.at[6 * t + k], sibling).wait_recv()
                fwd = _remote(blk, blk, send_sems.at[6 * t + 3 + k], recv_sems.at[6 * t + 3 + k], sibling)
                fwd.start()
                sent.append(fwd)
        for t in range(n):
            half = outs[t].shape[1] // 2
            for k, (cx, cy) in enumerate(chips):
                blk = outs[t].at[2 * cx + cy, pl.ds((1 - c) * half, half), :]
                _remote(blk, blk, send_sems.at[6 * t + 3 + k], recv_sems.at[6 * t + 3 + k], sibling).wait_recv()
        for cp in sent:
            cp.wait_send()

    return pl.pallas_call(
        body, name=name,
        out_shape=tuple(jax.ShapeDtypeStruct(a.shape, a.dtype) for a in bufs),
        in_specs=[ANY] * n, out_specs=tuple([ANY] * n), input_output_aliases={t: t for t in range(n)},
        scratch_shapes=[pltpu.SemaphoreType.DMA((6 * n,)), pltpu.SemaphoreType.DMA((6 * n,))],
    )(*bufs)


def _swap_halves(grads, *, name):
    n = len(grads)

    def body(*refs):
        ins, outs = refs[:n], refs[n:2 * n]
        send_sems, recv_sems = refs[2 * n:]
        x, y, c, _ = _place()
        copies = []
        for t in range(n):
            half = ins[t].shape[1] // 2
            cp = _remote(ins[t].at[:, pl.ds((1 - c) * half, half), :], outs[t],
                         send_sems.at[t], recv_sems.at[t], (x, y, 1 - c))
            cp.start()
            copies.append(cp)
        for cp in copies:
            cp.wait()

    return pl.pallas_call(
        body, name=name,
        out_shape=tuple(jax.ShapeDtypeStruct((a.shape[0], a.shape[1] // 2, a.shape[2]), a.dtype) for a in grads),
        in_specs=[ANY] * n, out_specs=tuple([ANY] * n),
        scratch_shapes=[pltpu.SemaphoreType.DMA((n,)), pltpu.SemaphoreType.DMA((n,))],
    )(*grads)


def _add_halves(grad, received, place, *, name, tr=256):
    ns, half, cols = received.shape
    tr = _tile(half, tr, 16)
    per = half // tr

    def body(place_ref, g_ref, r_ref, o_ref):
        o_ref[...] = (g_ref[...].astype(F32) + r_ref[...].astype(F32)).astype(BF16)

    grid_spec = pltpu.PrefetchScalarGridSpec(
        num_scalar_prefetch=1, grid=(ns, per),
        in_specs=[pl.BlockSpec((None, tr, cols), lambda s, i, pr: (s, pr[1] * per + i, 0)),
                  pl.BlockSpec((None, tr, cols), lambda s, i, pr: (s, i, 0))],
        out_specs=pl.BlockSpec((None, tr, cols), lambda s, i, pr: (s, i, 0)),
    )
    return pl.pallas_call(
        body, name=name, grid_spec=grid_spec, out_shape=jax.ShapeDtypeStruct(received.shape, BF16),
        compiler_params=_params("parallel", "parallel"),
    )(place, grad, received)


def _send_partials(parts, *, name):
    n = len(parts)

    def body(*refs):
        ins, outs = refs[:n], refs[n:2 * n]
        send_sems, recv_sems = refs[2 * n:]
        x, y, c, chips = _place()
        me = 2 * x + y
        sent = []
        for t in range(n):
            for k, (cx, cy) in enumerate(chips):
                cp = _remote(ins[t].at[2 * cx + cy], outs[t].at[me],
                             send_sems.at[3 * t + k], recv_sems.at[3 * t + k], (cx, cy, c))
                cp.start()
                sent.append(cp)
        for t in range(n):
            for k, (cx, cy) in enumerate(chips):
                slot = outs[t].at[2 * cx + cy]
                _remote(slot, slot, send_sems.at[3 * t + k], recv_sems.at[3 * t + k], (cx, cy, c)).wait_recv()
        for cp in sent:
            cp.wait_send()

    return pl.pallas_call(
        body, name=name,
        out_shape=tuple(jax.ShapeDtypeStruct(a.shape, a.dtype) for a in parts),
        in_specs=[ANY] * n, out_specs=tuple([ANY] * n),
        scratch_shapes=[pltpu.SemaphoreType.DMA((3 * n,)), pltpu.SemaphoreType.DMA((3 * n,))],
    )(*parts)


def _sum_chips(parts, slots, place, *, name, tr=256):
    ns, half, cols = slots.shape
    tr = _tile(half, tr, 16)
    per = half // tr

    def body(place_ref, p_ref, s_ref, o_ref):
        acc = p_ref[...].astype(F32)
        for k in range(ns):
            acc = acc + jnp.where(place_ref[0] == k, 0.0, s_ref[k].astype(F32))
        o_ref[...] = acc

    grid_spec = pltpu.PrefetchScalarGridSpec(
        num_scalar_prefetch=1, grid=(per,),
        in_specs=[pl.BlockSpec((None, tr, cols), lambda i, pr: (pr[0], i, 0)),
                  pl.BlockSpec((ns, tr, cols), lambda i, pr: (0, i, 0))],
        out_specs=pl.BlockSpec((tr, cols), lambda i, pr: (pr[1] * per + i, 0)),
    )
    return pl.pallas_call(
        body, name=name, grid_spec=grid_spec, out_shape=jax.ShapeDtypeStruct((2 * half, cols), F32),
        compiler_params=_params("parallel"),
    )(place, parts, slots)


def _join_halves(bufs, *, name):
    n = len(bufs)

    def body(*refs):
        outs = refs[n:2 * n]
        send_sems, recv_sems = refs[2 * n:]
        x, y, c, _ = _place()
        sent = []
        for t in range(n):
            half = outs[t].shape[0] // 2
            mine = outs[t].at[pl.ds(c * half, half), :]
            cp = _remote(mine, mine, send_sems.at[t], recv_sems.at[t], (x, y, 1 - c))
            cp.start()
            sent.append(cp)
        for t in range(n):
            half = outs[t].shape[0] // 2
            theirs = outs[t].at[pl.ds((1 - c) * half, half), :]
            _remote(theirs, theirs, send_sems.at[t], recv_sems.at[t], (x, y, 1 - c)).wait_recv()
        for cp in sent:
            cp.wait_send()

    return pl.pallas_call(
        body, name=name,
        out_shape=tuple(jax.ShapeDtypeStruct(a.shape, a.dtype) for a in bufs),
        in_specs=[ANY] * n, out_specs=tuple([ANY] * n), input_output_aliases={t: t for t in range(n)},
        scratch_shapes=[pltpu.SemaphoreType.DMA((n,)), pltpu.SemaphoreType.DMA((n,))],
    )(*bufs)


def _allgather_small(buf, *, name):
    rows = buf.shape[0]

    def body(x_ref, out_ref, send_sems, recv_sems, local_sem):
        x, y, c, chips = _place()
        sibling = (x, y, 1 - c)

        def slot(px, py, pc):
            return out_ref.at[4 * px + 2 * py + pc]

        def copy(k, block, to, src=None):
            return _remote(slot(*block) if src is None else src, slot(*block), send_sems.at[k], recv_sems.at[k], to)

        mine = pltpu.make_async_copy(x_ref, slot(x, y, c), local_sem)
        mine.start()
        first = [copy(0, (x, y, c), sibling, src=x_ref)]
        first += [copy(1 + k, (x, y, c), (*chip, c), src=x_ref) for k, chip in enumerate(chips)]
        for cp in first:
            cp.start()
        passed = [copy(4 + k, (*chip, c), sibling) for k, chip in enumerate(chips)]
        for k, chip in enumerate(chips):
            copy(1 + k, (*chip, c), (x, y, c)).wait_recv()
            passed[k].start()
        copy(0, (x, y, 1 - c), (x, y, c)).wait_recv()
        for k, chip in enumerate(chips):
            copy(4 + k, (*chip, 1 - c), (x, y, c)).wait_recv()
        for cp in first + passed:
            cp.wait_send()
        mine.wait()

    return pl.pallas_call(
        body, name=name, out_shape=jax.ShapeDtypeStruct((N_DEV, rows, LANES), buf.dtype),
        in_specs=[pl.BlockSpec(memory_space=pltpu.VMEM)], out_specs=pl.BlockSpec(memory_space=pltpu.VMEM),
        scratch_shapes=[pltpu.SemaphoreType.DMA((7,)), pltpu.SemaphoreType.DMA((7,)), pltpu.SemaphoreType.DMA],
    )(buf)


def _adamw_math(w, g, m, v):
    m = ADAM_B1 * m + (1.0 - ADAM_B1) * g
    v = ADAM_B2 * v + (1.0 - ADAM_B2) * (g * g)
    m_hat = m / (1.0 - ADAM_B1 ** ADAM_STEP)
    v_hat = v / (1.0 - ADAM_B2 ** ADAM_STEP)
    delta = -ADAM_LR * (m_hat / (jnp.sqrt(v_hat) + ADAM_EPS) + ADAM_WD * w)
    return delta, m, v


def _adamw(w, g, m, v, *, name, tr=256):
    rows, cols = w.shape
    tr = _tile(rows, tr, 8)

    def body(w_ref, g_ref, m_ref, v_ref, d_ref, mo_ref, vo_ref):
        d_ref[...], mo_ref[...], vo_ref[...] = _adamw_math(w_ref[...], g_ref[...], m_ref[...], v_ref[...])

    blk = pl.BlockSpec((tr, cols), lambda i: (i, 0))
    shape = jax.ShapeDtypeStruct((rows, cols), F32)
    return pl.pallas_call(
        body, name=name, out_shape=(shape, shape, shape), grid=(rows // tr,),
        in_specs=[blk] * 4, out_specs=(blk, blk, blk), compiler_params=_params("parallel"),
    )(w, g, m, v)


def _adamw_small(gathered, w, m, v, *, name):
    nd = gathered.shape[0]

    def body(gs_ref, w_ref, m_ref, v_ref, g_ref, d_ref, mo_ref, vo_ref):
        g = gs_ref[0]
        for k in range(1, nd):
            g = g + gs_ref[k]
        g_ref[...] = g
        d_ref[...], mo_ref[...], vo_ref[...] = _adamw_math(w_ref[...], g, m_ref[...], v_ref[...])

    shape = jax.ShapeDtypeStruct(w.shape, F32)
    return pl.pallas_call(body, name=name, out_shape=(shape, shape, shape, shape),
                          compiler_params=pltpu.CompilerParams(vmem_limit_bytes=VMEM_LIMIT_BYTES))(gathered, w, m, v)


def _pack(parts):
    flat = jnp.concatenate([p.reshape(-1).astype(F32) for p in parts])
    rows = -(-flat.shape[0] // (8 * LANES)) * 8
    return jnp.pad(flat, (0, rows * LANES - flat.shape[0])).reshape(rows, LANES)


def _unpack(buf, shapes):
    flat = buf.reshape(-1)
    out, pos = [], 0
    for shp in shapes:
        size = int(np.prod(shp))
        out.append(flat[pos:pos + size].reshape(shp))
        pos += size
    return out


def _shard_cols(g, lo, hi):
    cs = g.shape[2]
    pieces = []
    for j in range(g.shape[0]):
        a, b = max(lo, j * cs), min(hi, (j + 1) * cs)
        if a < b:
            pieces.append(g[j][:, a - j * cs:b - j * cs])
    return pieces


def _cols_from_segments(segments, lo, hi):
    pieces = []
    for first, last, src, at in segments:
        a, b = max(lo, first), min(hi, last)
        if a < b:
            pieces.append(src[:, at + a - first:at + b - first])
    return jnp.concatenate(pieces, axis=1)


def kernel(x, norm_mix_g, w_in, b_f, gmlp_ln_g, gmlp_ln_b, w_s, b_s, attn_out_g, gmlp_out_g, w_out, norm_ffn_g, w_ff1, w_ff2, norm_final_g, loss_target, m_norm_mix_g, m_w_in, m_b_f, m_gmlp_ln_g, m_gmlp_ln_b, m_w_s, m_b_s, m_attn_out_g, m_gmlp_out_g, m_w_out, m_norm_ffn_g, m_w_ff1, m_w_ff2, m_norm_final_g, v_norm_mix_g, v_w_in, v_b_f, v_gmlp_ln_g, v_gmlp_ln_b, v_w_s, v_b_s, v_attn_out_g, v_gmlp_out_g, v_w_out, v_norm_ffn_g, v_w_ff1, v_w_ff2, v_norm_final_g):
    seq, d_model = x.shape[1], x.shape[2]
    d_attn = d_model // 2
    n_heads = d_attn // HEAD_DIM
    qkv = 3 * d_attn
    shard_cols = w_in.shape[2]
    assert N_CHIPS * shard_cols == qkv + n_heads + 2 * d_attn
    xs = x.reshape(seq, d_model)
    target = loss_target.reshape(seq, d_model)

    place = jnp.stack([2 * lax.axis_index("x") + lax.axis_index("y"), lax.axis_index("c")]).astype(jnp.int32)
    names = ["w_in", "w_out", "w_ff1", "w_ff2"]
    g_in, g_out, g_ff1, g_ff2 = _allgather_weights(
        [_cast_into_slot(w[0], place, name="cast_" + nm) for w, nm in zip((w_in, w_out, w_ff1, w_ff2), names)],
        name="allgather_weights")
    n_cols = N_CHIPS * shard_cols
    w_main = jnp.concatenate(_shard_cols(g_in, 0, qkv) + _shard_cols(g_in, qkv + n_heads, n_cols), axis=1)
    w_f = jnp.pad(jnp.concatenate(_shard_cols(g_in, qkv, qkv + n_heads), axis=1), ((0, 0), (0, LANES - n_heads)))
    w_out_full = g_out.reshape(2 * d_attn, d_model)
    w_ff2_full = g_ff2.reshape(N_CHIPS * g_ff2.shape[1], d_model)
    b_f_pad = jnp.pad(b_f, ((0, 0), (0, LANES - n_heads)))
    b_col = b_s[0].T

    h = _rmsnorm_fwd(xs, norm_mix_g, name="norm_mix")
    z = _matmul(h, w_main, name="in_proj", out_dtype=BF16)
    zb, f_cum = _forget_fwd(h, w_f, b_f_pad, name="forget_fwd")
    f_heads = f_cum[:, :n_heads].T
    f_col, f_row = f_heads[:, :, None], f_heads[:, None, :]
    o, lse = _attn_fwd(z, f_col, f_row, n_heads, name="attn_fwd")
    merged = _mix_fwd(z, o, gmlp_ln_g, gmlp_ln_b, w_s[0], b_col, attn_out_g, gmlp_out_g, n_heads, name="mix_fwd")
    x1 = _matmul(merged, w_out_full, name="out_proj", out_dtype=F32, residual=xs)
    h2 = _rmsnorm_fwd(x1, norm_ffn_g, name="norm_ffn")
    a = _matmul(h2, g_ff1, name="ff1", out_dtype=BF16, relu=True, b_sharded=True)
    x2 = _matmul(a, w_ff2_full, name="ff2", out_dtype=F32, square_lhs=True, residual=x1)
    dx2, dg_final, loss = _loss_and_final_bwd(x2, target, norm_final_g.reshape(1, d_model), name="loss_head")

    da = _matmul(dx2, w_ff2_full, name="ff2_dlhs", out_dtype=BF16, trans_b=True, scale2_by=a)
    dw_ff2 = _matmul(a, dx2, name="ff2_dw", out_dtype=BF16, trans_a=True, square_lhs=True)
    dh2 = _matmul(da, g_ff1, name="ff1_dlhs", out_dtype=F32, trans_b=True, b_sharded=True)
    dw_ff1 = _matmul(h2, da, name="ff1_dw", out_dtype=BF16, trans_a=True, out_sharded=True)
    dx1, dg_ffn = _rmsnorm_bwd(dh2, x1, dx2, norm_ffn_g, name="norm_ffn_bwd")
    d_merged = _matmul(dx1, w_out_full, name="out_proj_dlhs", out_dtype=F32, trans_b=True)
    dw_out = _matmul(merged, dx1, name="out_proj_dw", out_dtype=BF16, trans_a=True)
    d_o, dzu, dzv, dw_s, db_col, dlg, dlb, dag, dgg = _mix_bwd(
        z, o, d_merged, gmlp_ln_g, gmlp_ln_b, w_s[0], b_col, attn_out_g, gmlp_out_g, n_heads, name="mix_bwd")
    dq, dk, dv, d_f_key, d_f_query = _attn_bwd(z, o, d_o, lse, f_col, f_row, n_heads, name="attn_bwd")
    d_f = d_f_key.reshape(n_heads, seq) + d_f_query.reshape(n_heads, seq)
    d_f_pad = jnp.pad(d_f.T, ((0, 0), (0, LANES - n_heads)))
    dzf, db_f = _forget_bwd(d_f_pad, zb, name="forget_bwd")
    dz = jnp.concatenate([dq, dk, dv, dzu, dzv], axis=1)
    dh_gate = _matmul(dzf, w_f, name="gate_dlhs", out_dtype=F32, trans_b=True)
    dh = _matmul(dz, w_main, name="in_proj_dlhs", out_dtype=F32, trans_b=True, residual=dh_gate, tk=2560)
    dw_main = _matmul(h, dz, name="in_proj_dw", out_dtype=BF16, trans_a=True)
    dw_f = _matmul(h, dzf, name="gate_dw", out_dtype=BF16, trans_a=True)
    grad_x, dg_mix = _rmsnorm_bwd(dh, xs, dx1, norm_mix_g, name="norm_mix_bwd")

    segments = [(0, qkv, dw_main, 0), (qkv, qkv + n_heads, dw_f, 0), (qkv + n_heads, n_cols, dw_main, qkv)]
    dw_in = jnp.stack([_cols_from_segments(segments, j * shard_cols, (j + 1) * shard_cols) for j in range(N_CHIPS)])
    local = [dw_in, dw_out.reshape(N_CHIPS, -1, d_model), dw_ff1, dw_ff2.reshape(N_CHIPS, -1, d_model)]
    from_sibling = _swap_halves(local, name="grads_swap_halves")
    pair_sums = [_add_halves(g, r, place, name="grads_pair_sum_" + nm) for g, r, nm in zip(local, from_sibling, names)]
    slots = _send_partials(pair_sums, name="grads_to_chips")
    reduced = [_sum_chips(p, s, place, name="grads_chip_sum_" + nm) for p, s, nm in zip(pair_sums, slots, names)]
    big_g = _join_halves(reduced, name="grads_join_halves")
    big = {}
    for nm, g, w, m, v in zip(names, big_g, (w_in, w_out, w_ff1, w_ff2), (m_w_in, m_w_out, m_w_ff1, m_w_ff2),
                              (v_w_in, v_w_out, v_w_ff1, v_w_ff2)):
        d, mo, vo = _adamw(w[0], g, m[0], v[0], name="adamw_" + nm)
        big[nm] = tuple(t[None] for t in (g, d, mo, vo))

    small_w = [norm_mix_g, b_f, gmlp_ln_g, gmlp_ln_b, w_s, b_s, attn_out_g, gmlp_out_g, norm_ffn_g, norm_final_g]
    small_m = [m_norm_mix_g, m_b_f, m_gmlp_ln_g, m_gmlp_ln_b, m_w_s, m_b_s, m_attn_out_g, m_gmlp_out_g, m_norm_ffn_g, m_norm_final_g]
    small_v = [v_norm_mix_g, v_b_f, v_gmlp_ln_g, v_gmlp_ln_b, v_w_s, v_b_s, v_attn_out_g, v_gmlp_out_g, v_norm_ffn_g, v_norm_final_g]
    small_g = [dg_mix, db_f[:, :n_heads], dlg, dlb, dw_s, db_col.T, dag, dgg, dg_ffn, dg_final]
    shapes = [w.shape for w in small_w]
    gathered = _allgather_small(_pack(small_g), name="allgather_small_grads")
    packed = _adamw_small(gathered, _pack(small_w), _pack(small_m), _pack(small_v), name="adamw_small")
    sg, sd, sm, sv = (_unpack(p, shapes) for p in packed)
    small_names = ["norm_mix_g", "b_f", "gmlp_ln_g", "gmlp_ln_b", "w_s", "b_s", "attn_out_g", "gmlp_out_g", "norm_ffn_g", "norm_final_g"]
    small = {nm: (sg[i], sd[i], sm[i], sv[i]) for i, nm in enumerate(small_names)}

    order = ["norm_mix_g", "w_in", "b_f", "gmlp_ln_g", "gmlp_ln_b", "w_s", "b_s", "attn_out_g", "gmlp_out_g", "w_out",
             "norm_ffn_g", "w_ff1", "w_ff2", "norm_final_g"]
    result = {**small, **big}
    total_loss = lax.psum(loss[0, 0], ("x", "y", "c"))
    outs = [total_loss, grad_x.reshape(x.shape)]
    for part in range(4):
        outs += [result[nm][part] for nm in order]
    return tuple(outs)
```

```python
import functools
import math

import numpy as np
import jax
import jax.numpy as jnp
from jax import lax
from jax.experimental import pallas as pl
from jax.experimental.pallas import tpu as pltpu

HEAD_DIM = 128
CHUNK = 128
EPS = 1e-6
LANES = 128
N_CHIPS = 4
N_DEV = 8
VMEM_LIMIT_BYTES = 56 * 1024 * 1024

ADAM_LR = 0.001
ADAM_B1 = 0.9
ADAM_B2 = 0.999
ADAM_EPS = 1e-08
ADAM_WD = 0.01
ADAM_STEP = 10

BF16 = jnp.bfloat16
F32 = jnp.float32
MESH = pl.DeviceIdType.MESH
ANY = pl.BlockSpec(memory_space=pl.ANY)
NEG_BIG = -1e30


def _params(*sem):
    return pltpu.CompilerParams(dimension_semantics=tuple(sem), vmem_limit_bytes=VMEM_LIMIT_BYTES)


def _tile(n, pref, unit):
    t = (min(pref, n) // unit) * unit
    while t >= unit:
        if n % t == 0:
            return t
        t -= unit
    return n


def _matmul(a, b, *, name, out_dtype, trans_a=False, trans_b=False, tm=1024, tn=1024, tk=2048,
            square_lhs=False, relu=False, residual=None, scale2_by=None,
            b_sharded=False, out_sharded=False):
    m, k = (a.shape[1], a.shape[0]) if trans_a else a.shape
    if b_sharded:
        if trans_b:
            n, ks = b.shape[1], b.shape[2]
            assert N_CHIPS * ks == k
        else:
            ns = b.shape[2]
            n = N_CHIPS * ns
            assert b.shape[1] == k
    else:
        n = b.shape[0] if trans_b else b.shape[1]
        assert (b.shape[1] if trans_b else b.shape[0]) == k
    tm = _tile(m, tm, 128)
    tn = _tile(n // N_CHIPS if (out_sharded or (b_sharded and not trans_b)) else n, tn, 128)
    tk = _tile(k // N_CHIPS if (b_sharded and trans_b) else k, tk, 128)
    nk = k // tk

    if trans_a:
        a_spec = pl.BlockSpec((tk, tm), lambda i, j, kk: (kk, i))
    else:
        a_spec = pl.BlockSpec((tm, tk), lambda i, j, kk: (i, kk))
    if b_sharded and trans_b:
        per = ks // tk
        assert per * tk == ks
        b_spec = pl.BlockSpec((None, tn, tk), lambda i, j, kk: (kk // per, j, kk % per))
    elif b_sharded:
        per = ns // tn
        assert per * tn == ns
        b_spec = pl.BlockSpec((None, tk, tn), lambda i, j, kk: (j // per, kk, j % per))
    elif trans_b:
        b_spec = pl.BlockSpec((tn, tk), lambda i, j, kk: (j, kk))
    else:
        b_spec = pl.BlockSpec((tk, tn), lambda i, j, kk: (kk, j))
    if out_sharded:
        ns_out = n // N_CHIPS
        per_o = ns_out // tn
        assert per_o * tn == ns_out
        out_shape = jax.ShapeDtypeStruct((N_CHIPS, m, ns_out), out_dtype)
        o_spec = pl.BlockSpec((None, tm, tn), lambda i, j, kk: (j // per_o, i, j % per_o))
    else:
        out_shape = jax.ShapeDtypeStruct((m, n), out_dtype)
        o_spec = pl.BlockSpec((tm, tn), lambda i, j, kk: (i, j))
    mn_spec = pl.BlockSpec((tm, tn), lambda i, j, kk: (i, j))

    operands, in_specs = [a, b], [a_spec, b_spec]
    if scale2_by is not None:
        operands.append(scale2_by)
        in_specs.append(mn_spec)
    if residual is not None:
        operands.append(residual)
        in_specs.append(mn_spec)
    dims = (((0 if trans_a else 1,), (1 if trans_b else 0,)), ((), ()))

    def body(*refs):
        a_ref, b_ref = refs[0], refs[1]
        pos = 2
        scale_ref = res_ref = None
        if scale2_by is not None:
            scale_ref = refs[pos]
            pos += 1
        if residual is not None:
            res_ref = refs[pos]
            pos += 1
        o_ref = refs[pos]
        kk = pl.program_id(2)

        av = a_ref[...]
        if square_lhs:
            av = av.astype(F32)
            av = av * av
        part = lax.dot_general(av.astype(BF16), b_ref[...].astype(BF16), dims, preferred_element_type=F32)

        def finish(r):
            if relu:
                r = jnp.maximum(r, 0.0)
            if scale_ref is not None:
                r = r * (2.0 * scale_ref[...].astype(F32))
            if res_ref is not None:
                r = r + res_ref[...].astype(F32)
            o_ref[...] = r.astype(out_dtype)

        if nk == 1:
            finish(part)
        else:
            acc_ref = refs[pos + 1]

            @pl.when(kk == 0)
            def _():
                acc_ref[...] = part

            @pl.when(jnp.logical_and(kk > 0, kk < nk - 1))
            def _():
                acc_ref[...] += part

            @pl.when(kk == nk - 1)
            def _():
                finish(acc_ref[...] + part)

    return pl.pallas_call(
        body, name=name, out_shape=out_shape, grid=(m // tm, n // tn, nk),
        in_specs=in_specs, out_specs=o_spec,
        scratch_shapes=[pltpu.VMEM((tm, tn), F32)] if nk > 1 else [],
        compiler_params=_params("parallel", "parallel", "arbitrary"),
    )(*operands)


def _rmsnorm_fwd(x, g, *, name, tr=512):
    s, d = x.shape
    tr = _tile(s, tr, 8)

    def body(x_ref, g_ref, o_ref):
        xv = x_ref[...]
        r = lax.rsqrt(jnp.mean(xv * xv, axis=-1, keepdims=True) + EPS)
        o_ref[...] = ((xv * r) * g_ref[...]).astype(BF16)

    return pl.pallas_call(
        body, name=name, out_shape=jax.ShapeDtypeStruct((s, d), BF16), grid=(s // tr,),
        in_specs=[pl.BlockSpec((tr, d), lambda i: (i, 0)), pl.BlockSpec((1, d), lambda i: (0, 0))],
        out_specs=pl.BlockSpec((tr, d), lambda i: (i, 0)),
        compiler_params=_params("parallel"),
    )(x, g)


def _rms_bwd_rows(dy, xv, g):
    d = xv.shape[-1]
    r = lax.rsqrt(jnp.mean(xv * xv, axis=-1, keepdims=True) + EPS)
    gdy = dy * g
    dot = jnp.sum(gdy * xv, axis=-1, keepdims=True)
    dx = gdy * r - xv * (r * r * r) * (dot / d)
    return dx, dy * (xv * r)


def _rmsnorm_bwd(dy, x, res, g, *, name, tr=256):
    s, d = x.shape
    tr = _tile(s, tr, 8)

    def body(dy_ref, x_ref, res_ref, g_ref, dx_ref, dg_ref):
        @pl.when(pl.program_id(0) == 0)
        def _():
            dg_ref[...] = jnp.zeros_like(dg_ref)

        dx, dg_rows = _rms_bwd_rows(dy_ref[...].astype(F32), x_ref[...], g_ref[...])
        dx_ref[...] = res_ref[...] + dx
        dg_ref[...] += jnp.sum(dg_rows, axis=0, keepdims=True)

    row = pl.BlockSpec((tr, d), lambda i: (i, 0))
    vec = pl.BlockSpec((1, d), lambda i: (0, 0))
    return pl.pallas_call(
        body, name=name,
        out_shape=(jax.ShapeDtypeStruct((s, d), F32), jax.ShapeDtypeStruct((1, d), F32)),
        grid=(s // tr,), in_specs=[row, row, row, vec], out_specs=(row, vec),
        compiler_params=_params("arbitrary"),
    )(dy, x, res, g)


def _loss_and_final_bwd(x2, target, g, *, name, tr=256):
    s, d = x2.shape
    tr = _tile(s, tr, 8)

    def body(x_ref, t_ref, g_ref, dx_ref, dg_ref, loss_ref):
        @pl.when(pl.program_id(0) == 0)
        def _():
            dg_ref[...] = jnp.zeros_like(dg_ref)
            loss_ref[...] = jnp.zeros_like(loss_ref)

        xv, gv = x_ref[...], g_ref[...]
        r = lax.rsqrt(jnp.mean(xv * xv, axis=-1, keepdims=True) + EPS)
        err = (xv * r) * gv - t_ref[...]
        row_loss = jnp.mean(err * err, axis=-1, keepdims=True)
        loss_ref[...] += 0.5 * jnp.sum(row_loss, axis=0, keepdims=True)
        dx, dg_rows = _rms_bwd_rows(err / d, xv, gv)
        dx_ref[...] = dx
        dg_ref[...] += jnp.sum(dg_rows, axis=0, keepdims=True)

    row = pl.BlockSpec((tr, d), lambda i: (i, 0))
    vec = pl.BlockSpec((1, d), lambda i: (0, 0))
    one = pl.BlockSpec((1, 1), lambda i: (0, 0))
    return pl.pallas_call(
        body, name=name,
        out_shape=(jax.ShapeDtypeStruct((s, d), F32), jax.ShapeDtypeStruct((1, d), F32),
                   jax.ShapeDtypeStruct((1, 1), F32)),
        grid=(s // tr,), in_specs=[row, row, vec], out_specs=(row, vec, one),
        compiler_params=_params("arbitrary"),
    )(x2, target, g)


def _tri_ones(n, lower):
    r = lax.broadcasted_iota(jnp.int32, (n, n), 0)
    c = lax.broadcasted_iota(jnp.int32, (n, n), 1)
    return jnp.where((c <= r) if lower else (c >= r), 1.0, 0.0).astype(F32)


def _forget_fwd(h, w_f, b_f, *, name, tr=256):
    s, d = h.shape
    tr = _tile(s, tr, 8)

    def body(h_ref, w_ref, b_ref, zb_ref, f_ref, carry):
        @pl.when(pl.program_id(0) == 0)
        def _():
            carry[...] = jnp.zeros_like(carry)

        zb = jnp.dot(h_ref[...], w_ref[...], preferred_element_type=F32) + b_ref[...]
        zb_ref[...] = zb
        log_f = jnp.minimum(zb, 0.0) - jnp.log(1.0 + jnp.exp(-jnp.abs(zb)))
        run = jnp.dot(_tri_ones(tr, True), log_f, preferred_element_type=F32,
                      precision=lax.Precision.HIGHEST) + carry[...]
        f_ref[...] = run
        carry[...] = run[tr - 1:tr, :]

    row = pl.BlockSpec((tr, LANES), lambda i: (i, 0))
    return pl.pallas_call(
        body, name=name,
        out_shape=(jax.ShapeDtypeStruct((s, LANES), F32), jax.ShapeDtypeStruct((s, LANES), F32)),
        grid=(s // tr,),
        in_specs=[pl.BlockSpec((tr, d), lambda i: (i, 0)), pl.BlockSpec((d, LANES), lambda i: (0, 0)),
                  pl.BlockSpec((1, LANES), lambda i: (0, 0))],
        out_specs=(row, row), scratch_shapes=[pltpu.VMEM((1, LANES), F32)],
        compiler_params=_params("arbitrary"),
    )(h, w_f, b_f)


def _forget_bwd(d_f, zb, *, name, tr=256):
    s = zb.shape[0]
    tr = _tile(s, tr, 8)
    nb = s // tr

    def body(df_ref, zb_ref, dz_ref, db_ref, carry):
        @pl.when(pl.program_id(0) == 0)
        def _():
            carry[...] = jnp.zeros_like(carry)
            db_ref[...] = jnp.zeros_like(db_ref)

        run = jnp.dot(_tri_ones(tr, False), df_ref[...], preferred_element_type=F32,
                      precision=lax.Precision.HIGHEST) + carry[...]
        carry[...] = run[0:1, :]
        dz = run / (1.0 + jnp.exp(zb_ref[...]))
        dz_ref[...] = dz.astype(BF16)
        db_ref[...] += jnp.sum(dz, axis=0, keepdims=True)

    row = pl.BlockSpec((tr, LANES), lambda i: (nb - 1 - i, 0))
    return pl.pallas_call(
        body, name=name,
        out_shape=(jax.ShapeDtypeStruct((s, LANES), BF16), jax.ShapeDtypeStruct((1, LANES), F32)),
        grid=(nb,), in_specs=[row, row], out_specs=(row, pl.BlockSpec((1, LANES), lambda i: (0, 0))),
        scratch_shapes=[pltpu.VMEM((1, LANES), F32)],
        compiler_params=_params("arbitrary"),
    )(d_f, zb)


def _pairs(nblk, by_kv):
    if by_kv:
        pr = [(i, j) for j in range(nblk) for i in range(j, nblk)]
    else:
        pr = [(i, j) for i in range(nblk) for j in range(i + 1)]
    return (jnp.asarray(np.array([p[0] for p in pr], np.int32)), jnp.asarray(np.array([p[1] for p in pr], np.int32)))


def _causal_mask(t):
    r = lax.broadcasted_iota(jnp.int32, (t, t), 0)
    c = lax.broadcasted_iota(jnp.int32, (t, t), 1)
    return c <= r


LOG2E = math.log2(math.e)
QK_TO_LOG2 = LOG2E / math.sqrt(HEAD_DIM)


def _attn_logits2(q, k, fk_row):
    sc = lax.dot_general(q, k, (((1,), (1,)), ((), ())), preferred_element_type=F32)
    return sc * QK_TO_LOG2 - fk_row * LOG2E


def _attn_fwd(z, f_row, n_heads, *, name, tb=512):
    s = z.shape[0]
    tb = _tile(s, tb, 128)
    nblk = s // tb
    rep = tb // LANES
    qi, kj = _pairs(nblk, by_kv=False)

    def body(qi_ref, kj_ref, q_ref, k_ref, v_ref, fk_ref, o_ref, lse_ref, m_sc, l_sc, acc_sc):
        p = pl.program_id(1)
        i, j = qi_ref[p], kj_ref[p]

        @pl.when(j == 0)
        def _():
            m_sc[...] = jnp.full_like(m_sc, NEG_BIG)
            l_sc[...] = jnp.zeros_like(l_sc)
            acc_sc[...] = jnp.zeros_like(acc_sc)

        def update(masked):
            s2 = _attn_logits2(q_ref[...], k_ref[...], fk_ref[...])
            if masked:
                s2 = jnp.where(_causal_mask(tb), s2, NEG_BIG)
            m_old = m_sc[...]
            m_new = jnp.maximum(m_old, jnp.max(s2, axis=-1, keepdims=True))
            alpha = jnp.exp2(m_old - m_new)
            pv = jnp.exp2(s2 - jnp.tile(m_new, (1, rep)))
            l_sc[...] = alpha * l_sc[...] + jnp.sum(pv, axis=-1, keepdims=True)
            acc_sc[...] = alpha * acc_sc[...] + jnp.dot(pv.astype(BF16), v_ref[...], preferred_element_type=F32)
            m_sc[...] = m_new

        @pl.when(j < i)
        def _():
            update(False)

        @pl.when(j == i)
        def _():
            update(True)
            o_ref[...] = (acc_sc[...] / l_sc[...]).astype(BF16)
            lse_ref[...] = m_sc[...] + jnp.log2(l_sc[...])

    h = n_heads
    grid_spec = pltpu.PrefetchScalarGridSpec(
        num_scalar_prefetch=2, grid=(h, int(qi.shape[0])),
        in_specs=[
            pl.BlockSpec((tb, HEAD_DIM), lambda hh, p, qi_r, kj_r: (qi_r[p], hh)),
            pl.BlockSpec((tb, HEAD_DIM), lambda hh, p, qi_r, kj_r: (kj_r[p], h + hh)),
            pl.BlockSpec((tb, HEAD_DIM), lambda hh, p, qi_r, kj_r: (kj_r[p], 2 * h + hh)),
            pl.BlockSpec((None, 1, tb), lambda hh, p, qi_r, kj_r: (hh, 0, kj_r[p])),
        ],
        out_specs=(
            pl.BlockSpec((tb, HEAD_DIM), lambda hh, p, qi_r, kj_r: (qi_r[p], hh)),
            pl.BlockSpec((None, tb, LANES), lambda hh, p, qi_r, kj_r: (hh, qi_r[p], 0)),
        ),
        scratch_shapes=[pltpu.VMEM((tb, LANES), F32), pltpu.VMEM((tb, LANES), F32), pltpu.VMEM((tb, HEAD_DIM), F32)],
    )
    return pl.pallas_call(
        body, name=name, grid_spec=grid_spec,
        out_shape=(jax.ShapeDtypeStruct((s, h * HEAD_DIM), BF16), jax.ShapeDtypeStruct((h, s, LANES), F32)),
        compiler_params=_params("parallel", "arbitrary"),
    )(qi, kj, z, z, z, f_row)


def _attn_bwd(z, o, d_o, lse2, f_row, n_heads, *, name, tb=512):
    s = z.shape[0]
    tb = _tile(s, tb, 128)
    nblk = s // tb
    rep = tb // LANES
    qi, kj = _pairs(nblk, by_kv=True)
    scale = 1.0 / math.sqrt(HEAD_DIM)
    h = n_heads

    def body(qi_ref, kj_ref, q_ref, k_ref, v_ref, o_ref, do_ref, lse_ref, fk_ref,
             dq_ref, dk_ref, dv_ref, df_ref, dfq_ref, dq_sc, dk_sc, dv_sc, df_sc, dfq_sc):
        p = pl.program_id(1)
        i, j = qi_ref[p], kj_ref[p]

        @pl.when(p == 0)
        def _():
            dq_sc[...] = jnp.zeros_like(dq_sc)
            dfq_sc[...] = jnp.zeros_like(dfq_sc)

        @pl.when(i == j)
        def _():
            dk_sc[...] = jnp.zeros_like(dk_sc)
            dv_sc[...] = jnp.zeros_like(dv_sc)
            df_sc[...] = jnp.zeros_like(df_sc)

        def update(masked):
            q, k, v, do = q_ref[...], k_ref[...], v_ref[...], do_ref[...]
            delta = jnp.sum(do.astype(F32) * o_ref[...].astype(F32), axis=-1, keepdims=True)
            pv = jnp.exp2(_attn_logits2(q, k, fk_ref[...]) - jnp.tile(lse_ref[...], (1, rep)))
            if masked:
                pv = jnp.where(_causal_mask(tb), pv, 0.0)
            dp = lax.dot_general(do, v, (((1,), (1,)), ((), ())), preferred_element_type=F32)
            ds = pv * (dp - delta)
            ds_b = ds.astype(BF16)
            dv_sc[...] += lax.dot_general(pv.astype(BF16), do, (((0,), (0,)), ((), ())), preferred_element_type=F32)
            dk_sc[...] += lax.dot_general(ds_b, q, (((0,), (0,)), ((), ())), preferred_element_type=F32)
            rows = pl.ds(pl.multiple_of(i * tb, tb), tb)
            dq_sc[rows, :] += jnp.dot(ds_b, k, preferred_element_type=F32)
            df_sc[...] -= jnp.sum(ds, axis=0, keepdims=True)
            dfq_sc[rows, :] += jnp.sum(ds, axis=1, keepdims=True)

        @pl.when(i > j)
        def _():
            update(False)

        @pl.when(i == j)
        def _():
            update(True)

        @pl.when(i == nblk - 1)
        def _():
            dk_ref[...] = (dk_sc[...] * scale).astype(BF16)
            dv_ref[...] = dv_sc[...].astype(BF16)
            df_ref[...] = df_sc[...]

        @pl.when(p == pl.num_programs(1) - 1)
        def _():
            dq_ref[...] = (dq_sc[...] * scale).astype(BF16)
            dfq_ref[...] = dfq_sc[...]

    qblk = lambda off: pl.BlockSpec((tb, HEAD_DIM), lambda hh, p, qi_r, kj_r: (qi_r[p], off + hh))
    kblk = lambda off: pl.BlockSpec((tb, HEAD_DIM), lambda hh, p, qi_r, kj_r: (kj_r[p], off + hh))
    qrep = pl.BlockSpec((None, tb, LANES), lambda hh, p, qi_r, kj_r: (hh, qi_r[p], 0))
    krow = pl.BlockSpec((None, 1, tb), lambda hh, p, qi_r, kj_r: (hh, 0, kj_r[p]))
    grid_spec = pltpu.PrefetchScalarGridSpec(
        num_scalar_prefetch=2, grid=(h, int(qi.shape[0])),
        in_specs=[qblk(0), kblk(h), kblk(2 * h), qblk(0), qblk(0), qrep, krow],
        out_specs=(
            pl.BlockSpec((s, HEAD_DIM), lambda hh, p, qi_r, kj_r: (0, hh)),
            kblk(0), kblk(0), krow,
            pl.BlockSpec((None, s, 1), lambda hh, p, qi_r, kj_r: (hh, 0, 0)),
        ),
        scratch_shapes=[pltpu.VMEM((s, HEAD_DIM), F32), pltpu.VMEM((tb, HEAD_DIM), F32),
                        pltpu.VMEM((tb, HEAD_DIM), F32), pltpu.VMEM((1, tb), F32), pltpu.VMEM((s, 1), F32)],
    )
    act = jax.ShapeDtypeStruct((s, h * HEAD_DIM), BF16)
    return pl.pallas_call(
        body, name=name, grid_spec=grid_spec,
        out_shape=(act, act, act, jax.ShapeDtypeStruct((h, 1, s), F32), jax.ShapeDtypeStruct((h, s, 1), F32)),
        compiler_params=_params("parallel", "arbitrary"),
    )(qi, kj, z, z, z, o, d_o, lse2, f_row)


GELU_C = math.sqrt(2.0 / math.pi)
GELU_A = 0.044715


def _gelu(x):
    return 0.5 * x * (1.0 + jnp.tanh(GELU_C * (x + GELU_A * (x * x * x))))


def _gelu_and_grad(x):
    t = jnp.tanh(GELU_C * (x + GELU_A * (x * x * x)))
    y = 0.5 * x * (1.0 + t)
    dy = 0.5 * (1.0 + t) + 0.5 * x * (1.0 - t * t) * (GELU_C * (1.0 + 3.0 * GELU_A * (x * x)))
    return y, dy


def _layernorm_parts(g):
    mu = jnp.mean(g, axis=-1, keepdims=True)
    xc = g - mu
    rs = lax.rsqrt(jnp.mean(xc * xc, axis=-1, keepdims=True) + EPS)
    return xc * rs, rs


def _spatial_mix(w_ref, bcol_ref, vv_b, n_heads, n_chunks):
    tril = _causal_mask(CHUNK)
    cols = []
    for hh in range(n_heads):
        wc = jnp.where(tril, w_ref[hh], 0.0).astype(BF16)
        lanes = slice(hh * HEAD_DIM, (hh + 1) * HEAD_DIM)
        rows = [jnp.dot(wc, vv_b[c * CHUNK:(c + 1) * CHUNK, lanes], preferred_element_type=F32)
                + bcol_ref[:, hh:hh + 1] for c in range(n_chunks)]
        cols.append(jnp.concatenate(rows, axis=0))
    return jnp.concatenate(cols, axis=1)


def _mix_fwd(z, o, ln_g, ln_b, w_s, b_col, attn_g, gm_g, n_heads, *, name, tr=256):
    s = z.shape[0]
    dg = n_heads * HEAD_DIM
    tr = _tile(s, tr, CHUNK)
    n_chunks = tr // CHUNK

    def body(zu_ref, zv_ref, o_ref, lg_ref, lb_ref, w_ref, bcol_ref, ag_ref, gg_ref, out_ref):
        u = _gelu(zu_ref[...].astype(F32))
        xhat, _ = _layernorm_parts(_gelu(zv_ref[...].astype(F32)))
        vv = xhat * lg_ref[...] + lb_ref[...]
        gm = u * _spatial_mix(w_ref, bcol_ref, vv.astype(BF16), n_heads, n_chunks)
        rg = lax.rsqrt(jnp.mean(gm * gm, axis=-1, keepdims=True) + EPS)
        ov = o_ref[...].astype(F32)
        ra = lax.rsqrt(jnp.mean(ov * ov, axis=-1, keepdims=True) + EPS)
        out_ref[:, :dg] = ((ov * ra) * ag_ref[...]).astype(BF16)
        out_ref[:, dg:] = ((gm * rg) * gg_ref[...]).astype(BF16)

    vec = pl.BlockSpec((1, dg), lambda i: (0, 0))
    return pl.pallas_call(
        body, name=name, out_shape=jax.ShapeDtypeStruct((s, 2 * dg), BF16), grid=(s // tr,),
        in_specs=[pl.BlockSpec((tr, dg), lambda i: (i, 3)), pl.BlockSpec((tr, dg), lambda i: (i, 4)),
                  pl.BlockSpec((tr, dg), lambda i: (i, 0)), vec, vec,
                  pl.BlockSpec((n_heads, CHUNK, CHUNK), lambda i: (0, 0, 0)),
                  pl.BlockSpec((CHUNK, n_heads), lambda i: (0, 0)), vec, vec],
        out_specs=pl.BlockSpec((tr, 2 * dg), lambda i: (i, 0)),
        compiler_params=_params("parallel"),
    )(z, z, o, ln_g, ln_b, w_s, b_col, attn_g, gm_g)


def _mix_bwd(z, o, d_merged, ln_g, ln_b, w_s, b_col, attn_g, gm_g, n_heads, *, name, tr=256):
    s = z.shape[0]
    dg = n_heads * HEAD_DIM
    tr = _tile(s, tr, CHUNK)
    n_chunks = tr // CHUNK

    def body(zu_ref, zv_ref, o_ref, dm_ref, lg_ref, lb_ref, w_ref, bcol_ref, ag_ref, gg_ref,
             do_ref, dzu_ref, dzv_ref, dw_ref, dbcol_ref, dlg_ref, dlb_ref, dag_ref, dgg_ref):
        @pl.when(pl.program_id(0) == 0)
        def _():
            for ref in (dw_ref, dbcol_ref, dlg_ref, dlb_ref, dag_ref, dgg_ref):
                ref[...] = jnp.zeros_like(ref)

        d_o, dag_rows = _rms_bwd_rows(dm_ref[:, :dg], o_ref[...].astype(F32), ag_ref[...])
        do_ref[...] = d_o.astype(BF16)
        dag_ref[...] += jnp.sum(dag_rows, axis=0, keepdims=True)

        u, du_dz = _gelu_and_grad(zu_ref[...].astype(F32))
        gv, dgv_dz = _gelu_and_grad(zv_ref[...].astype(F32))
        xhat, rs = _layernorm_parts(gv)
        lg = lg_ref[...]
        vv_b = (xhat * lg + lb_ref[...]).astype(BF16)
        mix = _spatial_mix(w_ref, bcol_ref, vv_b, n_heads, n_chunks)
        gm = u * mix
        d_gm, dgg_rows = _rms_bwd_rows(dm_ref[:, dg:], gm, gg_ref[...])
        dgg_ref[...] += jnp.sum(dgg_rows, axis=0, keepdims=True)
        dzu_ref[...] = ((d_gm * mix) * du_dz).astype(BF16)
        d_mix = d_gm * u
        d_mix_b = d_mix.astype(BF16)

        tril = _causal_mask(CHUNK)
        lane = lax.broadcasted_iota(jnp.int32, (CHUNK, n_heads), 1)
        cols = []
        db = jnp.zeros((CHUNK, n_heads), F32)
        for hh in range(n_heads):
            wc = jnp.where(tril, w_ref[hh], 0.0).astype(BF16)
            lanes = slice(hh * HEAD_DIM, (hh + 1) * HEAD_DIM)
            dw = jnp.zeros((CHUNK, CHUNK), F32)
            dmix_sum = jnp.zeros((CHUNK, HEAD_DIM), F32)
            rows = []
            for c in range(n_chunks):
                rws = slice(c * CHUNK, (c + 1) * CHUNK)
                dmb = d_mix_b[rws, lanes]
                dw += lax.dot_general(dmb, vv_b[rws, lanes], (((1,), (1,)), ((), ())), preferred_element_type=F32)
                rows.append(lax.dot_general(wc, dmb, (((0,), (0,)), ((), ())), preferred_element_type=F32))
                dmix_sum += d_mix[rws, lanes]
            dw_ref[hh] += jnp.where(tril, dw, 0.0)
            db += jnp.where(lane == hh, jnp.sum(dmix_sum, axis=-1, keepdims=True), 0.0)
            cols.append(jnp.concatenate(rows, axis=0))
        dbcol_ref[...] += db
        d_vv = jnp.concatenate(cols, axis=1)

        dlg_ref[...] += jnp.sum(d_vv * xhat, axis=0, keepdims=True)
        dlb_ref[...] += jnp.sum(d_vv, axis=0, keepdims=True)
        d_xhat = d_vv * lg
        d_gv = rs * (d_xhat - jnp.mean(d_xhat, axis=-1, keepdims=True)
                     - xhat * jnp.mean(d_xhat * xhat, axis=-1, keepdims=True))
        dzv_ref[...] = (d_gv * dgv_dz).astype(BF16)

    vec = pl.BlockSpec((1, dg), lambda i: (0, 0))
    wspec = pl.BlockSpec((n_heads, CHUNK, CHUNK), lambda i: (0, 0, 0))
    bspec = pl.BlockSpec((CHUNK, n_heads), lambda i: (0, 0))
    rowb = pl.BlockSpec((tr, dg), lambda i: (i, 0))
    act = jax.ShapeDtypeStruct((s, dg), BF16)
    vshape = jax.ShapeDtypeStruct((1, dg), F32)
    return pl.pallas_call(
        body, name=name,
        out_shape=(act, act, act, jax.ShapeDtypeStruct((n_heads, CHUNK, CHUNK), F32),
                   jax.ShapeDtypeStruct((CHUNK, n_heads), F32), vshape, vshape, vshape, vshape),
        grid=(s // tr,),
        in_specs=[pl.BlockSpec((tr, dg), lambda i: (i, 3)), pl.BlockSpec((tr, dg), lambda i: (i, 4)),
                  rowb, pl.BlockSpec((tr, 2 * dg), lambda i: (i, 0)), vec, vec, wspec, bspec, vec, vec],
        out_specs=(rowb, rowb, rowb, wspec, bspec, vec, vec, vec, vec),
        compiler_params=_params("arbitrary"),
    )(z, z, o, d_merged, ln_g, ln_b, w_s, b_col, attn_g, gm_g)


def _place():
    x, y, c = lax.axis_index("x"), lax.axis_index("y"), lax.axis_index("c")
    other_chips = [(1 - x, y), (x, 1 - y), (1 - x, 1 - y)]
    return x, y, c, other_chips


def _remote(src, dst, send_sem, recv_sem, to):
    return pltpu.make_async_remote_copy(src_ref=src, dst_ref=dst, send_sem=send_sem, recv_sem=recv_sem,
                                        device_id=to, device_id_type=MESH)


def _cast_into_slot(w, place, *, name, tr=256):
    rows, cols = w.shape
    tr = _tile(rows, tr, 16)

    def body(place_ref, w_ref, o_ref):
        o_ref[...] = w_ref[...].astype(BF16)

    grid_spec = pltpu.PrefetchScalarGridSpec(
        num_scalar_prefetch=1, grid=(rows // tr,),
        in_specs=[pl.BlockSpec((tr, cols), lambda i, pr: (i, 0))],
        out_specs=pl.BlockSpec((None, tr, cols), lambda i, pr: (pr[0], i, 0)),
    )
    return pl.pallas_call(
        body, name=name, grid_spec=grid_spec, out_shape=jax.ShapeDtypeStruct((N_CHIPS, rows, cols), BF16),
        compiler_params=_params("parallel"),
    )(place, w)


def _allgather_weights(bufs, *, name):
    n = len(bufs)

    def body(*refs):
        outs = refs[n:2 * n]
        send_sems, recv_sems = refs[2 * n:]
        x, y, c, chips = _place()
        me = 2 * x + y
        sibling = (x, y, 1 - c)
        sent = []
        for t in range(n):
            half = outs[t].shape[1] // 2
            mine = outs[t].at[me, pl.ds(c * half, half), :]
            for k, (cx, cy) in enumerate(chips):
                cp = _remote(mine, mine, send_sems.at[6 * t + k], recv_sems.at[6 * t + k], (cx, cy, c))
                cp.start()
                sent.append(cp)
        for t in range(n):
            half = outs[t].shape[1] // 2
            for k, (cx, cy) in enumerate(chips):
                blk = outs[t].at[2 * cx + cy, pl.ds(c * half, half), :]
                _remote(blk, blk, send_sems.at[6 * t + k], recv_sems.at[6 * t + k], sibling).wait_recv()
                fwd = _remote(blk, blk, send_sems.at[6 * t + 3 + k], recv_sems.at[6 * t + 3 + k], sibling)
                fwd.start()
                sent.append(fwd)
        for t in range(n):
            half = outs[t].shape[1] // 2
            for k, (cx, cy) in enumerate(chips):
                blk = outs[t].at[2 * cx + cy, pl.ds((1 - c) * half, half), :]
                _remote(blk, blk, send_sems.at[6 * t + 3 + k], recv_sems.at[6 * t + 3 + k], sibling).wait_recv()
        for cp in sent:
            cp.wait_send()

    return pl.pallas_call(
        body, name=name,
        out_shape=tuple(jax.ShapeDtypeStruct(a.shape, a.dtype) for a in bufs),
        in_specs=[ANY] * n, out_specs=tuple([ANY] * n), input_output_aliases={t: t for t in range(n)},
        scratch_shapes=[pltpu.SemaphoreType.DMA((6 * n,)), pltpu.SemaphoreType.DMA((6 * n,))],
    )(*bufs)


def _swap_halves(grads, *, name):
    n = len(grads)

    def body(*refs):
        ins, outs = refs[:n], refs[n:2 * n]
        send_sems, recv_sems = refs[2 * n:]
        x, y, c, _ = _place()
        copies = []
        for t in range(n):
            half = ins[t].shape[1] // 2
            cp = _remote(ins[t].at[:, pl.ds((1 - c) * half, half), :], outs[t],
                         send_sems.at[t], recv_sems.at[t], (x, y, 1 - c))
            cp.start()
            copies.append(cp)
        for cp in copies:
            cp.wait()

    return pl.pallas_call(
        body, name=name,
        out_shape=tuple(jax.ShapeDtypeStruct((a.shape[0], a.shape[1] // 2, a.shape[2]), a.dtype) for a in grads),
        in_specs=[ANY] * n, out_specs=tuple([ANY] * n),
        scratch_shapes=[pltpu.SemaphoreType.DMA((n,)), pltpu.SemaphoreType.DMA((n,))],
    )(*grads)


def _add_halves(grad, received, place, *, name, tr=256):
    ns, half, cols = received.shape
    tr = _tile(half, tr, 16)
    per = half // tr

    def body(place_ref, g_ref, r_ref, o_ref):
        o_ref[...] = (g_ref[...].astype(F32) + r_ref[...].astype(F32)).astype(BF16)

    grid_spec = pltpu.PrefetchScalarGridSpec(
        num_scalar_prefetch=1, grid=(ns, per),
        in_specs=[pl.BlockSpec((None, tr, cols), lambda s, i, pr: (s, pr[1] * per + i, 0)),
                  pl.BlockSpec((None, tr, cols), lambda s, i, pr: (s, i, 0))],
        out_specs=pl.BlockSpec((None, tr, cols), lambda s, i, pr: (s, i, 0)),
    )
    return pl.pallas_call(
        body, name=name, grid_spec=grid_spec, out_shape=jax.ShapeDtypeStruct(received.shape, BF16),
        compiler_params=_params("parallel", "parallel"),
    )(place, grad, received)


def _send_partials(parts, *, name):
    n = len(parts)

    def body(*refs):
        ins, outs = refs[:n], refs[n:2 * n]
        send_sems, recv_sems = refs[2 * n:]
        x, y, c, chips = _place()
        me = 2 * x + y
        sent = []
        for t in range(n):
            for k, (cx, cy) in enumerate(chips):
                cp = _remote(ins[t].at[2 * cx + cy], outs[t].at[me],
                             send_sems.at[3 * t + k], recv_sems.at[3 * t + k], (cx, cy, c))
                cp.start()
                sent.append(cp)
        for t in range(n):
            for k, (cx, cy) in enumerate(chips):
                slot = outs[t].at[2 * cx + cy]
                _remote(slot, slot, send_sems.at[3 * t + k], recv_sems.at[3 * t + k], (cx, cy, c)).wait_recv()
        for cp in sent:
            cp.wait_send()

    return pl.pallas_call(
        body, name=name,
        out_shape=tuple(jax.ShapeDtypeStruct(a.shape, a.dtype) for a in parts),
        in_specs=[ANY] * n, out_specs=tuple([ANY] * n),
        scratch_shapes=[pltpu.SemaphoreType.DMA((3 * n,)), pltpu.SemaphoreType.DMA((3 * n,))],
    )(*parts)


def _sum_chips(parts, slots, place, *, name, tr=256):
    ns, half, cols = slots.shape
    tr = _tile(half, tr, 16)
    per = half // tr

    def body(place_ref, p_ref, s_ref, o_ref):
        acc = p_ref[...].astype(F32)
        for k in range(ns):
            acc = acc + jnp.where(place_ref[0] == k, 0.0, s_ref[k].astype(F32))
        o_ref[...] = acc

    grid_spec = pltpu.PrefetchScalarGridSpec(
        num_scalar_prefetch=1, grid=(per,),
        in_specs=[pl.BlockSpec((None, tr, cols), lambda i, pr: (pr[0], i, 0)),
                  pl.BlockSpec((ns, tr, cols), lambda i, pr: (0, i, 0))],
        out_specs=pl.BlockSpec((tr, cols), lambda i, pr: (pr[1] * per + i, 0)),
    )
    return pl.pallas_call(
        body, name=name, grid_spec=grid_spec, out_shape=jax.ShapeDtypeStruct((2 * half, cols), F32),
        compiler_params=_params("parallel"),
    )(place, parts, slots)


def _join_halves(bufs, *, name):
    n = len(bufs)

    def body(*refs):
        outs = refs[n:2 * n]
        send_sems, recv_sems = refs[2 * n:]
        x, y, c, _ = _place()
        sent = []
        for t in range(n):
            half = outs[t].shape[0] // 2
            mine = outs[t].at[pl.ds(c * half, half), :]
            cp = _remote(mine, mine, send_sems.at[t], recv_sems.at[t], (x, y, 1 - c))
            cp.start()
            sent.append(cp)
        for t in range(n):
            half = outs[t].shape[0] // 2
            theirs = outs[t].at[pl.ds((1 - c) * half, half), :]
            _remote(theirs, theirs, send_sems.at[t], recv_sems.at[t], (x, y, 1 - c)).wait_recv()
        for cp in sent:
            cp.wait_send()

    return pl.pallas_call(
        body, name=name,
        out_shape=tuple(jax.ShapeDtypeStruct(a.shape, a.dtype) for a in bufs),
        in_specs=[ANY] * n, out_specs=tuple([ANY] * n), input_output_aliases={t: t for t in range(n)},
        scratch_shapes=[pltpu.SemaphoreType.DMA((n,)), pltpu.SemaphoreType.DMA((n,))],
    )(*bufs)


def _allgather_small(buf, *, name):
    rows = buf.shape[0]

    def body(x_ref, out_ref, send_sems, recv_sems, local_sem):
        x, y, c, chips = _place()
        sibling = (x, y, 1 - c)

        def slot(px, py, pc):
            return out_ref.at[4 * px + 2 * py + pc]

        def copy(k, block, to, src=None):
            return _remote(slot(*block) if src is None else src, slot(*block), send_sems.at[k], recv_sems.at[k], to)

        mine = pltpu.make_async_copy(x_ref, slot(x, y, c), local_sem)
        mine.start()
        first = [copy(0, (x, y, c), sibling, src=x_ref)]
        first += [copy(1 + k, (x, y, c), (*chip, c), src=x_ref) for k, chip in enumerate(chips)]
        for cp in first:
            cp.start()
        passed = [copy(4 + k, (*chip, c), sibling) for k, chip in enumerate(chips)]
        for k, chip in enumerate(chips):
            copy(1 + k, (*chip, c), (x, y, c)).wait_recv()
            passed[k].start()
        copy(0, (x, y, 1 - c), (x, y, c)).wait_recv()
        for k, chip in enumerate(chips):
            copy(4 + k, (*chip, 1 - c), (x, y, c)).wait_recv()
        for cp in first + passed:
            cp.wait_send()
        mine.wait()

    return pl.pallas_call(
        body, name=name, out_shape=jax.ShapeDtypeStruct((N_DEV, rows, LANES), buf.dtype),
        in_specs=[pl.BlockSpec(memory_space=pltpu.VMEM)], out_specs=pl.BlockSpec(memory_space=pltpu.VMEM),
        scratch_shapes=[pltpu.SemaphoreType.DMA((7,)), pltpu.SemaphoreType.DMA((7,)), pltpu.SemaphoreType.DMA],
    )(buf)


def _adamw_math(w, g, m, v):
    m = ADAM_B1 * m + (1.0 - ADAM_B1) * g
    v = ADAM_B2 * v + (1.0 - ADAM_B2) * (g * g)
    m_hat = m / (1.0 - ADAM_B1 ** ADAM_STEP)
    v_hat = v / (1.0 - ADAM_B2 ** ADAM_STEP)
    delta = -ADAM_LR * (m_hat / (jnp.sqrt(v_hat) + ADAM_EPS) + ADAM_WD * w)
    return delta, m, v


def _adamw(w, g, m, v, *, name, tr=256):
    rows, cols = w.shape
    tr = _tile(rows, tr, 8)

    def body(w_ref, g_ref, m_ref, v_ref, d_ref, mo_ref, vo_ref):
        d_ref[...], mo_ref[...], vo_ref[...] = _adamw_math(w_ref[...], g_ref[...], m_ref[...], v_ref[...])

    blk = pl.BlockSpec((tr, cols), lambda i: (i, 0))
    shape = jax.ShapeDtypeStruct((rows, cols), F32)
    return pl.pallas_call(
        body, name=name, out_shape=(shape, shape, shape), grid=(rows // tr,),
        in_specs=[blk] * 4, out_specs=(blk, blk, blk), compiler_params=_params("parallel"),
    )(w, g, m, v)


def _adamw_small(gathered, w, m, v, *, name):
    nd = gathered.shape[0]

    def body(gs_ref, w_ref, m_ref, v_ref, g_ref, d_ref, mo_ref, vo_ref):
        g = gs_ref[0]
        for k in range(1, nd):
            g = g + gs_ref[k]
        g_ref[...] = g
        d_ref[...], mo_ref[...], vo_ref[...] = _adamw_math(w_ref[...], g, m_ref[...], v_ref[...])

    shape = jax.ShapeDtypeStruct(w.shape, F32)
    return pl.pallas_call(body, name=name, out_shape=(shape, shape, shape, shape),
                          compiler_params=pltpu.CompilerParams(vmem_limit_bytes=VMEM_LIMIT_BYTES))(gathered, w, m, v)


def _pack(parts):
    flat = jnp.concatenate([p.reshape(-1).astype(F32) for p in parts])
    rows = -(-flat.shape[0] // (8 * LANES)) * 8
    return jnp.pad(flat, (0, rows * LANES - flat.shape[0])).reshape(rows, LANES)


def _unpack(buf, shapes):
    flat = buf.reshape(-1)
    out, pos = [], 0
    for shp in shapes:
        size = int(np.prod(shp))
        out.append(flat[pos:pos + size].reshape(shp))
        pos += size
    return out


def _shard_cols(g, lo, hi):
    cs = g.shape[2]
    pieces = []
    for j in range(g.shape[0]):
        a, b = max(lo, j * cs), min(hi, (j + 1) * cs)
        if a < b:
            pieces.append(g[j][:, a - j * cs:b - j * cs])
    return pieces


def _cols_from_segments(segments, lo, hi):
    pieces = []
    for first, last, src, at in segments:
        a, b = max(lo, first), min(hi, last)
        if a < b:
            pieces.append(src[:, at + a - first:at + b - first])
    return jnp.concatenate(pieces, axis=1)


def kernel(x, norm_mix_g, w_in, b_f, gmlp_ln_g, gmlp_ln_b, w_s, b_s, attn_out_g, gmlp_out_g, w_out, norm_ffn_g, w_ff1, w_ff2, norm_final_g, loss_target, m_norm_mix_g, m_w_in, m_b_f, m_gmlp_ln_g, m_gmlp_ln_b, m_w_s, m_b_s, m_attn_out_g, m_gmlp_out_g, m_w_out, m_norm_ffn_g, m_w_ff1, m_w_ff2, m_norm_final_g, v_norm_mix_g, v_w_in, v_b_f, v_gmlp_ln_g, v_gmlp_ln_b, v_w_s, v_b_s, v_attn_out_g, v_gmlp_out_g, v_w_out, v_norm_ffn_g, v_w_ff1, v_w_ff2, v_norm_final_g):
    seq, d_model = x.shape[1], x.shape[2]
    d_attn = d_model // 2
    n_heads = d_attn // HEAD_DIM
    qkv = 3 * d_attn
    shard_cols = w_in.shape[2]
    assert N_CHIPS * shard_cols == qkv + n_heads + 2 * d_attn
    xs = x.reshape(seq, d_model)
    target = loss_target.reshape(seq, d_model)

    place = jnp.stack([2 * lax.axis_index("x") + lax.axis_index("y"), lax.axis_index("c")]).astype(jnp.int32)
    names = ["w_in", "w_out", "w_ff1", "w_ff2"]
    g_in, g_out, g_ff1, g_ff2 = _allgather_weights(
        [_cast_into_slot(w[0], place, name="cast_" + nm) for w, nm in zip((w_in, w_out, w_ff1, w_ff2), names)],
        name="allgather_weights")
    n_cols = N_CHIPS * shard_cols
    w_main = jnp.concatenate(_shard_cols(g_in, 0, qkv) + _shard_cols(g_in, qkv + n_heads, n_cols), axis=1)
    w_f = jnp.pad(jnp.concatenate(_shard_cols(g_in, qkv, qkv + n_heads), axis=1), ((0, 0), (0, LANES - n_heads)))
    w_out_full = g_out.reshape(2 * d_attn, d_model)
    w_ff2_full = g_ff2.reshape(N_CHIPS * g_ff2.shape[1], d_model)
    b_f_pad = jnp.pad(b_f, ((0, 0), (0, LANES - n_heads)))
    b_col = b_s[0].T

    h = _rmsnorm_fwd(xs, norm_mix_g, name="norm_mix")
    z = _matmul(h, w_main, name="in_proj", out_dtype=BF16)
    zb, f_cum = _forget_fwd(h, w_f, b_f_pad, name="forget_fwd")
    f_heads = f_cum[:, :n_heads].T
    f_row = f_heads[:, None, :]
    o, lse2 = _attn_fwd(z, f_row, n_heads, name="attn_fwd")
    merged = _mix_fwd(z, o, gmlp_ln_g, gmlp_ln_b, w_s[0], b_col, attn_out_g, gmlp_out_g, n_heads, name="mix_fwd")
    x1 = _matmul(merged, w_out_full, name="out_proj", out_dtype=F32, residual=xs)
    h2 = _rmsnorm_fwd(x1, norm_ffn_g, name="norm_ffn")
    a = _matmul(h2, g_ff1, name="ff1", out_dtype=BF16, relu=True, b_sharded=True)
    x2 = _matmul(a, w_ff2_full, name="ff2", out_dtype=F32, square_lhs=True, residual=x1)
    dx2, dg_final, loss = _loss_and_final_bwd(x2, target, norm_final_g.reshape(1, d_model), name="loss_head")

    da = _matmul(dx2, w_ff2_full, name="ff2_dlhs", out_dtype=BF16, trans_b=True, scale2_by=a)
    dw_ff2 = _matmul(a, dx2, name="ff2_dw", out_dtype=BF16, trans_a=True, square_lhs=True)
    dh2 = _matmul(da, g_ff1, name="ff1_dlhs", out_dtype=F32, trans_b=True, b_sharded=True)
    dw_ff1 = _matmul(h2, da, name="ff1_dw", out_dtype=BF16, trans_a=True, out_sharded=True)
    dx1, dg_ffn = _rmsnorm_bwd(dh2, x1, dx2, norm_ffn_g, name="norm_ffn_bwd")
    d_merged = _matmul(dx1, w_out_full, name="out_proj_dlhs", out_dtype=F32, trans_b=True)
    dw_out = _matmul(merged, dx1, name="out_proj_dw", out_dtype=BF16, trans_a=True)
    d_o, dzu, dzv, dw_s, db_col, dlg, dlb, dag, dgg = _mix_bwd(
        z, o, d_merged, gmlp_ln_g, gmlp_ln_b, w_s[0], b_col, attn_out_g, gmlp_out_g, n_heads, name="mix_bwd")
    dq, dk, dv, d_f_key, d_f_query = _attn_bwd(z, o, d_o, lse2, f_row, n_heads, name="attn_bwd")
    d_f = d_f_key.reshape(n_heads, seq) + d_f_query.reshape(n_heads, seq)
    d_f_pad = jnp.pad(d_f.T, ((0, 0), (0, LANES - n_heads)))
    dzf, db_f = _forget_bwd(d_f_pad, zb, name="forget_bwd")
    dz = jnp.concatenate([dq, dk, dv, dzu, dzv], axis=1)
    dh_gate = _matmul(dzf, w_f, name="gate_dlhs", out_dtype=F32, trans_b=True)
    dh = _matmul(dz, w_main, name="in_proj_dlhs", out_dtype=F32, trans_b=True, residual=dh_gate, tk=2560)
    dw_main = _matmul(h, dz, name="in_proj_dw", out_dtype=BF16, trans_a=True)
    dw_f = _matmul(h, dzf, name="gate_dw", out_dtype=BF16, trans_a=True)
    grad_x, dg_mix = _rmsnorm_bwd(dh, xs, dx1, norm_mix_g, name="norm_mix_bwd")

    segments = [(0, qkv, dw_main, 0), (qkv, qkv + n_heads, dw_f, 0), (qkv + n_heads, n_cols, dw_main, qkv)]
    dw_in = jnp.stack([_cols_from_segments(segments, j * shard_cols, (j + 1) * shard_cols) for j in range(N_CHIPS)])
    local = [dw_in, dw_out.reshape(N_CHIPS, -1, d_model), dw_ff1, dw_ff2.reshape(N_CHIPS, -1, d_model)]
    from_sibling = _swap_halves(local, name="grads_swap_halves")
    pair_sums = [_add_halves(g, r, place, name="grads_pair_sum_" + nm) for g, r, nm in zip(local, from_sibling, names)]
    slots = _send_partials(pair_sums, name="grads_to_chips")
    reduced = [_sum_chips(p, s, place, name="grads_chip_sum_" + nm) for p, s, nm in zip(pair_sums, slots, names)]
    big_g = _join_halves(reduced, name="grads_join_halves")
    big = {}
    for nm, g, w, m, v in zip(names, big_g, (w_in, w_out, w_ff1, w_ff2), (m_w_in, m_w_out, m_w_ff1, m_w_ff2),
                              (v_w_in, v_w_out, v_w_ff1, v_w_ff2)):
        d, mo, vo = _adamw(w[0], g, m[0], v[0], name="adamw_" + nm)
        big[nm] = tuple(t[None] for t in (g, d, mo, vo))

    small_w = [norm_mix_g, b_f, gmlp_ln_g, gmlp_ln_b, w_s, b_s, attn_out_g, gmlp_out_g, norm_ffn_g, norm_final_g]
    small_m = [m_norm_mix_g, m_b_f, m_gmlp_ln_g, m_gmlp_ln_b, m_w_s, m_b_s, m_attn_out_g, m_gmlp_out_g, m_norm_ffn_g, m_norm_final_g]
    small_v = [v_norm_mix_g, v_b_f, v_gmlp_ln_g, v_gmlp_ln_b, v_w_s, v_b_s, v_attn_out_g, v_gmlp_out_g, v_norm_ffn_g, v_norm_final_g]
    small_g = [dg_mix, db_f[:, :n_heads], dlg, dlb, dw_s, db_col.T, dag, dgg, dg_ffn, dg_final]
    shapes = [w.shape for w in small_w]
    gathered = _allgather_small(_pack(small_g), name="allgather_small_grads")
    packed = _adamw_small(gathered, _pack(small_w), _pack(small_m), _pack(small_v), name="adamw_small")
    sg, sd, sm, sv = (_unpack(p, shapes) for p in packed)
    small_names = ["norm_mix_g", "b_f", "gmlp_ln_g", "gmlp_ln_b", "w_s", "b_s", "attn_out_g", "gmlp_out_g", "norm_ffn_g", "norm_final_g"]
    small = {nm: (sg[i], sd[i], sm[i], sv[i]) for i, nm in enumerate(small_names)}

    order = ["norm_mix_g", "w_in", "b_f", "gmlp_ln_g", "gmlp_ln_b", "w_s", "b_s", "attn_out_g", "gmlp_out_g", "w_out",
             "norm_ffn_g", "w_ff1", "w_ff2", "norm_final_g"]
    result = {**small, **big}
    total_loss = lax.psum(loss[0, 0], ("x", "y", "c"))
    outs = [total_loss, grad_x.reshape(x.shape)]
    for part in range(4):
        outs += [result[nm][part] for nm in order]
    return tuple(outs)
```

```python
import functools
import math

import numpy as np
import jax
import jax.numpy as jnp
from jax import lax
from jax.experimental import pallas as pl
from jax.experimental.pallas import tpu as pltpu

HEAD_DIM = 128
CHUNK = 128
EPS = 1e-6
LANES = 128
N_CHIPS = 4
N_DEV = 8
VMEM_LIMIT_BYTES = 56 * 1024 * 1024

ADAM_LR = 0.001
ADAM_B1 = 0.9
ADAM_B2 = 0.999
ADAM_EPS = 1e-08
ADAM_WD = 0.01
ADAM_STEP = 10

BF16 = jnp.bfloat16
F32 = jnp.float32
MESH = pl.DeviceIdType.MESH
ANY = pl.BlockSpec(memory_space=pl.ANY)
NEG_BIG = -1e30


def _params(*sem):
    return pltpu.CompilerParams(dimension_semantics=tuple(sem), vmem_limit_bytes=VMEM_LIMIT_BYTES)


def _tile(n, pref, unit):
    t = (min(pref, n) // unit) * unit
    while t >= unit:
        if n % t == 0:
            return t
        t -= unit
    return n


class _Phase:
    def __init__(self, arrays, out_shapes, in_place, n_sems, start, finish):
        self.arrays, self.out_shapes, self.in_place = list(arrays), list(out_shapes), in_place
        self.n_sems, self.start, self.finish = n_sems, start, finish

    @property
    def n_out(self):
        return len(self.arrays) if self.in_place else len(self.out_shapes)


def _run_phases(phases, steps, comm_in, comm_out, send_sems, recv_sems):
    at_in = at_out = at_sem = 0
    for ph in phases:
        for step in steps:
            getattr(ph, step)(comm_in[at_in:at_in + len(ph.arrays)], comm_out[at_out:at_out + ph.n_out],
                              lambda i, base=at_sem: send_sems.at[base + i], lambda i, base=at_sem: recv_sems.at[base + i])
        at_in, at_out, at_sem = at_in + len(ph.arrays), at_out + ph.n_out, at_sem + ph.n_sems


def _call(body, *, name, grid, in_specs, out_specs, out_shape, operands, semantics, scratch_shapes=(),
          n_prefetch=0, phases=()):
    in_specs, out_specs, out_shape = list(in_specs), list(out_specs), list(out_shape)
    scratch_shapes = list(scratch_shapes)
    n_in, n_out, n_scr = len(operands) - n_prefetch, len(out_shape), len(scratch_shapes)
    comm_in = [a for ph in phases for a in ph.arrays]
    comm_out = [jax.ShapeDtypeStruct(s.shape, s.dtype) for ph in phases
                for s in (ph.arrays if ph.in_place else ph.out_shapes)]
    aliases, at_in, at_out = {}, n_prefetch + n_in, n_out
    for ph in phases:
        if ph.in_place:
            aliases.update({at_in + r: at_out + r for r in range(len(ph.arrays))})
        at_in, at_out = at_in + len(ph.arrays), at_out + ph.n_out
    n_sems = sum(ph.n_sems for ph in phases)

    def hosted(*refs):
        pre, rest = refs[:n_prefetch], refs[n_prefetch:]
        ins, rest = rest[:n_in], rest[n_in:]
        cin, rest = rest[:len(comm_in)], rest[len(comm_in):]
        outs, rest = rest[:n_out], rest[n_out:]
        cout, rest = rest[:len(comm_out)], rest[len(comm_out):]
        scr = rest[:n_scr]
        if phases:
            send_sems, recv_sems = rest[n_scr:]
            ids = [pl.program_id(ax) for ax in range(len(grid))]
            first = functools.reduce(jnp.logical_and, [i == 0 for i in ids])
            last = functools.reduce(jnp.logical_and, [i == g - 1 for i, g in zip(ids, grid)])

            @pl.when(first)
            def _():
                _run_phases(phases, ("start",), cin, cout, send_sems, recv_sems)

        body(*pre, *ins, *outs, *scr)
        if phases:
            @pl.when(last)
            def _():
                _run_phases(phases, ("finish",), cin, cout, send_sems, recv_sems)

    all_in = in_specs + [ANY] * len(comm_in)
    all_out = out_specs + [ANY] * len(comm_out)
    all_scr = scratch_shapes + ([pltpu.SemaphoreType.DMA((n_sems,)), pltpu.SemaphoreType.DMA((n_sems,))] if phases else [])
    if phases:
        semantics = ("arbitrary",) * len(grid)
    kwargs = dict(name=name, out_shape=tuple(out_shape + comm_out), compiler_params=_params(*semantics),
                  input_output_aliases=aliases)
    if n_prefetch:
        kwargs["grid_spec"] = pltpu.PrefetchScalarGridSpec(
            num_scalar_prefetch=n_prefetch, grid=grid, in_specs=all_in, out_specs=tuple(all_out), scratch_shapes=all_scr)
    else:
        kwargs.update(grid=grid, in_specs=all_in, out_specs=tuple(all_out), scratch_shapes=all_scr)
    res = pl.pallas_call(hosted, **kwargs)(*operands, *comm_in)
    return tuple(res[:n_out]), tuple(res[n_out:])


def _only(results):
    outs, comm = results
    return outs[0] if len(outs) == 1 else outs, comm


def _matmul(a, b, *, name, out_dtype, trans_a=False, trans_b=False, tm=1024, tn=1024, tk=2048,
            square_lhs=False, relu=False, residual=None, scale2_by=None,
            b_sharded=False, out_sharded=False, phases=()):
    m, k = (a.shape[1], a.shape[0]) if trans_a else a.shape
    if b_sharded:
        if trans_b:
            n, ks = b.shape[1], b.shape[2]
            assert N_CHIPS * ks == k
        else:
            ns = b.shape[2]
            n = N_CHIPS * ns
            assert b.shape[1] == k
    else:
        n = b.shape[0] if trans_b else b.shape[1]
        assert (b.shape[1] if trans_b else b.shape[0]) == k
    tm = _tile(m, tm, 128)
    tn = _tile(n // N_CHIPS if (out_sharded or (b_sharded and not trans_b)) else n, tn, 128)
    tk = _tile(k // N_CHIPS if (b_sharded and trans_b) else k, tk, 128)
    nk = k // tk

    if trans_a:
        a_spec = pl.BlockSpec((tk, tm), lambda i, j, kk: (kk, i))
    else:
        a_spec = pl.BlockSpec((tm, tk), lambda i, j, kk: (i, kk))
    if b_sharded and trans_b:
        per = ks // tk
        assert per * tk == ks
        b_spec = pl.BlockSpec((None, tn, tk), lambda i, j, kk: (kk // per, j, kk % per))
    elif b_sharded:
        per = ns // tn
        assert per * tn == ns
        b_spec = pl.BlockSpec((None, tk, tn), lambda i, j, kk: (j // per, kk, j % per))
    elif trans_b:
        b_spec = pl.BlockSpec((tn, tk), lambda i, j, kk: (j, kk))
    else:
        b_spec = pl.BlockSpec((tk, tn), lambda i, j, kk: (kk, j))
    if out_sharded:
        ns_out = n // N_CHIPS
        per_o = ns_out // tn
        assert per_o * tn == ns_out
        out_shape = jax.ShapeDtypeStruct((N_CHIPS, m, ns_out), out_dtype)
        o_spec = pl.BlockSpec((None, tm, tn), lambda i, j, kk: (j // per_o, i, j % per_o))
    else:
        out_shape = jax.ShapeDtypeStruct((m, n), out_dtype)
        o_spec = pl.BlockSpec((tm, tn), lambda i, j, kk: (i, j))
    mn_spec = pl.BlockSpec((tm, tn), lambda i, j, kk: (i, j))

    operands, in_specs = [a, b], [a_spec, b_spec]
    if scale2_by is not None:
        operands.append(scale2_by)
        in_specs.append(mn_spec)
    if residual is not None:
        operands.append(residual)
        in_specs.append(mn_spec)
    dims = (((0 if trans_a else 1,), (1 if trans_b else 0,)), ((), ()))

    def body(*refs):
        a_ref, b_ref = refs[0], refs[1]
        pos = 2
        scale_ref = res_ref = None
        if scale2_by is not None:
            scale_ref = refs[pos]
            pos += 1
        if residual is not None:
            res_ref = refs[pos]
            pos += 1
        o_ref = refs[pos]
        kk = pl.program_id(2)

        av = a_ref[...]
        if square_lhs:
            av = av.astype(F32)
            av = av * av
        part = lax.dot_general(av.astype(BF16), b_ref[...].astype(BF16), dims, preferred_element_type=F32)

        def finish(r):
            if relu:
                r = jnp.maximum(r, 0.0)
            if scale_ref is not None:
                r = r * (2.0 * scale_ref[...].astype(F32))
            if res_ref is not None:
                r = r + res_ref[...].astype(F32)
            o_ref[...] = r.astype(out_dtype)

        if nk == 1:
            finish(part)
        else:
            acc_ref = refs[pos + 1]

            @pl.when(kk == 0)
            def _():
                acc_ref[...] = part

            @pl.when(jnp.logical_and(kk > 0, kk < nk - 1))
            def _():
                acc_ref[...] += part

            @pl.when(kk == nk - 1)
            def _():
                finish(acc_ref[...] + part)

    return _only(_call(
        body, name=name, out_shape=[out_shape], grid=(m // tm, n // tn, nk),
        in_specs=in_specs, out_specs=[o_spec], operands=operands,
        scratch_shapes=[pltpu.VMEM((tm, tn), F32)] if nk > 1 else [],
        semantics=("parallel", "parallel", "arbitrary"), phases=phases))


def _rmsnorm_fwd(x, g, *, name, tr=512):
    s, d = x.shape
    tr = _tile(s, tr, 8)

    def body(x_ref, g_ref, o_ref):
        xv = x_ref[...]
        r = lax.rsqrt(jnp.mean(xv * xv, axis=-1, keepdims=True) + EPS)
        o_ref[...] = ((xv * r) * g_ref[...]).astype(BF16)

    return pl.pallas_call(
        body, name=name, out_shape=jax.ShapeDtypeStruct((s, d), BF16), grid=(s // tr,),
        in_specs=[pl.BlockSpec((tr, d), lambda i: (i, 0)), pl.BlockSpec((1, d), lambda i: (0, 0))],
        out_specs=pl.BlockSpec((tr, d), lambda i: (i, 0)),
        compiler_params=_params("parallel"),
    )(x, g)


def _rms_bwd_rows(dy, xv, g):
    d = xv.shape[-1]
    r = lax.rsqrt(jnp.mean(xv * xv, axis=-1, keepdims=True) + EPS)
    gdy = dy * g
    dot = jnp.sum(gdy * xv, axis=-1, keepdims=True)
    dx = gdy * r - xv * (r * r * r) * (dot / d)
    return dx, dy * (xv * r)


def _rmsnorm_bwd(dy, x, res, g, *, name, tr=256, phases=()):
    s, d = x.shape
    tr = _tile(s, tr, 8)

    def body(dy_ref, x_ref, res_ref, g_ref, dx_ref, dg_ref):
        @pl.when(pl.program_id(0) == 0)
        def _():
            dg_ref[...] = jnp.zeros_like(dg_ref)

        dx, dg_rows = _rms_bwd_rows(dy_ref[...].astype(F32), x_ref[...], g_ref[...])
        dx_ref[...] = res_ref[...] + dx
        dg_ref[...] += jnp.sum(dg_rows, axis=0, keepdims=True)

    row = pl.BlockSpec((tr, d), lambda i: (i, 0))
    vec = pl.BlockSpec((1, d), lambda i: (0, 0))
    return _call(
        body, name=name, out_shape=[jax.ShapeDtypeStruct((s, d), F32), jax.ShapeDtypeStruct((1, d), F32)],
        grid=(s // tr,), in_specs=[row, row, row, vec], out_specs=[row, vec], operands=[dy, x, res, g],
        semantics=("arbitrary",), phases=phases)


def _loss_and_final_bwd(x2, target, g, *, name, tr=256):
    s, d = x2.shape
    tr = _tile(s, tr, 8)

    def body(x_ref, t_ref, g_ref, dx_ref, dg_ref, loss_ref):
        @pl.when(pl.program_id(0) == 0)
        def _():
            dg_ref[...] = jnp.zeros_like(dg_ref)
            loss_ref[...] = jnp.zeros_like(loss_ref)

        xv, gv = x_ref[...], g_ref[...]
        r = lax.rsqrt(jnp.mean(xv * xv, axis=-1, keepdims=True) + EPS)
        err = (xv * r) * gv - t_ref[...]
        row_loss = jnp.mean(err * err, axis=-1, keepdims=True)
        loss_ref[...] += 0.5 * jnp.sum(row_loss, axis=0, keepdims=True)
        dx, dg_rows = _rms_bwd_rows(err / d, xv, gv)
        dx_ref[...] = dx
        dg_ref[...] += jnp.sum(dg_rows, axis=0, keepdims=True)

    row = pl.BlockSpec((tr, d), lambda i: (i, 0))
    vec = pl.BlockSpec((1, d), lambda i: (0, 0))
    one = pl.BlockSpec((1, 1), lambda i: (0, 0))
    return pl.pallas_call(
        body, name=name,
        out_shape=(jax.ShapeDtypeStruct((s, d), F32), jax.ShapeDtypeStruct((1, d), F32),
                   jax.ShapeDtypeStruct((1, 1), F32)),
        grid=(s // tr,), in_specs=[row, row, vec], out_specs=(row, vec, one),
        compiler_params=_params("arbitrary"),
    )(x2, target, g)


def _tri_ones(n, lower):
    r = lax.broadcasted_iota(jnp.int32, (n, n), 0)
    c = lax.broadcasted_iota(jnp.int32, (n, n), 1)
    return jnp.where((c <= r) if lower else (c >= r), 1.0, 0.0).astype(F32)


def _forget_fwd(h, w_f, b_f, *, name, tr=256):
    s, d = h.shape
    tr = _tile(s, tr, 8)

    def body(h_ref, w_ref, b_ref, zb_ref, f_ref, carry):
        @pl.when(pl.program_id(0) == 0)
        def _():
            carry[...] = jnp.zeros_like(carry)

        zb = jnp.dot(h_ref[...], w_ref[...], preferred_element_type=F32) + b_ref[...]
        zb_ref[...] = zb
        log_f = jnp.minimum(zb, 0.0) - jnp.log(1.0 + jnp.exp(-jnp.abs(zb)))
        run = jnp.dot(_tri_ones(tr, True), log_f, preferred_element_type=F32,
                      precision=lax.Precision.HIGHEST) + carry[...]
        f_ref[...] = run
        carry[...] = run[tr - 1:tr, :]

    row = pl.BlockSpec((tr, LANES), lambda i: (i, 0))
    return pl.pallas_call(
        body, name=name,
        out_shape=(jax.ShapeDtypeStruct((s, LANES), F32), jax.ShapeDtypeStruct((s, LANES), F32)),
        grid=(s // tr,),
        in_specs=[pl.BlockSpec((tr, d), lambda i: (i, 0)), pl.BlockSpec((d, LANES), lambda i: (0, 0)),
                  pl.BlockSpec((1, LANES), lambda i: (0, 0))],
        out_specs=(row, row), scratch_shapes=[pltpu.VMEM((1, LANES), F32)],
        compiler_params=_params("arbitrary"),
    )(h, w_f, b_f)


def _forget_bwd(d_f, zb, *, name, tr=256):
    s = zb.shape[0]
    tr = _tile(s, tr, 8)
    nb = s // tr

    def body(df_ref, zb_ref, dz_ref, db_ref, carry):
        @pl.when(pl.program_id(0) == 0)
        def _():
            carry[...] = jnp.zeros_like(carry)
            db_ref[...] = jnp.zeros_like(db_ref)

        run = jnp.dot(_tri_ones(tr, False), df_ref[...], preferred_element_type=F32,
                      precision=lax.Precision.HIGHEST) + carry[...]
        carry[...] = run[0:1, :]
        dz = run / (1.0 + jnp.exp(zb_ref[...]))
        dz_ref[...] = dz.astype(BF16)
        db_ref[...] += jnp.sum(dz, axis=0, keepdims=True)

    row = pl.BlockSpec((tr, LANES), lambda i: (nb - 1 - i, 0))
    return pl.pallas_call(
        body, name=name,
        out_shape=(jax.ShapeDtypeStruct((s, LANES), BF16), jax.ShapeDtypeStruct((1, LANES), F32)),
        grid=(nb,), in_specs=[row, row], out_specs=(row, pl.BlockSpec((1, LANES), lambda i: (0, 0))),
        scratch_shapes=[pltpu.VMEM((1, LANES), F32)],
        compiler_params=_params("arbitrary"),
    )(d_f, zb)


def _pairs(nblk, by_kv):
    if by_kv:
        pr = [(i, j) for j in range(nblk) for i in range(j, nblk)]
    else:
        pr = [(i, j) for i in range(nblk) for j in range(i + 1)]
    return (jnp.asarray(np.array([p[0] for p in pr], np.int32)), jnp.asarray(np.array([p[1] for p in pr], np.int32)))


def _causal_mask(t):
    r = lax.broadcasted_iota(jnp.int32, (t, t), 0)
    c = lax.broadcasted_iota(jnp.int32, (t, t), 1)
    return c <= r


LOG2E = math.log2(math.e)
QK_TO_LOG2 = LOG2E / math.sqrt(HEAD_DIM)


def _attn_logits2(q, k, fk_row):
    sc = lax.dot_general(q, k, (((1,), (1,)), ((), ())), preferred_element_type=F32)
    return sc * QK_TO_LOG2 - fk_row * LOG2E


def _attn_fwd(z, f_row, n_heads, *, name, tb=512, phases=()):
    s = z.shape[0]
    tb = _tile(s, tb, 128)
    nblk = s // tb
    rep = tb // LANES
    qi, kj = _pairs(nblk, by_kv=False)

    def body(qi_ref, kj_ref, q_ref, k_ref, v_ref, fk_ref, o_ref, lse_ref, m_sc, l_sc, acc_sc):
        p = pl.program_id(1)
        i, j = qi_ref[p], kj_ref[p]

        @pl.when(j == 0)
        def _():
            m_sc[...] = jnp.full_like(m_sc, NEG_BIG)
            l_sc[...] = jnp.zeros_like(l_sc)
            acc_sc[...] = jnp.zeros_like(acc_sc)

        def update(masked):
            s2 = _attn_logits2(q_ref[...], k_ref[...], fk_ref[...])
            if masked:
                s2 = jnp.where(_causal_mask(tb), s2, NEG_BIG)
            m_old = m_sc[...]
            m_new = jnp.maximum(m_old, jnp.max(s2, axis=-1, keepdims=True))
            alpha = jnp.exp2(m_old - m_new)
            pv = jnp.exp2(s2 - jnp.tile(m_new, (1, rep)))
            l_sc[...] = alpha * l_sc[...] + jnp.sum(pv, axis=-1, keepdims=True)
            acc_sc[...] = alpha * acc_sc[...] + jnp.dot(pv.astype(BF16), v_ref[...], preferred_element_type=F32)
            m_sc[...] = m_new

        @pl.when(j < i)
        def _():
            update(False)

        @pl.when(j == i)
        def _():
            update(True)
            o_ref[...] = (acc_sc[...] / l_sc[...]).astype(BF16)
            lse_ref[...] = m_sc[...] + jnp.log2(l_sc[...])

    h = n_heads
    return _call(
        body, name=name, n_prefetch=2, grid=(h, int(qi.shape[0])),
        in_specs=[
            pl.BlockSpec((tb, HEAD_DIM), lambda hh, p, qi_r, kj_r: (qi_r[p], hh)),
            pl.BlockSpec((tb, HEAD_DIM), lambda hh, p, qi_r, kj_r: (kj_r[p], h + hh)),
            pl.BlockSpec((tb, HEAD_DIM), lambda hh, p, qi_r, kj_r: (kj_r[p], 2 * h + hh)),
            pl.BlockSpec((None, 1, tb), lambda hh, p, qi_r, kj_r: (hh, 0, kj_r[p])),
        ],
        out_specs=[
            pl.BlockSpec((tb, HEAD_DIM), lambda hh, p, qi_r, kj_r: (qi_r[p], hh)),
            pl.BlockSpec((None, tb, LANES), lambda hh, p, qi_r, kj_r: (hh, qi_r[p], 0)),
        ],
        scratch_shapes=[pltpu.VMEM((tb, LANES), F32), pltpu.VMEM((tb, LANES), F32), pltpu.VMEM((tb, HEAD_DIM), F32)],
        out_shape=[jax.ShapeDtypeStruct((s, h * HEAD_DIM), BF16), jax.ShapeDtypeStruct((h, s, LANES), F32)],
        operands=[qi, kj, z, z, z, f_row], semantics=("parallel", "arbitrary"), phases=phases)


def _attn_bwd(z, o, d_o, lse2, f_row, n_heads, *, name, tb=512, phases=()):
    s = z.shape[0]
    tb = _tile(s, tb, 128)
    nblk = s // tb
    rep = tb // LANES
    qi, kj = _pairs(nblk, by_kv=True)
    n_pairs = int(qi.shape[0])
    scale = 1.0 / math.sqrt(HEAD_DIM)
    h = n_heads

    def body(qi_ref, kj_ref, q_ref, k_ref, v_ref, o_ref, do_ref, lse_ref, fk_ref,
             dq_ref, dk_ref, dv_ref, df_ref, dfq_ref, dq_sc, dk_sc, dv_sc, df_sc, dfq_sc):
        p = pl.program_id(1)
        i, j = qi_ref[p], kj_ref[p]

        @pl.when(p == 0)
        def _():
            dq_sc[...] = jnp.zeros_like(dq_sc)
            dfq_sc[...] = jnp.zeros_like(dfq_sc)

        @pl.when(i == j)
        def _():
            dk_sc[...] = jnp.zeros_like(dk_sc)
            dv_sc[...] = jnp.zeros_like(dv_sc)
            df_sc[...] = jnp.zeros_like(df_sc)

        def update(masked):
            q, k, v, do = q_ref[...], k_ref[...], v_ref[...], do_ref[...]
            delta = jnp.sum(do.astype(F32) * o_ref[...].astype(F32), axis=-1, keepdims=True)
            pv = jnp.exp2(_attn_logits2(q, k, fk_ref[...]) - jnp.tile(lse_ref[...], (1, rep)))
            if masked:
                pv = jnp.where(_causal_mask(tb), pv, 0.0)
            dp = lax.dot_general(do, v, (((1,), (1,)), ((), ())), preferred_element_type=F32)
            ds = pv * (dp - delta)
            ds_b = ds.astype(BF16)
            dv_sc[...] += lax.dot_general(pv.astype(BF16), do, (((0,), (0,)), ((), ())), preferred_element_type=F32)
            dk_sc[...] += lax.dot_general(ds_b, q, (((0,), (0,)), ((), ())), preferred_element_type=F32)
            rows = pl.ds(pl.multiple_of(i * tb, tb), tb)
            dq_sc[rows, :] += jnp.dot(ds_b, k, preferred_element_type=F32)
            df_sc[...] -= jnp.sum(ds, axis=0, keepdims=True)
            dfq_sc[rows, :] += jnp.sum(ds, axis=1, keepdims=True)

        @pl.when(i > j)
        def _():
            update(False)

        @pl.when(i == j)
        def _():
            update(True)

        @pl.when(i == nblk - 1)
        def _():
            dk_ref[...] = (dk_sc[...] * scale).astype(BF16)
            dv_ref[...] = dv_sc[...].astype(BF16)
            df_ref[...] = df_sc[...]

        @pl.when(p == n_pairs - 1)
        def _():
            dq_ref[...] = (dq_sc[...] * scale).astype(BF16)
            dfq_ref[...] = dfq_sc[...]

    qblk = lambda off: pl.BlockSpec((tb, HEAD_DIM), lambda hh, p, qi_r, kj_r: (qi_r[p], off + hh))
    kblk = lambda off: pl.BlockSpec((tb, HEAD_DIM), lambda hh, p, qi_r, kj_r: (kj_r[p], off + hh))
    qrep = pl.BlockSpec((None, tb, LANES), lambda hh, p, qi_r, kj_r: (hh, qi_r[p], 0))
    krow = pl.BlockSpec((None, 1, tb), lambda hh, p, qi_r, kj_r: (hh, 0, kj_r[p]))
    act = jax.ShapeDtypeStruct((s, h * HEAD_DIM), BF16)
    return _call(
        body, name=name, n_prefetch=2, grid=(h, n_pairs),
        in_specs=[qblk(0), kblk(h), kblk(2 * h), qblk(0), qblk(0), qrep, krow],
        out_specs=[
            pl.BlockSpec((s, HEAD_DIM), lambda hh, p, qi_r, kj_r: (0, hh)),
            kblk(0), kblk(0), krow,
            pl.BlockSpec((None, s, 1), lambda hh, p, qi_r, kj_r: (hh, 0, 0)),
        ],
        scratch_shapes=[pltpu.VMEM((s, HEAD_DIM), F32), pltpu.VMEM((tb, HEAD_DIM), F32),
                        pltpu.VMEM((tb, HEAD_DIM), F32), pltpu.VMEM((1, tb), F32), pltpu.VMEM((s, 1), F32)],
        out_shape=[act, act, act, jax.ShapeDtypeStruct((h, 1, s), F32), jax.ShapeDtypeStruct((h, s, 1), F32)],
        operands=[qi, kj, z, z, z, o, d_o, lse2, f_row], semantics=("parallel", "arbitrary"), phases=phases)


GELU_C = math.sqrt(2.0 / math.pi)
GELU_A = 0.044715


def _gelu(x):
    return 0.5 * x * (1.0 + jnp.tanh(GELU_C * (x + GELU_A * (x * x * x))))


def _gelu_and_grad(x):
    t = jnp.tanh(GELU_C * (x + GELU_A * (x * x * x)))
    y = 0.5 * x * (1.0 + t)
    dy = 0.5 * (1.0 + t) + 0.5 * x * (1.0 - t * t) * (GELU_C * (1.0 + 3.0 * GELU_A * (x * x)))
    return y, dy


def _layernorm_parts(g):
    mu = jnp.mean(g, axis=-1, keepdims=True)
    xc = g - mu
    rs = lax.rsqrt(jnp.mean(xc * xc, axis=-1, keepdims=True) + EPS)
    return xc * rs, rs


def _spatial_mix(w_ref, bcol_ref, vv_b, n_heads, n_chunks):
    tril = _causal_mask(CHUNK)
    cols = []
    for hh in range(n_heads):
        wc = jnp.where(tril, w_ref[hh], 0.0).astype(BF16)
        lanes = slice(hh * HEAD_DIM, (hh + 1) * HEAD_DIM)
        rows = [jnp.dot(wc, vv_b[c * CHUNK:(c + 1) * CHUNK, lanes], preferred_element_type=F32)
                + bcol_ref[:, hh:hh + 1] for c in range(n_chunks)]
        cols.append(jnp.concatenate(rows, axis=0))
    return jnp.concatenate(cols, axis=1)


def _mix_fwd(z, o, ln_g, ln_b, w_s, b_col, attn_g, gm_g, n_heads, *, name, tr=256):
    s = z.shape[0]
    dg = n_heads * HEAD_DIM
    tr = _tile(s, tr, CHUNK)
    n_chunks = tr // CHUNK

    def body(zu_ref, zv_ref, o_ref, lg_ref, lb_ref, w_ref, bcol_ref, ag_ref, gg_ref, out_ref):
        u = _gelu(zu_ref[...].astype(F32))
        xhat, _ = _layernorm_parts(_gelu(zv_ref[...].astype(F32)))
        vv = xhat * lg_ref[...] + lb_ref[...]
        gm = u * _spatial_mix(w_ref, bcol_ref, vv.astype(BF16), n_heads, n_chunks)
        rg = lax.rsqrt(jnp.mean(gm * gm, axis=-1, keepdims=True) + EPS)
        ov = o_ref[...].astype(F32)
        ra = lax.rsqrt(jnp.mean(ov * ov, axis=-1, keepdims=True) + EPS)
        out_ref[:, :dg] = ((ov * ra) * ag_ref[...]).astype(BF16)
        out_ref[:, dg:] = ((gm * rg) * gg_ref[...]).astype(BF16)

    vec = pl.BlockSpec((1, dg), lambda i: (0, 0))
    return pl.pallas_call(
        body, name=name, out_shape=jax.ShapeDtypeStruct((s, 2 * dg), BF16), grid=(s // tr,),
        in_specs=[pl.BlockSpec((tr, dg), lambda i: (i, 3)), pl.BlockSpec((tr, dg), lambda i: (i, 4)),
                  pl.BlockSpec((tr, dg), lambda i: (i, 0)), vec, vec,
                  pl.BlockSpec((n_heads, CHUNK, CHUNK), lambda i: (0, 0, 0)),
                  pl.BlockSpec((CHUNK, n_heads), lambda i: (0, 0)), vec, vec],
        out_specs=pl.BlockSpec((tr, 2 * dg), lambda i: (i, 0)),
        compiler_params=_params("parallel"),
    )(z, z, o, ln_g, ln_b, w_s, b_col, attn_g, gm_g)


def _mix_bwd(z, o, d_merged, ln_g, ln_b, w_s, b_col, attn_g, gm_g, n_heads, *, name, tr=256):
    s = z.shape[0]
    dg = n_heads * HEAD_DIM
    tr = _tile(s, tr, CHUNK)
    n_chunks = tr // CHUNK

    def body(zu_ref, zv_ref, o_ref, dm_ref, lg_ref, lb_ref, w_ref, bcol_ref, ag_ref, gg_ref,
             do_ref, dzu_ref, dzv_ref, dw_ref, dbcol_ref, dlg_ref, dlb_ref, dag_ref, dgg_ref):
        @pl.when(pl.program_id(0) == 0)
        def _():
            for ref in (dw_ref, dbcol_ref, dlg_ref, dlb_ref, dag_ref, dgg_ref):
                ref[...] = jnp.zeros_like(ref)

        d_o, dag_rows = _rms_bwd_rows(dm_ref[:, :dg], o_ref[...].astype(F32), ag_ref[...])
        do_ref[...] = d_o.astype(BF16)
        dag_ref[...] += jnp.sum(dag_rows, axis=0, keepdims=True)

        u, du_dz = _gelu_and_grad(zu_ref[...].astype(F32))
        gv, dgv_dz = _gelu_and_grad(zv_ref[...].astype(F32))
        xhat, rs = _layernorm_parts(gv)
        lg = lg_ref[...]
        vv_b = (xhat * lg + lb_ref[...]).astype(BF16)
        mix = _spatial_mix(w_ref, bcol_ref, vv_b, n_heads, n_chunks)
        gm = u * mix
        d_gm, dgg_rows = _rms_bwd_rows(dm_ref[:, dg:], gm, gg_ref[...])
        dgg_ref[...] += jnp.sum(dgg_rows, axis=0, keepdims=True)
        dzu_ref[...] = ((d_gm * mix) * du_dz).astype(BF16)
        d_mix = d_gm * u
        d_mix_b = d_mix.astype(BF16)

        tril = _causal_mask(CHUNK)
        lane = lax.broadcasted_iota(jnp.int32, (CHUNK, n_heads), 1)
        cols = []
        db = jnp.zeros((CHUNK, n_heads), F32)
        for hh in range(n_heads):
            wc = jnp.where(tril, w_ref[hh], 0.0).astype(BF16)
            lanes = slice(hh * HEAD_DIM, (hh + 1) * HEAD_DIM)
            dw = jnp.zeros((CHUNK, CHUNK), F32)
            dmix_sum = jnp.zeros((CHUNK, HEAD_DIM), F32)
            rows = []
            for c in range(n_chunks):
                rws = slice(c * CHUNK, (c + 1) * CHUNK)
                dmb = d_mix_b[rws, lanes]
                dw += lax.dot_general(dmb, vv_b[rws, lanes], (((1,), (1,)), ((), ())), preferred_element_type=F32)
                rows.append(lax.dot_general(wc, dmb, (((0,), (0,)), ((), ())), preferred_element_type=F32))
                dmix_sum += d_mix[rws, lanes]
            dw_ref[hh] += jnp.where(tril, dw, 0.0)
            db += jnp.where(lane == hh, jnp.sum(dmix_sum, axis=-1, keepdims=True), 0.0)
            cols.append(jnp.concatenate(rows, axis=0))
        dbcol_ref[...] += db
        d_vv = jnp.concatenate(cols, axis=1)

        dlg_ref[...] += jnp.sum(d_vv * xhat, axis=0, keepdims=True)
        dlb_ref[...] += jnp.sum(d_vv, axis=0, keepdims=True)
        d_xhat = d_vv * lg
        d_gv = rs * (d_xhat - jnp.mean(d_xhat, axis=-1, keepdims=True)
                     - xhat * jnp.mean(d_xhat * xhat, axis=-1, keepdims=True))
        dzv_ref[...] = (d_gv * dgv_dz).astype(BF16)

    vec = pl.BlockSpec((1, dg), lambda i: (0, 0))
    wspec = pl.BlockSpec((n_heads, CHUNK, CHUNK), lambda i: (0, 0, 0))
    bspec = pl.BlockSpec((CHUNK, n_heads), lambda i: (0, 0))
    rowb = pl.BlockSpec((tr, dg), lambda i: (i, 0))
    act = jax.ShapeDtypeStruct((s, dg), BF16)
    vshape = jax.ShapeDtypeStruct((1, dg), F32)
    return pl.pallas_call(
        body, name=name,
        out_shape=(act, act, act, jax.ShapeDtypeStruct((n_heads, CHUNK, CHUNK), F32),
                   jax.ShapeDtypeStruct((CHUNK, n_heads), F32), vshape, vshape, vshape, vshape),
        grid=(s // tr,),
        in_specs=[pl.BlockSpec((tr, dg), lambda i: (i, 3)), pl.BlockSpec((tr, dg), lambda i: (i, 4)),
                  rowb, pl.BlockSpec((tr, 2 * dg), lambda i: (i, 0)), vec, vec, wspec, bspec, vec, vec],
        out_specs=(rowb, rowb, rowb, wspec, bspec, vec, vec, vec, vec),
        compiler_params=_params("arbitrary"),
    )(z, z, o, d_merged, ln_g, ln_b, w_s, b_col, attn_g, gm_g)


def _place():
    x, y, c = lax.axis_index("x"), lax.axis_index("y"), lax.axis_index("c")
    other_chips = [(1 - x, y), (x, 1 - y), (1 - x, 1 - y)]
    return x, y, c, other_chips


def _remote(src, dst, send_sem, recv_sem, to):
    return pltpu.make_async_remote_copy(src_ref=src, dst_ref=dst, send_sem=send_sem, recv_sem=recv_sem,
                                        device_id=to, device_id_type=MESH)


def _cast_into_slot(w, place, *, name, tr=256):
    rows, cols = w.shape
    tr = _tile(rows, tr, 16)

    def body(place_ref, w_ref, o_ref):
        o_ref[...] = w_ref[...].astype(BF16)

    grid_spec = pltpu.PrefetchScalarGridSpec(
        num_scalar_prefetch=1, grid=(rows // tr,),
        in_specs=[pl.BlockSpec((tr, cols), lambda i, pr: (i, 0))],
        out_specs=pl.BlockSpec((None, tr, cols), lambda i, pr: (pr[0], i, 0)),
    )
    return pl.pallas_call(
        body, name=name, grid_spec=grid_spec, out_shape=jax.ShapeDtypeStruct((N_CHIPS, rows, cols), BF16),
        compiler_params=_params("parallel"),
    )(place, w)


def _exchange(phases, *, name):
    comm_in = [a for ph in phases for a in ph.arrays]
    comm_out = [jax.ShapeDtypeStruct(s.shape, s.dtype) for ph in phases for s in (ph.arrays if ph.in_place else ph.out_shapes)]
    aliases, at_in, at_out = {}, 0, 0
    for ph in phases:
        if ph.in_place:
            aliases.update({at_in + r: at_out + r for r in range(len(ph.arrays))})
        at_in, at_out = at_in + len(ph.arrays), at_out + ph.n_out
    n_sems = sum(ph.n_sems for ph in phases)

    def body(*refs):
        cin, cout = refs[:len(comm_in)], refs[len(comm_in):len(comm_in) + len(comm_out)]
        send_sems, recv_sems = refs[len(comm_in) + len(comm_out):]
        _run_phases(phases, ("start", "finish"), cin, cout, send_sems, recv_sems)

    return pl.pallas_call(
        body, name=name, out_shape=tuple(comm_out), in_specs=[ANY] * len(comm_in), out_specs=tuple([ANY] * len(comm_out)),
        input_output_aliases=aliases,
        scratch_shapes=[pltpu.SemaphoreType.DMA((n_sems,)), pltpu.SemaphoreType.DMA((n_sems,))],
    )(*comm_in)


def _gather_over_ici(bufs):
    def copies(outs, send, recv, incoming):
        x, y, c, chips = _place()
        for t, buf in enumerate(outs):
            half = buf.shape[1] // 2
            for k, (cx, cy) in enumerate(chips):
                blk = buf.at[(2 * cx + cy) if incoming else (2 * x + y), pl.ds(c * half, half), :]
                yield _remote(blk, blk, send(3 * t + k), recv(3 * t + k), (cx, cy, c))

    def start(ins, outs, send, recv):
        for cp in copies(outs, send, recv, False):
            cp.start()

    def finish(ins, outs, send, recv):
        for cp in copies(outs, send, recv, True):
            cp.wait_recv()
        for cp in copies(outs, send, recv, False):
            cp.wait_send()

    return _Phase(bufs, [], True, 3 * len(bufs), start, finish)


def _gather_over_d2d(bufs):
    def copies(outs, send, recv, incoming):
        x, y, c, chips = _place()
        for t, buf in enumerate(outs):
            half = buf.shape[1] // 2
            for k, (cx, cy) in enumerate(chips):
                blk = buf.at[2 * cx + cy, pl.ds(((1 - c) if incoming else c) * half, half), :]
                yield _remote(blk, blk, send(3 * t + k), recv(3 * t + k), (x, y, 1 - c))

    def start(ins, outs, send, recv):
        for cp in copies(outs, send, recv, False):
            cp.start()

    def finish(ins, outs, send, recv):
        for cp in copies(outs, send, recv, True):
            cp.wait_recv()
        for cp in copies(outs, send, recv, False):
            cp.wait_send()

    return _Phase(bufs, [], True, 3 * len(bufs), start, finish)


def _then(first, second):
    n_first = first.n_sems

    def later(sem):
        return lambda i: sem(n_first + i)

    def start(ins, outs, send, recv):
        first.start(ins, outs, send, recv)
        first.finish(ins, outs, send, recv)
        second.start(ins, outs, later(send), later(recv))

    def finish(ins, outs, send, recv):
        second.finish(ins, outs, later(send), later(recv))

    return _Phase(first.arrays, [], True, n_first + second.n_sems, start, finish)


def _swap_halves(grads):
    def copies(ins, outs, send, recv):
        x, y, c, _ = _place()
        for t, g in enumerate(ins):
            half = g.shape[1] // 2
            yield _remote(g.at[:, pl.ds((1 - c) * half, half), :], outs[t], send(t), recv(t), (x, y, 1 - c))

    def start(ins, outs, send, recv):
        for cp in copies(ins, outs, send, recv):
            cp.start()

    def finish(ins, outs, send, recv):
        for cp in copies(ins, outs, send, recv):
            cp.wait()

    shapes = [jax.ShapeDtypeStruct((a.shape[0], a.shape[1] // 2, a.shape[2]), a.dtype) for a in grads]
    return _Phase(grads, shapes, False, len(grads), start, finish)


def _add_halves(grad, received, place, *, name, tr=256):
    ns, half, cols = received.shape
    tr = _tile(half, tr, 16)
    per = half // tr

    def body(place_ref, g_ref, r_ref, o_ref):
        o_ref[...] = (g_ref[...].astype(F32) + r_ref[...].astype(F32)).astype(BF16)

    grid_spec = pltpu.PrefetchScalarGridSpec(
        num_scalar_prefetch=1, grid=(ns, per),
        in_specs=[pl.BlockSpec((None, tr, cols), lambda s, i, pr: (s, pr[1] * per + i, 0)),
                  pl.BlockSpec((None, tr, cols), lambda s, i, pr: (s, i, 0))],
        out_specs=pl.BlockSpec((None, tr, cols), lambda s, i, pr: (s, i, 0)),
    )
    return pl.pallas_call(
        body, name=name, grid_spec=grid_spec, out_shape=jax.ShapeDtypeStruct(received.shape, BF16),
        compiler_params=_params("parallel", "parallel"),
    )(place, grad, received)


def _send_partials(parts):
    def start(ins, outs, send, recv):
        x, y, c, chips = _place()
        for t, part in enumerate(ins):
            for k, (cx, cy) in enumerate(chips):
                _remote(part.at[2 * cx + cy], outs[t].at[2 * x + y], send(3 * t + k), recv(3 * t + k), (cx, cy, c)).start()

    def finish(ins, outs, send, recv):
        x, y, c, chips = _place()
        for t, part in enumerate(ins):
            for k, (cx, cy) in enumerate(chips):
                slot = outs[t].at[2 * cx + cy]
                _remote(slot, slot, send(3 * t + k), recv(3 * t + k), (cx, cy, c)).wait_recv()
        for t, part in enumerate(ins):
            for k, (cx, cy) in enumerate(chips):
                sent = part.at[2 * cx + cy]
                _remote(sent, sent, send(3 * t + k), recv(3 * t + k), (cx, cy, c)).wait_send()

    shapes = [jax.ShapeDtypeStruct(a.shape, a.dtype) for a in parts]
    return _Phase(parts, shapes, False, 3 * len(parts), start, finish)


def _sum_chips(parts, slots, place, *, name, tr=256):
    ns, half, cols = slots.shape
    tr = _tile(half, tr, 16)
    per = half // tr

    def body(place_ref, p_ref, s_ref, o_ref):
        acc = p_ref[...].astype(F32)
        for k in range(ns):
            acc = acc + jnp.where(place_ref[0] == k, 0.0, s_ref[k].astype(F32))
        o_ref[...] = acc

    grid_spec = pltpu.PrefetchScalarGridSpec(
        num_scalar_prefetch=1, grid=(per,),
        in_specs=[pl.BlockSpec((None, tr, cols), lambda i, pr: (pr[0], i, 0)),
                  pl.BlockSpec((ns, tr, cols), lambda i, pr: (0, i, 0))],
        out_specs=pl.BlockSpec((tr, cols), lambda i, pr: (pr[1] * per + i, 0)),
    )
    return pl.pallas_call(
        body, name=name, grid_spec=grid_spec, out_shape=jax.ShapeDtypeStruct((2 * half, cols), F32),
        compiler_params=_params("parallel"),
    )(place, parts, slots)


def _join_halves(bufs):
    def copies(outs, send, recv, incoming):
        x, y, c, _ = _place()
        for t, buf in enumerate(outs):
            half = buf.shape[0] // 2
            rows = buf.at[pl.ds(((1 - c) if incoming else c) * half, half), :]
            yield _remote(rows, rows, send(t), recv(t), (x, y, 1 - c))

    def start(ins, outs, send, recv):
        for cp in copies(outs, send, recv, False):
            cp.start()

    def finish(ins, outs, send, recv):
        for cp in copies(outs, send, recv, True):
            cp.wait_recv()
        for cp in copies(outs, send, recv, False):
            cp.wait_send()

    return _Phase(bufs, [], True, len(bufs), start, finish)


def _allgather_small(buf, *, name):
    rows = buf.shape[0]

    def body(x_ref, out_ref, send_sems, recv_sems, local_sem):
        x, y, c, chips = _place()
        sibling = (x, y, 1 - c)

        def slot(px, py, pc):
            return out_ref.at[4 * px + 2 * py + pc]

        def copy(k, block, to, src=None):
            return _remote(slot(*block) if src is None else src, slot(*block), send_sems.at[k], recv_sems.at[k], to)

        mine = pltpu.make_async_copy(x_ref, slot(x, y, c), local_sem)
        mine.start()
        first = [copy(0, (x, y, c), sibling, src=x_ref)]
        first += [copy(1 + k, (x, y, c), (*chip, c), src=x_ref) for k, chip in enumerate(chips)]
        for cp in first:
            cp.start()
        passed = [copy(4 + k, (*chip, c), sibling) for k, chip in enumerate(chips)]
        for k, chip in enumerate(chips):
            copy(1 + k, (*chip, c), (x, y, c)).wait_recv()
            passed[k].start()
        copy(0, (x, y, 1 - c), (x, y, c)).wait_recv()
        for k, chip in enumerate(chips):
            copy(4 + k, (*chip, 1 - c), (x, y, c)).wait_recv()
        for cp in first + passed:
            cp.wait_send()
        mine.wait()

    return pl.pallas_call(
        body, name=name, out_shape=jax.ShapeDtypeStruct((N_DEV, rows, LANES), buf.dtype),
        in_specs=[pl.BlockSpec(memory_space=pltpu.VMEM)], out_specs=pl.BlockSpec(memory_space=pltpu.VMEM),
        scratch_shapes=[pltpu.SemaphoreType.DMA((7,)), pltpu.SemaphoreType.DMA((7,)), pltpu.SemaphoreType.DMA],
    )(buf)


def _adamw_math(w, g, m, v):
    m = ADAM_B1 * m + (1.0 - ADAM_B1) * g
    v = ADAM_B2 * v + (1.0 - ADAM_B2) * (g * g)
    m_hat = m / (1.0 - ADAM_B1 ** ADAM_STEP)
    v_hat = v / (1.0 - ADAM_B2 ** ADAM_STEP)
    delta = -ADAM_LR * (m_hat / (jnp.sqrt(v_hat) + ADAM_EPS) + ADAM_WD * w)
    return delta, m, v


def _adamw(w, g, m, v, *, name, tr=256):
    rows, cols = w.shape
    tr = _tile(rows, tr, 8)

    def body(w_ref, g_ref, m_ref, v_ref, d_ref, mo_ref, vo_ref):
        d_ref[...], mo_ref[...], vo_ref[...] = _adamw_math(w_ref[...], g_ref[...], m_ref[...], v_ref[...])

    blk = pl.BlockSpec((tr, cols), lambda i: (i, 0))
    shape = jax.ShapeDtypeStruct((rows, cols), F32)
    return pl.pallas_call(
        body, name=name, out_shape=(shape, shape, shape), grid=(rows // tr,),
        in_specs=[blk] * 4, out_specs=(blk, blk, blk), compiler_params=_params("parallel"),
    )(w, g, m, v)


def _adamw_small(gathered, w, m, v, *, name):
    nd = gathered.shape[0]

    def body(gs_ref, w_ref, m_ref, v_ref, g_ref, d_ref, mo_ref, vo_ref):
        g = gs_ref[0]
        for k in range(1, nd):
            g = g + gs_ref[k]
        g_ref[...] = g
        d_ref[...], mo_ref[...], vo_ref[...] = _adamw_math(w_ref[...], g, m_ref[...], v_ref[...])

    shape = jax.ShapeDtypeStruct(w.shape, F32)
    return pl.pallas_call(body, name=name, out_shape=(shape, shape, shape, shape),
                          compiler_params=pltpu.CompilerParams(vmem_limit_bytes=VMEM_LIMIT_BYTES))(gathered, w, m, v)


def _pack(parts):
    flat = jnp.concatenate([p.reshape(-1).astype(F32) for p in parts])
    rows = -(-flat.shape[0] // (8 * LANES)) * 8
    return jnp.pad(flat, (0, rows * LANES - flat.shape[0])).reshape(rows, LANES)


def _unpack(buf, shapes):
    flat = buf.reshape(-1)
    out, pos = [], 0
    for shp in shapes:
        size = int(np.prod(shp))
        out.append(flat[pos:pos + size].reshape(shp))
        pos += size
    return out


def _shard_cols(g, lo, hi):
    cs = g.shape[2]
    pieces = []
    for j in range(g.shape[0]):
        a, b = max(lo, j * cs), min(hi, (j + 1) * cs)
        if a < b:
            pieces.append(g[j][:, a - j * cs:b - j * cs])
    return pieces


def _cols_from_segments(segments, lo, hi):
    pieces = []
    for first, last, src, at in segments:
        a, b = max(lo, first), min(hi, last)
        if a < b:
            pieces.append(src[:, at + a - first:at + b - first])
    return jnp.concatenate(pieces, axis=1)


def kernel(x, norm_mix_g, w_in, b_f, gmlp_ln_g, gmlp_ln_b, w_s, b_s, attn_out_g, gmlp_out_g, w_out, norm_ffn_g, w_ff1, w_ff2, norm_final_g, loss_target, m_norm_mix_g, m_w_in, m_b_f, m_gmlp_ln_g, m_gmlp_ln_b, m_w_s, m_b_s, m_attn_out_g, m_gmlp_out_g, m_w_out, m_norm_ffn_g, m_w_ff1, m_w_ff2, m_norm_final_g, v_norm_mix_g, v_w_in, v_b_f, v_gmlp_ln_g, v_gmlp_ln_b, v_w_s, v_b_s, v_attn_out_g, v_gmlp_out_g, v_w_out, v_norm_ffn_g, v_w_ff1, v_w_ff2, v_norm_final_g):
    seq, d_model = x.shape[1], x.shape[2]
    d_attn = d_model // 2
    n_heads = d_attn // HEAD_DIM
    qkv = 3 * d_attn
    shard_cols = w_in.shape[2]
    assert N_CHIPS * shard_cols == qkv + n_heads + 2 * d_attn
    xs = x.reshape(seq, d_model)
    target = loss_target.reshape(seq, d_model)

    place = jnp.stack([2 * lax.axis_index("x") + lax.axis_index("y"), lax.axis_index("c")]).astype(jnp.int32)
    names = ["w_in", "w_out", "w_ff1", "w_ff2"]
    b_in, b_out, b_ff1, b_ff2 = [_cast_into_slot(w[0], place, name="cast_" + nm)
                                 for w, nm in zip((w_in, w_out, w_ff1, w_ff2), names)]
    (g_in,) = _exchange([_then(_gather_over_ici([b_in]), _gather_over_d2d([b_in]))], name="allgather_w_in")
    n_cols = N_CHIPS * shard_cols
    w_main = jnp.concatenate(_shard_cols(g_in, 0, qkv) + _shard_cols(g_in, qkv + n_heads, n_cols), axis=1)
    w_f = jnp.pad(jnp.concatenate(_shard_cols(g_in, qkv, qkv + n_heads), axis=1), ((0, 0), (0, LANES - n_heads)))
    b_f_pad = jnp.pad(b_f, ((0, 0), (0, LANES - n_heads)))
    b_col = b_s[0].T

    h = _rmsnorm_fwd(xs, norm_mix_g, name="norm_mix")
    z, (b_out,) = _matmul(h, w_main, name="in_proj", out_dtype=BF16, phases=[_gather_over_ici([b_out])])
    zb, f_cum = _forget_fwd(h, w_f, b_f_pad, name="forget_fwd")
    f_row = f_cum[:, :n_heads].T[:, None, :]
    (o, lse2), (b_ff1, b_out) = _attn_fwd(z, f_row, n_heads, name="attn_fwd",
                                          phases=[_gather_over_ici([b_ff1]), _gather_over_d2d([b_out])])
    merged = _mix_fwd(z, o, gmlp_ln_g, gmlp_ln_b, w_s[0], b_col, attn_out_g, gmlp_out_g, n_heads, name="mix_fwd")
    w_out_full = b_out.reshape(2 * d_attn, d_model)
    x1, (b_ff1,) = _matmul(merged, w_out_full, name="out_proj", out_dtype=F32, residual=xs,
                           phases=[_gather_over_d2d([b_ff1])])
    h2 = _rmsnorm_fwd(x1, norm_ffn_g, name="norm_ffn")
    a, (b_ff2,) = _matmul(h2, b_ff1, name="ff1", out_dtype=BF16, relu=True, b_sharded=True,
                          phases=[_gather_over_ici([b_ff2])])
    (b_ff2,) = _exchange([_gather_over_d2d([b_ff2])], name="allgather_w_ff2_d2d")
    w_ff2_full = b_ff2.reshape(N_CHIPS * b_ff2.shape[1], d_model)
    x2, _ = _matmul(a, w_ff2_full, name="ff2", out_dtype=F32, square_lhs=True, residual=x1)
    dx2, dg_final, loss = _loss_and_final_bwd(x2, target, norm_final_g.reshape(1, d_model), name="loss_head")

    def pair_sum(g, r, nm):
        return _add_halves(g, r, place, name="grads_pair_sum_" + nm)

    def chip_sum(p, q, nm):
        return _sum_chips(p, q, place, name="grads_chip_sum_" + nm)

    dw_ff2, _ = _matmul(a, dx2, name="ff2_dw", out_dtype=BF16, trans_a=True, square_lhs=True)
    dw_ff2 = dw_ff2.reshape(N_CHIPS, -1, d_model)
    da, (r_ff2,) = _matmul(dx2, w_ff2_full, name="ff2_dlhs", out_dtype=BF16, trans_b=True, scale2_by=a,
                           phases=[_swap_halves([dw_ff2])])
    ps_ff2 = pair_sum(dw_ff2, r_ff2, "w_ff2")
    dh2, (q_ff2,) = _matmul(da, b_ff1, name="ff1_dlhs", out_dtype=F32, trans_b=True, b_sharded=True,
                            phases=[_send_partials([ps_ff2])])
    g_ff2 = chip_sum(ps_ff2, q_ff2, "w_ff2")
    dw_ff1, (g_ff2,) = _matmul(h2, da, name="ff1_dw", out_dtype=BF16, trans_a=True, out_sharded=True,
                               phases=[_join_halves([g_ff2])])
    (dx1, dg_ffn), _ = _rmsnorm_bwd(dh2, x1, dx2, norm_ffn_g, name="norm_ffn_bwd")
    dw_out, _ = _matmul(merged, dx1, name="out_proj_dw", out_dtype=BF16, trans_a=True)
    dw_out = dw_out.reshape(N_CHIPS, -1, d_model)
    d_merged, (r_ff1, r_out) = _matmul(dx1, w_out_full, name="out_proj_dlhs", out_dtype=F32, trans_b=True,
                                       phases=[_swap_halves([dw_ff1, dw_out])])
    ps_ff1, ps_out = pair_sum(dw_ff1, r_ff1, "w_ff1"), pair_sum(dw_out, r_out, "w_out")
    d_o, dzu, dzv, dw_s, db_col, dlg, dlb, dag, dgg = _mix_bwd(
        z, o, d_merged, gmlp_ln_g, gmlp_ln_b, w_s[0], b_col, attn_out_g, gmlp_out_g, n_heads, name="mix_bwd")
    (dq, dk, dv, d_f_key, d_f_query), (q_ff1, q_out) = _attn_bwd(
        z, o, d_o, lse2, f_row, n_heads, name="attn_bwd", phases=[_send_partials([ps_ff1, ps_out])])
    g_ff1, g_out = chip_sum(ps_ff1, q_ff1, "w_ff1"), chip_sum(ps_out, q_out, "w_out")
    d_f = d_f_key.reshape(n_heads, seq) + d_f_query.reshape(n_heads, seq)
    d_f_pad = jnp.pad(d_f.T, ((0, 0), (0, LANES - n_heads)))
    dzf, db_f = _forget_bwd(d_f_pad, zb, name="forget_bwd")
    dz = jnp.concatenate([dq, dk, dv, dzu, dzv], axis=1)
    dw_main, (g_ff1, g_out) = _matmul(h, dz, name="in_proj_dw", out_dtype=BF16, trans_a=True,
                                      phases=[_join_halves([g_ff1, g_out])])
    dw_f, _ = _matmul(h, dzf, name="gate_dw", out_dtype=BF16, trans_a=True)
    segments = [(0, qkv, dw_main, 0), (qkv, qkv + n_heads, dw_f, 0), (qkv + n_heads, n_cols, dw_main, qkv)]
    dw_in = jnp.stack([_cols_from_segments(segments, j * shard_cols, (j + 1) * shard_cols) for j in range(N_CHIPS)])
    dh_gate, (r_in,) = _matmul(dzf, w_f, name="gate_dlhs", out_dtype=F32, trans_b=True, phases=[_swap_halves([dw_in])])
    ps_in = pair_sum(dw_in, r_in, "w_in")
    dh, (q_in,) = _matmul(dz, w_main, name="in_proj_dlhs", out_dtype=F32, trans_b=True, residual=dh_gate, tk=2560,
                          phases=[_send_partials([ps_in])])
    g_in_sum = chip_sum(ps_in, q_in, "w_in")
    (grad_x, dg_mix), (g_in_sum,) = _rmsnorm_bwd(dh, xs, dx1, norm_mix_g, name="norm_mix_bwd",
                                                 phases=[_join_halves([g_in_sum])])

    big = {}
    for nm, g, w, m, v in zip(names, (g_in_sum, g_out, g_ff1, g_ff2), (w_in, w_out, w_ff1, w_ff2),
                              (m_w_in, m_w_out, m_w_ff1, m_w_ff2), (v_w_in, v_w_out, v_w_ff1, v_w_ff2)):
        d, mo, vo = _adamw(w[0], g, m[0], v[0], name="adamw_" + nm)
        big[nm] = tuple(t[None] for t in (g, d, mo, vo))

    small_w = [norm_mix_g, b_f, gmlp_ln_g, gmlp_ln_b, w_s, b_s, attn_out_g, gmlp_out_g, norm_ffn_g, norm_final_g]
    small_m = [m_norm_mix_g, m_b_f, m_gmlp_ln_g, m_gmlp_ln_b, m_w_s, m_b_s, m_attn_out_g, m_gmlp_out_g, m_norm_ffn_g, m_norm_final_g]
    small_v = [v_norm_mix_g, v_b_f, v_gmlp_ln_g, v_gmlp_ln_b, v_w_s, v_b_s, v_attn_out_g, v_gmlp_out_g, v_norm_ffn_g, v_norm_final_g]
    small_g = [dg_mix, db_f[:, :n_heads], dlg, dlb, dw_s, db_col.T, dag, dgg, dg_ffn, dg_final]
    shapes = [w.shape for w in small_w]
    gathered = _allgather_small(_pack(small_g), name="allgather_small_grads")
    packed = _adamw_small(gathered, _pack(small_w), _pack(small_m), _pack(small_v), name="adamw_small")
    sg, sd, sm, sv = (_unpack(p, shapes) for p in packed)
    small_names = ["norm_mix_g", "b_f", "gmlp_ln_g", "gmlp_ln_b", "w_s", "b_s", "attn_out_g", "gmlp_out_g", "norm_ffn_g", "norm_final_g"]
    small = {nm: (sg[i], sd[i], sm[i], sv[i]) for i, nm in enumerate(small_names)}

    order = ["norm_mix_g", "w_in", "b_f", "gmlp_ln_g", "gmlp_ln_b", "w_s", "b_s", "attn_out_g", "gmlp_out_g", "w_out",
             "norm_ffn_g", "w_ff1", "w_ff2", "norm_final_g"]
    result = {**small, **big}
    total_loss = lax.psum(loss[0, 0], ("x", "y", "c"))
    outs = [total_loss, grad_x.reshape(x.shape)]
    for part in range(4):
        outs += [result[nm][part] for nm in order]
    return tuple(outs)
```

```python
import functools
import math

import numpy as np
import jax
import jax.numpy as jnp
from jax import lax
from jax.experimental import pallas as pl
from jax.experimental.pallas import tpu as pltpu

HEAD_DIM = 128
CHUNK = 128
EPS = 1e-6
LANES = 128
N_CHIPS = 4
N_DEV = 8
VMEM_LIMIT_BYTES = 56 * 1024 * 1024

ADAM_LR = 0.001
ADAM_B1 = 0.9
ADAM_B2 = 0.999
ADAM_EPS = 1e-08
ADAM_WD = 0.01
ADAM_STEP = 10

BF16 = jnp.bfloat16
F32 = jnp.float32
MESH = pl.DeviceIdType.MESH
ANY = pl.BlockSpec(memory_space=pl.ANY)
NEG_BIG = -1e30


def _params(*sem):
    return pltpu.CompilerParams(dimension_semantics=tuple(sem), vmem_limit_bytes=VMEM_LIMIT_BYTES)


def _tile(n, pref, unit):
    t = (min(pref, n) // unit) * unit
    while t >= unit:
        if n % t == 0:
            return t
        t -= unit
    return n


def _rc_tile(rows, cols, pref_rows=256, pref_cols=256):
    if rows % 16 == 0:
        return _tile(rows, pref_rows, 16), cols
    return rows, _tile(cols, pref_cols, LANES)


class _Phase:
    def __init__(self, arrays, out_shapes, in_place, n_sems, start, finish):
        self.arrays, self.out_shapes, self.in_place = list(arrays), list(out_shapes), in_place
        self.n_sems, self.start, self.finish = n_sems, start, finish

    @property
    def n_out(self):
        return len(self.arrays) if self.in_place else len(self.out_shapes)


def _run_phases(phases, steps, comm_in, comm_out, send_sems, recv_sems):
    at_in = at_out = at_sem = 0
    for ph in phases:
        for step in steps:
            getattr(ph, step)(comm_in[at_in:at_in + len(ph.arrays)], comm_out[at_out:at_out + ph.n_out],
                              lambda i, base=at_sem: send_sems.at[base + i], lambda i, base=at_sem: recv_sems.at[base + i])
        at_in, at_out, at_sem = at_in + len(ph.arrays), at_out + ph.n_out, at_sem + ph.n_sems


def _call(body, *, name, grid, in_specs, out_specs, out_shape, operands, semantics, scratch_shapes=(),
          n_prefetch=0, phases=()):
    in_specs, out_specs, out_shape = list(in_specs), list(out_specs), list(out_shape)
    scratch_shapes = list(scratch_shapes)
    n_in, n_out, n_scr = len(operands) - n_prefetch, len(out_shape), len(scratch_shapes)
    comm_in = [a for ph in phases for a in ph.arrays]
    comm_out = [jax.ShapeDtypeStruct(s.shape, s.dtype) for ph in phases
                for s in (ph.arrays if ph.in_place else ph.out_shapes)]
    aliases, at_in, at_out = {}, n_prefetch + n_in, n_out
    for ph in phases:
        if ph.in_place:
            aliases.update({at_in + r: at_out + r for r in range(len(ph.arrays))})
        at_in, at_out = at_in + len(ph.arrays), at_out + ph.n_out
    n_sems = sum(ph.n_sems for ph in phases)

    def hosted(*refs):
        pre, rest = refs[:n_prefetch], refs[n_prefetch:]
        ins, rest = rest[:n_in], rest[n_in:]
        cin, rest = rest[:len(comm_in)], rest[len(comm_in):]
        outs, rest = rest[:n_out], rest[n_out:]
        cout, rest = rest[:len(comm_out)], rest[len(comm_out):]
        scr = rest[:n_scr]
        if phases:
            send_sems, recv_sems = rest[n_scr:]
            ids = [pl.program_id(ax) for ax in range(len(grid))]
            first = functools.reduce(jnp.logical_and, [i == 0 for i in ids])
            last = functools.reduce(jnp.logical_and, [i == g - 1 for i, g in zip(ids, grid)])

            @pl.when(first)
            def _():
                _run_phases(phases, ("start",), cin, cout, send_sems, recv_sems)

        body(*pre, *ins, *outs, *scr)
        if phases:
            @pl.when(last)
            def _():
                _run_phases(phases, ("finish",), cin, cout, send_sems, recv_sems)

    all_in = in_specs + [ANY] * len(comm_in)
    all_out = out_specs + [ANY] * len(comm_out)
    all_scr = scratch_shapes + ([pltpu.SemaphoreType.DMA((n_sems,)), pltpu.SemaphoreType.DMA((n_sems,))] if phases else [])
    if phases:
        semantics = ("arbitrary",) * len(grid)
    kwargs = dict(name=name, out_shape=tuple(out_shape + comm_out), compiler_params=_params(*semantics),
                  input_output_aliases=aliases)
    if n_prefetch:
        kwargs["grid_spec"] = pltpu.PrefetchScalarGridSpec(
            num_scalar_prefetch=n_prefetch, grid=grid, in_specs=all_in, out_specs=tuple(all_out), scratch_shapes=all_scr)
    else:
        kwargs.update(grid=grid, in_specs=all_in, out_specs=tuple(all_out), scratch_shapes=all_scr)
    res = pl.pallas_call(hosted, **kwargs)(*operands, *comm_in)
    return tuple(res[:n_out]), tuple(res[n_out:])


def _only(results):
    outs, comm = results
    return outs[0] if len(outs) == 1 else outs, comm


def _matmul(a, b, *, name, out_dtype, trans_a=False, trans_b=False, tm=1024, tn=1024, tk=2048,
            square_lhs=False, relu=False, residual=None, scale2_by=None,
            b_sharded=False, out_sharded=False, phases=()):
    m, k = (a.shape[1], a.shape[0]) if trans_a else a.shape
    if b_sharded:
        if trans_b:
            n, ks = b.shape[1], b.shape[2]
            assert N_CHIPS * ks == k
        else:
            ns = b.shape[2]
            n = N_CHIPS * ns
            assert b.shape[1] == k
    else:
        n = b.shape[0] if trans_b else b.shape[1]
        assert (b.shape[1] if trans_b else b.shape[0]) == k
    tm = _tile(m, tm, 128)
    tn = _tile(n // N_CHIPS if (out_sharded or (b_sharded and not trans_b)) else n, tn, 128)
    tk = _tile(k // N_CHIPS if (b_sharded and trans_b) else k, tk, 128)
    nk = k // tk

    if trans_a:
        a_spec = pl.BlockSpec((tk, tm), lambda i, j, kk: (kk, i))
    else:
        a_spec = pl.BlockSpec((tm, tk), lambda i, j, kk: (i, kk))
    if b_sharded and trans_b:
        per = ks // tk
        assert per * tk == ks
        b_spec = pl.BlockSpec((None, tn, tk), lambda i, j, kk: (kk // per, j, kk % per))
    elif b_sharded:
        per = ns // tn
        assert per * tn == ns
        b_spec = pl.BlockSpec((None, tk, tn), lambda i, j, kk: (j // per, kk, j % per))
    elif trans_b:
        b_spec = pl.BlockSpec((tn, tk), lambda i, j, kk: (j, kk))
    else:
        b_spec = pl.BlockSpec((tk, tn), lambda i, j, kk: (kk, j))
    if out_sharded:
        ns_out = n // N_CHIPS
        per_o = ns_out // tn
        assert per_o * tn == ns_out
        out_shape = jax.ShapeDtypeStruct((N_CHIPS, m, ns_out), out_dtype)
        o_spec = pl.BlockSpec((None, tm, tn), lambda i, j, kk: (j // per_o, i, j % per_o))
    else:
        out_shape = jax.ShapeDtypeStruct((m, n), out_dtype)
        o_spec = pl.BlockSpec((tm, tn), lambda i, j, kk: (i, j))
    mn_spec = pl.BlockSpec((tm, tn), lambda i, j, kk: (i, j))

    operands, in_specs = [a, b], [a_spec, b_spec]
    if scale2_by is not None:
        operands.append(scale2_by)
        in_specs.append(mn_spec)
    if residual is not None:
        operands.append(residual)
        in_specs.append(mn_spec)
    dims = (((0 if trans_a else 1,), (1 if trans_b else 0,)), ((), ()))

    def body(*refs):
        a_ref, b_ref = refs[0], refs[1]
        pos = 2
        scale_ref = res_ref = None
        if scale2_by is not None:
            scale_ref = refs[pos]
            pos += 1
        if residual is not None:
            res_ref = refs[pos]
            pos += 1
        o_ref = refs[pos]
        kk = pl.program_id(2)

        av = a_ref[...]
        if square_lhs:
            av = av.astype(F32)
            av = av * av
        part = lax.dot_general(av.astype(BF16), b_ref[...].astype(BF16), dims, preferred_element_type=F32)

        def finish(r):
            if relu:
                r = jnp.maximum(r, 0.0)
            if scale_ref is not None:
                r = r * (2.0 * scale_ref[...].astype(F32))
            if res_ref is not None:
                r = r + res_ref[...].astype(F32)
            o_ref[...] = r.astype(out_dtype)

        if nk == 1:
            finish(part)
        else:
            acc_ref = refs[pos + 1]

            @pl.when(kk == 0)
            def _():
                acc_ref[...] = part

            @pl.when(jnp.logical_and(kk > 0, kk < nk - 1))
            def _():
                acc_ref[...] += part

            @pl.when(kk == nk - 1)
            def _():
                finish(acc_ref[...] + part)

    return _only(_call(
        body, name=name, out_shape=[out_shape], grid=(m // tm, n // tn, nk),
        in_specs=in_specs, out_specs=[o_spec], operands=operands,
        scratch_shapes=[pltpu.VMEM((tm, tn), F32)] if nk > 1 else [],
        semantics=("parallel", "parallel", "arbitrary"), phases=phases))


def _rmsnorm_fwd(x, g, *, name, tr=512):
    s, d = x.shape
    tr = _tile(s, tr, 8)

    def body(x_ref, g_ref, o_ref):
        xv = x_ref[...]
        r = lax.rsqrt(jnp.mean(xv * xv, axis=-1, keepdims=True) + EPS)
        o_ref[...] = ((xv * r) * g_ref[...]).astype(BF16)

    return pl.pallas_call(
        body, name=name, out_shape=jax.ShapeDtypeStruct((s, d), BF16), grid=(s // tr,),
        in_specs=[pl.BlockSpec((tr, d), lambda i: (i, 0)), pl.BlockSpec((1, d), lambda i: (0, 0))],
        out_specs=pl.BlockSpec((tr, d), lambda i: (i, 0)),
        compiler_params=_params("parallel"),
    )(x, g)


def _rms_bwd_rows(dy, xv, g):
    d = xv.shape[-1]
    r = lax.rsqrt(jnp.mean(xv * xv, axis=-1, keepdims=True) + EPS)
    gdy = dy * g
    dot = jnp.sum(gdy * xv, axis=-1, keepdims=True)
    dx = gdy * r - xv * (r * r * r) * (dot / d)
    return dx, dy * (xv * r)


def _rmsnorm_bwd(dy, x, res, g, *, name, tr=256, phases=()):
    s, d = x.shape
    tr = _tile(s, tr, 8)

    def body(dy_ref, x_ref, res_ref, g_ref, dx_ref, dg_ref):
        @pl.when(pl.program_id(0) == 0)
        def _():
            dg_ref[...] = jnp.zeros_like(dg_ref)

        dx, dg_rows = _rms_bwd_rows(dy_ref[...].astype(F32), x_ref[...], g_ref[...])
        dx_ref[...] = res_ref[...] + dx
        dg_ref[...] += jnp.sum(dg_rows, axis=0, keepdims=True)

    row = pl.BlockSpec((tr, d), lambda i: (i, 0))
    vec = pl.BlockSpec((1, d), lambda i: (0, 0))
    return _call(
        body, name=name, out_shape=[jax.ShapeDtypeStruct((s, d), F32), jax.ShapeDtypeStruct((1, d), F32)],
        grid=(s // tr,), in_specs=[row, row, row, vec], out_specs=[row, vec], operands=[dy, x, res, g],
        semantics=("arbitrary",), phases=phases)


def _loss_and_final_bwd(x2, target, g, *, name, tr=256):
    s, d = x2.shape
    tr = _tile(s, tr, 8)

    def body(x_ref, t_ref, g_ref, dx_ref, dg_ref, loss_ref):
        @pl.when(pl.program_id(0) == 0)
        def _():
            dg_ref[...] = jnp.zeros_like(dg_ref)
            loss_ref[...] = jnp.zeros_like(loss_ref)

        xv, gv = x_ref[...], g_ref[...]
        r = lax.rsqrt(jnp.mean(xv * xv, axis=-1, keepdims=True) + EPS)
        err = (xv * r) * gv - t_ref[...]
        row_loss = jnp.mean(err * err, axis=-1, keepdims=True)
        loss_ref[...] += 0.5 * jnp.sum(row_loss, axis=0, keepdims=True)
        dx, dg_rows = _rms_bwd_rows(err / d, xv, gv)
        dx_ref[...] = dx
        dg_ref[...] += jnp.sum(dg_rows, axis=0, keepdims=True)

    row = pl.BlockSpec((tr, d), lambda i: (i, 0))
    vec = pl.BlockSpec((1, d), lambda i: (0, 0))
    one = pl.BlockSpec((1, 1), lambda i: (0, 0))
    return pl.pallas_call(
        body, name=name,
        out_shape=(jax.ShapeDtypeStruct((s, d), F32), jax.ShapeDtypeStruct((1, d), F32),
                   jax.ShapeDtypeStruct((1, 1), F32)),
        grid=(s // tr,), in_specs=[row, row, vec], out_specs=(row, vec, one),
        compiler_params=_params("arbitrary"),
    )(x2, target, g)


def _tri_ones(n, lower):
    r = lax.broadcasted_iota(jnp.int32, (n, n), 0)
    c = lax.broadcasted_iota(jnp.int32, (n, n), 1)
    return jnp.where((c <= r) if lower else (c >= r), 1.0, 0.0).astype(F32)


def _forget_fwd(h, w_f, b_f, *, name, tr=256):
    s, d = h.shape
    tr = _tile(s, tr, 8)

    def body(h_ref, w_ref, b_ref, zb_ref, f_ref, carry):
        @pl.when(pl.program_id(0) == 0)
        def _():
            carry[...] = jnp.zeros_like(carry)

        zb = lax.dot_general(h_ref[...], w_ref[...], (((1,), (1,)), ((), ())), preferred_element_type=F32) + b_ref[...]
        zb_ref[...] = zb
        log_f = jnp.minimum(zb, 0.0) - jnp.log(1.0 + jnp.exp(-jnp.abs(zb)))
        run = jnp.dot(_tri_ones(tr, True), log_f, preferred_element_type=F32,
                      precision=lax.Precision.HIGHEST) + carry[...]
        f_ref[...] = run
        carry[...] = run[tr - 1:tr, :]

    row = pl.BlockSpec((tr, LANES), lambda i: (i, 0))
    return pl.pallas_call(
        body, name=name,
        out_shape=(jax.ShapeDtypeStruct((s, LANES), F32), jax.ShapeDtypeStruct((s, LANES), F32)),
        grid=(s // tr,),
        in_specs=[pl.BlockSpec((tr, d), lambda i: (i, 0)), pl.BlockSpec((LANES, d), lambda i: (0, 0)),
                  pl.BlockSpec((1, LANES), lambda i: (0, 0))],
        out_specs=(row, row), scratch_shapes=[pltpu.VMEM((1, LANES), F32)],
        compiler_params=_params("arbitrary"),
    )(h, w_f, b_f)


def _forget_bwd(d_f, zb, *, name, tr=256):
    s = zb.shape[0]
    tr = _tile(s, tr, 8)
    nb = s // tr

    def body(df_ref, zb_ref, dz_ref, db_ref, carry):
        @pl.when(pl.program_id(0) == 0)
        def _():
            carry[...] = jnp.zeros_like(carry)
            db_ref[...] = jnp.zeros_like(db_ref)

        run = jnp.dot(_tri_ones(tr, False), df_ref[...], preferred_element_type=F32,
                      precision=lax.Precision.HIGHEST) + carry[...]
        carry[...] = run[0:1, :]
        dz = run / (1.0 + jnp.exp(zb_ref[...]))
        dz_ref[...] = dz.astype(BF16)
        db_ref[...] += jnp.sum(dz, axis=0, keepdims=True)

    row = pl.BlockSpec((tr, LANES), lambda i: (nb - 1 - i, 0))
    return pl.pallas_call(
        body, name=name,
        out_shape=(jax.ShapeDtypeStruct((s, LANES), BF16), jax.ShapeDtypeStruct((1, LANES), F32)),
        grid=(nb,), in_specs=[row, row], out_specs=(row, pl.BlockSpec((1, LANES), lambda i: (0, 0))),
        scratch_shapes=[pltpu.VMEM((1, LANES), F32)],
        compiler_params=_params("arbitrary"),
    )(d_f, zb)


def _pairs(nblk, by_kv):
    if by_kv:
        pr = [(i, j) for j in range(nblk) for i in range(j, nblk)]
    else:
        pr = [(i, j) for i in range(nblk) for j in range(i + 1)]
    return (jnp.asarray(np.array([p[0] for p in pr], np.int32)), jnp.asarray(np.array([p[1] for p in pr], np.int32)))


def _causal_mask(t):
    r = lax.broadcasted_iota(jnp.int32, (t, t), 0)
    c = lax.broadcasted_iota(jnp.int32, (t, t), 1)
    return c <= r


LOG2E = math.log2(math.e)
QK_TO_LOG2 = LOG2E / math.sqrt(HEAD_DIM)


def _attn_logits2(q, k, fk_row):
    sc = lax.dot_general(q, k, (((1,), (1,)), ((), ())), preferred_element_type=F32)
    return sc * QK_TO_LOG2 - fk_row * LOG2E


def _attn_fwd(z, f_row, n_heads, *, name, tb=512, phases=()):
    s = z.shape[0]
    tb = _tile(s, tb, 128)
    nblk = s // tb
    rep = tb // LANES
    qi, kj = _pairs(nblk, by_kv=False)

    def body(qi_ref, kj_ref, q_ref, k_ref, v_ref, fk_ref, o_ref, lse_ref, m_sc, l_sc, acc_sc):
        p = pl.program_id(1)
        i, j = qi_ref[p], kj_ref[p]

        @pl.when(j == 0)
        def _():
            m_sc[...] = jnp.full_like(m_sc, NEG_BIG)
            l_sc[...] = jnp.zeros_like(l_sc)
            acc_sc[...] = jnp.zeros_like(acc_sc)

        def update(masked):
            s2 = _attn_logits2(q_ref[...], k_ref[...], fk_ref[...])
            if masked:
                s2 = jnp.where(_causal_mask(tb), s2, NEG_BIG)
            m_old = m_sc[...]
            m_new = jnp.maximum(m_old, jnp.max(s2, axis=-1, keepdims=True))
            alpha = jnp.exp2(m_old - m_new)
            pv = jnp.exp2(s2 - jnp.tile(m_new, (1, rep)))
            l_sc[...] = alpha * l_sc[...] + jnp.sum(pv, axis=-1, keepdims=True)
            acc_sc[...] = alpha * acc_sc[...] + jnp.dot(pv.astype(BF16), v_ref[...], preferred_element_type=F32)
            m_sc[...] = m_new

        @pl.when(j < i)
        def _():
            update(False)

        @pl.when(j == i)
        def _():
            update(True)
            o_ref[...] = (acc_sc[...] / l_sc[...]).astype(BF16)
            lse_ref[...] = m_sc[...] + jnp.log2(l_sc[...])

    h = n_heads
    return _call(
        body, name=name, n_prefetch=2, grid=(h, int(qi.shape[0])),
        in_specs=[
            pl.BlockSpec((tb, HEAD_DIM), lambda hh, p, qi_r, kj_r: (qi_r[p], hh)),
            pl.BlockSpec((tb, HEAD_DIM), lambda hh, p, qi_r, kj_r: (kj_r[p], h + hh)),
            pl.BlockSpec((tb, HEAD_DIM), lambda hh, p, qi_r, kj_r: (kj_r[p], 2 * h + hh)),
            pl.BlockSpec((None, 1, tb), lambda hh, p, qi_r, kj_r: (hh, 0, kj_r[p])),
        ],
        out_specs=[
            pl.BlockSpec((tb, HEAD_DIM), lambda hh, p, qi_r, kj_r: (qi_r[p], hh)),
            pl.BlockSpec((None, tb, LANES), lambda hh, p, qi_r, kj_r: (hh, qi_r[p], 0)),
        ],
        scratch_shapes=[pltpu.VMEM((tb, LANES), F32), pltpu.VMEM((tb, LANES), F32), pltpu.VMEM((tb, HEAD_DIM), F32)],
        out_shape=[jax.ShapeDtypeStruct((s, h * HEAD_DIM), BF16), jax.ShapeDtypeStruct((h, s, LANES), F32)],
        operands=[qi, kj, z, z, z, f_row], semantics=("parallel", "arbitrary"), phases=phases)


def _attn_bwd(z, o, d_o, lse2, f_row, n_heads, *, name, tb=512, phases=()):
    s = z.shape[0]
    tb = _tile(s, tb, 128)
    nblk = s // tb
    rep = tb // LANES
    qi, kj = _pairs(nblk, by_kv=True)
    n_pairs = int(qi.shape[0])
    scale = 1.0 / math.sqrt(HEAD_DIM)
    h = n_heads

    def body(qi_ref, kj_ref, q_ref, k_ref, v_ref, o_ref, do_ref, lse_ref, fk_ref,
             dq_ref, dk_ref, dv_ref, df_ref, dfq_ref, dq_sc, dk_sc, dv_sc, df_sc, dfq_sc):
        p = pl.program_id(1)
        i, j = qi_ref[p], kj_ref[p]

        @pl.when(p == 0)
        def _():
            dq_sc[...] = jnp.zeros_like(dq_sc)
            dfq_sc[...] = jnp.zeros_like(dfq_sc)

        @pl.when(i == j)
        def _():
            dk_sc[...] = jnp.zeros_like(dk_sc)
            dv_sc[...] = jnp.zeros_like(dv_sc)
            df_sc[...] = jnp.zeros_like(df_sc)

        def update(masked):
            q, k, v, do = q_ref[...], k_ref[...], v_ref[...], do_ref[...]
            delta = jnp.sum(do.astype(F32) * o_ref[...].astype(F32), axis=-1, keepdims=True)
            pv = jnp.exp2(_attn_logits2(q, k, fk_ref[...]) - jnp.tile(lse_ref[...], (1, rep)))
            if masked:
                pv = jnp.where(_causal_mask(tb), pv, 0.0)
            dp = lax.dot_general(do, v, (((1,), (1,)), ((), ())), preferred_element_type=F32)
            ds = pv * (dp - delta)
            ds_b = ds.astype(BF16)
            dv_sc[...] += lax.dot_general(pv.astype(BF16), do, (((0,), (0,)), ((), ())), preferred_element_type=F32)
            dk_sc[...] += lax.dot_general(ds_b, q, (((0,), (0,)), ((), ())), preferred_element_type=F32)
            rows = pl.ds(pl.multiple_of(i * tb, tb), tb)
            dq_sc[rows, :] += jnp.dot(ds_b, k, preferred_element_type=F32)
            df_sc[...] -= jnp.sum(ds, axis=0, keepdims=True)
            dfq_sc[rows, :] += jnp.broadcast_to(jnp.sum(ds, axis=1, keepdims=True), (tb, LANES))

        @pl.when(i > j)
        def _():
            update(False)

        @pl.when(i == j)
        def _():
            update(True)

        @pl.when(i == nblk - 1)
        def _():
            dk_ref[...] = (dk_sc[...] * scale).astype(BF16)
            dv_ref[...] = dv_sc[...].astype(BF16)
            df_ref[...] = df_sc[...]

        @pl.when(p == n_pairs - 1)
        def _():
            dq_ref[...] = (dq_sc[...] * scale).astype(BF16)
            dfq_ref[...] = jnp.transpose(dfq_sc[...])[0:1, :]

    qblk = lambda off: pl.BlockSpec((tb, HEAD_DIM), lambda hh, p, qi_r, kj_r: (qi_r[p], off + hh))
    kblk = lambda off: pl.BlockSpec((tb, HEAD_DIM), lambda hh, p, qi_r, kj_r: (kj_r[p], off + hh))
    qrep = pl.BlockSpec((None, tb, LANES), lambda hh, p, qi_r, kj_r: (hh, qi_r[p], 0))
    krow = pl.BlockSpec((None, 1, tb), lambda hh, p, qi_r, kj_r: (hh, 0, kj_r[p]))
    act = jax.ShapeDtypeStruct((s, h * HEAD_DIM), BF16)
    return _call(
        body, name=name, n_prefetch=2, grid=(h, n_pairs),
        in_specs=[qblk(0), kblk(h), kblk(2 * h), qblk(0), qblk(0), qrep, krow],
        out_specs=[
            pl.BlockSpec((s, HEAD_DIM), lambda hh, p, qi_r, kj_r: (0, hh)),
            kblk(0), kblk(0), krow,
            pl.BlockSpec((None, 1, s), lambda hh, p, qi_r, kj_r: (hh, 0, 0)),
        ],
        scratch_shapes=[pltpu.VMEM((s, HEAD_DIM), F32), pltpu.VMEM((tb, HEAD_DIM), F32),
                        pltpu.VMEM((tb, HEAD_DIM), F32), pltpu.VMEM((1, tb), F32), pltpu.VMEM((s, LANES), F32)],
        out_shape=[act, act, act, jax.ShapeDtypeStruct((h, 1, s), F32), jax.ShapeDtypeStruct((h, 1, s), F32)],
        operands=[qi, kj, z, z, z, o, d_o, lse2, f_row], semantics=("parallel", "arbitrary"), phases=phases)


GELU_C = math.sqrt(2.0 / math.pi)
GELU_A = 0.044715


def _gelu(x):
    return 0.5 * x * (1.0 + jnp.tanh(GELU_C * (x + GELU_A * (x * x * x))))


def _gelu_and_grad(x):
    t = jnp.tanh(GELU_C * (x + GELU_A * (x * x * x)))
    y = 0.5 * x * (1.0 + t)
    dy = 0.5 * (1.0 + t) + 0.5 * x * (1.0 - t * t) * (GELU_C * (1.0 + 3.0 * GELU_A * (x * x)))
    return y, dy


def _layernorm_parts(g):
    mu = jnp.mean(g, axis=-1, keepdims=True)
    xc = g - mu
    rs = lax.rsqrt(jnp.mean(xc * xc, axis=-1, keepdims=True) + EPS)
    return xc * rs, rs


def _spatial_mix(w_ref, bcol_ref, vv_b, n_heads, n_chunks):
    tril = _causal_mask(CHUNK)
    cols = []
    for hh in range(n_heads):
        wc = jnp.where(tril, w_ref[hh], 0.0).astype(BF16)
        lanes = slice(hh * HEAD_DIM, (hh + 1) * HEAD_DIM)
        rows = [jnp.dot(wc, vv_b[c * CHUNK:(c + 1) * CHUNK, lanes], preferred_element_type=F32)
                + bcol_ref[:, hh:hh + 1] for c in range(n_chunks)]
        cols.append(jnp.concatenate(rows, axis=0))
    return jnp.concatenate(cols, axis=1)


def _mix_fwd(z, o, ln_g, ln_b, w_s, b_col, attn_g, gm_g, n_heads, *, name, tr=256):
    s = z.shape[0]
    dg = n_heads * HEAD_DIM
    tr = _tile(s, tr, CHUNK)
    n_chunks = tr // CHUNK

    def body(zu_ref, zv_ref, o_ref, lg_ref, lb_ref, w_ref, bcol_ref, ag_ref, gg_ref, out_ref):
        u = _gelu(zu_ref[...].astype(F32))
        xhat, _ = _layernorm_parts(_gelu(zv_ref[...].astype(F32)))
        vv = xhat * lg_ref[...] + lb_ref[...]
        gm = u * _spatial_mix(w_ref, bcol_ref, vv.astype(BF16), n_heads, n_chunks)
        rg = lax.rsqrt(jnp.mean(gm * gm, axis=-1, keepdims=True) + EPS)
        ov = o_ref[...].astype(F32)
        ra = lax.rsqrt(jnp.mean(ov * ov, axis=-1, keepdims=True) + EPS)
        out_ref[:, :dg] = ((ov * ra) * ag_ref[...]).astype(BF16)
        out_ref[:, dg:] = ((gm * rg) * gg_ref[...]).astype(BF16)

    vec = pl.BlockSpec((1, dg), lambda i: (0, 0))
    return pl.pallas_call(
        body, name=name, out_shape=jax.ShapeDtypeStruct((s, 2 * dg), BF16), grid=(s // tr,),
        in_specs=[pl.BlockSpec((tr, dg), lambda i: (i, 3)), pl.BlockSpec((tr, dg), lambda i: (i, 4)),
                  pl.BlockSpec((tr, dg), lambda i: (i, 0)), vec, vec,
                  pl.BlockSpec((n_heads, CHUNK, CHUNK), lambda i: (0, 0, 0)),
                  pl.BlockSpec((CHUNK, n_heads), lambda i: (0, 0)), vec, vec],
        out_specs=pl.BlockSpec((tr, 2 * dg), lambda i: (i, 0)),
        compiler_params=_params("parallel"),
    )(z, z, o, ln_g, ln_b, w_s, b_col, attn_g, gm_g)


def _mix_bwd(z, o, d_merged, ln_g, ln_b, w_s, b_col, attn_g, gm_g, n_heads, *, name, tr=256):
    s = z.shape[0]
    dg = n_heads * HEAD_DIM
    tr = _tile(s, tr, CHUNK)
    n_chunks = tr // CHUNK

    def body(zu_ref, zv_ref, o_ref, dm_ref, lg_ref, lb_ref, w_ref, bcol_ref, ag_ref, gg_ref,
             do_ref, dzu_ref, dzv_ref, dw_ref, dbcol_ref, dlg_ref, dlb_ref, dag_ref, dgg_ref):
        @pl.when(pl.program_id(0) == 0)
        def _():
            for ref in (dw_ref, dbcol_ref, dlg_ref, dlb_ref, dag_ref, dgg_ref):
                ref[...] = jnp.zeros_like(ref)

        d_o, dag_rows = _rms_bwd_rows(dm_ref[:, :dg], o_ref[...].astype(F32), ag_ref[...])
        do_ref[...] = d_o.astype(BF16)
        dag_ref[...] += jnp.sum(dag_rows, axis=0, keepdims=True)

        u, du_dz = _gelu_and_grad(zu_ref[...].astype(F32))
        gv, dgv_dz = _gelu_and_grad(zv_ref[...].astype(F32))
        xhat, rs = _layernorm_parts(gv)
        lg = lg_ref[...]
        vv_b = (xhat * lg + lb_ref[...]).astype(BF16)
        mix = _spatial_mix(w_ref, bcol_ref, vv_b, n_heads, n_chunks)
        gm = u * mix
        d_gm, dgg_rows = _rms_bwd_rows(dm_ref[:, dg:], gm, gg_ref[...])
        dgg_ref[...] += jnp.sum(dgg_rows, axis=0, keepdims=True)
        dzu_ref[...] = ((d_gm * mix) * du_dz).astype(BF16)
        d_mix = d_gm * u
        d_mix_b = d_mix.astype(BF16)

        tril = _causal_mask(CHUNK)
        lane = lax.broadcasted_iota(jnp.int32, (CHUNK, n_heads), 1)
        cols = []
        db = jnp.zeros((CHUNK, n_heads), F32)
        for hh in range(n_heads):
            wc = jnp.where(tril, w_ref[hh], 0.0).astype(BF16)
            lanes = slice(hh * HEAD_DIM, (hh + 1) * HEAD_DIM)
            dw = jnp.zeros((CHUNK, CHUNK), F32)
            dmix_sum = jnp.zeros((CHUNK, HEAD_DIM), F32)
            rows = []
            for c in range(n_chunks):
                rws = slice(c * CHUNK, (c + 1) * CHUNK)
                dmb = d_mix_b[rws, lanes]
                dw += lax.dot_general(dmb, vv_b[rws, lanes], (((1,), (1,)), ((), ())), preferred_element_type=F32)
                rows.append(lax.dot_general(wc, dmb, (((0,), (0,)), ((), ())), preferred_element_type=F32))
                dmix_sum += d_mix[rws, lanes]
            dw_ref[hh] += jnp.where(tril, dw, 0.0)
            db += jnp.where(lane == hh, jnp.sum(dmix_sum, axis=-1, keepdims=True), 0.0)
            cols.append(jnp.concatenate(rows, axis=0))
        dbcol_ref[...] += db
        d_vv = jnp.concatenate(cols, axis=1)

        dlg_ref[...] += jnp.sum(d_vv * xhat, axis=0, keepdims=True)
        dlb_ref[...] += jnp.sum(d_vv, axis=0, keepdims=True)
        d_xhat = d_vv * lg
        d_gv = rs * (d_xhat - jnp.mean(d_xhat, axis=-1, keepdims=True)
                     - xhat * jnp.mean(d_xhat * xhat, axis=-1, keepdims=True))
        dzv_ref[...] = (d_gv * dgv_dz).astype(BF16)

    vec = pl.BlockSpec((1, dg), lambda i: (0, 0))
    wspec = pl.BlockSpec((n_heads, CHUNK, CHUNK), lambda i: (0, 0, 0))
    bspec = pl.BlockSpec((CHUNK, n_heads), lambda i: (0, 0))
    rowb = pl.BlockSpec((tr, dg), lambda i: (i, 0))
    act = jax.ShapeDtypeStruct((s, dg), BF16)
    vshape = jax.ShapeDtypeStruct((1, dg), F32)
    return pl.pallas_call(
        body, name=name,
        out_shape=(act, act, act, jax.ShapeDtypeStruct((n_heads, CHUNK, CHUNK), F32),
                   jax.ShapeDtypeStruct((CHUNK, n_heads), F32), vshape, vshape, vshape, vshape),
        grid=(s // tr,),
        in_specs=[pl.BlockSpec((tr, dg), lambda i: (i, 3)), pl.BlockSpec((tr, dg), lambda i: (i, 4)),
                  rowb, pl.BlockSpec((tr, 2 * dg), lambda i: (i, 0)), vec, vec, wspec, bspec, vec, vec],
        out_specs=(rowb, rowb, rowb, wspec, bspec, vec, vec, vec, vec),
        compiler_params=_params("arbitrary"),
    )(z, z, o, d_merged, ln_g, ln_b, w_s, b_col, attn_g, gm_g)


def _place():
    x, y, c = lax.axis_index("x"), lax.axis_index("y"), lax.axis_index("c")
    other_chips = [(1 - x, y), (x, 1 - y), (1 - x, 1 - y)]
    return x, y, c, other_chips


def _remote(src, dst, send_sem, recv_sem, to):
    return pltpu.make_async_remote_copy(src_ref=src, dst_ref=dst, send_sem=send_sem, recv_sem=recv_sem,
                                        device_id=to, device_id_type=MESH)


def _cast_into_slot(w, place, *, name):
    rows, cols = w.shape
    tr, tc = _rc_tile(rows, cols)

    def body(place_ref, w_ref, o_ref):
        o_ref[...] = w_ref[...].astype(BF16)

    grid_spec = pltpu.PrefetchScalarGridSpec(
        num_scalar_prefetch=1, grid=(rows // tr, cols // tc),
        in_specs=[pl.BlockSpec((tr, tc), lambda i, j, pr: (i, j))],
        out_specs=pl.BlockSpec((None, tr, tc), lambda i, j, pr: (pr[0], i, j)),
    )
    return pl.pallas_call(
        body, name=name, grid_spec=grid_spec, out_shape=jax.ShapeDtypeStruct((N_CHIPS, rows, cols), BF16),
        compiler_params=_params("parallel", "parallel"),
    )(place, w)


def _exchange(phases, *, name):
    comm_in = [a for ph in phases for a in ph.arrays]
    comm_out = [jax.ShapeDtypeStruct(s.shape, s.dtype) for ph in phases for s in (ph.arrays if ph.in_place else ph.out_shapes)]
    aliases, at_in, at_out = {}, 0, 0
    for ph in phases:
        if ph.in_place:
            aliases.update({at_in + r: at_out + r for r in range(len(ph.arrays))})
        at_in, at_out = at_in + len(ph.arrays), at_out + ph.n_out
    n_sems = sum(ph.n_sems for ph in phases)

    def body(*refs):
        cin, cout = refs[:len(comm_in)], refs[len(comm_in):len(comm_in) + len(comm_out)]
        send_sems, recv_sems = refs[len(comm_in) + len(comm_out):]
        _run_phases(phases, ("start", "finish"), cin, cout, send_sems, recv_sems)

    return pl.pallas_call(
        body, name=name, out_shape=tuple(comm_out), in_specs=[ANY] * len(comm_in), out_specs=tuple([ANY] * len(comm_out)),
        input_output_aliases=aliases,
        scratch_shapes=[pltpu.SemaphoreType.DMA((n_sems,)), pltpu.SemaphoreType.DMA((n_sems,))],
    )(*comm_in)


def _gather_over_ici(bufs):
    def copies(outs, send, recv, incoming):
        x, y, c, chips = _place()
        for t, buf in enumerate(outs):
            half = buf.shape[2] // 2
            for k, (cx, cy) in enumerate(chips):
                blk = buf.at[(2 * cx + cy) if incoming else (2 * x + y), :, pl.ds(c * half, half)]
                yield _remote(blk, blk, send(3 * t + k), recv(3 * t + k), (cx, cy, c))

    def start(ins, outs, send, recv):
        for cp in copies(outs, send, recv, False):
            cp.start()

    def finish(ins, outs, send, recv):
        for cp in copies(outs, send, recv, True):
            cp.wait_recv()
        for cp in copies(outs, send, recv, False):
            cp.wait_send()

    return _Phase(bufs, [], True, 3 * len(bufs), start, finish)


def _gather_over_d2d(bufs):
    def copies(outs, send, recv, incoming):
        x, y, c, chips = _place()
        for t, buf in enumerate(outs):
            half = buf.shape[2] // 2
            for k, (cx, cy) in enumerate(chips):
                blk = buf.at[2 * cx + cy, :, pl.ds(((1 - c) if incoming else c) * half, half)]
                yield _remote(blk, blk, send(3 * t + k), recv(3 * t + k), (x, y, 1 - c))

    def start(ins, outs, send, recv):
        for cp in copies(outs, send, recv, False):
            cp.start()

    def finish(ins, outs, send, recv):
        for cp in copies(outs, send, recv, True):
            cp.wait_recv()
        for cp in copies(outs, send, recv, False):
            cp.wait_send()

    return _Phase(bufs, [], True, 3 * len(bufs), start, finish)


def _then(first, second):
    n_first = first.n_sems

    def later(sem):
        return lambda i: sem(n_first + i)

    def start(ins, outs, send, recv):
        first.start(ins, outs, send, recv)
        first.finish(ins, outs, send, recv)
        second.start(ins, outs, later(send), later(recv))

    def finish(ins, outs, send, recv):
        second.finish(ins, outs, later(send), later(recv))

    return _Phase(first.arrays, [], True, n_first + second.n_sems, start, finish)


def _swap_halves(grads):
    def copies(ins, outs, send, recv):
        x, y, c, _ = _place()
        for t, g in enumerate(ins):
            half = g.shape[2] // 2
            yield _remote(g.at[:, :, pl.ds((1 - c) * half, half)], outs[t], send(t), recv(t), (x, y, 1 - c))

    def start(ins, outs, send, recv):
        for cp in copies(ins, outs, send, recv):
            cp.start()

    def finish(ins, outs, send, recv):
        for cp in copies(ins, outs, send, recv):
            cp.wait()

    shapes = [jax.ShapeDtypeStruct((a.shape[0], a.shape[1], a.shape[2] // 2), a.dtype) for a in grads]
    return _Phase(grads, shapes, False, len(grads), start, finish)


def _add_halves(grad, received, place, *, name):
    ns, rows, half = received.shape
    tr, tc = _rc_tile(rows, half)
    per = half // tc

    def body(place_ref, g_ref, r_ref, o_ref):
        o_ref[...] = (g_ref[...].astype(F32) + r_ref[...].astype(F32)).astype(BF16)

    grid_spec = pltpu.PrefetchScalarGridSpec(
        num_scalar_prefetch=1, grid=(ns, rows // tr, per),
        in_specs=[pl.BlockSpec((None, tr, tc), lambda s, i, j, pr: (s, i, pr[1] * per + j)),
                  pl.BlockSpec((None, tr, tc), lambda s, i, j, pr: (s, i, j))],
        out_specs=pl.BlockSpec((None, tr, tc), lambda s, i, j, pr: (s, i, j)),
    )
    return pl.pallas_call(
        body, name=name, grid_spec=grid_spec, out_shape=jax.ShapeDtypeStruct(received.shape, BF16),
        compiler_params=_params("parallel", "parallel", "parallel"),
    )(place, grad, received)


def _send_partials(parts):
    def start(ins, outs, send, recv):
        x, y, c, chips = _place()
        for t, part in enumerate(ins):
            for k, (cx, cy) in enumerate(chips):
                _remote(part.at[2 * cx + cy], outs[t].at[2 * x + y], send(3 * t + k), recv(3 * t + k), (cx, cy, c)).start()

    def finish(ins, outs, send, recv):
        x, y, c, chips = _place()
        for t, part in enumerate(ins):
            for k, (cx, cy) in enumerate(chips):
                slot = outs[t].at[2 * cx + cy]
                _remote(slot, slot, send(3 * t + k), recv(3 * t + k), (cx, cy, c)).wait_recv()
        for t, part in enumerate(ins):
            for k, (cx, cy) in enumerate(chips):
                sent = part.at[2 * cx + cy]
                _remote(sent, sent, send(3 * t + k), recv(3 * t + k), (cx, cy, c)).wait_send()

    shapes = [jax.ShapeDtypeStruct(a.shape, a.dtype) for a in parts]
    return _Phase(parts, shapes, False, 3 * len(parts), start, finish)


def _sum_chips(parts, slots, place, *, name):
    ns, rows, half = slots.shape
    tr, tc = _rc_tile(rows, half)
    per = half // tc

    def body(place_ref, p_ref, s_ref, o_ref):
        acc = p_ref[...].astype(F32)
        for k in range(ns):
            acc = acc + jnp.where(place_ref[0] == k, 0.0, s_ref[k].astype(F32))
        o_ref[...] = acc

    grid_spec = pltpu.PrefetchScalarGridSpec(
        num_scalar_prefetch=1, grid=(rows // tr, per),
        in_specs=[pl.BlockSpec((None, tr, tc), lambda i, j, pr: (pr[0], i, j)),
                  pl.BlockSpec((ns, tr, tc), lambda i, j, pr: (0, i, j))],
        out_specs=pl.BlockSpec((tr, tc), lambda i, j, pr: (i, pr[1] * per + j)),
    )
    return pl.pallas_call(
        body, name=name, grid_spec=grid_spec, out_shape=jax.ShapeDtypeStruct((rows, 2 * half), F32),
        compiler_params=_params("parallel", "parallel"),
    )(place, parts, slots)


def _join_halves(bufs):
    def copies(outs, send, recv, incoming):
        x, y, c, _ = _place()
        for t, buf in enumerate(outs):
            half = buf.shape[1] // 2
            cols = buf.at[:, pl.ds(((1 - c) if incoming else c) * half, half)]
            yield _remote(cols, cols, send(t), recv(t), (x, y, 1 - c))

    def start(ins, outs, send, recv):
        for cp in copies(outs, send, recv, False):
            cp.start()

    def finish(ins, outs, send, recv):
        for cp in copies(outs, send, recv, True):
            cp.wait_recv()
        for cp in copies(outs, send, recv, False):
            cp.wait_send()

    return _Phase(bufs, [], True, len(bufs), start, finish)


def _allgather_small(buf, *, name):
    rows = buf.shape[0]

    def body(x_ref, out_ref, send_sems, recv_sems, local_sem):
        x, y, c, chips = _place()
        sibling = (x, y, 1 - c)

        def slot(px, py, pc):
            return out_ref.at[4 * px + 2 * py + pc]

        def copy(k, block, to, src=None):
            return _remote(slot(*block) if src is None else src, slot(*block), send_sems.at[k], recv_sems.at[k], to)

        mine = pltpu.make_async_copy(x_ref, slot(x, y, c), local_sem)
        mine.start()
        first = [copy(0, (x, y, c), sibling, src=x_ref)]
        first += [copy(1 + k, (x, y, c), (*chip, c), src=x_ref) for k, chip in enumerate(chips)]
        for cp in first:
            cp.start()
        passed = [copy(4 + k, (*chip, c), sibling) for k, chip in enumerate(chips)]
        for k, chip in enumerate(chips):
            copy(1 + k, (*chip, c), (x, y, c)).wait_recv()
            passed[k].start()
        copy(0, (x, y, 1 - c), (x, y, c)).wait_recv()
        for k, chip in enumerate(chips):
            copy(4 + k, (*chip, 1 - c), (x, y, c)).wait_recv()
        for cp in first + passed:
            cp.wait_send()
        mine.wait()

    return pl.pallas_call(
        body, name=name, out_shape=jax.ShapeDtypeStruct((N_DEV, rows, LANES), buf.dtype),
        in_specs=[pl.BlockSpec(memory_space=pltpu.VMEM)], out_specs=pl.BlockSpec(memory_space=pltpu.VMEM),
        scratch_shapes=[pltpu.SemaphoreType.DMA((7,)), pltpu.SemaphoreType.DMA((7,)), pltpu.SemaphoreType.DMA],
    )(buf)


def _adamw_math(w, g, m, v):
    m = ADAM_B1 * m + (1.0 - ADAM_B1) * g
    v = ADAM_B2 * v + (1.0 - ADAM_B2) * (g * g)
    m_hat = m / (1.0 - ADAM_B1 ** ADAM_STEP)
    v_hat = v / (1.0 - ADAM_B2 ** ADAM_STEP)
    delta = -ADAM_LR * (m_hat / (jnp.sqrt(v_hat) + ADAM_EPS) + ADAM_WD * w)
    return delta, m, v


def _adamw(w, g, m, v, *, name):
    rows, cols = w.shape
    tr, tc = _rc_tile(rows, cols)

    def body(w_ref, g_ref, m_ref, v_ref, d_ref, mo_ref, vo_ref):
        d_ref[...], mo_ref[...], vo_ref[...] = _adamw_math(w_ref[...], g_ref[...], m_ref[...], v_ref[...])

    blk = pl.BlockSpec((tr, tc), lambda i, j: (i, j))
    shape = jax.ShapeDtypeStruct((rows, cols), F32)
    return pl.pallas_call(
        body, name=name, out_shape=(shape, shape, shape), grid=(rows // tr, cols // tc),
        in_specs=[blk] * 4, out_specs=(blk, blk, blk), compiler_params=_params("parallel", "parallel"),
    )(w, g, m, v)


def _adamw_small(gathered, w, m, v, *, name):
    nd = gathered.shape[0]

    def body(gs_ref, w_ref, m_ref, v_ref, g_ref, d_ref, mo_ref, vo_ref):
        g = gs_ref[0]
        for k in range(1, nd):
            g = g + gs_ref[k]
        g_ref[...] = g
        d_ref[...], mo_ref[...], vo_ref[...] = _adamw_math(w_ref[...], g, m_ref[...], v_ref[...])

    shape = jax.ShapeDtypeStruct(w.shape, F32)
    return pl.pallas_call(body, name=name, out_shape=(shape, shape, shape, shape),
                          compiler_params=pltpu.CompilerParams(vmem_limit_bytes=VMEM_LIMIT_BYTES))(gathered, w, m, v)


def _pack(parts):
    flat = jnp.concatenate([p.reshape(-1).astype(F32) for p in parts])
    rows = -(-flat.shape[0] // (8 * LANES)) * 8
    return jnp.pad(flat, (0, rows * LANES - flat.shape[0])).reshape(rows, LANES)


def _unpack(buf, shapes):
    flat = buf.reshape(-1)
    out, pos = [], 0
    for shp in shapes:
        size = int(np.prod(shp))
        out.append(flat[pos:pos + size].reshape(shp))
        pos += size
    return out


def _shard_rows(g, lo, hi):
    rs = g.shape[1]
    pieces = []
    for j in range(g.shape[0]):
        a, b = max(lo, j * rs), min(hi, (j + 1) * rs)
        if a < b:
            pieces.append(g[j, a - j * rs:b - j * rs])
    return pieces


def _rows_from_segments(segments, lo, hi):
    pieces = []
    for first, last, src, at in segments:
        a, b = max(lo, first), min(hi, last)
        if a < b:
            pieces.append(src[at + a - first:at + b - first])
    return jnp.concatenate(pieces, axis=0)


def kernel(x, norm_mix_g, w_in, b_f, gmlp_ln_g, gmlp_ln_b, w_s, b_s, attn_out_g, gmlp_out_g, w_out, norm_ffn_g, w_ff1, w_ff2, norm_final_g, loss_target, m_norm_mix_g, m_w_in, m_b_f, m_gmlp_ln_g, m_gmlp_ln_b, m_w_s, m_b_s, m_attn_out_g, m_gmlp_out_g, m_w_out, m_norm_ffn_g, m_w_ff1, m_w_ff2, m_norm_final_g, v_norm_mix_g, v_w_in, v_b_f, v_gmlp_ln_g, v_gmlp_ln_b, v_w_s, v_b_s, v_attn_out_g, v_gmlp_out_g, v_w_out, v_norm_ffn_g, v_w_ff1, v_w_ff2, v_norm_final_g):
    seq, d_model = x.shape[1], x.shape[2]
    d_attn = d_model // 2
    n_heads = d_attn // HEAD_DIM
    qkv = 3 * d_attn
    shard_cols = w_in.shape[2]
    assert N_CHIPS * shard_cols == qkv + n_heads + 2 * d_attn
    xs = x.reshape(seq, d_model)
    target = loss_target.reshape(seq, d_model)

    place = jnp.stack([2 * lax.axis_index("x") + lax.axis_index("y"), lax.axis_index("c")]).astype(jnp.int32)
    names = ["w_in", "w_out", "w_ff1", "w_ff2"]
    wt_in, mt_in, vt_in = w_in[0].T, m_w_in[0].T, v_w_in[0].T
    b_in, b_out, b_ff1, b_ff2 = [_cast_into_slot(w, place, name="cast_" + nm)
                                 for w, nm in zip((wt_in, w_out[0], w_ff1[0], w_ff2[0]), names)]
    (g_in,) = _exchange([_then(_gather_over_ici([b_in]), _gather_over_d2d([b_in]))], name="allgather_w_in")
    n_cols = N_CHIPS * shard_cols
    wt_main = jnp.concatenate(_shard_rows(g_in, 0, qkv) + _shard_rows(g_in, qkv + n_heads, n_cols), axis=0)
    wt_f = jnp.pad(jnp.concatenate(_shard_rows(g_in, qkv, qkv + n_heads), axis=0), ((0, LANES - n_heads), (0, 0)))
    b_f_pad = jnp.pad(b_f, ((0, 0), (0, LANES - n_heads)))
    b_col = b_s[0].T

    h = _rmsnorm_fwd(xs, norm_mix_g, name="norm_mix")
    z, (b_out,) = _matmul(h, wt_main, name="in_proj", out_dtype=BF16, trans_b=True, phases=[_gather_over_ici([b_out])])
    zb, f_cum = _forget_fwd(h, wt_f, b_f_pad, name="forget_fwd")
    f_row = f_cum[:, :n_heads].T[:, None, :]
    (o, lse2), (b_ff1, b_out) = _attn_fwd(z, f_row, n_heads, name="attn_fwd",
                                          phases=[_gather_over_ici([b_ff1]), _gather_over_d2d([b_out])])
    merged = _mix_fwd(z, o, gmlp_ln_g, gmlp_ln_b, w_s[0], b_col, attn_out_g, gmlp_out_g, n_heads, name="mix_fwd")
    w_out_full = b_out.reshape(2 * d_attn, d_model)
    x1, (b_ff1,) = _matmul(merged, w_out_full, name="out_proj", out_dtype=F32, residual=xs,
                           phases=[_gather_over_d2d([b_ff1])])
    h2 = _rmsnorm_fwd(x1, norm_ffn_g, name="norm_ffn")
    a, (b_ff2,) = _matmul(h2, b_ff1, name="ff1", out_dtype=BF16, relu=True, b_sharded=True,
                          phases=[_gather_over_ici([b_ff2])])
    (b_ff2,) = _exchange([_gather_over_d2d([b_ff2])], name="allgather_w_ff2_d2d")
    w_ff2_full = b_ff2.reshape(N_CHIPS * b_ff2.shape[1], d_model)
    x2, _ = _matmul(a, w_ff2_full, name="ff2", out_dtype=F32, square_lhs=True, residual=x1)
    dx2, dg_final, loss = _loss_and_final_bwd(x2, target, norm_final_g.reshape(1, d_model), name="loss_head")

    def pair_sum(g, r, nm):
        return _add_halves(g, r, place, name="grads_pair_sum_" + nm)

    def chip_sum(p, q, nm):
        return _sum_chips(p, q, place, name="grads_chip_sum_" + nm)

    dw_ff2, _ = _matmul(a, dx2, name="ff2_dw", out_dtype=BF16, trans_a=True, square_lhs=True)
    dw_ff2 = dw_ff2.reshape(N_CHIPS, -1, d_model)
    da, (r_ff2,) = _matmul(dx2, w_ff2_full, name="ff2_dlhs", out_dtype=BF16, trans_b=True, scale2_by=a,
                           phases=[_swap_halves([dw_ff2])])
    ps_ff2 = pair_sum(dw_ff2, r_ff2, "w_ff2")
    dh2, (q_ff2,) = _matmul(da, b_ff1, name="ff1_dlhs", out_dtype=F32, trans_b=True, b_sharded=True,
                            phases=[_send_partials([ps_ff2])])
    g_ff2 = chip_sum(ps_ff2, q_ff2, "w_ff2")
    dw_ff1, (g_ff2,) = _matmul(h2, da, name="ff1_dw", out_dtype=BF16, trans_a=True, out_sharded=True,
                               phases=[_join_halves([g_ff2])])
    (dx1, dg_ffn), _ = _rmsnorm_bwd(dh2, x1, dx2, norm_ffn_g, name="norm_ffn_bwd")
    dw_out, _ = _matmul(merged, dx1, name="out_proj_dw", out_dtype=BF16, trans_a=True)
    dw_out = dw_out.reshape(N_CHIPS, -1, d_model)
    d_merged, (r_ff1, r_out) = _matmul(dx1, w_out_full, name="out_proj_dlhs", out_dtype=F32, trans_b=True,
                                       phases=[_swap_halves([dw_ff1, dw_out])])
    ps_ff1, ps_out = pair_sum(dw_ff1, r_ff1, "w_ff1"), pair_sum(dw_out, r_out, "w_out")
    d_o, dzu, dzv, dw_s, db_col, dlg, dlb, dag, dgg = _mix_bwd(
        z, o, d_merged, gmlp_ln_g, gmlp_ln_b, w_s[0], b_col, attn_out_g, gmlp_out_g, n_heads, name="mix_bwd")
    (dq, dk, dv, d_f_key, d_f_query), (q_ff1, q_out) = _attn_bwd(
        z, o, d_o, lse2, f_row, n_heads, name="attn_bwd", phases=[_send_partials([ps_ff1, ps_out])])
    g_ff1, g_out = chip_sum(ps_ff1, q_ff1, "w_ff1"), chip_sum(ps_out, q_out, "w_out")
    d_f = d_f_key.reshape(n_heads, seq) + d_f_query.reshape(n_heads, seq)
    d_f_pad = jnp.pad(d_f.T, ((0, 0), (0, LANES - n_heads)))
    dzf, db_f = _forget_bwd(d_f_pad, zb, name="forget_bwd")
    dz = jnp.concatenate([dq, dk, dv, dzu, dzv], axis=1)
    dwt_main, (g_ff1, g_out) = _matmul(dz, h, name="in_proj_dw", out_dtype=BF16, trans_a=True,
                                       phases=[_join_halves([g_ff1, g_out])])
    dwt_f, _ = _matmul(dzf, h, name="gate_dw", out_dtype=BF16, trans_a=True)
    segments = [(0, qkv, dwt_main, 0), (qkv, qkv + n_heads, dwt_f, 0), (qkv + n_heads, n_cols, dwt_main, qkv)]
    dw_in = jnp.stack([_rows_from_segments(segments, j * shard_cols, (j + 1) * shard_cols) for j in range(N_CHIPS)])
    dh_gate, (r_in,) = _matmul(dzf, wt_f, name="gate_dlhs", out_dtype=F32, phases=[_swap_halves([dw_in])])
    ps_in = pair_sum(dw_in, r_in, "w_in")
    dh, (q_in,) = _matmul(dz, wt_main, name="in_proj_dlhs", out_dtype=F32, residual=dh_gate, tk=2560,
                          phases=[_send_partials([ps_in])])
    g_in_sum = chip_sum(ps_in, q_in, "w_in")
    (grad_x, dg_mix), _ = _rmsnorm_bwd(dh, xs, dx1, norm_mix_g, name="norm_mix_bwd")
    (g_in_sum,) = _exchange([_join_halves([g_in_sum])], name="grads_join_w_in")

    big = {}
    for nm, g, w, m, v in zip(names, (g_in_sum, g_out, g_ff1, g_ff2), (wt_in, w_out[0], w_ff1[0], w_ff2[0]),
                              (mt_in, m_w_out[0], m_w_ff1[0], m_w_ff2[0]), (vt_in, v_w_out[0], v_w_ff1[0], v_w_ff2[0])):
        d, mo, vo = _adamw(w, g, m, v, name="adamw_" + nm)
        big[nm] = tuple((t.T if nm == "w_in" else t)[None] for t in (g, d, mo, vo))

    small_w = [norm_mix_g, b_f, gmlp_ln_g, gmlp_ln_b, w_s, b_s, attn_out_g, gmlp_out_g, norm_ffn_g, norm_final_g]
    small_m = [m_norm_mix_g, m_b_f, m_gmlp_ln_g, m_gmlp_ln_b, m_w_s, m_b_s, m_attn_out_g, m_gmlp_out_g, m_norm_ffn_g, m_norm_final_g]
    small_v = [v_norm_mix_g, v_b_f, v_gmlp_ln_g, v_gmlp_ln_b, v_w_s, v_b_s, v_attn_out_g, v_gmlp_out_g, v_norm_ffn_g, v_norm_final_g]
    small_g = [dg_mix, db_f[:, :n_heads], dlg, dlb, dw_s, db_col.T, dag, dgg, dg_ffn, dg_final]
    shapes = [w.shape for w in small_w]
    gathered = _allgather_small(_pack(small_g), name="allgather_small_grads")
    packed = _adamw_small(gathered, _pack(small_w), _pack(small_m), _pack(small_v), name="adamw_small")
    sg, sd, sm, sv = (_unpack(p, shapes) for p in packed)
    small_names = ["norm_mix_g", "b_f", "gmlp_ln_g", "gmlp_ln_b", "w_s", "b_s", "attn_out_g", "gmlp_out_g", "norm_ffn_g", "norm_final_g"]
    small = {nm: (sg[i], sd[i], sm[i], sv[i]) for i, nm in enumerate(small_names)}

    order = ["norm_mix_g", "w_in", "b_f", "gmlp_ln_g", "gmlp_ln_b", "w_s", "b_s", "attn_out_g", "gmlp_out_g", "w_out",
             "norm_ffn_g", "w_ff1", "w_ff2", "norm_final_g"]
    result = {**small, **big}
    total_loss = lax.psum(loss[0, 0], ("x", "y", "c"))
    outs = [total_loss, grad_x.reshape(x.shape)]
    for part in range(4):
        outs += [result[nm][part] for nm in order]
    return tuple(outs)
```

```python
import functools
import math

import numpy as np
import jax
import jax.numpy as jnp
from jax import lax
from jax.experimental import pallas as pl
from jax.experimental.pallas import tpu as pltpu

HEAD_DIM = 128
CHUNK = 128
EPS = 1e-6
LANES = 128
N_CHIPS = 4
N_DEV = 8
VMEM_LIMIT_BYTES = 56 * 1024 * 1024

ADAM_LR = 0.001
ADAM_B1 = 0.9
ADAM_B2 = 0.999
ADAM_EPS = 1e-08
ADAM_WD = 0.01
ADAM_STEP = 10

BF16 = jnp.bfloat16
F32 = jnp.float32
MESH = pl.DeviceIdType.MESH
ANY = pl.BlockSpec(memory_space=pl.ANY)
NEG_BIG = -1e30


def _params(*sem):
    return pltpu.CompilerParams(dimension_semantics=tuple(sem), vmem_limit_bytes=VMEM_LIMIT_BYTES)


def _tile(n, pref, unit):
    t = (min(pref, n) // unit) * unit
    while t >= unit:
        if n % t == 0:
            return t
        t -= unit
    return n


def _rc_tile(rows, cols, pref_rows=256, pref_cols=256):
    if rows % 16 == 0:
        return _tile(rows, pref_rows, 16), cols
    return rows, _tile(cols, pref_cols, LANES)


class _Phase:
    def __init__(self, arrays, out_shapes, in_place, n_sems, start, finish):
        self.arrays, self.out_shapes, self.in_place = list(arrays), list(out_shapes), in_place
        self.n_sems, self.start, self.finish = n_sems, start, finish

    @property
    def n_out(self):
        return len(self.arrays) if self.in_place else len(self.out_shapes)


def _run_phases(phases, steps, comm_in, comm_out, send_sems, recv_sems):
    at_in = at_out = at_sem = 0
    for ph in phases:
        for step in steps:
            getattr(ph, step)(comm_in[at_in:at_in + len(ph.arrays)], comm_out[at_out:at_out + ph.n_out],
                              lambda i, base=at_sem: send_sems.at[base + i], lambda i, base=at_sem: recv_sems.at[base + i])
        at_in, at_out, at_sem = at_in + len(ph.arrays), at_out + ph.n_out, at_sem + ph.n_sems


def _call(body, *, name, grid, in_specs, out_specs, out_shape, operands, semantics, scratch_shapes=(),
          n_prefetch=0, phases=()):
    in_specs, out_specs, out_shape = list(in_specs), list(out_specs), list(out_shape)
    scratch_shapes = list(scratch_shapes)
    n_in, n_out, n_scr = len(operands) - n_prefetch, len(out_shape), len(scratch_shapes)
    comm_in = [a for ph in phases for a in ph.arrays]
    comm_out = [jax.ShapeDtypeStruct(s.shape, s.dtype) for ph in phases
                for s in (ph.arrays if ph.in_place else ph.out_shapes)]
    aliases, at_in, at_out = {}, n_prefetch + n_in, n_out
    for ph in phases:
        if ph.in_place:
            aliases.update({at_in + r: at_out + r for r in range(len(ph.arrays))})
        at_in, at_out = at_in + len(ph.arrays), at_out + ph.n_out
    n_sems = sum(ph.n_sems for ph in phases)

    def hosted(*refs):
        pre, rest = refs[:n_prefetch], refs[n_prefetch:]
        ins, rest = rest[:n_in], rest[n_in:]
        cin, rest = rest[:len(comm_in)], rest[len(comm_in):]
        outs, rest = rest[:n_out], rest[n_out:]
        cout, rest = rest[:len(comm_out)], rest[len(comm_out):]
        scr = rest[:n_scr]
        if phases:
            send_sems, recv_sems = rest[n_scr:]
            ids = [pl.program_id(ax) for ax in range(len(grid))]
            first = functools.reduce(jnp.logical_and, [i == 0 for i in ids])
            last = functools.reduce(jnp.logical_and, [i == g - 1 for i, g in zip(ids, grid)])

            @pl.when(first)
            def _():
                _run_phases(phases, ("start",), cin, cout, send_sems, recv_sems)

        body(*pre, *ins, *outs, *scr)
        if phases:
            @pl.when(last)
            def _():
                _run_phases(phases, ("finish",), cin, cout, send_sems, recv_sems)

    all_in = in_specs + [ANY] * len(comm_in)
    all_out = out_specs + [ANY] * len(comm_out)
    all_scr = scratch_shapes + ([pltpu.SemaphoreType.DMA((n_sems,)), pltpu.SemaphoreType.DMA((n_sems,))] if phases else [])
    if phases:
        semantics = ("arbitrary",) * len(grid)
    kwargs = dict(name=name, out_shape=tuple(out_shape + comm_out), compiler_params=_params(*semantics),
                  input_output_aliases=aliases)
    if n_prefetch:
        kwargs["grid_spec"] = pltpu.PrefetchScalarGridSpec(
            num_scalar_prefetch=n_prefetch, grid=grid, in_specs=all_in, out_specs=tuple(all_out), scratch_shapes=all_scr)
    else:
        kwargs.update(grid=grid, in_specs=all_in, out_specs=tuple(all_out), scratch_shapes=all_scr)
    res = pl.pallas_call(hosted, **kwargs)(*operands, *comm_in)
    return tuple(res[:n_out]), tuple(res[n_out:])


def _only(results):
    outs, comm = results
    return outs[0] if len(outs) == 1 else outs, comm


def _matmul(a, b, *, name, out_dtype, trans_a=False, trans_b=False, tm=1024, tn=1024, tk=2048,
            square_lhs=False, relu=False, residual=None, scale2_by=None,
            b_sharded=False, out_sharded=False, phases=()):
    m, k = (a.shape[1], a.shape[0]) if trans_a else a.shape
    if b_sharded:
        if trans_b:
            n, ks = b.shape[1], b.shape[2]
            assert N_CHIPS * ks == k
        else:
            ns = b.shape[2]
            n = N_CHIPS * ns
            assert b.shape[1] == k
    else:
        n = b.shape[0] if trans_b else b.shape[1]
        assert (b.shape[1] if trans_b else b.shape[0]) == k
    tm = _tile(m, tm, 128)
    tn = _tile(n // N_CHIPS if (out_sharded or (b_sharded and not trans_b)) else n, tn, 128)
    tk = _tile(k // N_CHIPS if (b_sharded and trans_b) else k, tk, 128)
    nk = k // tk

    if trans_a:
        a_spec = pl.BlockSpec((tk, tm), lambda i, j, kk: (kk, i))
    else:
        a_spec = pl.BlockSpec((tm, tk), lambda i, j, kk: (i, kk))
    if b_sharded and trans_b:
        per = ks // tk
        assert per * tk == ks
        b_spec = pl.BlockSpec((None, tn, tk), lambda i, j, kk: (kk // per, j, kk % per))
    elif b_sharded:
        per = ns // tn
        assert per * tn == ns
        b_spec = pl.BlockSpec((None, tk, tn), lambda i, j, kk: (j // per, kk, j % per))
    elif trans_b:
        b_spec = pl.BlockSpec((tn, tk), lambda i, j, kk: (j, kk))
    else:
        b_spec = pl.BlockSpec((tk, tn), lambda i, j, kk: (kk, j))
    if out_sharded:
        ns_out = n // N_CHIPS
        per_o = ns_out // tn
        assert per_o * tn == ns_out
        out_shape = jax.ShapeDtypeStruct((N_CHIPS, m, ns_out), out_dtype)
        o_spec = pl.BlockSpec((None, tm, tn), lambda i, j, kk: (j // per_o, i, j % per_o))
    else:
        out_shape = jax.ShapeDtypeStruct((m, n), out_dtype)
        o_spec = pl.BlockSpec((tm, tn), lambda i, j, kk: (i, j))
    mn_spec = pl.BlockSpec((tm, tn), lambda i, j, kk: (i, j))

    operands, in_specs = [a, b], [a_spec, b_spec]
    if scale2_by is not None:
        operands.append(scale2_by)
        in_specs.append(mn_spec)
    if residual is not None:
        operands.append(residual)
        in_specs.append(mn_spec)
    dims = (((0 if trans_a else 1,), (1 if trans_b else 0,)), ((), ()))

    def body(*refs):
        a_ref, b_ref = refs[0], refs[1]
        pos = 2
        scale_ref = res_ref = None
        if scale2_by is not None:
            scale_ref = refs[pos]
            pos += 1
        if residual is not None:
            res_ref = refs[pos]
            pos += 1
        o_ref = refs[pos]
        kk = pl.program_id(2)

        av = a_ref[...]
        if square_lhs:
            av = av.astype(F32)
            av = av * av
        part = lax.dot_general(av.astype(BF16), b_ref[...].astype(BF16), dims, preferred_element_type=F32)

        def finish(r):
            if relu:
                r = jnp.maximum(r, 0.0)
            if scale_ref is not None:
                r = r * (2.0 * scale_ref[...].astype(F32))
            if res_ref is not None:
                r = r + res_ref[...].astype(F32)
            o_ref[...] = r.astype(out_dtype)

        if nk == 1:
            finish(part)
        else:
            acc_ref = refs[pos + 1]

            @pl.when(kk == 0)
            def _():
                acc_ref[...] = part

            @pl.when(jnp.logical_and(kk > 0, kk < nk - 1))
            def _():
                acc_ref[...] += part

            @pl.when(kk == nk - 1)
            def _():
                finish(acc_ref[...] + part)

    return _only(_call(
        body, name=name, out_shape=[out_shape], grid=(m // tm, n // tn, nk),
        in_specs=in_specs, out_specs=[o_spec], operands=operands,
        scratch_shapes=[pltpu.VMEM((tm, tn), F32)] if nk > 1 else [],
        semantics=("parallel", "parallel", "arbitrary"), phases=phases))


def _rmsnorm_fwd(x, g, *, name, tr=512):
    s, d = x.shape
    tr = _tile(s, tr, 8)

    def body(x_ref, g_ref, o_ref):
        xv = x_ref[...]
        r = lax.rsqrt(jnp.mean(xv * xv, axis=-1, keepdims=True) + EPS)
        o_ref[...] = ((xv * r) * g_ref[...]).astype(BF16)

    return pl.pallas_call(
        body, name=name, out_shape=jax.ShapeDtypeStruct((s, d), BF16), grid=(s // tr,),
        in_specs=[pl.BlockSpec((tr, d), lambda i: (i, 0)), pl.BlockSpec((1, d), lambda i: (0, 0))],
        out_specs=pl.BlockSpec((tr, d), lambda i: (i, 0)),
        compiler_params=_params("parallel"),
    )(x, g)


def _rms_bwd_rows(dy, xv, g):
    d = xv.shape[-1]
    r = lax.rsqrt(jnp.mean(xv * xv, axis=-1, keepdims=True) + EPS)
    gdy = dy * g
    dot = jnp.sum(gdy * xv, axis=-1, keepdims=True)
    dx = gdy * r - xv * (r * r * r) * (dot / d)
    return dx, dy * (xv * r)


def _rmsnorm_bwd(dy, x, res, g, *, name, tr=256, phases=()):
    s, d = x.shape
    tr = _tile(s, tr, 8)

    def body(dy_ref, x_ref, res_ref, g_ref, dx_ref, dg_ref):
        @pl.when(pl.program_id(0) == 0)
        def _():
            dg_ref[...] = jnp.zeros_like(dg_ref)

        dx, dg_rows = _rms_bwd_rows(dy_ref[...].astype(F32), x_ref[...], g_ref[...])
        dx_ref[...] = res_ref[...] + dx
        dg_ref[...] += jnp.sum(dg_rows, axis=0, keepdims=True)

    row = pl.BlockSpec((tr, d), lambda i: (i, 0))
    vec = pl.BlockSpec((1, d), lambda i: (0, 0))
    return _call(
        body, name=name, out_shape=[jax.ShapeDtypeStruct((s, d), F32), jax.ShapeDtypeStruct((1, d), F32)],
        grid=(s // tr,), in_specs=[row, row, row, vec], out_specs=[row, vec], operands=[dy, x, res, g],
        semantics=("arbitrary",), phases=phases)


def _loss_and_final_bwd(x2, target, g, *, name, tr=256):
    s, d = x2.shape
    tr = _tile(s, tr, 8)

    def body(x_ref, t_ref, g_ref, dx_ref, dg_ref, loss_ref):
        @pl.when(pl.program_id(0) == 0)
        def _():
            dg_ref[...] = jnp.zeros_like(dg_ref)
            loss_ref[...] = jnp.zeros_like(loss_ref)

        xv, gv = x_ref[...], g_ref[...]
        r = lax.rsqrt(jnp.mean(xv * xv, axis=-1, keepdims=True) + EPS)
        err = (xv * r) * gv - t_ref[...]
        row_loss = jnp.mean(err * err, axis=-1, keepdims=True)
        loss_ref[...] += 0.5 * jnp.sum(row_loss, axis=0, keepdims=True)
        dx, dg_rows = _rms_bwd_rows(err / d, xv, gv)
        dx_ref[...] = dx
        dg_ref[...] += jnp.sum(dg_rows, axis=0, keepdims=True)

    row = pl.BlockSpec((tr, d), lambda i: (i, 0))
    vec = pl.BlockSpec((1, d), lambda i: (0, 0))
    one = pl.BlockSpec((1, 1), lambda i: (0, 0))
    return pl.pallas_call(
        body, name=name,
        out_shape=(jax.ShapeDtypeStruct((s, d), F32), jax.ShapeDtypeStruct((1, d), F32),
                   jax.ShapeDtypeStruct((1, 1), F32)),
        grid=(s // tr,), in_specs=[row, row, vec], out_specs=(row, vec, one),
        compiler_params=_params("arbitrary"),
    )(x2, target, g)


def _tri_ones(n, lower):
    r = lax.broadcasted_iota(jnp.int32, (n, n), 0)
    c = lax.broadcasted_iota(jnp.int32, (n, n), 1)
    return jnp.where((c <= r) if lower else (c >= r), 1.0, 0.0).astype(F32)


def _forget_fwd(h, w_f, b_f, *, name, tr=256):
    s, d = h.shape
    tr = _tile(s, tr, 8)

    def body(h_ref, w_ref, b_ref, zb_ref, f_ref, carry):
        @pl.when(pl.program_id(0) == 0)
        def _():
            carry[...] = jnp.zeros_like(carry)

        zb = lax.dot_general(h_ref[...], w_ref[...], (((1,), (1,)), ((), ())), preferred_element_type=F32) + b_ref[...]
        zb_ref[...] = zb
        log_f = jnp.minimum(zb, 0.0) - jnp.log(1.0 + jnp.exp(-jnp.abs(zb)))
        run = jnp.dot(_tri_ones(tr, True), log_f, preferred_element_type=F32,
                      precision=lax.Precision.HIGHEST) + carry[...]
        f_ref[...] = run
        carry[...] = run[tr - 1:tr, :]

    row = pl.BlockSpec((tr, LANES), lambda i: (i, 0))
    return pl.pallas_call(
        body, name=name,
        out_shape=(jax.ShapeDtypeStruct((s, LANES), F32), jax.ShapeDtypeStruct((s, LANES), F32)),
        grid=(s // tr,),
        in_specs=[pl.BlockSpec((tr, d), lambda i: (i, 0)), pl.BlockSpec((LANES, d), lambda i: (0, 0)),
                  pl.BlockSpec((1, LANES), lambda i: (0, 0))],
        out_specs=(row, row), scratch_shapes=[pltpu.VMEM((1, LANES), F32)],
        compiler_params=_params("arbitrary"),
    )(h, w_f, b_f)


def _forget_bwd(d_f, zb, *, name, tr=256):
    s = zb.shape[0]
    tr = _tile(s, tr, 8)
    nb = s // tr

    def body(df_ref, zb_ref, dz_ref, db_ref, carry):
        @pl.when(pl.program_id(0) == 0)
        def _():
            carry[...] = jnp.zeros_like(carry)
            db_ref[...] = jnp.zeros_like(db_ref)

        run = jnp.dot(_tri_ones(tr, False), df_ref[...], preferred_element_type=F32,
                      precision=lax.Precision.HIGHEST) + carry[...]
        carry[...] = run[0:1, :]
        dz = run / (1.0 + jnp.exp(zb_ref[...]))
        dz_ref[...] = dz.astype(BF16)
        db_ref[...] += jnp.sum(dz, axis=0, keepdims=True)

    row = pl.BlockSpec((tr, LANES), lambda i: (nb - 1 - i, 0))
    return pl.pallas_call(
        body, name=name,
        out_shape=(jax.ShapeDtypeStruct((s, LANES), BF16), jax.ShapeDtypeStruct((1, LANES), F32)),
        grid=(nb,), in_specs=[row, row], out_specs=(row, pl.BlockSpec((1, LANES), lambda i: (0, 0))),
        scratch_shapes=[pltpu.VMEM((1, LANES), F32)],
        compiler_params=_params("arbitrary"),
    )(d_f, zb)


def _pairs(nblk, by_kv):
    if by_kv:
        pr = [(i, j) for j in range(nblk) for i in range(j, nblk)]
    else:
        pr = [(i, j) for i in range(nblk) for j in range(i + 1)]
    return (jnp.asarray(np.array([p[0] for p in pr], np.int32)), jnp.asarray(np.array([p[1] for p in pr], np.int32)))


def _causal_mask(t):
    r = lax.broadcasted_iota(jnp.int32, (t, t), 0)
    c = lax.broadcasted_iota(jnp.int32, (t, t), 1)
    return c <= r


LOG2E = math.log2(math.e)
QK_TO_LOG2 = LOG2E / math.sqrt(HEAD_DIM)


def _attn_logits2(q, k, fk_row):
    sc = lax.dot_general(q, k, (((1,), (1,)), ((), ())), preferred_element_type=F32)
    return sc * QK_TO_LOG2 - fk_row * LOG2E


def _attn_fwd(z, f_row, n_heads, *, name, tb=1024, phases=()):
    s = z.shape[0]
    tb = _tile(s, tb, 128)
    nblk = s // tb
    rep = tb // LANES
    qi, kj = _pairs(nblk, by_kv=False)

    def body(qi_ref, kj_ref, q_ref, k_ref, v_ref, fk_ref, o_ref, lse_ref, m_sc, l_sc, acc_sc):
        p = pl.program_id(1)
        i, j = qi_ref[p], kj_ref[p]

        @pl.when(j == 0)
        def _():
            m_sc[...] = jnp.full_like(m_sc, NEG_BIG)
            l_sc[...] = jnp.zeros_like(l_sc)
            acc_sc[...] = jnp.zeros_like(acc_sc)

        def update(masked):
            s2 = _attn_logits2(q_ref[...], k_ref[...], fk_ref[...])
            if masked:
                s2 = jnp.where(_causal_mask(tb), s2, NEG_BIG)
            m_old = m_sc[...]
            m_new = jnp.maximum(m_old, jnp.max(s2, axis=-1, keepdims=True))
            alpha = jnp.exp2(m_old - m_new)
            pv = jnp.exp2(s2 - jnp.tile(m_new, (1, rep)))
            l_sc[...] = alpha * l_sc[...] + jnp.sum(pv, axis=-1, keepdims=True)
            acc_sc[...] = alpha * acc_sc[...] + jnp.dot(pv.astype(BF16), v_ref[...], preferred_element_type=F32)
            m_sc[...] = m_new

        @pl.when(j < i)
        def _():
            update(False)

        @pl.when(j == i)
        def _():
            update(True)
            o_ref[...] = (acc_sc[...] / l_sc[...]).astype(BF16)
            lse_ref[...] = m_sc[...] + jnp.log2(l_sc[...])

    h = n_heads
    return _call(
        body, name=name, n_prefetch=2, grid=(h, int(qi.shape[0])),
        in_specs=[
            pl.BlockSpec((tb, HEAD_DIM), lambda hh, p, qi_r, kj_r: (qi_r[p], hh)),
            pl.BlockSpec((tb, HEAD_DIM), lambda hh, p, qi_r, kj_r: (kj_r[p], h + hh)),
            pl.BlockSpec((tb, HEAD_DIM), lambda hh, p, qi_r, kj_r: (kj_r[p], 2 * h + hh)),
            pl.BlockSpec((None, 1, tb), lambda hh, p, qi_r, kj_r: (hh, 0, kj_r[p])),
        ],
        out_specs=[
            pl.BlockSpec((tb, HEAD_DIM), lambda hh, p, qi_r, kj_r: (qi_r[p], hh)),
            pl.BlockSpec((None, tb, LANES), lambda hh, p, qi_r, kj_r: (hh, qi_r[p], 0)),
        ],
        scratch_shapes=[pltpu.VMEM((tb, LANES), F32), pltpu.VMEM((tb, LANES), F32), pltpu.VMEM((tb, HEAD_DIM), F32)],
        out_shape=[jax.ShapeDtypeStruct((s, h * HEAD_DIM), BF16), jax.ShapeDtypeStruct((h, s, LANES), F32)],
        operands=[qi, kj, z, z, z, f_row], semantics=("parallel", "arbitrary"), phases=phases)


def _attn_bwd(z, o, d_o, lse2, f_row, n_heads, *, name, tb=1024, phases=()):
    s = z.shape[0]
    tb = _tile(s, tb, 128)
    nblk = s // tb
    rep = tb // LANES
    qi, kj = _pairs(nblk, by_kv=True)
    n_pairs = int(qi.shape[0])
    scale = 1.0 / math.sqrt(HEAD_DIM)
    h = n_heads

    def body(qi_ref, kj_ref, q_ref, k_ref, v_ref, o_ref, do_ref, lse_ref, fk_ref,
             dq_ref, dk_ref, dv_ref, df_ref, dfq_ref, dq_sc, dk_sc, dv_sc, df_sc, dfq_sc):
        p = pl.program_id(1)
        i, j = qi_ref[p], kj_ref[p]

        @pl.when(p == 0)
        def _():
            dq_sc[...] = jnp.zeros_like(dq_sc)
            dfq_sc[...] = jnp.zeros_like(dfq_sc)

        @pl.when(i == j)
        def _():
            dk_sc[...] = jnp.zeros_like(dk_sc)
            dv_sc[...] = jnp.zeros_like(dv_sc)
            df_sc[...] = jnp.zeros_like(df_sc)

        def update(masked):
            q, k, v, do = q_ref[...], k_ref[...], v_ref[...], do_ref[...]
            delta = jnp.sum(do.astype(F32) * o_ref[...].astype(F32), axis=-1, keepdims=True)
            pv = jnp.exp2(_attn_logits2(q, k, fk_ref[...]) - jnp.tile(lse_ref[...], (1, rep)))
            if masked:
                pv = jnp.where(_causal_mask(tb), pv, 0.0)
            dp = lax.dot_general(do, v, (((1,), (1,)), ((), ())), preferred_element_type=F32)
            ds = pv * (dp - delta)
            ds_b = ds.astype(BF16)
            dv_sc[...] += lax.dot_general(pv.astype(BF16), do, (((0,), (0,)), ((), ())), preferred_element_type=F32)
            dk_sc[...] += lax.dot_general(ds_b, q, (((0,), (0,)), ((), ())), preferred_element_type=F32)
            rows = pl.ds(pl.multiple_of(i * tb, tb), tb)
            dq_sc[rows, :] += jnp.dot(ds_b, k, preferred_element_type=F32)
            df_sc[...] -= jnp.sum(ds, axis=0, keepdims=True)
            dfq_sc[rows, :] += jnp.broadcast_to(jnp.sum(ds, axis=1, keepdims=True), (tb, LANES))

        @pl.when(i > j)
        def _():
            update(False)

        @pl.when(i == j)
        def _():
            update(True)

        @pl.when(i == nblk - 1)
        def _():
            dk_ref[...] = (dk_sc[...] * scale).astype(BF16)
            dv_ref[...] = dv_sc[...].astype(BF16)
            df_ref[...] = df_sc[...]

        @pl.when(p == n_pairs - 1)
        def _():
            dq_ref[...] = (dq_sc[...] * scale).astype(BF16)
            dfq_ref[...] = jnp.transpose(dfq_sc[...])[0:1, :]

    qblk = lambda off: pl.BlockSpec((tb, HEAD_DIM), lambda hh, p, qi_r, kj_r: (qi_r[p], off + hh))
    kblk = lambda off: pl.BlockSpec((tb, HEAD_DIM), lambda hh, p, qi_r, kj_r: (kj_r[p], off + hh))
    qrep = pl.BlockSpec((None, tb, LANES), lambda hh, p, qi_r, kj_r: (hh, qi_r[p], 0))
    krow = pl.BlockSpec((None, 1, tb), lambda hh, p, qi_r, kj_r: (hh, 0, kj_r[p]))
    act = jax.ShapeDtypeStruct((s, h * HEAD_DIM), BF16)
    return _call(
        body, name=name, n_prefetch=2, grid=(h, n_pairs),
        in_specs=[qblk(0), kblk(h), kblk(2 * h), qblk(0), qblk(0), qrep, krow],
        out_specs=[
            pl.BlockSpec((s, HEAD_DIM), lambda hh, p, qi_r, kj_r: (0, hh)),
            kblk(0), kblk(0), krow,
            pl.BlockSpec((None, 1, s), lambda hh, p, qi_r, kj_r: (hh, 0, 0)),
        ],
        scratch_shapes=[pltpu.VMEM((s, HEAD_DIM), F32), pltpu.VMEM((tb, HEAD_DIM), F32),
                        pltpu.VMEM((tb, HEAD_DIM), F32), pltpu.VMEM((1, tb), F32), pltpu.VMEM((s, LANES), F32)],
        out_shape=[act, act, act, jax.ShapeDtypeStruct((h, 1, s), F32), jax.ShapeDtypeStruct((h, 1, s), F32)],
        operands=[qi, kj, z, z, z, o, d_o, lse2, f_row], semantics=("parallel", "arbitrary"), phases=phases)


GELU_C = math.sqrt(2.0 / math.pi)
GELU_A = 0.044715


def _gelu(x):
    return 0.5 * x * (1.0 + jnp.tanh(GELU_C * (x + GELU_A * (x * x * x))))


def _gelu_and_grad(x):
    t = jnp.tanh(GELU_C * (x + GELU_A * (x * x * x)))
    y = 0.5 * x * (1.0 + t)
    dy = 0.5 * (1.0 + t) + 0.5 * x * (1.0 - t * t) * (GELU_C * (1.0 + 3.0 * GELU_A * (x * x)))
    return y, dy


def _layernorm_parts(g):
    mu = jnp.mean(g, axis=-1, keepdims=True)
    xc = g - mu
    rs = lax.rsqrt(jnp.mean(xc * xc, axis=-1, keepdims=True) + EPS)
    return xc * rs, rs


def _spatial_mix(w_ref, bcol_ref, vv_b, n_heads, n_chunks):
    tril = _causal_mask(CHUNK)
    cols = []
    for hh in range(n_heads):
        wc = jnp.where(tril, w_ref[hh], 0.0).astype(BF16)
        lanes = slice(hh * HEAD_DIM, (hh + 1) * HEAD_DIM)
        rows = [jnp.dot(wc, vv_b[c * CHUNK:(c + 1) * CHUNK, lanes], preferred_element_type=F32)
                + bcol_ref[:, hh:hh + 1] for c in range(n_chunks)]
        cols.append(jnp.concatenate(rows, axis=0))
    return jnp.concatenate(cols, axis=1)


def _mix_fwd(z, o, ln_g, ln_b, w_s, b_col, attn_g, gm_g, n_heads, *, name, tr=256):
    s = z.shape[0]
    dg = n_heads * HEAD_DIM
    tr = _tile(s, tr, CHUNK)
    n_chunks = tr // CHUNK

    def body(zu_ref, zv_ref, o_ref, lg_ref, lb_ref, w_ref, bcol_ref, ag_ref, gg_ref, out_ref):
        u = _gelu(zu_ref[...].astype(F32))
        xhat, _ = _layernorm_parts(_gelu(zv_ref[...].astype(F32)))
        vv = xhat * lg_ref[...] + lb_ref[...]
        gm = u * _spatial_mix(w_ref, bcol_ref, vv.astype(BF16), n_heads, n_chunks)
        rg = lax.rsqrt(jnp.mean(gm * gm, axis=-1, keepdims=True) + EPS)
        ov = o_ref[...].astype(F32)
        ra = lax.rsqrt(jnp.mean(ov * ov, axis=-1, keepdims=True) + EPS)
        out_ref[:, :dg] = ((ov * ra) * ag_ref[...]).astype(BF16)
        out_ref[:, dg:] = ((gm * rg) * gg_ref[...]).astype(BF16)

    vec = pl.BlockSpec((1, dg), lambda i: (0, 0))
    return pl.pallas_call(
        body, name=name, out_shape=jax.ShapeDtypeStruct((s, 2 * dg), BF16), grid=(s // tr,),
        in_specs=[pl.BlockSpec((tr, dg), lambda i: (i, 3)), pl.BlockSpec((tr, dg), lambda i: (i, 4)),
                  pl.BlockSpec((tr, dg), lambda i: (i, 0)), vec, vec,
                  pl.BlockSpec((n_heads, CHUNK, CHUNK), lambda i: (0, 0, 0)),
                  pl.BlockSpec((CHUNK, n_heads), lambda i: (0, 0)), vec, vec],
        out_specs=pl.BlockSpec((tr, 2 * dg), lambda i: (i, 0)),
        compiler_params=_params("parallel"),
    )(z, z, o, ln_g, ln_b, w_s, b_col, attn_g, gm_g)


def _mix_bwd(z, o, d_merged, ln_g, ln_b, w_s, b_col, attn_g, gm_g, n_heads, *, name, tr=256):
    s = z.shape[0]
    dg = n_heads * HEAD_DIM
    tr = _tile(s, tr, CHUNK)
    n_chunks = tr // CHUNK

    def body(zu_ref, zv_ref, o_ref, dm_ref, lg_ref, lb_ref, w_ref, bcol_ref, ag_ref, gg_ref,
             do_ref, dzu_ref, dzv_ref, dw_ref, dbcol_ref, dlg_ref, dlb_ref, dag_ref, dgg_ref):
        @pl.when(pl.program_id(0) == 0)
        def _():
            for ref in (dw_ref, dbcol_ref, dlg_ref, dlb_ref, dag_ref, dgg_ref):
                ref[...] = jnp.zeros_like(ref)

        d_o, dag_rows = _rms_bwd_rows(dm_ref[:, :dg], o_ref[...].astype(F32), ag_ref[...])
        do_ref[...] = d_o.astype(BF16)
        dag_ref[...] += jnp.sum(dag_rows, axis=0, keepdims=True)

        u, du_dz = _gelu_and_grad(zu_ref[...].astype(F32))
        gv, dgv_dz = _gelu_and_grad(zv_ref[...].astype(F32))
        xhat, rs = _layernorm_parts(gv)
        lg = lg_ref[...]
        vv_b = (xhat * lg + lb_ref[...]).astype(BF16)
        mix = _spatial_mix(w_ref, bcol_ref, vv_b, n_heads, n_chunks)
        gm = u * mix
        d_gm, dgg_rows = _rms_bwd_rows(dm_ref[:, dg:], gm, gg_ref[...])
        dgg_ref[...] += jnp.sum(dgg_rows, axis=0, keepdims=True)
        dzu_ref[...] = ((d_gm * mix) * du_dz).astype(BF16)
        d_mix = d_gm * u
        d_mix_b = d_mix.astype(BF16)

        tril = _causal_mask(CHUNK)
        lane = lax.broadcasted_iota(jnp.int32, (CHUNK, n_heads), 1)
        cols = []
        db = jnp.zeros((CHUNK, n_heads), F32)
        for hh in range(n_heads):
            wc = jnp.where(tril, w_ref[hh], 0.0).astype(BF16)
            lanes = slice(hh * HEAD_DIM, (hh + 1) * HEAD_DIM)
            dw = jnp.zeros((CHUNK, CHUNK), F32)
            dmix_sum = jnp.zeros((CHUNK, HEAD_DIM), F32)
            rows = []
            for c in range(n_chunks):
                rws = slice(c * CHUNK, (c + 1) * CHUNK)
                dmb = d_mix_b[rws, lanes]
                dw += lax.dot_general(dmb, vv_b[rws, lanes], (((1,), (1,)), ((), ())), preferred_element_type=F32)
                rows.append(lax.dot_general(wc, dmb, (((0,), (0,)), ((), ())), preferred_element_type=F32))
                dmix_sum += d_mix[rws, lanes]
            dw_ref[hh] += jnp.where(tril, dw, 0.0)
            db += jnp.where(lane == hh, jnp.sum(dmix_sum, axis=-1, keepdims=True), 0.0)
            cols.append(jnp.concatenate(rows, axis=0))
        dbcol_ref[...] += db
        d_vv = jnp.concatenate(cols, axis=1)

        dlg_ref[...] += jnp.sum(d_vv * xhat, axis=0, keepdims=True)
        dlb_ref[...] += jnp.sum(d_vv, axis=0, keepdims=True)
        d_xhat = d_vv * lg
        d_gv = rs * (d_xhat - jnp.mean(d_xhat, axis=-1, keepdims=True)
                     - xhat * jnp.mean(d_xhat * xhat, axis=-1, keepdims=True))
        dzv_ref[...] = (d_gv * dgv_dz).astype(BF16)

    vec = pl.BlockSpec((1, dg), lambda i: (0, 0))
    wspec = pl.BlockSpec((n_heads, CHUNK, CHUNK), lambda i: (0, 0, 0))
    bspec = pl.BlockSpec((CHUNK, n_heads), lambda i: (0, 0))
    rowb = pl.BlockSpec((tr, dg), lambda i: (i, 0))
    act = jax.ShapeDtypeStruct((s, dg), BF16)
    vshape = jax.ShapeDtypeStruct((1, dg), F32)
    return pl.pallas_call(
        body, name=name,
        out_shape=(act, act, act, jax.ShapeDtypeStruct((n_heads, CHUNK, CHUNK), F32),
                   jax.ShapeDtypeStruct((CHUNK, n_heads), F32), vshape, vshape, vshape, vshape),
        grid=(s // tr,),
        in_specs=[pl.BlockSpec((tr, dg), lambda i: (i, 3)), pl.BlockSpec((tr, dg), lambda i: (i, 4)),
                  rowb, pl.BlockSpec((tr, 2 * dg), lambda i: (i, 0)), vec, vec, wspec, bspec, vec, vec],
        out_specs=(rowb, rowb, rowb, wspec, bspec, vec, vec, vec, vec),
        compiler_params=_params("arbitrary"),
    )(z, z, o, d_merged, ln_g, ln_b, w_s, b_col, attn_g, gm_g)


def _place():
    x, y, c = lax.axis_index("x"), lax.axis_index("y"), lax.axis_index("c")
    other_chips = [(1 - x, y), (x, 1 - y), (1 - x, 1 - y)]
    return x, y, c, other_chips


def _remote(src, dst, send_sem, recv_sem, to):
    return pltpu.make_async_remote_copy(src_ref=src, dst_ref=dst, send_sem=send_sem, recv_sem=recv_sem,
                                        device_id=to, device_id_type=MESH)


def _cast_into_slot(w, place, *, name):
    rows, cols = w.shape
    tr, tc = _rc_tile(rows, cols)

    def body(place_ref, w_ref, o_ref):
        o_ref[...] = w_ref[...].astype(BF16)

    grid_spec = pltpu.PrefetchScalarGridSpec(
        num_scalar_prefetch=1, grid=(rows // tr, cols // tc),
        in_specs=[pl.BlockSpec((tr, tc), lambda i, j, pr: (i, j))],
        out_specs=pl.BlockSpec((None, tr, tc), lambda i, j, pr: (pr[0], i, j)),
    )
    return pl.pallas_call(
        body, name=name, grid_spec=grid_spec, out_shape=jax.ShapeDtypeStruct((N_CHIPS, rows, cols), BF16),
        compiler_params=_params("parallel", "parallel"),
    )(place, w)


def _exchange(phases, *, name):
    comm_in = [a for ph in phases for a in ph.arrays]
    comm_out = [jax.ShapeDtypeStruct(s.shape, s.dtype) for ph in phases for s in (ph.arrays if ph.in_place else ph.out_shapes)]
    aliases, at_in, at_out = {}, 0, 0
    for ph in phases:
        if ph.in_place:
            aliases.update({at_in + r: at_out + r for r in range(len(ph.arrays))})
        at_in, at_out = at_in + len(ph.arrays), at_out + ph.n_out
    n_sems = sum(ph.n_sems for ph in phases)

    def body(*refs):
        cin, cout = refs[:len(comm_in)], refs[len(comm_in):len(comm_in) + len(comm_out)]
        send_sems, recv_sems = refs[len(comm_in) + len(comm_out):]
        _run_phases(phases, ("start", "finish"), cin, cout, send_sems, recv_sems)

    return pl.pallas_call(
        body, name=name, out_shape=tuple(comm_out), in_specs=[ANY] * len(comm_in), out_specs=tuple([ANY] * len(comm_out)),
        input_output_aliases=aliases,
        scratch_shapes=[pltpu.SemaphoreType.DMA((n_sems,)), pltpu.SemaphoreType.DMA((n_sems,))],
    )(*comm_in)


GATHER_PARTS = 4


def _gather(bufs, stage, part=(0, GATHER_PARTS)):
    n = 3 * len(bufs)
    lo, hi = part

    def copies(outs, send, recv, d2d, incoming):
        x, y, c, chips = _place()
        for t, buf in enumerate(outs):
            half = buf.shape[2] // 2
            piece = half // GATHER_PARTS
            for k, (cx, cy) in enumerate(chips):
                i = 3 * t + k + (n if (d2d and stage == "both") else 0)
                cols = pl.ds(((1 - c) if (d2d and incoming) else c) * half + lo * piece, (hi - lo) * piece)
                blk = buf.at[(2 * cx + cy) if (d2d or incoming) else (2 * x + y), :, cols]
                yield _remote(blk, blk, send(i), recv(i), (x, y, 1 - c) if d2d else (cx, cy, c))

    def start(ins, outs, send, recv):
        for cp in copies(outs, send, recv, stage == "d2d", False):
            cp.start()

    def finish(ins, outs, send, recv):
        if stage == "both":
            for arrival, onward in zip(copies(outs, send, recv, False, True), copies(outs, send, recv, True, False)):
                arrival.wait_recv()
                onward.start()
        for cp in copies(outs, send, recv, stage != "ici", True):
            cp.wait_recv()
        for d2d in ((False, True) if stage == "both" else (stage == "d2d",)):
            for cp in copies(outs, send, recv, d2d, False):
                cp.wait_send()

    return _Phase(bufs, [], True, (2 if stage == "both" else 1) * n, start, finish)


def _merge(first, second):
    n_first = first.n_sems

    def later(sem):
        return lambda i: sem(n_first + i)

    def start(ins, outs, send, recv):
        first.start(ins, outs, send, recv)
        second.start(ins, outs, later(send), later(recv))

    def finish(ins, outs, send, recv):
        first.finish(ins, outs, send, recv)
        second.finish(ins, outs, later(send), later(recv))

    return _Phase(first.arrays, [], True, n_first + second.n_sems, start, finish)


def _swap_halves(grads):
    def copies(ins, outs, send, recv):
        x, y, c, _ = _place()
        for t, g in enumerate(ins):
            half = g.shape[2] // 2
            yield _remote(g.at[:, :, pl.ds((1 - c) * half, half)], outs[t], send(t), recv(t), (x, y, 1 - c))

    def start(ins, outs, send, recv):
        for cp in copies(ins, outs, send, recv):
            cp.start()

    def finish(ins, outs, send, recv):
        for cp in copies(ins, outs, send, recv):
            cp.wait()

    shapes = [jax.ShapeDtypeStruct((a.shape[0], a.shape[1], a.shape[2] // 2), a.dtype) for a in grads]
    return _Phase(grads, shapes, False, len(grads), start, finish)


def _add_halves(grad, received, place, *, name):
    ns, rows, half = received.shape
    tr, tc = _rc_tile(rows, half)
    per = half // tc

    def body(place_ref, g_ref, r_ref, o_ref):
        o_ref[...] = (g_ref[...].astype(F32) + r_ref[...].astype(F32)).astype(BF16)

    grid_spec = pltpu.PrefetchScalarGridSpec(
        num_scalar_prefetch=1, grid=(ns, rows // tr, per),
        in_specs=[pl.BlockSpec((None, tr, tc), lambda s, i, j, pr: (s, i, pr[1] * per + j)),
                  pl.BlockSpec((None, tr, tc), lambda s, i, j, pr: (s, i, j))],
        out_specs=pl.BlockSpec((None, tr, tc), lambda s, i, j, pr: (s, i, j)),
    )
    return pl.pallas_call(
        body, name=name, grid_spec=grid_spec, out_shape=jax.ShapeDtypeStruct(received.shape, BF16),
        compiler_params=_params("parallel", "parallel", "parallel"),
    )(place, grad, received)


def _send_partials(parts):
    def start(ins, outs, send, recv):
        x, y, c, chips = _place()
        for t, part in enumerate(ins):
            for k, (cx, cy) in enumerate(chips):
                _remote(part.at[2 * cx + cy], outs[t].at[2 * x + y], send(3 * t + k), recv(3 * t + k), (cx, cy, c)).start()

    def finish(ins, outs, send, recv):
        x, y, c, chips = _place()
        for t, part in enumerate(ins):
            for k, (cx, cy) in enumerate(chips):
                slot = outs[t].at[2 * cx + cy]
                _remote(slot, slot, send(3 * t + k), recv(3 * t + k), (cx, cy, c)).wait_recv()
        for t, part in enumerate(ins):
            for k, (cx, cy) in enumerate(chips):
                sent = part.at[2 * cx + cy]
                _remote(sent, sent, send(3 * t + k), recv(3 * t + k), (cx, cy, c)).wait_send()

    shapes = [jax.ShapeDtypeStruct(a.shape, a.dtype) for a in parts]
    return _Phase(parts, shapes, False, 3 * len(parts), start, finish)


def _sum_chips(parts, slots, place, *, name):
    ns, rows, half = slots.shape
    tr, tc = _rc_tile(rows, half)
    per = half // tc

    def body(place_ref, p_ref, s_ref, o_ref):
        acc = p_ref[...].astype(F32)
        for k in range(ns):
            acc = acc + jnp.where(place_ref[0] == k, 0.0, s_ref[k].astype(F32))
        o_ref[...] = acc

    grid_spec = pltpu.PrefetchScalarGridSpec(
        num_scalar_prefetch=1, grid=(rows // tr, per),
        in_specs=[pl.BlockSpec((None, tr, tc), lambda i, j, pr: (pr[0], i, j)),
                  pl.BlockSpec((ns, tr, tc), lambda i, j, pr: (0, i, j))],
        out_specs=pl.BlockSpec((tr, tc), lambda i, j, pr: (i, pr[1] * per + j)),
    )
    return pl.pallas_call(
        body, name=name, grid_spec=grid_spec, out_shape=jax.ShapeDtypeStruct((rows, 2 * half), F32),
        compiler_params=_params("parallel", "parallel"),
    )(place, parts, slots)


def _join_halves(bufs):
    def copies(outs, send, recv, incoming):
        x, y, c, _ = _place()
        for t, buf in enumerate(outs):
            half = buf.shape[1] // 2
            cols = buf.at[:, pl.ds(((1 - c) if incoming else c) * half, half)]
            yield _remote(cols, cols, send(t), recv(t), (x, y, 1 - c))

    def start(ins, outs, send, recv):
        for cp in copies(outs, send, recv, False):
            cp.start()

    def finish(ins, outs, send, recv):
        for cp in copies(outs, send, recv, True):
            cp.wait_recv()
        for cp in copies(outs, send, recv, False):
            cp.wait_send()

    return _Phase(bufs, [], True, len(bufs), start, finish)


def _allgather_small(buf, *, name):
    rows = buf.shape[0]

    def body(x_ref, out_ref, send_sems, recv_sems, local_sem):
        x, y, c, chips = _place()
        sibling = (x, y, 1 - c)

        def slot(px, py, pc):
            return out_ref.at[4 * px + 2 * py + pc]

        def copy(k, block, to, src=None):
            return _remote(slot(*block) if src is None else src, slot(*block), send_sems.at[k], recv_sems.at[k], to)

        mine = pltpu.make_async_copy(x_ref, slot(x, y, c), local_sem)
        mine.start()
        first = [copy(0, (x, y, c), sibling, src=x_ref)]
        first += [copy(1 + k, (x, y, c), (*chip, c), src=x_ref) for k, chip in enumerate(chips)]
        for cp in first:
            cp.start()
        passed = [copy(4 + k, (*chip, c), sibling) for k, chip in enumerate(chips)]
        for k, chip in enumerate(chips):
            copy(1 + k, (*chip, c), (x, y, c)).wait_recv()
            passed[k].start()
        copy(0, (x, y, 1 - c), (x, y, c)).wait_recv()
        for k, chip in enumerate(chips):
            copy(4 + k, (*chip, 1 - c), (x, y, c)).wait_recv()
        for cp in first + passed:
            cp.wait_send()
        mine.wait()

    return pl.pallas_call(
        body, name=name, out_shape=jax.ShapeDtypeStruct((N_DEV, rows, LANES), buf.dtype),
        in_specs=[pl.BlockSpec(memory_space=pltpu.VMEM)], out_specs=pl.BlockSpec(memory_space=pltpu.VMEM),
        scratch_shapes=[pltpu.SemaphoreType.DMA((7,)), pltpu.SemaphoreType.DMA((7,)), pltpu.SemaphoreType.DMA],
    )(buf)


def _adamw_math(w, g, m, v):
    m = ADAM_B1 * m + (1.0 - ADAM_B1) * g
    v = ADAM_B2 * v + (1.0 - ADAM_B2) * (g * g)
    m_hat = m / (1.0 - ADAM_B1 ** ADAM_STEP)
    v_hat = v / (1.0 - ADAM_B2 ** ADAM_STEP)
    delta = -ADAM_LR * (m_hat / (jnp.sqrt(v_hat) + ADAM_EPS) + ADAM_WD * w)
    return delta, m, v


def _adamw(w, g, m, v, *, name):
    rows, cols = w.shape
    tr, tc = _rc_tile(rows, cols)

    def body(w_ref, g_ref, m_ref, v_ref, d_ref, mo_ref, vo_ref):
        d_ref[...], mo_ref[...], vo_ref[...] = _adamw_math(w_ref[...], g_ref[...], m_ref[...], v_ref[...])

    blk = pl.BlockSpec((tr, tc), lambda i, j: (i, j))
    shape = jax.ShapeDtypeStruct((rows, cols), F32)
    return pl.pallas_call(
        body, name=name, out_shape=(shape, shape, shape), grid=(rows // tr, cols // tc),
        in_specs=[blk] * 4, out_specs=(blk, blk, blk), compiler_params=_params("parallel", "parallel"),
    )(w, g, m, v)


def _adamw_small(gathered, w, m, v, *, name):
    nd = gathered.shape[0]

    def body(gs_ref, w_ref, m_ref, v_ref, g_ref, d_ref, mo_ref, vo_ref):
        g = gs_ref[0]
        for k in range(1, nd):
            g = g + gs_ref[k]
        g_ref[...] = g
        d_ref[...], mo_ref[...], vo_ref[...] = _adamw_math(w_ref[...], g, m_ref[...], v_ref[...])

    shape = jax.ShapeDtypeStruct(w.shape, F32)
    return pl.pallas_call(body, name=name, out_shape=(shape, shape, shape, shape),
                          compiler_params=pltpu.CompilerParams(vmem_limit_bytes=VMEM_LIMIT_BYTES))(gathered, w, m, v)


def _pack(parts):
    flat = jnp.concatenate([p.reshape(-1).astype(F32) for p in parts])
    rows = -(-flat.shape[0] // (8 * LANES)) * 8
    return jnp.pad(flat, (0, rows * LANES - flat.shape[0])).reshape(rows, LANES)


def _unpack(buf, shapes):
    flat = buf.reshape(-1)
    out, pos = [], 0
    for shp in shapes:
        size = int(np.prod(shp))
        out.append(flat[pos:pos + size].reshape(shp))
        pos += size
    return out


def _shard_rows(g, lo, hi):
    rs = g.shape[1]
    pieces = []
    for j in range(g.shape[0]):
        a, b = max(lo, j * rs), min(hi, (j + 1) * rs)
        if a < b:
            pieces.append(g[j, a - j * rs:b - j * rs])
    return pieces


def _rows_from_segments(segments, lo, hi):
    pieces = []
    for first, last, src, at in segments:
        a, b = max(lo, first), min(hi, last)
        if a < b:
            pieces.append(src[at + a - first:at + b - first])
    return jnp.concatenate(pieces, axis=0)


def kernel(x, norm_mix_g, w_in, b_f, gmlp_ln_g, gmlp_ln_b, w_s, b_s, attn_out_g, gmlp_out_g, w_out, norm_ffn_g, w_ff1, w_ff2, norm_final_g, loss_target, m_norm_mix_g, m_w_in, m_b_f, m_gmlp_ln_g, m_gmlp_ln_b, m_w_s, m_b_s, m_attn_out_g, m_gmlp_out_g, m_w_out, m_norm_ffn_g, m_w_ff1, m_w_ff2, m_norm_final_g, v_norm_mix_g, v_w_in, v_b_f, v_gmlp_ln_g, v_gmlp_ln_b, v_w_s, v_b_s, v_attn_out_g, v_gmlp_out_g, v_w_out, v_norm_ffn_g, v_w_ff1, v_w_ff2, v_norm_final_g):
    seq, d_model = x.shape[1], x.shape[2]
    d_attn = d_model // 2
    n_heads = d_attn // HEAD_DIM
    qkv = 3 * d_attn
    shard_cols = w_in.shape[2]
    assert N_CHIPS * shard_cols == qkv + n_heads + 2 * d_attn
    xs = x.reshape(seq, d_model)
    target = loss_target.reshape(seq, d_model)

    place = jnp.stack([2 * lax.axis_index("x") + lax.axis_index("y"), lax.axis_index("c")]).astype(jnp.int32)
    names = ["w_in", "w_out", "w_ff1", "w_ff2"]
    wt_in, mt_in, vt_in = w_in[0].T, m_w_in[0].T, v_w_in[0].T
    b_in, b_out, b_ff1, b_ff2 = [_cast_into_slot(w, place, name="cast_" + nm)
                                 for w, nm in zip((wt_in, w_out[0], w_ff1[0], w_ff2[0]), names)]
    (g_in,) = _exchange([_gather([b_in], "both")], name="allgather_w_in")
    n_cols = N_CHIPS * shard_cols
    wt_main = jnp.concatenate(_shard_rows(g_in, 0, qkv) + _shard_rows(g_in, qkv + n_heads, n_cols), axis=0)
    wt_f = jnp.pad(jnp.concatenate(_shard_rows(g_in, qkv, qkv + n_heads), axis=0), ((0, LANES - n_heads), (0, 0)))
    b_f_pad = jnp.pad(b_f, ((0, 0), (0, LANES - n_heads)))
    b_col = b_s[0].T

    h = _rmsnorm_fwd(xs, norm_mix_g, name="norm_mix")
    first, rest = (0, 1), (1, GATHER_PARTS)
    z, (b_out, b_ff1) = _matmul(h, wt_main, name="in_proj", out_dtype=BF16, trans_b=True,
                                phases=[_gather([b_out], "ici"), _gather([b_ff1], "ici", first)])
    zb, f_cum = _forget_fwd(h, wt_f, b_f_pad, name="forget_fwd")
    f_row = f_cum[:, :n_heads].T[:, None, :]
    (o, lse2), (b_ff1, b_out) = _attn_fwd(z, f_row, n_heads, name="attn_fwd",
                                          phases=[_gather([b_ff1], "ici", rest), _gather([b_out], "d2d")])
    merged = _mix_fwd(z, o, gmlp_ln_g, gmlp_ln_b, w_s[0], b_col, attn_out_g, gmlp_out_g, n_heads, name="mix_fwd")
    w_out_full = b_out.reshape(2 * d_attn, d_model)
    x1, (b_ff1, b_ff2) = _matmul(merged, w_out_full, name="out_proj", out_dtype=F32, residual=xs,
                                 phases=[_gather([b_ff1], "d2d"), _gather([b_ff2], "ici", first)])
    h2 = _rmsnorm_fwd(x1, norm_ffn_g, name="norm_ffn")
    a, (b_ff2,) = _matmul(h2, b_ff1, name="ff1", out_dtype=BF16, relu=True, b_sharded=True,
                          phases=[_merge(_gather([b_ff2], "both", rest), _gather([b_ff2], "d2d", first))])
    w_ff2_full = b_ff2.reshape(N_CHIPS * b_ff2.shape[1], d_model)
    x2, _ = _matmul(a, w_ff2_full, name="ff2", out_dtype=F32, square_lhs=True, residual=x1)
    dx2, dg_final, loss = _loss_and_final_bwd(x2, target, norm_final_g.reshape(1, d_model), name="loss_head")

    def pair_sum(g, r, nm):
        return _add_halves(g, r, place, name="grads_pair_sum_" + nm)

    def chip_sum(p, q, nm):
        return _sum_chips(p, q, place, name="grads_chip_sum_" + nm)

    dw_ff2, _ = _matmul(a, dx2, name="ff2_dw", out_dtype=BF16, trans_a=True, square_lhs=True)
    dw_ff2 = dw_ff2.reshape(N_CHIPS, -1, d_model)
    da, (r_ff2,) = _matmul(dx2, w_ff2_full, name="ff2_dlhs", out_dtype=BF16, trans_b=True, scale2_by=a,
                           phases=[_swap_halves([dw_ff2])])
    ps_ff2 = pair_sum(dw_ff2, r_ff2, "w_ff2")
    dh2, (q_ff2,) = _matmul(da, b_ff1, name="ff1_dlhs", out_dtype=F32, trans_b=True, b_sharded=True,
                            phases=[_send_partials([ps_ff2])])
    g_ff2 = chip_sum(ps_ff2, q_ff2, "w_ff2")
    dw_ff1, (g_ff2,) = _matmul(h2, da, name="ff1_dw", out_dtype=BF16, trans_a=True, out_sharded=True,
                               phases=[_join_halves([g_ff2])])
    (dx1, dg_ffn), _ = _rmsnorm_bwd(dh2, x1, dx2, norm_ffn_g, name="norm_ffn_bwd")
    dw_out, _ = _matmul(merged, dx1, name="out_proj_dw", out_dtype=BF16, trans_a=True)
    dw_out = dw_out.reshape(N_CHIPS, -1, d_model)
    d_merged, (r_ff1, r_out) = _matmul(dx1, w_out_full, name="out_proj_dlhs", out_dtype=F32, trans_b=True,
                                       phases=[_swap_halves([dw_ff1, dw_out])])
    ps_ff1, ps_out = pair_sum(dw_ff1, r_ff1, "w_ff1"), pair_sum(dw_out, r_out, "w_out")
    d_o, dzu, dzv, dw_s, db_col, dlg, dlb, dag, dgg = _mix_bwd(
        z, o, d_merged, gmlp_ln_g, gmlp_ln_b, w_s[0], b_col, attn_out_g, gmlp_out_g, n_heads, name="mix_bwd")
    (dq, dk, dv, d_f_key, d_f_query), (q_ff1, q_out) = _attn_bwd(
        z, o, d_o, lse2, f_row, n_heads, name="attn_bwd", phases=[_send_partials([ps_ff1, ps_out])])
    g_ff1, g_out = chip_sum(ps_ff1, q_ff1, "w_ff1"), chip_sum(ps_out, q_out, "w_out")
    d_f = d_f_key.reshape(n_heads, seq) + d_f_query.reshape(n_heads, seq)
    d_f_pad = jnp.pad(d_f.T, ((0, 0), (0, LANES - n_heads)))
    dzf, db_f = _forget_bwd(d_f_pad, zb, name="forget_bwd")
    dz = jnp.concatenate([dq, dk, dv, dzu, dzv], axis=1)
    dwt_main, (g_ff1, g_out) = _matmul(dz, h, name="in_proj_dw", out_dtype=BF16, trans_a=True,
                                       phases=[_join_halves([g_ff1, g_out])])
    dwt_f, _ = _matmul(dzf, h, name="gate_dw", out_dtype=BF16, trans_a=True)
    segments = [(0, qkv, dwt_main, 0), (qkv, qkv + n_heads, dwt_f, 0), (qkv + n_heads, n_cols, dwt_main, qkv)]
    dw_in = jnp.stack([_rows_from_segments(segments, j * shard_cols, (j + 1) * shard_cols) for j in range(N_CHIPS)])
    dh_gate, (r_in,) = _matmul(dzf, wt_f, name="gate_dlhs", out_dtype=F32, phases=[_swap_halves([dw_in])])
    ps_in = pair_sum(dw_in, r_in, "w_in")
    dh, (q_in,) = _matmul(dz, wt_main, name="in_proj_dlhs", out_dtype=F32, residual=dh_gate, tk=2560,
                          phases=[_send_partials([ps_in])])
    g_in_sum = chip_sum(ps_in, q_in, "w_in")
    (grad_x, dg_mix), _ = _rmsnorm_bwd(dh, xs, dx1, norm_mix_g, name="norm_mix_bwd")
    (g_in_sum,) = _exchange([_join_halves([g_in_sum])], name="grads_join_w_in")

    big = {}
    for nm, g, w, m, v in zip(names, (g_in_sum, g_out, g_ff1, g_ff2), (wt_in, w_out[0], w_ff1[0], w_ff2[0]),
                              (mt_in, m_w_out[0], m_w_ff1[0], m_w_ff2[0]), (vt_in, v_w_out[0], v_w_ff1[0], v_w_ff2[0])):
        d, mo, vo = _adamw(w, g, m, v, name="adamw_" + nm)
        big[nm] = tuple((t.T if nm == "w_in" else t)[None] for t in (g, d, mo, vo))

    small_w = [norm_mix_g, b_f, gmlp_ln_g, gmlp_ln_b, w_s, b_s, attn_out_g, gmlp_out_g, norm_ffn_g, norm_final_g]
    small_m = [m_norm_mix_g, m_b_f, m_gmlp_ln_g, m_gmlp_ln_b, m_w_s, m_b_s, m_attn_out_g, m_gmlp_out_g, m_norm_ffn_g, m_norm_final_g]
    small_v = [v_norm_mix_g, v_b_f, v_gmlp_ln_g, v_gmlp_ln_b, v_w_s, v_b_s, v_attn_out_g, v_gmlp_out_g, v_norm_ffn_g, v_norm_final_g]
    small_g = [dg_mix, db_f[:, :n_heads], dlg, dlb, dw_s, db_col.T, dag, dgg, dg_ffn, dg_final]
    shapes = [w.shape for w in small_w]
    gathered = _allgather_small(_pack(small_g), name="allgather_small_grads")
    packed = _adamw_small(gathered, _pack(small_w), _pack(small_m), _pack(small_v), name="adamw_small")
    sg, sd, sm, sv = (_unpack(p, shapes) for p in packed)
    small_names = ["norm_mix_g", "b_f", "gmlp_ln_g", "gmlp_ln_b", "w_s", "b_s", "attn_out_g", "gmlp_out_g", "norm_ffn_g", "norm_final_g"]
    small = {nm: (sg[i], sd[i], sm[i], sv[i]) for i, nm in enumerate(small_names)}

    order = ["norm_mix_g", "w_in", "b_f", "gmlp_ln_g", "gmlp_ln_b", "w_s", "b_s", "attn_out_g", "gmlp_out_g", "w_out",
             "norm_ffn_g", "w_ff1", "w_ff2", "norm_final_g"]
    result = {**small, **big}
    total_loss = lax.psum(loss[0, 0], ("x", "y", "c"))
    outs = [total_loss, grad_x.reshape(x.shape)]
    for part in range(4):
        outs += [result[nm][part] for nm in order]
    return tuple(outs)
```

```python
import functools
import math

import numpy as np
import jax
import jax.numpy as jnp
from jax import lax
from jax.experimental import pallas as pl
from jax.experimental.pallas import tpu as pltpu

HEAD_DIM = 128
CHUNK = 128
EPS = 1e-6
LANES = 128
N_CHIPS = 4
N_DEV = 8
VMEM_LIMIT_BYTES = 56 * 1024 * 1024

ADAM_LR = 0.001
ADAM_B1 = 0.9
ADAM_B2 = 0.999
ADAM_EPS = 1e-08
ADAM_WD = 0.01
ADAM_STEP = 10

BF16 = jnp.bfloat16
F32 = jnp.float32
MESH = pl.DeviceIdType.MESH
ANY = pl.BlockSpec(memory_space=pl.ANY)
NEG_BIG = -1e30


def _params(*sem):
    return pltpu.CompilerParams(dimension_semantics=tuple(sem), vmem_limit_bytes=VMEM_LIMIT_BYTES)


def _tile(n, pref, unit):
    t = (min(pref, n) // unit) * unit
    while t >= unit:
        if n % t == 0:
            return t
        t -= unit
    return n


def _rc_tile(rows, cols, pref_rows=256, pref_cols=256):
    if rows % 16 == 0:
        return _tile(rows, pref_rows, 16), cols
    return rows, _tile(cols, pref_cols, LANES)


class _Phase:
    def __init__(self, arrays, out_shapes, in_place, n_sems, start, finish):
        self.arrays, self.out_shapes, self.in_place = list(arrays), list(out_shapes), in_place
        self.n_sems, self.start, self.finish = n_sems, start, finish

    @property
    def n_out(self):
        return len(self.arrays) if self.in_place else len(self.out_shapes)


def _run_phases(phases, steps, comm_in, comm_out, send_sems, recv_sems):
    at_in = at_out = at_sem = 0
    for ph in phases:
        for step in steps:
            getattr(ph, step)(comm_in[at_in:at_in + len(ph.arrays)], comm_out[at_out:at_out + ph.n_out],
                              lambda i, base=at_sem: send_sems.at[base + i], lambda i, base=at_sem: recv_sems.at[base + i])
        at_in, at_out, at_sem = at_in + len(ph.arrays), at_out + ph.n_out, at_sem + ph.n_sems


def _call(body, *, name, grid, in_specs, out_specs, out_shape, operands, semantics, scratch_shapes=(),
          n_prefetch=0, phases=()):
    in_specs, out_specs, out_shape = list(in_specs), list(out_specs), list(out_shape)
    scratch_shapes = list(scratch_shapes)
    n_in, n_out, n_scr = len(operands) - n_prefetch, len(out_shape), len(scratch_shapes)
    comm_in = [a for ph in phases for a in ph.arrays]
    comm_out = [jax.ShapeDtypeStruct(s.shape, s.dtype) for ph in phases
                for s in (ph.arrays if ph.in_place else ph.out_shapes)]
    aliases, at_in, at_out = {}, n_prefetch + n_in, n_out
    for ph in phases:
        if ph.in_place:
            aliases.update({at_in + r: at_out + r for r in range(len(ph.arrays))})
        at_in, at_out = at_in + len(ph.arrays), at_out + ph.n_out
    n_sems = sum(ph.n_sems for ph in phases)

    def hosted(*refs):
        pre, rest = refs[:n_prefetch], refs[n_prefetch:]
        ins, rest = rest[:n_in], rest[n_in:]
        cin, rest = rest[:len(comm_in)], rest[len(comm_in):]
        outs, rest = rest[:n_out], rest[n_out:]
        cout, rest = rest[:len(comm_out)], rest[len(comm_out):]
        scr = rest[:n_scr]
        if phases:
            send_sems, recv_sems = rest[n_scr:]
            ids = [pl.program_id(ax) for ax in range(len(grid))]
            first = functools.reduce(jnp.logical_and, [i == 0 for i in ids])
            last = functools.reduce(jnp.logical_and, [i == g - 1 for i, g in zip(ids, grid)])

            @pl.when(first)
            def _():
                _run_phases(phases, ("start",), cin, cout, send_sems, recv_sems)

        body(*pre, *ins, *outs, *scr)
        if phases:
            @pl.when(last)
            def _():
                _run_phases(phases, ("finish",), cin, cout, send_sems, recv_sems)

    all_in = in_specs + [ANY] * len(comm_in)
    all_out = out_specs + [ANY] * len(comm_out)
    all_scr = scratch_shapes + ([pltpu.SemaphoreType.DMA((n_sems,)), pltpu.SemaphoreType.DMA((n_sems,))] if phases else [])
    if phases:
        semantics = ("arbitrary",) * len(grid)
    kwargs = dict(name=name, out_shape=tuple(out_shape + comm_out), compiler_params=_params(*semantics),
                  input_output_aliases=aliases)
    if n_prefetch:
        kwargs["grid_spec"] = pltpu.PrefetchScalarGridSpec(
            num_scalar_prefetch=n_prefetch, grid=grid, in_specs=all_in, out_specs=tuple(all_out), scratch_shapes=all_scr)
    else:
        kwargs.update(grid=grid, in_specs=all_in, out_specs=tuple(all_out), scratch_shapes=all_scr)
    res = pl.pallas_call(hosted, **kwargs)(*operands, *comm_in)
    return tuple(res[:n_out]), tuple(res[n_out:])


def _only(results):
    outs, comm = results
    return outs[0] if len(outs) == 1 else outs, comm


def _matmul(a, b, *, name, out_dtype, trans_a=False, trans_b=False, tm=1024, tn=1024, tk=2048,
            square_lhs=False, relu=False, residual=None, scale2_by=None,
            b_sharded=False, out_sharded=False, phases=()):
    m, k = (a.shape[1], a.shape[0]) if trans_a else a.shape
    if b_sharded:
        if trans_b:
            n, ks = b.shape[1], b.shape[2]
            assert N_CHIPS * ks == k
        else:
            ns = b.shape[2]
            n = N_CHIPS * ns
            assert b.shape[1] == k
    else:
        n = b.shape[0] if trans_b else b.shape[1]
        assert (b.shape[1] if trans_b else b.shape[0]) == k
    tm = _tile(m, tm, 128)
    tn = _tile(n // N_CHIPS if (out_sharded or (b_sharded and not trans_b)) else n, tn, 128)
    tk = _tile(k // N_CHIPS if (b_sharded and trans_b) else k, tk, 128)
    nk = k // tk

    if trans_a:
        a_spec = pl.BlockSpec((tk, tm), lambda i, j, kk: (kk, i))
    else:
        a_spec = pl.BlockSpec((tm, tk), lambda i, j, kk: (i, kk))
    if b_sharded and trans_b:
        per = ks // tk
        assert per * tk == ks
        b_spec = pl.BlockSpec((None, tn, tk), lambda i, j, kk: (kk // per, j, kk % per))
    elif b_sharded:
        per = ns // tn
        assert per * tn == ns
        b_spec = pl.BlockSpec((None, tk, tn), lambda i, j, kk: (j // per, kk, j % per))
    elif trans_b:
        b_spec = pl.BlockSpec((tn, tk), lambda i, j, kk: (j, kk))
    else:
        b_spec = pl.BlockSpec((tk, tn), lambda i, j, kk: (kk, j))
    if out_sharded:
        ns_out = n // N_CHIPS
        per_o = ns_out // tn
        assert per_o * tn == ns_out
        out_shape = jax.ShapeDtypeStruct((N_CHIPS, m, ns_out), out_dtype)
        o_spec = pl.BlockSpec((None, tm, tn), lambda i, j, kk: (j // per_o, i, j % per_o))
    else:
        out_shape = jax.ShapeDtypeStruct((m, n), out_dtype)
        o_spec = pl.BlockSpec((tm, tn), lambda i, j, kk: (i, j))
    mn_spec = pl.BlockSpec((tm, tn), lambda i, j, kk: (i, j))

    operands, in_specs = [a, b], [a_spec, b_spec]
    if scale2_by is not None:
        operands.append(scale2_by)
        in_specs.append(mn_spec)
    if residual is not None:
        operands.append(residual)
        in_specs.append(mn_spec)
    dims = (((0 if trans_a else 1,), (1 if trans_b else 0,)), ((), ()))

    def body(*refs):
        a_ref, b_ref = refs[0], refs[1]
        pos = 2
        scale_ref = res_ref = None
        if scale2_by is not None:
            scale_ref = refs[pos]
            pos += 1
        if residual is not None:
            res_ref = refs[pos]
            pos += 1
        o_ref = refs[pos]
        kk = pl.program_id(2)

        av = a_ref[...]
        if square_lhs:
            av = av.astype(F32)
            av = av * av
        part = lax.dot_general(av.astype(BF16), b_ref[...].astype(BF16), dims, preferred_element_type=F32)

        def finish(r):
            if relu:
                r = jnp.maximum(r, 0.0)
            if scale_ref is not None:
                r = r * (2.0 * scale_ref[...].astype(F32))
            if res_ref is not None:
                r = r + res_ref[...].astype(F32)
            o_ref[...] = r.astype(out_dtype)

        if nk == 1:
            finish(part)
        else:
            acc_ref = refs[pos + 1]

            @pl.when(kk == 0)
            def _():
                acc_ref[...] = part

            @pl.when(jnp.logical_and(kk > 0, kk < nk - 1))
            def _():
                acc_ref[...] += part

            @pl.when(kk == nk - 1)
            def _():
                finish(acc_ref[...] + part)

    return _only(_call(
        body, name=name, out_shape=[out_shape], grid=(m // tm, n // tn, nk),
        in_specs=in_specs, out_specs=[o_spec], operands=operands,
        scratch_shapes=[pltpu.VMEM((tm, tn), F32)] if nk > 1 else [],
        semantics=("parallel", "parallel", "arbitrary"), phases=phases))


def _rmsnorm_fwd(x, g, *, name, tr=512):
    s, d = x.shape
    tr = _tile(s, tr, 8)

    def body(x_ref, g_ref, o_ref):
        xv = x_ref[...]
        r = lax.rsqrt(jnp.mean(xv * xv, axis=-1, keepdims=True) + EPS)
        o_ref[...] = ((xv * r) * g_ref[...]).astype(BF16)

    return pl.pallas_call(
        body, name=name, out_shape=jax.ShapeDtypeStruct((s, d), BF16), grid=(s // tr,),
        in_specs=[pl.BlockSpec((tr, d), lambda i: (i, 0)), pl.BlockSpec((1, d), lambda i: (0, 0))],
        out_specs=pl.BlockSpec((tr, d), lambda i: (i, 0)),
        compiler_params=_params("parallel"),
    )(x, g)


def _rms_bwd_rows(dy, xv, g):
    d = xv.shape[-1]
    r = lax.rsqrt(jnp.mean(xv * xv, axis=-1, keepdims=True) + EPS)
    gdy = dy * g
    dot = jnp.sum(gdy * xv, axis=-1, keepdims=True)
    dx = gdy * r - xv * (r * r * r) * (dot / d)
    return dx, dy * (xv * r)


def _rmsnorm_bwd(dy, x, res, g, *, name, tr=256, phases=()):
    s, d = x.shape
    tr = _tile(s, tr, 8)

    def body(dy_ref, x_ref, res_ref, g_ref, dx_ref, dg_ref):
        @pl.when(pl.program_id(0) == 0)
        def _():
            dg_ref[...] = jnp.zeros_like(dg_ref)

        dx, dg_rows = _rms_bwd_rows(dy_ref[...].astype(F32), x_ref[...], g_ref[...])
        dx_ref[...] = res_ref[...] + dx
        dg_ref[...] += jnp.sum(dg_rows, axis=0, keepdims=True)

    row = pl.BlockSpec((tr, d), lambda i: (i, 0))
    vec = pl.BlockSpec((1, d), lambda i: (0, 0))
    return _call(
        body, name=name, out_shape=[jax.ShapeDtypeStruct((s, d), F32), jax.ShapeDtypeStruct((1, d), F32)],
        grid=(s // tr,), in_specs=[row, row, row, vec], out_specs=[row, vec], operands=[dy, x, res, g],
        semantics=("arbitrary",), phases=phases)


def _loss_and_final_bwd(x2, target, g, *, name, tr=256):
    s, d = x2.shape
    tr = _tile(s, tr, 8)

    def body(x_ref, t_ref, g_ref, dx_ref, dg_ref, loss_ref):
        @pl.when(pl.program_id(0) == 0)
        def _():
            dg_ref[...] = jnp.zeros_like(dg_ref)
            loss_ref[...] = jnp.zeros_like(loss_ref)

        xv, gv = x_ref[...], g_ref[...]
        r = lax.rsqrt(jnp.mean(xv * xv, axis=-1, keepdims=True) + EPS)
        err = (xv * r) * gv - t_ref[...]
        row_loss = jnp.mean(err * err, axis=-1, keepdims=True)
        loss_ref[...] += 0.5 * jnp.sum(row_loss, axis=0, keepdims=True)
        dx, dg_rows = _rms_bwd_rows(err / d, xv, gv)
        dx_ref[...] = dx
        dg_ref[...] += jnp.sum(dg_rows, axis=0, keepdims=True)

    row = pl.BlockSpec((tr, d), lambda i: (i, 0))
    vec = pl.BlockSpec((1, d), lambda i: (0, 0))
    one = pl.BlockSpec((1, 1), lambda i: (0, 0))
    return pl.pallas_call(
        body, name=name,
        out_shape=(jax.ShapeDtypeStruct((s, d), F32), jax.ShapeDtypeStruct((1, d), F32),
                   jax.ShapeDtypeStruct((1, 1), F32)),
        grid=(s // tr,), in_specs=[row, row, vec], out_specs=(row, vec, one),
        compiler_params=_params("arbitrary"),
    )(x2, target, g)


def _tri_ones(n, lower):
    r = lax.broadcasted_iota(jnp.int32, (n, n), 0)
    c = lax.broadcasted_iota(jnp.int32, (n, n), 1)
    return jnp.where((c <= r) if lower else (c >= r), 1.0, 0.0).astype(F32)


def _forget_fwd(h, w_f, b_f, *, name, tr=256):
    s, d = h.shape
    tr = _tile(s, tr, 8)

    def body(h_ref, w_ref, b_ref, zb_ref, f_ref, carry):
        @pl.when(pl.program_id(0) == 0)
        def _():
            carry[...] = jnp.zeros_like(carry)

        zb = lax.dot_general(h_ref[...], w_ref[...], (((1,), (1,)), ((), ())), preferred_element_type=F32) + b_ref[...]
        zb_ref[...] = zb
        log_f = jnp.minimum(zb, 0.0) - jnp.log(1.0 + jnp.exp(-jnp.abs(zb)))
        run = jnp.dot(_tri_ones(tr, True), log_f, preferred_element_type=F32,
                      precision=lax.Precision.HIGHEST) + carry[...]
        f_ref[...] = run
        carry[...] = run[tr - 1:tr, :]

    row = pl.BlockSpec((tr, LANES), lambda i: (i, 0))
    return pl.pallas_call(
        body, name=name,
        out_shape=(jax.ShapeDtypeStruct((s, LANES), F32), jax.ShapeDtypeStruct((s, LANES), F32)),
        grid=(s // tr,),
        in_specs=[pl.BlockSpec((tr, d), lambda i: (i, 0)), pl.BlockSpec((LANES, d), lambda i: (0, 0)),
                  pl.BlockSpec((1, LANES), lambda i: (0, 0))],
        out_specs=(row, row), scratch_shapes=[pltpu.VMEM((1, LANES), F32)],
        compiler_params=_params("arbitrary"),
    )(h, w_f, b_f)


def _forget_bwd(d_f, zb, *, name, tr=256):
    s = zb.shape[0]
    tr = _tile(s, tr, 8)
    nb = s // tr

    def body(df_ref, zb_ref, dz_ref, db_ref, carry):
        @pl.when(pl.program_id(0) == 0)
        def _():
            carry[...] = jnp.zeros_like(carry)
            db_ref[...] = jnp.zeros_like(db_ref)

        run = jnp.dot(_tri_ones(tr, False), df_ref[...], preferred_element_type=F32,
                      precision=lax.Precision.HIGHEST) + carry[...]
        carry[...] = run[0:1, :]
        dz = run / (1.0 + jnp.exp(zb_ref[...]))
        dz_ref[...] = dz.astype(BF16)
        db_ref[...] += jnp.sum(dz, axis=0, keepdims=True)

    row = pl.BlockSpec((tr, LANES), lambda i: (nb - 1 - i, 0))
    return pl.pallas_call(
        body, name=name,
        out_shape=(jax.ShapeDtypeStruct((s, LANES), BF16), jax.ShapeDtypeStruct((1, LANES), F32)),
        grid=(nb,), in_specs=[row, row], out_specs=(row, pl.BlockSpec((1, LANES), lambda i: (0, 0))),
        scratch_shapes=[pltpu.VMEM((1, LANES), F32)],
        compiler_params=_params("arbitrary"),
    )(d_f, zb)


def _pairs(nblk, by_kv):
    if by_kv:
        pr = [(i, j) for j in range(nblk) for i in range(j, nblk)]
    else:
        pr = [(i, j) for i in range(nblk) for j in range(i + 1)]
    return (jnp.asarray(np.array([p[0] for p in pr], np.int32)), jnp.asarray(np.array([p[1] for p in pr], np.int32)))


def _causal_mask(t):
    r = lax.broadcasted_iota(jnp.int32, (t, t), 0)
    c = lax.broadcasted_iota(jnp.int32, (t, t), 1)
    return c <= r


LOG2E = math.log2(math.e)
QK_TO_LOG2 = LOG2E / math.sqrt(HEAD_DIM)


def _attn_logits2(q, k, fk_row):
    sc = lax.dot_general(q, k, (((1,), (1,)), ((), ())), preferred_element_type=F32)
    return sc * QK_TO_LOG2 - fk_row * LOG2E


def _attn_fwd(z, f_row, n_heads, *, name, tb=1024, phases=()):
    s = z.shape[0]
    tb = _tile(s, tb, 128)
    nblk = s // tb
    rep = tb // LANES
    qi, kj = _pairs(nblk, by_kv=False)

    def body(qi_ref, kj_ref, q_ref, k_ref, v_ref, fk_ref, o_ref, lse_ref, m_sc, l_sc, acc_sc):
        p = pl.program_id(1)
        i, j = qi_ref[p], kj_ref[p]

        @pl.when(j == 0)
        def _():
            m_sc[...] = jnp.full_like(m_sc, NEG_BIG)
            l_sc[...] = jnp.zeros_like(l_sc)
            acc_sc[...] = jnp.zeros_like(acc_sc)

        def update(masked):
            s2 = _attn_logits2(q_ref[...], k_ref[...], fk_ref[...])
            if masked:
                s2 = jnp.where(_causal_mask(tb), s2, NEG_BIG)
            m_old = m_sc[...]
            m_new = jnp.maximum(m_old, jnp.max(s2, axis=-1, keepdims=True))
            alpha = jnp.exp2(m_old - m_new)
            pv = jnp.exp2(s2 - jnp.tile(m_new, (1, rep)))
            l_sc[...] = alpha * l_sc[...] + jnp.sum(pv, axis=-1, keepdims=True)
            acc_sc[...] = alpha * acc_sc[...] + jnp.dot(pv.astype(BF16), v_ref[...], preferred_element_type=F32)
            m_sc[...] = m_new

        @pl.when(j < i)
        def _():
            update(False)

        @pl.when(j == i)
        def _():
            update(True)
            o_ref[...] = (acc_sc[...] / l_sc[...]).astype(BF16)
            lse_ref[...] = m_sc[...] + jnp.log2(l_sc[...])

    h = n_heads
    return _call(
        body, name=name, n_prefetch=2, grid=(h, int(qi.shape[0])),
        in_specs=[
            pl.BlockSpec((tb, HEAD_DIM), lambda hh, p, qi_r, kj_r: (qi_r[p], hh)),
            pl.BlockSpec((tb, HEAD_DIM), lambda hh, p, qi_r, kj_r: (kj_r[p], h + hh)),
            pl.BlockSpec((tb, HEAD_DIM), lambda hh, p, qi_r, kj_r: (kj_r[p], 2 * h + hh)),
            pl.BlockSpec((None, 1, tb), lambda hh, p, qi_r, kj_r: (hh, 0, kj_r[p])),
        ],
        out_specs=[
            pl.BlockSpec((tb, HEAD_DIM), lambda hh, p, qi_r, kj_r: (qi_r[p], hh)),
            pl.BlockSpec((None, tb, LANES), lambda hh, p, qi_r, kj_r: (hh, qi_r[p], 0)),
        ],
        scratch_shapes=[pltpu.VMEM((tb, LANES), F32), pltpu.VMEM((tb, LANES), F32), pltpu.VMEM((tb, HEAD_DIM), F32)],
        out_shape=[jax.ShapeDtypeStruct((s, h * HEAD_DIM), BF16), jax.ShapeDtypeStruct((h, s, LANES), F32)],
        operands=[qi, kj, z, z, z, f_row], semantics=("parallel", "arbitrary"), phases=phases)


def _attn_bwd(z, o, d_o, lse2, f_row, n_heads, *, name, tb=1024, phases=()):
    s = z.shape[0]
    tb = _tile(s, tb, 128)
    nblk = s // tb
    rep = tb // LANES
    qi, kj = _pairs(nblk, by_kv=True)
    n_pairs = int(qi.shape[0])
    scale = 1.0 / math.sqrt(HEAD_DIM)
    h = n_heads

    def body(qi_ref, kj_ref, q_ref, k_ref, v_ref, o_ref, do_ref, lse_ref, fk_ref,
             dq_ref, dk_ref, dv_ref, df_ref, dfq_ref, dq_sc, dk_sc, dv_sc, df_sc, dfq_sc):
        p = pl.program_id(1)
        i, j = qi_ref[p], kj_ref[p]

        @pl.when(p == 0)
        def _():
            dq_sc[...] = jnp.zeros_like(dq_sc)
            dfq_sc[...] = jnp.zeros_like(dfq_sc)

        @pl.when(i == j)
        def _():
            dk_sc[...] = jnp.zeros_like(dk_sc)
            dv_sc[...] = jnp.zeros_like(dv_sc)
            df_sc[...] = jnp.zeros_like(df_sc)

        def update(masked):
            q, k, v, do = q_ref[...], k_ref[...], v_ref[...], do_ref[...]
            delta = jnp.sum(do.astype(F32) * o_ref[...].astype(F32), axis=-1, keepdims=True)
            pv = jnp.exp2(_attn_logits2(q, k, fk_ref[...]) - jnp.tile(lse_ref[...], (1, rep)))
            if masked:
                pv = jnp.where(_causal_mask(tb), pv, 0.0)
            dp = lax.dot_general(do, v, (((1,), (1,)), ((), ())), preferred_element_type=F32)
            ds = pv * (dp - delta)
            ds_b = ds.astype(BF16)
            dv_sc[...] += lax.dot_general(pv.astype(BF16), do, (((0,), (0,)), ((), ())), preferred_element_type=F32)
            dk_sc[...] += lax.dot_general(ds_b, q, (((0,), (0,)), ((), ())), preferred_element_type=F32)
            rows = pl.ds(pl.multiple_of(i * tb, tb), tb)
            dq_sc[rows, :] += jnp.dot(ds_b, k, preferred_element_type=F32)
            df_sc[...] -= jnp.sum(ds, axis=0, keepdims=True)
            dfq_sc[rows, :] += jnp.broadcast_to(jnp.sum(ds, axis=1, keepdims=True), (tb, LANES))

        @pl.when(i > j)
        def _():
            update(False)

        @pl.when(i == j)
        def _():
            update(True)

        @pl.when(i == nblk - 1)
        def _():
            dk_ref[...] = (dk_sc[...] * scale).astype(BF16)
            dv_ref[...] = dv_sc[...].astype(BF16)
            df_ref[...] = df_sc[...]

        @pl.when(p == n_pairs - 1)
        def _():
            dq_ref[...] = (dq_sc[...] * scale).astype(BF16)
            dfq_ref[...] = jnp.transpose(dfq_sc[...])[0:1, :]

    qblk = lambda off: pl.BlockSpec((tb, HEAD_DIM), lambda hh, p, qi_r, kj_r: (qi_r[p], off + hh))
    kblk = lambda off: pl.BlockSpec((tb, HEAD_DIM), lambda hh, p, qi_r, kj_r: (kj_r[p], off + hh))
    qrep = pl.BlockSpec((None, tb, LANES), lambda hh, p, qi_r, kj_r: (hh, qi_r[p], 0))
    krow = pl.BlockSpec((None, 1, tb), lambda hh, p, qi_r, kj_r: (hh, 0, kj_r[p]))
    act = jax.ShapeDtypeStruct((s, h * HEAD_DIM), BF16)
    return _call(
        body, name=name, n_prefetch=2, grid=(h, n_pairs),
        in_specs=[qblk(0), kblk(h), kblk(2 * h), qblk(0), qblk(0), qrep, krow],
        out_specs=[
            pl.BlockSpec((s, HEAD_DIM), lambda hh, p, qi_r, kj_r: (0, hh)),
            kblk(0), kblk(0), krow,
            pl.BlockSpec((None, 1, s), lambda hh, p, qi_r, kj_r: (hh, 0, 0)),
        ],
        scratch_shapes=[pltpu.VMEM((s, HEAD_DIM), F32), pltpu.VMEM((tb, HEAD_DIM), F32),
                        pltpu.VMEM((tb, HEAD_DIM), F32), pltpu.VMEM((1, tb), F32), pltpu.VMEM((s, LANES), F32)],
        out_shape=[act, act, act, jax.ShapeDtypeStruct((h, 1, s), F32), jax.ShapeDtypeStruct((h, 1, s), F32)],
        operands=[qi, kj, z, z, z, o, d_o, lse2, f_row], semantics=("parallel", "arbitrary"), phases=phases)


GELU_C = math.sqrt(2.0 / math.pi)
GELU_A = 0.044715


def _gelu(x):
    return 0.5 * x * (1.0 + jnp.tanh(GELU_C * (x + GELU_A * (x * x * x))))


def _gelu_and_grad(x):
    t = jnp.tanh(GELU_C * (x + GELU_A * (x * x * x)))
    y = 0.5 * x * (1.0 + t)
    dy = 0.5 * (1.0 + t) + 0.5 * x * (1.0 - t * t) * (GELU_C * (1.0 + 3.0 * GELU_A * (x * x)))
    return y, dy


def _layernorm_parts(g):
    mu = jnp.mean(g, axis=-1, keepdims=True)
    xc = g - mu
    rs = lax.rsqrt(jnp.mean(xc * xc, axis=-1, keepdims=True) + EPS)
    return xc * rs, rs


def _spatial_mix(w_ref, bcol_ref, vv_b, n_heads, n_chunks):
    tril = _causal_mask(CHUNK)
    cols = []
    for hh in range(n_heads):
        wc = jnp.where(tril, w_ref[hh], 0.0).astype(BF16)
        lanes = slice(hh * HEAD_DIM, (hh + 1) * HEAD_DIM)
        rows = [jnp.dot(wc, vv_b[c * CHUNK:(c + 1) * CHUNK, lanes], preferred_element_type=F32)
                + bcol_ref[:, hh:hh + 1] for c in range(n_chunks)]
        cols.append(jnp.concatenate(rows, axis=0))
    return jnp.concatenate(cols, axis=1)


def _mix_fwd(z, o, ln_g, ln_b, w_s, b_col, attn_g, gm_g, n_heads, *, name, tr=256):
    s = z.shape[0]
    dg = n_heads * HEAD_DIM
    tr = _tile(s, tr, CHUNK)
    n_chunks = tr // CHUNK

    def body(zu_ref, zv_ref, o_ref, lg_ref, lb_ref, w_ref, bcol_ref, ag_ref, gg_ref, out_ref):
        u = _gelu(zu_ref[...].astype(F32))
        xhat, _ = _layernorm_parts(_gelu(zv_ref[...].astype(F32)))
        vv = xhat * lg_ref[...] + lb_ref[...]
        gm = u * _spatial_mix(w_ref, bcol_ref, vv.astype(BF16), n_heads, n_chunks)
        rg = lax.rsqrt(jnp.mean(gm * gm, axis=-1, keepdims=True) + EPS)
        ov = o_ref[...].astype(F32)
        ra = lax.rsqrt(jnp.mean(ov * ov, axis=-1, keepdims=True) + EPS)
        out_ref[:, :dg] = ((ov * ra) * ag_ref[...]).astype(BF16)
        out_ref[:, dg:] = ((gm * rg) * gg_ref[...]).astype(BF16)

    vec = pl.BlockSpec((1, dg), lambda i: (0, 0))
    return pl.pallas_call(
        body, name=name, out_shape=jax.ShapeDtypeStruct((s, 2 * dg), BF16), grid=(s // tr,),
        in_specs=[pl.BlockSpec((tr, dg), lambda i: (i, 3)), pl.BlockSpec((tr, dg), lambda i: (i, 4)),
                  pl.BlockSpec((tr, dg), lambda i: (i, 0)), vec, vec,
                  pl.BlockSpec((n_heads, CHUNK, CHUNK), lambda i: (0, 0, 0)),
                  pl.BlockSpec((CHUNK, n_heads), lambda i: (0, 0)), vec, vec],
        out_specs=pl.BlockSpec((tr, 2 * dg), lambda i: (i, 0)),
        compiler_params=_params("parallel"),
    )(z, z, o, ln_g, ln_b, w_s, b_col, attn_g, gm_g)


def _mix_bwd(z, o, d_merged, ln_g, ln_b, w_s, b_col, attn_g, gm_g, n_heads, *, name, tr=256):
    s = z.shape[0]
    dg = n_heads * HEAD_DIM
    tr = _tile(s, tr, CHUNK)
    n_chunks = tr // CHUNK

    def body(zu_ref, zv_ref, o_ref, dm_ref, lg_ref, lb_ref, w_ref, bcol_ref, ag_ref, gg_ref,
             do_ref, dzu_ref, dzv_ref, dw_ref, dbcol_ref, dlg_ref, dlb_ref, dag_ref, dgg_ref):
        @pl.when(pl.program_id(0) == 0)
        def _():
            for ref in (dw_ref, dbcol_ref, dlg_ref, dlb_ref, dag_ref, dgg_ref):
                ref[...] = jnp.zeros_like(ref)

        d_o, dag_rows = _rms_bwd_rows(dm_ref[:, :dg], o_ref[...].astype(F32), ag_ref[...])
        do_ref[...] = d_o.astype(BF16)
        dag_ref[...] += jnp.sum(dag_rows, axis=0, keepdims=True)

        u, du_dz = _gelu_and_grad(zu_ref[...].astype(F32))
        gv, dgv_dz = _gelu_and_grad(zv_ref[...].astype(F32))
        xhat, rs = _layernorm_parts(gv)
        lg = lg_ref[...]
        vv_b = (xhat * lg + lb_ref[...]).astype(BF16)
        mix = _spatial_mix(w_ref, bcol_ref, vv_b, n_heads, n_chunks)
        gm = u * mix
        d_gm, dgg_rows = _rms_bwd_rows(dm_ref[:, dg:], gm, gg_ref[...])
        dgg_ref[...] += jnp.sum(dgg_rows, axis=0, keepdims=True)
        dzu_ref[...] = ((d_gm * mix) * du_dz).astype(BF16)
        d_mix = d_gm * u
        d_mix_b = d_mix.astype(BF16)

        tril = _causal_mask(CHUNK)
        lane = lax.broadcasted_iota(jnp.int32, (CHUNK, n_heads), 1)
        cols = []
        db = jnp.zeros((CHUNK, n_heads), F32)
        for hh in range(n_heads):
            wc = jnp.where(tril, w_ref[hh], 0.0).astype(BF16)
            lanes = slice(hh * HEAD_DIM, (hh + 1) * HEAD_DIM)
            dw = jnp.zeros((CHUNK, CHUNK), F32)
            dmix_sum = jnp.zeros((CHUNK, HEAD_DIM), F32)
            rows = []
            for c in range(n_chunks):
                rws = slice(c * CHUNK, (c + 1) * CHUNK)
                dmb = d_mix_b[rws, lanes]
                dw += lax.dot_general(dmb, vv_b[rws, lanes], (((1,), (1,)), ((), ())), preferred_element_type=F32)
                rows.append(lax.dot_general(wc, dmb, (((0,), (0,)), ((), ())), preferred_element_type=F32))
                dmix_sum += d_mix[rws, lanes]
            dw_ref[hh] += jnp.where(tril, dw, 0.0)
            db += jnp.where(lane == hh, jnp.sum(dmix_sum, axis=-1, keepdims=True), 0.0)
            cols.append(jnp.concatenate(rows, axis=0))
        dbcol_ref[...] += db
        d_vv = jnp.concatenate(cols, axis=1)

        dlg_ref[...] += jnp.sum(d_vv * xhat, axis=0, keepdims=True)
        dlb_ref[...] += jnp.sum(d_vv, axis=0, keepdims=True)
        d_xhat = d_vv * lg
        d_gv = rs * (d_xhat - jnp.mean(d_xhat, axis=-1, keepdims=True)
                     - xhat * jnp.mean(d_xhat * xhat, axis=-1, keepdims=True))
        dzv_ref[...] = (d_gv * dgv_dz).astype(BF16)

    vec = pl.BlockSpec((1, dg), lambda i: (0, 0))
    wspec = pl.BlockSpec((n_heads, CHUNK, CHUNK), lambda i: (0, 0, 0))
    bspec = pl.BlockSpec((CHUNK, n_heads), lambda i: (0, 0))
    rowb = pl.BlockSpec((tr, dg), lambda i: (i, 0))
    act = jax.ShapeDtypeStruct((s, dg), BF16)
    vshape = jax.ShapeDtypeStruct((1, dg), F32)
    return pl.pallas_call(
        body, name=name,
        out_shape=(act, act, act, jax.ShapeDtypeStruct((n_heads, CHUNK, CHUNK), F32),
                   jax.ShapeDtypeStruct((CHUNK, n_heads), F32), vshape, vshape, vshape, vshape),
        grid=(s // tr,),
        in_specs=[pl.BlockSpec((tr, dg), lambda i: (i, 3)), pl.BlockSpec((tr, dg), lambda i: (i, 4)),
                  rowb, pl.BlockSpec((tr, 2 * dg), lambda i: (i, 0)), vec, vec, wspec, bspec, vec, vec],
        out_specs=(rowb, rowb, rowb, wspec, bspec, vec, vec, vec, vec),
        compiler_params=_params("arbitrary"),
    )(z, z, o, d_merged, ln_g, ln_b, w_s, b_col, attn_g, gm_g)


def _place():
    x, y, c = lax.axis_index("x"), lax.axis_index("y"), lax.axis_index("c")
    other_chips = [(1 - x, y), (x, 1 - y), (1 - x, 1 - y)]
    return x, y, c, other_chips


def _remote(src, dst, send_sem, recv_sem, to):
    return pltpu.make_async_remote_copy(src_ref=src, dst_ref=dst, send_sem=send_sem, recv_sem=recv_sem,
                                        device_id=to, device_id_type=MESH)


def _cast_into_slot(w, place, *, name):
    rows, cols = w.shape
    tr, tc = _rc_tile(rows, cols)

    def body(place_ref, w_ref, o_ref):
        o_ref[...] = w_ref[...].astype(BF16)

    grid_spec = pltpu.PrefetchScalarGridSpec(
        num_scalar_prefetch=1, grid=(rows // tr, cols // tc),
        in_specs=[pl.BlockSpec((tr, tc), lambda i, j, pr: (i, j))],
        out_specs=pl.BlockSpec((None, tr, tc), lambda i, j, pr: (pr[0], i, j)),
    )
    return pl.pallas_call(
        body, name=name, grid_spec=grid_spec, out_shape=jax.ShapeDtypeStruct((N_CHIPS, rows, cols), BF16),
        compiler_params=_params("parallel", "parallel"),
    )(place, w)


def _exchange(phases, *, name):
    comm_in = [a for ph in phases for a in ph.arrays]
    comm_out = [jax.ShapeDtypeStruct(s.shape, s.dtype) for ph in phases for s in (ph.arrays if ph.in_place else ph.out_shapes)]
    aliases, at_in, at_out = {}, 0, 0
    for ph in phases:
        if ph.in_place:
            aliases.update({at_in + r: at_out + r for r in range(len(ph.arrays))})
        at_in, at_out = at_in + len(ph.arrays), at_out + ph.n_out
    n_sems = sum(ph.n_sems for ph in phases)

    def body(*refs):
        cin, cout = refs[:len(comm_in)], refs[len(comm_in):len(comm_in) + len(comm_out)]
        send_sems, recv_sems = refs[len(comm_in) + len(comm_out):]
        _run_phases(phases, ("start", "finish"), cin, cout, send_sems, recv_sems)

    return pl.pallas_call(
        body, name=name, out_shape=tuple(comm_out), in_specs=[ANY] * len(comm_in), out_specs=tuple([ANY] * len(comm_out)),
        input_output_aliases=aliases,
        scratch_shapes=[pltpu.SemaphoreType.DMA((n_sems,)), pltpu.SemaphoreType.DMA((n_sems,))],
    )(*comm_in)


GATHER_PARTS = 4


def _gather(bufs, stage, part=(0, GATHER_PARTS)):
    n = 3 * len(bufs)
    lo, hi = part

    def copies(outs, send, recv, d2d, incoming):
        x, y, c, chips = _place()
        for t, buf in enumerate(outs):
            half = buf.shape[2] // 2
            piece = half // GATHER_PARTS
            for k, (cx, cy) in enumerate(chips):
                i = 3 * t + k + (n if (d2d and stage == "both") else 0)
                cols = pl.ds(((1 - c) if (d2d and incoming) else c) * half + lo * piece, (hi - lo) * piece)
                blk = buf.at[(2 * cx + cy) if (d2d or incoming) else (2 * x + y), :, cols]
                yield _remote(blk, blk, send(i), recv(i), (x, y, 1 - c) if d2d else (cx, cy, c))

    def start(ins, outs, send, recv):
        for cp in copies(outs, send, recv, stage == "d2d", False):
            cp.start()

    def finish(ins, outs, send, recv):
        if stage == "both":
            for arrival, onward in zip(copies(outs, send, recv, False, True), copies(outs, send, recv, True, False)):
                arrival.wait_recv()
                onward.start()
        for cp in copies(outs, send, recv, stage != "ici", True):
            cp.wait_recv()
        for d2d in ((False, True) if stage == "both" else (stage == "d2d",)):
            for cp in copies(outs, send, recv, d2d, False):
                cp.wait_send()

    return _Phase(bufs, [], True, (2 if stage == "both" else 1) * n, start, finish)


def _merge(first, second):
    n_first = first.n_sems

    def later(sem):
        return lambda i: sem(n_first + i)

    def start(ins, outs, send, recv):
        first.start(ins, outs, send, recv)
        second.start(ins, outs, later(send), later(recv))

    def finish(ins, outs, send, recv):
        first.finish(ins, outs, send, recv)
        second.finish(ins, outs, later(send), later(recv))

    return _Phase(first.arrays, [], True, n_first + second.n_sems, start, finish)


def _swap_halves(grads):
    def copies(ins, outs, send, recv):
        x, y, c, _ = _place()
        for t, g in enumerate(ins):
            half = g.shape[2] // 2
            yield _remote(g.at[:, :, pl.ds((1 - c) * half, half)], outs[t], send(t), recv(t), (x, y, 1 - c))

    def start(ins, outs, send, recv):
        for cp in copies(ins, outs, send, recv):
            cp.start()

    def finish(ins, outs, send, recv):
        for cp in copies(ins, outs, send, recv):
            cp.wait()

    shapes = [jax.ShapeDtypeStruct((a.shape[0], a.shape[1], a.shape[2] // 2), a.dtype) for a in grads]
    return _Phase(grads, shapes, False, len(grads), start, finish)


def _add_halves(grad, received, place, *, name):
    ns, rows, half = received.shape
    tr, tc = _rc_tile(rows, half)
    per = half // tc

    def body(place_ref, g_ref, r_ref, o_ref):
        o_ref[...] = (g_ref[...].astype(F32) + r_ref[...].astype(F32)).astype(BF16)

    grid_spec = pltpu.PrefetchScalarGridSpec(
        num_scalar_prefetch=1, grid=(ns, rows // tr, per),
        in_specs=[pl.BlockSpec((None, tr, tc), lambda s, i, j, pr: (s, i, pr[1] * per + j)),
                  pl.BlockSpec((None, tr, tc), lambda s, i, j, pr: (s, i, j))],
        out_specs=pl.BlockSpec((None, tr, tc), lambda s, i, j, pr: (s, i, j)),
    )
    return pl.pallas_call(
        body, name=name, grid_spec=grid_spec, out_shape=jax.ShapeDtypeStruct(received.shape, BF16),
        compiler_params=_params("parallel", "parallel", "parallel"),
    )(place, grad, received)


def _send_partials(parts):
    def start(ins, outs, send, recv):
        x, y, c, chips = _place()
        for t, part in enumerate(ins):
            for k, (cx, cy) in enumerate(chips):
                _remote(part.at[2 * cx + cy], outs[t].at[2 * x + y], send(3 * t + k), recv(3 * t + k), (cx, cy, c)).start()

    def finish(ins, outs, send, recv):
        x, y, c, chips = _place()
        for t, part in enumerate(ins):
            for k, (cx, cy) in enumerate(chips):
                slot = outs[t].at[2 * cx + cy]
                _remote(slot, slot, send(3 * t + k), recv(3 * t + k), (cx, cy, c)).wait_recv()
        for t, part in enumerate(ins):
            for k, (cx, cy) in enumerate(chips):
                sent = part.at[2 * cx + cy]
                _remote(sent, sent, send(3 * t + k), recv(3 * t + k), (cx, cy, c)).wait_send()

    shapes = [jax.ShapeDtypeStruct(a.shape, a.dtype) for a in parts]
    return _Phase(parts, shapes, False, 3 * len(parts), start, finish)


def _sum_chips(parts, slots, place, *, name):
    ns, rows, half = slots.shape
    tr, tc = _rc_tile(rows, half)
    per = half // tc

    def body(place_ref, p_ref, s_ref, o_ref):
        acc = p_ref[...].astype(F32)
        for k in range(ns):
            acc = acc + jnp.where(place_ref[0] == k, 0.0, s_ref[k].astype(F32))
        o_ref[...] = acc

    grid_spec = pltpu.PrefetchScalarGridSpec(
        num_scalar_prefetch=1, grid=(rows // tr, per),
        in_specs=[pl.BlockSpec((None, tr, tc), lambda i, j, pr: (pr[0], i, j)),
                  pl.BlockSpec((ns, tr, tc), lambda i, j, pr: (0, i, j))],
        out_specs=pl.BlockSpec((tr, tc), lambda i, j, pr: (i, pr[1] * per + j)),
    )
    return pl.pallas_call(
        body, name=name, grid_spec=grid_spec, out_shape=jax.ShapeDtypeStruct((rows, 2 * half), F32),
        compiler_params=_params("parallel", "parallel"),
    )(place, parts, slots)


def _join_halves(bufs):
    def copies(outs, send, recv, incoming):
        x, y, c, _ = _place()
        for t, buf in enumerate(outs):
            half = buf.shape[1] // 2
            cols = buf.at[:, pl.ds(((1 - c) if incoming else c) * half, half)]
            yield _remote(cols, cols, send(t), recv(t), (x, y, 1 - c))

    def start(ins, outs, send, recv):
        for cp in copies(outs, send, recv, False):
            cp.start()

    def finish(ins, outs, send, recv):
        for cp in copies(outs, send, recv, True):
            cp.wait_recv()
        for cp in copies(outs, send, recv, False):
            cp.wait_send()

    return _Phase(bufs, [], True, len(bufs), start, finish)


def _allgather_small(buf, *, name):
    rows = buf.shape[0]

    def body(x_ref, out_ref, send_sems, recv_sems, local_sem):
        x, y, c, chips = _place()
        sibling = (x, y, 1 - c)

        def slot(px, py, pc):
            return out_ref.at[4 * px + 2 * py + pc]

        def copy(k, block, to, src=None):
            return _remote(slot(*block) if src is None else src, slot(*block), send_sems.at[k], recv_sems.at[k], to)

        mine = pltpu.make_async_copy(x_ref, slot(x, y, c), local_sem)
        mine.start()
        first = [copy(0, (x, y, c), sibling, src=x_ref)]
        first += [copy(1 + k, (x, y, c), (*chip, c), src=x_ref) for k, chip in enumerate(chips)]
        for cp in first:
            cp.start()
        passed = [copy(4 + k, (*chip, c), sibling) for k, chip in enumerate(chips)]
        for k, chip in enumerate(chips):
            copy(1 + k, (*chip, c), (x, y, c)).wait_recv()
            passed[k].start()
        copy(0, (x, y, 1 - c), (x, y, c)).wait_recv()
        for k, chip in enumerate(chips):
            copy(4 + k, (*chip, 1 - c), (x, y, c)).wait_recv()
        for cp in first + passed:
            cp.wait_send()
        mine.wait()

    return pl.pallas_call(
        body, name=name, out_shape=jax.ShapeDtypeStruct((N_DEV, rows, LANES), buf.dtype),
        in_specs=[pl.BlockSpec(memory_space=pltpu.VMEM)], out_specs=pl.BlockSpec(memory_space=pltpu.VMEM),
        scratch_shapes=[pltpu.SemaphoreType.DMA((7,)), pltpu.SemaphoreType.DMA((7,)), pltpu.SemaphoreType.DMA],
    )(buf)


def _adamw_math(w, g, m, v):
    m = ADAM_B1 * m + (1.0 - ADAM_B1) * g
    v = ADAM_B2 * v + (1.0 - ADAM_B2) * (g * g)
    m_hat = m / (1.0 - ADAM_B1 ** ADAM_STEP)
    v_hat = v / (1.0 - ADAM_B2 ** ADAM_STEP)
    delta = -ADAM_LR * (m_hat / (jnp.sqrt(v_hat) + ADAM_EPS) + ADAM_WD * w)
    return delta, m, v


def _adamw(w, g, m, v, *, name):
    rows, cols = w.shape
    tr, tc = _rc_tile(rows, cols)

    def body(w_ref, g_ref, m_ref, v_ref, d_ref, mo_ref, vo_ref):
        d_ref[...], mo_ref[...], vo_ref[...] = _adamw_math(w_ref[...], g_ref[...], m_ref[...], v_ref[...])

    blk = pl.BlockSpec((tr, tc), lambda i, j: (i, j))
    shape = jax.ShapeDtypeStruct((rows, cols), F32)
    return pl.pallas_call(
        body, name=name, out_shape=(shape, shape, shape), grid=(rows // tr, cols // tc),
        in_specs=[blk] * 4, out_specs=(blk, blk, blk), compiler_params=_params("parallel", "parallel"),
    )(w, g, m, v)


def _adamw_small(gathered, w, m, v, *, name):
    nd = gathered.shape[0]

    def body(gs_ref, w_ref, m_ref, v_ref, g_ref, d_ref, mo_ref, vo_ref):
        g = gs_ref[0]
        for k in range(1, nd):
            g = g + gs_ref[k]
        g_ref[...] = g
        d_ref[...], mo_ref[...], vo_ref[...] = _adamw_math(w_ref[...], g, m_ref[...], v_ref[...])

    shape = jax.ShapeDtypeStruct(w.shape, F32)
    return pl.pallas_call(body, name=name, out_shape=(shape, shape, shape, shape),
                          compiler_params=pltpu.CompilerParams(vmem_limit_bytes=VMEM_LIMIT_BYTES))(gathered, w, m, v)


def _pack(parts):
    flat = jnp.concatenate([p.reshape(-1).astype(F32) for p in parts])
    rows = -(-flat.shape[0] // (8 * LANES)) * 8
    return jnp.pad(flat, (0, rows * LANES - flat.shape[0])).reshape(rows, LANES)


def _unpack(buf, shapes):
    flat = buf.reshape(-1)
    out, pos = [], 0
    for shp in shapes:
        size = int(np.prod(shp))
        out.append(flat[pos:pos + size].reshape(shp))
        pos += size
    return out


ROW_BLOCK = 128


def _realign_rows(sources, segments, out_shape, *, name):
    n_slots, rows, cols = out_shape
    n_src = len(sources)
    per_slot = -(-rows // ROW_BLOCK)
    table = np.zeros((6, n_slots * per_slot, n_src), np.int32)
    for so in range(n_slots):
        for first, last, src, src_slot, src_row in segments[so]:
            for blk in range(first // ROW_BLOCK, (last - 1) // ROW_BLOCK + 1):
                lo, hi = max(first, blk * ROW_BLOCK), min(last, (blk + 1) * ROW_BLOCK)
                base = src_row + (blk * ROW_BLOCK - first)
                m0 = (base + lo - blk * ROW_BLOCK) // ROW_BLOCK
                at = so * per_slot + blk
                assert table[4, at, src] == 0, "two segments of one block share a source operand"
                table[:, at, src] = (src_slot, m0, base - m0 * ROW_BLOCK, lo - blk * ROW_BLOCK, hi - blk * ROW_BLOCK,
                                     min(2 * ROW_BLOCK, sources[src].shape[1] - m0 * ROW_BLOCK))
    last_block = [-(-a.shape[1] // ROW_BLOCK) - 1 for a in sources]

    def body(slot_ref, blk_ref, off_ref, lo_ref, hi_ref, valid_ref, *refs):
        o_ref, acc = refs[2 * n_src], refs[2 * n_src + 1]
        at = (pl.program_id(0) * per_slot + pl.program_id(1)) * n_src
        acc[...] = jnp.zeros_like(acc)
        for p in range(n_src):
            @pl.when(hi_ref[at + p] > lo_ref[at + p])
            def _():
                two = jnp.concatenate([refs[2 * p][...], refs[2 * p + 1][...]], axis=0)
                src_row = lax.broadcasted_iota(jnp.int32, two.shape, 0)
                two = jnp.where(src_row < valid_ref[at + p], two, jnp.zeros_like(two))
                r = lax.broadcasted_iota(jnp.int32, (ROW_BLOCK, 2 * ROW_BLOCK), 0)
                c = lax.broadcasted_iota(jnp.int32, (ROW_BLOCK, 2 * ROW_BLOCK), 1)
                place = (c == r + off_ref[at + p]) & (r >= lo_ref[at + p]) & (r < hi_ref[at + p])
                acc[...] += jnp.dot(place.astype(two.dtype), two, preferred_element_type=F32)
        o_ref[...] = acc[...].astype(o_ref.dtype)

    def src_spec(p, second):
        def index(so, i, slot_r, blk_r, off_r, lo_r, hi_r, valid_r):
            at = (so * per_slot + i) * n_src + p
            return slot_r[at], jnp.minimum(blk_r[at] + second, last_block[p]), 0
        return pl.BlockSpec((None, ROW_BLOCK, cols), index)

    grid_spec = pltpu.PrefetchScalarGridSpec(
        num_scalar_prefetch=6, grid=(n_slots, per_slot),
        in_specs=[src_spec(p, second) for p in range(n_src) for second in (0, 1)],
        out_specs=pl.BlockSpec((None, ROW_BLOCK, cols), lambda so, i, *_: (so, i, 0)),
        scratch_shapes=[pltpu.VMEM((ROW_BLOCK, cols), F32)],
    )
    flat = [jnp.asarray(table[k].reshape(-1)) for k in range(6)]
    return pl.pallas_call(
        body, name=name, grid_spec=grid_spec, out_shape=jax.ShapeDtypeStruct(out_shape, sources[0].dtype),
        compiler_params=_params("parallel", "arbitrary"),
    )(*flat, *[a for a in sources for _ in (0, 1)])


def _shard_rows(g, lo, hi):
    rs = g.shape[1]
    pieces = []
    for j in range(g.shape[0]):
        a, b = max(lo, j * rs), min(hi, (j + 1) * rs)
        if a < b:
            pieces.append(g[j, a - j * rs:b - j * rs])
    return pieces


def _rows_from_segments(segments, lo, hi):
    pieces = []
    for first, last, src, at in segments:
        a, b = max(lo, first), min(hi, last)
        if a < b:
            pieces.append(src[at + a - first:at + b - first])
    return jnp.concatenate(pieces, axis=0)


def kernel(x, norm_mix_g, w_in, b_f, gmlp_ln_g, gmlp_ln_b, w_s, b_s, attn_out_g, gmlp_out_g, w_out, norm_ffn_g, w_ff1, w_ff2, norm_final_g, loss_target, m_norm_mix_g, m_w_in, m_b_f, m_gmlp_ln_g, m_gmlp_ln_b, m_w_s, m_b_s, m_attn_out_g, m_gmlp_out_g, m_w_out, m_norm_ffn_g, m_w_ff1, m_w_ff2, m_norm_final_g, v_norm_mix_g, v_w_in, v_b_f, v_gmlp_ln_g, v_gmlp_ln_b, v_w_s, v_b_s, v_attn_out_g, v_gmlp_out_g, v_w_out, v_norm_ffn_g, v_w_ff1, v_w_ff2, v_norm_final_g):
    seq, d_model = x.shape[1], x.shape[2]
    d_attn = d_model // 2
    n_heads = d_attn // HEAD_DIM
    qkv = 3 * d_attn
    shard_cols = w_in.shape[2]
    assert N_CHIPS * shard_cols == qkv + n_heads + 2 * d_attn
    xs = x.reshape(seq, d_model)
    target = loss_target.reshape(seq, d_model)

    place = jnp.stack([2 * lax.axis_index("x") + lax.axis_index("y"), lax.axis_index("c")]).astype(jnp.int32)
    names = ["w_in", "w_out", "w_ff1", "w_ff2"]
    wt_in, mt_in, vt_in = w_in[0].T, m_w_in[0].T, v_w_in[0].T
    b_in, b_out, b_ff1, b_ff2 = [_cast_into_slot(w, place, name="cast_" + nm)
                                 for w, nm in zip((wt_in, w_out[0], w_ff1[0], w_ff2[0]), names)]
    (g_in,) = _exchange([_gather([b_in], "both")], name="allgather_w_in")
    n_cols = N_CHIPS * shard_cols
    gate_slot, gate_row = divmod(qkv, shard_cols)
    assert gate_row + n_heads <= shard_cols
    pieces = []
    for j in range(N_CHIPS):
        if j == gate_slot:
            pieces += [(j, 0, gate_row), (j, gate_row + n_heads, shard_cols - gate_row - n_heads)]
        else:
            pieces.append((j, 0, shard_cols))
    fwd_segments, at = [[]], 0
    for order, (j, src_row, size) in enumerate(pieces):
        fwd_segments[0].append((at, at + size, order % 2, j, src_row))
        at += size
    wt_main = _realign_rows([g_in, g_in], fwd_segments, (1, n_cols - n_heads, d_model), name="w_in_rows")[0]
    wt_f = jnp.pad(jnp.concatenate(_shard_rows(g_in, qkv, qkv + n_heads), axis=0), ((0, LANES - n_heads), (0, 0)))
    b_f_pad = jnp.pad(b_f, ((0, 0), (0, LANES - n_heads)))
    b_col = b_s[0].T

    h = _rmsnorm_fwd(xs, norm_mix_g, name="norm_mix")
    first, rest = (0, 1), (1, GATHER_PARTS)
    z, (b_out, b_ff1) = _matmul(h, wt_main, name="in_proj", out_dtype=BF16, trans_b=True,
                                phases=[_gather([b_out], "ici"), _gather([b_ff1], "ici", first)])
    zb, f_cum = _forget_fwd(h, wt_f, b_f_pad, name="forget_fwd")
    f_row = f_cum[:, :n_heads].T[:, None, :]
    (o, lse2), (b_ff1, b_out) = _attn_fwd(z, f_row, n_heads, name="attn_fwd",
                                          phases=[_gather([b_ff1], "ici", rest), _gather([b_out], "d2d")])
    merged = _mix_fwd(z, o, gmlp_ln_g, gmlp_ln_b, w_s[0], b_col, attn_out_g, gmlp_out_g, n_heads, name="mix_fwd")
    w_out_full = b_out.reshape(2 * d_attn, d_model)
    x1, (b_ff1, b_ff2) = _matmul(merged, w_out_full, name="out_proj", out_dtype=F32, residual=xs,
                                 phases=[_gather([b_ff1], "d2d"), _gather([b_ff2], "ici", first)])
    h2 = _rmsnorm_fwd(x1, norm_ffn_g, name="norm_ffn")
    a, (b_ff2,) = _matmul(h2, b_ff1, name="ff1", out_dtype=BF16, relu=True, b_sharded=True,
                          phases=[_merge(_gather([b_ff2], "both", rest), _gather([b_ff2], "d2d", first))])
    w_ff2_full = b_ff2.reshape(N_CHIPS * b_ff2.shape[1], d_model)
    x2, _ = _matmul(a, w_ff2_full, name="ff2", out_dtype=F32, square_lhs=True, residual=x1)
    dx2, dg_final, loss = _loss_and_final_bwd(x2, target, norm_final_g.reshape(1, d_model), name="loss_head")

    def pair_sum(g, r, nm):
        return _add_halves(g, r, place, name="grads_pair_sum_" + nm)

    def chip_sum(p, q, nm):
        return _sum_chips(p, q, place, name="grads_chip_sum_" + nm)

    dw_ff2, _ = _matmul(a, dx2, name="ff2_dw", out_dtype=BF16, trans_a=True, square_lhs=True)
    dw_ff2 = dw_ff2.reshape(N_CHIPS, -1, d_model)
    da, (r_ff2,) = _matmul(dx2, w_ff2_full, name="ff2_dlhs", out_dtype=BF16, trans_b=True, scale2_by=a,
                           phases=[_swap_halves([dw_ff2])])
    ps_ff2 = pair_sum(dw_ff2, r_ff2, "w_ff2")
    dh2, (q_ff2,) = _matmul(da, b_ff1, name="ff1_dlhs", out_dtype=F32, trans_b=True, b_sharded=True,
                            phases=[_send_partials([ps_ff2])])
    g_ff2 = chip_sum(ps_ff2, q_ff2, "w_ff2")
    dw_ff1, (g_ff2,) = _matmul(h2, da, name="ff1_dw", out_dtype=BF16, trans_a=True, out_sharded=True,
                               phases=[_join_halves([g_ff2])])
    (dx1, dg_ffn), _ = _rmsnorm_bwd(dh2, x1, dx2, norm_ffn_g, name="norm_ffn_bwd")
    dw_out, _ = _matmul(merged, dx1, name="out_proj_dw", out_dtype=BF16, trans_a=True)
    dw_out = dw_out.reshape(N_CHIPS, -1, d_model)
    d_merged, (r_ff1, r_out) = _matmul(dx1, w_out_full, name="out_proj_dlhs", out_dtype=F32, trans_b=True,
                                       phases=[_swap_halves([dw_ff1, dw_out])])
    ps_ff1, ps_out = pair_sum(dw_ff1, r_ff1, "w_ff1"), pair_sum(dw_out, r_out, "w_out")
    d_o, dzu, dzv, dw_s, db_col, dlg, dlb, dag, dgg = _mix_bwd(
        z, o, d_merged, gmlp_ln_g, gmlp_ln_b, w_s[0], b_col, attn_out_g, gmlp_out_g, n_heads, name="mix_bwd")
    (dq, dk, dv, d_f_key, d_f_query), (q_ff1, q_out) = _attn_bwd(
        z, o, d_o, lse2, f_row, n_heads, name="attn_bwd", phases=[_send_partials([ps_ff1, ps_out])])
    g_ff1, g_out = chip_sum(ps_ff1, q_ff1, "w_ff1"), chip_sum(ps_out, q_out, "w_out")
    d_f = d_f_key.reshape(n_heads, seq) + d_f_query.reshape(n_heads, seq)
    d_f_pad = jnp.pad(d_f.T, ((0, 0), (0, LANES - n_heads)))
    dzf, db_f = _forget_bwd(d_f_pad, zb, name="forget_bwd")
    dz = jnp.concatenate([dq, dk, dv, dzu, dzv], axis=1)
    dwt_main, (g_ff1, g_out) = _matmul(dz, h, name="in_proj_dw", out_dtype=BF16, trans_a=True,
                                       phases=[_join_halves([g_ff1, g_out])])
    dwt_f, _ = _matmul(dzf, h, name="gate_dw", out_dtype=BF16, trans_a=True)
    bwd_segments = []
    for j in range(N_CHIPS):
        first = j * shard_cols
        if j < gate_slot:
            bwd_segments.append([(0, shard_cols, 0, 0, first)])
        elif j > gate_slot:
            bwd_segments.append([(0, shard_cols, 0, 0, first - n_heads)])
        else:
            bwd_segments.append([(0, gate_row, 0, 0, first), (gate_row, gate_row + n_heads, 1, 0, 0),
                                 (gate_row + n_heads, shard_cols, 2, 0, qkv)])
    dw_in = _realign_rows([dwt_main[None], dwt_f[None], dwt_main[None]], bwd_segments,
                          (N_CHIPS, shard_cols, d_model), name="dw_in_rows")
    dh_gate, (r_in,) = _matmul(dzf, wt_f, name="gate_dlhs", out_dtype=F32, phases=[_swap_halves([dw_in])])
    ps_in = pair_sum(dw_in, r_in, "w_in")
    dh, (q_in,) = _matmul(dz, wt_main, name="in_proj_dlhs", out_dtype=F32, residual=dh_gate, tk=2560,
                          phases=[_send_partials([ps_in])])
    g_in_sum = chip_sum(ps_in, q_in, "w_in")
    (grad_x, dg_mix), _ = _rmsnorm_bwd(dh, xs, dx1, norm_mix_g, name="norm_mix_bwd")
    (g_in_sum,) = _exchange([_join_halves([g_in_sum])], name="grads_join_w_in")

    big = {}
    for nm, g, w, m, v in zip(names, (g_in_sum, g_out, g_ff1, g_ff2), (wt_in, w_out[0], w_ff1[0], w_ff2[0]),
                              (mt_in, m_w_out[0], m_w_ff1[0], m_w_ff2[0]), (vt_in, v_w_out[0], v_w_ff1[0], v_w_ff2[0])):
        d, mo, vo = _adamw(w, g, m, v, name="adamw_" + nm)
        big[nm] = tuple((t.T if nm == "w_in" else t)[None] for t in (g, d, mo, vo))

    small_w = [norm_mix_g, b_f, gmlp_ln_g, gmlp_ln_b, w_s, b_s, attn_out_g, gmlp_out_g, norm_ffn_g, norm_final_g]
    small_m = [m_norm_mix_g, m_b_f, m_gmlp_ln_g, m_gmlp_ln_b, m_w_s, m_b_s, m_attn_out_g, m_gmlp_out_g, m_norm_ffn_g, m_norm_final_g]
    small_v = [v_norm_mix_g, v_b_f, v_gmlp_ln_g, v_gmlp_ln_b, v_w_s, v_b_s, v_attn_out_g, v_gmlp_out_g, v_norm_ffn_g, v_norm_final_g]
    small_g = [dg_mix, db_f[:, :n_heads], dlg, dlb, dw_s, db_col.T, dag, dgg, dg_ffn, dg_final]
    shapes = [w.shape for w in small_w]
    gathered = _allgather_small(_pack(small_g), name="allgather_small_grads")
    packed = _adamw_small(gathered, _pack(small_w), _pack(small_m), _pack(small_v), name="adamw_small")
    sg, sd, sm, sv = (_unpack(p, shapes) for p in packed)
    small_names = ["norm_mix_g", "b_f", "gmlp_ln_g", "gmlp_ln_b", "w_s", "b_s", "attn_out_g", "gmlp_out_g", "norm_ffn_g", "norm_final_g"]
    small = {nm: (sg[i], sd[i], sm[i], sv[i]) for i, nm in enumerate(small_names)}

    order = ["norm_mix_g", "w_in", "b_f", "gmlp_ln_g", "gmlp_ln_b", "w_s", "b_s", "attn_out_g", "gmlp_out_g", "w_out",
             "norm_ffn_g", "w_ff1", "w_ff2", "norm_final_g"]
    result = {**small, **big}
    total_loss = lax.psum(loss[0, 0], ("x", "y", "c"))
    outs = [total_loss, grad_x.reshape(x.shape)]
    for part in range(4):
        outs += [result[nm][part] for nm in order]
    return tuple(outs)
```

```python
import functools
import math

import numpy as np
import jax
import jax.numpy as jnp
from jax import lax
from jax.experimental import pallas as pl
from jax.experimental.pallas import tpu as pltpu

HEAD_DIM = 128
CHUNK = 128
EPS = 1e-6
LANES = 128
N_CHIPS = 4
N_DEV = 8
VMEM_LIMIT_BYTES = 56 * 1024 * 1024

ADAM_LR = 0.001
ADAM_B1 = 0.9
ADAM_B2 = 0.999
ADAM_EPS = 1e-08
ADAM_WD = 0.01
ADAM_STEP = 10

BF16 = jnp.bfloat16
F32 = jnp.float32
MESH = pl.DeviceIdType.MESH
ANY = pl.BlockSpec(memory_space=pl.ANY)
NEG_BIG = -1e30


def _params(*sem):
    return pltpu.CompilerParams(dimension_semantics=tuple(sem), vmem_limit_bytes=VMEM_LIMIT_BYTES)


def _tile(n, pref, unit):
    t = (min(pref, n) // unit) * unit
    while t >= unit:
        if n % t == 0:
            return t
        t -= unit
    return n


def _rc_tile(rows, cols, pref_rows=256, pref_cols=256):
    if rows % 16 == 0:
        return _tile(rows, pref_rows, 16), cols
    return rows, _tile(cols, pref_cols, LANES)


class _Phase:
    def __init__(self, arrays, out_shapes, in_place, n_sems, start, finish):
        self.arrays, self.out_shapes, self.in_place = list(arrays), list(out_shapes), in_place
        self.n_sems, self.start, self.finish = n_sems, start, finish

    @property
    def n_out(self):
        return len(self.arrays) if self.in_place else len(self.out_shapes)


def _run_phases(phases, steps, comm_in, comm_out, send_sems, recv_sems):
    at_in = at_out = at_sem = 0
    for ph in phases:
        for step in steps:
            getattr(ph, step)(comm_in[at_in:at_in + len(ph.arrays)], comm_out[at_out:at_out + ph.n_out],
                              lambda i, base=at_sem: send_sems.at[base + i], lambda i, base=at_sem: recv_sems.at[base + i])
        at_in, at_out, at_sem = at_in + len(ph.arrays), at_out + ph.n_out, at_sem + ph.n_sems


def _call(body, *, name, grid, in_specs, out_specs, out_shape, operands, semantics, scratch_shapes=(),
          n_prefetch=0, phases=()):
    in_specs, out_specs, out_shape = list(in_specs), list(out_specs), list(out_shape)
    scratch_shapes = list(scratch_shapes)
    n_in, n_out, n_scr = len(operands) - n_prefetch, len(out_shape), len(scratch_shapes)
    comm_in = [a for ph in phases for a in ph.arrays]
    comm_out = [jax.ShapeDtypeStruct(s.shape, s.dtype) for ph in phases
                for s in (ph.arrays if ph.in_place else ph.out_shapes)]
    aliases, at_in, at_out = {}, n_prefetch + n_in, n_out
    for ph in phases:
        if ph.in_place:
            aliases.update({at_in + r: at_out + r for r in range(len(ph.arrays))})
        at_in, at_out = at_in + len(ph.arrays), at_out + ph.n_out
    n_sems = sum(ph.n_sems for ph in phases)

    def hosted(*refs):
        pre, rest = refs[:n_prefetch], refs[n_prefetch:]
        ins, rest = rest[:n_in], rest[n_in:]
        cin, rest = rest[:len(comm_in)], rest[len(comm_in):]
        outs, rest = rest[:n_out], rest[n_out:]
        cout, rest = rest[:len(comm_out)], rest[len(comm_out):]
        scr = rest[:n_scr]
        if phases:
            send_sems, recv_sems = rest[n_scr:]
            ids = [pl.program_id(ax) for ax in range(len(grid))]
            first = functools.reduce(jnp.logical_and, [i == 0 for i in ids])
            last = functools.reduce(jnp.logical_and, [i == g - 1 for i, g in zip(ids, grid)])

            @pl.when(first)
            def _():
                _run_phases(phases, ("start",), cin, cout, send_sems, recv_sems)

        body(*pre, *ins, *outs, *scr)
        if phases:
            @pl.when(last)
            def _():
                _run_phases(phases, ("finish",), cin, cout, send_sems, recv_sems)

    all_in = in_specs + [ANY] * len(comm_in)
    all_out = out_specs + [ANY] * len(comm_out)
    all_scr = scratch_shapes + ([pltpu.SemaphoreType.DMA((n_sems,)), pltpu.SemaphoreType.DMA((n_sems,))] if phases else [])
    if phases:
        semantics = ("arbitrary",) * len(grid)
    kwargs = dict(name=name, out_shape=tuple(out_shape + comm_out), compiler_params=_params(*semantics),
                  input_output_aliases=aliases)
    if n_prefetch:
        kwargs["grid_spec"] = pltpu.PrefetchScalarGridSpec(
            num_scalar_prefetch=n_prefetch, grid=grid, in_specs=all_in, out_specs=tuple(all_out), scratch_shapes=all_scr)
    else:
        kwargs.update(grid=grid, in_specs=all_in, out_specs=tuple(all_out), scratch_shapes=all_scr)
    res = pl.pallas_call(hosted, **kwargs)(*operands, *comm_in)
    return tuple(res[:n_out]), tuple(res[n_out:])


def _only(results):
    outs, comm = results
    return outs[0] if len(outs) == 1 else outs, comm


def _matmul(a, b, *, name, out_dtype, trans_a=False, trans_b=False, tm=1024, tn=1024, tk=2048,
            square_lhs=False, relu=False, residual=None, scale2_by=None,
            b_sharded=False, out_sharded=False, phases=()):
    m, k = (a.shape[1], a.shape[0]) if trans_a else a.shape
    if b_sharded:
        if trans_b:
            n, ks = b.shape[1], b.shape[2]
            assert N_CHIPS * ks == k
        else:
            ns = b.shape[2]
            n = N_CHIPS * ns
            assert b.shape[1] == k
    else:
        n = b.shape[0] if trans_b else b.shape[1]
        assert (b.shape[1] if trans_b else b.shape[0]) == k
    tm = _tile(m, tm, 128)
    tn = _tile(n // N_CHIPS if (out_sharded or (b_sharded and not trans_b)) else n, tn, 128)
    tk = _tile(k // N_CHIPS if (b_sharded and trans_b) else k, tk, 128)
    nk = k // tk

    if trans_a:
        a_spec = pl.BlockSpec((tk, tm), lambda i, j, kk: (kk, i))
    else:
        a_spec = pl.BlockSpec((tm, tk), lambda i, j, kk: (i, kk))
    if b_sharded and trans_b:
        per = ks // tk
        assert per * tk == ks
        b_spec = pl.BlockSpec((None, tn, tk), lambda i, j, kk: (kk // per, j, kk % per))
    elif b_sharded:
        per = ns // tn
        assert per * tn == ns
        b_spec = pl.BlockSpec((None, tk, tn), lambda i, j, kk: (j // per, kk, j % per))
    elif trans_b:
        b_spec = pl.BlockSpec((tn, tk), lambda i, j, kk: (j, kk))
    else:
        b_spec = pl.BlockSpec((tk, tn), lambda i, j, kk: (kk, j))
    if out_sharded:
        ns_out = n // N_CHIPS
        per_o = ns_out // tn
        assert per_o * tn == ns_out
        out_shape = jax.ShapeDtypeStruct((N_CHIPS, m, ns_out), out_dtype)
        o_spec = pl.BlockSpec((None, tm, tn), lambda i, j, kk: (j // per_o, i, j % per_o))
    else:
        out_shape = jax.ShapeDtypeStruct((m, n), out_dtype)
        o_spec = pl.BlockSpec((tm, tn), lambda i, j, kk: (i, j))
    mn_spec = pl.BlockSpec((tm, tn), lambda i, j, kk: (i, j))

    operands, in_specs = [a, b], [a_spec, b_spec]
    if scale2_by is not None:
        operands.append(scale2_by)
        in_specs.append(mn_spec)
    if residual is not None:
        operands.append(residual)
        in_specs.append(mn_spec)
    dims = (((0 if trans_a else 1,), (1 if trans_b else 0,)), ((), ()))

    def body(*refs):
        a_ref, b_ref = refs[0], refs[1]
        pos = 2
        scale_ref = res_ref = None
        if scale2_by is not None:
            scale_ref = refs[pos]
            pos += 1
        if residual is not None:
            res_ref = refs[pos]
            pos += 1
        o_ref = refs[pos]
        kk = pl.program_id(2)

        av = a_ref[...]
        if square_lhs:
            av = av.astype(F32)
            av = av * av
        part = lax.dot_general(av.astype(BF16), b_ref[...].astype(BF16), dims, preferred_element_type=F32)

        def finish(r):
            if relu:
                r = jnp.maximum(r, 0.0)
            if scale_ref is not None:
                r = r * (2.0 * scale_ref[...].astype(F32))
            if res_ref is not None:
                r = r + res_ref[...].astype(F32)
            o_ref[...] = r.astype(out_dtype)

        if nk == 1:
            finish(part)
        else:
            acc_ref = refs[pos + 1]

            @pl.when(kk == 0)
            def _():
                acc_ref[...] = part

            @pl.when(jnp.logical_and(kk > 0, kk < nk - 1))
            def _():
                acc_ref[...] += part

            @pl.when(kk == nk - 1)
            def _():
                finish(acc_ref[...] + part)

    return _only(_call(
        body, name=name, out_shape=[out_shape], grid=(m // tm, n // tn, nk),
        in_specs=in_specs, out_specs=[o_spec], operands=operands,
        scratch_shapes=[pltpu.VMEM((tm, tn), F32)] if nk > 1 else [],
        semantics=("parallel", "parallel", "arbitrary"), phases=phases))


def _rmsnorm_fwd(x, g, *, name, tr=512, phases=()):
    s, d = x.shape
    tr = _tile(s, tr, 8)

    def body(x_ref, g_ref, o_ref):
        xv = x_ref[...]
        r = lax.rsqrt(jnp.mean(xv * xv, axis=-1, keepdims=True) + EPS)
        o_ref[...] = ((xv * r) * g_ref[...]).astype(BF16)

    return _only(_call(
        body, name=name, out_shape=[jax.ShapeDtypeStruct((s, d), BF16)], grid=(s // tr,),
        in_specs=[pl.BlockSpec((tr, d), lambda i: (i, 0)), pl.BlockSpec((1, d), lambda i: (0, 0))],
        out_specs=[pl.BlockSpec((tr, d), lambda i: (i, 0))], operands=[x, g],
        semantics=("parallel",), phases=phases))


def _rms_bwd_rows(dy, xv, g):
    d = xv.shape[-1]
    r = lax.rsqrt(jnp.mean(xv * xv, axis=-1, keepdims=True) + EPS)
    gdy = dy * g
    dot = jnp.sum(gdy * xv, axis=-1, keepdims=True)
    dx = gdy * r - xv * (r * r * r) * (dot / d)
    return dx, dy * (xv * r)


def _rmsnorm_bwd(dy, x, res, g, *, name, tr=256, phases=()):
    s, d = x.shape
    tr = _tile(s, tr, 8)

    def body(dy_ref, x_ref, res_ref, g_ref, dx_ref, dg_ref):
        @pl.when(pl.program_id(0) == 0)
        def _():
            dg_ref[...] = jnp.zeros_like(dg_ref)

        dx, dg_rows = _rms_bwd_rows(dy_ref[...].astype(F32), x_ref[...], g_ref[...])
        dx_ref[...] = res_ref[...] + dx
        dg_ref[...] += jnp.sum(dg_rows, axis=0, keepdims=True)

    row = pl.BlockSpec((tr, d), lambda i: (i, 0))
    vec = pl.BlockSpec((1, d), lambda i: (0, 0))
    return _call(
        body, name=name, out_shape=[jax.ShapeDtypeStruct((s, d), F32), jax.ShapeDtypeStruct((1, d), F32)],
        grid=(s // tr,), in_specs=[row, row, row, vec], out_specs=[row, vec], operands=[dy, x, res, g],
        semantics=("arbitrary",), phases=phases)


def _loss_and_final_bwd(x2, target, g, *, name, tr=256):
    s, d = x2.shape
    tr = _tile(s, tr, 8)

    def body(x_ref, t_ref, g_ref, dx_ref, dg_ref, loss_ref):
        @pl.when(pl.program_id(0) == 0)
        def _():
            dg_ref[...] = jnp.zeros_like(dg_ref)
            loss_ref[...] = jnp.zeros_like(loss_ref)

        xv, gv = x_ref[...], g_ref[...]
        r = lax.rsqrt(jnp.mean(xv * xv, axis=-1, keepdims=True) + EPS)
        err = (xv * r) * gv - t_ref[...]
        row_loss = jnp.mean(err * err, axis=-1, keepdims=True)
        loss_ref[...] += 0.5 * jnp.sum(row_loss, axis=0, keepdims=True)
        dx, dg_rows = _rms_bwd_rows(err / d, xv, gv)
        dx_ref[...] = dx
        dg_ref[...] += jnp.sum(dg_rows, axis=0, keepdims=True)

    row = pl.BlockSpec((tr, d), lambda i: (i, 0))
    vec = pl.BlockSpec((1, d), lambda i: (0, 0))
    one = pl.BlockSpec((1, 1), lambda i: (0, 0))
    return pl.pallas_call(
        body, name=name,
        out_shape=(jax.ShapeDtypeStruct((s, d), F32), jax.ShapeDtypeStruct((1, d), F32),
                   jax.ShapeDtypeStruct((1, 1), F32)),
        grid=(s // tr,), in_specs=[row, row, vec], out_specs=(row, vec, one),
        compiler_params=_params("arbitrary"),
    )(x2, target, g)


def _tri_ones(n, lower):
    r = lax.broadcasted_iota(jnp.int32, (n, n), 0)
    c = lax.broadcasted_iota(jnp.int32, (n, n), 1)
    return jnp.where((c <= r) if lower else (c >= r), 1.0, 0.0).astype(F32)


def _forget_fwd(h, w_f, b_f, *, name, tr=256):
    s, d = h.shape
    tr = _tile(s, tr, 8)

    def body(h_ref, w_ref, b_ref, zb_ref, f_ref, carry):
        @pl.when(pl.program_id(0) == 0)
        def _():
            carry[...] = jnp.zeros_like(carry)

        zb = lax.dot_general(h_ref[...], w_ref[...], (((1,), (1,)), ((), ())), preferred_element_type=F32) + b_ref[...]
        zb_ref[...] = zb
        log_f = jnp.minimum(zb, 0.0) - jnp.log(1.0 + jnp.exp(-jnp.abs(zb)))
        run = jnp.dot(_tri_ones(tr, True), log_f, preferred_element_type=F32,
                      precision=lax.Precision.HIGHEST) + carry[...]
        f_ref[...] = run
        carry[...] = run[tr - 1:tr, :]

    row = pl.BlockSpec((tr, LANES), lambda i: (i, 0))
    return pl.pallas_call(
        body, name=name,
        out_shape=(jax.ShapeDtypeStruct((s, LANES), F32), jax.ShapeDtypeStruct((s, LANES), F32)),
        grid=(s // tr,),
        in_specs=[pl.BlockSpec((tr, d), lambda i: (i, 0)), pl.BlockSpec((LANES, d), lambda i: (0, 0)),
                  pl.BlockSpec((1, LANES), lambda i: (0, 0))],
        out_specs=(row, row), scratch_shapes=[pltpu.VMEM((1, LANES), F32)],
        compiler_params=_params("arbitrary"),
    )(h, w_f, b_f)


def _forget_bwd(d_f, zb, *, name, tr=256):
    s = zb.shape[0]
    tr = _tile(s, tr, 8)
    nb = s // tr

    def body(df_ref, zb_ref, dz_ref, db_ref, carry):
        @pl.when(pl.program_id(0) == 0)
        def _():
            carry[...] = jnp.zeros_like(carry)
            db_ref[...] = jnp.zeros_like(db_ref)

        run = jnp.dot(_tri_ones(tr, False), df_ref[...], preferred_element_type=F32,
                      precision=lax.Precision.HIGHEST) + carry[...]
        carry[...] = run[0:1, :]
        dz = run / (1.0 + jnp.exp(zb_ref[...]))
        dz_ref[...] = dz.astype(BF16)
        db_ref[...] += jnp.sum(dz, axis=0, keepdims=True)

    row = pl.BlockSpec((tr, LANES), lambda i: (nb - 1 - i, 0))
    return pl.pallas_call(
        body, name=name,
        out_shape=(jax.ShapeDtypeStruct((s, LANES), BF16), jax.ShapeDtypeStruct((1, LANES), F32)),
        grid=(nb,), in_specs=[row, row], out_specs=(row, pl.BlockSpec((1, LANES), lambda i: (0, 0))),
        scratch_shapes=[pltpu.VMEM((1, LANES), F32)],
        compiler_params=_params("arbitrary"),
    )(d_f, zb)


def _pairs(nblk, by_kv):
    if by_kv:
        pr = [(i, j) for j in range(nblk) for i in range(j, nblk)]
    else:
        pr = [(i, j) for i in range(nblk) for j in range(i + 1)]
    return (jnp.asarray(np.array([p[0] for p in pr], np.int32)), jnp.asarray(np.array([p[1] for p in pr], np.int32)))


def _causal_mask(t):
    r = lax.broadcasted_iota(jnp.int32, (t, t), 0)
    c = lax.broadcasted_iota(jnp.int32, (t, t), 1)
    return c <= r


LOG2E = math.log2(math.e)
QK_TO_LOG2 = LOG2E / math.sqrt(HEAD_DIM)


def _attn_logits2(q, k, fk_row):
    sc = lax.dot_general(q, k, (((1,), (1,)), ((), ())), preferred_element_type=F32)
    return sc * QK_TO_LOG2 - fk_row * LOG2E


def _attn_fwd(z, f_row, n_heads, *, name, tb=1024, phases=()):
    s = z.shape[0]
    tb = _tile(s, tb, 128)
    nblk = s // tb
    rep = tb // LANES
    qi, kj = _pairs(nblk, by_kv=False)

    def body(qi_ref, kj_ref, q_ref, k_ref, v_ref, fk_ref, o_ref, lse_ref, m_sc, l_sc, acc_sc):
        p = pl.program_id(1)
        i, j = qi_ref[p], kj_ref[p]

        @pl.when(j == 0)
        def _():
            m_sc[...] = jnp.full_like(m_sc, NEG_BIG)
            l_sc[...] = jnp.zeros_like(l_sc)
            acc_sc[...] = jnp.zeros_like(acc_sc)

        def update(masked):
            s2 = _attn_logits2(q_ref[...], k_ref[...], fk_ref[...])
            if masked:
                s2 = jnp.where(_causal_mask(tb), s2, NEG_BIG)
            m_old = m_sc[...]
            m_new = jnp.maximum(m_old, jnp.max(s2, axis=-1, keepdims=True))
            alpha = jnp.exp2(m_old - m_new)
            pv = jnp.exp2(s2 - jnp.tile(m_new, (1, rep)))
            l_sc[...] = alpha * l_sc[...] + jnp.sum(pv, axis=-1, keepdims=True)
            acc_sc[...] = alpha * acc_sc[...] + jnp.dot(pv.astype(BF16), v_ref[...], preferred_element_type=F32)
            m_sc[...] = m_new

        @pl.when(j < i)
        def _():
            update(False)

        @pl.when(j == i)
        def _():
            update(True)
            o_ref[...] = (acc_sc[...] / l_sc[...]).astype(BF16)
            lse_ref[...] = m_sc[...] + jnp.log2(l_sc[...])

    h = n_heads
    return _call(
        body, name=name, n_prefetch=2, grid=(h, int(qi.shape[0])),
        in_specs=[
            pl.BlockSpec((tb, HEAD_DIM), lambda hh, p, qi_r, kj_r: (qi_r[p], hh)),
            pl.BlockSpec((tb, HEAD_DIM), lambda hh, p, qi_r, kj_r: (kj_r[p], h + hh)),
            pl.BlockSpec((tb, HEAD_DIM), lambda hh, p, qi_r, kj_r: (kj_r[p], 2 * h + hh)),
            pl.BlockSpec((None, 1, tb), lambda hh, p, qi_r, kj_r: (hh, 0, kj_r[p])),
        ],
        out_specs=[
            pl.BlockSpec((tb, HEAD_DIM), lambda hh, p, qi_r, kj_r: (qi_r[p], hh)),
            pl.BlockSpec((None, tb, LANES), lambda hh, p, qi_r, kj_r: (hh, qi_r[p], 0)),
        ],
        scratch_shapes=[pltpu.VMEM((tb, LANES), F32), pltpu.VMEM((tb, LANES), F32), pltpu.VMEM((tb, HEAD_DIM), F32)],
        out_shape=[jax.ShapeDtypeStruct((s, h * HEAD_DIM), BF16), jax.ShapeDtypeStruct((h, s, LANES), F32)],
        operands=[qi, kj, z, z, z, f_row], semantics=("parallel", "arbitrary"), phases=phases)


def _attn_bwd(z, o, d_o, lse2, f_row, n_heads, *, name, tb=1024, phases=()):
    s = z.shape[0]
    tb = _tile(s, tb, 128)
    nblk = s // tb
    rep = tb // LANES
    qi, kj = _pairs(nblk, by_kv=True)
    n_pairs = int(qi.shape[0])
    scale = 1.0 / math.sqrt(HEAD_DIM)
    h = n_heads

    def body(qi_ref, kj_ref, q_ref, k_ref, v_ref, o_ref, do_ref, lse_ref, fk_ref,
             dq_ref, dk_ref, dv_ref, df_ref, dfq_ref, dq_sc, dk_sc, dv_sc, df_sc, dfq_sc):
        p = pl.program_id(1)
        i, j = qi_ref[p], kj_ref[p]

        @pl.when(p == 0)
        def _():
            dq_sc[...] = jnp.zeros_like(dq_sc)
            dfq_sc[...] = jnp.zeros_like(dfq_sc)

        @pl.when(i == j)
        def _():
            dk_sc[...] = jnp.zeros_like(dk_sc)
            dv_sc[...] = jnp.zeros_like(dv_sc)
            df_sc[...] = jnp.zeros_like(df_sc)

        def update(masked):
            q, k, v, do = q_ref[...], k_ref[...], v_ref[...], do_ref[...]
            delta = jnp.sum(do.astype(F32) * o_ref[...].astype(F32), axis=-1, keepdims=True)
            pv = jnp.exp2(_attn_logits2(q, k, fk_ref[...]) - jnp.tile(lse_ref[...], (1, rep)))
            if masked:
                pv = jnp.where(_causal_mask(tb), pv, 0.0)
            dp = lax.dot_general(do, v, (((1,), (1,)), ((), ())), preferred_element_type=F32)
            ds = pv * (dp - delta)
            ds_b = ds.astype(BF16)
            dv_sc[...] += lax.dot_general(pv.astype(BF16), do, (((0,), (0,)), ((), ())), preferred_element_type=F32)
            dk_sc[...] += lax.dot_general(ds_b, q, (((0,), (0,)), ((), ())), preferred_element_type=F32)
            rows = pl.ds(pl.multiple_of(i * tb, tb), tb)
            dq_sc[rows, :] += jnp.dot(ds_b, k, preferred_element_type=F32)
            df_sc[...] -= jnp.sum(ds, axis=0, keepdims=True)
            dfq_sc[rows, :] += jnp.broadcast_to(jnp.sum(ds, axis=1, keepdims=True), (tb, LANES))

        @pl.when(i > j)
        def _():
            update(False)

        @pl.when(i == j)
        def _():
            update(True)

        @pl.when(i == nblk - 1)
        def _():
            dk_ref[...] = (dk_sc[...] * scale).astype(BF16)
            dv_ref[...] = dv_sc[...].astype(BF16)
            df_ref[...] = df_sc[...]

        @pl.when(p == n_pairs - 1)
        def _():
            dq_ref[...] = (dq_sc[...] * scale).astype(BF16)
            dfq_ref[...] = jnp.transpose(dfq_sc[...])[0:1, :]

    qblk = lambda off: pl.BlockSpec((tb, HEAD_DIM), lambda hh, p, qi_r, kj_r: (qi_r[p], off + hh))
    kblk = lambda off: pl.BlockSpec((tb, HEAD_DIM), lambda hh, p, qi_r, kj_r: (kj_r[p], off + hh))
    qrep = pl.BlockSpec((None, tb, LANES), lambda hh, p, qi_r, kj_r: (hh, qi_r[p], 0))
    krow = pl.BlockSpec((None, 1, tb), lambda hh, p, qi_r, kj_r: (hh, 0, kj_r[p]))
    act = jax.ShapeDtypeStruct((s, h * HEAD_DIM), BF16)
    return _call(
        body, name=name, n_prefetch=2, grid=(h, n_pairs),
        in_specs=[qblk(0), kblk(h), kblk(2 * h), qblk(0), qblk(0), qrep, krow],
        out_specs=[
            pl.BlockSpec((s, HEAD_DIM), lambda hh, p, qi_r, kj_r: (0, hh)),
            kblk(0), kblk(0), krow,
            pl.BlockSpec((None, 1, s), lambda hh, p, qi_r, kj_r: (hh, 0, 0)),
        ],
        scratch_shapes=[pltpu.VMEM((s, HEAD_DIM), F32), pltpu.VMEM((tb, HEAD_DIM), F32),
                        pltpu.VMEM((tb, HEAD_DIM), F32), pltpu.VMEM((1, tb), F32), pltpu.VMEM((s, LANES), F32)],
        out_shape=[act, act, act, jax.ShapeDtypeStruct((h, 1, s), F32), jax.ShapeDtypeStruct((h, 1, s), F32)],
        operands=[qi, kj, z, z, z, o, d_o, lse2, f_row], semantics=("parallel", "arbitrary"), phases=phases)


GELU_C = math.sqrt(2.0 / math.pi)
GELU_A = 0.044715


def _gelu(x):
    return 0.5 * x * (1.0 + jnp.tanh(GELU_C * (x + GELU_A * (x * x * x))))


def _gelu_and_grad(x):
    t = jnp.tanh(GELU_C * (x + GELU_A * (x * x * x)))
    y = 0.5 * x * (1.0 + t)
    dy = 0.5 * (1.0 + t) + 0.5 * x * (1.0 - t * t) * (GELU_C * (1.0 + 3.0 * GELU_A * (x * x)))
    return y, dy


def _layernorm_parts(g):
    mu = jnp.mean(g, axis=-1, keepdims=True)
    xc = g - mu
    rs = lax.rsqrt(jnp.mean(xc * xc, axis=-1, keepdims=True) + EPS)
    return xc * rs, rs


def _spatial_mix(w_ref, bcol_ref, vv_b, n_heads, n_chunks):
    tril = _causal_mask(CHUNK)
    cols = []
    for hh in range(n_heads):
        wc = jnp.where(tril, w_ref[hh], 0.0).astype(BF16)
        lanes = slice(hh * HEAD_DIM, (hh + 1) * HEAD_DIM)
        rows = [jnp.dot(wc, vv_b[c * CHUNK:(c + 1) * CHUNK, lanes], preferred_element_type=F32)
                + bcol_ref[:, hh:hh + 1] for c in range(n_chunks)]
        cols.append(jnp.concatenate(rows, axis=0))
    return jnp.concatenate(cols, axis=1)


def _mix_fwd(z, o, ln_g, ln_b, w_s, b_col, attn_g, gm_g, n_heads, *, name, tr=256):
    s = z.shape[0]
    dg = n_heads * HEAD_DIM
    tr = _tile(s, tr, CHUNK)
    n_chunks = tr // CHUNK

    def body(zu_ref, zv_ref, o_ref, lg_ref, lb_ref, w_ref, bcol_ref, ag_ref, gg_ref, out_ref):
        u = _gelu(zu_ref[...].astype(F32))
        xhat, _ = _layernorm_parts(_gelu(zv_ref[...].astype(F32)))
        vv = xhat * lg_ref[...] + lb_ref[...]
        gm = u * _spatial_mix(w_ref, bcol_ref, vv.astype(BF16), n_heads, n_chunks)
        rg = lax.rsqrt(jnp.mean(gm * gm, axis=-1, keepdims=True) + EPS)
        ov = o_ref[...].astype(F32)
        ra = lax.rsqrt(jnp.mean(ov * ov, axis=-1, keepdims=True) + EPS)
        out_ref[:, :dg] = ((ov * ra) * ag_ref[...]).astype(BF16)
        out_ref[:, dg:] = ((gm * rg) * gg_ref[...]).astype(BF16)

    vec = pl.BlockSpec((1, dg), lambda i: (0, 0))
    return pl.pallas_call(
        body, name=name, out_shape=jax.ShapeDtypeStruct((s, 2 * dg), BF16), grid=(s // tr,),
        in_specs=[pl.BlockSpec((tr, dg), lambda i: (i, 3)), pl.BlockSpec((tr, dg), lambda i: (i, 4)),
                  pl.BlockSpec((tr, dg), lambda i: (i, 0)), vec, vec,
                  pl.BlockSpec((n_heads, CHUNK, CHUNK), lambda i: (0, 0, 0)),
                  pl.BlockSpec((CHUNK, n_heads), lambda i: (0, 0)), vec, vec],
        out_specs=pl.BlockSpec((tr, 2 * dg), lambda i: (i, 0)),
        compiler_params=_params("parallel"),
    )(z, z, o, ln_g, ln_b, w_s, b_col, attn_g, gm_g)


def _mix_bwd(z, o, d_merged, ln_g, ln_b, w_s, b_col, attn_g, gm_g, n_heads, *, name, tr=256):
    s = z.shape[0]
    dg = n_heads * HEAD_DIM
    tr = _tile(s, tr, CHUNK)
    n_chunks = tr // CHUNK

    def body(zu_ref, zv_ref, o_ref, dm_ref, lg_ref, lb_ref, w_ref, bcol_ref, ag_ref, gg_ref,
             do_ref, dzu_ref, dzv_ref, dw_ref, dbcol_ref, dlg_ref, dlb_ref, dag_ref, dgg_ref):
        @pl.when(pl.program_id(0) == 0)
        def _():
            for ref in (dw_ref, dbcol_ref, dlg_ref, dlb_ref, dag_ref, dgg_ref):
                ref[...] = jnp.zeros_like(ref)

        d_o, dag_rows = _rms_bwd_rows(dm_ref[:, :dg], o_ref[...].astype(F32), ag_ref[...])
        do_ref[...] = d_o.astype(BF16)
        dag_ref[...] += jnp.sum(dag_rows, axis=0, keepdims=True)

        u, du_dz = _gelu_and_grad(zu_ref[...].astype(F32))
        gv, dgv_dz = _gelu_and_grad(zv_ref[...].astype(F32))
        xhat, rs = _layernorm_parts(gv)
        lg = lg_ref[...]
        vv_b = (xhat * lg + lb_ref[...]).astype(BF16)
        mix = _spatial_mix(w_ref, bcol_ref, vv_b, n_heads, n_chunks)
        gm = u * mix
        d_gm, dgg_rows = _rms_bwd_rows(dm_ref[:, dg:], gm, gg_ref[...])
        dgg_ref[...] += jnp.sum(dgg_rows, axis=0, keepdims=True)
        dzu_ref[...] = ((d_gm * mix) * du_dz).astype(BF16)
        d_mix = d_gm * u
        d_mix_b = d_mix.astype(BF16)

        tril = _causal_mask(CHUNK)
        lane = lax.broadcasted_iota(jnp.int32, (CHUNK, n_heads), 1)
        cols = []
        db = jnp.zeros((CHUNK, n_heads), F32)
        for hh in range(n_heads):
            wc = jnp.where(tril, w_ref[hh], 0.0).astype(BF16)
            lanes = slice(hh * HEAD_DIM, (hh + 1) * HEAD_DIM)
            dw = jnp.zeros((CHUNK, CHUNK), F32)
            dmix_sum = jnp.zeros((CHUNK, HEAD_DIM), F32)
            rows = []
            for c in range(n_chunks):
                rws = slice(c * CHUNK, (c + 1) * CHUNK)
                dmb = d_mix_b[rws, lanes]
                dw += lax.dot_general(dmb, vv_b[rws, lanes], (((1,), (1,)), ((), ())), preferred_element_type=F32)
                rows.append(lax.dot_general(wc, dmb, (((0,), (0,)), ((), ())), preferred_element_type=F32))
                dmix_sum += d_mix[rws, lanes]
            dw_ref[hh] += jnp.where(tril, dw, 0.0)
            db += jnp.where(lane == hh, jnp.sum(dmix_sum, axis=-1, keepdims=True), 0.0)
            cols.append(jnp.concatenate(rows, axis=0))
        dbcol_ref[...] += db
        d_vv = jnp.concatenate(cols, axis=1)

        dlg_ref[...] += jnp.sum(d_vv * xhat, axis=0, keepdims=True)
        dlb_ref[...] += jnp.sum(d_vv, axis=0, keepdims=True)
        d_xhat = d_vv * lg
        d_gv = rs * (d_xhat - jnp.mean(d_xhat, axis=-1, keepdims=True)
                     - xhat * jnp.mean(d_xhat * xhat, axis=-1, keepdims=True))
        dzv_ref[...] = (d_gv * dgv_dz).astype(BF16)

    vec = pl.BlockSpec((1, dg), lambda i: (0, 0))
    wspec = pl.BlockSpec((n_heads, CHUNK, CHUNK), lambda i: (0, 0, 0))
    bspec = pl.BlockSpec((CHUNK, n_heads), lambda i: (0, 0))
    rowb = pl.BlockSpec((tr, dg), lambda i: (i, 0))
    act = jax.ShapeDtypeStruct((s, dg), BF16)
    vshape = jax.ShapeDtypeStruct((1, dg), F32)
    return pl.pallas_call(
        body, name=name,
        out_shape=(act, act, act, jax.ShapeDtypeStruct((n_heads, CHUNK, CHUNK), F32),
                   jax.ShapeDtypeStruct((CHUNK, n_heads), F32), vshape, vshape, vshape, vshape),
        grid=(s // tr,),
        in_specs=[pl.BlockSpec((tr, dg), lambda i: (i, 3)), pl.BlockSpec((tr, dg), lambda i: (i, 4)),
                  rowb, pl.BlockSpec((tr, 2 * dg), lambda i: (i, 0)), vec, vec, wspec, bspec, vec, vec],
        out_specs=(rowb, rowb, rowb, wspec, bspec, vec, vec, vec, vec),
        compiler_params=_params("arbitrary"),
    )(z, z, o, d_merged, ln_g, ln_b, w_s, b_col, attn_g, gm_g)


def _place():
    x, y, c = lax.axis_index("x"), lax.axis_index("y"), lax.axis_index("c")
    other_chips = [(1 - x, y), (x, 1 - y), (1 - x, 1 - y)]
    return x, y, c, other_chips


def _remote(src, dst, send_sem, recv_sem, to):
    return pltpu.make_async_remote_copy(src_ref=src, dst_ref=dst, send_sem=send_sem, recv_sem=recv_sem,
                                        device_id=to, device_id_type=MESH)


def _cast_into_slot(w, place, *, name, phases=()):
    rows, cols = w.shape
    tr, tc = _rc_tile(rows, cols)

    def body(place_ref, w_ref, o_ref):
        o_ref[...] = w_ref[...].astype(BF16)

    return _only(_call(
        body, name=name, n_prefetch=1, grid=(rows // tr, cols // tc),
        in_specs=[pl.BlockSpec((tr, tc), lambda i, j, pr: (i, j))],
        out_specs=[pl.BlockSpec((None, tr, tc), lambda i, j, pr: (pr[0], i, j))],
        out_shape=[jax.ShapeDtypeStruct((N_CHIPS, rows, cols), BF16)], operands=[place, w],
        semantics=("parallel", "parallel"), phases=phases))


def _exchange(phases, *, name):
    comm_in = [a for ph in phases for a in ph.arrays]
    comm_out = [jax.ShapeDtypeStruct(s.shape, s.dtype) for ph in phases for s in (ph.arrays if ph.in_place else ph.out_shapes)]
    aliases, at_in, at_out = {}, 0, 0
    for ph in phases:
        if ph.in_place:
            aliases.update({at_in + r: at_out + r for r in range(len(ph.arrays))})
        at_in, at_out = at_in + len(ph.arrays), at_out + ph.n_out
    n_sems = sum(ph.n_sems for ph in phases)

    def body(*refs):
        cin, cout = refs[:len(comm_in)], refs[len(comm_in):len(comm_in) + len(comm_out)]
        send_sems, recv_sems = refs[len(comm_in) + len(comm_out):]
        _run_phases(phases, ("start", "finish"), cin, cout, send_sems, recv_sems)

    return pl.pallas_call(
        body, name=name, out_shape=tuple(comm_out), in_specs=[ANY] * len(comm_in), out_specs=tuple([ANY] * len(comm_out)),
        input_output_aliases=aliases,
        scratch_shapes=[pltpu.SemaphoreType.DMA((n_sems,)), pltpu.SemaphoreType.DMA((n_sems,))],
    )(*comm_in)


GATHER_PARTS = 4


def _gather(bufs, stage, part=(0, GATHER_PARTS)):
    n = 3 * len(bufs)
    lo, hi = part

    def copies(outs, send, recv, d2d, incoming):
        x, y, c, chips = _place()
        for t, buf in enumerate(outs):
            half = buf.shape[2] // 2
            piece = half // GATHER_PARTS
            for k, (cx, cy) in enumerate(chips):
                i = 3 * t + k + (n if (d2d and stage == "both") else 0)
                cols = pl.ds(((1 - c) if (d2d and incoming) else c) * half + lo * piece, (hi - lo) * piece)
                blk = buf.at[(2 * cx + cy) if (d2d or incoming) else (2 * x + y), :, cols]
                yield _remote(blk, blk, send(i), recv(i), (x, y, 1 - c) if d2d else (cx, cy, c))

    def start(ins, outs, send, recv):
        for cp in copies(outs, send, recv, stage == "d2d", False):
            cp.start()

    def finish(ins, outs, send, recv):
        if stage == "both":
            for arrival, onward in zip(copies(outs, send, recv, False, True), copies(outs, send, recv, True, False)):
                arrival.wait_recv()
                onward.start()
        for cp in copies(outs, send, recv, stage != "ici", True):
            cp.wait_recv()
        for d2d in ((False, True) if stage == "both" else (stage == "d2d",)):
            for cp in copies(outs, send, recv, d2d, False):
                cp.wait_send()

    return _Phase(bufs, [], True, (2 if stage == "both" else 1) * n, start, finish)


def _merge(first, second):
    n_first = first.n_sems

    def later(sem):
        return lambda i: sem(n_first + i)

    def start(ins, outs, send, recv):
        first.start(ins, outs, send, recv)
        second.start(ins, outs, later(send), later(recv))

    def finish(ins, outs, send, recv):
        first.finish(ins, outs, send, recv)
        second.finish(ins, outs, later(send), later(recv))

    return _Phase(first.arrays, [], True, n_first + second.n_sems, start, finish)


def _swap_halves(grads):
    def copies(ins, outs, send, recv):
        x, y, c, _ = _place()
        for t, g in enumerate(ins):
            half = g.shape[2] // 2
            yield _remote(g.at[:, :, pl.ds((1 - c) * half, half)], outs[t], send(t), recv(t), (x, y, 1 - c))

    def start(ins, outs, send, recv):
        for cp in copies(ins, outs, send, recv):
            cp.start()

    def finish(ins, outs, send, recv):
        for cp in copies(ins, outs, send, recv):
            cp.wait()

    shapes = [jax.ShapeDtypeStruct((a.shape[0], a.shape[1], a.shape[2] // 2), a.dtype) for a in grads]
    return _Phase(grads, shapes, False, len(grads), start, finish)


def _add_halves(grad, received, place, *, name):
    ns, rows, half = received.shape
    tr, tc = _rc_tile(rows, half, pref_rows=1024)
    per = half // tc

    def body(place_ref, g_ref, r_ref, o_ref):
        o_ref[...] = (g_ref[...].astype(F32) + r_ref[...].astype(F32)).astype(BF16)

    grid_spec = pltpu.PrefetchScalarGridSpec(
        num_scalar_prefetch=1, grid=(ns, rows // tr, per),
        in_specs=[pl.BlockSpec((None, tr, tc), lambda s, i, j, pr: (s, i, pr[1] * per + j)),
                  pl.BlockSpec((None, tr, tc), lambda s, i, j, pr: (s, i, j))],
        out_specs=pl.BlockSpec((None, tr, tc), lambda s, i, j, pr: (s, i, j)),
    )
    return pl.pallas_call(
        body, name=name, grid_spec=grid_spec, out_shape=jax.ShapeDtypeStruct(received.shape, BF16),
        compiler_params=_params("parallel", "parallel", "parallel"),
    )(place, grad, received)


def _send_partials(parts, piece=(0, 1)):
    k_th, n_pieces = piece

    def cols(part):
        width = part.shape[2] // n_pieces
        return pl.ds(k_th * width, width)

    def start(ins, outs, send, recv):
        x, y, c, chips = _place()
        for t, part in enumerate(ins):
            for k, (cx, cy) in enumerate(chips):
                _remote(part.at[2 * cx + cy, :, cols(part)], outs[t].at[2 * x + y],
                        send(3 * t + k), recv(3 * t + k), (cx, cy, c)).start()

    def finish(ins, outs, send, recv):
        x, y, c, chips = _place()
        for t, part in enumerate(ins):
            for k, (cx, cy) in enumerate(chips):
                slot = outs[t].at[2 * cx + cy]
                _remote(slot, slot, send(3 * t + k), recv(3 * t + k), (cx, cy, c)).wait_recv()
        for t, part in enumerate(ins):
            for k, (cx, cy) in enumerate(chips):
                sent = part.at[2 * cx + cy, :, cols(part)]
                _remote(sent, sent, send(3 * t + k), recv(3 * t + k), (cx, cy, c)).wait_send()

    shapes = [jax.ShapeDtypeStruct(a.shape[:2] + (a.shape[2] // n_pieces,), a.dtype) for a in parts]
    return _Phase(parts, shapes, False, 3 * len(parts), start, finish)


def _sum_chips(parts, slots, place, *, name, piece=(0, 1), into=None):
    ns, rows, width = slots.shape
    k_th, n_pieces = piece
    half = width * n_pieces
    tr, tc = _rc_tile(rows, width, pref_rows=512)
    per = width // tc

    def body(place_ref, p_ref, s_ref, *rest):
        acc = p_ref[...].astype(F32)
        for k in range(ns):
            acc = acc + jnp.where(place_ref[0] == k, 0.0, s_ref[k].astype(F32))
        rest[-1][...] = acc

    grid_spec = pltpu.PrefetchScalarGridSpec(
        num_scalar_prefetch=1, grid=(rows // tr, per),
        in_specs=[pl.BlockSpec((None, tr, tc), lambda i, j, pr: (pr[0], i, k_th * per + j)),
                  pl.BlockSpec((ns, tr, tc), lambda i, j, pr: (0, i, j))] + ([ANY] if into is not None else []),
        out_specs=pl.BlockSpec((tr, tc), lambda i, j, pr: (i, (pr[1] * n_pieces + k_th) * per + j)),
    )
    return pl.pallas_call(
        body, name=name, grid_spec=grid_spec, out_shape=jax.ShapeDtypeStruct((rows, 2 * half), F32),
        input_output_aliases={3: 0} if into is not None else {},
        compiler_params=_params("parallel", "parallel"),
    )(place, parts, slots, *([into] if into is not None else []))


def _join_halves(bufs):
    def copies(outs, send, recv, incoming):
        x, y, c, _ = _place()
        for t, buf in enumerate(outs):
            half = buf.shape[1] // 2
            cols = buf.at[:, pl.ds(((1 - c) if incoming else c) * half, half)]
            yield _remote(cols, cols, send(t), recv(t), (x, y, 1 - c))

    def start(ins, outs, send, recv):
        for cp in copies(outs, send, recv, False):
            cp.start()

    def finish(ins, outs, send, recv):
        for cp in copies(outs, send, recv, True):
            cp.wait_recv()
        for cp in copies(outs, send, recv, False):
            cp.wait_send()

    return _Phase(bufs, [], True, len(bufs), start, finish)


def _allgather_small(buf, *, name):
    rows = buf.shape[0]

    def body(x_ref, out_ref, send_sems, recv_sems, local_sem):
        x, y, c, chips = _place()
        sibling = (x, y, 1 - c)

        def slot(px, py, pc):
            return out_ref.at[4 * px + 2 * py + pc]

        def copy(k, block, to, src=None):
            return _remote(slot(*block) if src is None else src, slot(*block), send_sems.at[k], recv_sems.at[k], to)

        mine = pltpu.make_async_copy(x_ref, slot(x, y, c), local_sem)
        mine.start()
        first = [copy(0, (x, y, c), sibling, src=x_ref)]
        first += [copy(1 + k, (x, y, c), (*chip, c), src=x_ref) for k, chip in enumerate(chips)]
        for cp in first:
            cp.start()
        passed = [copy(4 + k, (*chip, c), sibling) for k, chip in enumerate(chips)]
        for k, chip in enumerate(chips):
            copy(1 + k, (*chip, c), (x, y, c)).wait_recv()
            passed[k].start()
        copy(0, (x, y, 1 - c), (x, y, c)).wait_recv()
        for k, chip in enumerate(chips):
            copy(4 + k, (*chip, 1 - c), (x, y, c)).wait_recv()
        for cp in first + passed:
            cp.wait_send()
        mine.wait()

    return pl.pallas_call(
        body, name=name, out_shape=jax.ShapeDtypeStruct((N_DEV, rows, LANES), buf.dtype),
        in_specs=[pl.BlockSpec(memory_space=pltpu.VMEM)], out_specs=pl.BlockSpec(memory_space=pltpu.VMEM),
        scratch_shapes=[pltpu.SemaphoreType.DMA((7,)), pltpu.SemaphoreType.DMA((7,)), pltpu.SemaphoreType.DMA],
    )(buf)


def _adamw_math(w, g, m, v):
    m = ADAM_B1 * m + (1.0 - ADAM_B1) * g
    v = ADAM_B2 * v + (1.0 - ADAM_B2) * (g * g)
    m_hat = m / (1.0 - ADAM_B1 ** ADAM_STEP)
    v_hat = v / (1.0 - ADAM_B2 ** ADAM_STEP)
    delta = -ADAM_LR * (m_hat / (jnp.sqrt(v_hat) + ADAM_EPS) + ADAM_WD * w)
    return delta, m, v


def _adamw(w, g, m, v, *, name):
    rows, cols = w.shape
    tr, tc = _rc_tile(rows, cols)

    def body(w_ref, g_ref, m_ref, v_ref, go_ref, d_ref, mo_ref, vo_ref):
        g = g_ref[...]
        go_ref[...] = g
        d_ref[...], mo_ref[...], vo_ref[...] = _adamw_math(w_ref[...], g, m_ref[...], v_ref[...])

    blk = pl.BlockSpec((tr, tc), lambda i, j: (i, j))
    shape = jax.ShapeDtypeStruct((rows, cols), F32)
    return pl.pallas_call(
        body, name=name, out_shape=(shape, shape, shape, shape), grid=(rows // tr, cols // tc),
        in_specs=[blk] * 4, out_specs=(blk, blk, blk, blk), compiler_params=_params("parallel", "parallel"),
    )(w, g, m, v)


def _adamw_small(gathered, w, m, v, *, name):
    nd = gathered.shape[0]

    def body(gs_ref, w_ref, m_ref, v_ref, g_ref, d_ref, mo_ref, vo_ref):
        g = gs_ref[0]
        for k in range(1, nd):
            g = g + gs_ref[k]
        g_ref[...] = g
        d_ref[...], mo_ref[...], vo_ref[...] = _adamw_math(w_ref[...], g, m_ref[...], v_ref[...])

    shape = jax.ShapeDtypeStruct(w.shape, F32)
    return pl.pallas_call(body, name=name, out_shape=(shape, shape, shape, shape),
                          compiler_params=pltpu.CompilerParams(vmem_limit_bytes=VMEM_LIMIT_BYTES))(gathered, w, m, v)


def _pack(parts):
    flat = jnp.concatenate([p.reshape(-1).astype(F32) for p in parts])
    rows = -(-flat.shape[0] // (8 * LANES)) * 8
    return jnp.pad(flat, (0, rows * LANES - flat.shape[0])).reshape(rows, LANES)


def _unpack(buf, shapes):
    flat = buf.reshape(-1)
    out, pos = [], 0
    for shp in shapes:
        size = int(np.prod(shp))
        out.append(flat[pos:pos + size].reshape(shp))
        pos += size
    return out


ROW_BLOCK = 128


def _realign_rows(sources, segments, out_shape, *, name):
    n_slots, rows, cols = out_shape
    n_src = len(sources)
    per_slot = -(-rows // ROW_BLOCK)
    table = np.zeros((6, n_slots * per_slot, n_src), np.int32)
    for so in range(n_slots):
        for first, last, src, src_slot, src_row in segments[so]:
            for blk in range(first // ROW_BLOCK, (last - 1) // ROW_BLOCK + 1):
                lo, hi = max(first, blk * ROW_BLOCK), min(last, (blk + 1) * ROW_BLOCK)
                base = src_row + (blk * ROW_BLOCK - first)
                m0 = (base + lo - blk * ROW_BLOCK) // ROW_BLOCK
                at = so * per_slot + blk
                assert table[4, at, src] == 0, "two segments of one block share a source operand"
                table[:, at, src] = (src_slot, m0, base - m0 * ROW_BLOCK, lo - blk * ROW_BLOCK, hi - blk * ROW_BLOCK,
                                     min(2 * ROW_BLOCK, sources[src].shape[1] - m0 * ROW_BLOCK))
    last_block = [-(-a.shape[1] // ROW_BLOCK) - 1 for a in sources]

    def body(slot_ref, blk_ref, off_ref, lo_ref, hi_ref, valid_ref, *refs):
        o_ref, acc = refs[2 * n_src], refs[2 * n_src + 1]
        at = (pl.program_id(0) * per_slot + pl.program_id(1)) * n_src
        acc[...] = jnp.zeros_like(acc)
        for p in range(n_src):
            @pl.when(hi_ref[at + p] > lo_ref[at + p])
            def _():
                two = jnp.concatenate([refs[2 * p][...], refs[2 * p + 1][...]], axis=0)
                src_row = lax.broadcasted_iota(jnp.int32, two.shape, 0)
                two = jnp.where(src_row < valid_ref[at + p], two, jnp.zeros_like(two))
                r = lax.broadcasted_iota(jnp.int32, (ROW_BLOCK, 2 * ROW_BLOCK), 0)
                c = lax.broadcasted_iota(jnp.int32, (ROW_BLOCK, 2 * ROW_BLOCK), 1)
                place = (c == r + off_ref[at + p]) & (r >= lo_ref[at + p]) & (r < hi_ref[at + p])
                acc[...] += jnp.dot(place.astype(two.dtype), two, preferred_element_type=F32)
        o_ref[...] = acc[...].astype(o_ref.dtype)

    def src_spec(p, second):
        def index(so, i, slot_r, blk_r, off_r, lo_r, hi_r, valid_r):
            at = (so * per_slot + i) * n_src + p
            return slot_r[at], jnp.minimum(blk_r[at] + second, last_block[p]), 0
        return pl.BlockSpec((None, ROW_BLOCK, cols), index)

    grid_spec = pltpu.PrefetchScalarGridSpec(
        num_scalar_prefetch=6, grid=(n_slots, per_slot),
        in_specs=[src_spec(p, second) for p in range(n_src) for second in (0, 1)],
        out_specs=pl.BlockSpec((None, ROW_BLOCK, cols), lambda so, i, *_: (so, i, 0)),
        scratch_shapes=[pltpu.VMEM((ROW_BLOCK, cols), F32)],
    )
    flat = [jnp.asarray(table[k].reshape(-1)) for k in range(6)]
    return pl.pallas_call(
        body, name=name, grid_spec=grid_spec, out_shape=jax.ShapeDtypeStruct(out_shape, sources[0].dtype),
        compiler_params=_params("parallel", "arbitrary"),
    )(*flat, *[a for a in sources for _ in (0, 1)])


def _shard_rows(g, lo, hi):
    rs = g.shape[1]
    pieces = []
    for j in range(g.shape[0]):
        a, b = max(lo, j * rs), min(hi, (j + 1) * rs)
        if a < b:
            pieces.append(g[j, a - j * rs:b - j * rs])
    return pieces


def kernel(x, norm_mix_g, w_in, b_f, gmlp_ln_g, gmlp_ln_b, w_s, b_s, attn_out_g, gmlp_out_g, w_out, norm_ffn_g, w_ff1, w_ff2, norm_final_g, loss_target, m_norm_mix_g, m_w_in, m_b_f, m_gmlp_ln_g, m_gmlp_ln_b, m_w_s, m_b_s, m_attn_out_g, m_gmlp_out_g, m_w_out, m_norm_ffn_g, m_w_ff1, m_w_ff2, m_norm_final_g, v_norm_mix_g, v_w_in, v_b_f, v_gmlp_ln_g, v_gmlp_ln_b, v_w_s, v_b_s, v_attn_out_g, v_gmlp_out_g, v_w_out, v_norm_ffn_g, v_w_ff1, v_w_ff2, v_norm_final_g):
    seq, d_model = x.shape[1], x.shape[2]
    d_attn = d_model // 2
    n_heads = d_attn // HEAD_DIM
    qkv = 3 * d_attn
    shard_cols = w_in.shape[2]
    assert N_CHIPS * shard_cols == qkv + n_heads + 2 * d_attn
    xs = x.reshape(seq, d_model)
    target = loss_target.reshape(seq, d_model)

    place = jnp.stack([2 * lax.axis_index("x") + lax.axis_index("y"), lax.axis_index("c")]).astype(jnp.int32)
    names = ["w_in", "w_out", "w_ff1", "w_ff2"]
    wt_in, mt_in, vt_in = w_in[0].T, m_w_in[0].T, v_w_in[0].T
    b_in, _ = _cast_into_slot(wt_in, place, name="cast_w_in")
    b_out, _ = _cast_into_slot(w_out[0], place, name="cast_w_out")
    b_ff1, (b_in,) = _cast_into_slot(w_ff1[0], place, name="cast_w_ff1", phases=[_gather([b_in], "ici", (0, 2))])
    b_ff2, (b_in,) = _cast_into_slot(w_ff2[0], place, name="cast_w_ff2",
                                     phases=[_merge(_gather([b_in], "ici", (2, 3)), _gather([b_in], "d2d", (0, 2)))])
    h, (b_in,) = _rmsnorm_fwd(xs, norm_mix_g, name="norm_mix",
                              phases=[_merge(_gather([b_in], "ici", (3, 4)), _gather([b_in], "d2d", (2, 3)))])
    (g_in,) = _exchange([_gather([b_in], "d2d", (3, 4))], name="allgather_w_in_tail")
    n_cols = N_CHIPS * shard_cols
    gate_slot, gate_row = divmod(qkv, shard_cols)
    assert gate_row + n_heads <= shard_cols
    pieces = []
    for j in range(N_CHIPS):
        if j == gate_slot:
            pieces += [(j, 0, gate_row), (j, gate_row + n_heads, shard_cols - gate_row - n_heads)]
        else:
            pieces.append((j, 0, shard_cols))
    fwd_segments, at = [[]], 0
    for order, (j, src_row, size) in enumerate(pieces):
        fwd_segments[0].append((at, at + size, order % 2, j, src_row))
        at += size
    wt_main = _realign_rows([g_in, g_in], fwd_segments, (1, n_cols - n_heads, d_model), name="w_in_rows")[0]
    wt_f = jnp.pad(jnp.concatenate(_shard_rows(g_in, qkv, qkv + n_heads), axis=0), ((0, LANES - n_heads), (0, 0)))
    b_f_pad = jnp.pad(b_f, ((0, 0), (0, LANES - n_heads)))
    b_col = b_s[0].T

    first, rest = (0, 1), (1, GATHER_PARTS)
    z, (b_out, b_ff1) = _matmul(h, wt_main, name="in_proj", out_dtype=BF16, trans_b=True,
                                phases=[_gather([b_out], "ici"), _gather([b_ff1], "ici", first)])
    zb, f_cum = _forget_fwd(h, wt_f, b_f_pad, name="forget_fwd")
    f_row = f_cum[:, :n_heads].T[:, None, :]
    (o, lse2), (b_ff1, b_out) = _attn_fwd(z, f_row, n_heads, name="attn_fwd",
                                          phases=[_gather([b_ff1], "ici", rest), _gather([b_out], "d2d")])
    merged = _mix_fwd(z, o, gmlp_ln_g, gmlp_ln_b, w_s[0], b_col, attn_out_g, gmlp_out_g, n_heads, name="mix_fwd")
    w_out_full = b_out.reshape(2 * d_attn, d_model)
    x1, (b_ff1, b_ff2) = _matmul(merged, w_out_full, name="out_proj", out_dtype=F32, residual=xs,
                                 phases=[_gather([b_ff1], "d2d"), _gather([b_ff2], "ici", first)])
    h2, _ = _rmsnorm_fwd(x1, norm_ffn_g, name="norm_ffn")
    a, (b_ff2,) = _matmul(h2, b_ff1, name="ff1", out_dtype=BF16, relu=True, b_sharded=True,
                          phases=[_merge(_gather([b_ff2], "both", rest), _gather([b_ff2], "d2d", first))])
    w_ff2_full = b_ff2.reshape(N_CHIPS * b_ff2.shape[1], d_model)
    x2, _ = _matmul(a, w_ff2_full, name="ff2", out_dtype=F32, square_lhs=True, residual=x1)
    dx2, dg_final, loss = _loss_and_final_bwd(x2, target, norm_final_g.reshape(1, d_model), name="loss_head")

    def pair_sum(g, r, nm):
        return _add_halves(g, r, place, name="grads_pair_sum_" + nm)

    def chip_sum(p, q, nm, **piece):
        return _sum_chips(p, q, place, name="grads_chip_sum_" + nm, **piece)

    dw_ff2, _ = _matmul(a, dx2, name="ff2_dw", out_dtype=BF16, trans_a=True, square_lhs=True)
    dw_ff2 = dw_ff2.reshape(N_CHIPS, -1, d_model)
    da, (r_ff2,) = _matmul(dx2, w_ff2_full, name="ff2_dlhs", out_dtype=BF16, trans_b=True, scale2_by=a,
                           phases=[_swap_halves([dw_ff2])])
    ps_ff2 = pair_sum(dw_ff2, r_ff2, "w_ff2")
    dh2, (q_ff2a,) = _matmul(da, b_ff1, name="ff1_dlhs", out_dtype=F32, trans_b=True, b_sharded=True,
                             phases=[_send_partials([ps_ff2], (0, 2))])
    dw_ff1, (q_ff2b,) = _matmul(h2, da, name="ff1_dw", out_dtype=BF16, trans_a=True, out_sharded=True,
                                phases=[_send_partials([ps_ff2], (1, 2))])
    g_ff2 = chip_sum(ps_ff2, q_ff2a, "w_ff2_a", piece=(0, 2))
    g_ff2 = chip_sum(ps_ff2, q_ff2b, "w_ff2_b", piece=(1, 2), into=g_ff2)
    (dx1, dg_ffn), (g_ff2,) = _rmsnorm_bwd(dh2, x1, dx2, norm_ffn_g, name="norm_ffn_bwd", phases=[_join_halves([g_ff2])])
    dw_out, _ = _matmul(merged, dx1, name="out_proj_dw", out_dtype=BF16, trans_a=True)
    dw_out = dw_out.reshape(N_CHIPS, -1, d_model)
    d_merged, (r_ff1, r_out) = _matmul(dx1, w_out_full, name="out_proj_dlhs", out_dtype=F32, trans_b=True,
                                       phases=[_swap_halves([dw_ff1, dw_out])])
    ps_ff1, ps_out = pair_sum(dw_ff1, r_ff1, "w_ff1"), pair_sum(dw_out, r_out, "w_out")
    d_o, dzu, dzv, dw_s, db_col, dlg, dlb, dag, dgg = _mix_bwd(
        z, o, d_merged, gmlp_ln_g, gmlp_ln_b, w_s[0], b_col, attn_out_g, gmlp_out_g, n_heads, name="mix_bwd")
    (dq, dk, dv, d_f_key, d_f_query), (q_ff1, q_out) = _attn_bwd(
        z, o, d_o, lse2, f_row, n_heads, name="attn_bwd", phases=[_send_partials([ps_ff1, ps_out])])
    g_ff1, g_out = chip_sum(ps_ff1, q_ff1, "w_ff1"), chip_sum(ps_out, q_out, "w_out")
    d_f = d_f_key.reshape(n_heads, seq) + d_f_query.reshape(n_heads, seq)
    d_f_pad = jnp.pad(d_f.T, ((0, 0), (0, LANES - n_heads)))
    dzf, db_f = _forget_bwd(d_f_pad, zb, name="forget_bwd")
    dz = jnp.concatenate([dq, dk, dv, dzu, dzv], axis=1)
    dwt_main, (g_ff1, g_out) = _matmul(dz, h, name="in_proj_dw", out_dtype=BF16, trans_a=True,
                                       phases=[_join_halves([g_ff1, g_out])])
    dwt_f, _ = _matmul(dzf, h, name="gate_dw", out_dtype=BF16, trans_a=True)
    bwd_segments = []
    for j in range(N_CHIPS):
        first = j * shard_cols
        if j < gate_slot:
            bwd_segments.append([(0, shard_cols, 0, 0, first)])
        elif j > gate_slot:
            bwd_segments.append([(0, shard_cols, 0, 0, first - n_heads)])
        else:
            bwd_segments.append([(0, gate_row, 0, 0, first), (gate_row, gate_row + n_heads, 1, 0, 0),
                                 (gate_row + n_heads, shard_cols, 2, 0, qkv)])
    dw_in = _realign_rows([dwt_main[None], dwt_f[None], dwt_main[None]], bwd_segments,
                          (N_CHIPS, shard_cols, d_model), name="dw_in_rows")
    dh_gate, (r_in,) = _matmul(dzf, wt_f, name="gate_dlhs", out_dtype=F32, phases=[_swap_halves([dw_in])])
    ps_in = pair_sum(dw_in, r_in, "w_in")
    dh, (q_in,) = _matmul(dz, wt_main, name="in_proj_dlhs", out_dtype=F32, residual=dh_gate, tk=2560,
                          phases=[_send_partials([ps_in])])
    g_in_sum = chip_sum(ps_in, q_in, "w_in")
    (grad_x, dg_mix), _ = _rmsnorm_bwd(dh, xs, dx1, norm_mix_g, name="norm_mix_bwd")
    (g_in_sum,) = _exchange([_join_halves([g_in_sum])], name="grads_join_w_in")

    big = {}
    for nm, g, w, m, v in zip(names, (g_in_sum, g_out, g_ff1, g_ff2), (wt_in, w_out[0], w_ff1[0], w_ff2[0]),
                              (mt_in, m_w_out[0], m_w_ff1[0], m_w_ff2[0]), (vt_in, v_w_out[0], v_w_ff1[0], v_w_ff2[0])):
        big[nm] = tuple((t.T if nm == "w_in" else t)[None] for t in _adamw(w, g, m, v, name="adamw_" + nm))

    small_w = [norm_mix_g, b_f, gmlp_ln_g, gmlp_ln_b, w_s, b_s, attn_out_g, gmlp_out_g, norm_ffn_g, norm_final_g]
    small_m = [m_norm_mix_g, m_b_f, m_gmlp_ln_g, m_gmlp_ln_b, m_w_s, m_b_s, m_attn_out_g, m_gmlp_out_g, m_norm_ffn_g, m_norm_final_g]
    small_v = [v_norm_mix_g, v_b_f, v_gmlp_ln_g, v_gmlp_ln_b, v_w_s, v_b_s, v_attn_out_g, v_gmlp_out_g, v_norm_ffn_g, v_norm_final_g]
    small_g = [dg_mix, db_f[:, :n_heads], dlg, dlb, dw_s, db_col.T, dag, dgg, dg_ffn, dg_final]
    shapes = [w.shape for w in small_w]
    gathered = _allgather_small(_pack(small_g), name="allgather_small_grads")
    packed = _adamw_small(gathered, _pack(small_w), _pack(small_m), _pack(small_v), name="adamw_small")
    sg, sd, sm, sv = (_unpack(p, shapes) for p in packed)
    small_names = ["norm_mix_g", "b_f", "gmlp_ln_g", "gmlp_ln_b", "w_s", "b_s", "attn_out_g", "gmlp_out_g", "norm_ffn_g", "norm_final_g"]
    small = {nm: (sg[i], sd[i], sm[i], sv[i]) for i, nm in enumerate(small_names)}

    order = ["norm_mix_g", "w_in", "b_f", "gmlp_ln_g", "gmlp_ln_b", "w_s", "b_s", "attn_out_g", "gmlp_out_g", "w_out",
             "norm_ffn_g", "w_ff1", "w_ff2", "norm_final_g"]
    result = {**small, **big}
    total_loss = lax.psum(loss[0, 0], ("x", "y", "c"))
    outs = [total_loss, grad_x.reshape(x.shape)]
    for part in range(4):
        outs += [result[nm][part] for nm in order]
    return tuple(outs)
```

```python
import functools
import math

import numpy as np
import jax
import jax.numpy as jnp
from jax import lax
from jax.experimental import pallas as pl
from jax.experimental.pallas import tpu as pltpu

HEAD_DIM = 128
CHUNK = 128
EPS = 1e-6
LANES = 128
N_CHIPS = 4
N_DEV = 8
VMEM_LIMIT_BYTES = 56 * 1024 * 1024

ADAM_LR = 0.001
ADAM_B1 = 0.9
ADAM_B2 = 0.999
ADAM_EPS = 1e-08
ADAM_WD = 0.01
ADAM_STEP = 10

BF16 = jnp.bfloat16
F32 = jnp.float32
MESH = pl.DeviceIdType.MESH
ANY = pl.BlockSpec(memory_space=pl.ANY)
NEG_BIG = -1e30


def _params(*sem):
    return pltpu.CompilerParams(dimension_semantics=tuple(sem), vmem_limit_bytes=VMEM_LIMIT_BYTES)


def _tile(n, pref, unit):
    t = (min(pref, n) // unit) * unit
    while t >= unit:
        if n % t == 0:
            return t
        t -= unit
    return n


def _rc_tile(rows, cols, pref_rows=256, pref_cols=256):
    if rows % 16 == 0:
        return _tile(rows, pref_rows, 16), cols
    return rows, _tile(cols, pref_cols, LANES)


class _Phase:
    def __init__(self, arrays, out_shapes, in_place, n_sems, start, finish):
        self.arrays, self.out_shapes, self.in_place = list(arrays), list(out_shapes), in_place
        self.n_sems, self.start, self.finish = n_sems, start, finish

    @property
    def n_out(self):
        return len(self.arrays) if self.in_place else len(self.out_shapes)


def _run_phases(phases, steps, comm_in, comm_out, send_sems, recv_sems):
    at_in = at_out = at_sem = 0
    for ph in phases:
        for step in steps:
            getattr(ph, step)(comm_in[at_in:at_in + len(ph.arrays)], comm_out[at_out:at_out + ph.n_out],
                              lambda i, base=at_sem: send_sems.at[base + i], lambda i, base=at_sem: recv_sems.at[base + i])
        at_in, at_out, at_sem = at_in + len(ph.arrays), at_out + ph.n_out, at_sem + ph.n_sems


def _call(body, *, name, grid, in_specs, out_specs, out_shape, operands, semantics, scratch_shapes=(),
          n_prefetch=0, phases=()):
    in_specs, out_specs, out_shape = list(in_specs), list(out_specs), list(out_shape)
    scratch_shapes = list(scratch_shapes)
    n_in, n_out, n_scr = len(operands) - n_prefetch, len(out_shape), len(scratch_shapes)
    comm_in = [a for ph in phases for a in ph.arrays]
    comm_out = [jax.ShapeDtypeStruct(s.shape, s.dtype) for ph in phases
                for s in (ph.arrays if ph.in_place else ph.out_shapes)]
    aliases, at_in, at_out = {}, n_prefetch + n_in, n_out
    for ph in phases:
        if ph.in_place:
            aliases.update({at_in + r: at_out + r for r in range(len(ph.arrays))})
        at_in, at_out = at_in + len(ph.arrays), at_out + ph.n_out
    n_sems = sum(ph.n_sems for ph in phases)

    def hosted(*refs):
        pre, rest = refs[:n_prefetch], refs[n_prefetch:]
        ins, rest = rest[:n_in], rest[n_in:]
        cin, rest = rest[:len(comm_in)], rest[len(comm_in):]
        outs, rest = rest[:n_out], rest[n_out:]
        cout, rest = rest[:len(comm_out)], rest[len(comm_out):]
        scr = rest[:n_scr]
        if phases:
            send_sems, recv_sems = rest[n_scr:]
            ids = [pl.program_id(ax) for ax in range(len(grid))]
            first = functools.reduce(jnp.logical_and, [i == 0 for i in ids])
            last = functools.reduce(jnp.logical_and, [i == g - 1 for i, g in zip(ids, grid)])

            @pl.when(first)
            def _():
                _run_phases(phases, ("start",), cin, cout, send_sems, recv_sems)

        body(*pre, *ins, *outs, *scr)
        if phases:
            @pl.when(last)
            def _():
                _run_phases(phases, ("finish",), cin, cout, send_sems, recv_sems)

    all_in = in_specs + [ANY] * len(comm_in)
    all_out = out_specs + [ANY] * len(comm_out)
    all_scr = scratch_shapes + ([pltpu.SemaphoreType.DMA((n_sems,)), pltpu.SemaphoreType.DMA((n_sems,))] if phases else [])
    if phases:
        semantics = ("arbitrary",) * len(grid)
    kwargs = dict(name=name, out_shape=tuple(out_shape + comm_out), compiler_params=_params(*semantics),
                  input_output_aliases=aliases)
    if n_prefetch:
        kwargs["grid_spec"] = pltpu.PrefetchScalarGridSpec(
            num_scalar_prefetch=n_prefetch, grid=grid, in_specs=all_in, out_specs=tuple(all_out), scratch_shapes=all_scr)
    else:
        kwargs.update(grid=grid, in_specs=all_in, out_specs=tuple(all_out), scratch_shapes=all_scr)
    res = pl.pallas_call(hosted, **kwargs)(*operands, *comm_in)
    return tuple(res[:n_out]), tuple(res[n_out:])


def _only(results):
    outs, comm = results
    return outs[0] if len(outs) == 1 else outs, comm


def _matmul(a, b, *, name, out_dtype, trans_a=False, trans_b=False, tm=1024, tn=1024, tk=2048,
            square_lhs=False, relu=False, residual=None, scale2_by=None,
            b_sharded=False, out_sharded=False, phases=()):
    m, k = (a.shape[1], a.shape[0]) if trans_a else a.shape
    if b_sharded:
        if trans_b:
            n, ks = b.shape[1], b.shape[2]
            assert N_CHIPS * ks == k
        else:
            ns = b.shape[2]
            n = N_CHIPS * ns
            assert b.shape[1] == k
    else:
        n = b.shape[0] if trans_b else b.shape[1]
        assert (b.shape[1] if trans_b else b.shape[0]) == k
    tm = _tile(m, tm, 128)
    tn = _tile(n // N_CHIPS if (out_sharded or (b_sharded and not trans_b)) else n, tn, 128)
    tk = _tile(k // N_CHIPS if (b_sharded and trans_b) else k, tk, 128)
    nk = k // tk

    if trans_a:
        a_spec = pl.BlockSpec((tk, tm), lambda i, j, kk: (kk, i))
    else:
        a_spec = pl.BlockSpec((tm, tk), lambda i, j, kk: (i, kk))
    if b_sharded and trans_b:
        per = ks // tk
        assert per * tk == ks
        b_spec = pl.BlockSpec((None, tn, tk), lambda i, j, kk: (kk // per, j, kk % per))
    elif b_sharded:
        per = ns // tn
        assert per * tn == ns
        b_spec = pl.BlockSpec((None, tk, tn), lambda i, j, kk: (j // per, kk, j % per))
    elif trans_b:
        b_spec = pl.BlockSpec((tn, tk), lambda i, j, kk: (j, kk))
    else:
        b_spec = pl.BlockSpec((tk, tn), lambda i, j, kk: (kk, j))
    if out_sharded:
        ns_out = n // N_CHIPS
        per_o = ns_out // tn
        assert per_o * tn == ns_out
        out_shape = jax.ShapeDtypeStruct((N_CHIPS, m, ns_out), out_dtype)
        o_spec = pl.BlockSpec((None, tm, tn), lambda i, j, kk: (j // per_o, i, j % per_o))
    else:
        out_shape = jax.ShapeDtypeStruct((m, n), out_dtype)
        o_spec = pl.BlockSpec((tm, tn), lambda i, j, kk: (i, j))
    mn_spec = pl.BlockSpec((tm, tn), lambda i, j, kk: (i, j))

    operands, in_specs = [a, b], [a_spec, b_spec]
    if scale2_by is not None:
        operands.append(scale2_by)
        in_specs.append(mn_spec)
    if residual is not None:
        operands.append(residual)
        in_specs.append(mn_spec)
    dims = (((0 if trans_a else 1,), (1 if trans_b else 0,)), ((), ()))

    def body(*refs):
        a_ref, b_ref = refs[0], refs[1]
        pos = 2
        scale_ref = res_ref = None
        if scale2_by is not None:
            scale_ref = refs[pos]
            pos += 1
        if residual is not None:
            res_ref = refs[pos]
            pos += 1
        o_ref = refs[pos]
        kk = pl.program_id(2)

        av = a_ref[...]
        if square_lhs:
            av = av.astype(F32)
            av = av * av
        part = lax.dot_general(av.astype(BF16), b_ref[...].astype(BF16), dims, preferred_element_type=F32)

        def finish(r):
            if relu:
                r = jnp.maximum(r, 0.0)
            if scale_ref is not None:
                r = r * (2.0 * scale_ref[...].astype(F32))
            if res_ref is not None:
                r = r + res_ref[...].astype(F32)
            o_ref[...] = r.astype(out_dtype)

        if nk == 1:
            finish(part)
        else:
            acc_ref = refs[pos + 1]

            @pl.when(kk == 0)
            def _():
                acc_ref[...] = part

            @pl.when(jnp.logical_and(kk > 0, kk < nk - 1))
            def _():
                acc_ref[...] += part

            @pl.when(kk == nk - 1)
            def _():
                finish(acc_ref[...] + part)

    return _only(_call(
        body, name=name, out_shape=[out_shape], grid=(m // tm, n // tn, nk),
        in_specs=in_specs, out_specs=[o_spec], operands=operands,
        scratch_shapes=[pltpu.VMEM((tm, tn), F32)] if nk > 1 else [],
        semantics=("parallel", "parallel", "arbitrary"), phases=phases))


def _rmsnorm_fwd(x, g, *, name, tr=512, phases=()):
    s, d = x.shape
    tr = _tile(s, tr, 8)

    def body(x_ref, g_ref, o_ref):
        xv = x_ref[...]
        r = lax.rsqrt(jnp.mean(xv * xv, axis=-1, keepdims=True) + EPS)
        o_ref[...] = ((xv * r) * g_ref[...]).astype(BF16)

    return _only(_call(
        body, name=name, out_shape=[jax.ShapeDtypeStruct((s, d), BF16)], grid=(s // tr,),
        in_specs=[pl.BlockSpec((tr, d), lambda i: (i, 0)), pl.BlockSpec((1, d), lambda i: (0, 0))],
        out_specs=[pl.BlockSpec((tr, d), lambda i: (i, 0))], operands=[x, g],
        semantics=("parallel",), phases=phases))


def _rms_bwd_rows(dy, xv, g):
    d = xv.shape[-1]
    r = lax.rsqrt(jnp.mean(xv * xv, axis=-1, keepdims=True) + EPS)
    gdy = dy * g
    dot = jnp.sum(gdy * xv, axis=-1, keepdims=True)
    dx = gdy * r - xv * (r * r * r) * (dot / d)
    return dx, dy * (xv * r)


def _rmsnorm_bwd(dy, x, res, g, *, name, tr=256, rounded_copy=False, phases=()):
    s, d = x.shape
    tr = _tile(s, tr, 8)

    def body(dy_ref, x_ref, res_ref, g_ref, dx_ref, dg_ref, *dxb_ref):
        @pl.when(pl.program_id(0) == 0)
        def _():
            dg_ref[...] = jnp.zeros_like(dg_ref)

        dx, dg_rows = _rms_bwd_rows(dy_ref[...].astype(F32), x_ref[...], g_ref[...])
        out = res_ref[...] + dx
        dx_ref[...] = out
        if rounded_copy:
            dxb_ref[0][...] = out.astype(BF16)
        dg_ref[...] += jnp.sum(dg_rows, axis=0, keepdims=True)

    row = pl.BlockSpec((tr, d), lambda i: (i, 0))
    vec = pl.BlockSpec((1, d), lambda i: (0, 0))
    extra = [jax.ShapeDtypeStruct((s, d), BF16)] if rounded_copy else []
    return _call(
        body, name=name,
        out_shape=[jax.ShapeDtypeStruct((s, d), F32), jax.ShapeDtypeStruct((1, d), F32)] + extra,
        grid=(s // tr,), in_specs=[row, row, row, vec], out_specs=[row, vec] + [row] * len(extra),
        operands=[dy, x, res, g], semantics=("arbitrary",), phases=phases)


def _loss_and_final_bwd(x2, target, g, *, name, tr=256):
    s, d = x2.shape
    tr = _tile(s, tr, 8)

    def body(x_ref, t_ref, g_ref, dx_ref, dxb_ref, dg_ref, loss_ref):
        @pl.when(pl.program_id(0) == 0)
        def _():
            dg_ref[...] = jnp.zeros_like(dg_ref)
            loss_ref[...] = jnp.zeros_like(loss_ref)

        xv, gv = x_ref[...], g_ref[...]
        r = lax.rsqrt(jnp.mean(xv * xv, axis=-1, keepdims=True) + EPS)
        err = (xv * r) * gv - t_ref[...]
        row_loss = jnp.mean(err * err, axis=-1, keepdims=True)
        loss_ref[...] += 0.5 * jnp.sum(row_loss, axis=0, keepdims=True)
        dx, dg_rows = _rms_bwd_rows(err / d, xv, gv)
        dx_ref[...] = dx
        dxb_ref[...] = dx.astype(BF16)
        dg_ref[...] += jnp.sum(dg_rows, axis=0, keepdims=True)

    row = pl.BlockSpec((tr, d), lambda i: (i, 0))
    vec = pl.BlockSpec((1, d), lambda i: (0, 0))
    one = pl.BlockSpec((1, 1), lambda i: (0, 0))
    return pl.pallas_call(
        body, name=name,
        out_shape=(jax.ShapeDtypeStruct((s, d), F32), jax.ShapeDtypeStruct((s, d), BF16),
                   jax.ShapeDtypeStruct((1, d), F32), jax.ShapeDtypeStruct((1, 1), F32)),
        grid=(s // tr,), in_specs=[row, row, vec], out_specs=(row, row, vec, one),
        compiler_params=_params("arbitrary"),
    )(x2, target, g)


def _tri_ones(n, lower):
    r = lax.broadcasted_iota(jnp.int32, (n, n), 0)
    c = lax.broadcasted_iota(jnp.int32, (n, n), 1)
    return jnp.where((c <= r) if lower else (c >= r), 1.0, 0.0).astype(F32)


def _forget_fwd(h, w_f, b_f, *, name, tr=256):
    s, d = h.shape
    tr = _tile(s, tr, 8)

    def body(h_ref, w_ref, b_ref, zb_ref, f_ref, carry):
        @pl.when(pl.program_id(0) == 0)
        def _():
            carry[...] = jnp.zeros_like(carry)

        zb = lax.dot_general(h_ref[...], w_ref[...], (((1,), (1,)), ((), ())), preferred_element_type=F32) + b_ref[...]
        zb_ref[...] = zb
        log_f = jnp.minimum(zb, 0.0) - jnp.log(1.0 + jnp.exp(-jnp.abs(zb)))
        run = jnp.dot(_tri_ones(tr, True), log_f, preferred_element_type=F32,
                      precision=lax.Precision.HIGHEST) + carry[...]
        f_ref[...] = run
        carry[...] = run[tr - 1:tr, :]

    row = pl.BlockSpec((tr, LANES), lambda i: (i, 0))
    return pl.pallas_call(
        body, name=name,
        out_shape=(jax.ShapeDtypeStruct((s, LANES), F32), jax.ShapeDtypeStruct((s, LANES), F32)),
        grid=(s // tr,),
        in_specs=[pl.BlockSpec((tr, d), lambda i: (i, 0)), pl.BlockSpec((LANES, d), lambda i: (0, 0)),
                  pl.BlockSpec((1, LANES), lambda i: (0, 0))],
        out_specs=(row, row), scratch_shapes=[pltpu.VMEM((1, LANES), F32)],
        compiler_params=_params("arbitrary"),
    )(h, w_f, b_f)


def _forget_bwd(d_f, zb, *, name, tr=256):
    s = zb.shape[0]
    tr = _tile(s, tr, 8)
    nb = s // tr

    def body(df_ref, zb_ref, dz_ref, db_ref, carry):
        @pl.when(pl.program_id(0) == 0)
        def _():
            carry[...] = jnp.zeros_like(carry)
            db_ref[...] = jnp.zeros_like(db_ref)

        run = jnp.dot(_tri_ones(tr, False), df_ref[...], preferred_element_type=F32,
                      precision=lax.Precision.HIGHEST) + carry[...]
        carry[...] = run[0:1, :]
        dz = run / (1.0 + jnp.exp(zb_ref[...]))
        dz_ref[...] = dz.astype(BF16)
        db_ref[...] += jnp.sum(dz, axis=0, keepdims=True)

    row = pl.BlockSpec((tr, LANES), lambda i: (nb - 1 - i, 0))
    return pl.pallas_call(
        body, name=name,
        out_shape=(jax.ShapeDtypeStruct((s, LANES), BF16), jax.ShapeDtypeStruct((1, LANES), F32)),
        grid=(nb,), in_specs=[row, row], out_specs=(row, pl.BlockSpec((1, LANES), lambda i: (0, 0))),
        scratch_shapes=[pltpu.VMEM((1, LANES), F32)],
        compiler_params=_params("arbitrary"),
    )(d_f, zb)


def _pairs(nblk, by_kv):
    if by_kv:
        pr = [(i, j) for j in range(nblk) for i in range(j, nblk)]
    else:
        pr = [(i, j) for i in range(nblk) for j in range(i + 1)]
    return (jnp.asarray(np.array([p[0] for p in pr], np.int32)), jnp.asarray(np.array([p[1] for p in pr], np.int32)))


def _causal_mask(t):
    r = lax.broadcasted_iota(jnp.int32, (t, t), 0)
    c = lax.broadcasted_iota(jnp.int32, (t, t), 1)
    return c <= r


LOG2E = math.log2(math.e)
QK_TO_LOG2 = LOG2E / math.sqrt(HEAD_DIM)


def _attn_logits2(q, k, fk_row):
    sc = lax.dot_general(q, k, (((1,), (1,)), ((), ())), preferred_element_type=F32)
    return sc * QK_TO_LOG2 - fk_row * LOG2E


def _attn_fwd(z, f_row, n_heads, *, name, tb=1024, phases=()):
    s = z.shape[0]
    tb = _tile(s, tb, 128)
    nblk = s // tb
    rep = tb // LANES
    qi, kj = _pairs(nblk, by_kv=False)

    def body(qi_ref, kj_ref, q_ref, k_ref, v_ref, fk_ref, o_ref, lse_ref, m_sc, l_sc, acc_sc):
        p = pl.program_id(1)
        i, j = qi_ref[p], kj_ref[p]

        @pl.when(j == 0)
        def _():
            m_sc[...] = jnp.full_like(m_sc, NEG_BIG)
            l_sc[...] = jnp.zeros_like(l_sc)
            acc_sc[...] = jnp.zeros_like(acc_sc)

        def update(masked):
            s2 = _attn_logits2(q_ref[...], k_ref[...], fk_ref[...])
            if masked:
                s2 = jnp.where(_causal_mask(tb), s2, NEG_BIG)
            m_old = m_sc[...]
            m_new = jnp.maximum(m_old, jnp.max(s2, axis=-1, keepdims=True))
            alpha = jnp.exp2(m_old - m_new)
            pv = jnp.exp2(s2 - jnp.tile(m_new, (1, rep)))
            l_sc[...] = alpha * l_sc[...] + jnp.sum(pv, axis=-1, keepdims=True)
            acc_sc[...] = alpha * acc_sc[...] + jnp.dot(pv.astype(BF16), v_ref[...], preferred_element_type=F32)
            m_sc[...] = m_new

        @pl.when(j < i)
        def _():
            update(False)

        @pl.when(j == i)
        def _():
            update(True)
            o_ref[...] = (acc_sc[...] / l_sc[...]).astype(BF16)
            lse_ref[...] = m_sc[...] + jnp.log2(l_sc[...])

    h = n_heads
    return _call(
        body, name=name, n_prefetch=2, grid=(h, int(qi.shape[0])),
        in_specs=[
            pl.BlockSpec((tb, HEAD_DIM), lambda hh, p, qi_r, kj_r: (qi_r[p], hh)),
            pl.BlockSpec((tb, HEAD_DIM), lambda hh, p, qi_r, kj_r: (kj_r[p], h + hh)),
            pl.BlockSpec((tb, HEAD_DIM), lambda hh, p, qi_r, kj_r: (kj_r[p], 2 * h + hh)),
            pl.BlockSpec((None, 1, tb), lambda hh, p, qi_r, kj_r: (hh, 0, kj_r[p])),
        ],
        out_specs=[
            pl.BlockSpec((tb, HEAD_DIM), lambda hh, p, qi_r, kj_r: (qi_r[p], hh)),
            pl.BlockSpec((None, tb, LANES), lambda hh, p, qi_r, kj_r: (hh, qi_r[p], 0)),
        ],
        scratch_shapes=[pltpu.VMEM((tb, LANES), F32), pltpu.VMEM((tb, LANES), F32), pltpu.VMEM((tb, HEAD_DIM), F32)],
        out_shape=[jax.ShapeDtypeStruct((s, h * HEAD_DIM), BF16), jax.ShapeDtypeStruct((h, s, LANES), F32)],
        operands=[qi, kj, z, z, z, f_row], semantics=("parallel", "arbitrary"), phases=phases)


def _attn_bwd(z, o, d_o, lse2, f_row, n_heads, *, name, tb=1024, phases=()):
    s = z.shape[0]
    tb = _tile(s, tb, 128)
    nblk = s // tb
    rep = tb // LANES
    qi, kj = _pairs(nblk, by_kv=True)
    n_pairs = int(qi.shape[0])
    scale = 1.0 / math.sqrt(HEAD_DIM)
    h = n_heads

    def body(qi_ref, kj_ref, q_ref, k_ref, v_ref, o_ref, do_ref, lse_ref, fk_ref,
             dq_ref, dk_ref, dv_ref, df_ref, dfq_ref, dq_sc, dk_sc, dv_sc, df_sc, dfq_sc):
        p = pl.program_id(1)
        i, j = qi_ref[p], kj_ref[p]

        @pl.when(p == 0)
        def _():
            dq_sc[...] = jnp.zeros_like(dq_sc)
            dfq_sc[...] = jnp.zeros_like(dfq_sc)

        @pl.when(i == j)
        def _():
            dk_sc[...] = jnp.zeros_like(dk_sc)
            dv_sc[...] = jnp.zeros_like(dv_sc)
            df_sc[...] = jnp.zeros_like(df_sc)

        def update(masked):
            q, k, v, do = q_ref[...], k_ref[...], v_ref[...], do_ref[...]
            delta = jnp.sum(do.astype(F32) * o_ref[...].astype(F32), axis=-1, keepdims=True)
            pv = jnp.exp2(_attn_logits2(q, k, fk_ref[...]) - jnp.tile(lse_ref[...], (1, rep)))
            if masked:
                pv = jnp.where(_causal_mask(tb), pv, 0.0)
            dp = lax.dot_general(do, v, (((1,), (1,)), ((), ())), preferred_element_type=F32)
            ds = pv * (dp - delta)
            ds_b = ds.astype(BF16)
            dv_sc[...] += lax.dot_general(pv.astype(BF16), do, (((0,), (0,)), ((), ())), preferred_element_type=F32)
            dk_sc[...] += lax.dot_general(ds_b, q, (((0,), (0,)), ((), ())), preferred_element_type=F32)
            rows = pl.ds(pl.multiple_of(i * tb, tb), tb)
            dq_sc[rows, :] += jnp.dot(ds_b, k, preferred_element_type=F32)
            df_sc[...] -= jnp.sum(ds, axis=0, keepdims=True)
            dfq_sc[rows, :] += jnp.broadcast_to(jnp.sum(ds, axis=1, keepdims=True), (tb, LANES))

        @pl.when(i > j)
        def _():
            update(False)

        @pl.when(i == j)
        def _():
            update(True)

        @pl.when(i == nblk - 1)
        def _():
            dk_ref[...] = (dk_sc[...] * scale).astype(BF16)
            dv_ref[...] = dv_sc[...].astype(BF16)
            df_ref[...] = df_sc[...]

        @pl.when(p == n_pairs - 1)
        def _():
            dq_ref[...] = (dq_sc[...] * scale).astype(BF16)
            dfq_ref[...] = jnp.transpose(dfq_sc[...])[0:1, :]

    qblk = lambda off: pl.BlockSpec((tb, HEAD_DIM), lambda hh, p, qi_r, kj_r: (qi_r[p], off + hh))
    kblk = lambda off: pl.BlockSpec((tb, HEAD_DIM), lambda hh, p, qi_r, kj_r: (kj_r[p], off + hh))
    qrep = pl.BlockSpec((None, tb, LANES), lambda hh, p, qi_r, kj_r: (hh, qi_r[p], 0))
    krow = pl.BlockSpec((None, 1, tb), lambda hh, p, qi_r, kj_r: (hh, 0, kj_r[p]))
    act = jax.ShapeDtypeStruct((s, h * HEAD_DIM), BF16)
    return _call(
        body, name=name, n_prefetch=2, grid=(h, n_pairs),
        in_specs=[qblk(0), kblk(h), kblk(2 * h), qblk(0), qblk(0), qrep, krow],
        out_specs=[
            pl.BlockSpec((s, HEAD_DIM), lambda hh, p, qi_r, kj_r: (0, hh)),
            kblk(0), kblk(0), krow,
            pl.BlockSpec((None, 1, s), lambda hh, p, qi_r, kj_r: (hh, 0, 0)),
        ],
        scratch_shapes=[pltpu.VMEM((s, HEAD_DIM), F32), pltpu.VMEM((tb, HEAD_DIM), F32),
                        pltpu.VMEM((tb, HEAD_DIM), F32), pltpu.VMEM((1, tb), F32), pltpu.VMEM((s, LANES), F32)],
        out_shape=[act, act, act, jax.ShapeDtypeStruct((h, 1, s), F32), jax.ShapeDtypeStruct((h, 1, s), F32)],
        operands=[qi, kj, z, z, z, o, d_o, lse2, f_row], semantics=("parallel", "arbitrary"), phases=phases)


GELU_C = math.sqrt(2.0 / math.pi)
GELU_A = 0.044715


def _gelu(x):
    return 0.5 * x * (1.0 + jnp.tanh(GELU_C * (x + GELU_A * (x * x * x))))


def _gelu_and_grad(x):
    t = jnp.tanh(GELU_C * (x + GELU_A * (x * x * x)))
    y = 0.5 * x * (1.0 + t)
    dy = 0.5 * (1.0 + t) + 0.5 * x * (1.0 - t * t) * (GELU_C * (1.0 + 3.0 * GELU_A * (x * x)))
    return y, dy


def _layernorm_parts(g):
    mu = jnp.mean(g, axis=-1, keepdims=True)
    xc = g - mu
    rs = lax.rsqrt(jnp.mean(xc * xc, axis=-1, keepdims=True) + EPS)
    return xc * rs, rs


def _spatial_mix(w_ref, bcol_ref, vv_b, n_heads, n_chunks):
    tril = _causal_mask(CHUNK)
    cols = []
    for hh in range(n_heads):
        wc = jnp.where(tril, w_ref[hh], 0.0).astype(BF16)
        lanes = slice(hh * HEAD_DIM, (hh + 1) * HEAD_DIM)
        rows = [jnp.dot(wc, vv_b[c * CHUNK:(c + 1) * CHUNK, lanes], preferred_element_type=F32)
                + bcol_ref[:, hh:hh + 1] for c in range(n_chunks)]
        cols.append(jnp.concatenate(rows, axis=0))
    return jnp.concatenate(cols, axis=1)


def _mix_fwd(z, o, ln_g, ln_b, w_s, b_col, attn_g, gm_g, n_heads, *, name, tr=256):
    s = z.shape[0]
    dg = n_heads * HEAD_DIM
    tr = _tile(s, tr, CHUNK)
    n_chunks = tr // CHUNK

    def body(zu_ref, zv_ref, o_ref, lg_ref, lb_ref, w_ref, bcol_ref, ag_ref, gg_ref, out_ref):
        u = _gelu(zu_ref[...].astype(F32))
        xhat, _ = _layernorm_parts(_gelu(zv_ref[...].astype(F32)))
        vv = xhat * lg_ref[...] + lb_ref[...]
        gm = u * _spatial_mix(w_ref, bcol_ref, vv.astype(BF16), n_heads, n_chunks)
        rg = lax.rsqrt(jnp.mean(gm * gm, axis=-1, keepdims=True) + EPS)
        ov = o_ref[...].astype(F32)
        ra = lax.rsqrt(jnp.mean(ov * ov, axis=-1, keepdims=True) + EPS)
        out_ref[:, :dg] = ((ov * ra) * ag_ref[...]).astype(BF16)
        out_ref[:, dg:] = ((gm * rg) * gg_ref[...]).astype(BF16)

    vec = pl.BlockSpec((1, dg), lambda i: (0, 0))
    return pl.pallas_call(
        body, name=name, out_shape=jax.ShapeDtypeStruct((s, 2 * dg), BF16), grid=(s // tr,),
        in_specs=[pl.BlockSpec((tr, dg), lambda i: (i, 3)), pl.BlockSpec((tr, dg), lambda i: (i, 4)),
                  pl.BlockSpec((tr, dg), lambda i: (i, 0)), vec, vec,
                  pl.BlockSpec((n_heads, CHUNK, CHUNK), lambda i: (0, 0, 0)),
                  pl.BlockSpec((CHUNK, n_heads), lambda i: (0, 0)), vec, vec],
        out_specs=pl.BlockSpec((tr, 2 * dg), lambda i: (i, 0)),
        compiler_params=_params("parallel"),
    )(z, z, o, ln_g, ln_b, w_s, b_col, attn_g, gm_g)


def _mix_bwd(z, o, d_merged, ln_g, ln_b, w_s, b_col, attn_g, gm_g, n_heads, *, name, tr=256):
    s = z.shape[0]
    dg = n_heads * HEAD_DIM
    tr = _tile(s, tr, CHUNK)
    n_chunks = tr // CHUNK

    def body(zu_ref, zv_ref, o_ref, dm_ref, lg_ref, lb_ref, w_ref, bcol_ref, ag_ref, gg_ref,
             do_ref, dzu_ref, dzv_ref, dw_ref, dbcol_ref, dlg_ref, dlb_ref, dag_ref, dgg_ref):
        @pl.when(pl.program_id(0) == 0)
        def _():
            for ref in (dw_ref, dbcol_ref, dlg_ref, dlb_ref, dag_ref, dgg_ref):
                ref[...] = jnp.zeros_like(ref)

        d_o, dag_rows = _rms_bwd_rows(dm_ref[:, :dg], o_ref[...].astype(F32), ag_ref[...])
        do_ref[...] = d_o.astype(BF16)
        dag_ref[...] += jnp.sum(dag_rows, axis=0, keepdims=True)

        u, du_dz = _gelu_and_grad(zu_ref[...].astype(F32))
        gv, dgv_dz = _gelu_and_grad(zv_ref[...].astype(F32))
        xhat, rs = _layernorm_parts(gv)
        lg = lg_ref[...]
        vv_b = (xhat * lg + lb_ref[...]).astype(BF16)
        mix = _spatial_mix(w_ref, bcol_ref, vv_b, n_heads, n_chunks)
        gm = u * mix
        d_gm, dgg_rows = _rms_bwd_rows(dm_ref[:, dg:], gm, gg_ref[...])
        dgg_ref[...] += jnp.sum(dgg_rows, axis=0, keepdims=True)
        dzu_ref[...] = ((d_gm * mix) * du_dz).astype(BF16)
        d_mix = d_gm * u
        d_mix_b = d_mix.astype(BF16)

        tril = _causal_mask(CHUNK)
        lane = lax.broadcasted_iota(jnp.int32, (CHUNK, n_heads), 1)
        cols = []
        db = jnp.zeros((CHUNK, n_heads), F32)
        for hh in range(n_heads):
            wc = jnp.where(tril, w_ref[hh], 0.0).astype(BF16)
            lanes = slice(hh * HEAD_DIM, (hh + 1) * HEAD_DIM)
            dw = jnp.zeros((CHUNK, CHUNK), F32)
            dmix_sum = jnp.zeros((CHUNK, HEAD_DIM), F32)
            rows = []
            for c in range(n_chunks):
                rws = slice(c * CHUNK, (c + 1) * CHUNK)
                dmb = d_mix_b[rws, lanes]
                dw += lax.dot_general(dmb, vv_b[rws, lanes], (((1,), (1,)), ((), ())), preferred_element_type=F32)
                rows.append(lax.dot_general(wc, dmb, (((0,), (0,)), ((), ())), preferred_element_type=F32))
                dmix_sum += d_mix[rws, lanes]
            dw_ref[hh] += jnp.where(tril, dw, 0.0)
            db += jnp.where(lane == hh, jnp.sum(dmix_sum, axis=-1, keepdims=True), 0.0)
            cols.append(jnp.concatenate(rows, axis=0))
        dbcol_ref[...] += db
        d_vv = jnp.concatenate(cols, axis=1)

        dlg_ref[...] += jnp.sum(d_vv * xhat, axis=0, keepdims=True)
        dlb_ref[...] += jnp.sum(d_vv, axis=0, keepdims=True)
        d_xhat = d_vv * lg
        d_gv = rs * (d_xhat - jnp.mean(d_xhat, axis=-1, keepdims=True)
                     - xhat * jnp.mean(d_xhat * xhat, axis=-1, keepdims=True))
        dzv_ref[...] = (d_gv * dgv_dz).astype(BF16)

    vec = pl.BlockSpec((1, dg), lambda i: (0, 0))
    wspec = pl.BlockSpec((n_heads, CHUNK, CHUNK), lambda i: (0, 0, 0))
    bspec = pl.BlockSpec((CHUNK, n_heads), lambda i: (0, 0))
    rowb = pl.BlockSpec((tr, dg), lambda i: (i, 0))
    act = jax.ShapeDtypeStruct((s, dg), BF16)
    vshape = jax.ShapeDtypeStruct((1, dg), F32)
    return pl.pallas_call(
        body, name=name,
        out_shape=(act, act, act, jax.ShapeDtypeStruct((n_heads, CHUNK, CHUNK), F32),
                   jax.ShapeDtypeStruct((CHUNK, n_heads), F32), vshape, vshape, vshape, vshape),
        grid=(s // tr,),
        in_specs=[pl.BlockSpec((tr, dg), lambda i: (i, 3)), pl.BlockSpec((tr, dg), lambda i: (i, 4)),
                  rowb, pl.BlockSpec((tr, 2 * dg), lambda i: (i, 0)), vec, vec, wspec, bspec, vec, vec],
        out_specs=(rowb, rowb, rowb, wspec, bspec, vec, vec, vec, vec),
        compiler_params=_params("arbitrary"),
    )(z, z, o, d_merged, ln_g, ln_b, w_s, b_col, attn_g, gm_g)


def _place():
    x, y, c = lax.axis_index("x"), lax.axis_index("y"), lax.axis_index("c")
    other_chips = [(1 - x, y), (x, 1 - y), (1 - x, 1 - y)]
    return x, y, c, other_chips


def _remote(src, dst, send_sem, recv_sem, to):
    return pltpu.make_async_remote_copy(src_ref=src, dst_ref=dst, send_sem=send_sem, recv_sem=recv_sem,
                                        device_id=to, device_id_type=MESH)


def _cast_into_slot(w, place, *, name, phases=()):
    rows, cols = w.shape
    tr, tc = _rc_tile(rows, cols)

    def body(place_ref, w_ref, o_ref):
        o_ref[...] = w_ref[...].astype(BF16)

    return _only(_call(
        body, name=name, n_prefetch=1, grid=(rows // tr, cols // tc),
        in_specs=[pl.BlockSpec((tr, tc), lambda i, j, pr: (i, j))],
        out_specs=[pl.BlockSpec((None, tr, tc), lambda i, j, pr: (pr[0], i, j))],
        out_shape=[jax.ShapeDtypeStruct((N_CHIPS, rows, cols), BF16)], operands=[place, w],
        semantics=("parallel", "parallel"), phases=phases))


def _exchange(phases, *, name):
    comm_in = [a for ph in phases for a in ph.arrays]
    comm_out = [jax.ShapeDtypeStruct(s.shape, s.dtype) for ph in phases for s in (ph.arrays if ph.in_place else ph.out_shapes)]
    aliases, at_in, at_out = {}, 0, 0
    for ph in phases:
        if ph.in_place:
            aliases.update({at_in + r: at_out + r for r in range(len(ph.arrays))})
        at_in, at_out = at_in + len(ph.arrays), at_out + ph.n_out
    n_sems = sum(ph.n_sems for ph in phases)

    def body(*refs):
        cin, cout = refs[:len(comm_in)], refs[len(comm_in):len(comm_in) + len(comm_out)]
        send_sems, recv_sems = refs[len(comm_in) + len(comm_out):]
        _run_phases(phases, ("start", "finish"), cin, cout, send_sems, recv_sems)

    return pl.pallas_call(
        body, name=name, out_shape=tuple(comm_out), in_specs=[ANY] * len(comm_in), out_specs=tuple([ANY] * len(comm_out)),
        input_output_aliases=aliases,
        scratch_shapes=[pltpu.SemaphoreType.DMA((n_sems,)), pltpu.SemaphoreType.DMA((n_sems,))],
    )(*comm_in)


GATHER_PARTS = 4


def _gather(bufs, stage, part=(0, GATHER_PARTS)):
    n = 3 * len(bufs)
    lo, hi = part

    def copies(outs, send, recv, d2d, incoming):
        x, y, c, chips = _place()
        for t, buf in enumerate(outs):
            half = buf.shape[2] // 2
            piece = half // GATHER_PARTS
            for k, (cx, cy) in enumerate(chips):
                i = 3 * t + k + (n if (d2d and stage == "both") else 0)
                cols = pl.ds(((1 - c) if (d2d and incoming) else c) * half + lo * piece, (hi - lo) * piece)
                blk = buf.at[(2 * cx + cy) if (d2d or incoming) else (2 * x + y), :, cols]
                yield _remote(blk, blk, send(i), recv(i), (x, y, 1 - c) if d2d else (cx, cy, c))

    def start(ins, outs, send, recv):
        for cp in copies(outs, send, recv, stage == "d2d", False):
            cp.start()

    def finish(ins, outs, send, recv):
        if stage == "both":
            for arrival, onward in zip(copies(outs, send, recv, False, True), copies(outs, send, recv, True, False)):
                arrival.wait_recv()
                onward.start()
        for cp in copies(outs, send, recv, stage != "ici", True):
            cp.wait_recv()
        for d2d in ((False, True) if stage == "both" else (stage == "d2d",)):
            for cp in copies(outs, send, recv, d2d, False):
                cp.wait_send()

    return _Phase(bufs, [], True, (2 if stage == "both" else 1) * n, start, finish)


def _merge(first, second):
    n_first = first.n_sems

    def later(sem):
        return lambda i: sem(n_first + i)

    def start(ins, outs, send, recv):
        first.start(ins, outs, send, recv)
        second.start(ins, outs, later(send), later(recv))

    def finish(ins, outs, send, recv):
        first.finish(ins, outs, send, recv)
        second.finish(ins, outs, later(send), later(recv))

    return _Phase(first.arrays, [], True, n_first + second.n_sems, start, finish)


def _swap_halves(grads):
    def copies(ins, outs, send, recv):
        x, y, c, _ = _place()
        for t, g in enumerate(ins):
            half = g.shape[2] // 2
            yield _remote(g.at[:, :, pl.ds((1 - c) * half, half)], outs[t], send(t), recv(t), (x, y, 1 - c))

    def start(ins, outs, send, recv):
        for cp in copies(ins, outs, send, recv):
            cp.start()

    def finish(ins, outs, send, recv):
        for cp in copies(ins, outs, send, recv):
            cp.wait()

    shapes = [jax.ShapeDtypeStruct((a.shape[0], a.shape[1], a.shape[2] // 2), a.dtype) for a in grads]
    return _Phase(grads, shapes, False, len(grads), start, finish)


def _add_halves(grad, received, place, *, name):
    ns, rows, half = received.shape
    tr, tc = _rc_tile(rows, half, pref_rows=1024)
    per = half // tc

    def body(place_ref, g_ref, r_ref, o_ref):
        o_ref[...] = (g_ref[...].astype(F32) + r_ref[...].astype(F32)).astype(BF16)

    grid_spec = pltpu.PrefetchScalarGridSpec(
        num_scalar_prefetch=1, grid=(ns, rows // tr, per),
        in_specs=[pl.BlockSpec((None, tr, tc), lambda s, i, j, pr: (s, i, pr[1] * per + j)),
                  pl.BlockSpec((None, tr, tc), lambda s, i, j, pr: (s, i, j))],
        out_specs=pl.BlockSpec((None, tr, tc), lambda s, i, j, pr: (s, i, j)),
    )
    return pl.pallas_call(
        body, name=name, grid_spec=grid_spec, out_shape=jax.ShapeDtypeStruct(received.shape, BF16),
        compiler_params=_params("parallel", "parallel", "parallel"),
    )(place, grad, received)


def _send_partials(parts, piece=(0, 1)):
    k_th, n_pieces = piece

    def cols(part):
        width = part.shape[2] // n_pieces
        return pl.ds(k_th * width, width)

    def start(ins, outs, send, recv):
        x, y, c, chips = _place()
        for t, part in enumerate(ins):
            for k, (cx, cy) in enumerate(chips):
                _remote(part.at[2 * cx + cy, :, cols(part)], outs[t].at[2 * x + y],
                        send(3 * t + k), recv(3 * t + k), (cx, cy, c)).start()

    def finish(ins, outs, send, recv):
        x, y, c, chips = _place()
        for t, part in enumerate(ins):
            for k, (cx, cy) in enumerate(chips):
                slot = outs[t].at[2 * cx + cy]
                _remote(slot, slot, send(3 * t + k), recv(3 * t + k), (cx, cy, c)).wait_recv()
        for t, part in enumerate(ins):
            for k, (cx, cy) in enumerate(chips):
                sent = part.at[2 * cx + cy, :, cols(part)]
                _remote(sent, sent, send(3 * t + k), recv(3 * t + k), (cx, cy, c)).wait_send()

    shapes = [jax.ShapeDtypeStruct(a.shape[:2] + (a.shape[2] // n_pieces,), a.dtype) for a in parts]
    return _Phase(parts, shapes, False, 3 * len(parts), start, finish)


def _sum_chips(parts, slots, place, *, name, piece=(0, 1), into=None):
    ns, rows, width = slots.shape
    k_th, n_pieces = piece
    half = width * n_pieces
    tr, tc = _rc_tile(rows, width, pref_rows=512)
    per = width // tc

    def body(place_ref, p_ref, s_ref, *rest):
        acc = p_ref[...].astype(F32)
        for k in range(ns):
            acc = acc + jnp.where(place_ref[0] == k, 0.0, s_ref[k].astype(F32))
        rest[-1][...] = acc

    grid_spec = pltpu.PrefetchScalarGridSpec(
        num_scalar_prefetch=1, grid=(rows // tr, per),
        in_specs=[pl.BlockSpec((None, tr, tc), lambda i, j, pr: (pr[0], i, k_th * per + j)),
                  pl.BlockSpec((ns, tr, tc), lambda i, j, pr: (0, i, j))] + ([ANY] if into is not None else []),
        out_specs=pl.BlockSpec((tr, tc), lambda i, j, pr: (i, (pr[1] * n_pieces + k_th) * per + j)),
    )
    return pl.pallas_call(
        body, name=name, grid_spec=grid_spec, out_shape=jax.ShapeDtypeStruct((rows, 2 * half), F32),
        input_output_aliases={3: 0} if into is not None else {},
        compiler_params=_params("parallel", "parallel"),
    )(place, parts, slots, *([into] if into is not None else []))


def _join_halves(bufs):
    def copies(outs, send, recv, incoming):
        x, y, c, _ = _place()
        for t, buf in enumerate(outs):
            half = buf.shape[1] // 2
            cols = buf.at[:, pl.ds(((1 - c) if incoming else c) * half, half)]
            yield _remote(cols, cols, send(t), recv(t), (x, y, 1 - c))

    def start(ins, outs, send, recv):
        for cp in copies(outs, send, recv, False):
            cp.start()

    def finish(ins, outs, send, recv):
        for cp in copies(outs, send, recv, True):
            cp.wait_recv()
        for cp in copies(outs, send, recv, False):
            cp.wait_send()

    return _Phase(bufs, [], True, len(bufs), start, finish)


def _allgather_small(buf, *, name):
    rows = buf.shape[0]

    def body(x_ref, out_ref, send_sems, recv_sems, local_sem):
        x, y, c, chips = _place()
        sibling = (x, y, 1 - c)

        def slot(px, py, pc):
            return out_ref.at[4 * px + 2 * py + pc]

        def copy(k, block, to, src=None):
            return _remote(slot(*block) if src is None else src, slot(*block), send_sems.at[k], recv_sems.at[k], to)

        mine = pltpu.make_async_copy(x_ref, slot(x, y, c), local_sem)
        mine.start()
        first = [copy(0, (x, y, c), sibling, src=x_ref)]
        first += [copy(1 + k, (x, y, c), (*chip, c), src=x_ref) for k, chip in enumerate(chips)]
        for cp in first:
            cp.start()
        passed = [copy(4 + k, (*chip, c), sibling) for k, chip in enumerate(chips)]
        for k, chip in enumerate(chips):
            copy(1 + k, (*chip, c), (x, y, c)).wait_recv()
            passed[k].start()
        copy(0, (x, y, 1 - c), (x, y, c)).wait_recv()
        for k, chip in enumerate(chips):
            copy(4 + k, (*chip, 1 - c), (x, y, c)).wait_recv()
        for cp in first + passed:
            cp.wait_send()
        mine.wait()

    return pl.pallas_call(
        body, name=name, out_shape=jax.ShapeDtypeStruct((N_DEV, rows, LANES), buf.dtype),
        in_specs=[pl.BlockSpec(memory_space=pltpu.VMEM)], out_specs=pl.BlockSpec(memory_space=pltpu.VMEM),
        scratch_shapes=[pltpu.SemaphoreType.DMA((7,)), pltpu.SemaphoreType.DMA((7,)), pltpu.SemaphoreType.DMA],
    )(buf)


def _adamw_math(w, g, m, v):
    m = ADAM_B1 * m + (1.0 - ADAM_B1) * g
    v = ADAM_B2 * v + (1.0 - ADAM_B2) * (g * g)
    m_hat = m / (1.0 - ADAM_B1 ** ADAM_STEP)
    v_hat = v / (1.0 - ADAM_B2 ** ADAM_STEP)
    delta = -ADAM_LR * (m_hat / (jnp.sqrt(v_hat) + ADAM_EPS) + ADAM_WD * w)
    return delta, m, v


def _adamw(w, g, m, v, *, name):
    rows, cols = w.shape
    tr, tc = _rc_tile(rows, cols)

    def body(w_ref, g_ref, m_ref, v_ref, go_ref, d_ref, mo_ref, vo_ref):
        g = g_ref[...]
        go_ref[...] = g
        d_ref[...], mo_ref[...], vo_ref[...] = _adamw_math(w_ref[...], g, m_ref[...], v_ref[...])

    blk = pl.BlockSpec((tr, tc), lambda i, j: (i, j))
    shape = jax.ShapeDtypeStruct((rows, cols), F32)
    return pl.pallas_call(
        body, name=name, out_shape=(shape, shape, shape, shape), grid=(rows // tr, cols // tc),
        in_specs=[blk] * 4, out_specs=(blk, blk, blk, blk), compiler_params=_params("parallel", "parallel"),
    )(w, g, m, v)


def _adamw_small(gathered, w, m, v, *, name):
    nd = gathered.shape[0]

    def body(gs_ref, w_ref, m_ref, v_ref, g_ref, d_ref, mo_ref, vo_ref):
        g = gs_ref[0]
        for k in range(1, nd):
            g = g + gs_ref[k]
        g_ref[...] = g
        d_ref[...], mo_ref[...], vo_ref[...] = _adamw_math(w_ref[...], g, m_ref[...], v_ref[...])

    shape = jax.ShapeDtypeStruct(w.shape, F32)
    return pl.pallas_call(body, name=name, out_shape=(shape, shape, shape, shape),
                          compiler_params=pltpu.CompilerParams(vmem_limit_bytes=VMEM_LIMIT_BYTES))(gathered, w, m, v)


def _pack(parts):
    flat = jnp.concatenate([p.reshape(-1).astype(F32) for p in parts])
    rows = -(-flat.shape[0] // (8 * LANES)) * 8
    return jnp.pad(flat, (0, rows * LANES - flat.shape[0])).reshape(rows, LANES)


def _unpack(buf, shapes):
    flat = buf.reshape(-1)
    out, pos = [], 0
    for shp in shapes:
        size = int(np.prod(shp))
        out.append(flat[pos:pos + size].reshape(shp))
        pos += size
    return out


ROW_BLOCK = 128


def _realign_rows(sources, segments, out_shape, *, name):
    n_slots, rows, cols = out_shape
    n_src = len(sources)
    per_slot = -(-rows // ROW_BLOCK)
    table = np.zeros((6, n_slots * per_slot, n_src), np.int32)
    for so in range(n_slots):
        for first, last, src, src_slot, src_row in segments[so]:
            for blk in range(first // ROW_BLOCK, (last - 1) // ROW_BLOCK + 1):
                lo, hi = max(first, blk * ROW_BLOCK), min(last, (blk + 1) * ROW_BLOCK)
                base = src_row + (blk * ROW_BLOCK - first)
                m0 = (base + lo - blk * ROW_BLOCK) // ROW_BLOCK
                at = so * per_slot + blk
                assert table[4, at, src] == 0, "two segments of one block share a source operand"
                table[:, at, src] = (src_slot, m0, base - m0 * ROW_BLOCK, lo - blk * ROW_BLOCK, hi - blk * ROW_BLOCK,
                                     min(2 * ROW_BLOCK, sources[src].shape[1] - m0 * ROW_BLOCK))
    last_block = [-(-a.shape[1] // ROW_BLOCK) - 1 for a in sources]

    def body(slot_ref, blk_ref, off_ref, lo_ref, hi_ref, valid_ref, *refs):
        o_ref, acc = refs[2 * n_src], refs[2 * n_src + 1]
        at = (pl.program_id(0) * per_slot + pl.program_id(1)) * n_src
        acc[...] = jnp.zeros_like(acc)
        for p in range(n_src):
            @pl.when(hi_ref[at + p] > lo_ref[at + p])
            def _():
                two = jnp.concatenate([refs[2 * p][...], refs[2 * p + 1][...]], axis=0)
                src_row = lax.broadcasted_iota(jnp.int32, two.shape, 0)
                two = jnp.where(src_row < valid_ref[at + p], two, jnp.zeros_like(two))
                r = lax.broadcasted_iota(jnp.int32, (ROW_BLOCK, 2 * ROW_BLOCK), 0)
                c = lax.broadcasted_iota(jnp.int32, (ROW_BLOCK, 2 * ROW_BLOCK), 1)
                place = (c == r + off_ref[at + p]) & (r >= lo_ref[at + p]) & (r < hi_ref[at + p])
                acc[...] += jnp.dot(place.astype(two.dtype), two, preferred_element_type=F32)
        o_ref[...] = acc[...].astype(o_ref.dtype)

    def src_spec(p, second):
        def index(so, i, slot_r, blk_r, off_r, lo_r, hi_r, valid_r):
            at = (so * per_slot + i) * n_src + p
            return slot_r[at], jnp.minimum(blk_r[at] + second, last_block[p]), 0
        return pl.BlockSpec((None, ROW_BLOCK, cols), index)

    grid_spec = pltpu.PrefetchScalarGridSpec(
        num_scalar_prefetch=6, grid=(n_slots, per_slot),
        in_specs=[src_spec(p, second) for p in range(n_src) for second in (0, 1)],
        out_specs=pl.BlockSpec((None, ROW_BLOCK, cols), lambda so, i, *_: (so, i, 0)),
        scratch_shapes=[pltpu.VMEM((ROW_BLOCK, cols), F32)],
    )
    flat = [jnp.asarray(table[k].reshape(-1)) for k in range(6)]
    return pl.pallas_call(
        body, name=name, grid_spec=grid_spec, out_shape=jax.ShapeDtypeStruct(out_shape, sources[0].dtype),
        compiler_params=_params("parallel", "arbitrary"),
    )(*flat, *[a for a in sources for _ in (0, 1)])


def _shard_rows(g, lo, hi):
    rs = g.shape[1]
    pieces = []
    for j in range(g.shape[0]):
        a, b = max(lo, j * rs), min(hi, (j + 1) * rs)
        if a < b:
            pieces.append(g[j, a - j * rs:b - j * rs])
    return pieces


def kernel(x, norm_mix_g, w_in, b_f, gmlp_ln_g, gmlp_ln_b, w_s, b_s, attn_out_g, gmlp_out_g, w_out, norm_ffn_g, w_ff1, w_ff2, norm_final_g, loss_target, m_norm_mix_g, m_w_in, m_b_f, m_gmlp_ln_g, m_gmlp_ln_b, m_w_s, m_b_s, m_attn_out_g, m_gmlp_out_g, m_w_out, m_norm_ffn_g, m_w_ff1, m_w_ff2, m_norm_final_g, v_norm_mix_g, v_w_in, v_b_f, v_gmlp_ln_g, v_gmlp_ln_b, v_w_s, v_b_s, v_attn_out_g, v_gmlp_out_g, v_w_out, v_norm_ffn_g, v_w_ff1, v_w_ff2, v_norm_final_g):
    seq, d_model = x.shape[1], x.shape[2]
    d_attn = d_model // 2
    n_heads = d_attn // HEAD_DIM
    qkv = 3 * d_attn
    shard_cols = w_in.shape[2]
    assert N_CHIPS * shard_cols == qkv + n_heads + 2 * d_attn
    xs = x.reshape(seq, d_model)
    target = loss_target.reshape(seq, d_model)

    place = jnp.stack([2 * lax.axis_index("x") + lax.axis_index("y"), lax.axis_index("c")]).astype(jnp.int32)
    names = ["w_in", "w_out", "w_ff1", "w_ff2"]
    wt_in, mt_in, vt_in = w_in[0].T, m_w_in[0].T, v_w_in[0].T
    b_in, _ = _cast_into_slot(wt_in, place, name="cast_w_in")
    b_out, _ = _cast_into_slot(w_out[0], place, name="cast_w_out")
    b_ff1, (b_in,) = _cast_into_slot(w_ff1[0], place, name="cast_w_ff1", phases=[_gather([b_in], "ici", (0, 2))])
    b_ff2, (b_in,) = _cast_into_slot(w_ff2[0], place, name="cast_w_ff2",
                                     phases=[_merge(_gather([b_in], "ici", (2, 3)), _gather([b_in], "d2d", (0, 2)))])
    h, (b_in,) = _rmsnorm_fwd(xs, norm_mix_g, name="norm_mix",
                              phases=[_merge(_gather([b_in], "ici", (3, 4)), _gather([b_in], "d2d", (2, 3)))])
    (g_in,) = _exchange([_gather([b_in], "d2d", (3, 4))], name="allgather_w_in_tail")
    n_cols = N_CHIPS * shard_cols
    gate_slot, gate_row = divmod(qkv, shard_cols)
    assert gate_row + n_heads <= shard_cols
    pieces = []
    for j in range(N_CHIPS):
        if j == gate_slot:
            pieces += [(j, 0, gate_row), (j, gate_row + n_heads, shard_cols - gate_row - n_heads)]
        else:
            pieces.append((j, 0, shard_cols))
    fwd_segments, at = [[]], 0
    for order, (j, src_row, size) in enumerate(pieces):
        fwd_segments[0].append((at, at + size, order % 2, j, src_row))
        at += size
    wt_main = _realign_rows([g_in, g_in], fwd_segments, (1, n_cols - n_heads, d_model), name="w_in_rows")[0]
    wt_f = jnp.pad(jnp.concatenate(_shard_rows(g_in, qkv, qkv + n_heads), axis=0), ((0, LANES - n_heads), (0, 0)))
    b_f_pad = jnp.pad(b_f, ((0, 0), (0, LANES - n_heads)))
    b_col = b_s[0].T

    first, rest = (0, 1), (1, GATHER_PARTS)
    z, (b_out, b_ff1) = _matmul(h, wt_main, name="in_proj", out_dtype=BF16, trans_b=True,
                                phases=[_gather([b_out], "ici"), _gather([b_ff1], "ici", first)])
    zb, f_cum = _forget_fwd(h, wt_f, b_f_pad, name="forget_fwd")
    f_row = f_cum[:, :n_heads].T[:, None, :]
    (o, lse2), (b_ff1, b_out) = _attn_fwd(z, f_row, n_heads, name="attn_fwd",
                                          phases=[_gather([b_ff1], "ici", rest), _gather([b_out], "d2d")])
    merged = _mix_fwd(z, o, gmlp_ln_g, gmlp_ln_b, w_s[0], b_col, attn_out_g, gmlp_out_g, n_heads, name="mix_fwd")
    w_out_full = b_out.reshape(2 * d_attn, d_model)
    x1, (b_ff1, b_ff2) = _matmul(merged, w_out_full, name="out_proj", out_dtype=F32, residual=xs,
                                 phases=[_gather([b_ff1], "d2d"), _gather([b_ff2], "ici", first)])
    h2, _ = _rmsnorm_fwd(x1, norm_ffn_g, name="norm_ffn")
    a, (b_ff2,) = _matmul(h2, b_ff1, name="ff1", out_dtype=BF16, relu=True, b_sharded=True,
                          phases=[_merge(_gather([b_ff2], "both", rest), _gather([b_ff2], "d2d", first))])
    w_ff2_full = b_ff2.reshape(N_CHIPS * b_ff2.shape[1], d_model)
    x2, _ = _matmul(a, w_ff2_full, name="ff2", out_dtype=F32, square_lhs=True, residual=x1)
    dx2, dx2_b, dg_final, loss = _loss_and_final_bwd(x2, target, norm_final_g.reshape(1, d_model), name="loss_head")

    def pair_sum(g, r, nm):
        return _add_halves(g, r, place, name="grads_pair_sum_" + nm)

    def chip_sum(p, q, nm, **piece):
        return _sum_chips(p, q, place, name="grads_chip_sum_" + nm, **piece)

    dw_ff2, _ = _matmul(a, dx2_b, name="ff2_dw", out_dtype=BF16, trans_a=True, square_lhs=True)
    dw_ff2 = dw_ff2.reshape(N_CHIPS, -1, d_model)
    da, (r_ff2,) = _matmul(dx2_b, w_ff2_full, name="ff2_dlhs", out_dtype=BF16, trans_b=True, scale2_by=a,
                           phases=[_swap_halves([dw_ff2])])
    ps_ff2 = pair_sum(dw_ff2, r_ff2, "w_ff2")
    dh2, (q_ff2a,) = _matmul(da, b_ff1, name="ff1_dlhs", out_dtype=F32, trans_b=True, b_sharded=True,
                             phases=[_send_partials([ps_ff2], (0, 2))])
    dw_ff1, (q_ff2b,) = _matmul(h2, da, name="ff1_dw", out_dtype=BF16, trans_a=True, out_sharded=True, tk=seq,
                                phases=[_send_partials([ps_ff2], (1, 2))])
    g_ff2 = chip_sum(ps_ff2, q_ff2a, "w_ff2_a", piece=(0, 2))
    g_ff2 = chip_sum(ps_ff2, q_ff2b, "w_ff2_b", piece=(1, 2), into=g_ff2)
    (dx1, dg_ffn, dx1_b), (g_ff2,) = _rmsnorm_bwd(dh2, x1, dx2, norm_ffn_g, name="norm_ffn_bwd", rounded_copy=True,
                                                   phases=[_join_halves([g_ff2])])
    dw_out, _ = _matmul(merged, dx1_b, name="out_proj_dw", out_dtype=BF16, trans_a=True, tk=seq)
    dw_out = dw_out.reshape(N_CHIPS, -1, d_model)
    d_merged, (r_ff1, r_out) = _matmul(dx1_b, w_out_full, name="out_proj_dlhs", out_dtype=F32, trans_b=True,
                                       phases=[_swap_halves([dw_ff1, dw_out])])
    ps_ff1, ps_out = pair_sum(dw_ff1, r_ff1, "w_ff1"), pair_sum(dw_out, r_out, "w_out")
    d_o, dzu, dzv, dw_s, db_col, dlg, dlb, dag, dgg = _mix_bwd(
        z, o, d_merged, gmlp_ln_g, gmlp_ln_b, w_s[0], b_col, attn_out_g, gmlp_out_g, n_heads, name="mix_bwd")
    (dq, dk, dv, d_f_key, d_f_query), (q_ff1, q_out) = _attn_bwd(
        z, o, d_o, lse2, f_row, n_heads, name="attn_bwd", phases=[_send_partials([ps_ff1, ps_out])])
    g_ff1, g_out = chip_sum(ps_ff1, q_ff1, "w_ff1"), chip_sum(ps_out, q_out, "w_out")
    d_f = d_f_key.reshape(n_heads, seq) + d_f_query.reshape(n_heads, seq)
    d_f_pad = jnp.pad(d_f.T, ((0, 0), (0, LANES - n_heads)))
    dzf, db_f = _forget_bwd(d_f_pad, zb, name="forget_bwd")
    dz = jnp.concatenate([dq, dk, dv, dzu, dzv], axis=1)
    dwt_main, (g_ff1, g_out) = _matmul(dz, h, name="in_proj_dw", out_dtype=BF16, trans_a=True, tk=seq,
                                       phases=[_join_halves([g_ff1, g_out])])
    dwt_f, _ = _matmul(dzf, h, name="gate_dw", out_dtype=BF16, trans_a=True)
    bwd_segments = []
    for j in range(N_CHIPS):
        first = j * shard_cols
        if j < gate_slot:
            bwd_segments.append([(0, shard_cols, 0, 0, first)])
        elif j > gate_slot:
            bwd_segments.append([(0, shard_cols, 0, 0, first - n_heads)])
        else:
            bwd_segments.append([(0, gate_row, 0, 0, first), (gate_row, gate_row + n_heads, 1, 0, 0),
                                 (gate_row + n_heads, shard_cols, 2, 0, qkv)])
    dw_in = _realign_rows([dwt_main[None], dwt_f[None], dwt_main[None]], bwd_segments,
                          (N_CHIPS, shard_cols, d_model), name="dw_in_rows")
    dh_gate, (r_in,) = _matmul(dzf, wt_f, name="gate_dlhs", out_dtype=F32, phases=[_swap_halves([dw_in])])
    ps_in = pair_sum(dw_in, r_in, "w_in")
    dh, (q_in,) = _matmul(dz, wt_main, name="in_proj_dlhs", out_dtype=F32, residual=dh_gate, tk=2560,
                          phases=[_send_partials([ps_in])])
    g_in_sum = chip_sum(ps_in, q_in, "w_in")
    (grad_x, dg_mix), _ = _rmsnorm_bwd(dh, xs, dx1, norm_mix_g, name="norm_mix_bwd")
    (g_in_sum,) = _exchange([_join_halves([g_in_sum])], name="grads_join_w_in")

    big = {}
    for nm, g, w, m, v in zip(names, (g_in_sum, g_out, g_ff1, g_ff2), (wt_in, w_out[0], w_ff1[0], w_ff2[0]),
                              (mt_in, m_w_out[0], m_w_ff1[0], m_w_ff2[0]), (vt_in, v_w_out[0], v_w_ff1[0], v_w_ff2[0])):
        big[nm] = tuple((t.T if nm == "w_in" else t)[None] for t in _adamw(w, g, m, v, name="adamw_" + nm))

    small_w = [norm_mix_g, b_f, gmlp_ln_g, gmlp_ln_b, w_s, b_s, attn_out_g, gmlp_out_g, norm_ffn_g, norm_final_g]
    small_m = [m_norm_mix_g, m_b_f, m_gmlp_ln_g, m_gmlp_ln_b, m_w_s, m_b_s, m_attn_out_g, m_gmlp_out_g, m_norm_ffn_g, m_norm_final_g]
    small_v = [v_norm_mix_g, v_b_f, v_gmlp_ln_g, v_gmlp_ln_b, v_w_s, v_b_s, v_attn_out_g, v_gmlp_out_g, v_norm_ffn_g, v_norm_final_g]
    small_g = [dg_mix, db_f[:, :n_heads], dlg, dlb, dw_s, db_col.T, dag, dgg, dg_ffn, dg_final]
    shapes = [w.shape for w in small_w]
    gathered = _allgather_small(_pack(small_g), name="allgather_small_grads")
    packed = _adamw_small(gathered, _pack(small_w), _pack(small_m), _pack(small_v), name="adamw_small")
    sg, sd, sm, sv = (_unpack(p, shapes) for p in packed)
    small_names = ["norm_mix_g", "b_f", "gmlp_ln_g", "gmlp_ln_b", "w_s", "b_s", "attn_out_g", "gmlp_out_g", "norm_ffn_g", "norm_final_g"]
    small = {nm: (sg[i], sd[i], sm[i], sv[i]) for i, nm in enumerate(small_names)}

    order = ["norm_mix_g", "w_in", "b_f", "gmlp_ln_g", "gmlp_ln_b", "w_s", "b_s", "attn_out_g", "gmlp_out_g", "w_out",
             "norm_ffn_g", "w_ff1", "w_ff2", "norm_final_g"]
    result = {**small, **big}
    total_loss = lax.psum(loss[0, 0], ("x", "y", "c"))
    outs = [total_loss, grad_x.reshape(x.shape)]
    for part in range(4):
        outs += [result[nm][part] for nm in order]
    return tuple(outs)
```

```python
import functools
import math

import numpy as np
import jax
import jax.numpy as jnp
from jax import lax
from jax.experimental import pallas as pl
from jax.experimental.pallas import tpu as pltpu

HEAD_DIM = 128
CHUNK = 128
EPS = 1e-6
LANES = 128
N_CHIPS = 4
N_DEV = 8
VMEM_LIMIT_BYTES = 56 * 1024 * 1024

ADAM_LR = 0.001
ADAM_B1 = 0.9
ADAM_B2 = 0.999
ADAM_EPS = 1e-08
ADAM_WD = 0.01
ADAM_STEP = 10

BF16 = jnp.bfloat16
F32 = jnp.float32
MESH = pl.DeviceIdType.MESH
ANY = pl.BlockSpec(memory_space=pl.ANY)
NEG_BIG = -1e30


def _params(*sem):
    return pltpu.CompilerParams(dimension_semantics=tuple(sem), vmem_limit_bytes=VMEM_LIMIT_BYTES)


def _tile(n, pref, unit):
    t = (min(pref, n) // unit) * unit
    while t >= unit:
        if n % t == 0:
            return t
        t -= unit
    return n


def _rc_tile(rows, cols, pref_rows=256, pref_cols=256):
    if rows % 16 == 0:
        return _tile(rows, pref_rows, 16), cols
    return rows, _tile(cols, pref_cols, LANES)


class _Phase:
    def __init__(self, arrays, out_shapes, in_place, n_sems, start, finish):
        self.arrays, self.out_shapes, self.in_place = list(arrays), list(out_shapes), in_place
        self.n_sems, self.start, self.finish = n_sems, start, finish

    @property
    def n_out(self):
        return len(self.arrays) if self.in_place else len(self.out_shapes)


def _run_phases(phases, steps, comm_in, comm_out, send_sems, recv_sems):
    at_in = at_out = at_sem = 0
    for ph in phases:
        for step in steps:
            getattr(ph, step)(comm_in[at_in:at_in + len(ph.arrays)], comm_out[at_out:at_out + ph.n_out],
                              lambda i, base=at_sem: send_sems.at[base + i], lambda i, base=at_sem: recv_sems.at[base + i])
        at_in, at_out, at_sem = at_in + len(ph.arrays), at_out + ph.n_out, at_sem + ph.n_sems


def _call(body, *, name, grid, in_specs, out_specs, out_shape, operands, semantics, scratch_shapes=(),
          n_prefetch=0, phases=()):
    in_specs, out_specs, out_shape = list(in_specs), list(out_specs), list(out_shape)
    scratch_shapes = list(scratch_shapes)
    n_in, n_out, n_scr = len(operands) - n_prefetch, len(out_shape), len(scratch_shapes)
    comm_in = [a for ph in phases for a in ph.arrays]
    comm_out = [jax.ShapeDtypeStruct(s.shape, s.dtype) for ph in phases
                for s in (ph.arrays if ph.in_place else ph.out_shapes)]
    aliases, at_in, at_out = {}, n_prefetch + n_in, n_out
    for ph in phases:
        if ph.in_place:
            aliases.update({at_in + r: at_out + r for r in range(len(ph.arrays))})
        at_in, at_out = at_in + len(ph.arrays), at_out + ph.n_out
    n_sems = sum(ph.n_sems for ph in phases)

    def hosted(*refs):
        pre, rest = refs[:n_prefetch], refs[n_prefetch:]
        ins, rest = rest[:n_in], rest[n_in:]
        cin, rest = rest[:len(comm_in)], rest[len(comm_in):]
        outs, rest = rest[:n_out], rest[n_out:]
        cout, rest = rest[:len(comm_out)], rest[len(comm_out):]
        scr = rest[:n_scr]
        if phases:
            send_sems, recv_sems = rest[n_scr:]
            ids = [pl.program_id(ax) for ax in range(len(grid))]
            first = functools.reduce(jnp.logical_and, [i == 0 for i in ids])
            last = functools.reduce(jnp.logical_and, [i == g - 1 for i, g in zip(ids, grid)])

            @pl.when(first)
            def _():
                _run_phases(phases, ("start",), cin, cout, send_sems, recv_sems)

        body(*pre, *ins, *outs, *scr)
        if phases:
            @pl.when(last)
            def _():
                _run_phases(phases, ("finish",), cin, cout, send_sems, recv_sems)

    all_in = in_specs + [ANY] * len(comm_in)
    all_out = out_specs + [ANY] * len(comm_out)
    all_scr = scratch_shapes + ([pltpu.SemaphoreType.DMA((n_sems,)), pltpu.SemaphoreType.DMA((n_sems,))] if phases else [])
    if phases:
        semantics = ("arbitrary",) * len(grid)
    kwargs = dict(name=name, out_shape=tuple(out_shape + comm_out), compiler_params=_params(*semantics),
                  input_output_aliases=aliases)
    if n_prefetch:
        kwargs["grid_spec"] = pltpu.PrefetchScalarGridSpec(
            num_scalar_prefetch=n_prefetch, grid=grid, in_specs=all_in, out_specs=tuple(all_out), scratch_shapes=all_scr)
    else:
        kwargs.update(grid=grid, in_specs=all_in, out_specs=tuple(all_out), scratch_shapes=all_scr)
    res = pl.pallas_call(hosted, **kwargs)(*operands, *comm_in)
    return tuple(res[:n_out]), tuple(res[n_out:])


def _only(results):
    outs, comm = results
    return outs[0] if len(outs) == 1 else outs, comm


def _matmul(a, b, *, name, out_dtype, trans_a=False, trans_b=False, tm=1024, tn=1024, tk=2048,
            square_lhs=False, relu=False, residual=None, scale2_by=None,
            b_sharded=False, out_sharded=False, phases=()):
    m, k = (a.shape[1], a.shape[0]) if trans_a else a.shape
    if b_sharded:
        if trans_b:
            n, ks = b.shape[1], b.shape[2]
            assert N_CHIPS * ks == k
        else:
            ns = b.shape[2]
            n = N_CHIPS * ns
            assert b.shape[1] == k
    else:
        n = b.shape[0] if trans_b else b.shape[1]
        assert (b.shape[1] if trans_b else b.shape[0]) == k
    tm = _tile(m, tm, 128)
    tn = _tile(n // N_CHIPS if (out_sharded or (b_sharded and not trans_b)) else n, tn, 128)
    tk = _tile(k // N_CHIPS if (b_sharded and trans_b) else k, tk, 128)
    nk = k // tk

    if trans_a:
        a_spec = pl.BlockSpec((tk, tm), lambda i, j, kk: (kk, i))
    else:
        a_spec = pl.BlockSpec((tm, tk), lambda i, j, kk: (i, kk))
    if b_sharded and trans_b:
        per = ks // tk
        assert per * tk == ks
        b_spec = pl.BlockSpec((None, tn, tk), lambda i, j, kk: (kk // per, j, kk % per))
    elif b_sharded:
        per = ns // tn
        assert per * tn == ns
        b_spec = pl.BlockSpec((None, tk, tn), lambda i, j, kk: (j // per, kk, j % per))
    elif trans_b:
        b_spec = pl.BlockSpec((tn, tk), lambda i, j, kk: (j, kk))
    else:
        b_spec = pl.BlockSpec((tk, tn), lambda i, j, kk: (kk, j))
    if out_sharded:
        ns_out = n // N_CHIPS
        per_o = ns_out // tn
        assert per_o * tn == ns_out
        out_shape = jax.ShapeDtypeStruct((N_CHIPS, m, ns_out), out_dtype)
        o_spec = pl.BlockSpec((None, tm, tn), lambda i, j, kk: (j // per_o, i, j % per_o))
    else:
        out_shape = jax.ShapeDtypeStruct((m, n), out_dtype)
        o_spec = pl.BlockSpec((tm, tn), lambda i, j, kk: (i, j))
    mn_spec = pl.BlockSpec((tm, tn), lambda i, j, kk: (i, j))

    operands, in_specs = [a, b], [a_spec, b_spec]
    if scale2_by is not None:
        operands.append(scale2_by)
        in_specs.append(mn_spec)
    if residual is not None:
        operands.append(residual)
        in_specs.append(mn_spec)
    dims = (((0 if trans_a else 1,), (1 if trans_b else 0,)), ((), ()))

    def body(*refs):
        a_ref, b_ref = refs[0], refs[1]
        pos = 2
        scale_ref = res_ref = None
        if scale2_by is not None:
            scale_ref = refs[pos]
            pos += 1
        if residual is not None:
            res_ref = refs[pos]
            pos += 1
        o_ref = refs[pos]
        kk = pl.program_id(2)

        av = a_ref[...]
        if square_lhs:
            av = av.astype(F32)
            av = av * av
        part = lax.dot_general(av.astype(BF16), b_ref[...].astype(BF16), dims, preferred_element_type=F32)

        def finish(r):
            if relu:
                r = jnp.maximum(r, 0.0)
            if scale_ref is not None:
                r = r * (2.0 * scale_ref[...].astype(F32))
            if res_ref is not None:
                r = r + res_ref[...].astype(F32)
            o_ref[...] = r.astype(out_dtype)

        if nk == 1:
            finish(part)
        else:
            acc_ref = refs[pos + 1]

            @pl.when(kk == 0)
            def _():
                acc_ref[...] = part

            @pl.when(jnp.logical_and(kk > 0, kk < nk - 1))
            def _():
                acc_ref[...] += part

            @pl.when(kk == nk - 1)
            def _():
                finish(acc_ref[...] + part)

    return _only(_call(
        body, name=name, out_shape=[out_shape], grid=(m // tm, n // tn, nk),
        in_specs=in_specs, out_specs=[o_spec], operands=operands,
        scratch_shapes=[pltpu.VMEM((tm, tn), F32)] if nk > 1 else [],
        semantics=("parallel", "parallel", "arbitrary"), phases=phases))


def _rmsnorm_fwd(x, g, *, name, tr=512, phases=()):
    s, d = x.shape
    tr = _tile(s, tr, 8)

    def body(x_ref, g_ref, o_ref):
        xv = x_ref[...]
        r = lax.rsqrt(jnp.mean(xv * xv, axis=-1, keepdims=True) + EPS)
        o_ref[...] = ((xv * r) * g_ref[...]).astype(BF16)

    return _only(_call(
        body, name=name, out_shape=[jax.ShapeDtypeStruct((s, d), BF16)], grid=(s // tr,),
        in_specs=[pl.BlockSpec((tr, d), lambda i: (i, 0)), pl.BlockSpec((1, d), lambda i: (0, 0))],
        out_specs=[pl.BlockSpec((tr, d), lambda i: (i, 0))], operands=[x, g],
        semantics=("parallel",), phases=phases))


def _rms_bwd_rows(dy, xv, g):
    d = xv.shape[-1]
    r = lax.rsqrt(jnp.mean(xv * xv, axis=-1, keepdims=True) + EPS)
    gdy = dy * g
    dot = jnp.sum(gdy * xv, axis=-1, keepdims=True)
    dx = gdy * r - xv * (r * r * r) * (dot / d)
    return dx, dy * (xv * r)


def _rmsnorm_bwd(dy, x, res, g, *, name, tr=256, rounded_copy=False, phases=()):
    s, d = x.shape
    tr = _tile(s, tr, 8)

    def body(dy_ref, x_ref, res_ref, g_ref, dx_ref, dg_ref, *dxb_ref):
        @pl.when(pl.program_id(0) == 0)
        def _():
            dg_ref[...] = jnp.zeros_like(dg_ref)

        dx, dg_rows = _rms_bwd_rows(dy_ref[...].astype(F32), x_ref[...], g_ref[...])
        out = res_ref[...] + dx
        dx_ref[...] = out
        if rounded_copy:
            dxb_ref[0][...] = out.astype(BF16)
        dg_ref[...] += jnp.sum(dg_rows, axis=0, keepdims=True)

    row = pl.BlockSpec((tr, d), lambda i: (i, 0))
    vec = pl.BlockSpec((1, d), lambda i: (0, 0))
    extra = [jax.ShapeDtypeStruct((s, d), BF16)] if rounded_copy else []
    return _call(
        body, name=name,
        out_shape=[jax.ShapeDtypeStruct((s, d), F32), jax.ShapeDtypeStruct((1, d), F32)] + extra,
        grid=(s // tr,), in_specs=[row, row, row, vec], out_specs=[row, vec] + [row] * len(extra),
        operands=[dy, x, res, g], semantics=("arbitrary",), phases=phases)


def _loss_and_final_bwd(x2, target, g, *, name, tr=256):
    s, d = x2.shape
    tr = _tile(s, tr, 8)

    def body(x_ref, t_ref, g_ref, dx_ref, dxb_ref, dg_ref, loss_ref):
        @pl.when(pl.program_id(0) == 0)
        def _():
            dg_ref[...] = jnp.zeros_like(dg_ref)
            loss_ref[...] = jnp.zeros_like(loss_ref)

        xv, gv = x_ref[...], g_ref[...]
        r = lax.rsqrt(jnp.mean(xv * xv, axis=-1, keepdims=True) + EPS)
        err = (xv * r) * gv - t_ref[...]
        row_loss = jnp.mean(err * err, axis=-1, keepdims=True)
        loss_ref[...] += 0.5 * jnp.sum(row_loss, axis=0, keepdims=True)
        dx, dg_rows = _rms_bwd_rows(err / d, xv, gv)
        dx_ref[...] = dx
        dxb_ref[...] = dx.astype(BF16)
        dg_ref[...] += jnp.sum(dg_rows, axis=0, keepdims=True)

    row = pl.BlockSpec((tr, d), lambda i: (i, 0))
    vec = pl.BlockSpec((1, d), lambda i: (0, 0))
    one = pl.BlockSpec((1, 1), lambda i: (0, 0))
    return pl.pallas_call(
        body, name=name,
        out_shape=(jax.ShapeDtypeStruct((s, d), F32), jax.ShapeDtypeStruct((s, d), BF16),
                   jax.ShapeDtypeStruct((1, d), F32), jax.ShapeDtypeStruct((1, 1), F32)),
        grid=(s // tr,), in_specs=[row, row, vec], out_specs=(row, row, vec, one),
        compiler_params=_params("arbitrary"),
    )(x2, target, g)


def _tri_ones(n, lower):
    r = lax.broadcasted_iota(jnp.int32, (n, n), 0)
    c = lax.broadcasted_iota(jnp.int32, (n, n), 1)
    return jnp.where((c <= r) if lower else (c >= r), 1.0, 0.0).astype(F32)


def _forget_fwd(h, w_f, b_f, *, name, tr=256):
    s, d = h.shape
    tr = _tile(s, tr, 8)

    def body(h_ref, w_ref, b_ref, zb_ref, f_ref, carry):
        @pl.when(pl.program_id(0) == 0)
        def _():
            carry[...] = jnp.zeros_like(carry)

        zb = lax.dot_general(h_ref[...], w_ref[...], (((1,), (1,)), ((), ())), preferred_element_type=F32) + b_ref[...]
        zb_ref[...] = zb
        log_f = jnp.minimum(zb, 0.0) - jnp.log(1.0 + jnp.exp(-jnp.abs(zb)))
        run = jnp.dot(_tri_ones(tr, True), log_f, preferred_element_type=F32,
                      precision=lax.Precision.HIGHEST) + carry[...]
        f_ref[...] = run
        carry[...] = run[tr - 1:tr, :]

    row = pl.BlockSpec((tr, LANES), lambda i: (i, 0))
    return pl.pallas_call(
        body, name=name,
        out_shape=(jax.ShapeDtypeStruct((s, LANES), F32), jax.ShapeDtypeStruct((s, LANES), F32)),
        grid=(s // tr,),
        in_specs=[pl.BlockSpec((tr, d), lambda i: (i, 0)), pl.BlockSpec((LANES, d), lambda i: (0, 0)),
                  pl.BlockSpec((1, LANES), lambda i: (0, 0))],
        out_specs=(row, row), scratch_shapes=[pltpu.VMEM((1, LANES), F32)],
        compiler_params=_params("arbitrary"),
    )(h, w_f, b_f)


def _forget_bwd(d_f, zb, *, name, tr=256):
    s = zb.shape[0]
    tr = _tile(s, tr, 8)
    nb = s // tr

    def body(df_ref, zb_ref, dz_ref, db_ref, carry):
        @pl.when(pl.program_id(0) == 0)
        def _():
            carry[...] = jnp.zeros_like(carry)
            db_ref[...] = jnp.zeros_like(db_ref)

        run = jnp.dot(_tri_ones(tr, False), df_ref[...], preferred_element_type=F32,
                      precision=lax.Precision.HIGHEST) + carry[...]
        carry[...] = run[0:1, :]
        dz = run / (1.0 + jnp.exp(zb_ref[...]))
        dz_ref[...] = dz.astype(BF16)
        db_ref[...] += jnp.sum(dz, axis=0, keepdims=True)

    row = pl.BlockSpec((tr, LANES), lambda i: (nb - 1 - i, 0))
    return pl.pallas_call(
        body, name=name,
        out_shape=(jax.ShapeDtypeStruct((s, LANES), BF16), jax.ShapeDtypeStruct((1, LANES), F32)),
        grid=(nb,), in_specs=[row, row], out_specs=(row, pl.BlockSpec((1, LANES), lambda i: (0, 0))),
        scratch_shapes=[pltpu.VMEM((1, LANES), F32)],
        compiler_params=_params("arbitrary"),
    )(d_f, zb)


def _pairs(nblk, by_kv):
    if by_kv:
        pr = [(i, j) for j in range(nblk) for i in range(j, nblk)]
    else:
        pr = [(i, j) for i in range(nblk) for j in range(i + 1)]
    return (jnp.asarray(np.array([p[0] for p in pr], np.int32)), jnp.asarray(np.array([p[1] for p in pr], np.int32)))


def _causal_mask(t):
    r = lax.broadcasted_iota(jnp.int32, (t, t), 0)
    c = lax.broadcasted_iota(jnp.int32, (t, t), 1)
    return c <= r


LOG2E = math.log2(math.e)
QK_TO_LOG2 = LOG2E / math.sqrt(HEAD_DIM)


def _attn_logits2(q, k, fk_row):
    sc = lax.dot_general(q, k, (((1,), (1,)), ((), ())), preferred_element_type=F32)
    return sc * QK_TO_LOG2 - fk_row * LOG2E


def _attn_fwd(z, f_row, n_heads, *, name, tb=1024, phases=()):
    s = z.shape[0]
    tb = _tile(s, tb, 128)
    nblk = s // tb
    rep = tb // LANES
    qi, kj = _pairs(nblk, by_kv=False)

    def body(qi_ref, kj_ref, q_ref, k_ref, v_ref, fk_ref, o_ref, lse_ref, m_sc, l_sc, acc_sc):
        p = pl.program_id(1)
        i, j = qi_ref[p], kj_ref[p]

        @pl.when(j == 0)
        def _():
            m_sc[...] = jnp.full_like(m_sc, NEG_BIG)
            l_sc[...] = jnp.zeros_like(l_sc)
            acc_sc[...] = jnp.zeros_like(acc_sc)

        def update(masked):
            s2 = _attn_logits2(q_ref[...], k_ref[...], fk_ref[...])
            if masked:
                s2 = jnp.where(_causal_mask(tb), s2, NEG_BIG)
            m_old = m_sc[...]
            m_new = jnp.maximum(m_old, jnp.max(s2, axis=-1, keepdims=True))
            alpha = jnp.exp2(m_old - m_new)
            pv = jnp.exp2(s2 - jnp.tile(m_new, (1, rep)))
            l_sc[...] = alpha * l_sc[...] + jnp.sum(pv, axis=-1, keepdims=True)
            acc_sc[...] = alpha * acc_sc[...] + jnp.dot(pv.astype(BF16), v_ref[...], preferred_element_type=F32)
            m_sc[...] = m_new

        @pl.when(j < i)
        def _():
            update(False)

        @pl.when(j == i)
        def _():
            update(True)
            o_ref[...] = (acc_sc[...] / l_sc[...]).astype(BF16)
            lse_ref[...] = m_sc[...] + jnp.log2(l_sc[...])

    h = n_heads
    return _call(
        body, name=name, n_prefetch=2, grid=(h, int(qi.shape[0])),
        in_specs=[
            pl.BlockSpec((tb, HEAD_DIM), lambda hh, p, qi_r, kj_r: (qi_r[p], hh)),
            pl.BlockSpec((tb, HEAD_DIM), lambda hh, p, qi_r, kj_r: (kj_r[p], h + hh)),
            pl.BlockSpec((tb, HEAD_DIM), lambda hh, p, qi_r, kj_r: (kj_r[p], 2 * h + hh)),
            pl.BlockSpec((None, 1, tb), lambda hh, p, qi_r, kj_r: (hh, 0, kj_r[p])),
        ],
        out_specs=[
            pl.BlockSpec((tb, HEAD_DIM), lambda hh, p, qi_r, kj_r: (qi_r[p], hh)),
            pl.BlockSpec((None, tb, LANES), lambda hh, p, qi_r, kj_r: (hh, qi_r[p], 0)),
        ],
        scratch_shapes=[pltpu.VMEM((tb, LANES), F32), pltpu.VMEM((tb, LANES), F32), pltpu.VMEM((tb, HEAD_DIM), F32)],
        out_shape=[jax.ShapeDtypeStruct((s, h * HEAD_DIM), BF16), jax.ShapeDtypeStruct((h, s, LANES), F32)],
        operands=[qi, kj, z, z, z, f_row], semantics=("parallel", "arbitrary"), phases=phases)


def _attn_bwd(z, o, d_o, lse2, f_row, n_heads, *, name, tb=1024, phases=()):
    s = z.shape[0]
    tb = _tile(s, tb, 128)
    nblk = s // tb
    rep = tb // LANES
    qi, kj = _pairs(nblk, by_kv=True)
    n_pairs = int(qi.shape[0])
    scale = 1.0 / math.sqrt(HEAD_DIM)
    h = n_heads

    def body(qi_ref, kj_ref, q_ref, k_ref, v_ref, o_ref, do_ref, lse_ref, fk_ref,
             dq_ref, dk_ref, dv_ref, df_ref, dfq_ref, dq_sc, dk_sc, dv_sc, df_sc, dfq_sc):
        p = pl.program_id(1)
        i, j = qi_ref[p], kj_ref[p]

        @pl.when(p == 0)
        def _():
            dq_sc[...] = jnp.zeros_like(dq_sc)
            dfq_sc[...] = jnp.zeros_like(dfq_sc)

        @pl.when(i == j)
        def _():
            dk_sc[...] = jnp.zeros_like(dk_sc)
            dv_sc[...] = jnp.zeros_like(dv_sc)
            df_sc[...] = jnp.zeros_like(df_sc)

        def update(masked):
            q, k, v, do = q_ref[...], k_ref[...], v_ref[...], do_ref[...]
            delta = jnp.sum(do.astype(F32) * o_ref[...].astype(F32), axis=-1, keepdims=True)
            pv = jnp.exp2(_attn_logits2(q, k, fk_ref[...]) - jnp.tile(lse_ref[...], (1, rep)))
            if masked:
                pv = jnp.where(_causal_mask(tb), pv, 0.0)
            dp = lax.dot_general(do, v, (((1,), (1,)), ((), ())), preferred_element_type=F32)
            ds = pv * (dp - delta)
            ds_b = ds.astype(BF16)
            dv_sc[...] += lax.dot_general(pv.astype(BF16), do, (((0,), (0,)), ((), ())), preferred_element_type=F32)
            dk_sc[...] += lax.dot_general(ds_b, q, (((0,), (0,)), ((), ())), preferred_element_type=F32)
            rows = pl.ds(pl.multiple_of(i * tb, tb), tb)
            dq_sc[rows, :] += jnp.dot(ds_b, k, preferred_element_type=F32)
            df_sc[...] -= jnp.sum(ds, axis=0, keepdims=True)
            dfq_sc[rows, :] += jnp.broadcast_to(jnp.sum(ds, axis=1, keepdims=True), (tb, LANES))

        @pl.when(i > j)
        def _():
            update(False)

        @pl.when(i == j)
        def _():
            update(True)

        @pl.when(i == nblk - 1)
        def _():
            dk_ref[...] = (dk_sc[...] * scale).astype(BF16)
            dv_ref[...] = dv_sc[...].astype(BF16)
            df_ref[...] = df_sc[...]

        @pl.when(p == n_pairs - 1)
        def _():
            dq_ref[...] = (dq_sc[...] * scale).astype(BF16)
            dfq_ref[...] = jnp.transpose(dfq_sc[...])[0:1, :]

    qblk = lambda off: pl.BlockSpec((tb, HEAD_DIM), lambda hh, p, qi_r, kj_r: (qi_r[p], off + hh))
    kblk = lambda off: pl.BlockSpec((tb, HEAD_DIM), lambda hh, p, qi_r, kj_r: (kj_r[p], off + hh))
    qrep = pl.BlockSpec((None, tb, LANES), lambda hh, p, qi_r, kj_r: (hh, qi_r[p], 0))
    krow = pl.BlockSpec((None, 1, tb), lambda hh, p, qi_r, kj_r: (hh, 0, kj_r[p]))
    act = jax.ShapeDtypeStruct((s, h * HEAD_DIM), BF16)
    return _call(
        body, name=name, n_prefetch=2, grid=(h, n_pairs),
        in_specs=[qblk(0), kblk(h), kblk(2 * h), qblk(0), qblk(0), qrep, krow],
        out_specs=[
            pl.BlockSpec((s, HEAD_DIM), lambda hh, p, qi_r, kj_r: (0, hh)),
            kblk(0), kblk(0), krow,
            pl.BlockSpec((None, 1, s), lambda hh, p, qi_r, kj_r: (hh, 0, 0)),
        ],
        scratch_shapes=[pltpu.VMEM((s, HEAD_DIM), F32), pltpu.VMEM((tb, HEAD_DIM), F32),
                        pltpu.VMEM((tb, HEAD_DIM), F32), pltpu.VMEM((1, tb), F32), pltpu.VMEM((s, LANES), F32)],
        out_shape=[act, act, act, jax.ShapeDtypeStruct((h, 1, s), F32), jax.ShapeDtypeStruct((h, 1, s), F32)],
        operands=[qi, kj, z, z, z, o, d_o, lse2, f_row], semantics=("parallel", "arbitrary"), phases=phases)


GELU_C = math.sqrt(2.0 / math.pi)
GELU_A = 0.044715


def _gelu(x):
    return 0.5 * x * (1.0 + jnp.tanh(GELU_C * (x + GELU_A * (x * x * x))))


def _gelu_and_grad(x):
    t = jnp.tanh(GELU_C * (x + GELU_A * (x * x * x)))
    y = 0.5 * x * (1.0 + t)
    dy = 0.5 * (1.0 + t) + 0.5 * x * (1.0 - t * t) * (GELU_C * (1.0 + 3.0 * GELU_A * (x * x)))
    return y, dy


def _layernorm_parts(g):
    mu = jnp.mean(g, axis=-1, keepdims=True)
    xc = g - mu
    rs = lax.rsqrt(jnp.mean(xc * xc, axis=-1, keepdims=True) + EPS)
    return xc * rs, rs


def _spatial_mix(w_ref, bcol_ref, vv_b, n_heads, n_chunks):
    tril = _causal_mask(CHUNK)
    cols = []
    for hh in range(n_heads):
        wc = jnp.where(tril, w_ref[hh], 0.0).astype(BF16)
        lanes = slice(hh * HEAD_DIM, (hh + 1) * HEAD_DIM)
        rows = [jnp.dot(wc, vv_b[c * CHUNK:(c + 1) * CHUNK, lanes], preferred_element_type=F32)
                + bcol_ref[:, hh:hh + 1] for c in range(n_chunks)]
        cols.append(jnp.concatenate(rows, axis=0))
    return jnp.concatenate(cols, axis=1)


def _mix_fwd(z, o, ln_g, ln_b, w_s, b_col, attn_g, gm_g, n_heads, *, name, tr=256):
    s = z.shape[0]
    dg = n_heads * HEAD_DIM
    tr = _tile(s, tr, CHUNK)
    n_chunks = tr // CHUNK

    def body(zu_ref, zv_ref, o_ref, lg_ref, lb_ref, w_ref, bcol_ref, ag_ref, gg_ref, out_ref):
        u = _gelu(zu_ref[...].astype(F32))
        xhat, _ = _layernorm_parts(_gelu(zv_ref[...].astype(F32)))
        vv = xhat * lg_ref[...] + lb_ref[...]
        gm = u * _spatial_mix(w_ref, bcol_ref, vv.astype(BF16), n_heads, n_chunks)
        rg = lax.rsqrt(jnp.mean(gm * gm, axis=-1, keepdims=True) + EPS)
        ov = o_ref[...].astype(F32)
        ra = lax.rsqrt(jnp.mean(ov * ov, axis=-1, keepdims=True) + EPS)
        out_ref[:, :dg] = ((ov * ra) * ag_ref[...]).astype(BF16)
        out_ref[:, dg:] = ((gm * rg) * gg_ref[...]).astype(BF16)

    vec = pl.BlockSpec((1, dg), lambda i: (0, 0))
    return pl.pallas_call(
        body, name=name, out_shape=jax.ShapeDtypeStruct((s, 2 * dg), BF16), grid=(s // tr,),
        in_specs=[pl.BlockSpec((tr, dg), lambda i: (i, 3)), pl.BlockSpec((tr, dg), lambda i: (i, 4)),
                  pl.BlockSpec((tr, dg), lambda i: (i, 0)), vec, vec,
                  pl.BlockSpec((n_heads, CHUNK, CHUNK), lambda i: (0, 0, 0)),
                  pl.BlockSpec((CHUNK, n_heads), lambda i: (0, 0)), vec, vec],
        out_specs=pl.BlockSpec((tr, 2 * dg), lambda i: (i, 0)),
        compiler_params=_params("parallel"),
    )(z, z, o, ln_g, ln_b, w_s, b_col, attn_g, gm_g)


def _mix_bwd(z, o, d_merged, ln_g, ln_b, w_s, b_col, attn_g, gm_g, n_heads, *, name, tr=256):
    s = z.shape[0]
    dg = n_heads * HEAD_DIM
    tr = _tile(s, tr, CHUNK)
    n_chunks = tr // CHUNK

    def body(zu_ref, zv_ref, o_ref, dm_ref, lg_ref, lb_ref, w_ref, bcol_ref, ag_ref, gg_ref,
             do_ref, dzu_ref, dzv_ref, dw_ref, dbcol_ref, dlg_ref, dlb_ref, dag_ref, dgg_ref):
        @pl.when(pl.program_id(0) == 0)
        def _():
            for ref in (dw_ref, dbcol_ref, dlg_ref, dlb_ref, dag_ref, dgg_ref):
                ref[...] = jnp.zeros_like(ref)

        d_o, dag_rows = _rms_bwd_rows(dm_ref[:, :dg], o_ref[...].astype(F32), ag_ref[...])
        do_ref[...] = d_o.astype(BF16)
        dag_ref[...] += jnp.sum(dag_rows, axis=0, keepdims=True)

        u, du_dz = _gelu_and_grad(zu_ref[...].astype(F32))
        gv, dgv_dz = _gelu_and_grad(zv_ref[...].astype(F32))
        xhat, rs = _layernorm_parts(gv)
        lg = lg_ref[...]
        vv_b = (xhat * lg + lb_ref[...]).astype(BF16)
        mix = _spatial_mix(w_ref, bcol_ref, vv_b, n_heads, n_chunks)
        gm = u * mix
        d_gm, dgg_rows = _rms_bwd_rows(dm_ref[:, dg:], gm, gg_ref[...])
        dgg_ref[...] += jnp.sum(dgg_rows, axis=0, keepdims=True)
        dzu_ref[...] = ((d_gm * mix) * du_dz).astype(BF16)
        d_mix = d_gm * u
        d_mix_b = d_mix.astype(BF16)

        tril = _causal_mask(CHUNK)
        lane = lax.broadcasted_iota(jnp.int32, (CHUNK, n_heads), 1)
        cols = []
        db = jnp.zeros((CHUNK, n_heads), F32)
        for hh in range(n_heads):
            wc = jnp.where(tril, w_ref[hh], 0.0).astype(BF16)
            lanes = slice(hh * HEAD_DIM, (hh + 1) * HEAD_DIM)
            dw = jnp.zeros((CHUNK, CHUNK), F32)
            dmix_sum = jnp.zeros((CHUNK, HEAD_DIM), F32)
            rows = []
            for c in range(n_chunks):
                rws = slice(c * CHUNK, (c + 1) * CHUNK)
                dmb = d_mix_b[rws, lanes]
                dw += lax.dot_general(dmb, vv_b[rws, lanes], (((1,), (1,)), ((), ())), preferred_element_type=F32)
                rows.append(lax.dot_general(wc, dmb, (((0,), (0,)), ((), ())), preferred_element_type=F32))
                dmix_sum += d_mix[rws, lanes]
            dw_ref[hh] += jnp.where(tril, dw, 0.0)
            db += jnp.where(lane == hh, jnp.sum(dmix_sum, axis=-1, keepdims=True), 0.0)
            cols.append(jnp.concatenate(rows, axis=0))
        dbcol_ref[...] += db
        d_vv = jnp.concatenate(cols, axis=1)

        dlg_ref[...] += jnp.sum(d_vv * xhat, axis=0, keepdims=True)
        dlb_ref[...] += jnp.sum(d_vv, axis=0, keepdims=True)
        d_xhat = d_vv * lg
        d_gv = rs * (d_xhat - jnp.mean(d_xhat, axis=-1, keepdims=True)
                     - xhat * jnp.mean(d_xhat * xhat, axis=-1, keepdims=True))
        dzv_ref[...] = (d_gv * dgv_dz).astype(BF16)

    vec = pl.BlockSpec((1, dg), lambda i: (0, 0))
    wspec = pl.BlockSpec((n_heads, CHUNK, CHUNK), lambda i: (0, 0, 0))
    bspec = pl.BlockSpec((CHUNK, n_heads), lambda i: (0, 0))
    rowb = pl.BlockSpec((tr, dg), lambda i: (i, 0))
    act = jax.ShapeDtypeStruct((s, dg), BF16)
    vshape = jax.ShapeDtypeStruct((1, dg), F32)
    return pl.pallas_call(
        body, name=name,
        out_shape=(act, act, act, jax.ShapeDtypeStruct((n_heads, CHUNK, CHUNK), F32),
                   jax.ShapeDtypeStruct((CHUNK, n_heads), F32), vshape, vshape, vshape, vshape),
        grid=(s // tr,),
        in_specs=[pl.BlockSpec((tr, dg), lambda i: (i, 3)), pl.BlockSpec((tr, dg), lambda i: (i, 4)),
                  rowb, pl.BlockSpec((tr, 2 * dg), lambda i: (i, 0)), vec, vec, wspec, bspec, vec, vec],
        out_specs=(rowb, rowb, rowb, wspec, bspec, vec, vec, vec, vec),
        compiler_params=_params("arbitrary"),
    )(z, z, o, d_merged, ln_g, ln_b, w_s, b_col, attn_g, gm_g)


def _place():
    x, y, c = lax.axis_index("x"), lax.axis_index("y"), lax.axis_index("c")
    other_chips = [(1 - x, y), (x, 1 - y), (1 - x, 1 - y)]
    return x, y, c, other_chips


def _remote(src, dst, send_sem, recv_sem, to):
    return pltpu.make_async_remote_copy(src_ref=src, dst_ref=dst, send_sem=send_sem, recv_sem=recv_sem,
                                        device_id=to, device_id_type=MESH)


def _cast_into_slot(w, place, *, name, phases=()):
    rows, cols = w.shape
    tr, tc = _rc_tile(rows, cols)

    def body(place_ref, w_ref, o_ref):
        o_ref[...] = w_ref[...].astype(BF16)

    return _only(_call(
        body, name=name, n_prefetch=1, grid=(rows // tr, cols // tc),
        in_specs=[pl.BlockSpec((tr, tc), lambda i, j, pr: (i, j))],
        out_specs=[pl.BlockSpec((None, tr, tc), lambda i, j, pr: (pr[0], i, j))],
        out_shape=[jax.ShapeDtypeStruct((N_CHIPS, rows, cols), BF16)], operands=[place, w],
        semantics=("parallel", "parallel"), phases=phases))


def _exchange(phases, *, name):
    comm_in = [a for ph in phases for a in ph.arrays]
    comm_out = [jax.ShapeDtypeStruct(s.shape, s.dtype) for ph in phases for s in (ph.arrays if ph.in_place else ph.out_shapes)]
    aliases, at_in, at_out = {}, 0, 0
    for ph in phases:
        if ph.in_place:
            aliases.update({at_in + r: at_out + r for r in range(len(ph.arrays))})
        at_in, at_out = at_in + len(ph.arrays), at_out + ph.n_out
    n_sems = sum(ph.n_sems for ph in phases)

    def body(*refs):
        cin, cout = refs[:len(comm_in)], refs[len(comm_in):len(comm_in) + len(comm_out)]
        send_sems, recv_sems = refs[len(comm_in) + len(comm_out):]
        _run_phases(phases, ("start", "finish"), cin, cout, send_sems, recv_sems)

    return pl.pallas_call(
        body, name=name, out_shape=tuple(comm_out), in_specs=[ANY] * len(comm_in), out_specs=tuple([ANY] * len(comm_out)),
        input_output_aliases=aliases,
        scratch_shapes=[pltpu.SemaphoreType.DMA((n_sems,)), pltpu.SemaphoreType.DMA((n_sems,))],
    )(*comm_in)


GATHER_PARTS = 4


def _gather(bufs, stage, part=(0, GATHER_PARTS)):
    n = 3 * len(bufs)
    lo, hi = part

    def copies(outs, send, recv, d2d, incoming):
        x, y, c, chips = _place()
        for t, buf in enumerate(outs):
            half = buf.shape[2] // 2
            piece = half // GATHER_PARTS
            for k, (cx, cy) in enumerate(chips):
                i = 3 * t + k + (n if (d2d and stage == "both") else 0)
                cols = pl.ds(((1 - c) if (d2d and incoming) else c) * half + lo * piece, (hi - lo) * piece)
                blk = buf.at[(2 * cx + cy) if (d2d or incoming) else (2 * x + y), :, cols]
                yield _remote(blk, blk, send(i), recv(i), (x, y, 1 - c) if d2d else (cx, cy, c))

    def start(ins, outs, send, recv):
        for cp in copies(outs, send, recv, stage == "d2d", False):
            cp.start()

    def finish(ins, outs, send, recv):
        if stage == "both":
            for arrival, onward in zip(copies(outs, send, recv, False, True), copies(outs, send, recv, True, False)):
                arrival.wait_recv()
                onward.start()
        for cp in copies(outs, send, recv, stage != "ici", True):
            cp.wait_recv()
        for d2d in ((False, True) if stage == "both" else (stage == "d2d",)):
            for cp in copies(outs, send, recv, d2d, False):
                cp.wait_send()

    return _Phase(bufs, [], True, (2 if stage == "both" else 1) * n, start, finish)


def _merge(first, second):
    n_first = first.n_sems

    def later(sem):
        return lambda i: sem(n_first + i)

    def start(ins, outs, send, recv):
        first.start(ins, outs, send, recv)
        second.start(ins, outs, later(send), later(recv))

    def finish(ins, outs, send, recv):
        first.finish(ins, outs, send, recv)
        second.finish(ins, outs, later(send), later(recv))

    return _Phase(first.arrays, [], True, n_first + second.n_sems, start, finish)


def _swap_halves(grads):
    def copies(ins, outs, send, recv):
        x, y, c, _ = _place()
        for t, g in enumerate(ins):
            half = g.shape[2] // 2
            yield _remote(g.at[:, :, pl.ds((1 - c) * half, half)], outs[t], send(t), recv(t), (x, y, 1 - c))

    def start(ins, outs, send, recv):
        for cp in copies(ins, outs, send, recv):
            cp.start()

    def finish(ins, outs, send, recv):
        for cp in copies(ins, outs, send, recv):
            cp.wait()

    shapes = [jax.ShapeDtypeStruct((a.shape[0], a.shape[1], a.shape[2] // 2), a.dtype) for a in grads]
    return _Phase(grads, shapes, False, len(grads), start, finish)


def _add_halves(grad, received, place, *, name):
    ns, rows, half = received.shape
    tr, tc = _rc_tile(rows, half, pref_rows=1024)
    per = half // tc

    def body(place_ref, g_ref, r_ref, o_ref):
        o_ref[...] = (g_ref[...].astype(F32) + r_ref[...].astype(F32)).astype(BF16)

    grid_spec = pltpu.PrefetchScalarGridSpec(
        num_scalar_prefetch=1, grid=(ns, rows // tr, per),
        in_specs=[pl.BlockSpec((None, tr, tc), lambda s, i, j, pr: (s, i, pr[1] * per + j)),
                  pl.BlockSpec((None, tr, tc), lambda s, i, j, pr: (s, i, j))],
        out_specs=pl.BlockSpec((None, tr, tc), lambda s, i, j, pr: (s, i, j)),
    )
    return pl.pallas_call(
        body, name=name, grid_spec=grid_spec, out_shape=jax.ShapeDtypeStruct(received.shape, BF16),
        compiler_params=_params("parallel", "parallel", "parallel"),
    )(place, grad, received)


def _send_partials(parts, piece=(0, 1)):
    k_th, n_pieces = piece

    def cols(part):
        width = part.shape[2] // n_pieces
        return pl.ds(k_th * width, width)

    def start(ins, outs, send, recv):
        x, y, c, chips = _place()
        for t, part in enumerate(ins):
            for k, (cx, cy) in enumerate(chips):
                _remote(part.at[2 * cx + cy, :, cols(part)], outs[t].at[2 * x + y],
                        send(3 * t + k), recv(3 * t + k), (cx, cy, c)).start()

    def finish(ins, outs, send, recv):
        x, y, c, chips = _place()
        for t, part in enumerate(ins):
            for k, (cx, cy) in enumerate(chips):
                slot = outs[t].at[2 * cx + cy]
                _remote(slot, slot, send(3 * t + k), recv(3 * t + k), (cx, cy, c)).wait_recv()
        for t, part in enumerate(ins):
            for k, (cx, cy) in enumerate(chips):
                sent = part.at[2 * cx + cy, :, cols(part)]
                _remote(sent, sent, send(3 * t + k), recv(3 * t + k), (cx, cy, c)).wait_send()

    shapes = [jax.ShapeDtypeStruct(a.shape[:2] + (a.shape[2] // n_pieces,), a.dtype) for a in parts]
    return _Phase(parts, shapes, False, 3 * len(parts), start, finish)


def _sum_chips(parts, slots, place, *, name, piece=(0, 1), into=None):
    ns, rows, width = slots.shape
    k_th, n_pieces = piece
    half = width * n_pieces
    tr, tc = _rc_tile(rows, width, pref_rows=512)
    per = width // tc

    def body(place_ref, p_ref, s_ref, *rest):
        acc = p_ref[...].astype(F32)
        for k in range(ns):
            acc = acc + jnp.where(place_ref[0] == k, 0.0, s_ref[k].astype(F32))
        rest[-1][...] = acc

    grid_spec = pltpu.PrefetchScalarGridSpec(
        num_scalar_prefetch=1, grid=(rows // tr, per),
        in_specs=[pl.BlockSpec((None, tr, tc), lambda i, j, pr: (pr[0], i, k_th * per + j)),
                  pl.BlockSpec((ns, tr, tc), lambda i, j, pr: (0, i, j))] + ([ANY] if into is not None else []),
        out_specs=pl.BlockSpec((tr, tc), lambda i, j, pr: (i, (pr[1] * n_pieces + k_th) * per + j)),
    )
    return pl.pallas_call(
        body, name=name, grid_spec=grid_spec, out_shape=jax.ShapeDtypeStruct((rows, 2 * half), F32),
        input_output_aliases={3: 0} if into is not None else {},
        compiler_params=_params("parallel", "parallel"),
    )(place, parts, slots, *([into] if into is not None else []))


def _join_halves(bufs):
    def copies(outs, send, recv, incoming):
        x, y, c, _ = _place()
        for t, buf in enumerate(outs):
            half = buf.shape[1] // 2
            cols = buf.at[:, pl.ds(((1 - c) if incoming else c) * half, half)]
            yield _remote(cols, cols, send(t), recv(t), (x, y, 1 - c))

    def start(ins, outs, send, recv):
        for cp in copies(outs, send, recv, False):
            cp.start()

    def finish(ins, outs, send, recv):
        for cp in copies(outs, send, recv, True):
            cp.wait_recv()
        for cp in copies(outs, send, recv, False):
            cp.wait_send()

    return _Phase(bufs, [], True, len(bufs), start, finish)


def _gather_small(buf):
    def slot(out, px, py, pc):
        return out.at[4 * px + 2 * py + pc]

    def start(ins, outs, send, recv):
        x, y, c, chips = _place()
        mine = slot(outs[0], x, y, c)
        _remote(ins[0], mine, send(0), recv(0), (x, y, 1 - c)).start()
        for k, (cx, cy) in enumerate(chips):
            _remote(ins[0], mine, send(1 + k), recv(1 + k), (cx, cy, c)).start()

    def finish(ins, outs, send, recv):
        x, y, c, chips = _place()
        sibling = (x, y, 1 - c)
        for k, (cx, cy) in enumerate(chips):
            arrived = slot(outs[0], cx, cy, c)
            _remote(arrived, arrived, send(1 + k), recv(1 + k), sibling).wait_recv()
            _remote(arrived, arrived, send(4 + k), recv(4 + k), sibling).start()
        theirs = slot(outs[0], x, y, 1 - c)
        _remote(theirs, theirs, send(0), recv(0), sibling).wait_recv()
        for k, (cx, cy) in enumerate(chips):
            passed = slot(outs[0], cx, cy, 1 - c)
            _remote(passed, passed, send(4 + k), recv(4 + k), sibling).wait_recv()
        for i in range(7):
            _remote(ins[0], ins[0], send(i), recv(i), sibling).wait_send()

    return _Phase([buf], [jax.ShapeDtypeStruct((N_DEV,) + buf.shape, buf.dtype)], False, 7, start, finish)


def _adamw_math(w, g, m, v):
    m = ADAM_B1 * m + (1.0 - ADAM_B1) * g
    v = ADAM_B2 * v + (1.0 - ADAM_B2) * (g * g)
    m_hat = m / (1.0 - ADAM_B1 ** ADAM_STEP)
    v_hat = v / (1.0 - ADAM_B2 ** ADAM_STEP)
    delta = -ADAM_LR * (m_hat / (jnp.sqrt(v_hat) + ADAM_EPS) + ADAM_WD * w)
    return delta, m, v


def _adamw(w, g, m, v, *, name):
    rows, cols = w.shape
    tr, tc = _rc_tile(rows, cols)

    def body(w_ref, g_ref, m_ref, v_ref, go_ref, d_ref, mo_ref, vo_ref):
        g = g_ref[...]
        go_ref[...] = g
        d_ref[...], mo_ref[...], vo_ref[...] = _adamw_math(w_ref[...], g, m_ref[...], v_ref[...])

    blk = pl.BlockSpec((tr, tc), lambda i, j: (i, j))
    shape = jax.ShapeDtypeStruct((rows, cols), F32)
    return pl.pallas_call(
        body, name=name, out_shape=(shape, shape, shape, shape), grid=(rows // tr, cols // tc),
        in_specs=[blk] * 4, out_specs=(blk, blk, blk, blk), compiler_params=_params("parallel", "parallel"),
    )(w, g, m, v)


def _adamw_small(gathered, own, place, w, m, v, *, name):
    nd = gathered.shape[0]

    def body(place_ref, gs_ref, own_ref, w_ref, m_ref, v_ref, g_ref, d_ref, mo_ref, vo_ref):
        me = 2 * place_ref[0] + place_ref[1]
        g = jnp.zeros(own_ref.shape, F32)
        for k in range(nd):
            g = g + jnp.where(me == k, own_ref[...], gs_ref[k])
        g_ref[...] = g
        d_ref[...], mo_ref[...], vo_ref[...] = _adamw_math(w_ref[...], g, m_ref[...], v_ref[...])

    whole = pl.BlockSpec(w.shape, lambda i, pr: (0, 0))
    grid_spec = pltpu.PrefetchScalarGridSpec(
        num_scalar_prefetch=1, grid=(1,),
        in_specs=[pl.BlockSpec(gathered.shape, lambda i, pr: (0, 0, 0)), whole, whole, whole, whole],
        out_specs=(whole, whole, whole, whole))
    shape = jax.ShapeDtypeStruct(w.shape, F32)
    return pl.pallas_call(body, name=name, grid_spec=grid_spec, out_shape=(shape, shape, shape, shape),
                          compiler_params=_params("arbitrary"))(place, gathered, own, w, m, v)


def _pack(parts):
    flat = jnp.concatenate([p.reshape(-1).astype(F32) for p in parts])
    rows = -(-flat.shape[0] // (8 * LANES)) * 8
    return jnp.pad(flat, (0, rows * LANES - flat.shape[0])).reshape(rows, LANES)


def _unpack(buf, shapes):
    flat = buf.reshape(-1)
    out, pos = [], 0
    for shp in shapes:
        size = int(np.prod(shp))
        out.append(flat[pos:pos + size].reshape(shp))
        pos += size
    return out


ROW_BLOCK = 256


def _realign_rows(sources, segments, out_shape, *, name):
    n_slots, rows, cols = out_shape
    n_src = len(sources)
    per_slot = -(-rows // ROW_BLOCK)
    table = np.zeros((6, n_slots * per_slot, n_src), np.int32)
    for so in range(n_slots):
        for first, last, src, src_slot, src_row in segments[so]:
            for blk in range(first // ROW_BLOCK, (last - 1) // ROW_BLOCK + 1):
                lo, hi = max(first, blk * ROW_BLOCK), min(last, (blk + 1) * ROW_BLOCK)
                base = src_row + (blk * ROW_BLOCK - first)
                m0 = (base + lo - blk * ROW_BLOCK) // ROW_BLOCK
                at = so * per_slot + blk
                assert table[4, at, src] == 0, "two segments of one block share a source operand"
                table[:, at, src] = (src_slot, m0, base - m0 * ROW_BLOCK, lo - blk * ROW_BLOCK, hi - blk * ROW_BLOCK,
                                     min(2 * ROW_BLOCK, sources[src].shape[1] - m0 * ROW_BLOCK))
    last_block = [-(-a.shape[1] // ROW_BLOCK) - 1 for a in sources]

    def body(slot_ref, blk_ref, off_ref, lo_ref, hi_ref, valid_ref, *refs):
        o_ref, acc = refs[2 * n_src], refs[2 * n_src + 1]
        at = (pl.program_id(0) * per_slot + pl.program_id(1)) * n_src
        acc[...] = jnp.zeros_like(acc)
        for p in range(n_src):
            @pl.when(hi_ref[at + p] > lo_ref[at + p])
            def _():
                two = jnp.concatenate([refs[2 * p][...], refs[2 * p + 1][...]], axis=0)
                src_row = lax.broadcasted_iota(jnp.int32, two.shape, 0)
                two = jnp.where(src_row < valid_ref[at + p], two, jnp.zeros_like(two))
                r = lax.broadcasted_iota(jnp.int32, (ROW_BLOCK, 2 * ROW_BLOCK), 0)
                c = lax.broadcasted_iota(jnp.int32, (ROW_BLOCK, 2 * ROW_BLOCK), 1)
                place = (c == r + off_ref[at + p]) & (r >= lo_ref[at + p]) & (r < hi_ref[at + p])
                acc[...] += jnp.dot(place.astype(two.dtype), two, preferred_element_type=F32)
        o_ref[...] = acc[...].astype(o_ref.dtype)

    def src_spec(p, second):
        def index(so, i, slot_r, blk_r, off_r, lo_r, hi_r, valid_r):
            at = (so * per_slot + i) * n_src + p
            return slot_r[at], jnp.minimum(blk_r[at] + second, last_block[p]), 0
        return pl.BlockSpec((None, ROW_BLOCK, cols), index)

    grid_spec = pltpu.PrefetchScalarGridSpec(
        num_scalar_prefetch=6, grid=(n_slots, per_slot),
        in_specs=[src_spec(p, second) for p in range(n_src) for second in (0, 1)],
        out_specs=pl.BlockSpec((None, ROW_BLOCK, cols), lambda so, i, *_: (so, i, 0)),
        scratch_shapes=[pltpu.VMEM((ROW_BLOCK, cols), F32)],
    )
    flat = [jnp.asarray(table[k].reshape(-1)) for k in range(6)]
    return pl.pallas_call(
        body, name=name, grid_spec=grid_spec, out_shape=jax.ShapeDtypeStruct(out_shape, sources[0].dtype),
        compiler_params=_params("parallel", "arbitrary"),
    )(*flat, *[a for a in sources for _ in (0, 1)])


def _shard_rows(g, lo, hi):
    rs = g.shape[1]
    pieces = []
    for j in range(g.shape[0]):
        a, b = max(lo, j * rs), min(hi, (j + 1) * rs)
        if a < b:
            pieces.append(g[j, a - j * rs:b - j * rs])
    return pieces


def kernel(x, norm_mix_g, w_in, b_f, gmlp_ln_g, gmlp_ln_b, w_s, b_s, attn_out_g, gmlp_out_g, w_out, norm_ffn_g, w_ff1, w_ff2, norm_final_g, loss_target, m_norm_mix_g, m_w_in, m_b_f, m_gmlp_ln_g, m_gmlp_ln_b, m_w_s, m_b_s, m_attn_out_g, m_gmlp_out_g, m_w_out, m_norm_ffn_g, m_w_ff1, m_w_ff2, m_norm_final_g, v_norm_mix_g, v_w_in, v_b_f, v_gmlp_ln_g, v_gmlp_ln_b, v_w_s, v_b_s, v_attn_out_g, v_gmlp_out_g, v_w_out, v_norm_ffn_g, v_w_ff1, v_w_ff2, v_norm_final_g):
    seq, d_model = x.shape[1], x.shape[2]
    d_attn = d_model // 2
    n_heads = d_attn // HEAD_DIM
    qkv = 3 * d_attn
    shard_cols = w_in.shape[2]
    assert N_CHIPS * shard_cols == qkv + n_heads + 2 * d_attn
    xs = x.reshape(seq, d_model)
    target = loss_target.reshape(seq, d_model)

    place = jnp.stack([2 * lax.axis_index("x") + lax.axis_index("y"), lax.axis_index("c")]).astype(jnp.int32)
    names = ["w_in", "w_out", "w_ff1", "w_ff2"]
    wt_in, mt_in, vt_in = w_in[0].T, m_w_in[0].T, v_w_in[0].T
    b_in, _ = _cast_into_slot(wt_in, place, name="cast_w_in")
    b_out, _ = _cast_into_slot(w_out[0], place, name="cast_w_out")
    b_ff1, (b_in,) = _cast_into_slot(w_ff1[0], place, name="cast_w_ff1", phases=[_gather([b_in], "ici", (0, 2))])
    b_ff2, (b_in,) = _cast_into_slot(w_ff2[0], place, name="cast_w_ff2",
                                     phases=[_merge(_gather([b_in], "ici", (2, 3)), _gather([b_in], "d2d", (0, 2)))])
    h, (b_in,) = _rmsnorm_fwd(xs, norm_mix_g, name="norm_mix",
                              phases=[_merge(_gather([b_in], "ici", (3, 4)), _gather([b_in], "d2d", (2, 3)))])
    (g_in,) = _exchange([_gather([b_in], "d2d", (3, 4))], name="allgather_w_in_tail")
    n_cols = N_CHIPS * shard_cols
    gate_slot, gate_row = divmod(qkv, shard_cols)
    assert gate_row + n_heads <= shard_cols
    pieces = []
    for j in range(N_CHIPS):
        if j == gate_slot:
            pieces += [(j, 0, gate_row), (j, gate_row + n_heads, shard_cols - gate_row - n_heads)]
        else:
            pieces.append((j, 0, shard_cols))
    fwd_segments, at = [[]], 0
    for order, (j, src_row, size) in enumerate(pieces):
        fwd_segments[0].append((at, at + size, order % 3, j, src_row))
        at += size
    wt_main = _realign_rows([g_in] * 3, fwd_segments, (1, n_cols - n_heads, d_model), name="w_in_rows")[0]
    wt_f = jnp.pad(jnp.concatenate(_shard_rows(g_in, qkv, qkv + n_heads), axis=0), ((0, LANES - n_heads), (0, 0)))
    b_f_pad = jnp.pad(b_f, ((0, 0), (0, LANES - n_heads)))
    b_col = b_s[0].T

    first, rest = (0, 1), (1, GATHER_PARTS)
    z, (b_out, b_ff1) = _matmul(h, wt_main, name="in_proj", out_dtype=BF16, trans_b=True,
                                phases=[_gather([b_out], "ici"), _gather([b_ff1], "ici", first)])
    zb, f_cum = _forget_fwd(h, wt_f, b_f_pad, name="forget_fwd")
    f_row = f_cum[:, :n_heads].T[:, None, :]
    (o, lse2), (b_ff1, b_out) = _attn_fwd(z, f_row, n_heads, name="attn_fwd",
                                          phases=[_gather([b_ff1], "ici", rest), _gather([b_out], "d2d")])
    merged = _mix_fwd(z, o, gmlp_ln_g, gmlp_ln_b, w_s[0], b_col, attn_out_g, gmlp_out_g, n_heads, name="mix_fwd")
    w_out_full = b_out.reshape(2 * d_attn, d_model)
    x1, (b_ff1, b_ff2) = _matmul(merged, w_out_full, name="out_proj", out_dtype=F32, residual=xs,
                                 phases=[_gather([b_ff1], "d2d"), _gather([b_ff2], "ici", first)])
    h2, _ = _rmsnorm_fwd(x1, norm_ffn_g, name="norm_ffn")
    a, (b_ff2,) = _matmul(h2, b_ff1, name="ff1", out_dtype=BF16, relu=True, b_sharded=True,
                          phases=[_merge(_gather([b_ff2], "both", rest), _gather([b_ff2], "d2d", first))])
    w_ff2_full = b_ff2.reshape(N_CHIPS * b_ff2.shape[1], d_model)
    x2, _ = _matmul(a, w_ff2_full, name="ff2", out_dtype=F32, square_lhs=True, residual=x1)
    dx2, dx2_b, dg_final, loss = _loss_and_final_bwd(x2, target, norm_final_g.reshape(1, d_model), name="loss_head")

    def pair_sum(g, r, nm):
        return _add_halves(g, r, place, name="grads_pair_sum_" + nm)

    def chip_sum(p, q, nm, **piece):
        return _sum_chips(p, q, place, name="grads_chip_sum_" + nm, **piece)

    dw_ff2, _ = _matmul(a, dx2_b, name="ff2_dw", out_dtype=BF16, trans_a=True, square_lhs=True)
    dw_ff2 = dw_ff2.reshape(N_CHIPS, -1, d_model)
    da, (r_ff2,) = _matmul(dx2_b, w_ff2_full, name="ff2_dlhs", out_dtype=BF16, trans_b=True, scale2_by=a,
                           phases=[_swap_halves([dw_ff2])])
    ps_ff2 = pair_sum(dw_ff2, r_ff2, "w_ff2")
    dh2, (q_ff2a,) = _matmul(da, b_ff1, name="ff1_dlhs", out_dtype=F32, trans_b=True, b_sharded=True,
                             phases=[_send_partials([ps_ff2], (0, 2))])
    dw_ff1, (q_ff2b,) = _matmul(h2, da, name="ff1_dw", out_dtype=BF16, trans_a=True, out_sharded=True, tk=seq,
                                phases=[_send_partials([ps_ff2], (1, 2))])
    g_ff2 = chip_sum(ps_ff2, q_ff2a, "w_ff2_a", piece=(0, 2))
    g_ff2 = chip_sum(ps_ff2, q_ff2b, "w_ff2_b", piece=(1, 2), into=g_ff2)
    (dx1, dg_ffn, dx1_b), (g_ff2,) = _rmsnorm_bwd(dh2, x1, dx2, norm_ffn_g, name="norm_ffn_bwd", rounded_copy=True,
                                                   phases=[_join_halves([g_ff2])])
    dw_out, _ = _matmul(merged, dx1_b, name="out_proj_dw", out_dtype=BF16, trans_a=True, tk=seq)
    dw_out = dw_out.reshape(N_CHIPS, -1, d_model)
    d_merged, (r_ff1, r_out) = _matmul(dx1_b, w_out_full, name="out_proj_dlhs", out_dtype=F32, trans_b=True,
                                       phases=[_swap_halves([dw_ff1, dw_out])])
    ps_ff1, ps_out = pair_sum(dw_ff1, r_ff1, "w_ff1"), pair_sum(dw_out, r_out, "w_out")
    d_o, dzu, dzv, dw_s, db_col, dlg, dlb, dag, dgg = _mix_bwd(
        z, o, d_merged, gmlp_ln_g, gmlp_ln_b, w_s[0], b_col, attn_out_g, gmlp_out_g, n_heads, name="mix_bwd")
    (dq, dk, dv, d_f_key, d_f_query), (q_ff1, q_out) = _attn_bwd(
        z, o, d_o, lse2, f_row, n_heads, name="attn_bwd", phases=[_send_partials([ps_ff1, ps_out])])
    g_ff1, g_out = chip_sum(ps_ff1, q_ff1, "w_ff1"), chip_sum(ps_out, q_out, "w_out")
    d_f = d_f_key.reshape(n_heads, seq) + d_f_query.reshape(n_heads, seq)
    d_f_pad = jnp.pad(d_f.T, ((0, 0), (0, LANES - n_heads)))
    dzf, db_f = _forget_bwd(d_f_pad, zb, name="forget_bwd")
    dz = jnp.concatenate([dq, dk, dv, dzu, dzv], axis=1)
    early_g = _pack([db_f[:, :n_heads], dlg, dlb, dw_s, db_col.T, dag, dgg, dg_ffn, dg_final])
    dwt_main, (g_ff1, g_out, early_all) = _matmul(dz, h, name="in_proj_dw", out_dtype=BF16, trans_a=True, tk=seq,
                                                  phases=[_join_halves([g_ff1, g_out]), _gather_small(early_g)])
    dwt_f, _ = _matmul(dzf, h, name="gate_dw", out_dtype=BF16, trans_a=True)
    bwd_segments = []
    for j in range(N_CHIPS):
        first = j * shard_cols
        if j < gate_slot:
            bwd_segments.append([(0, shard_cols, 0, 0, first)])
        elif j > gate_slot:
            bwd_segments.append([(0, shard_cols, 0, 0, first - n_heads)])
        else:
            bwd_segments.append([(0, gate_row, 0, 0, first), (gate_row, gate_row + n_heads, 1, 0, 0),
                                 (gate_row + n_heads, shard_cols, 2, 0, qkv)])
    dw_in = _realign_rows([dwt_main[None], dwt_f[None], dwt_main[None]], bwd_segments,
                          (N_CHIPS, shard_cols, d_model), name="dw_in_rows")
    dh_gate, (r_in,) = _matmul(dzf, wt_f, name="gate_dlhs", out_dtype=F32, phases=[_swap_halves([dw_in])])
    ps_in = pair_sum(dw_in, r_in, "w_in")
    dh, (q_in,) = _matmul(dz, wt_main, name="in_proj_dlhs", out_dtype=F32, residual=dh_gate, tk=2560,
                          phases=[_send_partials([ps_in])])
    g_in_sum = chip_sum(ps_in, q_in, "w_in")
    (grad_x, dg_mix), _ = _rmsnorm_bwd(dh, xs, dx1, norm_mix_g, name="norm_mix_bwd")
    late_g = _pack([dg_mix])
    g_in_sum, late_all = _exchange([_join_halves([g_in_sum]), _gather_small(late_g)], name="grads_join_w_in")

    big = {}
    for nm, g, w, m, v in zip(names, (g_in_sum, g_out, g_ff1, g_ff2), (wt_in, w_out[0], w_ff1[0], w_ff2[0]),
                              (mt_in, m_w_out[0], m_w_ff1[0], m_w_ff2[0]), (vt_in, v_w_out[0], v_w_ff1[0], v_w_ff2[0])):
        big[nm] = tuple((t.T if nm == "w_in" else t)[None] for t in _adamw(w, g, m, v, name="adamw_" + nm))

    small_params = dict(
        norm_mix_g=(norm_mix_g, m_norm_mix_g, v_norm_mix_g), b_f=(b_f, m_b_f, v_b_f),
        gmlp_ln_g=(gmlp_ln_g, m_gmlp_ln_g, v_gmlp_ln_g), gmlp_ln_b=(gmlp_ln_b, m_gmlp_ln_b, v_gmlp_ln_b),
        w_s=(w_s, m_w_s, v_w_s), b_s=(b_s, m_b_s, v_b_s), attn_out_g=(attn_out_g, m_attn_out_g, v_attn_out_g),
        gmlp_out_g=(gmlp_out_g, m_gmlp_out_g, v_gmlp_out_g), norm_ffn_g=(norm_ffn_g, m_norm_ffn_g, v_norm_ffn_g),
        norm_final_g=(norm_final_g, m_norm_final_g, v_norm_final_g))

    def small_step(group, grads_all, grads_own, label):
        w, m, v = ([small_params[nm][k] for nm in group] for k in range(3))
        packed = _adamw_small(grads_all, grads_own, place, _pack(w), _pack(m), _pack(v), name="adamw_small_" + label)
        parts = [_unpack(p, [a.shape for a in w]) for p in packed]
        return {nm: tuple(part[i] for part in parts) for i, nm in enumerate(group)}

    early = ["b_f", "gmlp_ln_g", "gmlp_ln_b", "w_s", "b_s", "attn_out_g", "gmlp_out_g", "norm_ffn_g", "norm_final_g"]
    small = {**small_step(early, early_all, early_g, "early"), **small_step(["norm_mix_g"], late_all, late_g, "late")}

    order = ["norm_mix_g", "w_in", "b_f", "gmlp_ln_g", "gmlp_ln_b", "w_s", "b_s", "attn_out_g", "gmlp_out_g", "w_out",
             "norm_ffn_g", "w_ff1", "w_ff2", "norm_final_g"]
    result = {**small, **big}
    total_loss = lax.psum(loss[0, 0], ("x", "y", "c"))
    outs = [total_loss, grad_x.reshape(x.shape)]
    for part in range(4):
        outs += [result[nm][part] for nm in order]
    return tuple(outs)
```

```python
import functools
import math

import numpy as np
import jax
import jax.numpy as jnp
from jax import lax
from jax.experimental import pallas as pl
from jax.experimental.pallas import tpu as pltpu

HEAD_DIM = 128
CHUNK = 128
EPS = 1e-6
LANES = 128
MXU_COLUMNS = 256
N_CHIPS = 4
N_DEV = 8
VMEM_LIMIT_BYTES = 56 * 1024 * 1024

ADAM_LR = 0.001
ADAM_B1 = 0.9
ADAM_B2 = 0.999
ADAM_EPS = 1e-08
ADAM_WD = 0.01
ADAM_STEP = 10

BF16 = jnp.bfloat16
F32 = jnp.float32
MESH = pl.DeviceIdType.MESH
ANY = pl.BlockSpec(memory_space=pl.ANY)
NEG_BIG = -1e30


def _params(*sem):
    return pltpu.CompilerParams(dimension_semantics=tuple(sem), vmem_limit_bytes=VMEM_LIMIT_BYTES)


def _tile(n, pref, unit):
    t = (min(pref, n) // unit) * unit
    while t >= unit:
        if n % t == 0:
            return t
        t -= unit
    return n


def _rc_tile(rows, cols, pref_rows=256, pref_cols=256):
    if rows % 16 == 0:
        return _tile(rows, pref_rows, 16), cols
    return rows, _tile(cols, pref_cols, LANES)


class _Phase:
    def __init__(self, arrays, out_shapes, in_place, n_sems, start, finish):
        self.arrays, self.out_shapes, self.in_place = list(arrays), list(out_shapes), in_place
        self.n_sems, self.start, self.finish = n_sems, start, finish

    @property
    def n_out(self):
        return len(self.arrays) if self.in_place else len(self.out_shapes)


def _run_phases(phases, steps, comm_in, comm_out, send_sems, recv_sems):
    at_in = at_out = at_sem = 0
    for ph in phases:
        for step in steps:
            getattr(ph, step)(comm_in[at_in:at_in + len(ph.arrays)], comm_out[at_out:at_out + ph.n_out],
                              lambda i, base=at_sem: send_sems.at[base + i], lambda i, base=at_sem: recv_sems.at[base + i])
        at_in, at_out, at_sem = at_in + len(ph.arrays), at_out + ph.n_out, at_sem + ph.n_sems


def _call(body, *, name, grid, in_specs, out_specs, out_shape, operands, semantics, scratch_shapes=(),
          n_prefetch=0, phases=()):
    in_specs, out_specs, out_shape = list(in_specs), list(out_specs), list(out_shape)
    scratch_shapes = list(scratch_shapes)
    n_in, n_out, n_scr = len(operands) - n_prefetch, len(out_shape), len(scratch_shapes)
    comm_in = [a for ph in phases for a in ph.arrays]
    comm_out = [jax.ShapeDtypeStruct(s.shape, s.dtype) for ph in phases
                for s in (ph.arrays if ph.in_place else ph.out_shapes)]
    aliases, at_in, at_out = {}, n_prefetch + n_in, n_out
    for ph in phases:
        if ph.in_place:
            aliases.update({at_in + r: at_out + r for r in range(len(ph.arrays))})
        at_in, at_out = at_in + len(ph.arrays), at_out + ph.n_out
    n_sems = sum(ph.n_sems for ph in phases)

    def hosted(*refs):
        pre, rest = refs[:n_prefetch], refs[n_prefetch:]
        ins, rest = rest[:n_in], rest[n_in:]
        cin, rest = rest[:len(comm_in)], rest[len(comm_in):]
        outs, rest = rest[:n_out], rest[n_out:]
        cout, rest = rest[:len(comm_out)], rest[len(comm_out):]
        scr = rest[:n_scr]
        if phases:
            send_sems, recv_sems = rest[n_scr:]
            ids = [pl.program_id(ax) for ax in range(len(grid))]
            first = functools.reduce(jnp.logical_and, [i == 0 for i in ids])
            last = functools.reduce(jnp.logical_and, [i == g - 1 for i, g in zip(ids, grid)])

            @pl.when(first)
            def _():
                _run_phases(phases, ("start",), cin, cout, send_sems, recv_sems)

        body(*pre, *ins, *outs, *scr)
        if phases:
            @pl.when(last)
            def _():
                _run_phases(phases, ("finish",), cin, cout, send_sems, recv_sems)

    all_in = in_specs + [ANY] * len(comm_in)
    all_out = out_specs + [ANY] * len(comm_out)
    all_scr = scratch_shapes + ([pltpu.SemaphoreType.DMA((n_sems,)), pltpu.SemaphoreType.DMA((n_sems,))] if phases else [])
    if phases:
        semantics = ("arbitrary",) * len(grid)
    kwargs = dict(name=name, out_shape=tuple(out_shape + comm_out), compiler_params=_params(*semantics),
                  input_output_aliases=aliases)
    if n_prefetch:
        kwargs["grid_spec"] = pltpu.PrefetchScalarGridSpec(
            num_scalar_prefetch=n_prefetch, grid=grid, in_specs=all_in, out_specs=tuple(all_out), scratch_shapes=all_scr)
    else:
        kwargs.update(grid=grid, in_specs=all_in, out_specs=tuple(all_out), scratch_shapes=all_scr)
    res = pl.pallas_call(hosted, **kwargs)(*operands, *comm_in)
    return tuple(res[:n_out]), tuple(res[n_out:])


def _only(results):
    outs, comm = results
    return outs[0] if len(outs) == 1 else outs, comm


def _matmul(a, b, *, name, out_dtype, trans_a=False, trans_b=False, tm=1024, tn=1024, tk=2048,
            square_lhs=False, relu=False, residual=None, scale2_by=None,
            b_sharded=False, out_sharded=False, phases=()):
    m, k = (a.shape[1], a.shape[0]) if trans_a else a.shape
    if b_sharded:
        if trans_b:
            n, ks = b.shape[1], b.shape[2]
            assert N_CHIPS * ks == k
        else:
            ns = b.shape[2]
            n = N_CHIPS * ns
            assert b.shape[1] == k
    else:
        n = b.shape[0] if trans_b else b.shape[1]
        assert (b.shape[1] if trans_b else b.shape[0]) == k
    tm = _tile(m, tm, 128)
    tn = _tile(n // N_CHIPS if (out_sharded or (b_sharded and not trans_b)) else n, tn, 128)
    tk = _tile(k // N_CHIPS if (b_sharded and trans_b) else k, tk, 128)
    nk = k // tk

    if trans_a:
        a_spec = pl.BlockSpec((tk, tm), lambda i, j, kk: (kk, i))
    else:
        a_spec = pl.BlockSpec((tm, tk), lambda i, j, kk: (i, kk))
    if b_sharded and trans_b:
        per = ks // tk
        assert per * tk == ks
        b_spec = pl.BlockSpec((None, tn, tk), lambda i, j, kk: (kk // per, j, kk % per))
    elif b_sharded:
        per = ns // tn
        assert per * tn == ns
        b_spec = pl.BlockSpec((None, tk, tn), lambda i, j, kk: (j // per, kk, j % per))
    elif trans_b:
        b_spec = pl.BlockSpec((tn, tk), lambda i, j, kk: (j, kk))
    else:
        b_spec = pl.BlockSpec((tk, tn), lambda i, j, kk: (kk, j))
    if out_sharded:
        ns_out = n // N_CHIPS
        per_o = ns_out // tn
        assert per_o * tn == ns_out
        out_shape = jax.ShapeDtypeStruct((N_CHIPS, m, ns_out), out_dtype)
        o_spec = pl.BlockSpec((None, tm, tn), lambda i, j, kk: (j // per_o, i, j % per_o))
    else:
        out_shape = jax.ShapeDtypeStruct((m, n), out_dtype)
        o_spec = pl.BlockSpec((tm, tn), lambda i, j, kk: (i, j))
    mn_spec = pl.BlockSpec((tm, tn), lambda i, j, kk: (i, j))

    operands, in_specs = [a, b], [a_spec, b_spec]
    if scale2_by is not None:
        operands.append(scale2_by)
        in_specs.append(mn_spec)
    if residual is not None:
        operands.append(residual)
        in_specs.append(mn_spec)
    dims = (((0 if trans_a else 1,), (1 if trans_b else 0,)), ((), ()))
    chunk = MXU_COLUMNS if tn % MXU_COLUMNS == 0 else tn

    def body(*refs):
        a_ref, b_ref = refs[0], refs[1]
        pos = 2
        scale_ref = res_ref = None
        if scale2_by is not None:
            scale_ref = refs[pos]
            pos += 1
        if residual is not None:
            res_ref = refs[pos]
            pos += 1
        o_ref = refs[pos]
        kk = pl.program_id(2)

        av = a_ref[...]
        if square_lhs:
            av = av.astype(F32)
            av = av * av
        av = av.astype(BF16)

        for lo in range(0, tn, chunk):
            cols = slice(lo, lo + chunk)
            bv = (b_ref[cols, :] if trans_b else b_ref[:, cols]).astype(BF16)
            part = lax.dot_general(av, bv, dims, preferred_element_type=F32)

            def finish(r, cols=cols):
                if relu:
                    r = jnp.maximum(r, 0.0)
                if scale_ref is not None:
                    r = r * (2.0 * scale_ref[:, cols].astype(F32))
                if res_ref is not None:
                    r = r + res_ref[:, cols].astype(F32)
                o_ref[:, cols] = r.astype(out_dtype)

            if nk == 1:
                finish(part)
            else:
                acc_ref = refs[pos + 1]

                @pl.when(kk == 0)
                def _(part=part, cols=cols):
                    acc_ref[:, cols] = part

                @pl.when(jnp.logical_and(kk > 0, kk < nk - 1))
                def _(part=part, cols=cols):
                    acc_ref[:, cols] += part

                @pl.when(kk == nk - 1)
                def _(part=part, cols=cols, finish=finish):
                    finish(acc_ref[:, cols] + part)

    return _only(_call(
        body, name=name, out_shape=[out_shape], grid=(m // tm, n // tn, nk),
        in_specs=in_specs, out_specs=[o_spec], operands=operands,
        scratch_shapes=[pltpu.VMEM((tm, tn), F32)] if nk > 1 else [],
        semantics=("parallel", "parallel", "arbitrary"), phases=phases))


def _rmsnorm_fwd(x, g, *, name, tr=512, phases=()):
    s, d = x.shape
    tr = _tile(s, tr, 8)

    def body(x_ref, g_ref, o_ref):
        xv = x_ref[...]
        r = lax.rsqrt(jnp.mean(xv * xv, axis=-1, keepdims=True) + EPS)
        o_ref[...] = ((xv * r) * g_ref[...]).astype(BF16)

    return _only(_call(
        body, name=name, out_shape=[jax.ShapeDtypeStruct((s, d), BF16)], grid=(s // tr,),
        in_specs=[pl.BlockSpec((tr, d), lambda i: (i, 0)), pl.BlockSpec((1, d), lambda i: (0, 0))],
        out_specs=[pl.BlockSpec((tr, d), lambda i: (i, 0))], operands=[x, g],
        semantics=("parallel",), phases=phases))


def _rms_bwd_rows(dy, xv, g):
    d = xv.shape[-1]
    r = lax.rsqrt(jnp.mean(xv * xv, axis=-1, keepdims=True) + EPS)
    gdy = dy * g
    dot = jnp.sum(gdy * xv, axis=-1, keepdims=True)
    dx = gdy * r - xv * (r * r * r) * (dot / d)
    return dx, dy * (xv * r)


def _rmsnorm_bwd(dy, x, res, g, *, name, tr=256, rounded_copy=False, phases=()):
    s, d = x.shape
    tr = _tile(s, tr, 8)

    def body(dy_ref, x_ref, res_ref, g_ref, dx_ref, dg_ref, *dxb_ref):
        @pl.when(pl.program_id(0) == 0)
        def _():
            dg_ref[...] = jnp.zeros_like(dg_ref)

        dx, dg_rows = _rms_bwd_rows(dy_ref[...].astype(F32), x_ref[...], g_ref[...])
        out = res_ref[...] + dx
        dx_ref[...] = out
        if rounded_copy:
            dxb_ref[0][...] = out.astype(BF16)
        dg_ref[...] += jnp.sum(dg_rows, axis=0, keepdims=True)

    row = pl.BlockSpec((tr, d), lambda i: (i, 0))
    vec = pl.BlockSpec((1, d), lambda i: (0, 0))
    extra = [jax.ShapeDtypeStruct((s, d), BF16)] if rounded_copy else []
    return _call(
        body, name=name,
        out_shape=[jax.ShapeDtypeStruct((s, d), F32), jax.ShapeDtypeStruct((1, d), F32)] + extra,
        grid=(s // tr,), in_specs=[row, row, row, vec], out_specs=[row, vec] + [row] * len(extra),
        operands=[dy, x, res, g], semantics=("arbitrary",), phases=phases)


def _loss_and_final_bwd(x2, target, g, *, name, tr=256):
    s, d = x2.shape
    tr = _tile(s, tr, 8)

    def body(x_ref, t_ref, g_ref, dx_ref, dxb_ref, dg_ref, loss_ref):
        @pl.when(pl.program_id(0) == 0)
        def _():
            dg_ref[...] = jnp.zeros_like(dg_ref)
            loss_ref[...] = jnp.zeros_like(loss_ref)

        xv, gv = x_ref[...], g_ref[...]
        r = lax.rsqrt(jnp.mean(xv * xv, axis=-1, keepdims=True) + EPS)
        err = (xv * r) * gv - t_ref[...]
        row_loss = jnp.mean(err * err, axis=-1, keepdims=True)
        loss_ref[...] += 0.5 * jnp.sum(row_loss, axis=0, keepdims=True)
        dx, dg_rows = _rms_bwd_rows(err / d, xv, gv)
        dx_ref[...] = dx
        dxb_ref[...] = dx.astype(BF16)
        dg_ref[...] += jnp.sum(dg_rows, axis=0, keepdims=True)

    row = pl.BlockSpec((tr, d), lambda i: (i, 0))
    vec = pl.BlockSpec((1, d), lambda i: (0, 0))
    one = pl.BlockSpec((1, 1), lambda i: (0, 0))
    return pl.pallas_call(
        body, name=name,
        out_shape=(jax.ShapeDtypeStruct((s, d), F32), jax.ShapeDtypeStruct((s, d), BF16),
                   jax.ShapeDtypeStruct((1, d), F32), jax.ShapeDtypeStruct((1, 1), F32)),
        grid=(s // tr,), in_specs=[row, row, vec], out_specs=(row, row, vec, one),
        compiler_params=_params("arbitrary"),
    )(x2, target, g)


def _tri_ones(n, lower):
    r = lax.broadcasted_iota(jnp.int32, (n, n), 0)
    c = lax.broadcasted_iota(jnp.int32, (n, n), 1)
    return jnp.where((c <= r) if lower else (c >= r), 1.0, 0.0).astype(F32)


def _forget_fwd(h, w_f, b_f, *, name, tr=256):
    s, d = h.shape
    tr = _tile(s, tr, 8)

    def body(h_ref, w_ref, b_ref, zb_ref, f_ref, carry):
        @pl.when(pl.program_id(0) == 0)
        def _():
            carry[...] = jnp.zeros_like(carry)

        zb = lax.dot_general(h_ref[...], w_ref[...], (((1,), (1,)), ((), ())), preferred_element_type=F32) + b_ref[...]
        zb_ref[...] = zb
        log_f = jnp.minimum(zb, 0.0) - jnp.log(1.0 + jnp.exp(-jnp.abs(zb)))
        run = jnp.dot(_tri_ones(tr, True), log_f, preferred_element_type=F32,
                      precision=lax.Precision.HIGHEST) + carry[...]
        f_ref[...] = run
        carry[...] = run[tr - 1:tr, :]

    row = pl.BlockSpec((tr, LANES), lambda i: (i, 0))
    return pl.pallas_call(
        body, name=name,
        out_shape=(jax.ShapeDtypeStruct((s, LANES), F32), jax.ShapeDtypeStruct((s, LANES), F32)),
        grid=(s // tr,),
        in_specs=[pl.BlockSpec((tr, d), lambda i: (i, 0)), pl.BlockSpec((LANES, d), lambda i: (0, 0)),
                  pl.BlockSpec((1, LANES), lambda i: (0, 0))],
        out_specs=(row, row), scratch_shapes=[pltpu.VMEM((1, LANES), F32)],
        compiler_params=_params("arbitrary"),
    )(h, w_f, b_f)


def _forget_bwd(d_f, zb, *, name, tr=256):
    s = zb.shape[0]
    tr = _tile(s, tr, 8)
    nb = s // tr

    def body(df_ref, zb_ref, dz_ref, db_ref, carry):
        @pl.when(pl.program_id(0) == 0)
        def _():
            carry[...] = jnp.zeros_like(carry)
            db_ref[...] = jnp.zeros_like(db_ref)

        run = jnp.dot(_tri_ones(tr, False), df_ref[...], preferred_element_type=F32,
                      precision=lax.Precision.HIGHEST) + carry[...]
        carry[...] = run[0:1, :]
        dz = run / (1.0 + jnp.exp(zb_ref[...]))
        dz_ref[...] = dz.astype(BF16)
        db_ref[...] += jnp.sum(dz, axis=0, keepdims=True)

    row = pl.BlockSpec((tr, LANES), lambda i: (nb - 1 - i, 0))
    return pl.pallas_call(
        body, name=name,
        out_shape=(jax.ShapeDtypeStruct((s, LANES), BF16), jax.ShapeDtypeStruct((1, LANES), F32)),
        grid=(nb,), in_specs=[row, row], out_specs=(row, pl.BlockSpec((1, LANES), lambda i: (0, 0))),
        scratch_shapes=[pltpu.VMEM((1, LANES), F32)],
        compiler_params=_params("arbitrary"),
    )(d_f, zb)


def _pairs(nblk, by_kv):
    if by_kv:
        pr = [(i, j) for j in range(nblk) for i in range(j, nblk)]
    else:
        pr = [(i, j) for i in range(nblk) for j in range(i + 1)]
    return (jnp.asarray(np.array([p[0] for p in pr], np.int32)), jnp.asarray(np.array([p[1] for p in pr], np.int32)))


def _causal_mask(t):
    r = lax.broadcasted_iota(jnp.int32, (t, t), 0)
    c = lax.broadcasted_iota(jnp.int32, (t, t), 1)
    return c <= r


LOG2E = math.log2(math.e)
QK_TO_LOG2 = LOG2E / math.sqrt(HEAD_DIM)


def _attn_logits2(q, k, fk_row):
    sc = lax.dot_general(q, k, (((1,), (1,)), ((), ())), preferred_element_type=F32)
    return sc * QK_TO_LOG2 - fk_row * LOG2E


def _attn_fwd(z, f_row, n_heads, *, name, tb=1024, phases=()):
    s = z.shape[0]
    tb = _tile(s, tb, 128)
    nblk = s // tb
    rep = tb // LANES
    qi, kj = _pairs(nblk, by_kv=False)

    def body(qi_ref, kj_ref, q_ref, k_ref, v_ref, fk_ref, o_ref, lse_ref, m_sc, l_sc, acc_sc):
        p = pl.program_id(1)
        i, j = qi_ref[p], kj_ref[p]

        @pl.when(j == 0)
        def _():
            m_sc[...] = jnp.full_like(m_sc, NEG_BIG)
            l_sc[...] = jnp.zeros_like(l_sc)
            acc_sc[...] = jnp.zeros_like(acc_sc)

        def update(masked):
            s2 = _attn_logits2(q_ref[...], k_ref[...], fk_ref[...])
            if masked:
                s2 = jnp.where(_causal_mask(tb), s2, NEG_BIG)
            m_old = m_sc[...]
            m_new = jnp.maximum(m_old, jnp.max(s2, axis=-1, keepdims=True))
            alpha = jnp.exp2(m_old - m_new)
            pv = jnp.exp2(s2 - jnp.tile(m_new, (1, rep)))
            l_sc[...] = alpha * l_sc[...] + jnp.sum(pv, axis=-1, keepdims=True)
            acc_sc[...] = alpha * acc_sc[...] + jnp.dot(pv.astype(BF16), v_ref[...], preferred_element_type=F32)
            m_sc[...] = m_new

        @pl.when(j < i)
        def _():
            update(False)

        @pl.when(j == i)
        def _():
            update(True)
            o_ref[...] = (acc_sc[...] / l_sc[...]).astype(BF16)
            lse_ref[...] = m_sc[...] + jnp.log2(l_sc[...])

    h = n_heads
    return _call(
        body, name=name, n_prefetch=2, grid=(h, int(qi.shape[0])),
        in_specs=[
            pl.BlockSpec((tb, HEAD_DIM), lambda hh, p, qi_r, kj_r: (qi_r[p], hh)),
            pl.BlockSpec((tb, HEAD_DIM), lambda hh, p, qi_r, kj_r: (kj_r[p], h + hh)),
            pl.BlockSpec((tb, HEAD_DIM), lambda hh, p, qi_r, kj_r: (kj_r[p], 2 * h + hh)),
            pl.BlockSpec((None, 1, tb), lambda hh, p, qi_r, kj_r: (hh, 0, kj_r[p])),
        ],
        out_specs=[
            pl.BlockSpec((tb, HEAD_DIM), lambda hh, p, qi_r, kj_r: (qi_r[p], hh)),
            pl.BlockSpec((None, tb, LANES), lambda hh, p, qi_r, kj_r: (hh, qi_r[p], 0)),
        ],
        scratch_shapes=[pltpu.VMEM((tb, LANES), F32), pltpu.VMEM((tb, LANES), F32), pltpu.VMEM((tb, HEAD_DIM), F32)],
        out_shape=[jax.ShapeDtypeStruct((s, h * HEAD_DIM), BF16), jax.ShapeDtypeStruct((h, s, LANES), F32)],
        operands=[qi, kj, z, z, z, f_row], semantics=("parallel", "arbitrary"), phases=phases)


def _attn_bwd(z, o, d_o, lse2, f_row, n_heads, *, name, tb=1024, phases=()):
    s = z.shape[0]
    tb = _tile(s, tb, 128)
    nblk = s // tb
    rep = tb // LANES
    qi, kj = _pairs(nblk, by_kv=True)
    n_pairs = int(qi.shape[0])
    scale = 1.0 / math.sqrt(HEAD_DIM)
    h = n_heads

    def body(qi_ref, kj_ref, q_ref, k_ref, v_ref, o_ref, do_ref, lse_ref, fk_ref,
             dq_ref, dk_ref, dv_ref, df_ref, dfq_ref, dq_sc, dk_sc, dv_sc, df_sc, dfq_sc):
        p = pl.program_id(1)
        i, j = qi_ref[p], kj_ref[p]

        @pl.when(p == 0)
        def _():
            dq_sc[...] = jnp.zeros_like(dq_sc)
            dfq_sc[...] = jnp.zeros_like(dfq_sc)

        @pl.when(i == j)
        def _():
            dk_sc[...] = jnp.zeros_like(dk_sc)
            dv_sc[...] = jnp.zeros_like(dv_sc)
            df_sc[...] = jnp.zeros_like(df_sc)

        def update(masked):
            q, k, v, do = q_ref[...], k_ref[...], v_ref[...], do_ref[...]
            delta = jnp.sum(do.astype(F32) * o_ref[...].astype(F32), axis=-1, keepdims=True)
            pv = jnp.exp2(_attn_logits2(q, k, fk_ref[...]) - jnp.tile(lse_ref[...], (1, rep)))
            if masked:
                pv = jnp.where(_causal_mask(tb), pv, 0.0)
            dp = lax.dot_general(do, v, (((1,), (1,)), ((), ())), preferred_element_type=F32)
            ds = pv * (dp - delta)
            ds_b = ds.astype(BF16)
            dv_sc[...] += lax.dot_general(pv.astype(BF16), do, (((0,), (0,)), ((), ())), preferred_element_type=F32)
            dk_sc[...] += lax.dot_general(ds_b, q, (((0,), (0,)), ((), ())), preferred_element_type=F32)
            rows = pl.ds(pl.multiple_of(i * tb, tb), tb)
            dq_sc[rows, :] += jnp.dot(ds_b, k, preferred_element_type=F32)
            df_sc[...] -= jnp.sum(ds, axis=0, keepdims=True)
            dfq_sc[rows, :] += jnp.broadcast_to(jnp.sum(ds, axis=1, keepdims=True), (tb, LANES))

        @pl.when(i > j)
        def _():
            update(False)

        @pl.when(i == j)
        def _():
            update(True)

        @pl.when(i == nblk - 1)
        def _():
            dk_ref[...] = (dk_sc[...] * scale).astype(BF16)
            dv_ref[...] = dv_sc[...].astype(BF16)
            df_ref[...] = df_sc[...]

        @pl.when(p == n_pairs - 1)
        def _():
            dq_ref[...] = (dq_sc[...] * scale).astype(BF16)
            dfq_ref[...] = jnp.transpose(dfq_sc[...])[0:1, :]

    qblk = lambda off: pl.BlockSpec((tb, HEAD_DIM), lambda hh, p, qi_r, kj_r: (qi_r[p], off + hh))
    kblk = lambda off: pl.BlockSpec((tb, HEAD_DIM), lambda hh, p, qi_r, kj_r: (kj_r[p], off + hh))
    qrep = pl.BlockSpec((None, tb, LANES), lambda hh, p, qi_r, kj_r: (hh, qi_r[p], 0))
    krow = pl.BlockSpec((None, 1, tb), lambda hh, p, qi_r, kj_r: (hh, 0, kj_r[p]))
    act = jax.ShapeDtypeStruct((s, h * HEAD_DIM), BF16)
    return _call(
        body, name=name, n_prefetch=2, grid=(h, n_pairs),
        in_specs=[qblk(0), kblk(h), kblk(2 * h), qblk(0), qblk(0), qrep, krow],
        out_specs=[
            pl.BlockSpec((s, HEAD_DIM), lambda hh, p, qi_r, kj_r: (0, hh)),
            kblk(0), kblk(0), krow,
            pl.BlockSpec((None, 1, s), lambda hh, p, qi_r, kj_r: (hh, 0, 0)),
        ],
        scratch_shapes=[pltpu.VMEM((s, HEAD_DIM), F32), pltpu.VMEM((tb, HEAD_DIM), F32),
                        pltpu.VMEM((tb, HEAD_DIM), F32), pltpu.VMEM((1, tb), F32), pltpu.VMEM((s, LANES), F32)],
        out_shape=[act, act, act, jax.ShapeDtypeStruct((h, 1, s), F32), jax.ShapeDtypeStruct((h, 1, s), F32)],
        operands=[qi, kj, z, z, z, o, d_o, lse2, f_row], semantics=("parallel", "arbitrary"), phases=phases)


GELU_C = math.sqrt(2.0 / math.pi)
GELU_A = 0.044715


def _gelu(x):
    return 0.5 * x * (1.0 + jnp.tanh(GELU_C * (x + GELU_A * (x * x * x))))


def _gelu_and_grad(x):
    t = jnp.tanh(GELU_C * (x + GELU_A * (x * x * x)))
    y = 0.5 * x * (1.0 + t)
    dy = 0.5 * (1.0 + t) + 0.5 * x * (1.0 - t * t) * (GELU_C * (1.0 + 3.0 * GELU_A * (x * x)))
    return y, dy


def _layernorm_parts(g):
    mu = jnp.mean(g, axis=-1, keepdims=True)
    xc = g - mu
    rs = lax.rsqrt(jnp.mean(xc * xc, axis=-1, keepdims=True) + EPS)
    return xc * rs, rs


def _spatial_mix(w_ref, bcol_ref, vv_b, n_heads, n_chunks):
    tril = _causal_mask(CHUNK)
    cols = []
    for hh in range(n_heads):
        wc = jnp.where(tril, w_ref[hh], 0.0).astype(BF16)
        lanes = slice(hh * HEAD_DIM, (hh + 1) * HEAD_DIM)
        rows = [jnp.dot(wc, vv_b[c * CHUNK:(c + 1) * CHUNK, lanes], preferred_element_type=F32)
                + bcol_ref[:, hh:hh + 1] for c in range(n_chunks)]
        cols.append(jnp.concatenate(rows, axis=0))
    return jnp.concatenate(cols, axis=1)


def _mix_fwd(z, o, ln_g, ln_b, w_s, b_col, attn_g, gm_g, n_heads, *, name, tr=256):
    s = z.shape[0]
    dg = n_heads * HEAD_DIM
    tr = _tile(s, tr, CHUNK)
    n_chunks = tr // CHUNK

    def body(zu_ref, zv_ref, o_ref, lg_ref, lb_ref, w_ref, bcol_ref, ag_ref, gg_ref, out_ref):
        u = _gelu(zu_ref[...].astype(F32))
        xhat, _ = _layernorm_parts(_gelu(zv_ref[...].astype(F32)))
        vv = xhat * lg_ref[...] + lb_ref[...]
        gm = u * _spatial_mix(w_ref, bcol_ref, vv.astype(BF16), n_heads, n_chunks)
        rg = lax.rsqrt(jnp.mean(gm * gm, axis=-1, keepdims=True) + EPS)
        ov = o_ref[...].astype(F32)
        ra = lax.rsqrt(jnp.mean(ov * ov, axis=-1, keepdims=True) + EPS)
        out_ref[:, :dg] = ((ov * ra) * ag_ref[...]).astype(BF16)
        out_ref[:, dg:] = ((gm * rg) * gg_ref[...]).astype(BF16)

    vec = pl.BlockSpec((1, dg), lambda i: (0, 0))
    return pl.pallas_call(
        body, name=name, out_shape=jax.ShapeDtypeStruct((s, 2 * dg), BF16), grid=(s // tr,),
        in_specs=[pl.BlockSpec((tr, dg), lambda i: (i, 3)), pl.BlockSpec((tr, dg), lambda i: (i, 4)),
                  pl.BlockSpec((tr, dg), lambda i: (i, 0)), vec, vec,
                  pl.BlockSpec((n_heads, CHUNK, CHUNK), lambda i: (0, 0, 0)),
                  pl.BlockSpec((CHUNK, n_heads), lambda i: (0, 0)), vec, vec],
        out_specs=pl.BlockSpec((tr, 2 * dg), lambda i: (i, 0)),
        compiler_params=_params("parallel"),
    )(z, z, o, ln_g, ln_b, w_s, b_col, attn_g, gm_g)


def _mix_bwd(z, o, d_merged, ln_g, ln_b, w_s, b_col, attn_g, gm_g, n_heads, *, name, tr=256):
    s = z.shape[0]
    dg = n_heads * HEAD_DIM
    tr = _tile(s, tr, CHUNK)
    n_chunks = tr // CHUNK

    def body(zu_ref, zv_ref, o_ref, dm_ref, lg_ref, lb_ref, w_ref, bcol_ref, ag_ref, gg_ref,
             do_ref, dzu_ref, dzv_ref, dw_ref, dbcol_ref, dlg_ref, dlb_ref, dag_ref, dgg_ref):
        @pl.when(pl.program_id(0) == 0)
        def _():
            for ref in (dw_ref, dbcol_ref, dlg_ref, dlb_ref, dag_ref, dgg_ref):
                ref[...] = jnp.zeros_like(ref)

        d_o, dag_rows = _rms_bwd_rows(dm_ref[:, :dg], o_ref[...].astype(F32), ag_ref[...])
        do_ref[...] = d_o.astype(BF16)
        dag_ref[...] += jnp.sum(dag_rows, axis=0, keepdims=True)

        u, du_dz = _gelu_and_grad(zu_ref[...].astype(F32))
        gv, dgv_dz = _gelu_and_grad(zv_ref[...].astype(F32))
        xhat, rs = _layernorm_parts(gv)
        lg = lg_ref[...]
        vv_b = (xhat * lg + lb_ref[...]).astype(BF16)
        mix = _spatial_mix(w_ref, bcol_ref, vv_b, n_heads, n_chunks)
        gm = u * mix
        d_gm, dgg_rows = _rms_bwd_rows(dm_ref[:, dg:], gm, gg_ref[...])
        dgg_ref[...] += jnp.sum(dgg_rows, axis=0, keepdims=True)
        dzu_ref[...] = ((d_gm * mix) * du_dz).astype(BF16)
        d_mix = d_gm * u
        d_mix_b = d_mix.astype(BF16)

        tril = _causal_mask(CHUNK)
        lane = lax.broadcasted_iota(jnp.int32, (CHUNK, n_heads), 1)
        cols = []
        db = jnp.zeros((CHUNK, n_heads), F32)
        for hh in range(n_heads):
            wc = jnp.where(tril, w_ref[hh], 0.0).astype(BF16)
            lanes = slice(hh * HEAD_DIM, (hh + 1) * HEAD_DIM)
            dw = jnp.zeros((CHUNK, CHUNK), F32)
            dmix_sum = jnp.zeros((CHUNK, HEAD_DIM), F32)
            rows = []
            for c in range(n_chunks):
                rws = slice(c * CHUNK, (c + 1) * CHUNK)
                dmb = d_mix_b[rws, lanes]
                dw += lax.dot_general(dmb, vv_b[rws, lanes], (((1,), (1,)), ((), ())), preferred_element_type=F32)
                rows.append(lax.dot_general(wc, dmb, (((0,), (0,)), ((), ())), preferred_element_type=F32))
                dmix_sum += d_mix[rws, lanes]
            dw_ref[hh] += jnp.where(tril, dw, 0.0)
            db += jnp.where(lane == hh, jnp.sum(dmix_sum, axis=-1, keepdims=True), 0.0)
            cols.append(jnp.concatenate(rows, axis=0))
        dbcol_ref[...] += db
        d_vv = jnp.concatenate(cols, axis=1)

        dlg_ref[...] += jnp.sum(d_vv * xhat, axis=0, keepdims=True)
        dlb_ref[...] += jnp.sum(d_vv, axis=0, keepdims=True)
        d_xhat = d_vv * lg
        d_gv = rs * (d_xhat - jnp.mean(d_xhat, axis=-1, keepdims=True)
                     - xhat * jnp.mean(d_xhat * xhat, axis=-1, keepdims=True))
        dzv_ref[...] = (d_gv * dgv_dz).astype(BF16)

    vec = pl.BlockSpec((1, dg), lambda i: (0, 0))
    wspec = pl.BlockSpec((n_heads, CHUNK, CHUNK), lambda i: (0, 0, 0))
    bspec = pl.BlockSpec((CHUNK, n_heads), lambda i: (0, 0))
    rowb = pl.BlockSpec((tr, dg), lambda i: (i, 0))
    act = jax.ShapeDtypeStruct((s, dg), BF16)
    vshape = jax.ShapeDtypeStruct((1, dg), F32)
    return pl.pallas_call(
        body, name=name,
        out_shape=(act, act, act, jax.ShapeDtypeStruct((n_heads, CHUNK, CHUNK), F32),
                   jax.ShapeDtypeStruct((CHUNK, n_heads), F32), vshape, vshape, vshape, vshape),
        grid=(s // tr,),
        in_specs=[pl.BlockSpec((tr, dg), lambda i: (i, 3)), pl.BlockSpec((tr, dg), lambda i: (i, 4)),
                  rowb, pl.BlockSpec((tr, 2 * dg), lambda i: (i, 0)), vec, vec, wspec, bspec, vec, vec],
        out_specs=(rowb, rowb, rowb, wspec, bspec, vec, vec, vec, vec),
        compiler_params=_params("arbitrary"),
    )(z, z, o, d_merged, ln_g, ln_b, w_s, b_col, attn_g, gm_g)


def _place():
    x, y, c = lax.axis_index("x"), lax.axis_index("y"), lax.axis_index("c")
    other_chips = [(1 - x, y), (x, 1 - y), (1 - x, 1 - y)]
    return x, y, c, other_chips


def _remote(src, dst, send_sem, recv_sem, to):
    return pltpu.make_async_remote_copy(src_ref=src, dst_ref=dst, send_sem=send_sem, recv_sem=recv_sem,
                                        device_id=to, device_id_type=MESH)


def _cast_into_slot(w, place, *, name, phases=()):
    rows, cols = w.shape
    tr, tc = _rc_tile(rows, cols)

    def body(place_ref, w_ref, o_ref):
        o_ref[...] = w_ref[...].astype(BF16)

    return _only(_call(
        body, name=name, n_prefetch=1, grid=(rows // tr, cols // tc),
        in_specs=[pl.BlockSpec((tr, tc), lambda i, j, pr: (i, j))],
        out_specs=[pl.BlockSpec((None, tr, tc), lambda i, j, pr: (pr[0], i, j))],
        out_shape=[jax.ShapeDtypeStruct((N_CHIPS, rows, cols), BF16)], operands=[place, w],
        semantics=("parallel", "parallel"), phases=phases))


def _exchange(phases, *, name):
    comm_in = [a for ph in phases for a in ph.arrays]
    comm_out = [jax.ShapeDtypeStruct(s.shape, s.dtype) for ph in phases for s in (ph.arrays if ph.in_place else ph.out_shapes)]
    aliases, at_in, at_out = {}, 0, 0
    for ph in phases:
        if ph.in_place:
            aliases.update({at_in + r: at_out + r for r in range(len(ph.arrays))})
        at_in, at_out = at_in + len(ph.arrays), at_out + ph.n_out
    n_sems = sum(ph.n_sems for ph in phases)

    def body(*refs):
        cin, cout = refs[:len(comm_in)], refs[len(comm_in):len(comm_in) + len(comm_out)]
        send_sems, recv_sems = refs[len(comm_in) + len(comm_out):]
        _run_phases(phases, ("start", "finish"), cin, cout, send_sems, recv_sems)

    return pl.pallas_call(
        body, name=name, out_shape=tuple(comm_out), in_specs=[ANY] * len(comm_in), out_specs=tuple([ANY] * len(comm_out)),
        input_output_aliases=aliases,
        scratch_shapes=[pltpu.SemaphoreType.DMA((n_sems,)), pltpu.SemaphoreType.DMA((n_sems,))],
    )(*comm_in)


GATHER_PARTS = 4


def _gather(bufs, stage, part=(0, GATHER_PARTS)):
    n = 3 * len(bufs)
    lo, hi = part

    def copies(outs, send, recv, d2d, incoming):
        x, y, c, chips = _place()
        for t, buf in enumerate(outs):
            half = buf.shape[2] // 2
            piece = half // GATHER_PARTS
            for k, (cx, cy) in enumerate(chips):
                i = 3 * t + k + (n if (d2d and stage == "both") else 0)
                cols = pl.ds(((1 - c) if (d2d and incoming) else c) * half + lo * piece, (hi - lo) * piece)
                blk = buf.at[(2 * cx + cy) if (d2d or incoming) else (2 * x + y), :, cols]
                yield _remote(blk, blk, send(i), recv(i), (x, y, 1 - c) if d2d else (cx, cy, c))

    def start(ins, outs, send, recv):
        for cp in copies(outs, send, recv, stage == "d2d", False):
            cp.start()

    def finish(ins, outs, send, recv):
        if stage == "both":
            for arrival, onward in zip(copies(outs, send, recv, False, True), copies(outs, send, recv, True, False)):
                arrival.wait_recv()
                onward.start()
        for cp in copies(outs, send, recv, stage != "ici", True):
            cp.wait_recv()
        for d2d in ((False, True) if stage == "both" else (stage == "d2d",)):
            for cp in copies(outs, send, recv, d2d, False):
                cp.wait_send()

    return _Phase(bufs, [], True, (2 if stage == "both" else 1) * n, start, finish)


def _merge(first, second):
    n_first = first.n_sems

    def later(sem):
        return lambda i: sem(n_first + i)

    def start(ins, outs, send, recv):
        first.start(ins, outs, send, recv)
        second.start(ins, outs, later(send), later(recv))

    def finish(ins, outs, send, recv):
        first.finish(ins, outs, send, recv)
        second.finish(ins, outs, later(send), later(recv))

    return _Phase(first.arrays, [], True, n_first + second.n_sems, start, finish)


def _swap_halves(grads):
    def copies(ins, outs, send, recv):
        x, y, c, _ = _place()
        for t, g in enumerate(ins):
            half = g.shape[2] // 2
            yield _remote(g.at[:, :, pl.ds((1 - c) * half, half)], outs[t], send(t), recv(t), (x, y, 1 - c))

    def start(ins, outs, send, recv):
        for cp in copies(ins, outs, send, recv):
            cp.start()

    def finish(ins, outs, send, recv):
        for cp in copies(ins, outs, send, recv):
            cp.wait()

    shapes = [jax.ShapeDtypeStruct((a.shape[0], a.shape[1], a.shape[2] // 2), a.dtype) for a in grads]
    return _Phase(grads, shapes, False, len(grads), start, finish)


def _add_halves(grad, received, place, *, name):
    ns, rows, half = received.shape
    tr, tc = _rc_tile(rows, half, pref_rows=1024)
    per = half // tc

    def body(place_ref, g_ref, r_ref, o_ref):
        o_ref[...] = (g_ref[...].astype(F32) + r_ref[...].astype(F32)).astype(BF16)

    grid_spec = pltpu.PrefetchScalarGridSpec(
        num_scalar_prefetch=1, grid=(ns, rows // tr, per),
        in_specs=[pl.BlockSpec((None, tr, tc), lambda s, i, j, pr: (s, i, pr[1] * per + j)),
                  pl.BlockSpec((None, tr, tc), lambda s, i, j, pr: (s, i, j))],
        out_specs=pl.BlockSpec((None, tr, tc), lambda s, i, j, pr: (s, i, j)),
    )
    return pl.pallas_call(
        body, name=name, grid_spec=grid_spec, out_shape=jax.ShapeDtypeStruct(received.shape, BF16),
        compiler_params=_params("parallel", "parallel", "parallel"),
    )(place, grad, received)


def _send_partials(parts, piece=(0, 1)):
    k_th, n_pieces = piece

    def cols(part):
        width = part.shape[2] // n_pieces
        return pl.ds(k_th * width, width)

    def start(ins, outs, send, recv):
        x, y, c, chips = _place()
        for t, part in enumerate(ins):
            for k, (cx, cy) in enumerate(chips):
                _remote(part.at[2 * cx + cy, :, cols(part)], outs[t].at[2 * x + y],
                        send(3 * t + k), recv(3 * t + k), (cx, cy, c)).start()

    def finish(ins, outs, send, recv):
        x, y, c, chips = _place()
        for t, part in enumerate(ins):
            for k, (cx, cy) in enumerate(chips):
                slot = outs[t].at[2 * cx + cy]
                _remote(slot, slot, send(3 * t + k), recv(3 * t + k), (cx, cy, c)).wait_recv()
        for t, part in enumerate(ins):
            for k, (cx, cy) in enumerate(chips):
                sent = part.at[2 * cx + cy, :, cols(part)]
                _remote(sent, sent, send(3 * t + k), recv(3 * t + k), (cx, cy, c)).wait_send()

    shapes = [jax.ShapeDtypeStruct(a.shape[:2] + (a.shape[2] // n_pieces,), a.dtype) for a in parts]
    return _Phase(parts, shapes, False, 3 * len(parts), start, finish)


def _sum_chips(parts, slots, place, *, name, piece=(0, 1), into=None):
    ns, rows, width = slots.shape
    k_th, n_pieces = piece
    half = width * n_pieces
    tr, tc = _rc_tile(rows, width, pref_rows=512)
    per = width // tc

    def body(place_ref, p_ref, s_ref, *rest):
        acc = p_ref[...].astype(F32)
        for k in range(ns):
            acc = acc + jnp.where(place_ref[0] == k, 0.0, s_ref[k].astype(F32))
        rest[-1][...] = acc

    grid_spec = pltpu.PrefetchScalarGridSpec(
        num_scalar_prefetch=1, grid=(rows // tr, per),
        in_specs=[pl.BlockSpec((None, tr, tc), lambda i, j, pr: (pr[0], i, k_th * per + j)),
                  pl.BlockSpec((ns, tr, tc), lambda i, j, pr: (0, i, j))] + ([ANY] if into is not None else []),
        out_specs=pl.BlockSpec((tr, tc), lambda i, j, pr: (i, (pr[1] * n_pieces + k_th) * per + j)),
    )
    return pl.pallas_call(
        body, name=name, grid_spec=grid_spec, out_shape=jax.ShapeDtypeStruct((rows, 2 * half), F32),
        input_output_aliases={3: 0} if into is not None else {},
        compiler_params=_params("parallel", "parallel"),
    )(place, parts, slots, *([into] if into is not None else []))


def _join_halves(bufs):
    def copies(outs, send, recv, incoming):
        x, y, c, _ = _place()
        for t, buf in enumerate(outs):
            half = buf.shape[1] // 2
            cols = buf.at[:, pl.ds(((1 - c) if incoming else c) * half, half)]
            yield _remote(cols, cols, send(t), recv(t), (x, y, 1 - c))

    def start(ins, outs, send, recv):
        for cp in copies(outs, send, recv, False):
            cp.start()

    def finish(ins, outs, send, recv):
        for cp in copies(outs, send, recv, True):
            cp.wait_recv()
        for cp in copies(outs, send, recv, False):
            cp.wait_send()

    return _Phase(bufs, [], True, len(bufs), start, finish)


def _gather_small(buf):
    def slot(out, px, py, pc):
        return out.at[4 * px + 2 * py + pc]

    def start(ins, outs, send, recv):
        x, y, c, chips = _place()
        mine = slot(outs[0], x, y, c)
        _remote(ins[0], mine, send(0), recv(0), (x, y, 1 - c)).start()
        for k, (cx, cy) in enumerate(chips):
            _remote(ins[0], mine, send(1 + k), recv(1 + k), (cx, cy, c)).start()

    def finish(ins, outs, send, recv):
        x, y, c, chips = _place()
        sibling = (x, y, 1 - c)
        for k, (cx, cy) in enumerate(chips):
            arrived = slot(outs[0], cx, cy, c)
            _remote(arrived, arrived, send(1 + k), recv(1 + k), sibling).wait_recv()
            _remote(arrived, arrived, send(4 + k), recv(4 + k), sibling).start()
        theirs = slot(outs[0], x, y, 1 - c)
        _remote(theirs, theirs, send(0), recv(0), sibling).wait_recv()
        for k, (cx, cy) in enumerate(chips):
            passed = slot(outs[0], cx, cy, 1 - c)
            _remote(passed, passed, send(4 + k), recv(4 + k), sibling).wait_recv()
        for i in range(7):
            _remote(ins[0], ins[0], send(i), recv(i), sibling).wait_send()

    return _Phase([buf], [jax.ShapeDtypeStruct((N_DEV,) + buf.shape, buf.dtype)], False, 7, start, finish)


def _adamw_math(w, g, m, v):
    m = ADAM_B1 * m + (1.0 - ADAM_B1) * g
    v = ADAM_B2 * v + (1.0 - ADAM_B2) * (g * g)
    m_hat = m / (1.0 - ADAM_B1 ** ADAM_STEP)
    v_hat = v / (1.0 - ADAM_B2 ** ADAM_STEP)
    delta = -ADAM_LR * (m_hat / (jnp.sqrt(v_hat) + ADAM_EPS) + ADAM_WD * w)
    return delta, m, v


def _adamw(w, g, m, v, *, name):
    rows, cols = w.shape
    tr, tc = _rc_tile(rows, cols)

    def body(w_ref, g_ref, m_ref, v_ref, go_ref, d_ref, mo_ref, vo_ref):
        g = g_ref[...]
        go_ref[...] = g
        d_ref[...], mo_ref[...], vo_ref[...] = _adamw_math(w_ref[...], g, m_ref[...], v_ref[...])

    blk = pl.BlockSpec((tr, tc), lambda i, j: (i, j))
    shape = jax.ShapeDtypeStruct((rows, cols), F32)
    return pl.pallas_call(
        body, name=name, out_shape=(shape, shape, shape, shape), grid=(rows // tr, cols // tc),
        in_specs=[blk] * 4, out_specs=(blk, blk, blk, blk), compiler_params=_params("parallel", "parallel"),
    )(w, g, m, v)


def _adamw_small(gathered, own, place, w, m, v, *, name):
    nd = gathered.shape[0]

    def body(place_ref, gs_ref, own_ref, w_ref, m_ref, v_ref, g_ref, d_ref, mo_ref, vo_ref):
        me = 2 * place_ref[0] + place_ref[1]
        g = jnp.zeros(own_ref.shape, F32)
        for k in range(nd):
            g = g + jnp.where(me == k, own_ref[...], gs_ref[k])
        g_ref[...] = g
        d_ref[...], mo_ref[...], vo_ref[...] = _adamw_math(w_ref[...], g, m_ref[...], v_ref[...])

    whole = pl.BlockSpec(w.shape, lambda i, pr: (0, 0))
    grid_spec = pltpu.PrefetchScalarGridSpec(
        num_scalar_prefetch=1, grid=(1,),
        in_specs=[pl.BlockSpec(gathered.shape, lambda i, pr: (0, 0, 0)), whole, whole, whole, whole],
        out_specs=(whole, whole, whole, whole))
    shape = jax.ShapeDtypeStruct(w.shape, F32)
    return pl.pallas_call(body, name=name, grid_spec=grid_spec, out_shape=(shape, shape, shape, shape),
                          compiler_params=_params("arbitrary"))(place, gathered, own, w, m, v)


def _pack(parts):
    flat = jnp.concatenate([p.reshape(-1).astype(F32) for p in parts])
    rows = -(-flat.shape[0] // (8 * LANES)) * 8
    return jnp.pad(flat, (0, rows * LANES - flat.shape[0])).reshape(rows, LANES)


def _unpack(buf, shapes):
    flat = buf.reshape(-1)
    out, pos = [], 0
    for shp in shapes:
        size = int(np.prod(shp))
        out.append(flat[pos:pos + size].reshape(shp))
        pos += size
    return out


ROW_BLOCK = 256


def _realign_rows(sources, segments, out_shape, *, name):
    n_slots, rows, cols = out_shape
    n_src = len(sources)
    per_slot = -(-rows // ROW_BLOCK)
    table = np.zeros((6, n_slots * per_slot, n_src), np.int32)
    for so in range(n_slots):
        for first, last, src, src_slot, src_row in segments[so]:
            for blk in range(first // ROW_BLOCK, (last - 1) // ROW_BLOCK + 1):
                lo, hi = max(first, blk * ROW_BLOCK), min(last, (blk + 1) * ROW_BLOCK)
                base = src_row + (blk * ROW_BLOCK - first)
                m0 = (base + lo - blk * ROW_BLOCK) // ROW_BLOCK
                at = so * per_slot + blk
                assert table[4, at, src] == 0, "two segments of one block share a source operand"
                table[:, at, src] = (src_slot, m0, base - m0 * ROW_BLOCK, lo - blk * ROW_BLOCK, hi - blk * ROW_BLOCK,
                                     min(2 * ROW_BLOCK, sources[src].shape[1] - m0 * ROW_BLOCK))
    last_block = [-(-a.shape[1] // ROW_BLOCK) - 1 for a in sources]

    def body(slot_ref, blk_ref, off_ref, lo_ref, hi_ref, valid_ref, *refs):
        o_ref, acc = refs[2 * n_src], refs[2 * n_src + 1]
        at = (pl.program_id(0) * per_slot + pl.program_id(1)) * n_src
        acc[...] = jnp.zeros_like(acc)
        for p in range(n_src):
            @pl.when(hi_ref[at + p] > lo_ref[at + p])
            def _():
                two = jnp.concatenate([refs[2 * p][...], refs[2 * p + 1][...]], axis=0)
                src_row = lax.broadcasted_iota(jnp.int32, two.shape, 0)
                two = jnp.where(src_row < valid_ref[at + p], two, jnp.zeros_like(two))
                r = lax.broadcasted_iota(jnp.int32, (ROW_BLOCK, 2 * ROW_BLOCK), 0)
                c = lax.broadcasted_iota(jnp.int32, (ROW_BLOCK, 2 * ROW_BLOCK), 1)
                place = (c == r + off_ref[at + p]) & (r >= lo_ref[at + p]) & (r < hi_ref[at + p])
                acc[...] += jnp.dot(place.astype(two.dtype), two, preferred_element_type=F32)
        o_ref[...] = acc[...].astype(o_ref.dtype)

    def src_spec(p, second):
        def index(so, i, slot_r, blk_r, off_r, lo_r, hi_r, valid_r):
            at = (so * per_slot + i) * n_src + p
            return slot_r[at], jnp.minimum(blk_r[at] + second, last_block[p]), 0
        return pl.BlockSpec((None, ROW_BLOCK, cols), index)

    grid_spec = pltpu.PrefetchScalarGridSpec(
        num_scalar_prefetch=6, grid=(n_slots, per_slot),
        in_specs=[src_spec(p, second) for p in range(n_src) for second in (0, 1)],
        out_specs=pl.BlockSpec((None, ROW_BLOCK, cols), lambda so, i, *_: (so, i, 0)),
        scratch_shapes=[pltpu.VMEM((ROW_BLOCK, cols), F32)],
    )
    flat = [jnp.asarray(table[k].reshape(-1)) for k in range(6)]
    return pl.pallas_call(
        body, name=name, grid_spec=grid_spec, out_shape=jax.ShapeDtypeStruct(out_shape, sources[0].dtype),
        compiler_params=_params("parallel", "arbitrary"),
    )(*flat, *[a for a in sources for _ in (0, 1)])


def _shard_rows(g, lo, hi):
    rs = g.shape[1]
    pieces = []
    for j in range(g.shape[0]):
        a, b = max(lo, j * rs), min(hi, (j + 1) * rs)
        if a < b:
            pieces.append(g[j, a - j * rs:b - j * rs])
    return pieces


def kernel(x, norm_mix_g, w_in, b_f, gmlp_ln_g, gmlp_ln_b, w_s, b_s, attn_out_g, gmlp_out_g, w_out, norm_ffn_g, w_ff1, w_ff2, norm_final_g, loss_target, m_norm_mix_g, m_w_in, m_b_f, m_gmlp_ln_g, m_gmlp_ln_b, m_w_s, m_b_s, m_attn_out_g, m_gmlp_out_g, m_w_out, m_norm_ffn_g, m_w_ff1, m_w_ff2, m_norm_final_g, v_norm_mix_g, v_w_in, v_b_f, v_gmlp_ln_g, v_gmlp_ln_b, v_w_s, v_b_s, v_attn_out_g, v_gmlp_out_g, v_w_out, v_norm_ffn_g, v_w_ff1, v_w_ff2, v_norm_final_g):
    seq, d_model = x.shape[1], x.shape[2]
    d_attn = d_model // 2
    n_heads = d_attn // HEAD_DIM
    qkv = 3 * d_attn
    shard_cols = w_in.shape[2]
    assert N_CHIPS * shard_cols == qkv + n_heads + 2 * d_attn
    xs = x.reshape(seq, d_model)
    target = loss_target.reshape(seq, d_model)

    place = jnp.stack([2 * lax.axis_index("x") + lax.axis_index("y"), lax.axis_index("c")]).astype(jnp.int32)
    names = ["w_in", "w_out", "w_ff1", "w_ff2"]
    wt_in, mt_in, vt_in = w_in[0].T, m_w_in[0].T, v_w_in[0].T
    b_in, _ = _cast_into_slot(wt_in, place, name="cast_w_in")
    b_out, _ = _cast_into_slot(w_out[0], place, name="cast_w_out")
    b_ff1, (b_in,) = _cast_into_slot(w_ff1[0], place, name="cast_w_ff1", phases=[_gather([b_in], "ici", (0, 2))])
    b_ff2, (b_in,) = _cast_into_slot(w_ff2[0], place, name="cast_w_ff2",
                                     phases=[_merge(_gather([b_in], "ici", (2, 3)), _gather([b_in], "d2d", (0, 2)))])
    h, (b_in,) = _rmsnorm_fwd(xs, norm_mix_g, name="norm_mix",
                              phases=[_merge(_gather([b_in], "ici", (3, 4)), _gather([b_in], "d2d", (2, 3)))])
    (g_in,) = _exchange([_gather([b_in], "d2d", (3, 4))], name="allgather_w_in_tail")
    n_cols = N_CHIPS * shard_cols
    gate_slot, gate_row = divmod(qkv, shard_cols)
    assert gate_row + n_heads <= shard_cols
    pieces = []
    for j in range(N_CHIPS):
        if j == gate_slot:
            pieces += [(j, 0, gate_row), (j, gate_row + n_heads, shard_cols - gate_row - n_heads)]
        else:
            pieces.append((j, 0, shard_cols))
    fwd_segments, at = [[]], 0
    for order, (j, src_row, size) in enumerate(pieces):
        fwd_segments[0].append((at, at + size, order % 3, j, src_row))
        at += size
    wt_main = _realign_rows([g_in] * 3, fwd_segments, (1, n_cols - n_heads, d_model), name="w_in_rows")[0]
    wt_f = jnp.pad(jnp.concatenate(_shard_rows(g_in, qkv, qkv + n_heads), axis=0), ((0, LANES - n_heads), (0, 0)))
    b_f_pad = jnp.pad(b_f, ((0, 0), (0, LANES - n_heads)))
    b_col = b_s[0].T

    first, rest = (0, 1), (1, GATHER_PARTS)
    z, (b_out, b_ff1) = _matmul(h, wt_main, name="in_proj", out_dtype=BF16, trans_b=True,
                                phases=[_gather([b_out], "ici"), _gather([b_ff1], "ici", first)])
    zb, f_cum = _forget_fwd(h, wt_f, b_f_pad, name="forget_fwd")
    f_row = f_cum[:, :n_heads].T[:, None, :]
    (o, lse2), (b_ff1, b_out) = _attn_fwd(z, f_row, n_heads, name="attn_fwd",
                                          phases=[_gather([b_ff1], "ici", rest), _gather([b_out], "d2d")])
    merged = _mix_fwd(z, o, gmlp_ln_g, gmlp_ln_b, w_s[0], b_col, attn_out_g, gmlp_out_g, n_heads, name="mix_fwd")
    w_out_full = b_out.reshape(2 * d_attn, d_model)
    x1, (b_ff1, b_ff2) = _matmul(merged, w_out_full, name="out_proj", out_dtype=F32, residual=xs,
                                 phases=[_gather([b_ff1], "d2d"), _gather([b_ff2], "ici", first)])
    h2, _ = _rmsnorm_fwd(x1, norm_ffn_g, name="norm_ffn")
    a, (b_ff2,) = _matmul(h2, b_ff1, name="ff1", out_dtype=BF16, relu=True, b_sharded=True,
                          phases=[_merge(_gather([b_ff2], "both", rest), _gather([b_ff2], "d2d", first))])
    w_ff2_full = b_ff2.reshape(N_CHIPS * b_ff2.shape[1], d_model)
    x2, _ = _matmul(a, w_ff2_full, name="ff2", out_dtype=F32, square_lhs=True, residual=x1)
    dx2, dx2_b, dg_final, loss = _loss_and_final_bwd(x2, target, norm_final_g.reshape(1, d_model), name="loss_head")

    def pair_sum(g, r, nm):
        return _add_halves(g, r, place, name="grads_pair_sum_" + nm)

    def chip_sum(p, q, nm, **piece):
        return _sum_chips(p, q, place, name="grads_chip_sum_" + nm, **piece)

    dw_ff2, _ = _matmul(a, dx2_b, name="ff2_dw", out_dtype=BF16, trans_a=True, square_lhs=True)
    dw_ff2 = dw_ff2.reshape(N_CHIPS, -1, d_model)
    da, (r_ff2,) = _matmul(dx2_b, w_ff2_full, name="ff2_dlhs", out_dtype=BF16, trans_b=True, scale2_by=a,
                           phases=[_swap_halves([dw_ff2])])
    ps_ff2 = pair_sum(dw_ff2, r_ff2, "w_ff2")
    dh2, (q_ff2a,) = _matmul(da, b_ff1, name="ff1_dlhs", out_dtype=F32, trans_b=True, b_sharded=True,
                             phases=[_send_partials([ps_ff2], (0, 2))])
    dw_ff1, (q_ff2b,) = _matmul(h2, da, name="ff1_dw", out_dtype=BF16, trans_a=True, out_sharded=True, tk=seq,
                                phases=[_send_partials([ps_ff2], (1, 2))])
    g_ff2 = chip_sum(ps_ff2, q_ff2a, "w_ff2_a", piece=(0, 2))
    g_ff2 = chip_sum(ps_ff2, q_ff2b, "w_ff2_b", piece=(1, 2), into=g_ff2)
    (dx1, dg_ffn, dx1_b), (g_ff2,) = _rmsnorm_bwd(dh2, x1, dx2, norm_ffn_g, name="norm_ffn_bwd", rounded_copy=True,
                                                   phases=[_join_halves([g_ff2])])
    dw_out, _ = _matmul(merged, dx1_b, name="out_proj_dw", out_dtype=BF16, trans_a=True, tk=seq)
    dw_out = dw_out.reshape(N_CHIPS, -1, d_model)
    d_merged, (r_ff1, r_out) = _matmul(dx1_b, w_out_full, name="out_proj_dlhs", out_dtype=F32, trans_b=True,
                                       phases=[_swap_halves([dw_ff1, dw_out])])
    ps_ff1, ps_out = pair_sum(dw_ff1, r_ff1, "w_ff1"), pair_sum(dw_out, r_out, "w_out")
    d_o, dzu, dzv, dw_s, db_col, dlg, dlb, dag, dgg = _mix_bwd(
        z, o, d_merged, gmlp_ln_g, gmlp_ln_b, w_s[0], b_col, attn_out_g, gmlp_out_g, n_heads, name="mix_bwd")
    (dq, dk, dv, d_f_key, d_f_query), (q_ff1, q_out) = _attn_bwd(
        z, o, d_o, lse2, f_row, n_heads, name="attn_bwd", phases=[_send_partials([ps_ff1, ps_out])])
    g_ff1, g_out = chip_sum(ps_ff1, q_ff1, "w_ff1"), chip_sum(ps_out, q_out, "w_out")
    d_f = d_f_key.reshape(n_heads, seq) + d_f_query.reshape(n_heads, seq)
    d_f_pad = jnp.pad(d_f.T, ((0, 0), (0, LANES - n_heads)))
    dzf, db_f = _forget_bwd(d_f_pad, zb, name="forget_bwd")
    dz = jnp.concatenate([dq, dk, dv, dzu, dzv], axis=1)
    early_g = _pack([db_f[:, :n_heads], dlg, dlb, dw_s, db_col.T, dag, dgg, dg_ffn, dg_final])
    dwt_main, (g_ff1, g_out, early_all) = _matmul(dz, h, name="in_proj_dw", out_dtype=BF16, trans_a=True, tk=seq,
                                                  phases=[_join_halves([g_ff1, g_out]), _gather_small(early_g)])
    dwt_f, _ = _matmul(dzf, h, name="gate_dw", out_dtype=BF16, trans_a=True)
    bwd_segments = []
    for j in range(N_CHIPS):
        first = j * shard_cols
        if j < gate_slot:
            bwd_segments.append([(0, shard_cols, 0, 0, first)])
        elif j > gate_slot:
            bwd_segments.append([(0, shard_cols, 0, 0, first - n_heads)])
        else:
            bwd_segments.append([(0, gate_row, 0, 0, first), (gate_row, gate_row + n_heads, 1, 0, 0),
                                 (gate_row + n_heads, shard_cols, 2, 0, qkv)])
    dw_in = _realign_rows([dwt_main[None], dwt_f[None], dwt_main[None]], bwd_segments,
                          (N_CHIPS, shard_cols, d_model), name="dw_in_rows")
    dh_gate, (r_in,) = _matmul(dzf, wt_f, name="gate_dlhs", out_dtype=F32, phases=[_swap_halves([dw_in])])
    ps_in = pair_sum(dw_in, r_in, "w_in")
    dh, (q_in,) = _matmul(dz, wt_main, name="in_proj_dlhs", out_dtype=F32, residual=dh_gate, tk=2560,
                          phases=[_send_partials([ps_in])])
    g_in_sum = chip_sum(ps_in, q_in, "w_in")
    (grad_x, dg_mix), _ = _rmsnorm_bwd(dh, xs, dx1, norm_mix_g, name="norm_mix_bwd")
    late_g = _pack([dg_mix])
    g_in_sum, late_all = _exchange([_join_halves([g_in_sum]), _gather_small(late_g)], name="grads_join_w_in")

    big = {}
    for nm, g, w, m, v in zip(names, (g_in_sum, g_out, g_ff1, g_ff2), (wt_in, w_out[0], w_ff1[0], w_ff2[0]),
                              (mt_in, m_w_out[0], m_w_ff1[0], m_w_ff2[0]), (vt_in, v_w_out[0], v_w_ff1[0], v_w_ff2[0])):
        big[nm] = tuple((t.T if nm == "w_in" else t)[None] for t in _adamw(w, g, m, v, name="adamw_" + nm))

    small_params = dict(
        norm_mix_g=(norm_mix_g, m_norm_mix_g, v_norm_mix_g), b_f=(b_f, m_b_f, v_b_f),
        gmlp_ln_g=(gmlp_ln_g, m_gmlp_ln_g, v_gmlp_ln_g), gmlp_ln_b=(gmlp_ln_b, m_gmlp_ln_b, v_gmlp_ln_b),
        w_s=(w_s, m_w_s, v_w_s), b_s=(b_s, m_b_s, v_b_s), attn_out_g=(attn_out_g, m_attn_out_g, v_attn_out_g),
        gmlp_out_g=(gmlp_out_g, m_gmlp_out_g, v_gmlp_out_g), norm_ffn_g=(norm_ffn_g, m_norm_ffn_g, v_norm_ffn_g),
        norm_final_g=(norm_final_g, m_norm_final_g, v_norm_final_g))

    def small_step(group, grads_all, grads_own, label):
        w, m, v = ([small_params[nm][k] for nm in group] for k in range(3))
        packed = _adamw_small(grads_all, grads_own, place, _pack(w), _pack(m), _pack(v), name="adamw_small_" + label)
        parts = [_unpack(p, [a.shape for a in w]) for p in packed]
        return {nm: tuple(part[i] for part in parts) for i, nm in enumerate(group)}

    early = ["b_f", "gmlp_ln_g", "gmlp_ln_b", "w_s", "b_s", "attn_out_g", "gmlp_out_g", "norm_ffn_g", "norm_final_g"]
    small = {**small_step(early, early_all, early_g, "early"), **small_step(["norm_mix_g"], late_all, late_g, "late")}

    order = ["norm_mix_g", "w_in", "b_f", "gmlp_ln_g", "gmlp_ln_b", "w_s", "b_s", "attn_out_g", "gmlp_out_g", "w_out",
             "norm_ffn_g", "w_ff1", "w_ff2", "norm_final_g"]
    result = {**small, **big}
    total_loss = lax.psum(loss[0, 0], ("x", "y", "c"))
    outs = [total_loss, grad_x.reshape(x.shape)]
    for part in range(4):
        outs += [result[nm][part] for nm in order]
    return tuple(outs)
```

```python
import functools
import math

import numpy as np
import jax
import jax.numpy as jnp
from jax import lax
from jax.experimental import pallas as pl
from jax.experimental.pallas import tpu as pltpu

HEAD_DIM = 128
CHUNK = 128
EPS = 1e-6
LANES = 128
MXU_COLUMNS = 256
N_CHIPS = 4
N_DEV = 8
VMEM_LIMIT_BYTES = 56 * 1024 * 1024

ADAM_LR = 0.001
ADAM_B1 = 0.9
ADAM_B2 = 0.999
ADAM_EPS = 1e-08
ADAM_WD = 0.01
ADAM_STEP = 10

BF16 = jnp.bfloat16
F32 = jnp.float32
MESH = pl.DeviceIdType.MESH
ANY = pl.BlockSpec(memory_space=pl.ANY)
NEG_BIG = -1e30


def _params(*sem):
    return pltpu.CompilerParams(dimension_semantics=tuple(sem), vmem_limit_bytes=VMEM_LIMIT_BYTES)


def _tile(n, pref, unit):
    t = (min(pref, n) // unit) * unit
    while t >= unit:
        if n % t == 0:
            return t
        t -= unit
    return n


def _rc_tile(rows, cols, pref_rows=256, pref_cols=256):
    if rows % 16 == 0:
        return _tile(rows, pref_rows, 16), cols
    return rows, _tile(cols, pref_cols, LANES)


class _Phase:
    def __init__(self, arrays, out_shapes, in_place, n_sems, start, finish):
        self.arrays, self.out_shapes, self.in_place = list(arrays), list(out_shapes), in_place
        self.n_sems, self.start, self.finish = n_sems, start, finish

    @property
    def n_out(self):
        return len(self.arrays) if self.in_place else len(self.out_shapes)


def _run_phases(phases, steps, comm_in, comm_out, send_sems, recv_sems):
    at_in = at_out = at_sem = 0
    for ph in phases:
        for step in steps:
            getattr(ph, step)(comm_in[at_in:at_in + len(ph.arrays)], comm_out[at_out:at_out + ph.n_out],
                              lambda i, base=at_sem: send_sems.at[base + i], lambda i, base=at_sem: recv_sems.at[base + i])
        at_in, at_out, at_sem = at_in + len(ph.arrays), at_out + ph.n_out, at_sem + ph.n_sems


def _call(body, *, name, grid, in_specs, out_specs, out_shape, operands, semantics, scratch_shapes=(),
          n_prefetch=0, phases=()):
    in_specs, out_specs, out_shape = list(in_specs), list(out_specs), list(out_shape)
    scratch_shapes = list(scratch_shapes)
    n_in, n_out, n_scr = len(operands) - n_prefetch, len(out_shape), len(scratch_shapes)
    comm_in = [a for ph in phases for a in ph.arrays]
    comm_out = [jax.ShapeDtypeStruct(s.shape, s.dtype) for ph in phases
                for s in (ph.arrays if ph.in_place else ph.out_shapes)]
    aliases, at_in, at_out = {}, n_prefetch + n_in, n_out
    for ph in phases:
        if ph.in_place:
            aliases.update({at_in + r: at_out + r for r in range(len(ph.arrays))})
        at_in, at_out = at_in + len(ph.arrays), at_out + ph.n_out
    n_sems = sum(ph.n_sems for ph in phases)

    def hosted(*refs):
        pre, rest = refs[:n_prefetch], refs[n_prefetch:]
        ins, rest = rest[:n_in], rest[n_in:]
        cin, rest = rest[:len(comm_in)], rest[len(comm_in):]
        outs, rest = rest[:n_out], rest[n_out:]
        cout, rest = rest[:len(comm_out)], rest[len(comm_out):]
        scr = rest[:n_scr]
        if phases:
            send_sems, recv_sems = rest[n_scr:]
            ids = [pl.program_id(ax) for ax in range(len(grid))]
            first = functools.reduce(jnp.logical_and, [i == 0 for i in ids])
            last = functools.reduce(jnp.logical_and, [i == g - 1 for i, g in zip(ids, grid)])

            @pl.when(first)
            def _():
                _run_phases(phases, ("start",), cin, cout, send_sems, recv_sems)

        body(*pre, *ins, *outs, *scr)
        if phases:
            @pl.when(last)
            def _():
                _run_phases(phases, ("finish",), cin, cout, send_sems, recv_sems)

    all_in = in_specs + [ANY] * len(comm_in)
    all_out = out_specs + [ANY] * len(comm_out)
    all_scr = scratch_shapes + ([pltpu.SemaphoreType.DMA((n_sems,)), pltpu.SemaphoreType.DMA((n_sems,))] if phases else [])
    if phases:
        semantics = ("arbitrary",) * len(grid)
    kwargs = dict(name=name, out_shape=tuple(out_shape + comm_out), compiler_params=_params(*semantics),
                  input_output_aliases=aliases)
    if n_prefetch:
        kwargs["grid_spec"] = pltpu.PrefetchScalarGridSpec(
            num_scalar_prefetch=n_prefetch, grid=grid, in_specs=all_in, out_specs=tuple(all_out), scratch_shapes=all_scr)
    else:
        kwargs.update(grid=grid, in_specs=all_in, out_specs=tuple(all_out), scratch_shapes=all_scr)
    res = pl.pallas_call(hosted, **kwargs)(*operands, *comm_in)
    return tuple(res[:n_out]), tuple(res[n_out:])


def _only(results):
    outs, comm = results
    return outs[0] if len(outs) == 1 else outs, comm


def _matmul(a, b, *, name, out_dtype, trans_a=False, trans_b=False, tm=1024, tn=1024, tk=2048,
            square_lhs=False, relu=False, residual=None, scale2_by=None,
            b_sharded=False, out_sharded=False, phases=()):
    m, k = (a.shape[1], a.shape[0]) if trans_a else a.shape
    if b_sharded:
        if trans_b:
            n, ks = b.shape[1], b.shape[2]
            assert N_CHIPS * ks == k
        else:
            ns = b.shape[2]
            n = N_CHIPS * ns
            assert b.shape[1] == k
    else:
        n = b.shape[0] if trans_b else b.shape[1]
        assert (b.shape[1] if trans_b else b.shape[0]) == k
    tm = _tile(m, tm, 128)
    tn = _tile(n // N_CHIPS if (out_sharded or (b_sharded and not trans_b)) else n, tn, 128)
    tk = _tile(k // N_CHIPS if (b_sharded and trans_b) else k, tk, 128)
    nk = k // tk

    if trans_a:
        a_spec = pl.BlockSpec((tk, tm), lambda i, j, kk: (kk, i))
    else:
        a_spec = pl.BlockSpec((tm, tk), lambda i, j, kk: (i, kk))
    if b_sharded and trans_b:
        per = ks // tk
        assert per * tk == ks
        b_spec = pl.BlockSpec((None, tn, tk), lambda i, j, kk: (kk // per, j, kk % per))
    elif b_sharded:
        per = ns // tn
        assert per * tn == ns
        b_spec = pl.BlockSpec((None, tk, tn), lambda i, j, kk: (j // per, kk, j % per))
    elif trans_b:
        b_spec = pl.BlockSpec((tn, tk), lambda i, j, kk: (j, kk))
    else:
        b_spec = pl.BlockSpec((tk, tn), lambda i, j, kk: (kk, j))
    if out_sharded:
        ns_out = n // N_CHIPS
        per_o = ns_out // tn
        assert per_o * tn == ns_out
        out_shape = jax.ShapeDtypeStruct((N_CHIPS, m, ns_out), out_dtype)
        o_spec = pl.BlockSpec((None, tm, tn), lambda i, j, kk: (j // per_o, i, j % per_o))
    else:
        out_shape = jax.ShapeDtypeStruct((m, n), out_dtype)
        o_spec = pl.BlockSpec((tm, tn), lambda i, j, kk: (i, j))
    mn_spec = pl.BlockSpec((tm, tn), lambda i, j, kk: (i, j))

    operands, in_specs = [a, b], [a_spec, b_spec]
    if scale2_by is not None:
        operands.append(scale2_by)
        in_specs.append(mn_spec)
    if residual is not None:
        operands.append(residual)
        in_specs.append(mn_spec)
    dims = (((0 if trans_a else 1,), (1 if trans_b else 0,)), ((), ()))
    chunk = MXU_COLUMNS if tn % MXU_COLUMNS == 0 else tn

    def body(*refs):
        a_ref, b_ref = refs[0], refs[1]
        pos = 2
        scale_ref = res_ref = None
        if scale2_by is not None:
            scale_ref = refs[pos]
            pos += 1
        if residual is not None:
            res_ref = refs[pos]
            pos += 1
        o_ref = refs[pos]
        kk = pl.program_id(2)

        av = a_ref[...]
        if square_lhs:
            av = av.astype(F32)
            av = av * av
        av = av.astype(BF16)

        def finish(r, cols):
            if relu:
                r = jnp.maximum(r, 0.0)
            if scale_ref is not None:
                r = r * (2.0 * scale_ref[:, cols].astype(F32))
            if res_ref is not None:
                r = r + res_ref[:, cols].astype(F32)
            o_ref[:, cols] = r.astype(out_dtype)

        if nk == 1:
            for lo in range(0, tn, chunk):
                cols = slice(lo, lo + chunk)
                bv = (b_ref[cols, :] if trans_b else b_ref[:, cols]).astype(BF16)
                finish(lax.dot_general(av, bv, dims, preferred_element_type=F32), cols)
        else:
            acc_ref = refs[pos + 1]
            part = lax.dot_general(av, b_ref[...].astype(BF16), dims, preferred_element_type=F32)

            @pl.when(kk == 0)
            def _():
                acc_ref[...] = part

            @pl.when(jnp.logical_and(kk > 0, kk < nk - 1))
            def _():
                acc_ref[...] += part

            @pl.when(kk == nk - 1)
            def _():
                finish(acc_ref[...] + part, slice(None))

    return _only(_call(
        body, name=name, out_shape=[out_shape], grid=(m // tm, n // tn, nk),
        in_specs=in_specs, out_specs=[o_spec], operands=operands,
        scratch_shapes=[pltpu.VMEM((tm, tn), F32)] if nk > 1 else [],
        semantics=("parallel", "parallel", "arbitrary"), phases=phases))


def _rmsnorm_fwd(x, g, *, name, tr=512, phases=()):
    s, d = x.shape
    tr = _tile(s, tr, 8)

    def body(x_ref, g_ref, o_ref):
        xv = x_ref[...]
        r = lax.rsqrt(jnp.mean(xv * xv, axis=-1, keepdims=True) + EPS)
        o_ref[...] = ((xv * r) * g_ref[...]).astype(BF16)

    return _only(_call(
        body, name=name, out_shape=[jax.ShapeDtypeStruct((s, d), BF16)], grid=(s // tr,),
        in_specs=[pl.BlockSpec((tr, d), lambda i: (i, 0)), pl.BlockSpec((1, d), lambda i: (0, 0))],
        out_specs=[pl.BlockSpec((tr, d), lambda i: (i, 0))], operands=[x, g],
        semantics=("parallel",), phases=phases))


def _rms_bwd_rows(dy, xv, g):
    d = xv.shape[-1]
    r = lax.rsqrt(jnp.mean(xv * xv, axis=-1, keepdims=True) + EPS)
    gdy = dy * g
    dot = jnp.sum(gdy * xv, axis=-1, keepdims=True)
    dx = gdy * r - xv * (r * r * r) * (dot / d)
    return dx, dy * (xv * r)


def _rmsnorm_bwd(dy, x, res, g, *, name, tr=256, rounded_copy=False, phases=()):
    s, d = x.shape
    tr = _tile(s, tr, 8)

    def body(dy_ref, x_ref, res_ref, g_ref, dx_ref, dg_ref, *dxb_ref):
        @pl.when(pl.program_id(0) == 0)
        def _():
            dg_ref[...] = jnp.zeros_like(dg_ref)

        dx, dg_rows = _rms_bwd_rows(dy_ref[...].astype(F32), x_ref[...], g_ref[...])
        out = res_ref[...] + dx
        dx_ref[...] = out
        if rounded_copy:
            dxb_ref[0][...] = out.astype(BF16)
        dg_ref[...] += jnp.sum(dg_rows, axis=0, keepdims=True)

    row = pl.BlockSpec((tr, d), lambda i: (i, 0))
    vec = pl.BlockSpec((1, d), lambda i: (0, 0))
    extra = [jax.ShapeDtypeStruct((s, d), BF16)] if rounded_copy else []
    return _call(
        body, name=name,
        out_shape=[jax.ShapeDtypeStruct((s, d), F32), jax.ShapeDtypeStruct((1, d), F32)] + extra,
        grid=(s // tr,), in_specs=[row, row, row, vec], out_specs=[row, vec] + [row] * len(extra),
        operands=[dy, x, res, g], semantics=("arbitrary",), phases=phases)


def _loss_and_final_bwd(x2, target, g, *, name, tr=256):
    s, d = x2.shape
    tr = _tile(s, tr, 8)

    def body(x_ref, t_ref, g_ref, dx_ref, dxb_ref, dg_ref, loss_ref):
        @pl.when(pl.program_id(0) == 0)
        def _():
            dg_ref[...] = jnp.zeros_like(dg_ref)
            loss_ref[...] = jnp.zeros_like(loss_ref)

        xv, gv = x_ref[...], g_ref[...]
        r = lax.rsqrt(jnp.mean(xv * xv, axis=-1, keepdims=True) + EPS)
        err = (xv * r) * gv - t_ref[...]
        row_loss = jnp.mean(err * err, axis=-1, keepdims=True)
        loss_ref[...] += 0.5 * jnp.sum(row_loss, axis=0, keepdims=True)
        dx, dg_rows = _rms_bwd_rows(err / d, xv, gv)
        dx_ref[...] = dx
        dxb_ref[...] = dx.astype(BF16)
        dg_ref[...] += jnp.sum(dg_rows, axis=0, keepdims=True)

    row = pl.BlockSpec((tr, d), lambda i: (i, 0))
    vec = pl.BlockSpec((1, d), lambda i: (0, 0))
    one = pl.BlockSpec((1, 1), lambda i: (0, 0))
    return pl.pallas_call(
        body, name=name,
        out_shape=(jax.ShapeDtypeStruct((s, d), F32), jax.ShapeDtypeStruct((s, d), BF16),
                   jax.ShapeDtypeStruct((1, d), F32), jax.ShapeDtypeStruct((1, 1), F32)),
        grid=(s // tr,), in_specs=[row, row, vec], out_specs=(row, row, vec, one),
        compiler_params=_params("arbitrary"),
    )(x2, target, g)


def _tri_ones(n, lower):
    r = lax.broadcasted_iota(jnp.int32, (n, n), 0)
    c = lax.broadcasted_iota(jnp.int32, (n, n), 1)
    return jnp.where((c <= r) if lower else (c >= r), 1.0, 0.0).astype(F32)


def _forget_fwd(h, w_f, b_f, *, name, tr=256):
    s, d = h.shape
    tr = _tile(s, tr, 8)

    def body(h_ref, w_ref, b_ref, zb_ref, f_ref, carry):
        @pl.when(pl.program_id(0) == 0)
        def _():
            carry[...] = jnp.zeros_like(carry)

        zb = lax.dot_general(h_ref[...], w_ref[...], (((1,), (1,)), ((), ())), preferred_element_type=F32) + b_ref[...]
        zb_ref[...] = zb
        log_f = jnp.minimum(zb, 0.0) - jnp.log(1.0 + jnp.exp(-jnp.abs(zb)))
        run = jnp.dot(_tri_ones(tr, True), log_f, preferred_element_type=F32,
                      precision=lax.Precision.HIGHEST) + carry[...]
        f_ref[...] = run
        carry[...] = run[tr - 1:tr, :]

    row = pl.BlockSpec((tr, LANES), lambda i: (i, 0))
    return pl.pallas_call(
        body, name=name,
        out_shape=(jax.ShapeDtypeStruct((s, LANES), F32), jax.ShapeDtypeStruct((s, LANES), F32)),
        grid=(s // tr,),
        in_specs=[pl.BlockSpec((tr, d), lambda i: (i, 0)), pl.BlockSpec((LANES, d), lambda i: (0, 0)),
                  pl.BlockSpec((1, LANES), lambda i: (0, 0))],
        out_specs=(row, row), scratch_shapes=[pltpu.VMEM((1, LANES), F32)],
        compiler_params=_params("arbitrary"),
    )(h, w_f, b_f)


def _forget_bwd(d_f, zb, *, name, tr=256):
    s = zb.shape[0]
    tr = _tile(s, tr, 8)
    nb = s // tr

    def body(df_ref, zb_ref, dz_ref, db_ref, carry):
        @pl.when(pl.program_id(0) == 0)
        def _():
            carry[...] = jnp.zeros_like(carry)
            db_ref[...] = jnp.zeros_like(db_ref)

        run = jnp.dot(_tri_ones(tr, False), df_ref[...], preferred_element_type=F32,
                      precision=lax.Precision.HIGHEST) + carry[...]
        carry[...] = run[0:1, :]
        dz = run / (1.0 + jnp.exp(zb_ref[...]))
        dz_ref[...] = dz.astype(BF16)
        db_ref[...] += jnp.sum(dz, axis=0, keepdims=True)

    row = pl.BlockSpec((tr, LANES), lambda i: (nb - 1 - i, 0))
    return pl.pallas_call(
        body, name=name,
        out_shape=(jax.ShapeDtypeStruct((s, LANES), BF16), jax.ShapeDtypeStruct((1, LANES), F32)),
        grid=(nb,), in_specs=[row, row], out_specs=(row, pl.BlockSpec((1, LANES), lambda i: (0, 0))),
        scratch_shapes=[pltpu.VMEM((1, LANES), F32)],
        compiler_params=_params("arbitrary"),
    )(d_f, zb)


def _pairs(nblk, by_kv):
    if by_kv:
        pr = [(i, j) for j in range(nblk) for i in range(j, nblk)]
    else:
        pr = [(i, j) for i in range(nblk) for j in range(i + 1)]
    return (jnp.asarray(np.array([p[0] for p in pr], np.int32)), jnp.asarray(np.array([p[1] for p in pr], np.int32)))


def _causal_mask(t):
    r = lax.broadcasted_iota(jnp.int32, (t, t), 0)
    c = lax.broadcasted_iota(jnp.int32, (t, t), 1)
    return c <= r


LOG2E = math.log2(math.e)
QK_TO_LOG2 = LOG2E / math.sqrt(HEAD_DIM)


def _attn_logits2(q, k, fk_row):
    sc = lax.dot_general(q, k, (((1,), (1,)), ((), ())), preferred_element_type=F32)
    return sc * QK_TO_LOG2 - fk_row * LOG2E


def _attn_fwd(z, f_row, n_heads, *, name, tb=1024, phases=()):
    s = z.shape[0]
    tb = _tile(s, tb, 128)
    nblk = s // tb
    rep = tb // LANES
    qi, kj = _pairs(nblk, by_kv=False)

    def body(qi_ref, kj_ref, q_ref, k_ref, v_ref, fk_ref, o_ref, lse_ref, m_sc, l_sc, acc_sc):
        p = pl.program_id(1)
        i, j = qi_ref[p], kj_ref[p]

        @pl.when(j == 0)
        def _():
            m_sc[...] = jnp.full_like(m_sc, NEG_BIG)
            l_sc[...] = jnp.zeros_like(l_sc)
            acc_sc[...] = jnp.zeros_like(acc_sc)

        def update(masked):
            s2 = _attn_logits2(q_ref[...], k_ref[...], fk_ref[...])
            if masked:
                s2 = jnp.where(_causal_mask(tb), s2, NEG_BIG)
            m_old = m_sc[...]
            m_new = jnp.maximum(m_old, jnp.max(s2, axis=-1, keepdims=True))
            alpha = jnp.exp2(m_old - m_new)
            pv = jnp.exp2(s2 - jnp.tile(m_new, (1, rep)))
            l_sc[...] = alpha * l_sc[...] + jnp.sum(pv, axis=-1, keepdims=True)
            acc_sc[...] = alpha * acc_sc[...] + jnp.dot(pv.astype(BF16), v_ref[...], preferred_element_type=F32)
            m_sc[...] = m_new

        @pl.when(j < i)
        def _():
            update(False)

        @pl.when(j == i)
        def _():
            update(True)
            o_ref[...] = (acc_sc[...] / l_sc[...]).astype(BF16)
            lse_ref[...] = m_sc[...] + jnp.log2(l_sc[...])

    h = n_heads
    return _call(
        body, name=name, n_prefetch=2, grid=(h, int(qi.shape[0])),
        in_specs=[
            pl.BlockSpec((tb, HEAD_DIM), lambda hh, p, qi_r, kj_r: (qi_r[p], hh)),
            pl.BlockSpec((tb, HEAD_DIM), lambda hh, p, qi_r, kj_r: (kj_r[p], h + hh)),
            pl.BlockSpec((tb, HEAD_DIM), lambda hh, p, qi_r, kj_r: (kj_r[p], 2 * h + hh)),
            pl.BlockSpec((None, 1, tb), lambda hh, p, qi_r, kj_r: (hh, 0, kj_r[p])),
        ],
        out_specs=[
            pl.BlockSpec((tb, HEAD_DIM), lambda hh, p, qi_r, kj_r: (qi_r[p], hh)),
            pl.BlockSpec((None, tb, LANES), lambda hh, p, qi_r, kj_r: (hh, qi_r[p], 0)),
        ],
        scratch_shapes=[pltpu.VMEM((tb, LANES), F32), pltpu.VMEM((tb, LANES), F32), pltpu.VMEM((tb, HEAD_DIM), F32)],
        out_shape=[jax.ShapeDtypeStruct((s, h * HEAD_DIM), BF16), jax.ShapeDtypeStruct((h, s, LANES), F32)],
        operands=[qi, kj, z, z, z, f_row], semantics=("parallel", "arbitrary"), phases=phases)


def _attn_bwd(z, o, d_o, lse2, f_row, n_heads, *, name, tb=1024, phases=()):
    s = z.shape[0]
    tb = _tile(s, tb, 128)
    nblk = s // tb
    rep = tb // LANES
    qi, kj = _pairs(nblk, by_kv=True)
    n_pairs = int(qi.shape[0])
    scale = 1.0 / math.sqrt(HEAD_DIM)
    h = n_heads

    def body(qi_ref, kj_ref, q_ref, k_ref, v_ref, o_ref, do_ref, lse_ref, fk_ref,
             dq_ref, dk_ref, dv_ref, df_ref, dfq_ref, dq_sc, dk_sc, dv_sc, df_sc, dfq_sc):
        p = pl.program_id(1)
        i, j = qi_ref[p], kj_ref[p]

        @pl.when(p == 0)
        def _():
            dq_sc[...] = jnp.zeros_like(dq_sc)
            dfq_sc[...] = jnp.zeros_like(dfq_sc)

        @pl.when(i == j)
        def _():
            dk_sc[...] = jnp.zeros_like(dk_sc)
            dv_sc[...] = jnp.zeros_like(dv_sc)
            df_sc[...] = jnp.zeros_like(df_sc)

        def update(masked):
            q, k, v, do = q_ref[...], k_ref[...], v_ref[...], do_ref[...]
            delta = jnp.sum(do.astype(F32) * o_ref[...].astype(F32), axis=-1, keepdims=True)
            pv = jnp.exp2(_attn_logits2(q, k, fk_ref[...]) - jnp.tile(lse_ref[...], (1, rep)))
            if masked:
                pv = jnp.where(_causal_mask(tb), pv, 0.0)
            dp = lax.dot_general(do, v, (((1,), (1,)), ((), ())), preferred_element_type=F32)
            ds = pv * (dp - delta)
            ds_b = ds.astype(BF16)
            dv_sc[...] += lax.dot_general(pv.astype(BF16), do, (((0,), (0,)), ((), ())), preferred_element_type=F32)
            dk_sc[...] += lax.dot_general(ds_b, q, (((0,), (0,)), ((), ())), preferred_element_type=F32)
            rows = pl.ds(pl.multiple_of(i * tb, tb), tb)
            dq_sc[rows, :] += jnp.dot(ds_b, k, preferred_element_type=F32)
            df_sc[...] -= jnp.sum(ds, axis=0, keepdims=True)
            dfq_sc[rows, :] += jnp.broadcast_to(jnp.sum(ds, axis=1, keepdims=True), (tb, LANES))

        @pl.when(i > j)
        def _():
            update(False)

        @pl.when(i == j)
        def _():
            update(True)

        @pl.when(i == nblk - 1)
        def _():
            dk_ref[...] = (dk_sc[...] * scale).astype(BF16)
            dv_ref[...] = dv_sc[...].astype(BF16)
            df_ref[...] = df_sc[...]

        @pl.when(p == n_pairs - 1)
        def _():
            dq_ref[...] = (dq_sc[...] * scale).astype(BF16)
            dfq_ref[...] = jnp.transpose(dfq_sc[...])[0:1, :]

    qblk = lambda off: pl.BlockSpec((tb, HEAD_DIM), lambda hh, p, qi_r, kj_r: (qi_r[p], off + hh))
    kblk = lambda off: pl.BlockSpec((tb, HEAD_DIM), lambda hh, p, qi_r, kj_r: (kj_r[p], off + hh))
    qrep = pl.BlockSpec((None, tb, LANES), lambda hh, p, qi_r, kj_r: (hh, qi_r[p], 0))
    krow = pl.BlockSpec((None, 1, tb), lambda hh, p, qi_r, kj_r: (hh, 0, kj_r[p]))
    act = jax.ShapeDtypeStruct((s, h * HEAD_DIM), BF16)
    return _call(
        body, name=name, n_prefetch=2, grid=(h, n_pairs),
        in_specs=[qblk(0), kblk(h), kblk(2 * h), qblk(0), qblk(0), qrep, krow],
        out_specs=[
            pl.BlockSpec((s, HEAD_DIM), lambda hh, p, qi_r, kj_r: (0, hh)),
            kblk(0), kblk(0), krow,
            pl.BlockSpec((None, 1, s), lambda hh, p, qi_r, kj_r: (hh, 0, 0)),
        ],
        scratch_shapes=[pltpu.VMEM((s, HEAD_DIM), F32), pltpu.VMEM((tb, HEAD_DIM), F32),
                        pltpu.VMEM((tb, HEAD_DIM), F32), pltpu.VMEM((1, tb), F32), pltpu.VMEM((s, LANES), F32)],
        out_shape=[act, act, act, jax.ShapeDtypeStruct((h, 1, s), F32), jax.ShapeDtypeStruct((h, 1, s), F32)],
        operands=[qi, kj, z, z, z, o, d_o, lse2, f_row], semantics=("parallel", "arbitrary"), phases=phases)


GELU_C = math.sqrt(2.0 / math.pi)
GELU_A = 0.044715


def _gelu(x):
    return 0.5 * x * (1.0 + jnp.tanh(GELU_C * (x + GELU_A * (x * x * x))))


def _gelu_and_grad(x):
    t = jnp.tanh(GELU_C * (x + GELU_A * (x * x * x)))
    y = 0.5 * x * (1.0 + t)
    dy = 0.5 * (1.0 + t) + 0.5 * x * (1.0 - t * t) * (GELU_C * (1.0 + 3.0 * GELU_A * (x * x)))
    return y, dy


def _layernorm_parts(g):
    mu = jnp.mean(g, axis=-1, keepdims=True)
    xc = g - mu
    rs = lax.rsqrt(jnp.mean(xc * xc, axis=-1, keepdims=True) + EPS)
    return xc * rs, rs


def _spatial_mix(w_ref, bcol_ref, vv_b, n_heads, n_chunks):
    tril = _causal_mask(CHUNK)
    cols = []
    for hh in range(n_heads):
        wc = jnp.where(tril, w_ref[hh], 0.0).astype(BF16)
        lanes = slice(hh * HEAD_DIM, (hh + 1) * HEAD_DIM)
        rows = [jnp.dot(wc, vv_b[c * CHUNK:(c + 1) * CHUNK, lanes], preferred_element_type=F32)
                + bcol_ref[:, hh:hh + 1] for c in range(n_chunks)]
        cols.append(jnp.concatenate(rows, axis=0))
    return jnp.concatenate(cols, axis=1)


def _mix_fwd(z, o, ln_g, ln_b, w_s, b_col, attn_g, gm_g, n_heads, *, name, tr=256):
    s = z.shape[0]
    dg = n_heads * HEAD_DIM
    tr = _tile(s, tr, CHUNK)
    n_chunks = tr // CHUNK

    def body(zu_ref, zv_ref, o_ref, lg_ref, lb_ref, w_ref, bcol_ref, ag_ref, gg_ref, out_ref):
        u = _gelu(zu_ref[...].astype(F32))
        xhat, _ = _layernorm_parts(_gelu(zv_ref[...].astype(F32)))
        vv = xhat * lg_ref[...] + lb_ref[...]
        gm = u * _spatial_mix(w_ref, bcol_ref, vv.astype(BF16), n_heads, n_chunks)
        rg = lax.rsqrt(jnp.mean(gm * gm, axis=-1, keepdims=True) + EPS)
        ov = o_ref[...].astype(F32)
        ra = lax.rsqrt(jnp.mean(ov * ov, axis=-1, keepdims=True) + EPS)
        out_ref[:, :dg] = ((ov * ra) * ag_ref[...]).astype(BF16)
        out_ref[:, dg:] = ((gm * rg) * gg_ref[...]).astype(BF16)

    vec = pl.BlockSpec((1, dg), lambda i: (0, 0))
    return pl.pallas_call(
        body, name=name, out_shape=jax.ShapeDtypeStruct((s, 2 * dg), BF16), grid=(s // tr,),
        in_specs=[pl.BlockSpec((tr, dg), lambda i: (i, 3)), pl.BlockSpec((tr, dg), lambda i: (i, 4)),
                  pl.BlockSpec((tr, dg), lambda i: (i, 0)), vec, vec,
                  pl.BlockSpec((n_heads, CHUNK, CHUNK), lambda i: (0, 0, 0)),
                  pl.BlockSpec((CHUNK, n_heads), lambda i: (0, 0)), vec, vec],
        out_specs=pl.BlockSpec((tr, 2 * dg), lambda i: (i, 0)),
        compiler_params=_params("parallel"),
    )(z, z, o, ln_g, ln_b, w_s, b_col, attn_g, gm_g)


def _mix_bwd(z, o, d_merged, ln_g, ln_b, w_s, b_col, attn_g, gm_g, n_heads, *, name, tr=256):
    s = z.shape[0]
    dg = n_heads * HEAD_DIM
    tr = _tile(s, tr, CHUNK)
    n_chunks = tr // CHUNK

    def body(zu_ref, zv_ref, o_ref, dm_ref, lg_ref, lb_ref, w_ref, bcol_ref, ag_ref, gg_ref,
             do_ref, dzu_ref, dzv_ref, dw_ref, dbcol_ref, dlg_ref, dlb_ref, dag_ref, dgg_ref):
        @pl.when(pl.program_id(0) == 0)
        def _():
            for ref in (dw_ref, dbcol_ref, dlg_ref, dlb_ref, dag_ref, dgg_ref):
                ref[...] = jnp.zeros_like(ref)

        d_o, dag_rows = _rms_bwd_rows(dm_ref[:, :dg], o_ref[...].astype(F32), ag_ref[...])
        do_ref[...] = d_o.astype(BF16)
        dag_ref[...] += jnp.sum(dag_rows, axis=0, keepdims=True)

        u, du_dz = _gelu_and_grad(zu_ref[...].astype(F32))
        gv, dgv_dz = _gelu_and_grad(zv_ref[...].astype(F32))
        xhat, rs = _layernorm_parts(gv)
        lg = lg_ref[...]
        vv_b = (xhat * lg + lb_ref[...]).astype(BF16)
        mix = _spatial_mix(w_ref, bcol_ref, vv_b, n_heads, n_chunks)
        gm = u * mix
        d_gm, dgg_rows = _rms_bwd_rows(dm_ref[:, dg:], gm, gg_ref[...])
        dgg_ref[...] += jnp.sum(dgg_rows, axis=0, keepdims=True)
        dzu_ref[...] = ((d_gm * mix) * du_dz).astype(BF16)
        d_mix = d_gm * u
        d_mix_b = d_mix.astype(BF16)

        tril = _causal_mask(CHUNK)
        lane = lax.broadcasted_iota(jnp.int32, (CHUNK, n_heads), 1)
        cols = []
        db = jnp.zeros((CHUNK, n_heads), F32)
        for hh in range(n_heads):
            wc = jnp.where(tril, w_ref[hh], 0.0).astype(BF16)
            lanes = slice(hh * HEAD_DIM, (hh + 1) * HEAD_DIM)
            dw = jnp.zeros((CHUNK, CHUNK), F32)
            dmix_sum = jnp.zeros((CHUNK, HEAD_DIM), F32)
            rows = []
            for c in range(n_chunks):
                rws = slice(c * CHUNK, (c + 1) * CHUNK)
                dmb = d_mix_b[rws, lanes]
                dw += lax.dot_general(dmb, vv_b[rws, lanes], (((1,), (1,)), ((), ())), preferred_element_type=F32)
                rows.append(lax.dot_general(wc, dmb, (((0,), (0,)), ((), ())), preferred_element_type=F32))
                dmix_sum += d_mix[rws, lanes]
            dw_ref[hh] += jnp.where(tril, dw, 0.0)
            db += jnp.where(lane == hh, jnp.sum(dmix_sum, axis=-1, keepdims=True), 0.0)
            cols.append(jnp.concatenate(rows, axis=0))
        dbcol_ref[...] += db
        d_vv = jnp.concatenate(cols, axis=1)

        dlg_ref[...] += jnp.sum(d_vv * xhat, axis=0, keepdims=True)
        dlb_ref[...] += jnp.sum(d_vv, axis=0, keepdims=True)
        d_xhat = d_vv * lg
        d_gv = rs * (d_xhat - jnp.mean(d_xhat, axis=-1, keepdims=True)
                     - xhat * jnp.mean(d_xhat * xhat, axis=-1, keepdims=True))
        dzv_ref[...] = (d_gv * dgv_dz).astype(BF16)

    vec = pl.BlockSpec((1, dg), lambda i: (0, 0))
    wspec = pl.BlockSpec((n_heads, CHUNK, CHUNK), lambda i: (0, 0, 0))
    bspec = pl.BlockSpec((CHUNK, n_heads), lambda i: (0, 0))
    rowb = pl.BlockSpec((tr, dg), lambda i: (i, 0))
    act = jax.ShapeDtypeStruct((s, dg), BF16)
    vshape = jax.ShapeDtypeStruct((1, dg), F32)
    return pl.pallas_call(
        body, name=name,
        out_shape=(act, act, act, jax.ShapeDtypeStruct((n_heads, CHUNK, CHUNK), F32),
                   jax.ShapeDtypeStruct((CHUNK, n_heads), F32), vshape, vshape, vshape, vshape),
        grid=(s // tr,),
        in_specs=[pl.BlockSpec((tr, dg), lambda i: (i, 3)), pl.BlockSpec((tr, dg), lambda i: (i, 4)),
                  rowb, pl.BlockSpec((tr, 2 * dg), lambda i: (i, 0)), vec, vec, wspec, bspec, vec, vec],
        out_specs=(rowb, rowb, rowb, wspec, bspec, vec, vec, vec, vec),
        compiler_params=_params("arbitrary"),
    )(z, z, o, d_merged, ln_g, ln_b, w_s, b_col, attn_g, gm_g)


def _place():
    x, y, c = lax.axis_index("x"), lax.axis_index("y"), lax.axis_index("c")
    other_chips = [(1 - x, y), (x, 1 - y), (1 - x, 1 - y)]
    return x, y, c, other_chips


def _remote(src, dst, send_sem, recv_sem, to):
    return pltpu.make_async_remote_copy(src_ref=src, dst_ref=dst, send_sem=send_sem, recv_sem=recv_sem,
                                        device_id=to, device_id_type=MESH)


def _cast_into_slot(w, place, *, name, phases=()):
    rows, cols = w.shape
    tr, tc = _rc_tile(rows, cols)

    def body(place_ref, w_ref, o_ref):
        o_ref[...] = w_ref[...].astype(BF16)

    return _only(_call(
        body, name=name, n_prefetch=1, grid=(rows // tr, cols // tc),
        in_specs=[pl.BlockSpec((tr, tc), lambda i, j, pr: (i, j))],
        out_specs=[pl.BlockSpec((None, tr, tc), lambda i, j, pr: (pr[0], i, j))],
        out_shape=[jax.ShapeDtypeStruct((N_CHIPS, rows, cols), BF16)], operands=[place, w],
        semantics=("parallel", "parallel"), phases=phases))


def _exchange(phases, *, name):
    comm_in = [a for ph in phases for a in ph.arrays]
    comm_out = [jax.ShapeDtypeStruct(s.shape, s.dtype) for ph in phases for s in (ph.arrays if ph.in_place else ph.out_shapes)]
    aliases, at_in, at_out = {}, 0, 0
    for ph in phases:
        if ph.in_place:
            aliases.update({at_in + r: at_out + r for r in range(len(ph.arrays))})
        at_in, at_out = at_in + len(ph.arrays), at_out + ph.n_out
    n_sems = sum(ph.n_sems for ph in phases)

    def body(*refs):
        cin, cout = refs[:len(comm_in)], refs[len(comm_in):len(comm_in) + len(comm_out)]
        send_sems, recv_sems = refs[len(comm_in) + len(comm_out):]
        _run_phases(phases, ("start", "finish"), cin, cout, send_sems, recv_sems)

    return pl.pallas_call(
        body, name=name, out_shape=tuple(comm_out), in_specs=[ANY] * len(comm_in), out_specs=tuple([ANY] * len(comm_out)),
        input_output_aliases=aliases,
        scratch_shapes=[pltpu.SemaphoreType.DMA((n_sems,)), pltpu.SemaphoreType.DMA((n_sems,))],
    )(*comm_in)


GATHER_PARTS = 4


def _gather(bufs, stage, part=(0, GATHER_PARTS)):
    n = 3 * len(bufs)
    lo, hi = part

    def copies(outs, send, recv, d2d, incoming):
        x, y, c, chips = _place()
        for t, buf in enumerate(outs):
            half = buf.shape[2] // 2
            piece = half // GATHER_PARTS
            for k, (cx, cy) in enumerate(chips):
                i = 3 * t + k + (n if (d2d and stage == "both") else 0)
                cols = pl.ds(((1 - c) if (d2d and incoming) else c) * half + lo * piece, (hi - lo) * piece)
                blk = buf.at[(2 * cx + cy) if (d2d or incoming) else (2 * x + y), :, cols]
                yield _remote(blk, blk, send(i), recv(i), (x, y, 1 - c) if d2d else (cx, cy, c))

    def start(ins, outs, send, recv):
        for cp in copies(outs, send, recv, stage == "d2d", False):
            cp.start()

    def finish(ins, outs, send, recv):
        if stage == "both":
            for arrival, onward in zip(copies(outs, send, recv, False, True), copies(outs, send, recv, True, False)):
                arrival.wait_recv()
                onward.start()
        for cp in copies(outs, send, recv, stage != "ici", True):
            cp.wait_recv()
        for d2d in ((False, True) if stage == "both" else (stage == "d2d",)):
            for cp in copies(outs, send, recv, d2d, False):
                cp.wait_send()

    return _Phase(bufs, [], True, (2 if stage == "both" else 1) * n, start, finish)


def _merge(first, second):
    n_first = first.n_sems

    def later(sem):
        return lambda i: sem(n_first + i)

    def start(ins, outs, send, recv):
        first.start(ins, outs, send, recv)
        second.start(ins, outs, later(send), later(recv))

    def finish(ins, outs, send, recv):
        first.finish(ins, outs, send, recv)
        second.finish(ins, outs, later(send), later(recv))

    return _Phase(first.arrays, [], True, n_first + second.n_sems, start, finish)


def _swap_halves(grads):
    def copies(ins, outs, send, recv):
        x, y, c, _ = _place()
        for t, g in enumerate(ins):
            half = g.shape[2] // 2
            yield _remote(g.at[:, :, pl.ds((1 - c) * half, half)], outs[t], send(t), recv(t), (x, y, 1 - c))

    def start(ins, outs, send, recv):
        for cp in copies(ins, outs, send, recv):
            cp.start()

    def finish(ins, outs, send, recv):
        for cp in copies(ins, outs, send, recv):
            cp.wait()

    shapes = [jax.ShapeDtypeStruct((a.shape[0], a.shape[1], a.shape[2] // 2), a.dtype) for a in grads]
    return _Phase(grads, shapes, False, len(grads), start, finish)


def _add_halves(grad, received, place, *, name):
    ns, rows, half = received.shape
    tr, tc = _rc_tile(rows, half, pref_rows=1024)
    per = half // tc

    def body(place_ref, g_ref, r_ref, o_ref):
        o_ref[...] = (g_ref[...].astype(F32) + r_ref[...].astype(F32)).astype(BF16)

    grid_spec = pltpu.PrefetchScalarGridSpec(
        num_scalar_prefetch=1, grid=(ns, rows // tr, per),
        in_specs=[pl.BlockSpec((None, tr, tc), lambda s, i, j, pr: (s, i, pr[1] * per + j)),
                  pl.BlockSpec((None, tr, tc), lambda s, i, j, pr: (s, i, j))],
        out_specs=pl.BlockSpec((None, tr, tc), lambda s, i, j, pr: (s, i, j)),
    )
    return pl.pallas_call(
        body, name=name, grid_spec=grid_spec, out_shape=jax.ShapeDtypeStruct(received.shape, BF16),
        compiler_params=_params("parallel", "parallel", "parallel"),
    )(place, grad, received)


def _send_partials(parts, piece=(0, 1)):
    k_th, n_pieces = piece

    def cols(part):
        width = part.shape[2] // n_pieces
        return pl.ds(k_th * width, width)

    def start(ins, outs, send, recv):
        x, y, c, chips = _place()
        for t, part in enumerate(ins):
            for k, (cx, cy) in enumerate(chips):
                _remote(part.at[2 * cx + cy, :, cols(part)], outs[t].at[2 * x + y],
                        send(3 * t + k), recv(3 * t + k), (cx, cy, c)).start()

    def finish(ins, outs, send, recv):
        x, y, c, chips = _place()
        for t, part in enumerate(ins):
            for k, (cx, cy) in enumerate(chips):
                slot = outs[t].at[2 * cx + cy]
                _remote(slot, slot, send(3 * t + k), recv(3 * t + k), (cx, cy, c)).wait_recv()
        for t, part in enumerate(ins):
            for k, (cx, cy) in enumerate(chips):
                sent = part.at[2 * cx + cy, :, cols(part)]
                _remote(sent, sent, send(3 * t + k), recv(3 * t + k), (cx, cy, c)).wait_send()

    shapes = [jax.ShapeDtypeStruct(a.shape[:2] + (a.shape[2] // n_pieces,), a.dtype) for a in parts]
    return _Phase(parts, shapes, False, 3 * len(parts), start, finish)


def _sum_chips(parts, slots, place, *, name, piece=(0, 1), into=None):
    ns, rows, width = slots.shape
    k_th, n_pieces = piece
    half = width * n_pieces
    tr, tc = _rc_tile(rows, width, pref_rows=512)
    per = width // tc

    def body(place_ref, p_ref, s_ref, *rest):
        acc = p_ref[...].astype(F32)
        for k in range(ns):
            acc = acc + jnp.where(place_ref[0] == k, 0.0, s_ref[k].astype(F32))
        rest[-1][...] = acc

    grid_spec = pltpu.PrefetchScalarGridSpec(
        num_scalar_prefetch=1, grid=(rows // tr, per),
        in_specs=[pl.BlockSpec((None, tr, tc), lambda i, j, pr: (pr[0], i, k_th * per + j)),
                  pl.BlockSpec((ns, tr, tc), lambda i, j, pr: (0, i, j))] + ([ANY] if into is not None else []),
        out_specs=pl.BlockSpec((tr, tc), lambda i, j, pr: (i, (pr[1] * n_pieces + k_th) * per + j)),
    )
    return pl.pallas_call(
        body, name=name, grid_spec=grid_spec, out_shape=jax.ShapeDtypeStruct((rows, 2 * half), F32),
        input_output_aliases={3: 0} if into is not None else {},
        compiler_params=_params("parallel", "parallel"),
    )(place, parts, slots, *([into] if into is not None else []))


def _join_halves(bufs):
    def copies(outs, send, recv, incoming):
        x, y, c, _ = _place()
        for t, buf in enumerate(outs):
            half = buf.shape[1] // 2
            cols = buf.at[:, pl.ds(((1 - c) if incoming else c) * half, half)]
            yield _remote(cols, cols, send(t), recv(t), (x, y, 1 - c))

    def start(ins, outs, send, recv):
        for cp in copies(outs, send, recv, False):
            cp.start()

    def finish(ins, outs, send, recv):
        for cp in copies(outs, send, recv, True):
            cp.wait_recv()
        for cp in copies(outs, send, recv, False):
            cp.wait_send()

    return _Phase(bufs, [], True, len(bufs), start, finish)


def _gather_small(buf):
    def slot(out, px, py, pc):
        return out.at[4 * px + 2 * py + pc]

    def start(ins, outs, send, recv):
        x, y, c, chips = _place()
        mine = slot(outs[0], x, y, c)
        _remote(ins[0], mine, send(0), recv(0), (x, y, 1 - c)).start()
        for k, (cx, cy) in enumerate(chips):
            _remote(ins[0], mine, send(1 + k), recv(1 + k), (cx, cy, c)).start()

    def finish(ins, outs, send, recv):
        x, y, c, chips = _place()
        sibling = (x, y, 1 - c)
        for k, (cx, cy) in enumerate(chips):
            arrived = slot(outs[0], cx, cy, c)
            _remote(arrived, arrived, send(1 + k), recv(1 + k), sibling).wait_recv()
            _remote(arrived, arrived, send(4 + k), recv(4 + k), sibling).start()
        theirs = slot(outs[0], x, y, 1 - c)
        _remote(theirs, theirs, send(0), recv(0), sibling).wait_recv()
        for k, (cx, cy) in enumerate(chips):
            passed = slot(outs[0], cx, cy, 1 - c)
            _remote(passed, passed, send(4 + k), recv(4 + k), sibling).wait_recv()
        for i in range(7):
            _remote(ins[0], ins[0], send(i), recv(i), sibling).wait_send()

    return _Phase([buf], [jax.ShapeDtypeStruct((N_DEV,) + buf.shape, buf.dtype)], False, 7, start, finish)


def _adamw_math(w, g, m, v):
    m = ADAM_B1 * m + (1.0 - ADAM_B1) * g
    v = ADAM_B2 * v + (1.0 - ADAM_B2) * (g * g)
    m_hat = m / (1.0 - ADAM_B1 ** ADAM_STEP)
    v_hat = v / (1.0 - ADAM_B2 ** ADAM_STEP)
    delta = -ADAM_LR * (m_hat / (jnp.sqrt(v_hat) + ADAM_EPS) + ADAM_WD * w)
    return delta, m, v


def _adamw(w, g, m, v, *, name):
    rows, cols = w.shape
    tr, tc = _rc_tile(rows, cols)

    def body(w_ref, g_ref, m_ref, v_ref, go_ref, d_ref, mo_ref, vo_ref):
        g = g_ref[...]
        go_ref[...] = g
        d_ref[...], mo_ref[...], vo_ref[...] = _adamw_math(w_ref[...], g, m_ref[...], v_ref[...])

    blk = pl.BlockSpec((tr, tc), lambda i, j: (i, j))
    shape = jax.ShapeDtypeStruct((rows, cols), F32)
    return pl.pallas_call(
        body, name=name, out_shape=(shape, shape, shape, shape), grid=(rows // tr, cols // tc),
        in_specs=[blk] * 4, out_specs=(blk, blk, blk, blk), compiler_params=_params("parallel", "parallel"),
    )(w, g, m, v)


def _adamw_small(gathered, own, place, w, m, v, *, name):
    nd = gathered.shape[0]

    def body(place_ref, gs_ref, own_ref, w_ref, m_ref, v_ref, g_ref, d_ref, mo_ref, vo_ref):
        me = 2 * place_ref[0] + place_ref[1]
        g = jnp.zeros(own_ref.shape, F32)
        for k in range(nd):
            g = g + jnp.where(me == k, own_ref[...], gs_ref[k])
        g_ref[...] = g
        d_ref[...], mo_ref[...], vo_ref[...] = _adamw_math(w_ref[...], g, m_ref[...], v_ref[...])

    whole = pl.BlockSpec(w.shape, lambda i, pr: (0, 0))
    grid_spec = pltpu.PrefetchScalarGridSpec(
        num_scalar_prefetch=1, grid=(1,),
        in_specs=[pl.BlockSpec(gathered.shape, lambda i, pr: (0, 0, 0)), whole, whole, whole, whole],
        out_specs=(whole, whole, whole, whole))
    shape = jax.ShapeDtypeStruct(w.shape, F32)
    return pl.pallas_call(body, name=name, grid_spec=grid_spec, out_shape=(shape, shape, shape, shape),
                          compiler_params=_params("arbitrary"))(place, gathered, own, w, m, v)


def _pack(parts):
    flat = jnp.concatenate([p.reshape(-1).astype(F32) for p in parts])
    rows = -(-flat.shape[0] // (8 * LANES)) * 8
    return jnp.pad(flat, (0, rows * LANES - flat.shape[0])).reshape(rows, LANES)


def _unpack(buf, shapes):
    flat = buf.reshape(-1)
    out, pos = [], 0
    for shp in shapes:
        size = int(np.prod(shp))
        out.append(flat[pos:pos + size].reshape(shp))
        pos += size
    return out


ROW_BLOCK = 256


def _realign_rows(sources, segments, out_shape, *, name):
    n_slots, rows, cols = out_shape
    n_src = len(sources)
    per_slot = -(-rows // ROW_BLOCK)
    table = np.zeros((6, n_slots * per_slot, n_src), np.int32)
    for so in range(n_slots):
        for first, last, src, src_slot, src_row in segments[so]:
            for blk in range(first // ROW_BLOCK, (last - 1) // ROW_BLOCK + 1):
                lo, hi = max(first, blk * ROW_BLOCK), min(last, (blk + 1) * ROW_BLOCK)
                base = src_row + (blk * ROW_BLOCK - first)
                m0 = (base + lo - blk * ROW_BLOCK) // ROW_BLOCK
                at = so * per_slot + blk
                assert table[4, at, src] == 0, "two segments of one block share a source operand"
                table[:, at, src] = (src_slot, m0, base - m0 * ROW_BLOCK, lo - blk * ROW_BLOCK, hi - blk * ROW_BLOCK,
                                     min(2 * ROW_BLOCK, sources[src].shape[1] - m0 * ROW_BLOCK))
    last_block = [-(-a.shape[1] // ROW_BLOCK) - 1 for a in sources]

    def body(slot_ref, blk_ref, off_ref, lo_ref, hi_ref, valid_ref, *refs):
        o_ref, acc = refs[2 * n_src], refs[2 * n_src + 1]
        at = (pl.program_id(0) * per_slot + pl.program_id(1)) * n_src
        acc[...] = jnp.zeros_like(acc)
        for p in range(n_src):
            @pl.when(hi_ref[at + p] > lo_ref[at + p])
            def _():
                two = jnp.concatenate([refs[2 * p][...], refs[2 * p + 1][...]], axis=0)
                src_row = lax.broadcasted_iota(jnp.int32, two.shape, 0)
                two = jnp.where(src_row < valid_ref[at + p], two, jnp.zeros_like(two))
                r = lax.broadcasted_iota(jnp.int32, (ROW_BLOCK, 2 * ROW_BLOCK), 0)
                c = lax.broadcasted_iota(jnp.int32, (ROW_BLOCK, 2 * ROW_BLOCK), 1)
                place = (c == r + off_ref[at + p]) & (r >= lo_ref[at + p]) & (r < hi_ref[at + p])
                acc[...] += jnp.dot(place.astype(two.dtype), two, preferred_element_type=F32)
        o_ref[...] = acc[...].astype(o_ref.dtype)

    def src_spec(p, second):
        def index(so, i, slot_r, blk_r, off_r, lo_r, hi_r, valid_r):
            at = (so * per_slot + i) * n_src + p
            return slot_r[at], jnp.minimum(blk_r[at] + second, last_block[p]), 0
        return pl.BlockSpec((None, ROW_BLOCK, cols), index)

    grid_spec = pltpu.PrefetchScalarGridSpec(
        num_scalar_prefetch=6, grid=(n_slots, per_slot),
        in_specs=[src_spec(p, second) for p in range(n_src) for second in (0, 1)],
        out_specs=pl.BlockSpec((None, ROW_BLOCK, cols), lambda so, i, *_: (so, i, 0)),
        scratch_shapes=[pltpu.VMEM((ROW_BLOCK, cols), F32)],
    )
    flat = [jnp.asarray(table[k].reshape(-1)) for k in range(6)]
    return pl.pallas_call(
        body, name=name, grid_spec=grid_spec, out_shape=jax.ShapeDtypeStruct(out_shape, sources[0].dtype),
        compiler_params=_params("parallel", "arbitrary"),
    )(*flat, *[a for a in sources for _ in (0, 1)])


def _shard_rows(g, lo, hi):
    rs = g.shape[1]
    pieces = []
    for j in range(g.shape[0]):
        a, b = max(lo, j * rs), min(hi, (j + 1) * rs)
        if a < b:
            pieces.append(g[j, a - j * rs:b - j * rs])
    return pieces


def kernel(x, norm_mix_g, w_in, b_f, gmlp_ln_g, gmlp_ln_b, w_s, b_s, attn_out_g, gmlp_out_g, w_out, norm_ffn_g, w_ff1, w_ff2, norm_final_g, loss_target, m_norm_mix_g, m_w_in, m_b_f, m_gmlp_ln_g, m_gmlp_ln_b, m_w_s, m_b_s, m_attn_out_g, m_gmlp_out_g, m_w_out, m_norm_ffn_g, m_w_ff1, m_w_ff2, m_norm_final_g, v_norm_mix_g, v_w_in, v_b_f, v_gmlp_ln_g, v_gmlp_ln_b, v_w_s, v_b_s, v_attn_out_g, v_gmlp_out_g, v_w_out, v_norm_ffn_g, v_w_ff1, v_w_ff2, v_norm_final_g):
    seq, d_model = x.shape[1], x.shape[2]
    d_attn = d_model // 2
    n_heads = d_attn // HEAD_DIM
    qkv = 3 * d_attn
    shard_cols = w_in.shape[2]
    assert N_CHIPS * shard_cols == qkv + n_heads + 2 * d_attn
    xs = x.reshape(seq, d_model)
    target = loss_target.reshape(seq, d_model)

    place = jnp.stack([2 * lax.axis_index("x") + lax.axis_index("y"), lax.axis_index("c")]).astype(jnp.int32)
    names = ["w_in", "w_out", "w_ff1", "w_ff2"]
    wt_in, mt_in, vt_in = w_in[0].T, m_w_in[0].T, v_w_in[0].T
    b_in, _ = _cast_into_slot(wt_in, place, name="cast_w_in")
    b_out, _ = _cast_into_slot(w_out[0], place, name="cast_w_out")
    b_ff1, (b_in,) = _cast_into_slot(w_ff1[0], place, name="cast_w_ff1", phases=[_gather([b_in], "ici", (0, 2))])
    b_ff2, (b_in,) = _cast_into_slot(w_ff2[0], place, name="cast_w_ff2",
                                     phases=[_merge(_gather([b_in], "ici", (2, 3)), _gather([b_in], "d2d", (0, 2)))])
    h, (b_in,) = _rmsnorm_fwd(xs, norm_mix_g, name="norm_mix",
                              phases=[_merge(_gather([b_in], "ici", (3, 4)), _gather([b_in], "d2d", (2, 3)))])
    (g_in,) = _exchange([_gather([b_in], "d2d", (3, 4))], name="allgather_w_in_tail")
    n_cols = N_CHIPS * shard_cols
    gate_slot, gate_row = divmod(qkv, shard_cols)
    assert gate_row + n_heads <= shard_cols
    pieces = []
    for j in range(N_CHIPS):
        if j == gate_slot:
            pieces += [(j, 0, gate_row), (j, gate_row + n_heads, shard_cols - gate_row - n_heads)]
        else:
            pieces.append((j, 0, shard_cols))
    fwd_segments, at = [[]], 0
    for order, (j, src_row, size) in enumerate(pieces):
        fwd_segments[0].append((at, at + size, order % 3, j, src_row))
        at += size
    wt_main = _realign_rows([g_in] * 3, fwd_segments, (1, n_cols - n_heads, d_model), name="w_in_rows")[0]
    wt_f = jnp.pad(jnp.concatenate(_shard_rows(g_in, qkv, qkv + n_heads), axis=0), ((0, LANES - n_heads), (0, 0)))
    b_f_pad = jnp.pad(b_f, ((0, 0), (0, LANES - n_heads)))
    b_col = b_s[0].T

    first, rest = (0, 1), (1, GATHER_PARTS)
    z, (b_out, b_ff1) = _matmul(h, wt_main, name="in_proj", out_dtype=BF16, trans_b=True, tm=2048,
                                phases=[_gather([b_out], "ici"), _gather([b_ff1], "ici", first)])
    zb, f_cum = _forget_fwd(h, wt_f, b_f_pad, name="forget_fwd")
    f_row = f_cum[:, :n_heads].T[:, None, :]
    (o, lse2), (b_ff1, b_out) = _attn_fwd(z, f_row, n_heads, name="attn_fwd",
                                          phases=[_gather([b_ff1], "ici", rest), _gather([b_out], "d2d")])
    merged = _mix_fwd(z, o, gmlp_ln_g, gmlp_ln_b, w_s[0], b_col, attn_out_g, gmlp_out_g, n_heads, name="mix_fwd")
    w_out_full = b_out.reshape(2 * d_attn, d_model)
    x1, (b_ff1, b_ff2) = _matmul(merged, w_out_full, name="out_proj", out_dtype=F32, residual=xs,
                                 phases=[_gather([b_ff1], "d2d"), _gather([b_ff2], "ici", first)])
    h2, _ = _rmsnorm_fwd(x1, norm_ffn_g, name="norm_ffn")
    a, (b_ff2,) = _matmul(h2, b_ff1, name="ff1", out_dtype=BF16, relu=True, b_sharded=True, tm=2048,
                          phases=[_merge(_gather([b_ff2], "both", rest), _gather([b_ff2], "d2d", first))])
    w_ff2_full = b_ff2.reshape(N_CHIPS * b_ff2.shape[1], d_model)
    x2, _ = _matmul(a, w_ff2_full, name="ff2", out_dtype=F32, square_lhs=True, residual=x1)
    dx2, dx2_b, dg_final, loss = _loss_and_final_bwd(x2, target, norm_final_g.reshape(1, d_model), name="loss_head")

    def pair_sum(g, r, nm):
        return _add_halves(g, r, place, name="grads_pair_sum_" + nm)

    def chip_sum(p, q, nm, **piece):
        return _sum_chips(p, q, place, name="grads_chip_sum_" + nm, **piece)

    dw_ff2, _ = _matmul(a, dx2_b, name="ff2_dw", out_dtype=BF16, trans_a=True, square_lhs=True)
    dw_ff2 = dw_ff2.reshape(N_CHIPS, -1, d_model)
    da, (r_ff2,) = _matmul(dx2_b, w_ff2_full, name="ff2_dlhs", out_dtype=BF16, trans_b=True, scale2_by=a, tm=2048,
                           phases=[_swap_halves([dw_ff2])])
    ps_ff2 = pair_sum(dw_ff2, r_ff2, "w_ff2")
    dh2, (q_ff2a,) = _matmul(da, b_ff1, name="ff1_dlhs", out_dtype=F32, trans_b=True, b_sharded=True,
                             phases=[_send_partials([ps_ff2], (0, 2))])
    dw_ff1, (q_ff2b,) = _matmul(h2, da, name="ff1_dw", out_dtype=BF16, trans_a=True, out_sharded=True, tk=seq,
                                phases=[_send_partials([ps_ff2], (1, 2))])
    g_ff2 = chip_sum(ps_ff2, q_ff2a, "w_ff2_a", piece=(0, 2))
    g_ff2 = chip_sum(ps_ff2, q_ff2b, "w_ff2_b", piece=(1, 2), into=g_ff2)
    (dx1, dg_ffn, dx1_b), (g_ff2,) = _rmsnorm_bwd(dh2, x1, dx2, norm_ffn_g, name="norm_ffn_bwd", rounded_copy=True,
                                                   phases=[_join_halves([g_ff2])])
    dw_out, _ = _matmul(merged, dx1_b, name="out_proj_dw", out_dtype=BF16, trans_a=True, tk=seq)
    dw_out = dw_out.reshape(N_CHIPS, -1, d_model)
    d_merged, (r_ff1, r_out) = _matmul(dx1_b, w_out_full, name="out_proj_dlhs", out_dtype=F32, trans_b=True,
                                       phases=[_swap_halves([dw_ff1, dw_out])])
    ps_ff1, ps_out = pair_sum(dw_ff1, r_ff1, "w_ff1"), pair_sum(dw_out, r_out, "w_out")
    d_o, dzu, dzv, dw_s, db_col, dlg, dlb, dag, dgg = _mix_bwd(
        z, o, d_merged, gmlp_ln_g, gmlp_ln_b, w_s[0], b_col, attn_out_g, gmlp_out_g, n_heads, name="mix_bwd")
    (dq, dk, dv, d_f_key, d_f_query), (q_ff1, q_out) = _attn_bwd(
        z, o, d_o, lse2, f_row, n_heads, name="attn_bwd", phases=[_send_partials([ps_ff1, ps_out])])
    g_ff1, g_out = chip_sum(ps_ff1, q_ff1, "w_ff1"), chip_sum(ps_out, q_out, "w_out")
    d_f = d_f_key.reshape(n_heads, seq) + d_f_query.reshape(n_heads, seq)
    d_f_pad = jnp.pad(d_f.T, ((0, 0), (0, LANES - n_heads)))
    dzf, db_f = _forget_bwd(d_f_pad, zb, name="forget_bwd")
    dz = jnp.concatenate([dq, dk, dv, dzu, dzv], axis=1)
    early_g = _pack([db_f[:, :n_heads], dlg, dlb, dw_s, db_col.T, dag, dgg, dg_ffn, dg_final])
    dwt_main, (g_ff1, g_out, early_all) = _matmul(dz, h, name="in_proj_dw", out_dtype=BF16, trans_a=True, tk=seq,
                                                  phases=[_join_halves([g_ff1, g_out]), _gather_small(early_g)])
    dwt_f, _ = _matmul(dzf, h, name="gate_dw", out_dtype=BF16, trans_a=True)
    bwd_segments = []
    for j in range(N_CHIPS):
        first = j * shard_cols
        if j < gate_slot:
            bwd_segments.append([(0, shard_cols, 0, 0, first)])
        elif j > gate_slot:
            bwd_segments.append([(0, shard_cols, 0, 0, first - n_heads)])
        else:
            bwd_segments.append([(0, gate_row, 0, 0, first), (gate_row, gate_row + n_heads, 1, 0, 0),
                                 (gate_row + n_heads, shard_cols, 2, 0, qkv)])
    dw_in = _realign_rows([dwt_main[None], dwt_f[None], dwt_main[None]], bwd_segments,
                          (N_CHIPS, shard_cols, d_model), name="dw_in_rows")
    dh_gate, (r_in,) = _matmul(dzf, wt_f, name="gate_dlhs", out_dtype=F32, phases=[_swap_halves([dw_in])])
    ps_in = pair_sum(dw_in, r_in, "w_in")
    dh, (q_in,) = _matmul(dz, wt_main, name="in_proj_dlhs", out_dtype=F32, residual=dh_gate, tk=2560,
                          phases=[_send_partials([ps_in])])
    g_in_sum = chip_sum(ps_in, q_in, "w_in")
    (grad_x, dg_mix), _ = _rmsnorm_bwd(dh, xs, dx1, norm_mix_g, name="norm_mix_bwd")
    late_g = _pack([dg_mix])
    g_in_sum, late_all = _exchange([_join_halves([g_in_sum]), _gather_small(late_g)], name="grads_join_w_in")

    big = {}
    for nm, g, w, m, v in zip(names, (g_in_sum, g_out, g_ff1, g_ff2), (wt_in, w_out[0], w_ff1[0], w_ff2[0]),
                              (mt_in, m_w_out[0], m_w_ff1[0], m_w_ff2[0]), (vt_in, v_w_out[0], v_w_ff1[0], v_w_ff2[0])):
        big[nm] = tuple((t.T if nm == "w_in" else t)[None] for t in _adamw(w, g, m, v, name="adamw_" + nm))

    small_params = dict(
        norm_mix_g=(norm_mix_g, m_norm_mix_g, v_norm_mix_g), b_f=(b_f, m_b_f, v_b_f),
        gmlp_ln_g=(gmlp_ln_g, m_gmlp_ln_g, v_gmlp_ln_g), gmlp_ln_b=(gmlp_ln_b, m_gmlp_ln_b, v_gmlp_ln_b),
        w_s=(w_s, m_w_s, v_w_s), b_s=(b_s, m_b_s, v_b_s), attn_out_g=(attn_out_g, m_attn_out_g, v_attn_out_g),
        gmlp_out_g=(gmlp_out_g, m_gmlp_out_g, v_gmlp_out_g), norm_ffn_g=(norm_ffn_g, m_norm_ffn_g, v_norm_ffn_g),
        norm_final_g=(norm_final_g, m_norm_final_g, v_norm_final_g))

    def small_step(group, grads_all, grads_own, label):
        w, m, v = ([small_params[nm][k] for nm in group] for k in range(3))
        packed = _adamw_small(grads_all, grads_own, place, _pack(w), _pack(m), _pack(v), name="adamw_small_" + label)
        parts = [_unpack(p, [a.shape for a in w]) for p in packed]
        return {nm: tuple(part[i] for part in parts) for i, nm in enumerate(group)}

    early = ["b_f", "gmlp_ln_g", "gmlp_ln_b", "w_s", "b_s", "attn_out_g", "gmlp_out_g", "norm_ffn_g", "norm_final_g"]
    small = {**small_step(early, early_all, early_g, "early"), **small_step(["norm_mix_g"], late_all, late_g, "late")}

    order = ["norm_mix_g", "w_in", "b_f", "gmlp_ln_g", "gmlp_ln_b", "w_s", "b_s", "attn_out_g", "gmlp_out_g", "w_out",
             "norm_ffn_g", "w_ff1", "w_ff2", "norm_final_g"]
    result = {**small, **big}
    total_loss = lax.psum(loss[0, 0], ("x", "y", "c"))
    outs = [total_loss, grad_x.reshape(x.shape)]
    for part in range(4):
        outs += [result[nm][part] for nm in order]
    return tuple(outs)
```

```python
import functools
import math

import numpy as np
import jax
import jax.numpy as jnp
from jax import lax
from jax.experimental import pallas as pl
from jax.experimental.pallas import tpu as pltpu

HEAD_DIM = 128
CHUNK = 128
EPS = 1e-6
LANES = 128
MXU_COLUMNS = 256
N_CHIPS = 4
N_DEV = 8
VMEM_LIMIT_BYTES = 56 * 1024 * 1024

ADAM_LR = 0.001
ADAM_B1 = 0.9
ADAM_B2 = 0.999
ADAM_EPS = 1e-08
ADAM_WD = 0.01
ADAM_STEP = 10

BF16 = jnp.bfloat16
F32 = jnp.float32
MESH = pl.DeviceIdType.MESH
ANY = pl.BlockSpec(memory_space=pl.ANY)
NEG_BIG = -1e30


def _params(*sem):
    return pltpu.CompilerParams(dimension_semantics=tuple(sem), vmem_limit_bytes=VMEM_LIMIT_BYTES)


def _tile(n, pref, unit):
    t = (min(pref, n) // unit) * unit
    while t >= unit:
        if n % t == 0:
            return t
        t -= unit
    return n


def _rc_tile(rows, cols, pref_rows=256, pref_cols=256):
    if rows % 16 == 0:
        return _tile(rows, pref_rows, 16), cols
    return rows, _tile(cols, pref_cols, LANES)


class _Phase:
    def __init__(self, arrays, out_shapes, in_place, n_sems, start, finish):
        self.arrays, self.out_shapes, self.in_place = list(arrays), list(out_shapes), in_place
        self.n_sems, self.start, self.finish = n_sems, start, finish

    @property
    def n_out(self):
        return len(self.arrays) if self.in_place else len(self.out_shapes)


def _run_phases(phases, steps, comm_in, comm_out, send_sems, recv_sems):
    at_in = at_out = at_sem = 0
    for ph in phases:
        for step in steps:
            getattr(ph, step)(comm_in[at_in:at_in + len(ph.arrays)], comm_out[at_out:at_out + ph.n_out],
                              lambda i, base=at_sem: send_sems.at[base + i], lambda i, base=at_sem: recv_sems.at[base + i])
        at_in, at_out, at_sem = at_in + len(ph.arrays), at_out + ph.n_out, at_sem + ph.n_sems


def _call(body, *, name, grid, in_specs, out_specs, out_shape, operands, semantics, scratch_shapes=(),
          n_prefetch=0, phases=()):
    in_specs, out_specs, out_shape = list(in_specs), list(out_specs), list(out_shape)
    scratch_shapes = list(scratch_shapes)
    n_in, n_out, n_scr = len(operands) - n_prefetch, len(out_shape), len(scratch_shapes)
    comm_in = [a for ph in phases for a in ph.arrays]
    comm_out = [jax.ShapeDtypeStruct(s.shape, s.dtype) for ph in phases
                for s in (ph.arrays if ph.in_place else ph.out_shapes)]
    aliases, at_in, at_out = {}, n_prefetch + n_in, n_out
    for ph in phases:
        if ph.in_place:
            aliases.update({at_in + r: at_out + r for r in range(len(ph.arrays))})
        at_in, at_out = at_in + len(ph.arrays), at_out + ph.n_out
    n_sems = sum(ph.n_sems for ph in phases)

    def hosted(*refs):
        pre, rest = refs[:n_prefetch], refs[n_prefetch:]
        ins, rest = rest[:n_in], rest[n_in:]
        cin, rest = rest[:len(comm_in)], rest[len(comm_in):]
        outs, rest = rest[:n_out], rest[n_out:]
        cout, rest = rest[:len(comm_out)], rest[len(comm_out):]
        scr = rest[:n_scr]
        if phases:
            send_sems, recv_sems = rest[n_scr:]
            ids = [pl.program_id(ax) for ax in range(len(grid))]
            first = functools.reduce(jnp.logical_and, [i == 0 for i in ids])
            last = functools.reduce(jnp.logical_and, [i == g - 1 for i, g in zip(ids, grid)])

            @pl.when(first)
            def _():
                _run_phases(phases, ("start",), cin, cout, send_sems, recv_sems)

        body(*pre, *ins, *outs, *scr)
        if phases:
            @pl.when(last)
            def _():
                _run_phases(phases, ("finish",), cin, cout, send_sems, recv_sems)

    all_in = in_specs + [ANY] * len(comm_in)
    all_out = out_specs + [ANY] * len(comm_out)
    all_scr = scratch_shapes + ([pltpu.SemaphoreType.DMA((n_sems,)), pltpu.SemaphoreType.DMA((n_sems,))] if phases else [])
    if phases:
        semantics = ("arbitrary",) * len(grid)
    kwargs = dict(name=name, out_shape=tuple(out_shape + comm_out), compiler_params=_params(*semantics),
                  input_output_aliases=aliases)
    if n_prefetch:
        kwargs["grid_spec"] = pltpu.PrefetchScalarGridSpec(
            num_scalar_prefetch=n_prefetch, grid=grid, in_specs=all_in, out_specs=tuple(all_out), scratch_shapes=all_scr)
    else:
        kwargs.update(grid=grid, in_specs=all_in, out_specs=tuple(all_out), scratch_shapes=all_scr)
    res = pl.pallas_call(hosted, **kwargs)(*operands, *comm_in)
    return tuple(res[:n_out]), tuple(res[n_out:])


def _only(results):
    outs, comm = results
    return outs[0] if len(outs) == 1 else outs, comm


def _matmul(a, b, *, name, out_dtype, trans_a=False, trans_b=False, tm=1024, tn=1024, tk=2048,
            square_lhs=False, relu=False, residual=None, scale2_by=None,
            b_sharded=False, out_sharded=False, phases=()):
    m, k = (a.shape[1], a.shape[0]) if trans_a else a.shape
    if b_sharded:
        if trans_b:
            n, ks = b.shape[1], b.shape[2]
            assert N_CHIPS * ks == k
        else:
            ns = b.shape[2]
            n = N_CHIPS * ns
            assert b.shape[1] == k
    else:
        n = b.shape[0] if trans_b else b.shape[1]
        assert (b.shape[1] if trans_b else b.shape[0]) == k
    tm = _tile(m, tm, 128)
    tn = _tile(n // N_CHIPS if (out_sharded or (b_sharded and not trans_b)) else n, tn, 128)
    tk = _tile(k // N_CHIPS if (b_sharded and trans_b) else k, tk, 128)
    nk = k // tk

    if trans_a:
        a_spec = pl.BlockSpec((tk, tm), lambda i, j, kk: (kk, i))
    else:
        a_spec = pl.BlockSpec((tm, tk), lambda i, j, kk: (i, kk))
    if b_sharded and trans_b:
        per = ks // tk
        assert per * tk == ks
        b_spec = pl.BlockSpec((None, tn, tk), lambda i, j, kk: (kk // per, j, kk % per))
    elif b_sharded:
        per = ns // tn
        assert per * tn == ns
        b_spec = pl.BlockSpec((None, tk, tn), lambda i, j, kk: (j // per, kk, j % per))
    elif trans_b:
        b_spec = pl.BlockSpec((tn, tk), lambda i, j, kk: (j, kk))
    else:
        b_spec = pl.BlockSpec((tk, tn), lambda i, j, kk: (kk, j))
    if out_sharded:
        ns_out = n // N_CHIPS
        per_o = ns_out // tn
        assert per_o * tn == ns_out
        out_shape = jax.ShapeDtypeStruct((N_CHIPS, m, ns_out), out_dtype)
        o_spec = pl.BlockSpec((None, tm, tn), lambda i, j, kk: (j // per_o, i, j % per_o))
    else:
        out_shape = jax.ShapeDtypeStruct((m, n), out_dtype)
        o_spec = pl.BlockSpec((tm, tn), lambda i, j, kk: (i, j))
    mn_spec = pl.BlockSpec((tm, tn), lambda i, j, kk: (i, j))

    operands, in_specs = [a, b], [a_spec, b_spec]
    if scale2_by is not None:
        operands.append(scale2_by)
        in_specs.append(mn_spec)
    if residual is not None:
        operands.append(residual)
        in_specs.append(mn_spec)
    dims = (((0 if trans_a else 1,), (1 if trans_b else 0,)), ((), ()))
    chunk = MXU_COLUMNS if tn % MXU_COLUMNS == 0 else tn

    def body(*refs):
        a_ref, b_ref = refs[0], refs[1]
        pos = 2
        scale_ref = res_ref = None
        if scale2_by is not None:
            scale_ref = refs[pos]
            pos += 1
        if residual is not None:
            res_ref = refs[pos]
            pos += 1
        o_ref = refs[pos]
        kk = pl.program_id(2)

        av = a_ref[...]
        if square_lhs:
            av = av.astype(F32)
            av = av * av
        av = av.astype(BF16)

        def finish(r, cols):
            if relu:
                r = jnp.maximum(r, 0.0)
            if scale_ref is not None:
                r = r * (2.0 * scale_ref[:, cols].astype(F32))
            if res_ref is not None:
                r = r + res_ref[:, cols].astype(F32)
            o_ref[:, cols] = r.astype(out_dtype)

        if nk == 1:
            for lo in range(0, tn, chunk):
                cols = slice(lo, lo + chunk)
                bv = (b_ref[cols, :] if trans_b else b_ref[:, cols]).astype(BF16)
                finish(lax.dot_general(av, bv, dims, preferred_element_type=F32), cols)
        else:
            acc_ref = refs[pos + 1]
            part = lax.dot_general(av, b_ref[...].astype(BF16), dims, preferred_element_type=F32)

            @pl.when(kk == 0)
            def _():
                acc_ref[...] = part

            @pl.when(jnp.logical_and(kk > 0, kk < nk - 1))
            def _():
                acc_ref[...] += part

            @pl.when(kk == nk - 1)
            def _():
                finish(acc_ref[...] + part, slice(None))

    return _only(_call(
        body, name=name, out_shape=[out_shape], grid=(m // tm, n // tn, nk),
        in_specs=in_specs, out_specs=[o_spec], operands=operands,
        scratch_shapes=[pltpu.VMEM((tm, tn), F32)] if nk > 1 else [],
        semantics=("parallel", "parallel", "arbitrary"), phases=phases))


def _rmsnorm_fwd(x, g, *, name, tr=512, phases=()):
    s, d = x.shape
    tr = _tile(s, tr, 8)

    def body(x_ref, g_ref, o_ref):
        xv = x_ref[...]
        r = lax.rsqrt(jnp.mean(xv * xv, axis=-1, keepdims=True) + EPS)
        o_ref[...] = ((xv * r) * g_ref[...]).astype(BF16)

    return _only(_call(
        body, name=name, out_shape=[jax.ShapeDtypeStruct((s, d), BF16)], grid=(s // tr,),
        in_specs=[pl.BlockSpec((tr, d), lambda i: (i, 0)), pl.BlockSpec((1, d), lambda i: (0, 0))],
        out_specs=[pl.BlockSpec((tr, d), lambda i: (i, 0))], operands=[x, g],
        semantics=("parallel",), phases=phases))


def _rms_bwd_rows(dy, xv, g):
    d = xv.shape[-1]
    r = lax.rsqrt(jnp.mean(xv * xv, axis=-1, keepdims=True) + EPS)
    gdy = dy * g
    dot = jnp.sum(gdy * xv, axis=-1, keepdims=True)
    dx = gdy * r - xv * (r * r * r) * (dot / d)
    return dx, dy * (xv * r)


def _rmsnorm_bwd(dy, x, res, g, *, name, tr=256, rounded_copy=False, phases=()):
    s, d = x.shape
    tr = _tile(s, tr, 8)

    def body(dy_ref, x_ref, res_ref, g_ref, dx_ref, dg_ref, *dxb_ref):
        @pl.when(pl.program_id(0) == 0)
        def _():
            dg_ref[...] = jnp.zeros_like(dg_ref)

        dx, dg_rows = _rms_bwd_rows(dy_ref[...].astype(F32), x_ref[...], g_ref[...])
        out = res_ref[...] + dx
        dx_ref[...] = out
        if rounded_copy:
            dxb_ref[0][...] = out.astype(BF16)
        dg_ref[...] += jnp.sum(dg_rows, axis=0, keepdims=True)

    row = pl.BlockSpec((tr, d), lambda i: (i, 0))
    vec = pl.BlockSpec((1, d), lambda i: (0, 0))
    extra = [jax.ShapeDtypeStruct((s, d), BF16)] if rounded_copy else []
    return _call(
        body, name=name,
        out_shape=[jax.ShapeDtypeStruct((s, d), F32), jax.ShapeDtypeStruct((1, d), F32)] + extra,
        grid=(s // tr,), in_specs=[row, row, row, vec], out_specs=[row, vec] + [row] * len(extra),
        operands=[dy, x, res, g], semantics=("arbitrary",), phases=phases)


def _loss_and_final_bwd(x2, target, g, *, name, tr=256):
    s, d = x2.shape
    tr = _tile(s, tr, 8)

    def body(x_ref, t_ref, g_ref, dx_ref, dxb_ref, dg_ref, loss_ref):
        @pl.when(pl.program_id(0) == 0)
        def _():
            dg_ref[...] = jnp.zeros_like(dg_ref)
            loss_ref[...] = jnp.zeros_like(loss_ref)

        xv, gv = x_ref[...], g_ref[...]
        r = lax.rsqrt(jnp.mean(xv * xv, axis=-1, keepdims=True) + EPS)
        err = (xv * r) * gv - t_ref[...]
        row_loss = jnp.mean(err * err, axis=-1, keepdims=True)
        loss_ref[...] += 0.5 * jnp.sum(row_loss, axis=0, keepdims=True)
        dx, dg_rows = _rms_bwd_rows(err / d, xv, gv)
        dx_ref[...] = dx
        dxb_ref[...] = dx.astype(BF16)
        dg_ref[...] += jnp.sum(dg_rows, axis=0, keepdims=True)

    row = pl.BlockSpec((tr, d), lambda i: (i, 0))
    vec = pl.BlockSpec((1, d), lambda i: (0, 0))
    one = pl.BlockSpec((1, 1), lambda i: (0, 0))
    return pl.pallas_call(
        body, name=name,
        out_shape=(jax.ShapeDtypeStruct((s, d), F32), jax.ShapeDtypeStruct((s, d), BF16),
                   jax.ShapeDtypeStruct((1, d), F32), jax.ShapeDtypeStruct((1, 1), F32)),
        grid=(s // tr,), in_specs=[row, row, vec], out_specs=(row, row, vec, one),
        compiler_params=_params("arbitrary"),
    )(x2, target, g)


def _tri_ones(n, lower):
    r = lax.broadcasted_iota(jnp.int32, (n, n), 0)
    c = lax.broadcasted_iota(jnp.int32, (n, n), 1)
    return jnp.where((c <= r) if lower else (c >= r), 1.0, 0.0).astype(F32)


def _forget_fwd(h, w_f, b_f, *, name, tr=256):
    s, d = h.shape
    tr = _tile(s, tr, 8)

    def body(h_ref, w_ref, b_ref, zb_ref, f_ref, carry):
        @pl.when(pl.program_id(0) == 0)
        def _():
            carry[...] = jnp.zeros_like(carry)

        zb = lax.dot_general(h_ref[...], w_ref[...], (((1,), (1,)), ((), ())), preferred_element_type=F32) + b_ref[...]
        zb_ref[...] = zb
        log_f = jnp.minimum(zb, 0.0) - jnp.log(1.0 + jnp.exp(-jnp.abs(zb)))
        run = jnp.dot(_tri_ones(tr, True), log_f, preferred_element_type=F32,
                      precision=lax.Precision.HIGHEST) + carry[...]
        f_ref[...] = run
        carry[...] = run[tr - 1:tr, :]

    row = pl.BlockSpec((tr, LANES), lambda i: (i, 0))
    return pl.pallas_call(
        body, name=name,
        out_shape=(jax.ShapeDtypeStruct((s, LANES), F32), jax.ShapeDtypeStruct((s, LANES), F32)),
        grid=(s // tr,),
        in_specs=[pl.BlockSpec((tr, d), lambda i: (i, 0)), pl.BlockSpec((LANES, d), lambda i: (0, 0)),
                  pl.BlockSpec((1, LANES), lambda i: (0, 0))],
        out_specs=(row, row), scratch_shapes=[pltpu.VMEM((1, LANES), F32)],
        compiler_params=_params("arbitrary"),
    )(h, w_f, b_f)


def _forget_bwd(d_f, zb, *, name, tr=256):
    s = zb.shape[0]
    tr = _tile(s, tr, 8)
    nb = s // tr

    def body(df_ref, zb_ref, dz_ref, db_ref, carry):
        @pl.when(pl.program_id(0) == 0)
        def _():
            carry[...] = jnp.zeros_like(carry)
            db_ref[...] = jnp.zeros_like(db_ref)

        run = jnp.dot(_tri_ones(tr, False), df_ref[...], preferred_element_type=F32,
                      precision=lax.Precision.HIGHEST) + carry[...]
        carry[...] = run[0:1, :]
        dz = run / (1.0 + jnp.exp(zb_ref[...]))
        dz_ref[...] = dz.astype(BF16)
        db_ref[...] += jnp.sum(dz, axis=0, keepdims=True)

    row = pl.BlockSpec((tr, LANES), lambda i: (nb - 1 - i, 0))
    return pl.pallas_call(
        body, name=name,
        out_shape=(jax.ShapeDtypeStruct((s, LANES), BF16), jax.ShapeDtypeStruct((1, LANES), F32)),
        grid=(nb,), in_specs=[row, row], out_specs=(row, pl.BlockSpec((1, LANES), lambda i: (0, 0))),
        scratch_shapes=[pltpu.VMEM((1, LANES), F32)],
        compiler_params=_params("arbitrary"),
    )(d_f, zb)


def _pairs(nblk, by_kv):
    if by_kv:
        pr = [(i, j) for j in range(nblk) for i in range(j, nblk)]
    else:
        pr = [(i, j) for i in range(nblk) for j in range(i + 1)]
    return (jnp.asarray(np.array([p[0] for p in pr], np.int32)), jnp.asarray(np.array([p[1] for p in pr], np.int32)))


def _causal_mask(rows, keys):
    r = lax.broadcasted_iota(jnp.int32, (rows[1] - rows[0], keys[1] - keys[0]), 0) + rows[0]
    c = lax.broadcasted_iota(jnp.int32, (rows[1] - rows[0], keys[1] - keys[0]), 1) + keys[0]
    return c <= r


def _diagonal_pieces(tb):
    half = tb // 2
    if half % LANES:
        return [((0, tb), (0, tb))]
    return [((0, half), (0, half)), ((half, tb), (0, tb))]


LOG2E = math.log2(math.e)
QK_TO_LOG2 = LOG2E / math.sqrt(HEAD_DIM)


def _attn_logits2(q, k, fk_row):
    sc = lax.dot_general(q, k, (((1,), (1,)), ((), ())), preferred_element_type=F32)
    return sc * QK_TO_LOG2 - fk_row * LOG2E


def _attn_fwd(z, f_row, n_heads, *, name, tb=1024, phases=()):
    s = z.shape[0]
    tb = _tile(s, tb, 128)
    nblk = s // tb
    qi, kj = _pairs(nblk, by_kv=False)

    def body(qi_ref, kj_ref, q_ref, k_ref, v_ref, fk_ref, o_ref, lse_ref, m_sc, l_sc, acc_sc):
        p = pl.program_id(1)
        i, j = qi_ref[p], kj_ref[p]

        @pl.when(j == 0)
        def _():
            m_sc[...] = jnp.full_like(m_sc, NEG_BIG)
            l_sc[...] = jnp.zeros_like(l_sc)
            acc_sc[...] = jnp.zeros_like(acc_sc)

        def update(rows, keys, masked):
            rs, ks = slice(*rows), slice(*keys)
            s2 = _attn_logits2(q_ref[rs, :], k_ref[ks, :], fk_ref[:, ks])
            if masked:
                s2 = jnp.where(_causal_mask(rows, keys), s2, NEG_BIG)
            m_old = m_sc[rs, :]
            m_new = jnp.maximum(m_old, jnp.max(s2, axis=-1, keepdims=True))
            alpha = jnp.exp2(m_old - m_new)
            pv = jnp.exp2(s2 - jnp.tile(m_new, (1, (keys[1] - keys[0]) // LANES)))
            l_sc[rs, :] = alpha * l_sc[rs, :] + jnp.sum(pv, axis=-1, keepdims=True)
            acc_sc[rs, :] = alpha * acc_sc[rs, :] + jnp.dot(pv.astype(BF16), v_ref[ks, :], preferred_element_type=F32)
            m_sc[rs, :] = m_new

        @pl.when(j < i)
        def _():
            update((0, tb), (0, tb), False)

        @pl.when(j == i)
        def _():
            for rows, keys in _diagonal_pieces(tb):
                update(rows, keys, True)
            o_ref[...] = (acc_sc[...] / l_sc[...]).astype(BF16)
            lse_ref[...] = m_sc[...] + jnp.log2(l_sc[...])

    h = n_heads
    return _call(
        body, name=name, n_prefetch=2, grid=(h, int(qi.shape[0])),
        in_specs=[
            pl.BlockSpec((tb, HEAD_DIM), lambda hh, p, qi_r, kj_r: (qi_r[p], hh)),
            pl.BlockSpec((tb, HEAD_DIM), lambda hh, p, qi_r, kj_r: (kj_r[p], h + hh)),
            pl.BlockSpec((tb, HEAD_DIM), lambda hh, p, qi_r, kj_r: (kj_r[p], 2 * h + hh)),
            pl.BlockSpec((None, 1, tb), lambda hh, p, qi_r, kj_r: (hh, 0, kj_r[p])),
        ],
        out_specs=[
            pl.BlockSpec((tb, HEAD_DIM), lambda hh, p, qi_r, kj_r: (qi_r[p], hh)),
            pl.BlockSpec((None, tb, LANES), lambda hh, p, qi_r, kj_r: (hh, qi_r[p], 0)),
        ],
        scratch_shapes=[pltpu.VMEM((tb, LANES), F32), pltpu.VMEM((tb, LANES), F32), pltpu.VMEM((tb, HEAD_DIM), F32)],
        out_shape=[jax.ShapeDtypeStruct((s, h * HEAD_DIM), BF16), jax.ShapeDtypeStruct((h, s, LANES), F32)],
        operands=[qi, kj, z, z, z, f_row], semantics=("parallel", "arbitrary"), phases=phases)


def _attn_bwd(z, o, d_o, lse2, f_row, n_heads, *, name, tb=1024, phases=()):
    s = z.shape[0]
    tb = _tile(s, tb, 128)
    nblk = s // tb
    qi, kj = _pairs(nblk, by_kv=True)
    n_pairs = int(qi.shape[0])
    scale = 1.0 / math.sqrt(HEAD_DIM)
    h = n_heads

    def body(qi_ref, kj_ref, q_ref, k_ref, v_ref, o_ref, do_ref, lse_ref, fk_ref,
             dq_ref, dk_ref, dv_ref, df_ref, dfq_ref, dq_sc, dk_sc, dv_sc, df_sc, dfq_sc):
        p = pl.program_id(1)
        i, j = qi_ref[p], kj_ref[p]

        @pl.when(p == 0)
        def _():
            dq_sc[...] = jnp.zeros_like(dq_sc)
            dfq_sc[...] = jnp.zeros_like(dfq_sc)

        @pl.when(i == j)
        def _():
            dk_sc[...] = jnp.zeros_like(dk_sc)
            dv_sc[...] = jnp.zeros_like(dv_sc)
            df_sc[...] = jnp.zeros_like(df_sc)

        def update(rows, keys, masked):
            rs, ks, n_rows = slice(*rows), slice(*keys), rows[1] - rows[0]
            q, k, v, do = q_ref[rs, :], k_ref[ks, :], v_ref[ks, :], do_ref[rs, :]
            delta = jnp.sum(do.astype(F32) * o_ref[rs, :].astype(F32), axis=-1, keepdims=True)
            pv = jnp.exp2(_attn_logits2(q, k, fk_ref[:, ks]) - jnp.tile(lse_ref[rs, :], (1, (keys[1] - keys[0]) // LANES)))
            if masked:
                pv = jnp.where(_causal_mask(rows, keys), pv, 0.0)
            dp = lax.dot_general(do, v, (((1,), (1,)), ((), ())), preferred_element_type=F32)
            ds = pv * (dp - delta)
            ds_b = ds.astype(BF16)
            dv_sc[ks, :] += lax.dot_general(pv.astype(BF16), do, (((0,), (0,)), ((), ())), preferred_element_type=F32)
            dk_sc[ks, :] += lax.dot_general(ds_b, q, (((0,), (0,)), ((), ())), preferred_element_type=F32)
            at = pl.ds(pl.multiple_of(i * tb + rows[0], LANES), n_rows)
            dq_sc[at, :] += jnp.dot(ds_b, k, preferred_element_type=F32)
            df_sc[:, ks] -= jnp.sum(ds, axis=0, keepdims=True)
            dfq_sc[at, :] += jnp.broadcast_to(jnp.sum(ds, axis=1, keepdims=True), (n_rows, LANES))

        @pl.when(i > j)
        def _():
            update((0, tb), (0, tb), False)

        @pl.when(i == j)
        def _():
            for rows, keys in _diagonal_pieces(tb):
                update(rows, keys, True)

        @pl.when(i == nblk - 1)
        def _():
            dk_ref[...] = (dk_sc[...] * scale).astype(BF16)
            dv_ref[...] = dv_sc[...].astype(BF16)
            df_ref[...] = df_sc[...]

        @pl.when(p == n_pairs - 1)
        def _():
            dq_ref[...] = (dq_sc[...] * scale).astype(BF16)
            dfq_ref[...] = jnp.transpose(dfq_sc[...])[0:1, :]

    qblk = lambda off: pl.BlockSpec((tb, HEAD_DIM), lambda hh, p, qi_r, kj_r: (qi_r[p], off + hh))
    kblk = lambda off: pl.BlockSpec((tb, HEAD_DIM), lambda hh, p, qi_r, kj_r: (kj_r[p], off + hh))
    qrep = pl.BlockSpec((None, tb, LANES), lambda hh, p, qi_r, kj_r: (hh, qi_r[p], 0))
    krow = pl.BlockSpec((None, 1, tb), lambda hh, p, qi_r, kj_r: (hh, 0, kj_r[p]))
    act = jax.ShapeDtypeStruct((s, h * HEAD_DIM), BF16)
    return _call(
        body, name=name, n_prefetch=2, grid=(h, n_pairs),
        in_specs=[qblk(0), kblk(h), kblk(2 * h), qblk(0), qblk(0), qrep, krow],
        out_specs=[
            pl.BlockSpec((s, HEAD_DIM), lambda hh, p, qi_r, kj_r: (0, hh)),
            kblk(0), kblk(0), krow,
            pl.BlockSpec((None, 1, s), lambda hh, p, qi_r, kj_r: (hh, 0, 0)),
        ],
        scratch_shapes=[pltpu.VMEM((s, HEAD_DIM), F32), pltpu.VMEM((tb, HEAD_DIM), F32),
                        pltpu.VMEM((tb, HEAD_DIM), F32), pltpu.VMEM((1, tb), F32), pltpu.VMEM((s, LANES), F32)],
        out_shape=[act, act, act, jax.ShapeDtypeStruct((h, 1, s), F32), jax.ShapeDtypeStruct((h, 1, s), F32)],
        operands=[qi, kj, z, z, z, o, d_o, lse2, f_row], semantics=("parallel", "arbitrary"), phases=phases)


GELU_C = math.sqrt(2.0 / math.pi)
GELU_A = 0.044715


def _gelu(x):
    return 0.5 * x * (1.0 + jnp.tanh(GELU_C * (x + GELU_A * (x * x * x))))


def _gelu_and_grad(x):
    t = jnp.tanh(GELU_C * (x + GELU_A * (x * x * x)))
    y = 0.5 * x * (1.0 + t)
    dy = 0.5 * (1.0 + t) + 0.5 * x * (1.0 - t * t) * (GELU_C * (1.0 + 3.0 * GELU_A * (x * x)))
    return y, dy


def _layernorm_parts(g):
    mu = jnp.mean(g, axis=-1, keepdims=True)
    xc = g - mu
    rs = lax.rsqrt(jnp.mean(xc * xc, axis=-1, keepdims=True) + EPS)
    return xc * rs, rs


def _spatial_mix(w_ref, bcol_ref, vv_b, n_heads, n_chunks):
    tril = _causal_mask((0, CHUNK), (0, CHUNK))
    cols = []
    for hh in range(n_heads):
        wc = jnp.where(tril, w_ref[hh], 0.0).astype(BF16)
        lanes = slice(hh * HEAD_DIM, (hh + 1) * HEAD_DIM)
        rows = [jnp.dot(wc, vv_b[c * CHUNK:(c + 1) * CHUNK, lanes], preferred_element_type=F32)
                + bcol_ref[:, hh:hh + 1] for c in range(n_chunks)]
        cols.append(jnp.concatenate(rows, axis=0))
    return jnp.concatenate(cols, axis=1)


def _mix_fwd(z, o, ln_g, ln_b, w_s, b_col, attn_g, gm_g, n_heads, *, name, tr=256):
    s = z.shape[0]
    dg = n_heads * HEAD_DIM
    tr = _tile(s, tr, CHUNK)
    n_chunks = tr // CHUNK

    def body(zu_ref, zv_ref, o_ref, lg_ref, lb_ref, w_ref, bcol_ref, ag_ref, gg_ref, out_ref):
        u = _gelu(zu_ref[...].astype(F32))
        xhat, _ = _layernorm_parts(_gelu(zv_ref[...].astype(F32)))
        vv = xhat * lg_ref[...] + lb_ref[...]
        gm = u * _spatial_mix(w_ref, bcol_ref, vv.astype(BF16), n_heads, n_chunks)
        rg = lax.rsqrt(jnp.mean(gm * gm, axis=-1, keepdims=True) + EPS)
        ov = o_ref[...].astype(F32)
        ra = lax.rsqrt(jnp.mean(ov * ov, axis=-1, keepdims=True) + EPS)
        out_ref[:, :dg] = ((ov * ra) * ag_ref[...]).astype(BF16)
        out_ref[:, dg:] = ((gm * rg) * gg_ref[...]).astype(BF16)

    vec = pl.BlockSpec((1, dg), lambda i: (0, 0))
    return pl.pallas_call(
        body, name=name, out_shape=jax.ShapeDtypeStruct((s, 2 * dg), BF16), grid=(s // tr,),
        in_specs=[pl.BlockSpec((tr, dg), lambda i: (i, 3)), pl.BlockSpec((tr, dg), lambda i: (i, 4)),
                  pl.BlockSpec((tr, dg), lambda i: (i, 0)), vec, vec,
                  pl.BlockSpec((n_heads, CHUNK, CHUNK), lambda i: (0, 0, 0)),
                  pl.BlockSpec((CHUNK, n_heads), lambda i: (0, 0)), vec, vec],
        out_specs=pl.BlockSpec((tr, 2 * dg), lambda i: (i, 0)),
        compiler_params=_params("parallel"),
    )(z, z, o, ln_g, ln_b, w_s, b_col, attn_g, gm_g)


def _mix_bwd(z, o, d_merged, ln_g, ln_b, w_s, b_col, attn_g, gm_g, n_heads, *, name, tr=256):
    s = z.shape[0]
    dg = n_heads * HEAD_DIM
    tr = _tile(s, tr, CHUNK)
    n_chunks = tr // CHUNK

    def body(zu_ref, zv_ref, o_ref, dm_ref, lg_ref, lb_ref, w_ref, bcol_ref, ag_ref, gg_ref,
             do_ref, dzu_ref, dzv_ref, dw_ref, dbcol_ref, dlg_ref, dlb_ref, dag_ref, dgg_ref):
        @pl.when(pl.program_id(0) == 0)
        def _():
            for ref in (dw_ref, dbcol_ref, dlg_ref, dlb_ref, dag_ref, dgg_ref):
                ref[...] = jnp.zeros_like(ref)

        d_o, dag_rows = _rms_bwd_rows(dm_ref[:, :dg], o_ref[...].astype(F32), ag_ref[...])
        do_ref[...] = d_o.astype(BF16)
        dag_ref[...] += jnp.sum(dag_rows, axis=0, keepdims=True)

        u, du_dz = _gelu_and_grad(zu_ref[...].astype(F32))
        gv, dgv_dz = _gelu_and_grad(zv_ref[...].astype(F32))
        xhat, rs = _layernorm_parts(gv)
        lg = lg_ref[...]
        vv_b = (xhat * lg + lb_ref[...]).astype(BF16)
        mix = _spatial_mix(w_ref, bcol_ref, vv_b, n_heads, n_chunks)
        gm = u * mix
        d_gm, dgg_rows = _rms_bwd_rows(dm_ref[:, dg:], gm, gg_ref[...])
        dgg_ref[...] += jnp.sum(dgg_rows, axis=0, keepdims=True)
        dzu_ref[...] = ((d_gm * mix) * du_dz).astype(BF16)
        d_mix = d_gm * u
        d_mix_b = d_mix.astype(BF16)

        tril = _causal_mask((0, CHUNK), (0, CHUNK))
        lane = lax.broadcasted_iota(jnp.int32, (CHUNK, n_heads), 1)
        cols = []
        db = jnp.zeros((CHUNK, n_heads), F32)
        for hh in range(n_heads):
            wc = jnp.where(tril, w_ref[hh], 0.0).astype(BF16)
            lanes = slice(hh * HEAD_DIM, (hh + 1) * HEAD_DIM)
            dw = jnp.zeros((CHUNK, CHUNK), F32)
            dmix_sum = jnp.zeros((CHUNK, HEAD_DIM), F32)
            rows = []
            for c in range(n_chunks):
                rws = slice(c * CHUNK, (c + 1) * CHUNK)
                dmb = d_mix_b[rws, lanes]
                dw += lax.dot_general(dmb, vv_b[rws, lanes], (((1,), (1,)), ((), ())), preferred_element_type=F32)
                rows.append(lax.dot_general(wc, dmb, (((0,), (0,)), ((), ())), preferred_element_type=F32))
                dmix_sum += d_mix[rws, lanes]
            dw_ref[hh] += jnp.where(tril, dw, 0.0)
            db += jnp.where(lane == hh, jnp.sum(dmix_sum, axis=-1, keepdims=True), 0.0)
            cols.append(jnp.concatenate(rows, axis=0))
        dbcol_ref[...] += db
        d_vv = jnp.concatenate(cols, axis=1)

        dlg_ref[...] += jnp.sum(d_vv * xhat, axis=0, keepdims=True)
        dlb_ref[...] += jnp.sum(d_vv, axis=0, keepdims=True)
        d_xhat = d_vv * lg
        d_gv = rs * (d_xhat - jnp.mean(d_xhat, axis=-1, keepdims=True)
                     - xhat * jnp.mean(d_xhat * xhat, axis=-1, keepdims=True))
        dzv_ref[...] = (d_gv * dgv_dz).astype(BF16)

    vec = pl.BlockSpec((1, dg), lambda i: (0, 0))
    wspec = pl.BlockSpec((n_heads, CHUNK, CHUNK), lambda i: (0, 0, 0))
    bspec = pl.BlockSpec((CHUNK, n_heads), lambda i: (0, 0))
    rowb = pl.BlockSpec((tr, dg), lambda i: (i, 0))
    act = jax.ShapeDtypeStruct((s, dg), BF16)
    vshape = jax.ShapeDtypeStruct((1, dg), F32)
    return pl.pallas_call(
        body, name=name,
        out_shape=(act, act, act, jax.ShapeDtypeStruct((n_heads, CHUNK, CHUNK), F32),
                   jax.ShapeDtypeStruct((CHUNK, n_heads), F32), vshape, vshape, vshape, vshape),
        grid=(s // tr,),
        in_specs=[pl.BlockSpec((tr, dg), lambda i: (i, 3)), pl.BlockSpec((tr, dg), lambda i: (i, 4)),
                  rowb, pl.BlockSpec((tr, 2 * dg), lambda i: (i, 0)), vec, vec, wspec, bspec, vec, vec],
        out_specs=(rowb, rowb, rowb, wspec, bspec, vec, vec, vec, vec),
        compiler_params=_params("arbitrary"),
    )(z, z, o, d_merged, ln_g, ln_b, w_s, b_col, attn_g, gm_g)


def _place():
    x, y, c = lax.axis_index("x"), lax.axis_index("y"), lax.axis_index("c")
    other_chips = [(1 - x, y), (x, 1 - y), (1 - x, 1 - y)]
    return x, y, c, other_chips


def _remote(src, dst, send_sem, recv_sem, to):
    return pltpu.make_async_remote_copy(src_ref=src, dst_ref=dst, send_sem=send_sem, recv_sem=recv_sem,
                                        device_id=to, device_id_type=MESH)


def _cast_into_slot(w, place, *, name, phases=()):
    rows, cols = w.shape
    tr, tc = _rc_tile(rows, cols)

    def body(place_ref, w_ref, o_ref):
        o_ref[...] = w_ref[...].astype(BF16)

    return _only(_call(
        body, name=name, n_prefetch=1, grid=(rows // tr, cols // tc),
        in_specs=[pl.BlockSpec((tr, tc), lambda i, j, pr: (i, j))],
        out_specs=[pl.BlockSpec((None, tr, tc), lambda i, j, pr: (pr[0], i, j))],
        out_shape=[jax.ShapeDtypeStruct((N_CHIPS, rows, cols), BF16)], operands=[place, w],
        semantics=("parallel", "parallel"), phases=phases))


def _exchange(phases, *, name):
    comm_in = [a for ph in phases for a in ph.arrays]
    comm_out = [jax.ShapeDtypeStruct(s.shape, s.dtype) for ph in phases for s in (ph.arrays if ph.in_place else ph.out_shapes)]
    aliases, at_in, at_out = {}, 0, 0
    for ph in phases:
        if ph.in_place:
            aliases.update({at_in + r: at_out + r for r in range(len(ph.arrays))})
        at_in, at_out = at_in + len(ph.arrays), at_out + ph.n_out
    n_sems = sum(ph.n_sems for ph in phases)

    def body(*refs):
        cin, cout = refs[:len(comm_in)], refs[len(comm_in):len(comm_in) + len(comm_out)]
        send_sems, recv_sems = refs[len(comm_in) + len(comm_out):]
        _run_phases(phases, ("start", "finish"), cin, cout, send_sems, recv_sems)

    return pl.pallas_call(
        body, name=name, out_shape=tuple(comm_out), in_specs=[ANY] * len(comm_in), out_specs=tuple([ANY] * len(comm_out)),
        input_output_aliases=aliases,
        scratch_shapes=[pltpu.SemaphoreType.DMA((n_sems,)), pltpu.SemaphoreType.DMA((n_sems,))],
    )(*comm_in)


GATHER_PARTS = 4


def _gather(bufs, stage, part=(0, GATHER_PARTS)):
    n = 3 * len(bufs)
    lo, hi = part

    def copies(outs, send, recv, d2d, incoming):
        x, y, c, chips = _place()
        for t, buf in enumerate(outs):
            half = buf.shape[2] // 2
            piece = half // GATHER_PARTS
            for k, (cx, cy) in enumerate(chips):
                i = 3 * t + k + (n if (d2d and stage == "both") else 0)
                cols = pl.ds(((1 - c) if (d2d and incoming) else c) * half + lo * piece, (hi - lo) * piece)
                blk = buf.at[(2 * cx + cy) if (d2d or incoming) else (2 * x + y), :, cols]
                yield _remote(blk, blk, send(i), recv(i), (x, y, 1 - c) if d2d else (cx, cy, c))

    def start(ins, outs, send, recv):
        for cp in copies(outs, send, recv, stage == "d2d", False):
            cp.start()

    def finish(ins, outs, send, recv):
        if stage == "both":
            for arrival, onward in zip(copies(outs, send, recv, False, True), copies(outs, send, recv, True, False)):
                arrival.wait_recv()
                onward.start()
        for cp in copies(outs, send, recv, stage != "ici", True):
            cp.wait_recv()
        for d2d in ((False, True) if stage == "both" else (stage == "d2d",)):
            for cp in copies(outs, send, recv, d2d, False):
                cp.wait_send()

    return _Phase(bufs, [], True, (2 if stage == "both" else 1) * n, start, finish)


def _merge(first, second):
    n_first = first.n_sems

    def later(sem):
        return lambda i: sem(n_first + i)

    def start(ins, outs, send, recv):
        first.start(ins, outs, send, recv)
        second.start(ins, outs, later(send), later(recv))

    def finish(ins, outs, send, recv):
        first.finish(ins, outs, send, recv)
        second.finish(ins, outs, later(send), later(recv))

    return _Phase(first.arrays, [], True, n_first + second.n_sems, start, finish)


def _swap_halves(grads):
    def copies(ins, outs, send, recv):
        x, y, c, _ = _place()
        for t, g in enumerate(ins):
            half = g.shape[2] // 2
            yield _remote(g.at[:, :, pl.ds((1 - c) * half, half)], outs[t], send(t), recv(t), (x, y, 1 - c))

    def start(ins, outs, send, recv):
        for cp in copies(ins, outs, send, recv):
            cp.start()

    def finish(ins, outs, send, recv):
        for cp in copies(ins, outs, send, recv):
            cp.wait()

    shapes = [jax.ShapeDtypeStruct((a.shape[0], a.shape[1], a.shape[2] // 2), a.dtype) for a in grads]
    return _Phase(grads, shapes, False, len(grads), start, finish)


def _add_halves(grad, received, place, *, name):
    ns, rows, half = received.shape
    tr, tc = _rc_tile(rows, half, pref_rows=1024)
    per = half // tc

    def body(place_ref, g_ref, r_ref, o_ref):
        o_ref[...] = (g_ref[...].astype(F32) + r_ref[...].astype(F32)).astype(BF16)

    grid_spec = pltpu.PrefetchScalarGridSpec(
        num_scalar_prefetch=1, grid=(ns, rows // tr, per),
        in_specs=[pl.BlockSpec((None, tr, tc), lambda s, i, j, pr: (s, i, pr[1] * per + j)),
                  pl.BlockSpec((None, tr, tc), lambda s, i, j, pr: (s, i, j))],
        out_specs=pl.BlockSpec((None, tr, tc), lambda s, i, j, pr: (s, i, j)),
    )
    return pl.pallas_call(
        body, name=name, grid_spec=grid_spec, out_shape=jax.ShapeDtypeStruct(received.shape, BF16),
        compiler_params=_params("parallel", "parallel", "parallel"),
    )(place, grad, received)


def _send_partials(parts, piece=(0, 1)):
    k_th, n_pieces = piece

    def cols(part):
        width = part.shape[2] // n_pieces
        return pl.ds(k_th * width, width)

    def start(ins, outs, send, recv):
        x, y, c, chips = _place()
        for t, part in enumerate(ins):
            for k, (cx, cy) in enumerate(chips):
                _remote(part.at[2 * cx + cy, :, cols(part)], outs[t].at[2 * x + y],
                        send(3 * t + k), recv(3 * t + k), (cx, cy, c)).start()

    def finish(ins, outs, send, recv):
        x, y, c, chips = _place()
        for t, part in enumerate(ins):
            for k, (cx, cy) in enumerate(chips):
                slot = outs[t].at[2 * cx + cy]
                _remote(slot, slot, send(3 * t + k), recv(3 * t + k), (cx, cy, c)).wait_recv()
        for t, part in enumerate(ins):
            for k, (cx, cy) in enumerate(chips):
                sent = part.at[2 * cx + cy, :, cols(part)]
                _remote(sent, sent, send(3 * t + k), recv(3 * t + k), (cx, cy, c)).wait_send()

    shapes = [jax.ShapeDtypeStruct(a.shape[:2] + (a.shape[2] // n_pieces,), a.dtype) for a in parts]
    return _Phase(parts, shapes, False, 3 * len(parts), start, finish)


def _sum_chips(parts, slots, place, *, name, piece=(0, 1), into=None):
    ns, rows, width = slots.shape
    k_th, n_pieces = piece
    half = width * n_pieces
    tr, tc = _rc_tile(rows, width, pref_rows=512)
    per = width // tc

    def body(place_ref, p_ref, s_ref, *rest):
        acc = p_ref[...].astype(F32)
        for k in range(ns):
            acc = acc + jnp.where(place_ref[0] == k, 0.0, s_ref[k].astype(F32))
        rest[-1][...] = acc

    grid_spec = pltpu.PrefetchScalarGridSpec(
        num_scalar_prefetch=1, grid=(rows // tr, per),
        in_specs=[pl.BlockSpec((None, tr, tc), lambda i, j, pr: (pr[0], i, k_th * per + j)),
                  pl.BlockSpec((ns, tr, tc), lambda i, j, pr: (0, i, j))] + ([ANY] if into is not None else []),
        out_specs=pl.BlockSpec((tr, tc), lambda i, j, pr: (i, (pr[1] * n_pieces + k_th) * per + j)),
    )
    return pl.pallas_call(
        body, name=name, grid_spec=grid_spec, out_shape=jax.ShapeDtypeStruct((rows, 2 * half), F32),
        input_output_aliases={3: 0} if into is not None else {},
        compiler_params=_params("parallel", "parallel"),
    )(place, parts, slots, *([into] if into is not None else []))


def _join_halves(bufs):
    def copies(outs, send, recv, incoming):
        x, y, c, _ = _place()
        for t, buf in enumerate(outs):
            half = buf.shape[1] // 2
            cols = buf.at[:, pl.ds(((1 - c) if incoming else c) * half, half)]
            yield _remote(cols, cols, send(t), recv(t), (x, y, 1 - c))

    def start(ins, outs, send, recv):
        for cp in copies(outs, send, recv, False):
            cp.start()

    def finish(ins, outs, send, recv):
        for cp in copies(outs, send, recv, True):
            cp.wait_recv()
        for cp in copies(outs, send, recv, False):
            cp.wait_send()

    return _Phase(bufs, [], True, len(bufs), start, finish)


def _gather_small(buf):
    def slot(out, px, py, pc):
        return out.at[4 * px + 2 * py + pc]

    def start(ins, outs, send, recv):
        x, y, c, chips = _place()
        mine = slot(outs[0], x, y, c)
        _remote(ins[0], mine, send(0), recv(0), (x, y, 1 - c)).start()
        for k, (cx, cy) in enumerate(chips):
            _remote(ins[0], mine, send(1 + k), recv(1 + k), (cx, cy, c)).start()

    def finish(ins, outs, send, recv):
        x, y, c, chips = _place()
        sibling = (x, y, 1 - c)
        for k, (cx, cy) in enumerate(chips):
            arrived = slot(outs[0], cx, cy, c)
            _remote(arrived, arrived, send(1 + k), recv(1 + k), sibling).wait_recv()
            _remote(arrived, arrived, send(4 + k), recv(4 + k), sibling).start()
        theirs = slot(outs[0], x, y, 1 - c)
        _remote(theirs, theirs, send(0), recv(0), sibling).wait_recv()
        for k, (cx, cy) in enumerate(chips):
            passed = slot(outs[0], cx, cy, 1 - c)
            _remote(passed, passed, send(4 + k), recv(4 + k), sibling).wait_recv()
        for i in range(7):
            _remote(ins[0], ins[0], send(i), recv(i), sibling).wait_send()

    return _Phase([buf], [jax.ShapeDtypeStruct((N_DEV,) + buf.shape, buf.dtype)], False, 7, start, finish)


def _adamw_math(w, g, m, v):
    m = ADAM_B1 * m + (1.0 - ADAM_B1) * g
    v = ADAM_B2 * v + (1.0 - ADAM_B2) * (g * g)
    m_hat = m / (1.0 - ADAM_B1 ** ADAM_STEP)
    v_hat = v / (1.0 - ADAM_B2 ** ADAM_STEP)
    delta = -ADAM_LR * (m_hat / (jnp.sqrt(v_hat) + ADAM_EPS) + ADAM_WD * w)
    return delta, m, v


def _adamw(w, g, m, v, *, name):
    rows, cols = w.shape
    tr, tc = _rc_tile(rows, cols)

    def body(w_ref, g_ref, m_ref, v_ref, go_ref, d_ref, mo_ref, vo_ref):
        g = g_ref[...]
        go_ref[...] = g
        d_ref[...], mo_ref[...], vo_ref[...] = _adamw_math(w_ref[...], g, m_ref[...], v_ref[...])

    blk = pl.BlockSpec((tr, tc), lambda i, j: (i, j))
    shape = jax.ShapeDtypeStruct((rows, cols), F32)
    return pl.pallas_call(
        body, name=name, out_shape=(shape, shape, shape, shape), grid=(rows // tr, cols // tc),
        in_specs=[blk] * 4, out_specs=(blk, blk, blk, blk), compiler_params=_params("parallel", "parallel"),
    )(w, g, m, v)


def _adamw_small(gathered, own, place, w, m, v, *, name):
    nd = gathered.shape[0]

    def body(place_ref, gs_ref, own_ref, w_ref, m_ref, v_ref, g_ref, d_ref, mo_ref, vo_ref):
        me = 2 * place_ref[0] + place_ref[1]
        g = jnp.zeros(own_ref.shape, F32)
        for k in range(nd):
            g = g + jnp.where(me == k, own_ref[...], gs_ref[k])
        g_ref[...] = g
        d_ref[...], mo_ref[...], vo_ref[...] = _adamw_math(w_ref[...], g, m_ref[...], v_ref[...])

    whole = pl.BlockSpec(w.shape, lambda i, pr: (0, 0))
    grid_spec = pltpu.PrefetchScalarGridSpec(
        num_scalar_prefetch=1, grid=(1,),
        in_specs=[pl.BlockSpec(gathered.shape, lambda i, pr: (0, 0, 0)), whole, whole, whole, whole],
        out_specs=(whole, whole, whole, whole))
    shape = jax.ShapeDtypeStruct(w.shape, F32)
    return pl.pallas_call(body, name=name, grid_spec=grid_spec, out_shape=(shape, shape, shape, shape),
                          compiler_params=_params("arbitrary"))(place, gathered, own, w, m, v)


def _pack(parts):
    flat = jnp.concatenate([p.reshape(-1).astype(F32) for p in parts])
    rows = -(-flat.shape[0] // (8 * LANES)) * 8
    return jnp.pad(flat, (0, rows * LANES - flat.shape[0])).reshape(rows, LANES)


def _unpack(buf, shapes):
    flat = buf.reshape(-1)
    out, pos = [], 0
    for shp in shapes:
        size = int(np.prod(shp))
        out.append(flat[pos:pos + size].reshape(shp))
        pos += size
    return out


ROW_BLOCK = 256


def _realign_rows(sources, segments, out_shape, *, name):
    n_slots, rows, cols = out_shape
    n_src = len(sources)
    per_slot = -(-rows // ROW_BLOCK)
    table = np.zeros((6, n_slots * per_slot, n_src), np.int32)
    for so in range(n_slots):
        for first, last, src, src_slot, src_row in segments[so]:
            for blk in range(first // ROW_BLOCK, (last - 1) // ROW_BLOCK + 1):
                lo, hi = max(first, blk * ROW_BLOCK), min(last, (blk + 1) * ROW_BLOCK)
                base = src_row + (blk * ROW_BLOCK - first)
                m0 = (base + lo - blk * ROW_BLOCK) // ROW_BLOCK
                at = so * per_slot + blk
                assert table[4, at, src] == 0, "two segments of one block share a source operand"
                table[:, at, src] = (src_slot, m0, base - m0 * ROW_BLOCK, lo - blk * ROW_BLOCK, hi - blk * ROW_BLOCK,
                                     min(2 * ROW_BLOCK, sources[src].shape[1] - m0 * ROW_BLOCK))
    last_block = [-(-a.shape[1] // ROW_BLOCK) - 1 for a in sources]

    def body(slot_ref, blk_ref, off_ref, lo_ref, hi_ref, valid_ref, *refs):
        o_ref, acc = refs[2 * n_src], refs[2 * n_src + 1]
        at = (pl.program_id(0) * per_slot + pl.program_id(1)) * n_src
        acc[...] = jnp.zeros_like(acc)
        for p in range(n_src):
            @pl.when(hi_ref[at + p] > lo_ref[at + p])
            def _():
                two = jnp.concatenate([refs[2 * p][...], refs[2 * p + 1][...]], axis=0)
                src_row = lax.broadcasted_iota(jnp.int32, two.shape, 0)
                two = jnp.where(src_row < valid_ref[at + p], two, jnp.zeros_like(two))
                r = lax.broadcasted_iota(jnp.int32, (ROW_BLOCK, 2 * ROW_BLOCK), 0)
                c = lax.broadcasted_iota(jnp.int32, (ROW_BLOCK, 2 * ROW_BLOCK), 1)
                place = (c == r + off_ref[at + p]) & (r >= lo_ref[at + p]) & (r < hi_ref[at + p])
                acc[...] += jnp.dot(place.astype(two.dtype), two, preferred_element_type=F32)
        o_ref[...] = acc[...].astype(o_ref.dtype)

    def src_spec(p, second):
        def index(so, i, slot_r, blk_r, off_r, lo_r, hi_r, valid_r):
            at = (so * per_slot + i) * n_src + p
            return slot_r[at], jnp.minimum(blk_r[at] + second, last_block[p]), 0
        return pl.BlockSpec((None, ROW_BLOCK, cols), index)

    grid_spec = pltpu.PrefetchScalarGridSpec(
        num_scalar_prefetch=6, grid=(n_slots, per_slot),
        in_specs=[src_spec(p, second) for p in range(n_src) for second in (0, 1)],
        out_specs=pl.BlockSpec((None, ROW_BLOCK, cols), lambda so, i, *_: (so, i, 0)),
        scratch_shapes=[pltpu.VMEM((ROW_BLOCK, cols), F32)],
    )
    flat = [jnp.asarray(table[k].reshape(-1)) for k in range(6)]
    return pl.pallas_call(
        body, name=name, grid_spec=grid_spec, out_shape=jax.ShapeDtypeStruct(out_shape, sources[0].dtype),
        compiler_params=_params("parallel", "arbitrary"),
    )(*flat, *[a for a in sources for _ in (0, 1)])


def _shard_rows(g, lo, hi):
    rs = g.shape[1]
    pieces = []
    for j in range(g.shape[0]):
        a, b = max(lo, j * rs), min(hi, (j + 1) * rs)
        if a < b:
            pieces.append(g[j, a - j * rs:b - j * rs])
    return pieces


def kernel(x, norm_mix_g, w_in, b_f, gmlp_ln_g, gmlp_ln_b, w_s, b_s, attn_out_g, gmlp_out_g, w_out, norm_ffn_g, w_ff1, w_ff2, norm_final_g, loss_target, m_norm_mix_g, m_w_in, m_b_f, m_gmlp_ln_g, m_gmlp_ln_b, m_w_s, m_b_s, m_attn_out_g, m_gmlp_out_g, m_w_out, m_norm_ffn_g, m_w_ff1, m_w_ff2, m_norm_final_g, v_norm_mix_g, v_w_in, v_b_f, v_gmlp_ln_g, v_gmlp_ln_b, v_w_s, v_b_s, v_attn_out_g, v_gmlp_out_g, v_w_out, v_norm_ffn_g, v_w_ff1, v_w_ff2, v_norm_final_g):
    seq, d_model = x.shape[1], x.shape[2]
    d_attn = d_model // 2
    n_heads = d_attn // HEAD_DIM
    qkv = 3 * d_attn
    shard_cols = w_in.shape[2]
    assert N_CHIPS * shard_cols == qkv + n_heads + 2 * d_attn
    xs = x.reshape(seq, d_model)
    target = loss_target.reshape(seq, d_model)

    place = jnp.stack([2 * lax.axis_index("x") + lax.axis_index("y"), lax.axis_index("c")]).astype(jnp.int32)
    names = ["w_in", "w_out", "w_ff1", "w_ff2"]
    wt_in, mt_in, vt_in = w_in[0].T, m_w_in[0].T, v_w_in[0].T
    b_in, _ = _cast_into_slot(wt_in, place, name="cast_w_in")
    b_out, _ = _cast_into_slot(w_out[0], place, name="cast_w_out")
    b_ff1, (b_in,) = _cast_into_slot(w_ff1[0], place, name="cast_w_ff1", phases=[_gather([b_in], "ici", (0, 2))])
    b_ff2, (b_in,) = _cast_into_slot(w_ff2[0], place, name="cast_w_ff2",
                                     phases=[_merge(_gather([b_in], "ici", (2, 3)), _gather([b_in], "d2d", (0, 2)))])
    h, (b_in,) = _rmsnorm_fwd(xs, norm_mix_g, name="norm_mix",
                              phases=[_merge(_gather([b_in], "ici", (3, 4)), _gather([b_in], "d2d", (2, 3)))])
    (g_in,) = _exchange([_gather([b_in], "d2d", (3, 4))], name="allgather_w_in_tail")
    n_cols = N_CHIPS * shard_cols
    gate_slot, gate_row = divmod(qkv, shard_cols)
    assert gate_row + n_heads <= shard_cols
    pieces = []
    for j in range(N_CHIPS):
        if j == gate_slot:
            pieces += [(j, 0, gate_row), (j, gate_row + n_heads, shard_cols - gate_row - n_heads)]
        else:
            pieces.append((j, 0, shard_cols))
    fwd_segments, at = [[]], 0
    for order, (j, src_row, size) in enumerate(pieces):
        fwd_segments[0].append((at, at + size, order % 3, j, src_row))
        at += size
    wt_main = _realign_rows([g_in] * 3, fwd_segments, (1, n_cols - n_heads, d_model), name="w_in_rows")[0]
    wt_f = jnp.pad(jnp.concatenate(_shard_rows(g_in, qkv, qkv + n_heads), axis=0), ((0, LANES - n_heads), (0, 0)))
    b_f_pad = jnp.pad(b_f, ((0, 0), (0, LANES - n_heads)))
    b_col = b_s[0].T

    first, rest = (0, 1), (1, GATHER_PARTS)
    z, (b_out, b_ff1) = _matmul(h, wt_main, name="in_proj", out_dtype=BF16, trans_b=True, tm=2048,
                                phases=[_gather([b_out], "ici"), _gather([b_ff1], "ici", first)])
    zb, f_cum = _forget_fwd(h, wt_f, b_f_pad, name="forget_fwd")
    f_row = f_cum[:, :n_heads].T[:, None, :]
    (o, lse2), (b_ff1, b_out) = _attn_fwd(z, f_row, n_heads, name="attn_fwd",
                                          phases=[_gather([b_ff1], "ici", rest), _gather([b_out], "d2d")])
    merged = _mix_fwd(z, o, gmlp_ln_g, gmlp_ln_b, w_s[0], b_col, attn_out_g, gmlp_out_g, n_heads, name="mix_fwd")
    w_out_full = b_out.reshape(2 * d_attn, d_model)
    x1, (b_ff1, b_ff2) = _matmul(merged, w_out_full, name="out_proj", out_dtype=F32, residual=xs,
                                 phases=[_gather([b_ff1], "d2d"), _gather([b_ff2], "ici", first)])
    h2, _ = _rmsnorm_fwd(x1, norm_ffn_g, name="norm_ffn")
    a, (b_ff2,) = _matmul(h2, b_ff1, name="ff1", out_dtype=BF16, relu=True, b_sharded=True, tm=2048,
                          phases=[_merge(_gather([b_ff2], "both", rest), _gather([b_ff2], "d2d", first))])
    w_ff2_full = b_ff2.reshape(N_CHIPS * b_ff2.shape[1], d_model)
    x2, _ = _matmul(a, w_ff2_full, name="ff2", out_dtype=F32, square_lhs=True, residual=x1)
    dx2, dx2_b, dg_final, loss = _loss_and_final_bwd(x2, target, norm_final_g.reshape(1, d_model), name="loss_head")

    def pair_sum(g, r, nm):
        return _add_halves(g, r, place, name="grads_pair_sum_" + nm)

    def chip_sum(p, q, nm, **piece):
        return _sum_chips(p, q, place, name="grads_chip_sum_" + nm, **piece)

    dw_ff2, _ = _matmul(a, dx2_b, name="ff2_dw", out_dtype=BF16, trans_a=True, square_lhs=True)
    dw_ff2 = dw_ff2.reshape(N_CHIPS, -1, d_model)
    da, (r_ff2,) = _matmul(dx2_b, w_ff2_full, name="ff2_dlhs", out_dtype=BF16, trans_b=True, scale2_by=a, tm=2048,
                           phases=[_swap_halves([dw_ff2])])
    ps_ff2 = pair_sum(dw_ff2, r_ff2, "w_ff2")
    dh2, (q_ff2a,) = _matmul(da, b_ff1, name="ff1_dlhs", out_dtype=F32, trans_b=True, b_sharded=True,
                             phases=[_send_partials([ps_ff2], (0, 2))])
    dw_ff1, (q_ff2b,) = _matmul(h2, da, name="ff1_dw", out_dtype=BF16, trans_a=True, out_sharded=True, tk=seq,
                                phases=[_send_partials([ps_ff2], (1, 2))])
    g_ff2 = chip_sum(ps_ff2, q_ff2a, "w_ff2_a", piece=(0, 2))
    g_ff2 = chip_sum(ps_ff2, q_ff2b, "w_ff2_b", piece=(1, 2), into=g_ff2)
    (dx1, dg_ffn, dx1_b), (g_ff2,) = _rmsnorm_bwd(dh2, x1, dx2, norm_ffn_g, name="norm_ffn_bwd", rounded_copy=True,
                                                   phases=[_join_halves([g_ff2])])
    dw_out, _ = _matmul(merged, dx1_b, name="out_proj_dw", out_dtype=BF16, trans_a=True, tk=seq)
    dw_out = dw_out.reshape(N_CHIPS, -1, d_model)
    d_merged, (r_ff1, r_out) = _matmul(dx1_b, w_out_full, name="out_proj_dlhs", out_dtype=F32, trans_b=True,
                                       phases=[_swap_halves([dw_ff1, dw_out])])
    ps_ff1, ps_out = pair_sum(dw_ff1, r_ff1, "w_ff1"), pair_sum(dw_out, r_out, "w_out")
    d_o, dzu, dzv, dw_s, db_col, dlg, dlb, dag, dgg = _mix_bwd(
        z, o, d_merged, gmlp_ln_g, gmlp_ln_b, w_s[0], b_col, attn_out_g, gmlp_out_g, n_heads, name="mix_bwd")
    (dq, dk, dv, d_f_key, d_f_query), (q_ff1, q_out) = _attn_bwd(
        z, o, d_o, lse2, f_row, n_heads, name="attn_bwd", phases=[_send_partials([ps_ff1, ps_out])])
    g_ff1, g_out = chip_sum(ps_ff1, q_ff1, "w_ff1"), chip_sum(ps_out, q_out, "w_out")
    d_f = d_f_key.reshape(n_heads, seq) + d_f_query.reshape(n_heads, seq)
    d_f_pad = jnp.pad(d_f.T, ((0, 0), (0, LANES - n_heads)))
    dzf, db_f = _forget_bwd(d_f_pad, zb, name="forget_bwd")
    dz = jnp.concatenate([dq, dk, dv, dzu, dzv], axis=1)
    early_g = _pack([db_f[:, :n_heads], dlg, dlb, dw_s, db_col.T, dag, dgg, dg_ffn, dg_final])
    dwt_main, (g_ff1, g_out, early_all) = _matmul(dz, h, name="in_proj_dw", out_dtype=BF16, trans_a=True, tk=seq,
                                                  phases=[_join_halves([g_ff1, g_out]), _gather_small(early_g)])
    dwt_f, _ = _matmul(dzf, h, name="gate_dw", out_dtype=BF16, trans_a=True)
    bwd_segments = []
    for j in range(N_CHIPS):
        first = j * shard_cols
        if j < gate_slot:
            bwd_segments.append([(0, shard_cols, 0, 0, first)])
        elif j > gate_slot:
            bwd_segments.append([(0, shard_cols, 0, 0, first - n_heads)])
        else:
            bwd_segments.append([(0, gate_row, 0, 0, first), (gate_row, gate_row + n_heads, 1, 0, 0),
                                 (gate_row + n_heads, shard_cols, 2, 0, qkv)])
    dw_in = _realign_rows([dwt_main[None], dwt_f[None], dwt_main[None]], bwd_segments,
                          (N_CHIPS, shard_cols, d_model), name="dw_in_rows")
    dh_gate, (r_in,) = _matmul(dzf, wt_f, name="gate_dlhs", out_dtype=F32, phases=[_swap_halves([dw_in])])
    ps_in = pair_sum(dw_in, r_in, "w_in")
    dh, (q_in,) = _matmul(dz, wt_main, name="in_proj_dlhs", out_dtype=F32, residual=dh_gate, tk=2560,
                          phases=[_send_partials([ps_in])])
    g_in_sum = chip_sum(ps_in, q_in, "w_in")
    (grad_x, dg_mix), _ = _rmsnorm_bwd(dh, xs, dx1, norm_mix_g, name="norm_mix_bwd")
    late_g = _pack([dg_mix])
    g_in_sum, late_all = _exchange([_join_halves([g_in_sum]), _gather_small(late_g)], name="grads_join_w_in")

    big = {}
    for nm, g, w, m, v in zip(names, (g_in_sum, g_out, g_ff1, g_ff2), (wt_in, w_out[0], w_ff1[0], w_ff2[0]),
                              (mt_in, m_w_out[0], m_w_ff1[0], m_w_ff2[0]), (vt_in, v_w_out[0], v_w_ff1[0], v_w_ff2[0])):
        big[nm] = tuple((t.T if nm == "w_in" else t)[None] for t in _adamw(w, g, m, v, name="adamw_" + nm))

    small_params = dict(
        norm_mix_g=(norm_mix_g, m_norm_mix_g, v_norm_mix_g), b_f=(b_f, m_b_f, v_b_f),
        gmlp_ln_g=(gmlp_ln_g, m_gmlp_ln_g, v_gmlp_ln_g), gmlp_ln_b=(gmlp_ln_b, m_gmlp_ln_b, v_gmlp_ln_b),
        w_s=(w_s, m_w_s, v_w_s), b_s=(b_s, m_b_s, v_b_s), attn_out_g=(attn_out_g, m_attn_out_g, v_attn_out_g),
        gmlp_out_g=(gmlp_out_g, m_gmlp_out_g, v_gmlp_out_g), norm_ffn_g=(norm_ffn_g, m_norm_ffn_g, v_norm_ffn_g),
        norm_final_g=(norm_final_g, m_norm_final_g, v_norm_final_g))

    def small_step(group, grads_all, grads_own, label):
        w, m, v = ([small_params[nm][k] for nm in group] for k in range(3))
        packed = _adamw_small(grads_all, grads_own, place, _pack(w), _pack(m), _pack(v), name="adamw_small_" + label)
        parts = [_unpack(p, [a.shape for a in w]) for p in packed]
        return {nm: tuple(part[i] for part in parts) for i, nm in enumerate(group)}

    early = ["b_f", "gmlp_ln_g", "gmlp_ln_b", "w_s", "b_s", "attn_out_g", "gmlp_out_g", "norm_ffn_g", "norm_final_g"]
    small = {**small_step(early, early_all, early_g, "early"), **small_step(["norm_mix_g"], late_all, late_g, "late")}

    order = ["norm_mix_g", "w_in", "b_f", "gmlp_ln_g", "gmlp_ln_b", "w_s", "b_s", "attn_out_g", "gmlp_out_g", "w_out",
             "norm_ffn_g", "w_ff1", "w_ff2", "norm_final_g"]
    result = {**small, **big}
    total_loss = lax.psum(loss[0, 0], ("x", "y", "c"))
    outs = [total_loss, grad_x.reshape(x.shape)]
    for part in range(4):
        outs += [result[nm][part] for nm in order]
    return tuple(outs)
```

```python
import functools
import math

import numpy as np
import jax
import jax.numpy as jnp
from jax import lax
from jax.experimental import pallas as pl
from jax.experimental.pallas import tpu as pltpu

HEAD_DIM = 128
CHUNK = 128
EPS = 1e-6
LANES = 128
MXU_COLUMNS = 256
N_CHIPS = 4
N_DEV = 8
VMEM_LIMIT_BYTES = 56 * 1024 * 1024

ADAM_LR = 0.001
ADAM_B1 = 0.9
ADAM_B2 = 0.999
ADAM_EPS = 1e-08
ADAM_WD = 0.01
ADAM_STEP = 10

BF16 = jnp.bfloat16
F32 = jnp.float32
MESH = pl.DeviceIdType.MESH
ANY = pl.BlockSpec(memory_space=pl.ANY)
NEG_BIG = -1e30


def _params(*sem):
    return pltpu.CompilerParams(dimension_semantics=tuple(sem), vmem_limit_bytes=VMEM_LIMIT_BYTES)


def _tile(n, pref, unit):
    t = (min(pref, n) // unit) * unit
    while t >= unit:
        if n % t == 0:
            return t
        t -= unit
    return n


def _rc_tile(rows, cols, pref_rows=256, pref_cols=256):
    if rows % 16 == 0:
        return _tile(rows, pref_rows, 16), cols
    return rows, _tile(cols, pref_cols, LANES)


class _Phase:
    def __init__(self, arrays, out_shapes, in_place, n_sems, start, finish):
        self.arrays, self.out_shapes, self.in_place = list(arrays), list(out_shapes), in_place
        self.n_sems, self.start, self.finish = n_sems, start, finish

    @property
    def n_out(self):
        return len(self.arrays) if self.in_place else len(self.out_shapes)


def _run_phases(phases, steps, comm_in, comm_out, send_sems, recv_sems):
    at_in = at_out = at_sem = 0
    for ph in phases:
        for step in steps:
            getattr(ph, step)(comm_in[at_in:at_in + len(ph.arrays)], comm_out[at_out:at_out + ph.n_out],
                              lambda i, base=at_sem: send_sems.at[base + i], lambda i, base=at_sem: recv_sems.at[base + i])
        at_in, at_out, at_sem = at_in + len(ph.arrays), at_out + ph.n_out, at_sem + ph.n_sems


def _call(body, *, name, grid, in_specs, out_specs, out_shape, operands, semantics, scratch_shapes=(),
          n_prefetch=0, phases=()):
    in_specs, out_specs, out_shape = list(in_specs), list(out_specs), list(out_shape)
    scratch_shapes = list(scratch_shapes)
    n_in, n_out, n_scr = len(operands) - n_prefetch, len(out_shape), len(scratch_shapes)
    comm_in = [a for ph in phases for a in ph.arrays]
    comm_out = [jax.ShapeDtypeStruct(s.shape, s.dtype) for ph in phases
                for s in (ph.arrays if ph.in_place else ph.out_shapes)]
    aliases, at_in, at_out = {}, n_prefetch + n_in, n_out
    for ph in phases:
        if ph.in_place:
            aliases.update({at_in + r: at_out + r for r in range(len(ph.arrays))})
        at_in, at_out = at_in + len(ph.arrays), at_out + ph.n_out
    n_sems = sum(ph.n_sems for ph in phases)

    def hosted(*refs):
        pre, rest = refs[:n_prefetch], refs[n_prefetch:]
        ins, rest = rest[:n_in], rest[n_in:]
        cin, rest = rest[:len(comm_in)], rest[len(comm_in):]
        outs, rest = rest[:n_out], rest[n_out:]
        cout, rest = rest[:len(comm_out)], rest[len(comm_out):]
        scr = rest[:n_scr]
        if phases:
            send_sems, recv_sems = rest[n_scr:]
            ids = [pl.program_id(ax) for ax in range(len(grid))]
            first = functools.reduce(jnp.logical_and, [i == 0 for i in ids])
            last = functools.reduce(jnp.logical_and, [i == g - 1 for i, g in zip(ids, grid)])

            @pl.when(first)
            def _():
                _run_phases(phases, ("start",), cin, cout, send_sems, recv_sems)

        body(*pre, *ins, *outs, *scr)
        if phases:
            @pl.when(last)
            def _():
                _run_phases(phases, ("finish",), cin, cout, send_sems, recv_sems)

    all_in = in_specs + [ANY] * len(comm_in)
    all_out = out_specs + [ANY] * len(comm_out)
    all_scr = scratch_shapes + ([pltpu.SemaphoreType.DMA((n_sems,)), pltpu.SemaphoreType.DMA((n_sems,))] if phases else [])
    if phases:
        semantics = ("arbitrary",) * len(grid)
    kwargs = dict(name=name, out_shape=tuple(out_shape + comm_out), compiler_params=_params(*semantics),
                  input_output_aliases=aliases)
    if n_prefetch:
        kwargs["grid_spec"] = pltpu.PrefetchScalarGridSpec(
            num_scalar_prefetch=n_prefetch, grid=grid, in_specs=all_in, out_specs=tuple(all_out), scratch_shapes=all_scr)
    else:
        kwargs.update(grid=grid, in_specs=all_in, out_specs=tuple(all_out), scratch_shapes=all_scr)
    res = pl.pallas_call(hosted, **kwargs)(*operands, *comm_in)
    return tuple(res[:n_out]), tuple(res[n_out:])


def _only(results):
    outs, comm = results
    return outs[0] if len(outs) == 1 else outs, comm


def _matmul(a, b, *, name, out_dtype, trans_a=False, trans_b=False, tm=1024, tn=1024, tk=2048,
            square_lhs=False, relu=False, residual=None, scale2_by=None,
            b_sharded=False, out_sharded=False, phases=()):
    m, k = (a.shape[1], a.shape[0]) if trans_a else a.shape
    if b_sharded:
        if trans_b:
            n, ks = b.shape[1], b.shape[2]
            assert N_CHIPS * ks == k
        else:
            ns = b.shape[2]
            n = N_CHIPS * ns
            assert b.shape[1] == k
    else:
        n = b.shape[0] if trans_b else b.shape[1]
        assert (b.shape[1] if trans_b else b.shape[0]) == k
    tm = _tile(m, tm, 128)
    tn = _tile(n // N_CHIPS if (out_sharded or (b_sharded and not trans_b)) else n, tn, 128)
    tk = _tile(k // N_CHIPS if (b_sharded and trans_b) else k, tk, 128)
    nk = k // tk

    if trans_a:
        a_spec = pl.BlockSpec((tk, tm), lambda i, j, kk: (kk, i))
    else:
        a_spec = pl.BlockSpec((tm, tk), lambda i, j, kk: (i, kk))
    if b_sharded and trans_b:
        per = ks // tk
        assert per * tk == ks
        b_spec = pl.BlockSpec((None, tn, tk), lambda i, j, kk: (kk // per, j, kk % per))
    elif b_sharded:
        per = ns // tn
        assert per * tn == ns
        b_spec = pl.BlockSpec((None, tk, tn), lambda i, j, kk: (j // per, kk, j % per))
    elif trans_b:
        b_spec = pl.BlockSpec((tn, tk), lambda i, j, kk: (j, kk))
    else:
        b_spec = pl.BlockSpec((tk, tn), lambda i, j, kk: (kk, j))
    if out_sharded:
        ns_out = n // N_CHIPS
        per_o = ns_out // tn
        assert per_o * tn == ns_out
        out_shape = jax.ShapeDtypeStruct((N_CHIPS, m, ns_out), out_dtype)
        o_spec = pl.BlockSpec((None, tm, tn), lambda i, j, kk: (j // per_o, i, j % per_o))
    else:
        out_shape = jax.ShapeDtypeStruct((m, n), out_dtype)
        o_spec = pl.BlockSpec((tm, tn), lambda i, j, kk: (i, j))
    mn_spec = pl.BlockSpec((tm, tn), lambda i, j, kk: (i, j))

    operands, in_specs = [a, b], [a_spec, b_spec]
    if scale2_by is not None:
        operands.append(scale2_by)
        in_specs.append(mn_spec)
    if residual is not None:
        operands.append(residual)
        in_specs.append(mn_spec)
    dims = (((0 if trans_a else 1,), (1 if trans_b else 0,)), ((), ()))
    chunk = MXU_COLUMNS if tn % MXU_COLUMNS == 0 else tn

    def body(*refs):
        a_ref, b_ref = refs[0], refs[1]
        pos = 2
        scale_ref = res_ref = None
        if scale2_by is not None:
            scale_ref = refs[pos]
            pos += 1
        if residual is not None:
            res_ref = refs[pos]
            pos += 1
        o_ref = refs[pos]
        kk = pl.program_id(2)

        av = a_ref[...]
        if square_lhs:
            av = av.astype(F32)
            av = av * av
        av = av.astype(BF16)

        def finish(r, cols):
            if relu:
                r = jnp.maximum(r, 0.0)
            if scale_ref is not None:
                r = r * (2.0 * scale_ref[:, cols].astype(F32))
            if res_ref is not None:
                r = r + res_ref[:, cols].astype(F32)
            o_ref[:, cols] = r.astype(out_dtype)

        if nk == 1:
            for lo in range(0, tn, chunk):
                cols = slice(lo, lo + chunk)
                bv = (b_ref[cols, :] if trans_b else b_ref[:, cols]).astype(BF16)
                finish(lax.dot_general(av, bv, dims, preferred_element_type=F32), cols)
        else:
            acc_ref = refs[pos + 1]
            part = lax.dot_general(av, b_ref[...].astype(BF16), dims, preferred_element_type=F32)

            @pl.when(kk == 0)
            def _():
                acc_ref[...] = part

            @pl.when(jnp.logical_and(kk > 0, kk < nk - 1))
            def _():
                acc_ref[...] += part

            @pl.when(kk == nk - 1)
            def _():
                finish(acc_ref[...] + part, slice(None))

    return _only(_call(
        body, name=name, out_shape=[out_shape], grid=(m // tm, n // tn, nk),
        in_specs=in_specs, out_specs=[o_spec], operands=operands,
        scratch_shapes=[pltpu.VMEM((tm, tn), F32)] if nk > 1 else [],
        semantics=("parallel", "parallel", "arbitrary"), phases=phases))


def _rmsnorm_fwd(x, g, *, name, tr=512, phases=()):
    s, d = x.shape
    tr = _tile(s, tr, 8)

    def body(x_ref, g_ref, o_ref):
        xv = x_ref[...]
        r = lax.rsqrt(jnp.mean(xv * xv, axis=-1, keepdims=True) + EPS)
        o_ref[...] = ((xv * r) * g_ref[...]).astype(BF16)

    return _only(_call(
        body, name=name, out_shape=[jax.ShapeDtypeStruct((s, d), BF16)], grid=(s // tr,),
        in_specs=[pl.BlockSpec((tr, d), lambda i: (i, 0)), pl.BlockSpec((1, d), lambda i: (0, 0))],
        out_specs=[pl.BlockSpec((tr, d), lambda i: (i, 0))], operands=[x, g],
        semantics=("parallel",), phases=phases))


def _rms_bwd_rows(dy, xv, g):
    d = xv.shape[-1]
    r = lax.rsqrt(jnp.mean(xv * xv, axis=-1, keepdims=True) + EPS)
    gdy = dy * g
    dot = jnp.sum(gdy * xv, axis=-1, keepdims=True)
    dx = gdy * r - xv * (r * r * r) * (dot / d)
    return dx, dy * (xv * r)


def _rmsnorm_bwd(dy, x, res, g, *, name, tr=256, rounded_copy=False, phases=()):
    s, d = x.shape
    tr = _tile(s, tr, 8)

    def body(dy_ref, x_ref, res_ref, g_ref, dx_ref, dg_ref, *dxb_ref):
        @pl.when(pl.program_id(0) == 0)
        def _():
            dg_ref[...] = jnp.zeros_like(dg_ref)

        dx, dg_rows = _rms_bwd_rows(dy_ref[...].astype(F32), x_ref[...], g_ref[...])
        out = res_ref[...] + dx
        dx_ref[...] = out
        if rounded_copy:
            dxb_ref[0][...] = out.astype(BF16)
        dg_ref[...] += jnp.sum(dg_rows, axis=0, keepdims=True)

    row = pl.BlockSpec((tr, d), lambda i: (i, 0))
    vec = pl.BlockSpec((1, d), lambda i: (0, 0))
    extra = [jax.ShapeDtypeStruct((s, d), BF16)] if rounded_copy else []
    return _call(
        body, name=name,
        out_shape=[jax.ShapeDtypeStruct((s, d), F32), jax.ShapeDtypeStruct((1, d), F32)] + extra,
        grid=(s // tr,), in_specs=[row, row, row, vec], out_specs=[row, vec] + [row] * len(extra),
        operands=[dy, x, res, g], semantics=("arbitrary",), phases=phases)


def _loss_and_final_bwd(x2, target, g, *, name, tr=256):
    s, d = x2.shape
    tr = _tile(s, tr, 8)

    def body(x_ref, t_ref, g_ref, dx_ref, dxb_ref, dg_ref, loss_ref):
        @pl.when(pl.program_id(0) == 0)
        def _():
            dg_ref[...] = jnp.zeros_like(dg_ref)
            loss_ref[...] = jnp.zeros_like(loss_ref)

        xv, gv = x_ref[...], g_ref[...]
        r = lax.rsqrt(jnp.mean(xv * xv, axis=-1, keepdims=True) + EPS)
        err = (xv * r) * gv - t_ref[...]
        row_loss = jnp.mean(err * err, axis=-1, keepdims=True)
        loss_ref[...] += 0.5 * jnp.sum(row_loss, axis=0, keepdims=True)
        dx, dg_rows = _rms_bwd_rows(err / d, xv, gv)
        dx_ref[...] = dx
        dxb_ref[...] = dx.astype(BF16)
        dg_ref[...] += jnp.sum(dg_rows, axis=0, keepdims=True)

    row = pl.BlockSpec((tr, d), lambda i: (i, 0))
    vec = pl.BlockSpec((1, d), lambda i: (0, 0))
    one = pl.BlockSpec((1, 1), lambda i: (0, 0))
    return pl.pallas_call(
        body, name=name,
        out_shape=(jax.ShapeDtypeStruct((s, d), F32), jax.ShapeDtypeStruct((s, d), BF16),
                   jax.ShapeDtypeStruct((1, d), F32), jax.ShapeDtypeStruct((1, 1), F32)),
        grid=(s // tr,), in_specs=[row, row, vec], out_specs=(row, row, vec, one),
        compiler_params=_params("arbitrary"),
    )(x2, target, g)


def _tri_ones(n, lower):
    r = lax.broadcasted_iota(jnp.int32, (n, n), 0)
    c = lax.broadcasted_iota(jnp.int32, (n, n), 1)
    return jnp.where((c <= r) if lower else (c >= r), 1.0, 0.0).astype(F32)


def _forget_fwd(h, w_f, b_f, *, name, tr=256):
    s, d = h.shape
    tr = _tile(s, tr, 8)

    def body(h_ref, w_ref, b_ref, zb_ref, f_ref, carry):
        @pl.when(pl.program_id(0) == 0)
        def _():
            carry[...] = jnp.zeros_like(carry)

        zb = lax.dot_general(h_ref[...], w_ref[...], (((1,), (1,)), ((), ())), preferred_element_type=F32) + b_ref[...]
        zb_ref[...] = zb
        log_f = jnp.minimum(zb, 0.0) - jnp.log(1.0 + jnp.exp(-jnp.abs(zb)))
        run = jnp.dot(_tri_ones(tr, True), log_f, preferred_element_type=F32,
                      precision=lax.Precision.HIGHEST) + carry[...]
        f_ref[...] = run
        carry[...] = run[tr - 1:tr, :]

    row = pl.BlockSpec((tr, LANES), lambda i: (i, 0))
    return pl.pallas_call(
        body, name=name,
        out_shape=(jax.ShapeDtypeStruct((s, LANES), F32), jax.ShapeDtypeStruct((s, LANES), F32)),
        grid=(s // tr,),
        in_specs=[pl.BlockSpec((tr, d), lambda i: (i, 0)), pl.BlockSpec((LANES, d), lambda i: (0, 0)),
                  pl.BlockSpec((1, LANES), lambda i: (0, 0))],
        out_specs=(row, row), scratch_shapes=[pltpu.VMEM((1, LANES), F32)],
        compiler_params=_params("arbitrary"),
    )(h, w_f, b_f)


def _forget_bwd(d_f, zb, *, name, tr=256):
    s = zb.shape[0]
    tr = _tile(s, tr, 8)
    nb = s // tr

    def body(df_ref, zb_ref, dz_ref, db_ref, carry):
        @pl.when(pl.program_id(0) == 0)
        def _():
            carry[...] = jnp.zeros_like(carry)
            db_ref[...] = jnp.zeros_like(db_ref)

        run = jnp.dot(_tri_ones(tr, False), df_ref[...], preferred_element_type=F32,
                      precision=lax.Precision.HIGHEST) + carry[...]
        carry[...] = run[0:1, :]
        dz = run / (1.0 + jnp.exp(zb_ref[...]))
        dz_ref[...] = dz.astype(BF16)
        db_ref[...] += jnp.sum(dz, axis=0, keepdims=True)

    row = pl.BlockSpec((tr, LANES), lambda i: (nb - 1 - i, 0))
    return pl.pallas_call(
        body, name=name,
        out_shape=(jax.ShapeDtypeStruct((s, LANES), BF16), jax.ShapeDtypeStruct((1, LANES), F32)),
        grid=(nb,), in_specs=[row, row], out_specs=(row, pl.BlockSpec((1, LANES), lambda i: (0, 0))),
        scratch_shapes=[pltpu.VMEM((1, LANES), F32)],
        compiler_params=_params("arbitrary"),
    )(d_f, zb)


def _pairs(nblk, by_kv):
    if by_kv:
        pr = [(i, j) for j in range(nblk) for i in range(j, nblk)]
    else:
        pr = [(i, j) for i in range(nblk) for j in range(i + 1)]
    return (jnp.asarray(np.array([p[0] for p in pr], np.int32)), jnp.asarray(np.array([p[1] for p in pr], np.int32)))


def _causal_mask(rows, keys):
    r = lax.broadcasted_iota(jnp.int32, (rows[1] - rows[0], keys[1] - keys[0]), 0) + rows[0]
    c = lax.broadcasted_iota(jnp.int32, (rows[1] - rows[0], keys[1] - keys[0]), 1) + keys[0]
    return c <= r


def _diagonal_pieces(tb):
    half = tb // 2
    if half % LANES:
        return [((0, tb), (0, tb))]
    return [((0, half), (0, half)), ((half, tb), (0, tb))]


LOG2E = math.log2(math.e)
QK_TO_LOG2 = LOG2E / math.sqrt(HEAD_DIM)


def _attn_logits2(q, k, fk_row):
    sc = lax.dot_general(q, k, (((1,), (1,)), ((), ())), preferred_element_type=F32)
    return sc * QK_TO_LOG2 - fk_row * LOG2E


def _attn_fwd(z, f_row, n_heads, *, name, tb=1024, phases=()):
    s = z.shape[0]
    tb = _tile(s, tb, 128)
    nblk = s // tb
    qi, kj = _pairs(nblk, by_kv=False)

    def body(qi_ref, kj_ref, q_ref, k_ref, v_ref, fk_ref, o_ref, lse_ref, m_sc, l_sc, acc_sc):
        p = pl.program_id(1)
        i, j = qi_ref[p], kj_ref[p]

        @pl.when(j == 0)
        def _():
            m_sc[...] = jnp.full_like(m_sc, NEG_BIG)
            l_sc[...] = jnp.zeros_like(l_sc)
            acc_sc[...] = jnp.zeros_like(acc_sc)

        def update(rows, keys, masked):
            rs, ks = slice(*rows), slice(*keys)
            s2 = _attn_logits2(q_ref[rs, :], k_ref[ks, :], fk_ref[:, ks])
            if masked:
                s2 = jnp.where(_causal_mask(rows, keys), s2, NEG_BIG)
            m_old = m_sc[rs, :]
            m_new = jnp.maximum(m_old, jnp.max(s2, axis=-1, keepdims=True))
            alpha = jnp.exp2(m_old - m_new)
            pv = jnp.exp2(s2 - jnp.tile(m_new, (1, (keys[1] - keys[0]) // LANES)))
            l_sc[rs, :] = alpha * l_sc[rs, :] + jnp.sum(pv, axis=-1, keepdims=True)
            acc_sc[rs, :] = alpha * acc_sc[rs, :] + jnp.dot(pv.astype(BF16), v_ref[ks, :], preferred_element_type=F32)
            m_sc[rs, :] = m_new

        @pl.when(j < i)
        def _():
            update((0, tb), (0, tb), False)

        @pl.when(j == i)
        def _():
            for rows, keys in _diagonal_pieces(tb):
                update(rows, keys, True)
            o_ref[...] = (acc_sc[...] / l_sc[...]).astype(BF16)
            lse_ref[...] = m_sc[...] + jnp.log2(l_sc[...])

    h = n_heads
    return _call(
        body, name=name, n_prefetch=2, grid=(h, int(qi.shape[0])),
        in_specs=[
            pl.BlockSpec((tb, HEAD_DIM), lambda hh, p, qi_r, kj_r: (qi_r[p], hh)),
            pl.BlockSpec((tb, HEAD_DIM), lambda hh, p, qi_r, kj_r: (kj_r[p], h + hh)),
            pl.BlockSpec((tb, HEAD_DIM), lambda hh, p, qi_r, kj_r: (kj_r[p], 2 * h + hh)),
            pl.BlockSpec((None, 1, tb), lambda hh, p, qi_r, kj_r: (hh, 0, kj_r[p])),
        ],
        out_specs=[
            pl.BlockSpec((tb, HEAD_DIM), lambda hh, p, qi_r, kj_r: (qi_r[p], hh)),
            pl.BlockSpec((None, tb, LANES), lambda hh, p, qi_r, kj_r: (hh, qi_r[p], 0)),
        ],
        scratch_shapes=[pltpu.VMEM((tb, LANES), F32), pltpu.VMEM((tb, LANES), F32), pltpu.VMEM((tb, HEAD_DIM), F32)],
        out_shape=[jax.ShapeDtypeStruct((s, h * HEAD_DIM), BF16), jax.ShapeDtypeStruct((h, s, LANES), F32)],
        operands=[qi, kj, z, z, z, f_row], semantics=("parallel", "arbitrary"), phases=phases)


def _attn_bwd(z, o, d_o, lse2, f_row, n_heads, *, name, tb=1024, phases=()):
    s = z.shape[0]
    tb = _tile(s, tb, 128)
    nblk = s // tb
    qi, kj = _pairs(nblk, by_kv=True)
    n_pairs = int(qi.shape[0])
    scale = 1.0 / math.sqrt(HEAD_DIM)
    h = n_heads

    def body(qi_ref, kj_ref, q_ref, k_ref, v_ref, o_ref, do_ref, lse_ref, fk_ref,
             dq_ref, dk_ref, dv_ref, df_ref, dfq_ref, dq_sc, dk_sc, dv_sc, df_sc, dfq_sc):
        p = pl.program_id(1)
        i, j = qi_ref[p], kj_ref[p]

        @pl.when(p == 0)
        def _():
            dq_sc[...] = jnp.zeros_like(dq_sc)
            dfq_sc[...] = jnp.zeros_like(dfq_sc)

        @pl.when(i == j)
        def _():
            dk_sc[...] = jnp.zeros_like(dk_sc)
            dv_sc[...] = jnp.zeros_like(dv_sc)
            df_sc[...] = jnp.zeros_like(df_sc)

        def update(rows, keys, masked):
            rs, ks, n_rows = slice(*rows), slice(*keys), rows[1] - rows[0]
            q, k, v, do = q_ref[rs, :], k_ref[ks, :], v_ref[ks, :], do_ref[rs, :]
            delta = jnp.sum(do.astype(F32) * o_ref[rs, :].astype(F32), axis=-1, keepdims=True)
            pv = jnp.exp2(_attn_logits2(q, k, fk_ref[:, ks]) - jnp.tile(lse_ref[rs, :], (1, (keys[1] - keys[0]) // LANES)))
            if masked:
                pv = jnp.where(_causal_mask(rows, keys), pv, 0.0)
            dp = lax.dot_general(do, v, (((1,), (1,)), ((), ())), preferred_element_type=F32)
            ds = pv * (dp - delta)
            ds_b = ds.astype(BF16)
            dv_sc[ks, :] += lax.dot_general(pv.astype(BF16), do, (((0,), (0,)), ((), ())), preferred_element_type=F32)
            dk_sc[ks, :] += lax.dot_general(ds_b, q, (((0,), (0,)), ((), ())), preferred_element_type=F32)
            at = pl.ds(pl.multiple_of(i * tb + rows[0], LANES), n_rows)
            dq_sc[at, :] += jnp.dot(ds_b, k, preferred_element_type=F32)
            df_sc[:, ks] -= jnp.sum(ds, axis=0, keepdims=True)
            dfq_sc[at, :] += jnp.broadcast_to(jnp.sum(ds, axis=1, keepdims=True), (n_rows, LANES))

        @pl.when(i > j)
        def _():
            update((0, tb), (0, tb), False)

        @pl.when(i == j)
        def _():
            for rows, keys in _diagonal_pieces(tb):
                update(rows, keys, True)

        @pl.when(i == nblk - 1)
        def _():
            dk_ref[...] = (dk_sc[...] * scale).astype(BF16)
            dv_ref[...] = dv_sc[...].astype(BF16)
            df_ref[...] = df_sc[...]

        @pl.when(p == n_pairs - 1)
        def _():
            dq_ref[...] = (dq_sc[...] * scale).astype(BF16)
            dfq_ref[...] = jnp.transpose(dfq_sc[...])[0:1, :]

    qblk = lambda off: pl.BlockSpec((tb, HEAD_DIM), lambda hh, p, qi_r, kj_r: (qi_r[p], off + hh))
    kblk = lambda off: pl.BlockSpec((tb, HEAD_DIM), lambda hh, p, qi_r, kj_r: (kj_r[p], off + hh))
    qrep = pl.BlockSpec((None, tb, LANES), lambda hh, p, qi_r, kj_r: (hh, qi_r[p], 0))
    krow = pl.BlockSpec((None, 1, tb), lambda hh, p, qi_r, kj_r: (hh, 0, kj_r[p]))
    act = jax.ShapeDtypeStruct((s, h * HEAD_DIM), BF16)
    return _call(
        body, name=name, n_prefetch=2, grid=(h, n_pairs),
        in_specs=[qblk(0), kblk(h), kblk(2 * h), qblk(0), qblk(0), qrep, krow],
        out_specs=[
            pl.BlockSpec((s, HEAD_DIM), lambda hh, p, qi_r, kj_r: (0, hh)),
            kblk(0), kblk(0), krow,
            pl.BlockSpec((None, 1, s), lambda hh, p, qi_r, kj_r: (hh, 0, 0)),
        ],
        scratch_shapes=[pltpu.VMEM((s, HEAD_DIM), F32), pltpu.VMEM((tb, HEAD_DIM), F32),
                        pltpu.VMEM((tb, HEAD_DIM), F32), pltpu.VMEM((1, tb), F32), pltpu.VMEM((s, LANES), F32)],
        out_shape=[act, act, act, jax.ShapeDtypeStruct((h, 1, s), F32), jax.ShapeDtypeStruct((h, 1, s), F32)],
        operands=[qi, kj, z, z, z, o, d_o, lse2, f_row], semantics=("parallel", "arbitrary"), phases=phases)


GELU_C = math.sqrt(2.0 / math.pi)
GELU_A = 0.044715


def _gelu(x):
    return 0.5 * x * (1.0 + jnp.tanh(GELU_C * (x + GELU_A * (x * x * x))))


def _gelu_and_grad(x):
    t = jnp.tanh(GELU_C * (x + GELU_A * (x * x * x)))
    y = 0.5 * x * (1.0 + t)
    dy = 0.5 * (1.0 + t) + 0.5 * x * (1.0 - t * t) * (GELU_C * (1.0 + 3.0 * GELU_A * (x * x)))
    return y, dy


def _layernorm_parts(g):
    mu = jnp.mean(g, axis=-1, keepdims=True)
    xc = g - mu
    rs = lax.rsqrt(jnp.mean(xc * xc, axis=-1, keepdims=True) + EPS)
    return xc * rs, rs


def _spatial_mix(w_ref, bcol_ref, vv_b, n_heads, n_chunks):
    tril = _causal_mask((0, CHUNK), (0, CHUNK))
    cols = []
    for hh in range(n_heads):
        wc = jnp.where(tril, w_ref[hh], 0.0).astype(BF16)
        lanes = slice(hh * HEAD_DIM, (hh + 1) * HEAD_DIM)
        rows = [jnp.dot(wc, vv_b[c * CHUNK:(c + 1) * CHUNK, lanes], preferred_element_type=F32)
                + bcol_ref[:, hh:hh + 1] for c in range(n_chunks)]
        cols.append(jnp.concatenate(rows, axis=0))
    return jnp.concatenate(cols, axis=1)


def _mix_fwd(z, o, ln_g, ln_b, w_s, b_col, attn_g, gm_g, n_heads, *, name, tr=256):
    s = z.shape[0]
    dg = n_heads * HEAD_DIM
    tr = _tile(s, tr, CHUNK)
    n_chunks = tr // CHUNK

    def body(zu_ref, zv_ref, o_ref, lg_ref, lb_ref, w_ref, bcol_ref, ag_ref, gg_ref, out_ref):
        u = _gelu(zu_ref[...].astype(F32))
        xhat, _ = _layernorm_parts(_gelu(zv_ref[...].astype(F32)))
        vv = xhat * lg_ref[...] + lb_ref[...]
        gm = u * _spatial_mix(w_ref, bcol_ref, vv.astype(BF16), n_heads, n_chunks)
        rg = lax.rsqrt(jnp.mean(gm * gm, axis=-1, keepdims=True) + EPS)
        ov = o_ref[...].astype(F32)
        ra = lax.rsqrt(jnp.mean(ov * ov, axis=-1, keepdims=True) + EPS)
        out_ref[:, :dg] = ((ov * ra) * ag_ref[...]).astype(BF16)
        out_ref[:, dg:] = ((gm * rg) * gg_ref[...]).astype(BF16)

    vec = pl.BlockSpec((1, dg), lambda i: (0, 0))
    return pl.pallas_call(
        body, name=name, out_shape=jax.ShapeDtypeStruct((s, 2 * dg), BF16), grid=(s // tr,),
        in_specs=[pl.BlockSpec((tr, dg), lambda i: (i, 3)), pl.BlockSpec((tr, dg), lambda i: (i, 4)),
                  pl.BlockSpec((tr, dg), lambda i: (i, 0)), vec, vec,
                  pl.BlockSpec((n_heads, CHUNK, CHUNK), lambda i: (0, 0, 0)),
                  pl.BlockSpec((CHUNK, n_heads), lambda i: (0, 0)), vec, vec],
        out_specs=pl.BlockSpec((tr, 2 * dg), lambda i: (i, 0)),
        compiler_params=_params("parallel"),
    )(z, z, o, ln_g, ln_b, w_s, b_col, attn_g, gm_g)


def _mix_bwd(z, o, d_merged, ln_g, ln_b, w_s, b_col, attn_g, gm_g, n_heads, *, name, tr=256):
    s = z.shape[0]
    dg = n_heads * HEAD_DIM
    tr = _tile(s, tr, CHUNK)
    n_chunks = tr // CHUNK

    def body(zu_ref, zv_ref, o_ref, dm_ref, lg_ref, lb_ref, w_ref, bcol_ref, ag_ref, gg_ref,
             do_ref, dzu_ref, dzv_ref, dw_ref, dbcol_ref, dlg_ref, dlb_ref, dag_ref, dgg_ref):
        @pl.when(pl.program_id(0) == 0)
        def _():
            for ref in (dw_ref, dbcol_ref, dlg_ref, dlb_ref, dag_ref, dgg_ref):
                ref[...] = jnp.zeros_like(ref)

        d_o, dag_rows = _rms_bwd_rows(dm_ref[:, :dg], o_ref[...].astype(F32), ag_ref[...])
        do_ref[...] = d_o.astype(BF16)
        dag_ref[...] += jnp.sum(dag_rows, axis=0, keepdims=True)

        u, du_dz = _gelu_and_grad(zu_ref[...].astype(F32))
        gv, dgv_dz = _gelu_and_grad(zv_ref[...].astype(F32))
        xhat, rs = _layernorm_parts(gv)
        lg = lg_ref[...]
        vv_b = (xhat * lg + lb_ref[...]).astype(BF16)
        mix = _spatial_mix(w_ref, bcol_ref, vv_b, n_heads, n_chunks)
        gm = u * mix
        d_gm, dgg_rows = _rms_bwd_rows(dm_ref[:, dg:], gm, gg_ref[...])
        dgg_ref[...] += jnp.sum(dgg_rows, axis=0, keepdims=True)
        dzu_ref[...] = ((d_gm * mix) * du_dz).astype(BF16)
        d_mix = d_gm * u
        d_mix_b = d_mix.astype(BF16)

        tril = _causal_mask((0, CHUNK), (0, CHUNK))
        lane = lax.broadcasted_iota(jnp.int32, (CHUNK, n_heads), 1)
        cols = []
        db = jnp.zeros((CHUNK, n_heads), F32)
        for hh in range(n_heads):
            wc = jnp.where(tril, w_ref[hh], 0.0).astype(BF16)
            lanes = slice(hh * HEAD_DIM, (hh + 1) * HEAD_DIM)
            dw = jnp.zeros((CHUNK, CHUNK), F32)
            dmix_sum = jnp.zeros((CHUNK, HEAD_DIM), F32)
            rows = []
            for c in range(n_chunks):
                rws = slice(c * CHUNK, (c + 1) * CHUNK)
                dmb = d_mix_b[rws, lanes]
                dw += lax.dot_general(dmb, vv_b[rws, lanes], (((1,), (1,)), ((), ())), preferred_element_type=F32)
                rows.append(lax.dot_general(wc, dmb, (((0,), (0,)), ((), ())), preferred_element_type=F32))
                dmix_sum += d_mix[rws, lanes]
            dw_ref[hh] += jnp.where(tril, dw, 0.0)
            db += jnp.where(lane == hh, jnp.sum(dmix_sum, axis=-1, keepdims=True), 0.0)
            cols.append(jnp.concatenate(rows, axis=0))
        dbcol_ref[...] += db
        d_vv = jnp.concatenate(cols, axis=1)

        dlg_ref[...] += jnp.sum(d_vv * xhat, axis=0, keepdims=True)
        dlb_ref[...] += jnp.sum(d_vv, axis=0, keepdims=True)
        d_xhat = d_vv * lg
        d_gv = rs * (d_xhat - jnp.mean(d_xhat, axis=-1, keepdims=True)
                     - xhat * jnp.mean(d_xhat * xhat, axis=-1, keepdims=True))
        dzv_ref[...] = (d_gv * dgv_dz).astype(BF16)

    vec = pl.BlockSpec((1, dg), lambda i: (0, 0))
    wspec = pl.BlockSpec((n_heads, CHUNK, CHUNK), lambda i: (0, 0, 0))
    bspec = pl.BlockSpec((CHUNK, n_heads), lambda i: (0, 0))
    rowb = pl.BlockSpec((tr, dg), lambda i: (i, 0))
    act = jax.ShapeDtypeStruct((s, dg), BF16)
    vshape = jax.ShapeDtypeStruct((1, dg), F32)
    return pl.pallas_call(
        body, name=name,
        out_shape=(act, act, act, jax.ShapeDtypeStruct((n_heads, CHUNK, CHUNK), F32),
                   jax.ShapeDtypeStruct((CHUNK, n_heads), F32), vshape, vshape, vshape, vshape),
        grid=(s // tr,),
        in_specs=[pl.BlockSpec((tr, dg), lambda i: (i, 3)), pl.BlockSpec((tr, dg), lambda i: (i, 4)),
                  rowb, pl.BlockSpec((tr, 2 * dg), lambda i: (i, 0)), vec, vec, wspec, bspec, vec, vec],
        out_specs=(rowb, rowb, rowb, wspec, bspec, vec, vec, vec, vec),
        compiler_params=_params("arbitrary"),
    )(z, z, o, d_merged, ln_g, ln_b, w_s, b_col, attn_g, gm_g)


def _place():
    x, y, c = lax.axis_index("x"), lax.axis_index("y"), lax.axis_index("c")
    other_chips = [(1 - x, y), (x, 1 - y), (1 - x, 1 - y)]
    return x, y, c, other_chips


def _remote(src, dst, send_sem, recv_sem, to):
    return pltpu.make_async_remote_copy(src_ref=src, dst_ref=dst, send_sem=send_sem, recv_sem=recv_sem,
                                        device_id=to, device_id_type=MESH)


def _cast_into_slot(w, place, *, name, phases=()):
    rows, cols = w.shape
    tr, tc = _rc_tile(rows, cols)

    def body(place_ref, w_ref, o_ref):
        o_ref[...] = w_ref[...].astype(BF16)

    return _only(_call(
        body, name=name, n_prefetch=1, grid=(rows // tr, cols // tc),
        in_specs=[pl.BlockSpec((tr, tc), lambda i, j, pr: (i, j))],
        out_specs=[pl.BlockSpec((None, tr, tc), lambda i, j, pr: (pr[0], i, j))],
        out_shape=[jax.ShapeDtypeStruct((N_CHIPS, rows, cols), BF16)], operands=[place, w],
        semantics=("parallel", "parallel"), phases=phases))


def _casts_and_norm(weights, place, x, g, *, name, rows=256, phases=()):
    cols = x.shape[1]
    jobs = [w.shape[0] // rows for w in weights] + [x.shape[0] // rows]
    assert all(w.shape[1] == cols and w.shape[0] % rows == 0 for w in weights) and x.shape[0] % rows == 0
    first = [sum(jobs[:k]) for k in range(len(jobs))]

    def strip(k):
        return lambda t: jnp.clip(t - first[k], 0, jobs[k] - 1)

    def body(place_ref, *refs):
        n = len(weights)
        w_refs, x_ref, g_ref, outs = refs[:n], refs[n], refs[n + 1], refs[n + 2:]
        t = pl.program_id(0)
        for k in range(n):
            @pl.when(jnp.logical_and(t >= first[k], t < first[k] + jobs[k]))
            def _(k=k):
                outs[k][...] = w_refs[k][...].astype(BF16)

        @pl.when(t >= first[n])
        def _():
            xv = x_ref[...]
            r = lax.rsqrt(jnp.mean(xv * xv, axis=-1, keepdims=True) + EPS)
            outs[n][...] = ((xv * r) * g_ref[...]).astype(BF16)

    in_specs = [pl.BlockSpec((rows, cols), lambda t, pr, k=k: (strip(k)(t), 0)) for k in range(len(weights))]
    in_specs += [pl.BlockSpec((rows, cols), lambda t, pr: (strip(len(weights))(t), 0)),
                 pl.BlockSpec((1, cols), lambda t, pr: (0, 0))]
    out_specs = [pl.BlockSpec((None, rows, cols), lambda t, pr, k=k: (pr[0], strip(k)(t), 0)) for k in range(len(weights))]
    out_specs.append(pl.BlockSpec((rows, cols), lambda t, pr: (strip(len(weights))(t), 0)))
    out_shape = [jax.ShapeDtypeStruct((N_CHIPS,) + w.shape, BF16) for w in weights] + [jax.ShapeDtypeStruct(x.shape, BF16)]
    return _call(body, name=name, n_prefetch=1, grid=(sum(jobs),), in_specs=in_specs, out_specs=out_specs,
                 out_shape=out_shape, operands=[place, *weights, x, g], semantics=("arbitrary",), phases=phases)


def _exchange(phases, *, name):
    comm_in = [a for ph in phases for a in ph.arrays]
    comm_out = [jax.ShapeDtypeStruct(s.shape, s.dtype) for ph in phases for s in (ph.arrays if ph.in_place else ph.out_shapes)]
    aliases, at_in, at_out = {}, 0, 0
    for ph in phases:
        if ph.in_place:
            aliases.update({at_in + r: at_out + r for r in range(len(ph.arrays))})
        at_in, at_out = at_in + len(ph.arrays), at_out + ph.n_out
    n_sems = sum(ph.n_sems for ph in phases)

    def body(*refs):
        cin, cout = refs[:len(comm_in)], refs[len(comm_in):len(comm_in) + len(comm_out)]
        send_sems, recv_sems = refs[len(comm_in) + len(comm_out):]
        _run_phases(phases, ("start", "finish"), cin, cout, send_sems, recv_sems)

    return pl.pallas_call(
        body, name=name, out_shape=tuple(comm_out), in_specs=[ANY] * len(comm_in), out_specs=tuple([ANY] * len(comm_out)),
        input_output_aliases=aliases,
        scratch_shapes=[pltpu.SemaphoreType.DMA((n_sems,)), pltpu.SemaphoreType.DMA((n_sems,))],
    )(*comm_in)


GATHER_PARTS = 4


def _gather(bufs, stage, part=(0, GATHER_PARTS)):
    n = 3 * len(bufs)
    lo, hi = part

    def copies(outs, send, recv, d2d, incoming):
        x, y, c, chips = _place()
        for t, buf in enumerate(outs):
            half = buf.shape[2] // 2
            piece = half // GATHER_PARTS
            for k, (cx, cy) in enumerate(chips):
                i = 3 * t + k + (n if (d2d and stage == "both") else 0)
                cols = pl.ds(((1 - c) if (d2d and incoming) else c) * half + lo * piece, (hi - lo) * piece)
                blk = buf.at[(2 * cx + cy) if (d2d or incoming) else (2 * x + y), :, cols]
                yield _remote(blk, blk, send(i), recv(i), (x, y, 1 - c) if d2d else (cx, cy, c))

    def start(ins, outs, send, recv):
        for cp in copies(outs, send, recv, stage == "d2d", False):
            cp.start()

    def finish(ins, outs, send, recv):
        if stage == "both":
            for arrival, onward in zip(copies(outs, send, recv, False, True), copies(outs, send, recv, True, False)):
                arrival.wait_recv()
                onward.start()
        for cp in copies(outs, send, recv, stage != "ici", True):
            cp.wait_recv()
        for d2d in ((False, True) if stage == "both" else (stage == "d2d",)):
            for cp in copies(outs, send, recv, d2d, False):
                cp.wait_send()

    return _Phase(bufs, [], True, (2 if stage == "both" else 1) * n, start, finish)


def _merge(first, second):
    n_first = first.n_sems

    def later(sem):
        return lambda i: sem(n_first + i)

    def start(ins, outs, send, recv):
        first.start(ins, outs, send, recv)
        second.start(ins, outs, later(send), later(recv))

    def finish(ins, outs, send, recv):
        first.finish(ins, outs, send, recv)
        second.finish(ins, outs, later(send), later(recv))

    return _Phase(first.arrays, [], True, n_first + second.n_sems, start, finish)


def _swap_halves(grads):
    def copies(ins, outs, send, recv):
        x, y, c, _ = _place()
        for t, g in enumerate(ins):
            half = g.shape[2] // 2
            yield _remote(g.at[:, :, pl.ds((1 - c) * half, half)], outs[t], send(t), recv(t), (x, y, 1 - c))

    def start(ins, outs, send, recv):
        for cp in copies(ins, outs, send, recv):
            cp.start()

    def finish(ins, outs, send, recv):
        for cp in copies(ins, outs, send, recv):
            cp.wait()

    shapes = [jax.ShapeDtypeStruct((a.shape[0], a.shape[1], a.shape[2] // 2), a.dtype) for a in grads]
    return _Phase(grads, shapes, False, len(grads), start, finish)


def _add_halves(grad, received, place, *, name):
    ns, rows, half = received.shape
    tr, tc = _rc_tile(rows, half, pref_rows=1024)
    per = half // tc

    def body(place_ref, g_ref, r_ref, o_ref):
        o_ref[...] = (g_ref[...].astype(F32) + r_ref[...].astype(F32)).astype(BF16)

    grid_spec = pltpu.PrefetchScalarGridSpec(
        num_scalar_prefetch=1, grid=(ns, rows // tr, per),
        in_specs=[pl.BlockSpec((None, tr, tc), lambda s, i, j, pr: (s, i, pr[1] * per + j)),
                  pl.BlockSpec((None, tr, tc), lambda s, i, j, pr: (s, i, j))],
        out_specs=pl.BlockSpec((None, tr, tc), lambda s, i, j, pr: (s, i, j)),
    )
    return pl.pallas_call(
        body, name=name, grid_spec=grid_spec, out_shape=jax.ShapeDtypeStruct(received.shape, BF16),
        compiler_params=_params("parallel", "parallel", "parallel"),
    )(place, grad, received)


def _send_partials(parts, piece=(0, 1)):
    k_th, n_pieces = piece

    def cols(part):
        width = part.shape[2] // n_pieces
        return pl.ds(k_th * width, width)

    def start(ins, outs, send, recv):
        x, y, c, chips = _place()
        for t, part in enumerate(ins):
            for k, (cx, cy) in enumerate(chips):
                _remote(part.at[2 * cx + cy, :, cols(part)], outs[t].at[2 * x + y],
                        send(3 * t + k), recv(3 * t + k), (cx, cy, c)).start()

    def finish(ins, outs, send, recv):
        x, y, c, chips = _place()
        for t, part in enumerate(ins):
            for k, (cx, cy) in enumerate(chips):
                slot = outs[t].at[2 * cx + cy]
                _remote(slot, slot, send(3 * t + k), recv(3 * t + k), (cx, cy, c)).wait_recv()
        for t, part in enumerate(ins):
            for k, (cx, cy) in enumerate(chips):
                sent = part.at[2 * cx + cy, :, cols(part)]
                _remote(sent, sent, send(3 * t + k), recv(3 * t + k), (cx, cy, c)).wait_send()

    shapes = [jax.ShapeDtypeStruct(a.shape[:2] + (a.shape[2] // n_pieces,), a.dtype) for a in parts]
    return _Phase(parts, shapes, False, 3 * len(parts), start, finish)


def _sum_chips(parts, slots, place, *, name, piece=(0, 1), into=None):
    ns, rows, width = slots.shape
    k_th, n_pieces = piece
    half = width * n_pieces
    tr, tc = _rc_tile(rows, width, pref_rows=512)
    per = width // tc

    def body(place_ref, p_ref, s_ref, *rest):
        acc = p_ref[...].astype(F32)
        for k in range(ns):
            acc = acc + jnp.where(place_ref[0] == k, 0.0, s_ref[k].astype(F32))
        rest[-1][...] = acc

    grid_spec = pltpu.PrefetchScalarGridSpec(
        num_scalar_prefetch=1, grid=(rows // tr, per),
        in_specs=[pl.BlockSpec((None, tr, tc), lambda i, j, pr: (pr[0], i, k_th * per + j)),
                  pl.BlockSpec((ns, tr, tc), lambda i, j, pr: (0, i, j))] + ([ANY] if into is not None else []),
        out_specs=pl.BlockSpec((tr, tc), lambda i, j, pr: (i, (pr[1] * n_pieces + k_th) * per + j)),
    )
    return pl.pallas_call(
        body, name=name, grid_spec=grid_spec, out_shape=jax.ShapeDtypeStruct((rows, 2 * half), F32),
        input_output_aliases={3: 0} if into is not None else {},
        compiler_params=_params("parallel", "parallel"),
    )(place, parts, slots, *([into] if into is not None else []))


def _join_halves(bufs):
    def copies(outs, send, recv, incoming):
        x, y, c, _ = _place()
        for t, buf in enumerate(outs):
            half = buf.shape[1] // 2
            cols = buf.at[:, pl.ds(((1 - c) if incoming else c) * half, half)]
            yield _remote(cols, cols, send(t), recv(t), (x, y, 1 - c))

    def start(ins, outs, send, recv):
        for cp in copies(outs, send, recv, False):
            cp.start()

    def finish(ins, outs, send, recv):
        for cp in copies(outs, send, recv, True):
            cp.wait_recv()
        for cp in copies(outs, send, recv, False):
            cp.wait_send()

    return _Phase(bufs, [], True, len(bufs), start, finish)


def _gather_small(buf):
    def slot(out, px, py, pc):
        return out.at[4 * px + 2 * py + pc]

    def start(ins, outs, send, recv):
        x, y, c, chips = _place()
        mine = slot(outs[0], x, y, c)
        _remote(ins[0], mine, send(0), recv(0), (x, y, 1 - c)).start()
        for k, (cx, cy) in enumerate(chips):
            _remote(ins[0], mine, send(1 + k), recv(1 + k), (cx, cy, c)).start()

    def finish(ins, outs, send, recv):
        x, y, c, chips = _place()
        sibling = (x, y, 1 - c)
        for k, (cx, cy) in enumerate(chips):
            arrived = slot(outs[0], cx, cy, c)
            _remote(arrived, arrived, send(1 + k), recv(1 + k), sibling).wait_recv()
            _remote(arrived, arrived, send(4 + k), recv(4 + k), sibling).start()
        theirs = slot(outs[0], x, y, 1 - c)
        _remote(theirs, theirs, send(0), recv(0), sibling).wait_recv()
        for k, (cx, cy) in enumerate(chips):
            passed = slot(outs[0], cx, cy, 1 - c)
            _remote(passed, passed, send(4 + k), recv(4 + k), sibling).wait_recv()
        for i in range(7):
            _remote(ins[0], ins[0], send(i), recv(i), sibling).wait_send()

    return _Phase([buf], [jax.ShapeDtypeStruct((N_DEV,) + buf.shape, buf.dtype)], False, 7, start, finish)


def _adamw_math(w, g, m, v):
    m = ADAM_B1 * m + (1.0 - ADAM_B1) * g
    v = ADAM_B2 * v + (1.0 - ADAM_B2) * (g * g)
    m_hat = m / (1.0 - ADAM_B1 ** ADAM_STEP)
    v_hat = v / (1.0 - ADAM_B2 ** ADAM_STEP)
    delta = -ADAM_LR * (m_hat / (jnp.sqrt(v_hat) + ADAM_EPS) + ADAM_WD * w)
    return delta, m, v


def _adamw(w, g, m, v, *, name):
    rows, cols = w.shape
    tr, tc = _rc_tile(rows, cols)

    def body(w_ref, g_ref, m_ref, v_ref, go_ref, d_ref, mo_ref, vo_ref):
        g = g_ref[...]
        go_ref[...] = g
        d_ref[...], mo_ref[...], vo_ref[...] = _adamw_math(w_ref[...], g, m_ref[...], v_ref[...])

    blk = pl.BlockSpec((tr, tc), lambda i, j: (i, j))
    shape = jax.ShapeDtypeStruct((rows, cols), F32)
    return pl.pallas_call(
        body, name=name, out_shape=(shape, shape, shape, shape), grid=(rows // tr, cols // tc),
        in_specs=[blk] * 4, out_specs=(blk, blk, blk, blk), compiler_params=_params("parallel", "parallel"),
    )(w, g, m, v)


def _adamw_small(gathered, own, place, w, m, v, *, name):
    nd = gathered.shape[0]

    def body(place_ref, gs_ref, own_ref, w_ref, m_ref, v_ref, g_ref, d_ref, mo_ref, vo_ref):
        me = 2 * place_ref[0] + place_ref[1]
        g = jnp.zeros(own_ref.shape, F32)
        for k in range(nd):
            g = g + jnp.where(me == k, own_ref[...], gs_ref[k])
        g_ref[...] = g
        d_ref[...], mo_ref[...], vo_ref[...] = _adamw_math(w_ref[...], g, m_ref[...], v_ref[...])

    whole = pl.BlockSpec(w.shape, lambda i, pr: (0, 0))
    grid_spec = pltpu.PrefetchScalarGridSpec(
        num_scalar_prefetch=1, grid=(1,),
        in_specs=[pl.BlockSpec(gathered.shape, lambda i, pr: (0, 0, 0)), whole, whole, whole, whole],
        out_specs=(whole, whole, whole, whole))
    shape = jax.ShapeDtypeStruct(w.shape, F32)
    return pl.pallas_call(body, name=name, grid_spec=grid_spec, out_shape=(shape, shape, shape, shape),
                          compiler_params=_params("arbitrary"))(place, gathered, own, w, m, v)


def _pack(parts):
    flat = jnp.concatenate([p.reshape(-1).astype(F32) for p in parts])
    rows = -(-flat.shape[0] // (8 * LANES)) * 8
    return jnp.pad(flat, (0, rows * LANES - flat.shape[0])).reshape(rows, LANES)


def _unpack(buf, shapes):
    flat = buf.reshape(-1)
    out, pos = [], 0
    for shp in shapes:
        size = int(np.prod(shp))
        out.append(flat[pos:pos + size].reshape(shp))
        pos += size
    return out


ROW_BLOCK = 256


def _realign_rows(sources, segments, out_shape, *, name):
    n_slots, rows, cols = out_shape
    n_src = len(sources)
    per_slot = -(-rows // ROW_BLOCK)
    table = np.zeros((6, n_slots * per_slot, n_src), np.int32)
    for so in range(n_slots):
        for first, last, src, src_slot, src_row in segments[so]:
            for blk in range(first // ROW_BLOCK, (last - 1) // ROW_BLOCK + 1):
                lo, hi = max(first, blk * ROW_BLOCK), min(last, (blk + 1) * ROW_BLOCK)
                base = src_row + (blk * ROW_BLOCK - first)
                m0 = (base + lo - blk * ROW_BLOCK) // ROW_BLOCK
                at = so * per_slot + blk
                assert table[4, at, src] == 0, "two segments of one block share a source operand"
                table[:, at, src] = (src_slot, m0, base - m0 * ROW_BLOCK, lo - blk * ROW_BLOCK, hi - blk * ROW_BLOCK,
                                     min(2 * ROW_BLOCK, sources[src].shape[1] - m0 * ROW_BLOCK))
    last_block = [-(-a.shape[1] // ROW_BLOCK) - 1 for a in sources]

    def body(slot_ref, blk_ref, off_ref, lo_ref, hi_ref, valid_ref, *refs):
        o_ref, acc = refs[2 * n_src], refs[2 * n_src + 1]
        at = (pl.program_id(0) * per_slot + pl.program_id(1)) * n_src
        acc[...] = jnp.zeros_like(acc)
        for p in range(n_src):
            @pl.when(hi_ref[at + p] > lo_ref[at + p])
            def _():
                two = jnp.concatenate([refs[2 * p][...], refs[2 * p + 1][...]], axis=0)
                src_row = lax.broadcasted_iota(jnp.int32, two.shape, 0)
                two = jnp.where(src_row < valid_ref[at + p], two, jnp.zeros_like(two))
                r = lax.broadcasted_iota(jnp.int32, (ROW_BLOCK, 2 * ROW_BLOCK), 0)
                c = lax.broadcasted_iota(jnp.int32, (ROW_BLOCK, 2 * ROW_BLOCK), 1)
                place = (c == r + off_ref[at + p]) & (r >= lo_ref[at + p]) & (r < hi_ref[at + p])
                acc[...] += jnp.dot(place.astype(two.dtype), two, preferred_element_type=F32)
        o_ref[...] = acc[...].astype(o_ref.dtype)

    def src_spec(p, second):
        def index(so, i, slot_r, blk_r, off_r, lo_r, hi_r, valid_r):
            at = (so * per_slot + i) * n_src + p
            return slot_r[at], jnp.minimum(blk_r[at] + second, last_block[p]), 0
        return pl.BlockSpec((None, ROW_BLOCK, cols), index)

    grid_spec = pltpu.PrefetchScalarGridSpec(
        num_scalar_prefetch=6, grid=(n_slots, per_slot),
        in_specs=[src_spec(p, second) for p in range(n_src) for second in (0, 1)],
        out_specs=pl.BlockSpec((None, ROW_BLOCK, cols), lambda so, i, *_: (so, i, 0)),
        scratch_shapes=[pltpu.VMEM((ROW_BLOCK, cols), F32)],
    )
    flat = [jnp.asarray(table[k].reshape(-1)) for k in range(6)]
    return pl.pallas_call(
        body, name=name, grid_spec=grid_spec, out_shape=jax.ShapeDtypeStruct(out_shape, sources[0].dtype),
        compiler_params=_params("parallel", "arbitrary"),
    )(*flat, *[a for a in sources for _ in (0, 1)])


def _shard_rows(g, lo, hi):
    rs = g.shape[1]
    pieces = []
    for j in range(g.shape[0]):
        a, b = max(lo, j * rs), min(hi, (j + 1) * rs)
        if a < b:
            pieces.append(g[j, a - j * rs:b - j * rs])
    return pieces


def kernel(x, norm_mix_g, w_in, b_f, gmlp_ln_g, gmlp_ln_b, w_s, b_s, attn_out_g, gmlp_out_g, w_out, norm_ffn_g, w_ff1, w_ff2, norm_final_g, loss_target, m_norm_mix_g, m_w_in, m_b_f, m_gmlp_ln_g, m_gmlp_ln_b, m_w_s, m_b_s, m_attn_out_g, m_gmlp_out_g, m_w_out, m_norm_ffn_g, m_w_ff1, m_w_ff2, m_norm_final_g, v_norm_mix_g, v_w_in, v_b_f, v_gmlp_ln_g, v_gmlp_ln_b, v_w_s, v_b_s, v_attn_out_g, v_gmlp_out_g, v_w_out, v_norm_ffn_g, v_w_ff1, v_w_ff2, v_norm_final_g):
    seq, d_model = x.shape[1], x.shape[2]
    d_attn = d_model // 2
    n_heads = d_attn // HEAD_DIM
    qkv = 3 * d_attn
    shard_cols = w_in.shape[2]
    assert N_CHIPS * shard_cols == qkv + n_heads + 2 * d_attn
    xs = x.reshape(seq, d_model)
    target = loss_target.reshape(seq, d_model)

    place = jnp.stack([2 * lax.axis_index("x") + lax.axis_index("y"), lax.axis_index("c")]).astype(jnp.int32)
    names = ["w_in", "w_out", "w_ff1", "w_ff2"]
    wt_in, mt_in, vt_in = w_in[0].T, m_w_in[0].T, v_w_in[0].T
    b_in, _ = _cast_into_slot(wt_in, place, name="cast_w_in")
    (b_out, b_ff1, b_ff2, h), (g_in,) = _casts_and_norm(
        [w_out[0], w_ff1[0], w_ff2[0]], place, xs, norm_mix_g, name="casts_and_norm_mix", phases=[_gather([b_in], "both")])
    n_cols = N_CHIPS * shard_cols
    gate_slot, gate_row = divmod(qkv, shard_cols)
    assert gate_row + n_heads <= shard_cols
    pieces = []
    for j in range(N_CHIPS):
        if j == gate_slot:
            pieces += [(j, 0, gate_row), (j, gate_row + n_heads, shard_cols - gate_row - n_heads)]
        else:
            pieces.append((j, 0, shard_cols))
    fwd_segments, at = [[]], 0
    for order, (j, src_row, size) in enumerate(pieces):
        fwd_segments[0].append((at, at + size, order % 3, j, src_row))
        at += size
    wt_main = _realign_rows([g_in] * 3, fwd_segments, (1, n_cols - n_heads, d_model), name="w_in_rows")[0]
    wt_f = jnp.pad(jnp.concatenate(_shard_rows(g_in, qkv, qkv + n_heads), axis=0), ((0, LANES - n_heads), (0, 0)))
    b_f_pad = jnp.pad(b_f, ((0, 0), (0, LANES - n_heads)))
    b_col = b_s[0].T

    first, rest = (0, 1), (1, GATHER_PARTS)
    z, (b_out, b_ff1) = _matmul(h, wt_main, name="in_proj", out_dtype=BF16, trans_b=True, tm=2048,
                                phases=[_gather([b_out], "ici"), _gather([b_ff1], "ici", first)])
    zb, f_cum = _forget_fwd(h, wt_f, b_f_pad, name="forget_fwd")
    f_row = f_cum[:, :n_heads].T[:, None, :]
    (o, lse2), (b_ff1, b_out) = _attn_fwd(z, f_row, n_heads, name="attn_fwd",
                                          phases=[_gather([b_ff1], "ici", rest), _gather([b_out], "d2d")])
    merged = _mix_fwd(z, o, gmlp_ln_g, gmlp_ln_b, w_s[0], b_col, attn_out_g, gmlp_out_g, n_heads, name="mix_fwd")
    w_out_full = b_out.reshape(2 * d_attn, d_model)
    x1, (b_ff1, b_ff2) = _matmul(merged, w_out_full, name="out_proj", out_dtype=F32, residual=xs,
                                 phases=[_gather([b_ff1], "d2d"), _gather([b_ff2], "ici", first)])
    h2, _ = _rmsnorm_fwd(x1, norm_ffn_g, name="norm_ffn")
    a, (b_ff2,) = _matmul(h2, b_ff1, name="ff1", out_dtype=BF16, relu=True, b_sharded=True, tm=2048,
                          phases=[_merge(_gather([b_ff2], "both", rest), _gather([b_ff2], "d2d", first))])
    w_ff2_full = b_ff2.reshape(N_CHIPS * b_ff2.shape[1], d_model)
    x2, _ = _matmul(a, w_ff2_full, name="ff2", out_dtype=F32, square_lhs=True, residual=x1)
    dx2, dx2_b, dg_final, loss = _loss_and_final_bwd(x2, target, norm_final_g.reshape(1, d_model), name="loss_head")

    def pair_sum(g, r, nm):
        return _add_halves(g, r, place, name="grads_pair_sum_" + nm)

    def chip_sum(p, q, nm, **piece):
        return _sum_chips(p, q, place, name="grads_chip_sum_" + nm, **piece)

    dw_ff2, _ = _matmul(a, dx2_b, name="ff2_dw", out_dtype=BF16, trans_a=True, square_lhs=True)
    dw_ff2 = dw_ff2.reshape(N_CHIPS, -1, d_model)
    da, (r_ff2,) = _matmul(dx2_b, w_ff2_full, name="ff2_dlhs", out_dtype=BF16, trans_b=True, scale2_by=a, tm=2048,
                           phases=[_swap_halves([dw_ff2])])
    ps_ff2 = pair_sum(dw_ff2, r_ff2, "w_ff2")
    dh2, (q_ff2a,) = _matmul(da, b_ff1, name="ff1_dlhs", out_dtype=F32, trans_b=True, b_sharded=True,
                             phases=[_send_partials([ps_ff2], (0, 2))])
    dw_ff1, (q_ff2b,) = _matmul(h2, da, name="ff1_dw", out_dtype=BF16, trans_a=True, out_sharded=True, tk=seq,
                                phases=[_send_partials([ps_ff2], (1, 2))])
    g_ff2 = chip_sum(ps_ff2, q_ff2a, "w_ff2_a", piece=(0, 2))
    g_ff2 = chip_sum(ps_ff2, q_ff2b, "w_ff2_b", piece=(1, 2), into=g_ff2)
    (dx1, dg_ffn, dx1_b), (g_ff2,) = _rmsnorm_bwd(dh2, x1, dx2, norm_ffn_g, name="norm_ffn_bwd", rounded_copy=True,
                                                   phases=[_join_halves([g_ff2])])
    dw_out, _ = _matmul(merged, dx1_b, name="out_proj_dw", out_dtype=BF16, trans_a=True, tk=seq)
    dw_out = dw_out.reshape(N_CHIPS, -1, d_model)
    d_merged, (r_ff1, r_out) = _matmul(dx1_b, w_out_full, name="out_proj_dlhs", out_dtype=F32, trans_b=True,
                                       phases=[_swap_halves([dw_ff1, dw_out])])
    ps_ff1, ps_out = pair_sum(dw_ff1, r_ff1, "w_ff1"), pair_sum(dw_out, r_out, "w_out")
    d_o, dzu, dzv, dw_s, db_col, dlg, dlb, dag, dgg = _mix_bwd(
        z, o, d_merged, gmlp_ln_g, gmlp_ln_b, w_s[0], b_col, attn_out_g, gmlp_out_g, n_heads, name="mix_bwd")
    (dq, dk, dv, d_f_key, d_f_query), (q_ff1, q_out) = _attn_bwd(
        z, o, d_o, lse2, f_row, n_heads, name="attn_bwd", phases=[_send_partials([ps_ff1, ps_out])])
    g_ff1, g_out = chip_sum(ps_ff1, q_ff1, "w_ff1"), chip_sum(ps_out, q_out, "w_out")
    d_f = d_f_key.reshape(n_heads, seq) + d_f_query.reshape(n_heads, seq)
    d_f_pad = jnp.pad(d_f.T, ((0, 0), (0, LANES - n_heads)))
    dzf, db_f = _forget_bwd(d_f_pad, zb, name="forget_bwd")
    dz = jnp.concatenate([dq, dk, dv, dzu, dzv], axis=1)
    early_g = _pack([db_f[:, :n_heads], dlg, dlb, dw_s, db_col.T, dag, dgg, dg_ffn, dg_final])
    dwt_main, (g_ff1, g_out, early_all) = _matmul(dz, h, name="in_proj_dw", out_dtype=BF16, trans_a=True, tk=seq,
                                                  phases=[_join_halves([g_ff1, g_out]), _gather_small(early_g)])
    dwt_f, _ = _matmul(dzf, h, name="gate_dw", out_dtype=BF16, trans_a=True)
    bwd_segments = []
    for j in range(N_CHIPS):
        first = j * shard_cols
        if j < gate_slot:
            bwd_segments.append([(0, shard_cols, 0, 0, first)])
        elif j > gate_slot:
            bwd_segments.append([(0, shard_cols, 0, 0, first - n_heads)])
        else:
            bwd_segments.append([(0, gate_row, 0, 0, first), (gate_row, gate_row + n_heads, 1, 0, 0),
                                 (gate_row + n_heads, shard_cols, 2, 0, qkv)])
    dw_in = _realign_rows([dwt_main[None], dwt_f[None], dwt_main[None]], bwd_segments,
                          (N_CHIPS, shard_cols, d_model), name="dw_in_rows")
    dh_gate, (r_in,) = _matmul(dzf, wt_f, name="gate_dlhs", out_dtype=F32, phases=[_swap_halves([dw_in])])
    ps_in = pair_sum(dw_in, r_in, "w_in")
    dh, (q_in,) = _matmul(dz, wt_main, name="in_proj_dlhs", out_dtype=F32, residual=dh_gate, tk=2560,
                          phases=[_send_partials([ps_in])])
    g_in_sum = chip_sum(ps_in, q_in, "w_in")
    (grad_x, dg_mix), _ = _rmsnorm_bwd(dh, xs, dx1, norm_mix_g, name="norm_mix_bwd")
    late_g = _pack([dg_mix])
    g_in_sum, late_all = _exchange([_join_halves([g_in_sum]), _gather_small(late_g)], name="grads_join_w_in")

    big = {}
    for nm, g, w, m, v in zip(names, (g_in_sum, g_out, g_ff1, g_ff2), (wt_in, w_out[0], w_ff1[0], w_ff2[0]),
                              (mt_in, m_w_out[0], m_w_ff1[0], m_w_ff2[0]), (vt_in, v_w_out[0], v_w_ff1[0], v_w_ff2[0])):
        big[nm] = tuple((t.T if nm == "w_in" else t)[None] for t in _adamw(w, g, m, v, name="adamw_" + nm))

    small_params = dict(
        norm_mix_g=(norm_mix_g, m_norm_mix_g, v_norm_mix_g), b_f=(b_f, m_b_f, v_b_f),
        gmlp_ln_g=(gmlp_ln_g, m_gmlp_ln_g, v_gmlp_ln_g), gmlp_ln_b=(gmlp_ln_b, m_gmlp_ln_b, v_gmlp_ln_b),
        w_s=(w_s, m_w_s, v_w_s), b_s=(b_s, m_b_s, v_b_s), attn_out_g=(attn_out_g, m_attn_out_g, v_attn_out_g),
        gmlp_out_g=(gmlp_out_g, m_gmlp_out_g, v_gmlp_out_g), norm_ffn_g=(norm_ffn_g, m_norm_ffn_g, v_norm_ffn_g),
        norm_final_g=(norm_final_g, m_norm_final_g, v_norm_final_g))

    def small_step(group, grads_all, grads_own, label):
        w, m, v = ([small_params[nm][k] for nm in group] for k in range(3))
        packed = _adamw_small(grads_all, grads_own, place, _pack(w), _pack(m), _pack(v), name="adamw_small_" + label)
        parts = [_unpack(p, [a.shape for a in w]) for p in packed]
        return {nm: tuple(part[i] for part in parts) for i, nm in enumerate(group)}

    early = ["b_f", "gmlp_ln_g", "gmlp_ln_b", "w_s", "b_s", "attn_out_g", "gmlp_out_g", "norm_ffn_g", "norm_final_g"]
    small = {**small_step(early, early_all, early_g, "early"), **small_step(["norm_mix_g"], late_all, late_g, "late")}

    order = ["norm_mix_g", "w_in", "b_f", "gmlp_ln_g", "gmlp_ln_b", "w_s", "b_s", "attn_out_g", "gmlp_out_g", "w_out",
             "norm_ffn_g", "w_ff1", "w_ff2", "norm_final_g"]
    result = {**small, **big}
    total_loss = lax.psum(loss[0, 0], ("x", "y", "c"))
    outs = [total_loss, grad_x.reshape(x.shape)]
    for part in range(4):
        outs += [result[nm][part] for nm in order]
    return tuple(outs)
```

```python
import functools
import math

import numpy as np
import jax
import jax.numpy as jnp
from jax import lax
from jax.experimental import pallas as pl
from jax.experimental.pallas import tpu as pltpu

HEAD_DIM = 128
CHUNK = 128
EPS = 1e-6
LANES = 128
MXU_COLUMNS = 256
N_CHIPS = 4
N_DEV = 8
VMEM_LIMIT_BYTES = 56 * 1024 * 1024

ADAM_LR = 0.001
ADAM_B1 = 0.9
ADAM_B2 = 0.999
ADAM_EPS = 1e-08
ADAM_WD = 0.01
ADAM_STEP = 10

BF16 = jnp.bfloat16
F32 = jnp.float32
MESH = pl.DeviceIdType.MESH
ANY = pl.BlockSpec(memory_space=pl.ANY)
NEG_BIG = -1e30


def _params(*sem):
    return pltpu.CompilerParams(dimension_semantics=tuple(sem), vmem_limit_bytes=VMEM_LIMIT_BYTES)


def _tile(n, pref, unit):
    t = (min(pref, n) // unit) * unit
    while t >= unit:
        if n % t == 0:
            return t
        t -= unit
    return n


def _rc_tile(rows, cols, pref_rows=256, pref_cols=256):
    if rows % 16 == 0:
        return _tile(rows, pref_rows, 16), cols
    return rows, _tile(cols, pref_cols, LANES)


class _Phase:
    def __init__(self, arrays, out_shapes, in_place, n_sems, start, finish):
        self.arrays, self.out_shapes, self.in_place = list(arrays), list(out_shapes), in_place
        self.n_sems, self.start, self.finish = n_sems, start, finish

    @property
    def n_out(self):
        return len(self.arrays) if self.in_place else len(self.out_shapes)


def _run_phases(phases, steps, comm_in, comm_out, send_sems, recv_sems):
    at_in = at_out = at_sem = 0
    for ph in phases:
        for step in steps:
            getattr(ph, step)(comm_in[at_in:at_in + len(ph.arrays)], comm_out[at_out:at_out + ph.n_out],
                              lambda i, base=at_sem: send_sems.at[base + i], lambda i, base=at_sem: recv_sems.at[base + i])
        at_in, at_out, at_sem = at_in + len(ph.arrays), at_out + ph.n_out, at_sem + ph.n_sems


def _call(body, *, name, grid, in_specs, out_specs, out_shape, operands, semantics, scratch_shapes=(),
          n_prefetch=0, phases=()):
    in_specs, out_specs, out_shape = list(in_specs), list(out_specs), list(out_shape)
    scratch_shapes = list(scratch_shapes)
    n_in, n_out, n_scr = len(operands) - n_prefetch, len(out_shape), len(scratch_shapes)
    comm_in = [a for ph in phases for a in ph.arrays]
    comm_out = [jax.ShapeDtypeStruct(s.shape, s.dtype) for ph in phases
                for s in (ph.arrays if ph.in_place else ph.out_shapes)]
    aliases, at_in, at_out = {}, n_prefetch + n_in, n_out
    for ph in phases:
        if ph.in_place:
            aliases.update({at_in + r: at_out + r for r in range(len(ph.arrays))})
        at_in, at_out = at_in + len(ph.arrays), at_out + ph.n_out
    n_sems = sum(ph.n_sems for ph in phases)

    def hosted(*refs):
        pre, rest = refs[:n_prefetch], refs[n_prefetch:]
        ins, rest = rest[:n_in], rest[n_in:]
        cin, rest = rest[:len(comm_in)], rest[len(comm_in):]
        outs, rest = rest[:n_out], rest[n_out:]
        cout, rest = rest[:len(comm_out)], rest[len(comm_out):]
        scr = rest[:n_scr]
        if phases:
            send_sems, recv_sems = rest[n_scr:]
            ids = [pl.program_id(ax) for ax in range(len(grid))]
            first = functools.reduce(jnp.logical_and, [i == 0 for i in ids])
            last = functools.reduce(jnp.logical_and, [i == g - 1 for i, g in zip(ids, grid)])

            @pl.when(first)
            def _():
                _run_phases(phases, ("start",), cin, cout, send_sems, recv_sems)

        body(*pre, *ins, *outs, *scr)
        if phases:
            @pl.when(last)
            def _():
                _run_phases(phases, ("finish",), cin, cout, send_sems, recv_sems)

    all_in = in_specs + [ANY] * len(comm_in)
    all_out = out_specs + [ANY] * len(comm_out)
    all_scr = scratch_shapes + ([pltpu.SemaphoreType.DMA((n_sems,)), pltpu.SemaphoreType.DMA((n_sems,))] if phases else [])
    if phases:
        semantics = ("arbitrary",) * len(grid)
    kwargs = dict(name=name, out_shape=tuple(out_shape + comm_out), compiler_params=_params(*semantics),
                  input_output_aliases=aliases)
    if n_prefetch:
        kwargs["grid_spec"] = pltpu.PrefetchScalarGridSpec(
            num_scalar_prefetch=n_prefetch, grid=grid, in_specs=all_in, out_specs=tuple(all_out), scratch_shapes=all_scr)
    else:
        kwargs.update(grid=grid, in_specs=all_in, out_specs=tuple(all_out), scratch_shapes=all_scr)
    res = pl.pallas_call(hosted, **kwargs)(*operands, *comm_in)
    return tuple(res[:n_out]), tuple(res[n_out:])


def _only(results):
    outs, comm = results
    return outs[0] if len(outs) == 1 else outs, comm


def _matmul(a, b, *, name, out_dtype, trans_a=False, trans_b=False, tm=1024, tn=1024, tk=2048,
            square_lhs=False, relu=False, residual=None, scale2_by=None,
            b_sharded=False, out_sharded=False, phases=()):
    m, k = (a.shape[1], a.shape[0]) if trans_a else a.shape
    if b_sharded:
        if trans_b:
            n, ks = b.shape[1], b.shape[2]
            assert N_CHIPS * ks == k
        else:
            ns = b.shape[2]
            n = N_CHIPS * ns
            assert b.shape[1] == k
    else:
        n = b.shape[0] if trans_b else b.shape[1]
        assert (b.shape[1] if trans_b else b.shape[0]) == k
    tm = _tile(m, tm, 128)
    tn = _tile(n // N_CHIPS if (out_sharded or (b_sharded and not trans_b)) else n, tn, 128)
    tk = _tile(k // N_CHIPS if (b_sharded and trans_b) else k, tk, 128)
    nk = k // tk

    if trans_a:
        a_spec = pl.BlockSpec((tk, tm), lambda i, j, kk: (kk, i))
    else:
        a_spec = pl.BlockSpec((tm, tk), lambda i, j, kk: (i, kk))
    if b_sharded and trans_b:
        per = ks // tk
        assert per * tk == ks
        b_spec = pl.BlockSpec((None, tn, tk), lambda i, j, kk: (kk // per, j, kk % per))
    elif b_sharded:
        per = ns // tn
        assert per * tn == ns
        b_spec = pl.BlockSpec((None, tk, tn), lambda i, j, kk: (j // per, kk, j % per))
    elif trans_b:
        b_spec = pl.BlockSpec((tn, tk), lambda i, j, kk: (j, kk))
    else:
        b_spec = pl.BlockSpec((tk, tn), lambda i, j, kk: (kk, j))
    if out_sharded:
        ns_out = n // N_CHIPS
        per_o = ns_out // tn
        assert per_o * tn == ns_out
        out_shape = jax.ShapeDtypeStruct((N_CHIPS, m, ns_out), out_dtype)
        o_spec = pl.BlockSpec((None, tm, tn), lambda i, j, kk: (j // per_o, i, j % per_o))
    else:
        out_shape = jax.ShapeDtypeStruct((m, n), out_dtype)
        o_spec = pl.BlockSpec((tm, tn), lambda i, j, kk: (i, j))
    mn_spec = pl.BlockSpec((tm, tn), lambda i, j, kk: (i, j))

    operands, in_specs = [a, b], [a_spec, b_spec]
    if scale2_by is not None:
        operands.append(scale2_by)
        in_specs.append(mn_spec)
    if residual is not None:
        operands.append(residual)
        in_specs.append(mn_spec)
    dims = (((0 if trans_a else 1,), (1 if trans_b else 0,)), ((), ()))
    chunk = MXU_COLUMNS if tn % MXU_COLUMNS == 0 else tn

    def body(*refs):
        a_ref, b_ref = refs[0], refs[1]
        pos = 2
        scale_ref = res_ref = None
        if scale2_by is not None:
            scale_ref = refs[pos]
            pos += 1
        if residual is not None:
            res_ref = refs[pos]
            pos += 1
        o_ref = refs[pos]
        kk = pl.program_id(2)

        av = a_ref[...]
        if square_lhs:
            av = av.astype(F32)
            av = av * av
        av = av.astype(BF16)

        def finish(r, cols):
            if relu:
                r = jnp.maximum(r, 0.0)
            if scale_ref is not None:
                r = r * (2.0 * scale_ref[:, cols].astype(F32))
            if res_ref is not None:
                r = r + res_ref[:, cols].astype(F32)
            o_ref[:, cols] = r.astype(out_dtype)

        if nk == 1:
            for lo in range(0, tn, chunk):
                cols = slice(lo, lo + chunk)
                bv = (b_ref[cols, :] if trans_b else b_ref[:, cols]).astype(BF16)
                finish(lax.dot_general(av, bv, dims, preferred_element_type=F32), cols)
        else:
            acc_ref = refs[pos + 1]
            part = lax.dot_general(av, b_ref[...].astype(BF16), dims, preferred_element_type=F32)

            @pl.when(kk == 0)
            def _():
                acc_ref[...] = part

            @pl.when(jnp.logical_and(kk > 0, kk < nk - 1))
            def _():
                acc_ref[...] += part

            @pl.when(kk == nk - 1)
            def _():
                finish(acc_ref[...] + part, slice(None))

    return _only(_call(
        body, name=name, out_shape=[out_shape], grid=(m // tm, n // tn, nk),
        in_specs=in_specs, out_specs=[o_spec], operands=operands,
        scratch_shapes=[pltpu.VMEM((tm, tn), F32)] if nk > 1 else [],
        semantics=("parallel", "parallel", "arbitrary"), phases=phases))


def _rmsnorm_fwd(x, g, *, name, tr=512, phases=()):
    s, d = x.shape
    tr = _tile(s, tr, 8)

    def body(x_ref, g_ref, o_ref):
        xv = x_ref[...]
        r = lax.rsqrt(jnp.mean(xv * xv, axis=-1, keepdims=True) + EPS)
        o_ref[...] = ((xv * r) * g_ref[...]).astype(BF16)

    return _only(_call(
        body, name=name, out_shape=[jax.ShapeDtypeStruct((s, d), BF16)], grid=(s // tr,),
        in_specs=[pl.BlockSpec((tr, d), lambda i: (i, 0)), pl.BlockSpec((1, d), lambda i: (0, 0))],
        out_specs=[pl.BlockSpec((tr, d), lambda i: (i, 0))], operands=[x, g],
        semantics=("parallel",), phases=phases))


def _rms_bwd_rows(dy, xv, g):
    d = xv.shape[-1]
    r = lax.rsqrt(jnp.mean(xv * xv, axis=-1, keepdims=True) + EPS)
    gdy = dy * g
    dot = jnp.sum(gdy * xv, axis=-1, keepdims=True)
    dx = gdy * r - xv * (r * r * r) * (dot / d)
    return dx, dy * (xv * r)


def _rmsnorm_bwd(dy, x, res, g, *, name, tr=256, rounded_copy=False, phases=()):
    s, d = x.shape
    tr = _tile(s, tr, 8)

    def body(dy_ref, x_ref, res_ref, g_ref, dx_ref, dg_ref, *dxb_ref):
        @pl.when(pl.program_id(0) == 0)
        def _():
            dg_ref[...] = jnp.zeros_like(dg_ref)

        dx, dg_rows = _rms_bwd_rows(dy_ref[...].astype(F32), x_ref[...], g_ref[...])
        out = res_ref[...] + dx
        dx_ref[...] = out
        if rounded_copy:
            dxb_ref[0][...] = out.astype(BF16)
        dg_ref[...] += jnp.sum(dg_rows, axis=0, keepdims=True)

    row = pl.BlockSpec((tr, d), lambda i: (i, 0))
    vec = pl.BlockSpec((1, d), lambda i: (0, 0))
    extra = [jax.ShapeDtypeStruct((s, d), BF16)] if rounded_copy else []
    return _call(
        body, name=name,
        out_shape=[jax.ShapeDtypeStruct((s, d), F32), jax.ShapeDtypeStruct((1, d), F32)] + extra,
        grid=(s // tr,), in_specs=[row, row, row, vec], out_specs=[row, vec] + [row] * len(extra),
        operands=[dy, x, res, g], semantics=("arbitrary",), phases=phases)


def _loss_and_final_bwd(x2, target, g, *, name, tr=256):
    s, d = x2.shape
    tr = _tile(s, tr, 8)

    def body(x_ref, t_ref, g_ref, dx_ref, dxb_ref, dg_ref, loss_ref):
        @pl.when(pl.program_id(0) == 0)
        def _():
            dg_ref[...] = jnp.zeros_like(dg_ref)
            loss_ref[...] = jnp.zeros_like(loss_ref)

        xv, gv = x_ref[...], g_ref[...]
        r = lax.rsqrt(jnp.mean(xv * xv, axis=-1, keepdims=True) + EPS)
        err = (xv * r) * gv - t_ref[...]
        row_loss = jnp.mean(err * err, axis=-1, keepdims=True)
        loss_ref[...] += 0.5 * jnp.sum(row_loss, axis=0, keepdims=True)
        dx, dg_rows = _rms_bwd_rows(err / d, xv, gv)
        dx_ref[...] = dx
        dxb_ref[...] = dx.astype(BF16)
        dg_ref[...] += jnp.sum(dg_rows, axis=0, keepdims=True)

    row = pl.BlockSpec((tr, d), lambda i: (i, 0))
    vec = pl.BlockSpec((1, d), lambda i: (0, 0))
    one = pl.BlockSpec((1, 1), lambda i: (0, 0))
    return pl.pallas_call(
        body, name=name,
        out_shape=(jax.ShapeDtypeStruct((s, d), F32), jax.ShapeDtypeStruct((s, d), BF16),
                   jax.ShapeDtypeStruct((1, d), F32), jax.ShapeDtypeStruct((1, 1), F32)),
        grid=(s // tr,), in_specs=[row, row, vec], out_specs=(row, row, vec, one),
        compiler_params=_params("arbitrary"),
    )(x2, target, g)


def _tri_ones(n, lower):
    r = lax.broadcasted_iota(jnp.int32, (n, n), 0)
    c = lax.broadcasted_iota(jnp.int32, (n, n), 1)
    return jnp.where((c <= r) if lower else (c >= r), 1.0, 0.0).astype(F32)


def _forget_fwd(h, w_f, b_f, *, name, tr=256):
    s, d = h.shape
    tr = _tile(s, tr, 8)

    def body(h_ref, w_ref, b_ref, zb_ref, f_ref, carry):
        @pl.when(pl.program_id(0) == 0)
        def _():
            carry[...] = jnp.zeros_like(carry)

        zb = lax.dot_general(h_ref[...], w_ref[...], (((1,), (1,)), ((), ())), preferred_element_type=F32) + b_ref[...]
        zb_ref[...] = zb
        log_f = jnp.minimum(zb, 0.0) - jnp.log(1.0 + jnp.exp(-jnp.abs(zb)))
        run = jnp.dot(_tri_ones(tr, True), log_f, preferred_element_type=F32,
                      precision=lax.Precision.HIGHEST) + carry[...]
        f_ref[...] = run
        carry[...] = run[tr - 1:tr, :]

    row = pl.BlockSpec((tr, LANES), lambda i: (i, 0))
    return pl.pallas_call(
        body, name=name,
        out_shape=(jax.ShapeDtypeStruct((s, LANES), F32), jax.ShapeDtypeStruct((s, LANES), F32)),
        grid=(s // tr,),
        in_specs=[pl.BlockSpec((tr, d), lambda i: (i, 0)), pl.BlockSpec((LANES, d), lambda i: (0, 0)),
                  pl.BlockSpec((1, LANES), lambda i: (0, 0))],
        out_specs=(row, row), scratch_shapes=[pltpu.VMEM((1, LANES), F32)],
        compiler_params=_params("arbitrary"),
    )(h, w_f, b_f)


def _forget_bwd(d_f, zb, *, name, tr=256):
    s = zb.shape[0]
    tr = _tile(s, tr, 8)
    nb = s // tr

    def body(df_ref, zb_ref, dz_ref, db_ref, carry):
        @pl.when(pl.program_id(0) == 0)
        def _():
            carry[...] = jnp.zeros_like(carry)
            db_ref[...] = jnp.zeros_like(db_ref)

        run = jnp.dot(_tri_ones(tr, False), df_ref[...], preferred_element_type=F32,
                      precision=lax.Precision.HIGHEST) + carry[...]
        carry[...] = run[0:1, :]
        dz = run / (1.0 + jnp.exp(zb_ref[...]))
        dz_ref[...] = dz.astype(BF16)
        db_ref[...] += jnp.sum(dz, axis=0, keepdims=True)

    row = pl.BlockSpec((tr, LANES), lambda i: (nb - 1 - i, 0))
    return pl.pallas_call(
        body, name=name,
        out_shape=(jax.ShapeDtypeStruct((s, LANES), BF16), jax.ShapeDtypeStruct((1, LANES), F32)),
        grid=(nb,), in_specs=[row, row], out_specs=(row, pl.BlockSpec((1, LANES), lambda i: (0, 0))),
        scratch_shapes=[pltpu.VMEM((1, LANES), F32)],
        compiler_params=_params("arbitrary"),
    )(d_f, zb)


def _pairs(nblk, by_kv):
    if by_kv:
        pr = [(i, j) for j in range(nblk) for i in range(j, nblk)]
    else:
        pr = [(i, j) for i in range(nblk) for j in range(i + 1)]
    return (jnp.asarray(np.array([p[0] for p in pr], np.int32)), jnp.asarray(np.array([p[1] for p in pr], np.int32)))


def _causal_mask(rows, keys):
    r = lax.broadcasted_iota(jnp.int32, (rows[1] - rows[0], keys[1] - keys[0]), 0) + rows[0]
    c = lax.broadcasted_iota(jnp.int32, (rows[1] - rows[0], keys[1] - keys[0]), 1) + keys[0]
    return c <= r


def _diagonal_pieces(tb):
    half = tb // 2
    if half % LANES:
        return [((0, tb), (0, tb))]
    return [((0, half), (0, half)), ((half, tb), (0, tb))]


LOG2E = math.log2(math.e)
QK_TO_LOG2 = LOG2E / math.sqrt(HEAD_DIM)


def _attn_logits2(q, k, fk_row):
    sc = lax.dot_general(q, k, (((1,), (1,)), ((), ())), preferred_element_type=F32)
    return sc * QK_TO_LOG2 - fk_row * LOG2E


def _attn_fwd(z, f_row, n_heads, *, name, tb=1024, phases=()):
    s = z.shape[0]
    tb = _tile(s, tb, 128)
    nblk = s // tb
    qi, kj = _pairs(nblk, by_kv=False)

    def body(qi_ref, kj_ref, q_ref, k_ref, v_ref, fk_ref, o_ref, lse_ref, m_sc, l_sc, acc_sc):
        p = pl.program_id(1)
        i, j = qi_ref[p], kj_ref[p]

        @pl.when(j == 0)
        def _():
            m_sc[...] = jnp.full_like(m_sc, NEG_BIG)
            l_sc[...] = jnp.zeros_like(l_sc)
            acc_sc[...] = jnp.zeros_like(acc_sc)

        def update(rows, keys, masked):
            rs, ks = slice(*rows), slice(*keys)
            s2 = _attn_logits2(q_ref[rs, :], k_ref[ks, :], fk_ref[:, ks])
            if masked:
                s2 = jnp.where(_causal_mask(rows, keys), s2, NEG_BIG)
            m_old = m_sc[rs, :]
            m_new = jnp.maximum(m_old, jnp.max(s2, axis=-1, keepdims=True))
            alpha = jnp.exp2(m_old - m_new)
            pv = jnp.exp2(s2 - jnp.tile(m_new, (1, (keys[1] - keys[0]) // LANES)))
            l_sc[rs, :] = alpha * l_sc[rs, :] + jnp.sum(pv, axis=-1, keepdims=True)
            acc_sc[rs, :] = alpha * acc_sc[rs, :] + jnp.dot(pv.astype(BF16), v_ref[ks, :], preferred_element_type=F32)
            m_sc[rs, :] = m_new

        @pl.when(j < i)
        def _():
            update((0, tb), (0, tb), False)

        @pl.when(j == i)
        def _():
            for rows, keys in _diagonal_pieces(tb):
                update(rows, keys, True)
            o_ref[...] = (acc_sc[...] / l_sc[...]).astype(BF16)
            lse_ref[...] = m_sc[...] + jnp.log2(l_sc[...])

    h = n_heads
    return _call(
        body, name=name, n_prefetch=2, grid=(h, int(qi.shape[0])),
        in_specs=[
            pl.BlockSpec((tb, HEAD_DIM), lambda hh, p, qi_r, kj_r: (qi_r[p], hh)),
            pl.BlockSpec((tb, HEAD_DIM), lambda hh, p, qi_r, kj_r: (kj_r[p], h + hh)),
            pl.BlockSpec((tb, HEAD_DIM), lambda hh, p, qi_r, kj_r: (kj_r[p], 2 * h + hh)),
            pl.BlockSpec((None, 1, tb), lambda hh, p, qi_r, kj_r: (hh, 0, kj_r[p])),
        ],
        out_specs=[
            pl.BlockSpec((tb, HEAD_DIM), lambda hh, p, qi_r, kj_r: (qi_r[p], hh)),
            pl.BlockSpec((None, tb, LANES), lambda hh, p, qi_r, kj_r: (hh, qi_r[p], 0)),
        ],
        scratch_shapes=[pltpu.VMEM((tb, LANES), F32), pltpu.VMEM((tb, LANES), F32), pltpu.VMEM((tb, HEAD_DIM), F32)],
        out_shape=[jax.ShapeDtypeStruct((s, h * HEAD_DIM), BF16), jax.ShapeDtypeStruct((h, s, LANES), F32)],
        operands=[qi, kj, z, z, z, f_row], semantics=("parallel", "arbitrary"), phases=phases)


def _attn_bwd(z, o, d_o, lse2, f_row, n_heads, *, name, tb=1024, phases=()):
    s = z.shape[0]
    tb = _tile(s, tb, 128)
    nblk = s // tb
    qi, kj = _pairs(nblk, by_kv=True)
    n_pairs = int(qi.shape[0])
    scale = 1.0 / math.sqrt(HEAD_DIM)
    h = n_heads

    def body(qi_ref, kj_ref, q_ref, k_ref, v_ref, o_ref, do_ref, lse_ref, fk_ref,
             dq_ref, dk_ref, dv_ref, df_ref, dfq_ref, dq_sc, dk_sc, dv_sc, df_sc, dfq_sc):
        p = pl.program_id(1)
        i, j = qi_ref[p], kj_ref[p]

        @pl.when(p == 0)
        def _():
            dq_sc[...] = jnp.zeros_like(dq_sc)
            dfq_sc[...] = jnp.zeros_like(dfq_sc)

        @pl.when(i == j)
        def _():
            dk_sc[...] = jnp.zeros_like(dk_sc)
            dv_sc[...] = jnp.zeros_like(dv_sc)
            df_sc[...] = jnp.zeros_like(df_sc)

        def update(rows, keys, masked):
            rs, ks, n_rows = slice(*rows), slice(*keys), rows[1] - rows[0]
            q, k, v, do = q_ref[rs, :], k_ref[ks, :], v_ref[ks, :], do_ref[rs, :]
            delta = jnp.sum(do.astype(F32) * o_ref[rs, :].astype(F32), axis=-1, keepdims=True)
            pv = jnp.exp2(_attn_logits2(q, k, fk_ref[:, ks]) - jnp.tile(lse_ref[rs, :], (1, (keys[1] - keys[0]) // LANES)))
            if masked:
                pv = jnp.where(_causal_mask(rows, keys), pv, 0.0)
            dp = lax.dot_general(do, v, (((1,), (1,)), ((), ())), preferred_element_type=F32)
            ds = pv * (dp - delta)
            ds_b = ds.astype(BF16)
            dv_sc[ks, :] += lax.dot_general(pv.astype(BF16), do, (((0,), (0,)), ((), ())), preferred_element_type=F32)
            dk_sc[ks, :] += lax.dot_general(ds_b, q, (((0,), (0,)), ((), ())), preferred_element_type=F32)
            at = pl.ds(pl.multiple_of(i * tb + rows[0], LANES), n_rows)
            dq_sc[at, :] += jnp.dot(ds_b, k, preferred_element_type=F32)
            df_sc[:, ks] -= jnp.sum(ds, axis=0, keepdims=True)
            dfq_sc[at, :] += jnp.broadcast_to(jnp.sum(ds, axis=1, keepdims=True), (n_rows, LANES))

        @pl.when(i > j)
        def _():
            update((0, tb), (0, tb), False)

        @pl.when(i == j)
        def _():
            for rows, keys in _diagonal_pieces(tb):
                update(rows, keys, True)

        @pl.when(i == nblk - 1)
        def _():
            dk_ref[...] = (dk_sc[...] * scale).astype(BF16)
            dv_ref[...] = dv_sc[...].astype(BF16)
            df_ref[...] = df_sc[...]

        @pl.when(p == n_pairs - 1)
        def _():
            dq_ref[...] = (dq_sc[...] * scale).astype(BF16)
            dfq_ref[...] = jnp.transpose(dfq_sc[...])[0:1, :]

    qblk = lambda off: pl.BlockSpec((tb, HEAD_DIM), lambda hh, p, qi_r, kj_r: (qi_r[p], off + hh))
    kblk = lambda off: pl.BlockSpec((tb, HEAD_DIM), lambda hh, p, qi_r, kj_r: (kj_r[p], off + hh))
    qrep = pl.BlockSpec((None, tb, LANES), lambda hh, p, qi_r, kj_r: (hh, qi_r[p], 0))
    krow = pl.BlockSpec((None, 1, tb), lambda hh, p, qi_r, kj_r: (hh, 0, kj_r[p]))
    act = jax.ShapeDtypeStruct((s, h * HEAD_DIM), BF16)
    return _call(
        body, name=name, n_prefetch=2, grid=(h, n_pairs),
        in_specs=[qblk(0), kblk(h), kblk(2 * h), qblk(0), qblk(0), qrep, krow],
        out_specs=[
            pl.BlockSpec((s, HEAD_DIM), lambda hh, p, qi_r, kj_r: (0, hh)),
            kblk(0), kblk(0), krow,
            pl.BlockSpec((None, 1, s), lambda hh, p, qi_r, kj_r: (hh, 0, 0)),
        ],
        scratch_shapes=[pltpu.VMEM((s, HEAD_DIM), F32), pltpu.VMEM((tb, HEAD_DIM), F32),
                        pltpu.VMEM((tb, HEAD_DIM), F32), pltpu.VMEM((1, tb), F32), pltpu.VMEM((s, LANES), F32)],
        out_shape=[act, act, act, jax.ShapeDtypeStruct((h, 1, s), F32), jax.ShapeDtypeStruct((h, 1, s), F32)],
        operands=[qi, kj, z, z, z, o, d_o, lse2, f_row], semantics=("parallel", "arbitrary"), phases=phases)


GELU_C = math.sqrt(2.0 / math.pi)
GELU_A = 0.044715


def _gelu(x):
    return 0.5 * x * (1.0 + jnp.tanh(GELU_C * (x + GELU_A * (x * x * x))))


def _gelu_and_grad(x):
    t = jnp.tanh(GELU_C * (x + GELU_A * (x * x * x)))
    y = 0.5 * x * (1.0 + t)
    dy = 0.5 * (1.0 + t) + 0.5 * x * (1.0 - t * t) * (GELU_C * (1.0 + 3.0 * GELU_A * (x * x)))
    return y, dy


def _layernorm_parts(g):
    mu = jnp.mean(g, axis=-1, keepdims=True)
    xc = g - mu
    rs = lax.rsqrt(jnp.mean(xc * xc, axis=-1, keepdims=True) + EPS)
    return xc * rs, rs


def _spatial_mix(w_ref, bcol_ref, vv_b, n_heads, n_chunks):
    tril = _causal_mask((0, CHUNK), (0, CHUNK))
    cols = []
    for hh in range(n_heads):
        wc = jnp.where(tril, w_ref[hh], 0.0).astype(BF16)
        lanes = slice(hh * HEAD_DIM, (hh + 1) * HEAD_DIM)
        rows = [jnp.dot(wc, vv_b[c * CHUNK:(c + 1) * CHUNK, lanes], preferred_element_type=F32)
                + bcol_ref[:, hh:hh + 1] for c in range(n_chunks)]
        cols.append(jnp.concatenate(rows, axis=0))
    return jnp.concatenate(cols, axis=1)


def _mix_fwd(z, o, ln_g, ln_b, w_s, b_col, attn_g, gm_g, n_heads, *, name, tr=256):
    s = z.shape[0]
    dg = n_heads * HEAD_DIM
    tr = _tile(s, tr, CHUNK)
    n_chunks = tr // CHUNK

    def body(zu_ref, zv_ref, o_ref, lg_ref, lb_ref, w_ref, bcol_ref, ag_ref, gg_ref, out_ref):
        u = _gelu(zu_ref[...].astype(F32))
        xhat, _ = _layernorm_parts(_gelu(zv_ref[...].astype(F32)))
        vv = xhat * lg_ref[...] + lb_ref[...]
        gm = u * _spatial_mix(w_ref, bcol_ref, vv.astype(BF16), n_heads, n_chunks)
        rg = lax.rsqrt(jnp.mean(gm * gm, axis=-1, keepdims=True) + EPS)
        ov = o_ref[...].astype(F32)
        ra = lax.rsqrt(jnp.mean(ov * ov, axis=-1, keepdims=True) + EPS)
        out_ref[:, :dg] = ((ov * ra) * ag_ref[...]).astype(BF16)
        out_ref[:, dg:] = ((gm * rg) * gg_ref[...]).astype(BF16)

    vec = pl.BlockSpec((1, dg), lambda i: (0, 0))
    return pl.pallas_call(
        body, name=name, out_shape=jax.ShapeDtypeStruct((s, 2 * dg), BF16), grid=(s // tr,),
        in_specs=[pl.BlockSpec((tr, dg), lambda i: (i, 3)), pl.BlockSpec((tr, dg), lambda i: (i, 4)),
                  pl.BlockSpec((tr, dg), lambda i: (i, 0)), vec, vec,
                  pl.BlockSpec((n_heads, CHUNK, CHUNK), lambda i: (0, 0, 0)),
                  pl.BlockSpec((CHUNK, n_heads), lambda i: (0, 0)), vec, vec],
        out_specs=pl.BlockSpec((tr, 2 * dg), lambda i: (i, 0)),
        compiler_params=_params("parallel"),
    )(z, z, o, ln_g, ln_b, w_s, b_col, attn_g, gm_g)


def _mix_bwd(z, o, d_merged, ln_g, ln_b, w_s, b_col, attn_g, gm_g, n_heads, *, name, tr=256):
    s = z.shape[0]
    dg = n_heads * HEAD_DIM
    tr = _tile(s, tr, CHUNK)
    n_chunks = tr // CHUNK

    def body(zu_ref, zv_ref, o_ref, dm_ref, lg_ref, lb_ref, w_ref, bcol_ref, ag_ref, gg_ref,
             do_ref, dzu_ref, dzv_ref, dw_ref, dbcol_ref, dlg_ref, dlb_ref, dag_ref, dgg_ref):
        @pl.when(pl.program_id(0) == 0)
        def _():
            for ref in (dw_ref, dbcol_ref, dlg_ref, dlb_ref, dag_ref, dgg_ref):
                ref[...] = jnp.zeros_like(ref)

        d_o, dag_rows = _rms_bwd_rows(dm_ref[:, :dg], o_ref[...].astype(F32), ag_ref[...])
        do_ref[...] = d_o.astype(BF16)
        dag_ref[...] += jnp.sum(dag_rows, axis=0, keepdims=True)

        u, du_dz = _gelu_and_grad(zu_ref[...].astype(F32))
        gv, dgv_dz = _gelu_and_grad(zv_ref[...].astype(F32))
        xhat, rs = _layernorm_parts(gv)
        lg = lg_ref[...]
        vv_b = (xhat * lg + lb_ref[...]).astype(BF16)
        mix = _spatial_mix(w_ref, bcol_ref, vv_b, n_heads, n_chunks)
        gm = u * mix
        d_gm, dgg_rows = _rms_bwd_rows(dm_ref[:, dg:], gm, gg_ref[...])
        dgg_ref[...] += jnp.sum(dgg_rows, axis=0, keepdims=True)
        dzu_ref[...] = ((d_gm * mix) * du_dz).astype(BF16)
        d_mix = d_gm * u
        d_mix_b = d_mix.astype(BF16)

        tril = _causal_mask((0, CHUNK), (0, CHUNK))
        lane = lax.broadcasted_iota(jnp.int32, (CHUNK, n_heads), 1)
        cols = []
        db = jnp.zeros((CHUNK, n_heads), F32)
        for hh in range(n_heads):
            wc = jnp.where(tril, w_ref[hh], 0.0).astype(BF16)
            lanes = slice(hh * HEAD_DIM, (hh + 1) * HEAD_DIM)
            dw = jnp.zeros((CHUNK, CHUNK), F32)
            dmix_sum = jnp.zeros((CHUNK, HEAD_DIM), F32)
            rows = []
            for c in range(n_chunks):
                rws = slice(c * CHUNK, (c + 1) * CHUNK)
                dmb = d_mix_b[rws, lanes]
                dw += lax.dot_general(dmb, vv_b[rws, lanes], (((1,), (1,)), ((), ())), preferred_element_type=F32)
                rows.append(lax.dot_general(wc, dmb, (((0,), (0,)), ((), ())), preferred_element_type=F32))
                dmix_sum += d_mix[rws, lanes]
            dw_ref[hh] += jnp.where(tril, dw, 0.0)
            db += jnp.where(lane == hh, jnp.sum(dmix_sum, axis=-1, keepdims=True), 0.0)
            cols.append(jnp.concatenate(rows, axis=0))
        dbcol_ref[...] += db
        d_vv = jnp.concatenate(cols, axis=1)

        dlg_ref[...] += jnp.sum(d_vv * xhat, axis=0, keepdims=True)
        dlb_ref[...] += jnp.sum(d_vv, axis=0, keepdims=True)
        d_xhat = d_vv * lg
        d_gv = rs * (d_xhat - jnp.mean(d_xhat, axis=-1, keepdims=True)
                     - xhat * jnp.mean(d_xhat * xhat, axis=-1, keepdims=True))
        dzv_ref[...] = (d_gv * dgv_dz).astype(BF16)

    vec = pl.BlockSpec((1, dg), lambda i: (0, 0))
    wspec = pl.BlockSpec((n_heads, CHUNK, CHUNK), lambda i: (0, 0, 0))
    bspec = pl.BlockSpec((CHUNK, n_heads), lambda i: (0, 0))
    rowb = pl.BlockSpec((tr, dg), lambda i: (i, 0))
    act = jax.ShapeDtypeStruct((s, dg), BF16)
    vshape = jax.ShapeDtypeStruct((1, dg), F32)
    return pl.pallas_call(
        body, name=name,
        out_shape=(act, act, act, jax.ShapeDtypeStruct((n_heads, CHUNK, CHUNK), F32),
                   jax.ShapeDtypeStruct((CHUNK, n_heads), F32), vshape, vshape, vshape, vshape),
        grid=(s // tr,),
        in_specs=[pl.BlockSpec((tr, dg), lambda i: (i, 3)), pl.BlockSpec((tr, dg), lambda i: (i, 4)),
                  rowb, pl.BlockSpec((tr, 2 * dg), lambda i: (i, 0)), vec, vec, wspec, bspec, vec, vec],
        out_specs=(rowb, rowb, rowb, wspec, bspec, vec, vec, vec, vec),
        compiler_params=_params("arbitrary"),
    )(z, z, o, d_merged, ln_g, ln_b, w_s, b_col, attn_g, gm_g)


def _place():
    x, y, c = lax.axis_index("x"), lax.axis_index("y"), lax.axis_index("c")
    other_chips = [(1 - x, y), (x, 1 - y), (1 - x, 1 - y)]
    return x, y, c, other_chips


def _remote(src, dst, send_sem, recv_sem, to):
    return pltpu.make_async_remote_copy(src_ref=src, dst_ref=dst, send_sem=send_sem, recv_sem=recv_sem,
                                        device_id=to, device_id_type=MESH)


def _cast_into_slot(w, place, *, name, phases=()):
    rows, cols = w.shape
    tr, tc = _rc_tile(rows, cols)

    def body(place_ref, w_ref, o_ref):
        o_ref[...] = w_ref[...].astype(BF16)

    return _only(_call(
        body, name=name, n_prefetch=1, grid=(rows // tr, cols // tc),
        in_specs=[pl.BlockSpec((tr, tc), lambda i, j, pr: (i, j))],
        out_specs=[pl.BlockSpec((None, tr, tc), lambda i, j, pr: (pr[0], i, j))],
        out_shape=[jax.ShapeDtypeStruct((N_CHIPS, rows, cols), BF16)], operands=[place, w],
        semantics=("parallel", "parallel"), phases=phases))


def _casts_and_norm(weights, place, x, g, *, name, rows=256, phases=()):
    cols = x.shape[1]
    jobs = [w.shape[0] // rows for w in weights] + [x.shape[0] // rows]
    assert all(w.shape[1] == cols and w.shape[0] % rows == 0 for w in weights) and x.shape[0] % rows == 0
    first = [sum(jobs[:k]) for k in range(len(jobs))]

    def strip(k):
        return lambda t: jnp.clip(t - first[k], 0, jobs[k] - 1)

    def body(place_ref, *refs):
        n = len(weights)
        w_refs, x_ref, g_ref, outs = refs[:n], refs[n], refs[n + 1], refs[n + 2:]
        t = pl.program_id(0)
        for k in range(n):
            @pl.when(jnp.logical_and(t >= first[k], t < first[k] + jobs[k]))
            def _(k=k):
                outs[k][...] = w_refs[k][...].astype(BF16)

        @pl.when(t >= first[n])
        def _():
            xv = x_ref[...]
            r = lax.rsqrt(jnp.mean(xv * xv, axis=-1, keepdims=True) + EPS)
            outs[n][...] = ((xv * r) * g_ref[...]).astype(BF16)

    in_specs = [pl.BlockSpec((rows, cols), lambda t, pr, k=k: (strip(k)(t), 0)) for k in range(len(weights))]
    in_specs += [pl.BlockSpec((rows, cols), lambda t, pr: (strip(len(weights))(t), 0)),
                 pl.BlockSpec((1, cols), lambda t, pr: (0, 0))]
    out_specs = [pl.BlockSpec((None, rows, cols), lambda t, pr, k=k: (pr[0], strip(k)(t), 0)) for k in range(len(weights))]
    out_specs.append(pl.BlockSpec((rows, cols), lambda t, pr: (strip(len(weights))(t), 0)))
    out_shape = [jax.ShapeDtypeStruct((N_CHIPS,) + w.shape, BF16) for w in weights] + [jax.ShapeDtypeStruct(x.shape, BF16)]
    return _call(body, name=name, n_prefetch=1, grid=(sum(jobs),), in_specs=in_specs, out_specs=out_specs,
                 out_shape=out_shape, operands=[place, *weights, x, g], semantics=("arbitrary",), phases=phases)


def _exchange(phases, *, name):
    comm_in = [a for ph in phases for a in ph.arrays]
    comm_out = [jax.ShapeDtypeStruct(s.shape, s.dtype) for ph in phases for s in (ph.arrays if ph.in_place else ph.out_shapes)]
    aliases, at_in, at_out = {}, 0, 0
    for ph in phases:
        if ph.in_place:
            aliases.update({at_in + r: at_out + r for r in range(len(ph.arrays))})
        at_in, at_out = at_in + len(ph.arrays), at_out + ph.n_out
    n_sems = sum(ph.n_sems for ph in phases)

    def body(*refs):
        cin, cout = refs[:len(comm_in)], refs[len(comm_in):len(comm_in) + len(comm_out)]
        send_sems, recv_sems = refs[len(comm_in) + len(comm_out):]
        _run_phases(phases, ("start", "finish"), cin, cout, send_sems, recv_sems)

    return pl.pallas_call(
        body, name=name, out_shape=tuple(comm_out), in_specs=[ANY] * len(comm_in), out_specs=tuple([ANY] * len(comm_out)),
        input_output_aliases=aliases,
        scratch_shapes=[pltpu.SemaphoreType.DMA((n_sems,)), pltpu.SemaphoreType.DMA((n_sems,))],
    )(*comm_in)


GATHER_PARTS = 8


def _gather(bufs, stage, part=(0, GATHER_PARTS)):
    n = 3 * len(bufs)
    lo, hi = part

    def copies(outs, send, recv, d2d, incoming):
        x, y, c, chips = _place()
        for t, buf in enumerate(outs):
            half = buf.shape[2] // 2
            piece = half // GATHER_PARTS
            for k, (cx, cy) in enumerate(chips):
                i = 3 * t + k + (n if (d2d and stage == "both") else 0)
                cols = pl.ds(((1 - c) if (d2d and incoming) else c) * half + lo * piece, (hi - lo) * piece)
                blk = buf.at[(2 * cx + cy) if (d2d or incoming) else (2 * x + y), :, cols]
                yield _remote(blk, blk, send(i), recv(i), (x, y, 1 - c) if d2d else (cx, cy, c))

    def start(ins, outs, send, recv):
        for cp in copies(outs, send, recv, stage == "d2d", False):
            cp.start()

    def finish(ins, outs, send, recv):
        if stage == "both":
            for arrival, onward in zip(copies(outs, send, recv, False, True), copies(outs, send, recv, True, False)):
                arrival.wait_recv()
                onward.start()
        for cp in copies(outs, send, recv, stage != "ici", True):
            cp.wait_recv()
        for d2d in ((False, True) if stage == "both" else (stage == "d2d",)):
            for cp in copies(outs, send, recv, d2d, False):
                cp.wait_send()

    return _Phase(bufs, [], True, (2 if stage == "both" else 1) * n, start, finish)


def _merge(first, second):
    n_first = first.n_sems

    def later(sem):
        return lambda i: sem(n_first + i)

    def start(ins, outs, send, recv):
        first.start(ins, outs, send, recv)
        second.start(ins, outs, later(send), later(recv))

    def finish(ins, outs, send, recv):
        first.finish(ins, outs, send, recv)
        second.finish(ins, outs, later(send), later(recv))

    return _Phase(first.arrays, [], True, n_first + second.n_sems, start, finish)


def _swap_halves(grads):
    def copies(ins, outs, send, recv):
        x, y, c, _ = _place()
        for t, g in enumerate(ins):
            half = g.shape[2] // 2
            yield _remote(g.at[:, :, pl.ds((1 - c) * half, half)], outs[t], send(t), recv(t), (x, y, 1 - c))

    def start(ins, outs, send, recv):
        for cp in copies(ins, outs, send, recv):
            cp.start()

    def finish(ins, outs, send, recv):
        for cp in copies(ins, outs, send, recv):
            cp.wait()

    shapes = [jax.ShapeDtypeStruct((a.shape[0], a.shape[1], a.shape[2] // 2), a.dtype) for a in grads]
    return _Phase(grads, shapes, False, len(grads), start, finish)


def _add_halves(grad, received, place, *, name):
    ns, rows, half = received.shape
    tr, tc = _rc_tile(rows, half, pref_rows=1024)
    per = half // tc

    def body(place_ref, g_ref, r_ref, o_ref):
        o_ref[...] = (g_ref[...].astype(F32) + r_ref[...].astype(F32)).astype(BF16)

    grid_spec = pltpu.PrefetchScalarGridSpec(
        num_scalar_prefetch=1, grid=(ns, rows // tr, per),
        in_specs=[pl.BlockSpec((None, tr, tc), lambda s, i, j, pr: (s, i, pr[1] * per + j)),
                  pl.BlockSpec((None, tr, tc), lambda s, i, j, pr: (s, i, j))],
        out_specs=pl.BlockSpec((None, tr, tc), lambda s, i, j, pr: (s, i, j)),
    )
    return pl.pallas_call(
        body, name=name, grid_spec=grid_spec, out_shape=jax.ShapeDtypeStruct(received.shape, BF16),
        compiler_params=_params("parallel", "parallel", "parallel"),
    )(place, grad, received)


def _send_partials(parts, piece=(0, 1)):
    k_th, n_pieces = piece

    def cols(part):
        width = part.shape[2] // n_pieces
        return pl.ds(k_th * width, width)

    def start(ins, outs, send, recv):
        x, y, c, chips = _place()
        for t, part in enumerate(ins):
            for k, (cx, cy) in enumerate(chips):
                _remote(part.at[2 * cx + cy, :, cols(part)], outs[t].at[2 * x + y],
                        send(3 * t + k), recv(3 * t + k), (cx, cy, c)).start()

    def finish(ins, outs, send, recv):
        x, y, c, chips = _place()
        for t, part in enumerate(ins):
            for k, (cx, cy) in enumerate(chips):
                slot = outs[t].at[2 * cx + cy]
                _remote(slot, slot, send(3 * t + k), recv(3 * t + k), (cx, cy, c)).wait_recv()
        for t, part in enumerate(ins):
            for k, (cx, cy) in enumerate(chips):
                sent = part.at[2 * cx + cy, :, cols(part)]
                _remote(sent, sent, send(3 * t + k), recv(3 * t + k), (cx, cy, c)).wait_send()

    shapes = [jax.ShapeDtypeStruct(a.shape[:2] + (a.shape[2] // n_pieces,), a.dtype) for a in parts]
    return _Phase(parts, shapes, False, 3 * len(parts), start, finish)


def _sum_chips(parts, slots, place, *, name, piece=(0, 1), into=None):
    ns, rows, width = slots.shape
    k_th, n_pieces = piece
    half = width * n_pieces
    tr, tc = _rc_tile(rows, width, pref_rows=512)
    per = width // tc

    def body(place_ref, p_ref, s_ref, *rest):
        acc = p_ref[...].astype(F32)
        for k in range(ns):
            acc = acc + jnp.where(place_ref[0] == k, 0.0, s_ref[k].astype(F32))
        rest[-1][...] = acc

    grid_spec = pltpu.PrefetchScalarGridSpec(
        num_scalar_prefetch=1, grid=(rows // tr, per),
        in_specs=[pl.BlockSpec((None, tr, tc), lambda i, j, pr: (pr[0], i, k_th * per + j)),
                  pl.BlockSpec((ns, tr, tc), lambda i, j, pr: (0, i, j))] + ([ANY] if into is not None else []),
        out_specs=pl.BlockSpec((tr, tc), lambda i, j, pr: (i, (pr[1] * n_pieces + k_th) * per + j)),
    )
    return pl.pallas_call(
        body, name=name, grid_spec=grid_spec, out_shape=jax.ShapeDtypeStruct((rows, 2 * half), F32),
        input_output_aliases={3: 0} if into is not None else {},
        compiler_params=_params("parallel", "parallel"),
    )(place, parts, slots, *([into] if into is not None else []))


def _join_halves(bufs):
    def copies(outs, send, recv, incoming):
        x, y, c, _ = _place()
        for t, buf in enumerate(outs):
            half = buf.shape[1] // 2
            cols = buf.at[:, pl.ds(((1 - c) if incoming else c) * half, half)]
            yield _remote(cols, cols, send(t), recv(t), (x, y, 1 - c))

    def start(ins, outs, send, recv):
        for cp in copies(outs, send, recv, False):
            cp.start()

    def finish(ins, outs, send, recv):
        for cp in copies(outs, send, recv, True):
            cp.wait_recv()
        for cp in copies(outs, send, recv, False):
            cp.wait_send()

    return _Phase(bufs, [], True, len(bufs), start, finish)


def _gather_small(buf):
    def slot(out, px, py, pc):
        return out.at[4 * px + 2 * py + pc]

    def start(ins, outs, send, recv):
        x, y, c, chips = _place()
        mine = slot(outs[0], x, y, c)
        _remote(ins[0], mine, send(0), recv(0), (x, y, 1 - c)).start()
        for k, (cx, cy) in enumerate(chips):
            _remote(ins[0], mine, send(1 + k), recv(1 + k), (cx, cy, c)).start()

    def finish(ins, outs, send, recv):
        x, y, c, chips = _place()
        sibling = (x, y, 1 - c)
        for k, (cx, cy) in enumerate(chips):
            arrived = slot(outs[0], cx, cy, c)
            _remote(arrived, arrived, send(1 + k), recv(1 + k), sibling).wait_recv()
            _remote(arrived, arrived, send(4 + k), recv(4 + k), sibling).start()
        theirs = slot(outs[0], x, y, 1 - c)
        _remote(theirs, theirs, send(0), recv(0), sibling).wait_recv()
        for k, (cx, cy) in enumerate(chips):
            passed = slot(outs[0], cx, cy, 1 - c)
            _remote(passed, passed, send(4 + k), recv(4 + k), sibling).wait_recv()
        for i in range(7):
            _remote(ins[0], ins[0], send(i), recv(i), sibling).wait_send()

    return _Phase([buf], [jax.ShapeDtypeStruct((N_DEV,) + buf.shape, buf.dtype)], False, 7, start, finish)


def _adamw_math(w, g, m, v):
    m = ADAM_B1 * m + (1.0 - ADAM_B1) * g
    v = ADAM_B2 * v + (1.0 - ADAM_B2) * (g * g)
    m_hat = m / (1.0 - ADAM_B1 ** ADAM_STEP)
    v_hat = v / (1.0 - ADAM_B2 ** ADAM_STEP)
    delta = -ADAM_LR * (m_hat / (jnp.sqrt(v_hat) + ADAM_EPS) + ADAM_WD * w)
    return delta, m, v


def _adamw(w, g, m, v, *, name):
    rows, cols = w.shape
    tr, tc = _rc_tile(rows, cols)

    def body(w_ref, g_ref, m_ref, v_ref, go_ref, d_ref, mo_ref, vo_ref):
        g = g_ref[...]
        go_ref[...] = g
        d_ref[...], mo_ref[...], vo_ref[...] = _adamw_math(w_ref[...], g, m_ref[...], v_ref[...])

    blk = pl.BlockSpec((tr, tc), lambda i, j: (i, j))
    shape = jax.ShapeDtypeStruct((rows, cols), F32)
    return pl.pallas_call(
        body, name=name, out_shape=(shape, shape, shape, shape), grid=(rows // tr, cols // tc),
        in_specs=[blk] * 4, out_specs=(blk, blk, blk, blk), compiler_params=_params("parallel", "parallel"),
    )(w, g, m, v)


def _adamw_small(gathered, own, place, w, m, v, *, name):
    nd = gathered.shape[0]

    def body(place_ref, gs_ref, own_ref, w_ref, m_ref, v_ref, g_ref, d_ref, mo_ref, vo_ref):
        me = 2 * place_ref[0] + place_ref[1]
        g = jnp.zeros(own_ref.shape, F32)
        for k in range(nd):
            g = g + jnp.where(me == k, own_ref[...], gs_ref[k])
        g_ref[...] = g
        d_ref[...], mo_ref[...], vo_ref[...] = _adamw_math(w_ref[...], g, m_ref[...], v_ref[...])

    whole = pl.BlockSpec(w.shape, lambda i, pr: (0, 0))
    grid_spec = pltpu.PrefetchScalarGridSpec(
        num_scalar_prefetch=1, grid=(1,),
        in_specs=[pl.BlockSpec(gathered.shape, lambda i, pr: (0, 0, 0)), whole, whole, whole, whole],
        out_specs=(whole, whole, whole, whole))
    shape = jax.ShapeDtypeStruct(w.shape, F32)
    return pl.pallas_call(body, name=name, grid_spec=grid_spec, out_shape=(shape, shape, shape, shape),
                          compiler_params=_params("arbitrary"))(place, gathered, own, w, m, v)


def _pack(parts):
    flat = jnp.concatenate([p.reshape(-1).astype(F32) for p in parts])
    rows = -(-flat.shape[0] // (8 * LANES)) * 8
    return jnp.pad(flat, (0, rows * LANES - flat.shape[0])).reshape(rows, LANES)


def _unpack(buf, shapes):
    flat = buf.reshape(-1)
    out, pos = [], 0
    for shp in shapes:
        size = int(np.prod(shp))
        out.append(flat[pos:pos + size].reshape(shp))
        pos += size
    return out


ROW_BLOCK = 256


def _realign_rows(sources, segments, out_shape, *, name, phases=()):
    n_slots, rows, cols = out_shape
    n_src = len(sources)
    per_slot = -(-rows // ROW_BLOCK)
    table = np.zeros((6, n_slots * per_slot, n_src), np.int32)
    for so in range(n_slots):
        for first, last, src, src_slot, src_row in segments[so]:
            for blk in range(first // ROW_BLOCK, (last - 1) // ROW_BLOCK + 1):
                lo, hi = max(first, blk * ROW_BLOCK), min(last, (blk + 1) * ROW_BLOCK)
                base = src_row + (blk * ROW_BLOCK - first)
                m0 = (base + lo - blk * ROW_BLOCK) // ROW_BLOCK
                at = so * per_slot + blk
                assert table[4, at, src] == 0, "two segments of one block share a source operand"
                table[:, at, src] = (src_slot, m0, base - m0 * ROW_BLOCK, lo - blk * ROW_BLOCK, hi - blk * ROW_BLOCK,
                                     min(2 * ROW_BLOCK, sources[src].shape[1] - m0 * ROW_BLOCK))
    last_block = [-(-a.shape[1] // ROW_BLOCK) - 1 for a in sources]

    def body(slot_ref, blk_ref, off_ref, lo_ref, hi_ref, valid_ref, *refs):
        o_ref, acc = refs[2 * n_src], refs[2 * n_src + 1]
        at = (pl.program_id(0) * per_slot + pl.program_id(1)) * n_src
        acc[...] = jnp.zeros_like(acc)
        for p in range(n_src):
            @pl.when(hi_ref[at + p] > lo_ref[at + p])
            def _():
                two = jnp.concatenate([refs[2 * p][...], refs[2 * p + 1][...]], axis=0)
                src_row = lax.broadcasted_iota(jnp.int32, two.shape, 0)
                two = jnp.where(src_row < valid_ref[at + p], two, jnp.zeros_like(two))
                r = lax.broadcasted_iota(jnp.int32, (ROW_BLOCK, 2 * ROW_BLOCK), 0)
                c = lax.broadcasted_iota(jnp.int32, (ROW_BLOCK, 2 * ROW_BLOCK), 1)
                place = (c == r + off_ref[at + p]) & (r >= lo_ref[at + p]) & (r < hi_ref[at + p])
                acc[...] += jnp.dot(place.astype(two.dtype), two, preferred_element_type=F32)
        o_ref[...] = acc[...].astype(o_ref.dtype)

    def src_spec(p, second):
        def index(so, i, slot_r, blk_r, off_r, lo_r, hi_r, valid_r):
            at = (so * per_slot + i) * n_src + p
            return slot_r[at], jnp.minimum(blk_r[at] + second, last_block[p]), 0
        return pl.BlockSpec((None, ROW_BLOCK, cols), index)

    flat = [jnp.asarray(table[k].reshape(-1)) for k in range(6)]
    return _only(_call(
        body, name=name, n_prefetch=6, grid=(n_slots, per_slot),
        in_specs=[src_spec(p, second) for p in range(n_src) for second in (0, 1)],
        out_specs=[pl.BlockSpec((None, ROW_BLOCK, cols), lambda so, i, *_: (so, i, 0))],
        scratch_shapes=[pltpu.VMEM((ROW_BLOCK, cols), F32)],
        out_shape=[jax.ShapeDtypeStruct(out_shape, sources[0].dtype)],
        operands=[*flat, *[a for a in sources for _ in (0, 1)]], semantics=("parallel", "arbitrary"), phases=phases))


def _shard_rows(g, lo, hi):
    rs = g.shape[1]
    pieces = []
    for j in range(g.shape[0]):
        a, b = max(lo, j * rs), min(hi, (j + 1) * rs)
        if a < b:
            pieces.append(g[j, a - j * rs:b - j * rs])
    return pieces


def kernel(x, norm_mix_g, w_in, b_f, gmlp_ln_g, gmlp_ln_b, w_s, b_s, attn_out_g, gmlp_out_g, w_out, norm_ffn_g, w_ff1, w_ff2, norm_final_g, loss_target, m_norm_mix_g, m_w_in, m_b_f, m_gmlp_ln_g, m_gmlp_ln_b, m_w_s, m_b_s, m_attn_out_g, m_gmlp_out_g, m_w_out, m_norm_ffn_g, m_w_ff1, m_w_ff2, m_norm_final_g, v_norm_mix_g, v_w_in, v_b_f, v_gmlp_ln_g, v_gmlp_ln_b, v_w_s, v_b_s, v_attn_out_g, v_gmlp_out_g, v_w_out, v_norm_ffn_g, v_w_ff1, v_w_ff2, v_norm_final_g):
    seq, d_model = x.shape[1], x.shape[2]
    d_attn = d_model // 2
    n_heads = d_attn // HEAD_DIM
    qkv = 3 * d_attn
    shard_cols = w_in.shape[2]
    assert N_CHIPS * shard_cols == qkv + n_heads + 2 * d_attn
    xs = x.reshape(seq, d_model)
    target = loss_target.reshape(seq, d_model)

    place = jnp.stack([2 * lax.axis_index("x") + lax.axis_index("y"), lax.axis_index("c")]).astype(jnp.int32)
    names = ["w_in", "w_out", "w_ff1", "w_ff2"]
    wt_in, mt_in, vt_in = w_in[0].T, m_w_in[0].T, v_w_in[0].T
    b_in, _ = _cast_into_slot(wt_in, place, name="cast_w_in")
    (b_out, b_ff1, b_ff2, h), (g_in,) = _casts_and_norm(
        [w_out[0], w_ff1[0], w_ff2[0]], place, xs, norm_mix_g, name="casts_and_norm_mix", phases=[_gather([b_in], "both")])
    n_cols = N_CHIPS * shard_cols
    gate_slot, gate_row = divmod(qkv, shard_cols)
    assert gate_row + n_heads <= shard_cols
    pieces = []
    for j in range(N_CHIPS):
        if j == gate_slot:
            pieces += [(j, 0, gate_row), (j, gate_row + n_heads, shard_cols - gate_row - n_heads)]
        else:
            pieces.append((j, 0, shard_cols))
    fwd_segments, at = [[]], 0
    for order, (j, src_row, size) in enumerate(pieces):
        fwd_segments[0].append((at, at + size, order % 3, j, src_row))
        at += size
    wt_main, (b_out,) = _realign_rows([g_in] * 3, fwd_segments, (1, n_cols - n_heads, d_model), name="w_in_rows",
                                      phases=[_gather([b_out], "ici")])
    wt_main = wt_main[0]
    wt_f = jnp.pad(jnp.concatenate(_shard_rows(g_in, qkv, qkv + n_heads), axis=0), ((0, LANES - n_heads), (0, 0)))
    b_f_pad = jnp.pad(b_f, ((0, 0), (0, LANES - n_heads)))
    b_col = b_s[0].T

    z, (b_ff1,) = _matmul(h, wt_main, name="in_proj", out_dtype=BF16, trans_b=True, tm=2048,
                          phases=[_gather([b_ff1], "ici", (0, 4))])
    zb, f_cum = _forget_fwd(h, wt_f, b_f_pad, name="forget_fwd")
    f_row = f_cum[:, :n_heads].T[:, None, :]
    (o, lse2), (b_ff1, b_out, b_ff2) = _attn_fwd(z, f_row, n_heads, name="attn_fwd",
                                          phases=[_merge(_gather([b_ff1], "ici", (4, 8)), _gather([b_ff1], "d2d", (0, 4))),
                                                  _gather([b_out], "d2d"), _gather([b_ff2], "ici", (0, 2))])
    merged = _mix_fwd(z, o, gmlp_ln_g, gmlp_ln_b, w_s[0], b_col, attn_out_g, gmlp_out_g, n_heads, name="mix_fwd")
    w_out_full = b_out.reshape(2 * d_attn, d_model)
    x1, (b_ff1, b_ff2) = _matmul(merged, w_out_full, name="out_proj", out_dtype=F32, residual=xs,
                                 phases=[_gather([b_ff1], "d2d", (4, 8)),
                                         _merge(_gather([b_ff2], "ici", (2, 4)), _gather([b_ff2], "d2d", (0, 2)))])
    h2, _ = _rmsnorm_fwd(x1, norm_ffn_g, name="norm_ffn")
    a, (b_ff2,) = _matmul(h2, b_ff1, name="ff1", out_dtype=BF16, relu=True, b_sharded=True, tm=2048,
                          phases=[_merge(_gather([b_ff2], "both", (4, 8)), _gather([b_ff2], "d2d", (2, 4)))])
    w_ff2_full = b_ff2.reshape(N_CHIPS * b_ff2.shape[1], d_model)
    x2, _ = _matmul(a, w_ff2_full, name="ff2", out_dtype=F32, square_lhs=True, residual=x1)
    dx2, dx2_b, dg_final, loss = _loss_and_final_bwd(x2, target, norm_final_g.reshape(1, d_model), name="loss_head")

    def pair_sum(g, r, nm):
        return _add_halves(g, r, place, name="grads_pair_sum_" + nm)

    def chip_sum(p, q, nm, **piece):
        return _sum_chips(p, q, place, name="grads_chip_sum_" + nm, **piece)

    dw_ff2, _ = _matmul(a, dx2_b, name="ff2_dw", out_dtype=BF16, trans_a=True, square_lhs=True)
    dw_ff2 = dw_ff2.reshape(N_CHIPS, -1, d_model)
    da, (r_ff2,) = _matmul(dx2_b, w_ff2_full, name="ff2_dlhs", out_dtype=BF16, trans_b=True, scale2_by=a, tm=2048,
                           phases=[_swap_halves([dw_ff2])])
    ps_ff2 = pair_sum(dw_ff2, r_ff2, "w_ff2")
    dh2, (q_ff2a,) = _matmul(da, b_ff1, name="ff1_dlhs", out_dtype=F32, trans_b=True, b_sharded=True,
                             phases=[_send_partials([ps_ff2], (0, 2))])
    dw_ff1, (q_ff2b,) = _matmul(h2, da, name="ff1_dw", out_dtype=BF16, trans_a=True, out_sharded=True, tk=seq,
                                phases=[_send_partials([ps_ff2], (1, 2))])
    g_ff2 = chip_sum(ps_ff2, q_ff2a, "w_ff2_a", piece=(0, 2))
    g_ff2 = chip_sum(ps_ff2, q_ff2b, "w_ff2_b", piece=(1, 2), into=g_ff2)
    (dx1, dg_ffn, dx1_b), (g_ff2,) = _rmsnorm_bwd(dh2, x1, dx2, norm_ffn_g, name="norm_ffn_bwd", rounded_copy=True,
                                                   phases=[_join_halves([g_ff2])])
    dw_out, _ = _matmul(merged, dx1_b, name="out_proj_dw", out_dtype=BF16, trans_a=True, tk=seq)
    dw_out = dw_out.reshape(N_CHIPS, -1, d_model)
    d_merged, (r_ff1, r_out) = _matmul(dx1_b, w_out_full, name="out_proj_dlhs", out_dtype=F32, trans_b=True,
                                       phases=[_swap_halves([dw_ff1, dw_out])])
    ps_ff1, ps_out = pair_sum(dw_ff1, r_ff1, "w_ff1"), pair_sum(dw_out, r_out, "w_out")
    d_o, dzu, dzv, dw_s, db_col, dlg, dlb, dag, dgg = _mix_bwd(
        z, o, d_merged, gmlp_ln_g, gmlp_ln_b, w_s[0], b_col, attn_out_g, gmlp_out_g, n_heads, name="mix_bwd")
    (dq, dk, dv, d_f_key, d_f_query), (q_ff1, q_out) = _attn_bwd(
        z, o, d_o, lse2, f_row, n_heads, name="attn_bwd", phases=[_send_partials([ps_ff1, ps_out])])
    g_ff1, g_out = chip_sum(ps_ff1, q_ff1, "w_ff1"), chip_sum(ps_out, q_out, "w_out")
    d_f = d_f_key.reshape(n_heads, seq) + d_f_query.reshape(n_heads, seq)
    d_f_pad = jnp.pad(d_f.T, ((0, 0), (0, LANES - n_heads)))
    dzf, db_f = _forget_bwd(d_f_pad, zb, name="forget_bwd")
    dz = jnp.concatenate([dq, dk, dv, dzu, dzv], axis=1)
    early_g = _pack([db_f[:, :n_heads], dlg, dlb, dw_s, db_col.T, dag, dgg, dg_ffn, dg_final])
    dwt_main, (g_ff1, g_out, early_all) = _matmul(dz, h, name="in_proj_dw", out_dtype=BF16, trans_a=True, tk=seq,
                                                  phases=[_join_halves([g_ff1, g_out]), _gather_small(early_g)])
    dwt_f, _ = _matmul(dzf, h, name="gate_dw", out_dtype=BF16, trans_a=True)
    bwd_segments = []
    for j in range(N_CHIPS):
        first = j * shard_cols
        if j < gate_slot:
            bwd_segments.append([(0, shard_cols, 0, 0, first)])
        elif j > gate_slot:
            bwd_segments.append([(0, shard_cols, 0, 0, first - n_heads)])
        else:
            bwd_segments.append([(0, gate_row, 0, 0, first), (gate_row, gate_row + n_heads, 1, 0, 0),
                                 (gate_row + n_heads, shard_cols, 2, 0, qkv)])
    dw_in, _ = _realign_rows([dwt_main[None], dwt_f[None], dwt_main[None]], bwd_segments,
                             (N_CHIPS, shard_cols, d_model), name="dw_in_rows")
    dh_gate, (r_in,) = _matmul(dzf, wt_f, name="gate_dlhs", out_dtype=F32, phases=[_swap_halves([dw_in])])
    ps_in = pair_sum(dw_in, r_in, "w_in")
    dh, (q_in,) = _matmul(dz, wt_main, name="in_proj_dlhs", out_dtype=F32, residual=dh_gate, tk=2560,
                          phases=[_send_partials([ps_in])])
    g_in_sum = chip_sum(ps_in, q_in, "w_in")
    (grad_x, dg_mix), _ = _rmsnorm_bwd(dh, xs, dx1, norm_mix_g, name="norm_mix_bwd")
    late_g = _pack([dg_mix])
    g_in_sum, late_all = _exchange([_join_halves([g_in_sum]), _gather_small(late_g)], name="grads_join_w_in")

    big = {}
    for nm, g, w, m, v in zip(names, (g_in_sum, g_out, g_ff1, g_ff2), (wt_in, w_out[0], w_ff1[0], w_ff2[0]),
                              (mt_in, m_w_out[0], m_w_ff1[0], m_w_ff2[0]), (vt_in, v_w_out[0], v_w_ff1[0], v_w_ff2[0])):
        big[nm] = tuple((t.T if nm == "w_in" else t)[None] for t in _adamw(w, g, m, v, name="adamw_" + nm))

    small_params = dict(
        norm_mix_g=(norm_mix_g, m_norm_mix_g, v_norm_mix_g), b_f=(b_f, m_b_f, v_b_f),
        gmlp_ln_g=(gmlp_ln_g, m_gmlp_ln_g, v_gmlp_ln_g), gmlp_ln_b=(gmlp_ln_b, m_gmlp_ln_b, v_gmlp_ln_b),
        w_s=(w_s, m_w_s, v_w_s), b_s=(b_s, m_b_s, v_b_s), attn_out_g=(attn_out_g, m_attn_out_g, v_attn_out_g),
        gmlp_out_g=(gmlp_out_g, m_gmlp_out_g, v_gmlp_out_g), norm_ffn_g=(norm_ffn_g, m_norm_ffn_g, v_norm_ffn_g),
        norm_final_g=(norm_final_g, m_norm_final_g, v_norm_final_g))

    def small_step(group, grads_all, grads_own, label):
        w, m, v = ([small_params[nm][k] for nm in group] for k in range(3))
        packed = _adamw_small(grads_all, grads_own, place, _pack(w), _pack(m), _pack(v), name="adamw_small_" + label)
        parts = [_unpack(p, [a.shape for a in w]) for p in packed]
        return {nm: tuple(part[i] for part in parts) for i, nm in enumerate(group)}

    early = ["b_f", "gmlp_ln_g", "gmlp_ln_b", "w_s", "b_s", "attn_out_g", "gmlp_out_g", "norm_ffn_g", "norm_final_g"]
    small = {**small_step(early, early_all, early_g, "early"), **small_step(["norm_mix_g"], late_all, late_g, "late")}

    order = ["norm_mix_g", "w_in", "b_f", "gmlp_ln_g", "gmlp_ln_b", "w_s", "b_s", "attn_out_g", "gmlp_out_g", "w_out",
             "norm_ffn_g", "w_ff1", "w_ff2", "norm_final_g"]
    result = {**small, **big}
    total_loss = lax.psum(loss[0, 0], ("x", "y", "c"))
    outs = [total_loss, grad_x.reshape(x.shape)]
    for part in range(4):
        outs += [result[nm][part] for nm in order]
    return tuple(outs)
```

```python
import functools
import math

import numpy as np
import jax
import jax.numpy as jnp
from jax import lax
from jax.experimental import pallas as pl
from jax.experimental.pallas import tpu as pltpu

HEAD_DIM = 128
CHUNK = 128
EPS = 1e-6
LANES = 128
MXU_COLUMNS = 256
N_CHIPS = 4
N_DEV = 8
VMEM_LIMIT_BYTES = 56 * 1024 * 1024

ADAM_LR = 0.001
ADAM_B1 = 0.9
ADAM_B2 = 0.999
ADAM_EPS = 1e-08
ADAM_WD = 0.01
ADAM_STEP = 10

BF16 = jnp.bfloat16
F32 = jnp.float32
MESH = pl.DeviceIdType.MESH
ANY = pl.BlockSpec(memory_space=pl.ANY)
NEG_BIG = -1e30


def _params(*sem):
    return pltpu.CompilerParams(dimension_semantics=tuple(sem), vmem_limit_bytes=VMEM_LIMIT_BYTES)


def _tile(n, pref, unit):
    t = (min(pref, n) // unit) * unit
    while t >= unit:
        if n % t == 0:
            return t
        t -= unit
    return n


def _rc_tile(rows, cols, pref_rows=256, pref_cols=256):
    if rows % 16 == 0:
        return _tile(rows, pref_rows, 16), cols
    return rows, _tile(cols, pref_cols, LANES)


class _Phase:
    def __init__(self, arrays, out_shapes, in_place, n_sems, start, finish):
        self.arrays, self.out_shapes, self.in_place = list(arrays), list(out_shapes), in_place
        self.n_sems, self.start, self.finish = n_sems, start, finish

    @property
    def n_out(self):
        return len(self.arrays) if self.in_place else len(self.out_shapes)


def _run_phases(phases, steps, comm_in, comm_out, send_sems, recv_sems):
    at_in = at_out = at_sem = 0
    for ph in phases:
        for step in steps:
            getattr(ph, step)(comm_in[at_in:at_in + len(ph.arrays)], comm_out[at_out:at_out + ph.n_out],
                              lambda i, base=at_sem: send_sems.at[base + i], lambda i, base=at_sem: recv_sems.at[base + i])
        at_in, at_out, at_sem = at_in + len(ph.arrays), at_out + ph.n_out, at_sem + ph.n_sems


def _call(body, *, name, grid, in_specs, out_specs, out_shape, operands, semantics, scratch_shapes=(),
          n_prefetch=0, phases=()):
    in_specs, out_specs, out_shape = list(in_specs), list(out_specs), list(out_shape)
    scratch_shapes = list(scratch_shapes)
    n_in, n_out, n_scr = len(operands) - n_prefetch, len(out_shape), len(scratch_shapes)
    comm_in = [a for ph in phases for a in ph.arrays]
    comm_out = [jax.ShapeDtypeStruct(s.shape, s.dtype) for ph in phases
                for s in (ph.arrays if ph.in_place else ph.out_shapes)]
    aliases, at_in, at_out = {}, n_prefetch + n_in, n_out
    for ph in phases:
        if ph.in_place:
            aliases.update({at_in + r: at_out + r for r in range(len(ph.arrays))})
        at_in, at_out = at_in + len(ph.arrays), at_out + ph.n_out
    n_sems = sum(ph.n_sems for ph in phases)

    def hosted(*refs):
        pre, rest = refs[:n_prefetch], refs[n_prefetch:]
        ins, rest = rest[:n_in], rest[n_in:]
        cin, rest = rest[:len(comm_in)], rest[len(comm_in):]
        outs, rest = rest[:n_out], rest[n_out:]
        cout, rest = rest[:len(comm_out)], rest[len(comm_out):]
        scr = rest[:n_scr]
        if phases:
            send_sems, recv_sems = rest[n_scr:]
            ids = [pl.program_id(ax) for ax in range(len(grid))]
            first = functools.reduce(jnp.logical_and, [i == 0 for i in ids])
            last = functools.reduce(jnp.logical_and, [i == g - 1 for i, g in zip(ids, grid)])

            @pl.when(first)
            def _():
                _run_phases(phases, ("start",), cin, cout, send_sems, recv_sems)

        body(*pre, *ins, *outs, *scr)
        if phases:
            @pl.when(last)
            def _():
                _run_phases(phases, ("finish",), cin, cout, send_sems, recv_sems)

    all_in = in_specs + [ANY] * len(comm_in)
    all_out = out_specs + [ANY] * len(comm_out)
    all_scr = scratch_shapes + ([pltpu.SemaphoreType.DMA((n_sems,)), pltpu.SemaphoreType.DMA((n_sems,))] if phases else [])
    if phases:
        semantics = ("arbitrary",) * len(grid)
    kwargs = dict(name=name, out_shape=tuple(out_shape + comm_out), compiler_params=_params(*semantics),
                  input_output_aliases=aliases)
    if n_prefetch:
        kwargs["grid_spec"] = pltpu.PrefetchScalarGridSpec(
            num_scalar_prefetch=n_prefetch, grid=grid, in_specs=all_in, out_specs=tuple(all_out), scratch_shapes=all_scr)
    else:
        kwargs.update(grid=grid, in_specs=all_in, out_specs=tuple(all_out), scratch_shapes=all_scr)
    res = pl.pallas_call(hosted, **kwargs)(*operands, *comm_in)
    return tuple(res[:n_out]), tuple(res[n_out:])


def _only(results):
    outs, comm = results
    return outs[0] if len(outs) == 1 else outs, comm


def _matmul(a, b, *, name, out_dtype, trans_a=False, trans_b=False, tm=1024, tn=1024, tk=2048,
            square_lhs=False, relu=False, residual=None, scale2_by=None,
            b_sharded=False, out_sharded=False, phases=()):
    m, k = (a.shape[1], a.shape[0]) if trans_a else a.shape
    if b_sharded:
        if trans_b:
            n, ks = b.shape[1], b.shape[2]
            assert N_CHIPS * ks == k
        else:
            ns = b.shape[2]
            n = N_CHIPS * ns
            assert b.shape[1] == k
    else:
        n = b.shape[0] if trans_b else b.shape[1]
        assert (b.shape[1] if trans_b else b.shape[0]) == k
    tm = _tile(m, tm, 128)
    tn = _tile(n // N_CHIPS if (out_sharded or (b_sharded and not trans_b)) else n, tn, 128)
    tk = _tile(k // N_CHIPS if (b_sharded and trans_b) else k, tk, 128)
    nk = k // tk

    if trans_a:
        a_spec = pl.BlockSpec((tk, tm), lambda i, j, kk: (kk, i))
    else:
        a_spec = pl.BlockSpec((tm, tk), lambda i, j, kk: (i, kk))
    if b_sharded and trans_b:
        per = ks // tk
        assert per * tk == ks
        b_spec = pl.BlockSpec((None, tn, tk), lambda i, j, kk: (kk // per, j, kk % per))
    elif b_sharded:
        per = ns // tn
        assert per * tn == ns
        b_spec = pl.BlockSpec((None, tk, tn), lambda i, j, kk: (j // per, kk, j % per))
    elif trans_b:
        b_spec = pl.BlockSpec((tn, tk), lambda i, j, kk: (j, kk))
    else:
        b_spec = pl.BlockSpec((tk, tn), lambda i, j, kk: (kk, j))
    if out_sharded:
        ns_out = n // N_CHIPS
        per_o = ns_out // tn
        assert per_o * tn == ns_out
        out_shape = jax.ShapeDtypeStruct((N_CHIPS, m, ns_out), out_dtype)
        o_spec = pl.BlockSpec((None, tm, tn), lambda i, j, kk: (j // per_o, i, j % per_o))
    else:
        out_shape = jax.ShapeDtypeStruct((m, n), out_dtype)
        o_spec = pl.BlockSpec((tm, tn), lambda i, j, kk: (i, j))
    mn_spec = pl.BlockSpec((tm, tn), lambda i, j, kk: (i, j))

    operands, in_specs = [a, b], [a_spec, b_spec]
    if scale2_by is not None:
        operands.append(scale2_by)
        in_specs.append(mn_spec)
    if residual is not None:
        operands.append(residual)
        in_specs.append(mn_spec)
    dims = (((0 if trans_a else 1,), (1 if trans_b else 0,)), ((), ()))
    chunk = MXU_COLUMNS if tn % MXU_COLUMNS == 0 else tn

    def body(*refs):
        a_ref, b_ref = refs[0], refs[1]
        pos = 2
        scale_ref = res_ref = None
        if scale2_by is not None:
            scale_ref = refs[pos]
            pos += 1
        if residual is not None:
            res_ref = refs[pos]
            pos += 1
        o_ref = refs[pos]
        kk = pl.program_id(2)

        av = a_ref[...]
        if square_lhs:
            av = av.astype(F32)
            av = av * av
        av = av.astype(BF16)

        def finish(r, cols):
            if relu:
                r = jnp.maximum(r, 0.0)
            if scale_ref is not None:
                r = r * (2.0 * scale_ref[:, cols].astype(F32))
            if res_ref is not None:
                r = r + res_ref[:, cols].astype(F32)
            o_ref[:, cols] = r.astype(out_dtype)

        if nk == 1:
            for lo in range(0, tn, chunk):
                cols = slice(lo, lo + chunk)
                bv = (b_ref[cols, :] if trans_b else b_ref[:, cols]).astype(BF16)
                finish(lax.dot_general(av, bv, dims, preferred_element_type=F32), cols)
        else:
            acc_ref = refs[pos + 1]
            part = lax.dot_general(av, b_ref[...].astype(BF16), dims, preferred_element_type=F32)

            @pl.when(kk == 0)
            def _():
                acc_ref[...] = part

            @pl.when(jnp.logical_and(kk > 0, kk < nk - 1))
            def _():
                acc_ref[...] += part

            @pl.when(kk == nk - 1)
            def _():
                finish(acc_ref[...] + part, slice(None))

    return _only(_call(
        body, name=name, out_shape=[out_shape], grid=(m // tm, n // tn, nk),
        in_specs=in_specs, out_specs=[o_spec], operands=operands,
        scratch_shapes=[pltpu.VMEM((tm, tn), F32)] if nk > 1 else [],
        semantics=("parallel", "parallel", "arbitrary"), phases=phases))


def _rmsnorm_fwd(x, g, *, name, tr=512, phases=()):
    s, d = x.shape
    tr = _tile(s, tr, 8)

    def body(x_ref, g_ref, o_ref):
        xv = x_ref[...]
        r = lax.rsqrt(jnp.mean(xv * xv, axis=-1, keepdims=True) + EPS)
        o_ref[...] = ((xv * r) * g_ref[...]).astype(BF16)

    return _only(_call(
        body, name=name, out_shape=[jax.ShapeDtypeStruct((s, d), BF16)], grid=(s // tr,),
        in_specs=[pl.BlockSpec((tr, d), lambda i: (i, 0)), pl.BlockSpec((1, d), lambda i: (0, 0))],
        out_specs=[pl.BlockSpec((tr, d), lambda i: (i, 0))], operands=[x, g],
        semantics=("parallel",), phases=phases))


def _rms_bwd_rows(dy, xv, g):
    d = xv.shape[-1]
    r = lax.rsqrt(jnp.mean(xv * xv, axis=-1, keepdims=True) + EPS)
    gdy = dy * g
    dot = jnp.sum(gdy * xv, axis=-1, keepdims=True)
    dx = gdy * r - xv * (r * r * r) * (dot / d)
    return dx, dy * (xv * r)


def _rmsnorm_bwd(dy, x, res, g, *, name, tr=256, rounded_copy=False, phases=()):
    s, d = x.shape
    tr = _tile(s, tr, 8)

    def body(dy_ref, x_ref, res_ref, g_ref, dx_ref, dg_ref, *dxb_ref):
        @pl.when(pl.program_id(0) == 0)
        def _():
            dg_ref[...] = jnp.zeros_like(dg_ref)

        dx, dg_rows = _rms_bwd_rows(dy_ref[...].astype(F32), x_ref[...], g_ref[...])
        out = res_ref[...] + dx
        dx_ref[...] = out
        if rounded_copy:
            dxb_ref[0][...] = out.astype(BF16)
        dg_ref[...] += jnp.sum(dg_rows, axis=0, keepdims=True)

    row = pl.BlockSpec((tr, d), lambda i: (i, 0))
    vec = pl.BlockSpec((1, d), lambda i: (0, 0))
    extra = [jax.ShapeDtypeStruct((s, d), BF16)] if rounded_copy else []
    return _call(
        body, name=name,
        out_shape=[jax.ShapeDtypeStruct((s, d), F32), jax.ShapeDtypeStruct((1, d), F32)] + extra,
        grid=(s // tr,), in_specs=[row, row, row, vec], out_specs=[row, vec] + [row] * len(extra),
        operands=[dy, x, res, g], semantics=("arbitrary",), phases=phases)


def _loss_and_final_bwd(x2, target, g, *, name, tr=256):
    s, d = x2.shape
    tr = _tile(s, tr, 8)

    def body(x_ref, t_ref, g_ref, dx_ref, dxb_ref, dg_ref, loss_ref):
        @pl.when(pl.program_id(0) == 0)
        def _():
            dg_ref[...] = jnp.zeros_like(dg_ref)
            loss_ref[...] = jnp.zeros_like(loss_ref)

        xv, gv = x_ref[...], g_ref[...]
        r = lax.rsqrt(jnp.mean(xv * xv, axis=-1, keepdims=True) + EPS)
        err = (xv * r) * gv - t_ref[...]
        row_loss = jnp.mean(err * err, axis=-1, keepdims=True)
        loss_ref[...] += 0.5 * jnp.sum(row_loss, axis=0, keepdims=True)
        dx, dg_rows = _rms_bwd_rows(err / d, xv, gv)
        dx_ref[...] = dx
        dxb_ref[...] = dx.astype(BF16)
        dg_ref[...] += jnp.sum(dg_rows, axis=0, keepdims=True)

    row = pl.BlockSpec((tr, d), lambda i: (i, 0))
    vec = pl.BlockSpec((1, d), lambda i: (0, 0))
    one = pl.BlockSpec((1, 1), lambda i: (0, 0))
    return pl.pallas_call(
        body, name=name,
        out_shape=(jax.ShapeDtypeStruct((s, d), F32), jax.ShapeDtypeStruct((s, d), BF16),
                   jax.ShapeDtypeStruct((1, d), F32), jax.ShapeDtypeStruct((1, 1), F32)),
        grid=(s // tr,), in_specs=[row, row, vec], out_specs=(row, row, vec, one),
        compiler_params=_params("arbitrary"),
    )(x2, target, g)


def _tri_ones(n, lower):
    r = lax.broadcasted_iota(jnp.int32, (n, n), 0)
    c = lax.broadcasted_iota(jnp.int32, (n, n), 1)
    return jnp.where((c <= r) if lower else (c >= r), 1.0, 0.0).astype(F32)


def _forget_fwd(h, w_f, b_f, *, name, tr=256):
    s, d = h.shape
    tr = _tile(s, tr, 8)

    def body(h_ref, w_ref, b_ref, zb_ref, f_ref, carry):
        @pl.when(pl.program_id(0) == 0)
        def _():
            carry[...] = jnp.zeros_like(carry)

        zb = lax.dot_general(h_ref[...], w_ref[...], (((1,), (1,)), ((), ())), preferred_element_type=F32) + b_ref[...]
        zb_ref[...] = zb
        log_f = jnp.minimum(zb, 0.0) - jnp.log(1.0 + jnp.exp(-jnp.abs(zb)))
        run = jnp.dot(_tri_ones(tr, True), log_f, preferred_element_type=F32,
                      precision=lax.Precision.HIGHEST) + carry[...]
        f_ref[...] = run
        carry[...] = run[tr - 1:tr, :]

    row = pl.BlockSpec((tr, LANES), lambda i: (i, 0))
    return pl.pallas_call(
        body, name=name,
        out_shape=(jax.ShapeDtypeStruct((s, LANES), F32), jax.ShapeDtypeStruct((s, LANES), F32)),
        grid=(s // tr,),
        in_specs=[pl.BlockSpec((tr, d), lambda i: (i, 0)), pl.BlockSpec((LANES, d), lambda i: (0, 0)),
                  pl.BlockSpec((1, LANES), lambda i: (0, 0))],
        out_specs=(row, row), scratch_shapes=[pltpu.VMEM((1, LANES), F32)],
        compiler_params=_params("arbitrary"),
    )(h, w_f, b_f)


def _forget_bwd(d_f, zb, *, name, tr=256):
    s = zb.shape[0]
    tr = _tile(s, tr, 8)
    nb = s // tr

    def body(df_ref, zb_ref, dz_ref, db_ref, carry):
        @pl.when(pl.program_id(0) == 0)
        def _():
            carry[...] = jnp.zeros_like(carry)
            db_ref[...] = jnp.zeros_like(db_ref)

        run = jnp.dot(_tri_ones(tr, False), df_ref[...], preferred_element_type=F32,
                      precision=lax.Precision.HIGHEST) + carry[...]
        carry[...] = run[0:1, :]
        dz = run / (1.0 + jnp.exp(zb_ref[...]))
        dz_ref[...] = dz.astype(BF16)
        db_ref[...] += jnp.sum(dz, axis=0, keepdims=True)

    row = pl.BlockSpec((tr, LANES), lambda i: (nb - 1 - i, 0))
    return pl.pallas_call(
        body, name=name,
        out_shape=(jax.ShapeDtypeStruct((s, LANES), BF16), jax.ShapeDtypeStruct((1, LANES), F32)),
        grid=(nb,), in_specs=[row, row], out_specs=(row, pl.BlockSpec((1, LANES), lambda i: (0, 0))),
        scratch_shapes=[pltpu.VMEM((1, LANES), F32)],
        compiler_params=_params("arbitrary"),
    )(d_f, zb)


def _pairs(nblk, by_kv):
    if by_kv:
        pr = [(i, j) for j in range(nblk) for i in range(j, nblk)]
    else:
        pr = [(i, j) for i in range(nblk) for j in range(i + 1)]
    return (jnp.asarray(np.array([p[0] for p in pr], np.int32)), jnp.asarray(np.array([p[1] for p in pr], np.int32)))


def _causal_mask(rows, keys):
    r = lax.broadcasted_iota(jnp.int32, (rows[1] - rows[0], keys[1] - keys[0]), 0) + rows[0]
    c = lax.broadcasted_iota(jnp.int32, (rows[1] - rows[0], keys[1] - keys[0]), 1) + keys[0]
    return c <= r


def _diagonal_pieces(tb):
    half = tb // 2
    if half % LANES:
        return [((0, tb), (0, tb))]
    return [((0, half), (0, half)), ((half, tb), (0, tb))]


LOG2E = math.log2(math.e)
QK_TO_LOG2 = LOG2E / math.sqrt(HEAD_DIM)


def _attn_logits2(q, k, fk_row):
    sc = lax.dot_general(q, k, (((1,), (1,)), ((), ())), preferred_element_type=F32)
    return sc * QK_TO_LOG2 - fk_row * LOG2E


def _attn_fwd(z, f_row, n_heads, *, name, tb=1024, phases=()):
    s = z.shape[0]
    tb = _tile(s, tb, 128)
    nblk = s // tb
    qi, kj = _pairs(nblk, by_kv=False)

    def body(qi_ref, kj_ref, q_ref, k_ref, v_ref, fk_ref, o_ref, lse_ref, m_sc, l_sc, acc_sc):
        p = pl.program_id(1)
        i, j = qi_ref[p], kj_ref[p]

        @pl.when(j == 0)
        def _():
            m_sc[...] = jnp.full_like(m_sc, NEG_BIG)
            l_sc[...] = jnp.zeros_like(l_sc)
            acc_sc[...] = jnp.zeros_like(acc_sc)

        def update(rows, keys, masked):
            rs, ks = slice(*rows), slice(*keys)
            s2 = _attn_logits2(q_ref[rs, :], k_ref[ks, :], fk_ref[:, ks])
            if masked:
                s2 = jnp.where(_causal_mask(rows, keys), s2, NEG_BIG)
            m_old = m_sc[rs, :]
            m_new = jnp.maximum(m_old, jnp.max(s2, axis=-1, keepdims=True))
            alpha = jnp.exp2(m_old - m_new)
            pv = jnp.exp2(s2 - jnp.tile(m_new, (1, (keys[1] - keys[0]) // LANES)))
            l_sc[rs, :] = alpha * l_sc[rs, :] + jnp.sum(pv, axis=-1, keepdims=True)
            acc_sc[rs, :] = alpha * acc_sc[rs, :] + jnp.dot(pv.astype(BF16), v_ref[ks, :], preferred_element_type=F32)
            m_sc[rs, :] = m_new

        @pl.when(j < i)
        def _():
            update((0, tb), (0, tb), False)

        @pl.when(j == i)
        def _():
            for rows, keys in _diagonal_pieces(tb):
                update(rows, keys, True)
            o_ref[...] = (acc_sc[...] / l_sc[...]).astype(BF16)
            lse_ref[...] = m_sc[...] + jnp.log2(l_sc[...])

    h = n_heads
    return _call(
        body, name=name, n_prefetch=2, grid=(h, int(qi.shape[0])),
        in_specs=[
            pl.BlockSpec((tb, HEAD_DIM), lambda hh, p, qi_r, kj_r: (qi_r[p], hh)),
            pl.BlockSpec((tb, HEAD_DIM), lambda hh, p, qi_r, kj_r: (kj_r[p], h + hh)),
            pl.BlockSpec((tb, HEAD_DIM), lambda hh, p, qi_r, kj_r: (kj_r[p], 2 * h + hh)),
            pl.BlockSpec((None, 1, tb), lambda hh, p, qi_r, kj_r: (hh, 0, kj_r[p])),
        ],
        out_specs=[
            pl.BlockSpec((tb, HEAD_DIM), lambda hh, p, qi_r, kj_r: (qi_r[p], hh)),
            pl.BlockSpec((None, tb, LANES), lambda hh, p, qi_r, kj_r: (hh, qi_r[p], 0)),
        ],
        scratch_shapes=[pltpu.VMEM((tb, LANES), F32), pltpu.VMEM((tb, LANES), F32), pltpu.VMEM((tb, HEAD_DIM), F32)],
        out_shape=[jax.ShapeDtypeStruct((s, h * HEAD_DIM), BF16), jax.ShapeDtypeStruct((h, s, LANES), F32)],
        operands=[qi, kj, z, z, z, f_row], semantics=("parallel", "arbitrary"), phases=phases)


def _attn_bwd(z, o, d_o, lse2, f_row, n_heads, *, name, tb=1024, phases=()):
    s = z.shape[0]
    tb = _tile(s, tb, 128)
    nblk = s // tb
    qi, kj = _pairs(nblk, by_kv=True)
    n_pairs = int(qi.shape[0])
    scale = 1.0 / math.sqrt(HEAD_DIM)
    h = n_heads

    def body(qi_ref, kj_ref, q_ref, k_ref, v_ref, o_ref, do_ref, lse_ref, fk_ref,
             dq_ref, dk_ref, dv_ref, df_ref, dfq_ref, dq_sc, dk_sc, dv_sc, df_sc, dfq_sc):
        p = pl.program_id(1)
        i, j = qi_ref[p], kj_ref[p]

        @pl.when(p == 0)
        def _():
            dq_sc[...] = jnp.zeros_like(dq_sc)
            dfq_sc[...] = jnp.zeros_like(dfq_sc)

        @pl.when(i == j)
        def _():
            dk_sc[...] = jnp.zeros_like(dk_sc)
            dv_sc[...] = jnp.zeros_like(dv_sc)
            df_sc[...] = jnp.zeros_like(df_sc)

        def update(rows, keys, masked):
            rs, ks, n_rows = slice(*rows), slice(*keys), rows[1] - rows[0]
            q, k, v, do = q_ref[rs, :], k_ref[ks, :], v_ref[ks, :], do_ref[rs, :]
            delta = jnp.sum(do.astype(F32) * o_ref[rs, :].astype(F32), axis=-1, keepdims=True)
            pv = jnp.exp2(_attn_logits2(q, k, fk_ref[:, ks]) - jnp.tile(lse_ref[rs, :], (1, (keys[1] - keys[0]) // LANES)))
            if masked:
                pv = jnp.where(_causal_mask(rows, keys), pv, 0.0)
            dp = lax.dot_general(do, v, (((1,), (1,)), ((), ())), preferred_element_type=F32)
            ds = pv * (dp - delta)
            ds_b = ds.astype(BF16)
            dv_sc[ks, :] += lax.dot_general(pv.astype(BF16), do, (((0,), (0,)), ((), ())), preferred_element_type=F32)
            dk_sc[ks, :] += lax.dot_general(ds_b, q, (((0,), (0,)), ((), ())), preferred_element_type=F32)
            at = pl.ds(pl.multiple_of(i * tb + rows[0], LANES), n_rows)
            dq_sc[at, :] += jnp.dot(ds_b, k, preferred_element_type=F32)
            df_sc[:, ks] -= jnp.sum(ds, axis=0, keepdims=True)
            dfq_sc[at, :] += jnp.broadcast_to(jnp.sum(ds, axis=1, keepdims=True), (n_rows, LANES))

        @pl.when(i > j)
        def _():
            update((0, tb), (0, tb), False)

        @pl.when(i == j)
        def _():
            for rows, keys in _diagonal_pieces(tb):
                update(rows, keys, True)

        @pl.when(i == nblk - 1)
        def _():
            dk_ref[...] = (dk_sc[...] * scale).astype(BF16)
            dv_ref[...] = dv_sc[...].astype(BF16)
            df_ref[...] = df_sc[...]

        @pl.when(p == n_pairs - 1)
        def _():
            dq_ref[...] = (dq_sc[...] * scale).astype(BF16)
            dfq_ref[...] = jnp.transpose(dfq_sc[...])[0:1, :]

    qblk = lambda off: pl.BlockSpec((tb, HEAD_DIM), lambda hh, p, qi_r, kj_r: (qi_r[p], off + hh))
    kblk = lambda off: pl.BlockSpec((tb, HEAD_DIM), lambda hh, p, qi_r, kj_r: (kj_r[p], off + hh))
    qrep = pl.BlockSpec((None, tb, LANES), lambda hh, p, qi_r, kj_r: (hh, qi_r[p], 0))
    krow = pl.BlockSpec((None, 1, tb), lambda hh, p, qi_r, kj_r: (hh, 0, kj_r[p]))
    act = jax.ShapeDtypeStruct((s, h * HEAD_DIM), BF16)
    return _call(
        body, name=name, n_prefetch=2, grid=(h, n_pairs),
        in_specs=[qblk(0), kblk(h), kblk(2 * h), qblk(0), qblk(0), qrep, krow],
        out_specs=[
            pl.BlockSpec((s, HEAD_DIM), lambda hh, p, qi_r, kj_r: (0, hh)),
            kblk(0), kblk(0), krow,
            pl.BlockSpec((None, 1, s), lambda hh, p, qi_r, kj_r: (hh, 0, 0)),
        ],
        scratch_shapes=[pltpu.VMEM((s, HEAD_DIM), F32), pltpu.VMEM((tb, HEAD_DIM), F32),
                        pltpu.VMEM((tb, HEAD_DIM), F32), pltpu.VMEM((1, tb), F32), pltpu.VMEM((s, LANES), F32)],
        out_shape=[act, act, act, jax.ShapeDtypeStruct((h, 1, s), F32), jax.ShapeDtypeStruct((h, 1, s), F32)],
        operands=[qi, kj, z, z, z, o, d_o, lse2, f_row], semantics=("parallel", "arbitrary"), phases=phases)


GELU_C = math.sqrt(2.0 / math.pi)
GELU_A = 0.044715


def _gelu(x):
    return 0.5 * x * (1.0 + jnp.tanh(GELU_C * (x + GELU_A * (x * x * x))))


def _gelu_and_grad(x):
    t = jnp.tanh(GELU_C * (x + GELU_A * (x * x * x)))
    y = 0.5 * x * (1.0 + t)
    dy = 0.5 * (1.0 + t) + 0.5 * x * (1.0 - t * t) * (GELU_C * (1.0 + 3.0 * GELU_A * (x * x)))
    return y, dy


def _layernorm_parts(g):
    mu = jnp.mean(g, axis=-1, keepdims=True)
    xc = g - mu
    rs = lax.rsqrt(jnp.mean(xc * xc, axis=-1, keepdims=True) + EPS)
    return xc * rs, rs


def _spatial_mix(w_ref, bcol_ref, vv_b, n_heads, n_chunks):
    tril = _causal_mask((0, CHUNK), (0, CHUNK))
    cols = []
    for hh in range(n_heads):
        wc = jnp.where(tril, w_ref[hh], 0.0).astype(BF16)
        lanes = slice(hh * HEAD_DIM, (hh + 1) * HEAD_DIM)
        rows = [jnp.dot(wc, vv_b[c * CHUNK:(c + 1) * CHUNK, lanes], preferred_element_type=F32)
                + bcol_ref[:, hh:hh + 1] for c in range(n_chunks)]
        cols.append(jnp.concatenate(rows, axis=0))
    return jnp.concatenate(cols, axis=1)


def _mix_fwd(z, o, ln_g, ln_b, w_s, b_col, attn_g, gm_g, n_heads, *, name, tr=256):
    s = z.shape[0]
    dg = n_heads * HEAD_DIM
    tr = _tile(s, tr, CHUNK)
    n_chunks = tr // CHUNK

    def body(zu_ref, zv_ref, o_ref, lg_ref, lb_ref, w_ref, bcol_ref, ag_ref, gg_ref, out_ref):
        u = _gelu(zu_ref[...].astype(F32))
        xhat, _ = _layernorm_parts(_gelu(zv_ref[...].astype(F32)))
        vv = xhat * lg_ref[...] + lb_ref[...]
        gm = u * _spatial_mix(w_ref, bcol_ref, vv.astype(BF16), n_heads, n_chunks)
        rg = lax.rsqrt(jnp.mean(gm * gm, axis=-1, keepdims=True) + EPS)
        ov = o_ref[...].astype(F32)
        ra = lax.rsqrt(jnp.mean(ov * ov, axis=-1, keepdims=True) + EPS)
        out_ref[:, :dg] = ((ov * ra) * ag_ref[...]).astype(BF16)
        out_ref[:, dg:] = ((gm * rg) * gg_ref[...]).astype(BF16)

    vec = pl.BlockSpec((1, dg), lambda i: (0, 0))
    return pl.pallas_call(
        body, name=name, out_shape=jax.ShapeDtypeStruct((s, 2 * dg), BF16), grid=(s // tr,),
        in_specs=[pl.BlockSpec((tr, dg), lambda i: (i, 3)), pl.BlockSpec((tr, dg), lambda i: (i, 4)),
                  pl.BlockSpec((tr, dg), lambda i: (i, 0)), vec, vec,
                  pl.BlockSpec((n_heads, CHUNK, CHUNK), lambda i: (0, 0, 0)),
                  pl.BlockSpec((CHUNK, n_heads), lambda i: (0, 0)), vec, vec],
        out_specs=pl.BlockSpec((tr, 2 * dg), lambda i: (i, 0)),
        compiler_params=_params("parallel"),
    )(z, z, o, ln_g, ln_b, w_s, b_col, attn_g, gm_g)


def _mix_bwd(z, o, d_merged, ln_g, ln_b, w_s, b_col, attn_g, gm_g, n_heads, *, name, tr=256):
    s = z.shape[0]
    dg = n_heads * HEAD_DIM
    tr = _tile(s, tr, CHUNK)
    n_chunks = tr // CHUNK

    def body(zu_ref, zv_ref, o_ref, dm_ref, lg_ref, lb_ref, w_ref, bcol_ref, ag_ref, gg_ref,
             do_ref, dzu_ref, dzv_ref, dw_ref, dbcol_ref, dlg_ref, dlb_ref, dag_ref, dgg_ref):
        @pl.when(pl.program_id(0) == 0)
        def _():
            for ref in (dw_ref, dbcol_ref, dlg_ref, dlb_ref, dag_ref, dgg_ref):
                ref[...] = jnp.zeros_like(ref)

        d_o, dag_rows = _rms_bwd_rows(dm_ref[:, :dg], o_ref[...].astype(F32), ag_ref[...])
        do_ref[...] = d_o.astype(BF16)
        dag_ref[...] += jnp.sum(dag_rows, axis=0, keepdims=True)

        u, du_dz = _gelu_and_grad(zu_ref[...].astype(F32))
        gv, dgv_dz = _gelu_and_grad(zv_ref[...].astype(F32))
        xhat, rs = _layernorm_parts(gv)
        lg = lg_ref[...]
        vv_b = (xhat * lg + lb_ref[...]).astype(BF16)
        mix = _spatial_mix(w_ref, bcol_ref, vv_b, n_heads, n_chunks)
        gm = u * mix
        d_gm, dgg_rows = _rms_bwd_rows(dm_ref[:, dg:], gm, gg_ref[...])
        dgg_ref[...] += jnp.sum(dgg_rows, axis=0, keepdims=True)
        dzu_ref[...] = ((d_gm * mix) * du_dz).astype(BF16)
        d_mix = d_gm * u
        d_mix_b = d_mix.astype(BF16)

        tril = _causal_mask((0, CHUNK), (0, CHUNK))
        lane = lax.broadcasted_iota(jnp.int32, (CHUNK, n_heads), 1)
        cols = []
        db = jnp.zeros((CHUNK, n_heads), F32)
        for hh in range(n_heads):
            wc = jnp.where(tril, w_ref[hh], 0.0).astype(BF16)
            lanes = slice(hh * HEAD_DIM, (hh + 1) * HEAD_DIM)
            dw = jnp.zeros((CHUNK, CHUNK), F32)
            dmix_sum = jnp.zeros((CHUNK, HEAD_DIM), F32)
            rows = []
            for c in range(n_chunks):
                rws = slice(c * CHUNK, (c + 1) * CHUNK)
                dmb = d_mix_b[rws, lanes]
                dw += lax.dot_general(dmb, vv_b[rws, lanes], (((1,), (1,)), ((), ())), preferred_element_type=F32)
                rows.append(lax.dot_general(wc, dmb, (((0,), (0,)), ((), ())), preferred_element_type=F32))
                dmix_sum += d_mix[rws, lanes]
            dw_ref[hh] += jnp.where(tril, dw, 0.0)
            db += jnp.where(lane == hh, jnp.sum(dmix_sum, axis=-1, keepdims=True), 0.0)
            cols.append(jnp.concatenate(rows, axis=0))
        dbcol_ref[...] += db
        d_vv = jnp.concatenate(cols, axis=1)

        dlg_ref[...] += jnp.sum(d_vv * xhat, axis=0, keepdims=True)
        dlb_ref[...] += jnp.sum(d_vv, axis=0, keepdims=True)
        d_xhat = d_vv * lg
        d_gv = rs * (d_xhat - jnp.mean(d_xhat, axis=-1, keepdims=True)
                     - xhat * jnp.mean(d_xhat * xhat, axis=-1, keepdims=True))
        dzv_ref[...] = (d_gv * dgv_dz).astype(BF16)

    vec = pl.BlockSpec((1, dg), lambda i: (0, 0))
    wspec = pl.BlockSpec((n_heads, CHUNK, CHUNK), lambda i: (0, 0, 0))
    bspec = pl.BlockSpec((CHUNK, n_heads), lambda i: (0, 0))
    rowb = pl.BlockSpec((tr, dg), lambda i: (i, 0))
    act = jax.ShapeDtypeStruct((s, dg), BF16)
    vshape = jax.ShapeDtypeStruct((1, dg), F32)
    return pl.pallas_call(
        body, name=name,
        out_shape=(act, act, act, jax.ShapeDtypeStruct((n_heads, CHUNK, CHUNK), F32),
                   jax.ShapeDtypeStruct((CHUNK, n_heads), F32), vshape, vshape, vshape, vshape),
        grid=(s // tr,),
        in_specs=[pl.BlockSpec((tr, dg), lambda i: (i, 3)), pl.BlockSpec((tr, dg), lambda i: (i, 4)),
                  rowb, pl.BlockSpec((tr, 2 * dg), lambda i: (i, 0)), vec, vec, wspec, bspec, vec, vec],
        out_specs=(rowb, rowb, rowb, wspec, bspec, vec, vec, vec, vec),
        compiler_params=_params("arbitrary"),
    )(z, z, o, d_merged, ln_g, ln_b, w_s, b_col, attn_g, gm_g)


def _place():
    x, y, c = lax.axis_index("x"), lax.axis_index("y"), lax.axis_index("c")
    other_chips = [(1 - x, y), (x, 1 - y), (1 - x, 1 - y)]
    return x, y, c, other_chips


def _remote(src, dst, send_sem, recv_sem, to):
    return pltpu.make_async_remote_copy(src_ref=src, dst_ref=dst, send_sem=send_sem, recv_sem=recv_sem,
                                        device_id=to, device_id_type=MESH)


def _cast_into_slot(w, place, *, name, phases=()):
    rows, cols = w.shape
    tr, tc = _rc_tile(rows, cols)

    def body(place_ref, w_ref, o_ref):
        o_ref[...] = w_ref[...].astype(BF16)

    return _only(_call(
        body, name=name, n_prefetch=1, grid=(rows // tr, cols // tc),
        in_specs=[pl.BlockSpec((tr, tc), lambda i, j, pr: (i, j))],
        out_specs=[pl.BlockSpec((None, tr, tc), lambda i, j, pr: (pr[0], i, j))],
        out_shape=[jax.ShapeDtypeStruct((N_CHIPS, rows, cols), BF16)], operands=[place, w],
        semantics=("parallel", "parallel"), phases=phases))


def _casts_and_norm(weights, place, x, g, *, name, rows=256, phases=()):
    cols = x.shape[1]
    jobs = [w.shape[0] // rows for w in weights] + [x.shape[0] // rows]
    assert all(w.shape[1] == cols and w.shape[0] % rows == 0 for w in weights) and x.shape[0] % rows == 0
    first = [sum(jobs[:k]) for k in range(len(jobs))]

    def strip(k):
        return lambda t: jnp.clip(t - first[k], 0, jobs[k] - 1)

    def body(place_ref, *refs):
        n = len(weights)
        w_refs, x_ref, g_ref, outs = refs[:n], refs[n], refs[n + 1], refs[n + 2:]
        t = pl.program_id(0)
        for k in range(n):
            @pl.when(jnp.logical_and(t >= first[k], t < first[k] + jobs[k]))
            def _(k=k):
                outs[k][...] = w_refs[k][...].astype(BF16)

        @pl.when(t >= first[n])
        def _():
            xv = x_ref[...]
            r = lax.rsqrt(jnp.mean(xv * xv, axis=-1, keepdims=True) + EPS)
            outs[n][...] = ((xv * r) * g_ref[...]).astype(BF16)

    in_specs = [pl.BlockSpec((rows, cols), lambda t, pr, k=k: (strip(k)(t), 0)) for k in range(len(weights))]
    in_specs += [pl.BlockSpec((rows, cols), lambda t, pr: (strip(len(weights))(t), 0)),
                 pl.BlockSpec((1, cols), lambda t, pr: (0, 0))]
    out_specs = [pl.BlockSpec((None, rows, cols), lambda t, pr, k=k: (pr[0], strip(k)(t), 0)) for k in range(len(weights))]
    out_specs.append(pl.BlockSpec((rows, cols), lambda t, pr: (strip(len(weights))(t), 0)))
    out_shape = [jax.ShapeDtypeStruct((N_CHIPS,) + w.shape, BF16) for w in weights] + [jax.ShapeDtypeStruct(x.shape, BF16)]
    return _call(body, name=name, n_prefetch=1, grid=(sum(jobs),), in_specs=in_specs, out_specs=out_specs,
                 out_shape=out_shape, operands=[place, *weights, x, g], semantics=("arbitrary",), phases=phases)


def _exchange(phases, *, name):
    comm_in = [a for ph in phases for a in ph.arrays]
    comm_out = [jax.ShapeDtypeStruct(s.shape, s.dtype) for ph in phases for s in (ph.arrays if ph.in_place else ph.out_shapes)]
    aliases, at_in, at_out = {}, 0, 0
    for ph in phases:
        if ph.in_place:
            aliases.update({at_in + r: at_out + r for r in range(len(ph.arrays))})
        at_in, at_out = at_in + len(ph.arrays), at_out + ph.n_out
    n_sems = sum(ph.n_sems for ph in phases)

    def body(*refs):
        cin, cout = refs[:len(comm_in)], refs[len(comm_in):len(comm_in) + len(comm_out)]
        send_sems, recv_sems = refs[len(comm_in) + len(comm_out):]
        _run_phases(phases, ("start", "finish"), cin, cout, send_sems, recv_sems)

    return pl.pallas_call(
        body, name=name, out_shape=tuple(comm_out), in_specs=[ANY] * len(comm_in), out_specs=tuple([ANY] * len(comm_out)),
        input_output_aliases=aliases,
        scratch_shapes=[pltpu.SemaphoreType.DMA((n_sems,)), pltpu.SemaphoreType.DMA((n_sems,))],
    )(*comm_in)


GATHER_PARTS = 4


def _gather(bufs, stage, part=(0, GATHER_PARTS)):
    n = 3 * len(bufs)
    lo, hi = part

    def copies(outs, send, recv, d2d, incoming):
        x, y, c, chips = _place()
        for t, buf in enumerate(outs):
            half = buf.shape[2] // 2
            piece = half // GATHER_PARTS
            for k, (cx, cy) in enumerate(chips):
                i = 3 * t + k + (n if (d2d and stage == "both") else 0)
                cols = pl.ds(((1 - c) if (d2d and incoming) else c) * half + lo * piece, (hi - lo) * piece)
                blk = buf.at[(2 * cx + cy) if (d2d or incoming) else (2 * x + y), :, cols]
                yield _remote(blk, blk, send(i), recv(i), (x, y, 1 - c) if d2d else (cx, cy, c))

    def start(ins, outs, send, recv):
        for cp in copies(outs, send, recv, stage == "d2d", False):
            cp.start()

    def finish(ins, outs, send, recv):
        if stage == "both":
            for arrival, onward in zip(copies(outs, send, recv, False, True), copies(outs, send, recv, True, False)):
                arrival.wait_recv()
                onward.start()
        for cp in copies(outs, send, recv, stage != "ici", True):
            cp.wait_recv()
        for d2d in ((False, True) if stage == "both" else (stage == "d2d",)):
            for cp in copies(outs, send, recv, d2d, False):
                cp.wait_send()

    return _Phase(bufs, [], True, (2 if stage == "both" else 1) * n, start, finish)


def _merge(first, second):
    n_first = first.n_sems

    def later(sem):
        return lambda i: sem(n_first + i)

    def start(ins, outs, send, recv):
        first.start(ins, outs, send, recv)
        second.start(ins, outs, later(send), later(recv))

    def finish(ins, outs, send, recv):
        first.finish(ins, outs, send, recv)
        second.finish(ins, outs, later(send), later(recv))

    return _Phase(first.arrays, [], True, n_first + second.n_sems, start, finish)


def _gather_by_parts(bufs, lo, hi):
    phase = _gather(bufs, "both", (lo, lo + 1))
    for part in range(lo + 1, hi):
        phase = _merge(phase, _gather(bufs, "both", (part, part + 1)))
    return phase


def _swap_halves(grads):
    def copies(ins, outs, send, recv):
        x, y, c, _ = _place()
        for t, g in enumerate(ins):
            half = g.shape[2] // 2
            yield _remote(g.at[:, :, pl.ds((1 - c) * half, half)], outs[t], send(t), recv(t), (x, y, 1 - c))

    def start(ins, outs, send, recv):
        for cp in copies(ins, outs, send, recv):
            cp.start()

    def finish(ins, outs, send, recv):
        for cp in copies(ins, outs, send, recv):
            cp.wait()

    shapes = [jax.ShapeDtypeStruct((a.shape[0], a.shape[1], a.shape[2] // 2), a.dtype) for a in grads]
    return _Phase(grads, shapes, False, len(grads), start, finish)


def _add_halves(grad, received, place, *, name):
    ns, rows, half = received.shape
    tr, tc = _rc_tile(rows, half, pref_rows=1024)
    per = half // tc

    def body(place_ref, g_ref, r_ref, o_ref):
        o_ref[...] = (g_ref[...].astype(F32) + r_ref[...].astype(F32)).astype(BF16)

    grid_spec = pltpu.PrefetchScalarGridSpec(
        num_scalar_prefetch=1, grid=(ns, rows // tr, per),
        in_specs=[pl.BlockSpec((None, tr, tc), lambda s, i, j, pr: (s, i, pr[1] * per + j)),
                  pl.BlockSpec((None, tr, tc), lambda s, i, j, pr: (s, i, j))],
        out_specs=pl.BlockSpec((None, tr, tc), lambda s, i, j, pr: (s, i, j)),
    )
    return pl.pallas_call(
        body, name=name, grid_spec=grid_spec, out_shape=jax.ShapeDtypeStruct(received.shape, BF16),
        compiler_params=_params("parallel", "parallel", "parallel"),
    )(place, grad, received)


def _send_partials(parts, piece=(0, 1)):
    k_th, n_pieces = piece

    def cols(part):
        width = part.shape[2] // n_pieces
        return pl.ds(k_th * width, width)

    def start(ins, outs, send, recv):
        x, y, c, chips = _place()
        for t, part in enumerate(ins):
            for k, (cx, cy) in enumerate(chips):
                _remote(part.at[2 * cx + cy, :, cols(part)], outs[t].at[2 * x + y],
                        send(3 * t + k), recv(3 * t + k), (cx, cy, c)).start()

    def finish(ins, outs, send, recv):
        x, y, c, chips = _place()
        for t, part in enumerate(ins):
            for k, (cx, cy) in enumerate(chips):
                slot = outs[t].at[2 * cx + cy]
                _remote(slot, slot, send(3 * t + k), recv(3 * t + k), (cx, cy, c)).wait_recv()
        for t, part in enumerate(ins):
            for k, (cx, cy) in enumerate(chips):
                sent = part.at[2 * cx + cy, :, cols(part)]
                _remote(sent, sent, send(3 * t + k), recv(3 * t + k), (cx, cy, c)).wait_send()

    shapes = [jax.ShapeDtypeStruct(a.shape[:2] + (a.shape[2] // n_pieces,), a.dtype) for a in parts]
    return _Phase(parts, shapes, False, 3 * len(parts), start, finish)


def _sum_chips(parts, slots, place, *, name, piece=(0, 1), into=None):
    ns, rows, width = slots.shape
    k_th, n_pieces = piece
    half = width * n_pieces
    tr, tc = _rc_tile(rows, width, pref_rows=512)
    per = width // tc

    def body(place_ref, p_ref, s_ref, *rest):
        acc = p_ref[...].astype(F32)
        for k in range(ns):
            acc = acc + jnp.where(place_ref[0] == k, 0.0, s_ref[k].astype(F32))
        rest[-1][...] = acc

    grid_spec = pltpu.PrefetchScalarGridSpec(
        num_scalar_prefetch=1, grid=(rows // tr, per),
        in_specs=[pl.BlockSpec((None, tr, tc), lambda i, j, pr: (pr[0], i, k_th * per + j)),
                  pl.BlockSpec((ns, tr, tc), lambda i, j, pr: (0, i, j))] + ([ANY] if into is not None else []),
        out_specs=pl.BlockSpec((tr, tc), lambda i, j, pr: (i, (pr[1] * n_pieces + k_th) * per + j)),
    )
    return pl.pallas_call(
        body, name=name, grid_spec=grid_spec, out_shape=jax.ShapeDtypeStruct((rows, 2 * half), F32),
        input_output_aliases={3: 0} if into is not None else {},
        compiler_params=_params("parallel", "parallel"),
    )(place, parts, slots, *([into] if into is not None else []))


def _join_halves(bufs):
    def copies(outs, send, recv, incoming):
        x, y, c, _ = _place()
        for t, buf in enumerate(outs):
            half = buf.shape[1] // 2
            cols = buf.at[:, pl.ds(((1 - c) if incoming else c) * half, half)]
            yield _remote(cols, cols, send(t), recv(t), (x, y, 1 - c))

    def start(ins, outs, send, recv):
        for cp in copies(outs, send, recv, False):
            cp.start()

    def finish(ins, outs, send, recv):
        for cp in copies(outs, send, recv, True):
            cp.wait_recv()
        for cp in copies(outs, send, recv, False):
            cp.wait_send()

    return _Phase(bufs, [], True, len(bufs), start, finish)


def _gather_small(buf):
    def slot(out, px, py, pc):
        return out.at[4 * px + 2 * py + pc]

    def start(ins, outs, send, recv):
        x, y, c, chips = _place()
        mine = slot(outs[0], x, y, c)
        _remote(ins[0], mine, send(0), recv(0), (x, y, 1 - c)).start()
        for k, (cx, cy) in enumerate(chips):
            _remote(ins[0], mine, send(1 + k), recv(1 + k), (cx, cy, c)).start()

    def finish(ins, outs, send, recv):
        x, y, c, chips = _place()
        sibling = (x, y, 1 - c)
        for k, (cx, cy) in enumerate(chips):
            arrived = slot(outs[0], cx, cy, c)
            _remote(arrived, arrived, send(1 + k), recv(1 + k), sibling).wait_recv()
            _remote(arrived, arrived, send(4 + k), recv(4 + k), sibling).start()
        theirs = slot(outs[0], x, y, 1 - c)
        _remote(theirs, theirs, send(0), recv(0), sibling).wait_recv()
        for k, (cx, cy) in enumerate(chips):
            passed = slot(outs[0], cx, cy, 1 - c)
            _remote(passed, passed, send(4 + k), recv(4 + k), sibling).wait_recv()
        for i in range(7):
            _remote(ins[0], ins[0], send(i), recv(i), sibling).wait_send()

    return _Phase([buf], [jax.ShapeDtypeStruct((N_DEV,) + buf.shape, buf.dtype)], False, 7, start, finish)


def _adamw_math(w, g, m, v):
    m = ADAM_B1 * m + (1.0 - ADAM_B1) * g
    v = ADAM_B2 * v + (1.0 - ADAM_B2) * (g * g)
    m_hat = m / (1.0 - ADAM_B1 ** ADAM_STEP)
    v_hat = v / (1.0 - ADAM_B2 ** ADAM_STEP)
    delta = -ADAM_LR * (m_hat / (jnp.sqrt(v_hat) + ADAM_EPS) + ADAM_WD * w)
    return delta, m, v


def _adamw(w, g, m, v, *, name):
    rows, cols = w.shape
    tr, tc = _rc_tile(rows, cols)

    def body(w_ref, g_ref, m_ref, v_ref, go_ref, d_ref, mo_ref, vo_ref):
        g = g_ref[...]
        go_ref[...] = g
        d_ref[...], mo_ref[...], vo_ref[...] = _adamw_math(w_ref[...], g, m_ref[...], v_ref[...])

    blk = pl.BlockSpec((tr, tc), lambda i, j: (i, j))
    shape = jax.ShapeDtypeStruct((rows, cols), F32)
    return pl.pallas_call(
        body, name=name, out_shape=(shape, shape, shape, shape), grid=(rows // tr, cols // tc),
        in_specs=[blk] * 4, out_specs=(blk, blk, blk, blk), compiler_params=_params("parallel", "parallel"),
    )(w, g, m, v)


def _adamw_small(gathered, own, place, w, m, v, *, name):
    nd = gathered.shape[0]

    def body(place_ref, gs_ref, own_ref, w_ref, m_ref, v_ref, g_ref, d_ref, mo_ref, vo_ref):
        me = 2 * place_ref[0] + place_ref[1]
        g = jnp.zeros(own_ref.shape, F32)
        for k in range(nd):
            g = g + jnp.where(me == k, own_ref[...], gs_ref[k])
        g_ref[...] = g
        d_ref[...], mo_ref[...], vo_ref[...] = _adamw_math(w_ref[...], g, m_ref[...], v_ref[...])

    whole = pl.BlockSpec(w.shape, lambda i, pr: (0, 0))
    grid_spec = pltpu.PrefetchScalarGridSpec(
        num_scalar_prefetch=1, grid=(1,),
        in_specs=[pl.BlockSpec(gathered.shape, lambda i, pr: (0, 0, 0)), whole, whole, whole, whole],
        out_specs=(whole, whole, whole, whole))
    shape = jax.ShapeDtypeStruct(w.shape, F32)
    return pl.pallas_call(body, name=name, grid_spec=grid_spec, out_shape=(shape, shape, shape, shape),
                          compiler_params=_params("arbitrary"))(place, gathered, own, w, m, v)


def _pack(parts):
    flat = jnp.concatenate([p.reshape(-1).astype(F32) for p in parts])
    rows = -(-flat.shape[0] // (8 * LANES)) * 8
    return jnp.pad(flat, (0, rows * LANES - flat.shape[0])).reshape(rows, LANES)


def _unpack(buf, shapes):
    flat = buf.reshape(-1)
    out, pos = [], 0
    for shp in shapes:
        size = int(np.prod(shp))
        out.append(flat[pos:pos + size].reshape(shp))
        pos += size
    return out


ROW_BLOCK = 256


def _realign_rows(sources, segments, out_shape, *, name):
    n_slots, rows, cols = out_shape
    n_src = len(sources)
    per_slot = -(-rows // ROW_BLOCK)
    table = np.zeros((6, n_slots * per_slot, n_src), np.int32)
    for so in range(n_slots):
        for first, last, src, src_slot, src_row in segments[so]:
            for blk in range(first // ROW_BLOCK, (last - 1) // ROW_BLOCK + 1):
                lo, hi = max(first, blk * ROW_BLOCK), min(last, (blk + 1) * ROW_BLOCK)
                base = src_row + (blk * ROW_BLOCK - first)
                m0 = (base + lo - blk * ROW_BLOCK) // ROW_BLOCK
                at = so * per_slot + blk
                assert table[4, at, src] == 0, "two segments of one block share a source operand"
                table[:, at, src] = (src_slot, m0, base - m0 * ROW_BLOCK, lo - blk * ROW_BLOCK, hi - blk * ROW_BLOCK,
                                     min(2 * ROW_BLOCK, sources[src].shape[1] - m0 * ROW_BLOCK))
    last_block = [-(-a.shape[1] // ROW_BLOCK) - 1 for a in sources]

    def body(slot_ref, blk_ref, off_ref, lo_ref, hi_ref, valid_ref, *refs):
        o_ref, acc = refs[2 * n_src], refs[2 * n_src + 1]
        at = (pl.program_id(0) * per_slot + pl.program_id(1)) * n_src
        acc[...] = jnp.zeros_like(acc)
        for p in range(n_src):
            @pl.when(hi_ref[at + p] > lo_ref[at + p])
            def _():
                two = jnp.concatenate([refs[2 * p][...], refs[2 * p + 1][...]], axis=0)
                src_row = lax.broadcasted_iota(jnp.int32, two.shape, 0)
                two = jnp.where(src_row < valid_ref[at + p], two, jnp.zeros_like(two))
                r = lax.broadcasted_iota(jnp.int32, (ROW_BLOCK, 2 * ROW_BLOCK), 0)
                c = lax.broadcasted_iota(jnp.int32, (ROW_BLOCK, 2 * ROW_BLOCK), 1)
                place = (c == r + off_ref[at + p]) & (r >= lo_ref[at + p]) & (r < hi_ref[at + p])
                acc[...] += jnp.dot(place.astype(two.dtype), two, preferred_element_type=F32)
        o_ref[...] = acc[...].astype(o_ref.dtype)

    def src_spec(p, second):
        def index(so, i, slot_r, blk_r, off_r, lo_r, hi_r, valid_r):
            at = (so * per_slot + i) * n_src + p
            return slot_r[at], jnp.minimum(blk_r[at] + second, last_block[p]), 0
        return pl.BlockSpec((None, ROW_BLOCK, cols), index)

    grid_spec = pltpu.PrefetchScalarGridSpec(
        num_scalar_prefetch=6, grid=(n_slots, per_slot),
        in_specs=[src_spec(p, second) for p in range(n_src) for second in (0, 1)],
        out_specs=pl.BlockSpec((None, ROW_BLOCK, cols), lambda so, i, *_: (so, i, 0)),
        scratch_shapes=[pltpu.VMEM((ROW_BLOCK, cols), F32)],
    )
    flat = [jnp.asarray(table[k].reshape(-1)) for k in range(6)]
    return pl.pallas_call(
        body, name=name, grid_spec=grid_spec, out_shape=jax.ShapeDtypeStruct(out_shape, sources[0].dtype),
        compiler_params=_params("parallel", "arbitrary"),
    )(*flat, *[a for a in sources for _ in (0, 1)])


def _shard_rows(g, lo, hi):
    rs = g.shape[1]
    pieces = []
    for j in range(g.shape[0]):
        a, b = max(lo, j * rs), min(hi, (j + 1) * rs)
        if a < b:
            pieces.append(g[j, a - j * rs:b - j * rs])
    return pieces


def kernel(x, norm_mix_g, w_in, b_f, gmlp_ln_g, gmlp_ln_b, w_s, b_s, attn_out_g, gmlp_out_g, w_out, norm_ffn_g, w_ff1, w_ff2, norm_final_g, loss_target, m_norm_mix_g, m_w_in, m_b_f, m_gmlp_ln_g, m_gmlp_ln_b, m_w_s, m_b_s, m_attn_out_g, m_gmlp_out_g, m_w_out, m_norm_ffn_g, m_w_ff1, m_w_ff2, m_norm_final_g, v_norm_mix_g, v_w_in, v_b_f, v_gmlp_ln_g, v_gmlp_ln_b, v_w_s, v_b_s, v_attn_out_g, v_gmlp_out_g, v_w_out, v_norm_ffn_g, v_w_ff1, v_w_ff2, v_norm_final_g):
    seq, d_model = x.shape[1], x.shape[2]
    d_attn = d_model // 2
    n_heads = d_attn // HEAD_DIM
    qkv = 3 * d_attn
    shard_cols = w_in.shape[2]
    assert N_CHIPS * shard_cols == qkv + n_heads + 2 * d_attn
    xs = x.reshape(seq, d_model)
    target = loss_target.reshape(seq, d_model)

    place = jnp.stack([2 * lax.axis_index("x") + lax.axis_index("y"), lax.axis_index("c")]).astype(jnp.int32)
    names = ["w_in", "w_out", "w_ff1", "w_ff2"]
    wt_in, mt_in, vt_in = w_in[0].T, m_w_in[0].T, v_w_in[0].T
    b_in, _ = _cast_into_slot(wt_in, place, name="cast_w_in")
    (b_out, b_ff1, b_ff2, h), (g_in,) = _casts_and_norm(
        [w_out[0], w_ff1[0], w_ff2[0]], place, xs, norm_mix_g, name="casts_and_norm_mix",
        phases=[_gather_by_parts([b_in], 0, GATHER_PARTS)])
    n_cols = N_CHIPS * shard_cols
    gate_slot, gate_row = divmod(qkv, shard_cols)
    assert gate_row + n_heads <= shard_cols
    pieces = []
    for j in range(N_CHIPS):
        if j == gate_slot:
            pieces += [(j, 0, gate_row), (j, gate_row + n_heads, shard_cols - gate_row - n_heads)]
        else:
            pieces.append((j, 0, shard_cols))
    fwd_segments, at = [[]], 0
    for order, (j, src_row, size) in enumerate(pieces):
        fwd_segments[0].append((at, at + size, order % 3, j, src_row))
        at += size
    wt_main = _realign_rows([g_in] * 3, fwd_segments, (1, n_cols - n_heads, d_model), name="w_in_rows")[0]
    wt_f = jnp.pad(jnp.concatenate(_shard_rows(g_in, qkv, qkv + n_heads), axis=0), ((0, LANES - n_heads), (0, 0)))
    b_f_pad = jnp.pad(b_f, ((0, 0), (0, LANES - n_heads)))
    b_col = b_s[0].T

    first, rest = (0, 1), (1, GATHER_PARTS)
    z, (b_out, b_ff1) = _matmul(h, wt_main, name="in_proj", out_dtype=BF16, trans_b=True, tm=2048,
                                phases=[_gather([b_out], "ici"), _gather([b_ff1], "ici", first)])
    zb, f_cum = _forget_fwd(h, wt_f, b_f_pad, name="forget_fwd")
    f_row = f_cum[:, :n_heads].T[:, None, :]
    (o, lse2), (b_ff1, b_out) = _attn_fwd(z, f_row, n_heads, name="attn_fwd",
                                          phases=[_gather([b_ff1], "ici", rest), _gather([b_out], "d2d")])
    merged = _mix_fwd(z, o, gmlp_ln_g, gmlp_ln_b, w_s[0], b_col, attn_out_g, gmlp_out_g, n_heads, name="mix_fwd")
    w_out_full = b_out.reshape(2 * d_attn, d_model)
    x1, (b_ff1, b_ff2) = _matmul(merged, w_out_full, name="out_proj", out_dtype=F32, residual=xs,
                                 phases=[_gather([b_ff1], "d2d"), _gather([b_ff2], "ici", first)])
    h2, _ = _rmsnorm_fwd(x1, norm_ffn_g, name="norm_ffn")
    a, (b_ff2,) = _matmul(h2, b_ff1, name="ff1", out_dtype=BF16, relu=True, b_sharded=True, tm=2048,
                          phases=[_merge(_gather([b_ff2], "d2d", first), _gather_by_parts([b_ff2], 1, GATHER_PARTS))])
    w_ff2_full = b_ff2.reshape(N_CHIPS * b_ff2.shape[1], d_model)
    x2, _ = _matmul(a, w_ff2_full, name="ff2", out_dtype=F32, square_lhs=True, residual=x1)
    dx2, dx2_b, dg_final, loss = _loss_and_final_bwd(x2, target, norm_final_g.reshape(1, d_model), name="loss_head")

    def pair_sum(g, r, nm):
        return _add_halves(g, r, place, name="grads_pair_sum_" + nm)

    def chip_sum(p, q, nm, **piece):
        return _sum_chips(p, q, place, name="grads_chip_sum_" + nm, **piece)

    dw_ff2, _ = _matmul(a, dx2_b, name="ff2_dw", out_dtype=BF16, trans_a=True, square_lhs=True)
    dw_ff2 = dw_ff2.reshape(N_CHIPS, -1, d_model)
    da, (r_ff2,) = _matmul(dx2_b, w_ff2_full, name="ff2_dlhs", out_dtype=BF16, trans_b=True, scale2_by=a, tm=2048,
                           phases=[_swap_halves([dw_ff2])])
    ps_ff2 = pair_sum(dw_ff2, r_ff2, "w_ff2")
    dh2, (q_ff2a,) = _matmul(da, b_ff1, name="ff1_dlhs", out_dtype=F32, trans_b=True, b_sharded=True,
                             phases=[_send_partials([ps_ff2], (0, 2))])
    dw_ff1, (q_ff2b,) = _matmul(h2, da, name="ff1_dw", out_dtype=BF16, trans_a=True, out_sharded=True, tk=seq,
                                phases=[_send_partials([ps_ff2], (1, 2))])
    g_ff2 = chip_sum(ps_ff2, q_ff2a, "w_ff2_a", piece=(0, 2))
    g_ff2 = chip_sum(ps_ff2, q_ff2b, "w_ff2_b", piece=(1, 2), into=g_ff2)
    (dx1, dg_ffn, dx1_b), (g_ff2,) = _rmsnorm_bwd(dh2, x1, dx2, norm_ffn_g, name="norm_ffn_bwd", rounded_copy=True,
                                                   phases=[_join_halves([g_ff2])])
    dw_out, _ = _matmul(merged, dx1_b, name="out_proj_dw", out_dtype=BF16, trans_a=True, tk=seq)
    dw_out = dw_out.reshape(N_CHIPS, -1, d_model)
    d_merged, (r_ff1, r_out) = _matmul(dx1_b, w_out_full, name="out_proj_dlhs", out_dtype=F32, trans_b=True,
                                       phases=[_swap_halves([dw_ff1, dw_out])])
    ps_ff1, ps_out = pair_sum(dw_ff1, r_ff1, "w_ff1"), pair_sum(dw_out, r_out, "w_out")
    d_o, dzu, dzv, dw_s, db_col, dlg, dlb, dag, dgg = _mix_bwd(
        z, o, d_merged, gmlp_ln_g, gmlp_ln_b, w_s[0], b_col, attn_out_g, gmlp_out_g, n_heads, name="mix_bwd")
    (dq, dk, dv, d_f_key, d_f_query), (q_ff1, q_out) = _attn_bwd(
        z, o, d_o, lse2, f_row, n_heads, name="attn_bwd", phases=[_send_partials([ps_ff1, ps_out])])
    g_ff1, g_out = chip_sum(ps_ff1, q_ff1, "w_ff1"), chip_sum(ps_out, q_out, "w_out")
    d_f = d_f_key.reshape(n_heads, seq) + d_f_query.reshape(n_heads, seq)
    d_f_pad = jnp.pad(d_f.T, ((0, 0), (0, LANES - n_heads)))
    dzf, db_f = _forget_bwd(d_f_pad, zb, name="forget_bwd")
    dz = jnp.concatenate([dq, dk, dv, dzu, dzv], axis=1)
    early_g = _pack([db_f[:, :n_heads], dlg, dlb, dw_s, db_col.T, dag, dgg, dg_ffn, dg_final])
    dwt_main, (g_ff1, g_out, early_all) = _matmul(dz, h, name="in_proj_dw", out_dtype=BF16, trans_a=True, tk=seq,
                                                  phases=[_join_halves([g_ff1, g_out]), _gather_small(early_g)])
    dwt_f, _ = _matmul(dzf, h, name="gate_dw", out_dtype=BF16, trans_a=True)
    bwd_segments = []
    for j in range(N_CHIPS):
        first = j * shard_cols
        if j < gate_slot:
            bwd_segments.append([(0, shard_cols, 0, 0, first)])
        elif j > gate_slot:
            bwd_segments.append([(0, shard_cols, 0, 0, first - n_heads)])
        else:
            bwd_segments.append([(0, gate_row, 0, 0, first), (gate_row, gate_row + n_heads, 1, 0, 0),
                                 (gate_row + n_heads, shard_cols, 2, 0, qkv)])
    dw_in = _realign_rows([dwt_main[None], dwt_f[None], dwt_main[None]], bwd_segments,
                          (N_CHIPS, shard_cols, d_model), name="dw_in_rows")
    dh_gate, (r_in,) = _matmul(dzf, wt_f, name="gate_dlhs", out_dtype=F32, phases=[_swap_halves([dw_in])])
    ps_in = pair_sum(dw_in, r_in, "w_in")
    dh, (q_in,) = _matmul(dz, wt_main, name="in_proj_dlhs", out_dtype=F32, residual=dh_gate, tk=2560,
                          phases=[_send_partials([ps_in])])
    g_in_sum = chip_sum(ps_in, q_in, "w_in")
    (grad_x, dg_mix), _ = _rmsnorm_bwd(dh, xs, dx1, norm_mix_g, name="norm_mix_bwd")
    late_g = _pack([dg_mix])
    g_in_sum, late_all = _exchange([_join_halves([g_in_sum]), _gather_small(late_g)], name="grads_join_w_in")

    big = {}
    for nm, g, w, m, v in zip(names, (g_in_sum, g_out, g_ff1, g_ff2), (wt_in, w_out[0], w_ff1[0], w_ff2[0]),
                              (mt_in, m_w_out[0], m_w_ff1[0], m_w_ff2[0]), (vt_in, v_w_out[0], v_w_ff1[0], v_w_ff2[0])):
        big[nm] = tuple((t.T if nm == "w_in" else t)[None] for t in _adamw(w, g, m, v, name="adamw_" + nm))

    small_params = dict(
        norm_mix_g=(norm_mix_g, m_norm_mix_g, v_norm_mix_g), b_f=(b_f, m_b_f, v_b_f),
        gmlp_ln_g=(gmlp_ln_g, m_gmlp_ln_g, v_gmlp_ln_g), gmlp_ln_b=(gmlp_ln_b, m_gmlp_ln_b, v_gmlp_ln_b),
        w_s=(w_s, m_w_s, v_w_s), b_s=(b_s, m_b_s, v_b_s), attn_out_g=(attn_out_g, m_attn_out_g, v_attn_out_g),
        gmlp_out_g=(gmlp_out_g, m_gmlp_out_g, v_gmlp_out_g), norm_ffn_g=(norm_ffn_g, m_norm_ffn_g, v_norm_ffn_g),
        norm_final_g=(norm_final_g, m_norm_final_g, v_norm_final_g))

    def small_step(group, grads_all, grads_own, label):
        w, m, v = ([small_params[nm][k] for nm in group] for k in range(3))
        packed = _adamw_small(grads_all, grads_own, place, _pack(w), _pack(m), _pack(v), name="adamw_small_" + label)
        parts = [_unpack(p, [a.shape for a in w]) for p in packed]
        return {nm: tuple(part[i] for part in parts) for i, nm in enumerate(group)}

    early = ["b_f", "gmlp_ln_g", "gmlp_ln_b", "w_s", "b_s", "attn_out_g", "gmlp_out_g", "norm_ffn_g", "norm_final_g"]
    small = {**small_step(early, early_all, early_g, "early"), **small_step(["norm_mix_g"], late_all, late_g, "late")}

    order = ["norm_mix_g", "w_in", "b_f", "gmlp_ln_g", "gmlp_ln_b", "w_s", "b_s", "attn_out_g", "gmlp_out_g", "w_out",
             "norm_ffn_g", "w_ff1", "w_ff2", "norm_final_g"]
    result = {**small, **big}
    total_loss = lax.psum(loss[0, 0], ("x", "y", "c"))
    outs = [total_loss, grad_x.reshape(x.shape)]
    for part in range(4):
        outs += [result[nm][part] for nm in order]
    return tuple(outs)
```

```python
import functools
import math

import numpy as np
import jax
import jax.numpy as jnp
from jax import lax
from jax.experimental import pallas as pl
from jax.experimental.pallas import tpu as pltpu

HEAD_DIM = 128
CHUNK = 128
EPS = 1e-6
LANES = 128
MXU_COLUMNS = 256
N_CHIPS = 4
N_DEV = 8
VMEM_LIMIT_BYTES = 56 * 1024 * 1024

ADAM_LR = 0.001
ADAM_B1 = 0.9
ADAM_B2 = 0.999
ADAM_EPS = 1e-08
ADAM_WD = 0.01
ADAM_STEP = 10

BF16 = jnp.bfloat16
F32 = jnp.float32
MESH = pl.DeviceIdType.MESH
ANY = pl.BlockSpec(memory_space=pl.ANY)
NEG_BIG = -1e30


def _params(*sem):
    return pltpu.CompilerParams(dimension_semantics=tuple(sem), vmem_limit_bytes=VMEM_LIMIT_BYTES)


def _tile(n, pref, unit):
    t = (min(pref, n) // unit) * unit
    while t >= unit:
        if n % t == 0:
            return t
        t -= unit
    return n


def _rc_tile(rows, cols, pref_rows=256, pref_cols=256):
    if rows % 16 == 0:
        return _tile(rows, pref_rows, 16), cols
    return rows, _tile(cols, pref_cols, LANES)


class _Phase:
    def __init__(self, arrays, out_shapes, in_place, n_sems, start, finish):
        self.arrays, self.out_shapes, self.in_place = list(arrays), list(out_shapes), in_place
        self.n_sems, self.start, self.finish = n_sems, start, finish

    @property
    def n_out(self):
        return len(self.arrays) if self.in_place else len(self.out_shapes)


def _run_phases(phases, steps, comm_in, comm_out, send_sems, recv_sems):
    at_in = at_out = at_sem = 0
    for ph in phases:
        for step in steps:
            getattr(ph, step)(comm_in[at_in:at_in + len(ph.arrays)], comm_out[at_out:at_out + ph.n_out],
                              lambda i, base=at_sem: send_sems.at[base + i], lambda i, base=at_sem: recv_sems.at[base + i])
        at_in, at_out, at_sem = at_in + len(ph.arrays), at_out + ph.n_out, at_sem + ph.n_sems


def _call(body, *, name, grid, in_specs, out_specs, out_shape, operands, semantics, scratch_shapes=(),
          n_prefetch=0, phases=()):
    in_specs, out_specs, out_shape = list(in_specs), list(out_specs), list(out_shape)
    scratch_shapes = list(scratch_shapes)
    n_in, n_out, n_scr = len(operands) - n_prefetch, len(out_shape), len(scratch_shapes)
    comm_in = [a for ph in phases for a in ph.arrays]
    comm_out = [jax.ShapeDtypeStruct(s.shape, s.dtype) for ph in phases
                for s in (ph.arrays if ph.in_place else ph.out_shapes)]
    aliases, at_in, at_out = {}, n_prefetch + n_in, n_out
    for ph in phases:
        if ph.in_place:
            aliases.update({at_in + r: at_out + r for r in range(len(ph.arrays))})
        at_in, at_out = at_in + len(ph.arrays), at_out + ph.n_out
    n_sems = sum(ph.n_sems for ph in phases)

    def hosted(*refs):
        pre, rest = refs[:n_prefetch], refs[n_prefetch:]
        ins, rest = rest[:n_in], rest[n_in:]
        cin, rest = rest[:len(comm_in)], rest[len(comm_in):]
        outs, rest = rest[:n_out], rest[n_out:]
        cout, rest = rest[:len(comm_out)], rest[len(comm_out):]
        scr = rest[:n_scr]
        if phases:
            send_sems, recv_sems = rest[n_scr:]
            ids = [pl.program_id(ax) for ax in range(len(grid))]
            first = functools.reduce(jnp.logical_and, [i == 0 for i in ids])
            last = functools.reduce(jnp.logical_and, [i == g - 1 for i, g in zip(ids, grid)])

            @pl.when(first)
            def _():
                _run_phases(phases, ("start",), cin, cout, send_sems, recv_sems)

        body(*pre, *ins, *outs, *scr)
        if phases:
            @pl.when(last)
            def _():
                _run_phases(phases, ("finish",), cin, cout, send_sems, recv_sems)

    all_in = in_specs + [ANY] * len(comm_in)
    all_out = out_specs + [ANY] * len(comm_out)
    all_scr = scratch_shapes + ([pltpu.SemaphoreType.DMA((n_sems,)), pltpu.SemaphoreType.DMA((n_sems,))] if phases else [])
    if phases:
        semantics = ("arbitrary",) * len(grid)
    kwargs = dict(name=name, out_shape=tuple(out_shape + comm_out), compiler_params=_params(*semantics),
                  input_output_aliases=aliases)
    if n_prefetch:
        kwargs["grid_spec"] = pltpu.PrefetchScalarGridSpec(
            num_scalar_prefetch=n_prefetch, grid=grid, in_specs=all_in, out_specs=tuple(all_out), scratch_shapes=all_scr)
    else:
        kwargs.update(grid=grid, in_specs=all_in, out_specs=tuple(all_out), scratch_shapes=all_scr)
    res = pl.pallas_call(hosted, **kwargs)(*operands, *comm_in)
    return tuple(res[:n_out]), tuple(res[n_out:])


def _only(results):
    outs, comm = results
    return outs[0] if len(outs) == 1 else outs, comm


def _matmul(a, b, *, name, out_dtype, trans_a=False, trans_b=False, tm=1024, tn=1024, tk=2048,
            square_lhs=False, relu=False, residual=None, scale2_by=None,
            b_sharded=False, out_sharded=False, phases=()):
    m, k = (a.shape[1], a.shape[0]) if trans_a else a.shape
    if b_sharded:
        if trans_b:
            n, ks = b.shape[1], b.shape[2]
            assert N_CHIPS * ks == k
        else:
            ns = b.shape[2]
            n = N_CHIPS * ns
            assert b.shape[1] == k
    else:
        n = b.shape[0] if trans_b else b.shape[1]
        assert (b.shape[1] if trans_b else b.shape[0]) == k
    tm = _tile(m, tm, 128)
    tn = _tile(n // N_CHIPS if (out_sharded or (b_sharded and not trans_b)) else n, tn, 128)
    tk = _tile(k // N_CHIPS if (b_sharded and trans_b) else k, tk, 128)
    nk = k // tk

    if trans_a:
        a_spec = pl.BlockSpec((tk, tm), lambda i, j, kk: (kk, i))
    else:
        a_spec = pl.BlockSpec((tm, tk), lambda i, j, kk: (i, kk))
    if b_sharded and trans_b:
        per = ks // tk
        assert per * tk == ks
        b_spec = pl.BlockSpec((None, tn, tk), lambda i, j, kk: (kk // per, j, kk % per))
    elif b_sharded:
        per = ns // tn
        assert per * tn == ns
        b_spec = pl.BlockSpec((None, tk, tn), lambda i, j, kk: (j // per, kk, j % per))
    elif trans_b:
        b_spec = pl.BlockSpec((tn, tk), lambda i, j, kk: (j, kk))
    else:
        b_spec = pl.BlockSpec((tk, tn), lambda i, j, kk: (kk, j))
    if out_sharded:
        ns_out = n // N_CHIPS
        per_o = ns_out // tn
        assert per_o * tn == ns_out
        out_shape = jax.ShapeDtypeStruct((N_CHIPS, m, ns_out), out_dtype)
        o_spec = pl.BlockSpec((None, tm, tn), lambda i, j, kk: (j // per_o, i, j % per_o))
    else:
        out_shape = jax.ShapeDtypeStruct((m, n), out_dtype)
        o_spec = pl.BlockSpec((tm, tn), lambda i, j, kk: (i, j))
    mn_spec = pl.BlockSpec((tm, tn), lambda i, j, kk: (i, j))

    operands, in_specs = [a, b], [a_spec, b_spec]
    if scale2_by is not None:
        operands.append(scale2_by)
        in_specs.append(mn_spec)
    if residual is not None:
        operands.append(residual)
        in_specs.append(mn_spec)
    dims = (((0 if trans_a else 1,), (1 if trans_b else 0,)), ((), ()))
    chunk = MXU_COLUMNS if tn % MXU_COLUMNS == 0 else tn

    def body(*refs):
        a_ref, b_ref = refs[0], refs[1]
        pos = 2
        scale_ref = res_ref = None
        if scale2_by is not None:
            scale_ref = refs[pos]
            pos += 1
        if residual is not None:
            res_ref = refs[pos]
            pos += 1
        o_ref = refs[pos]
        kk = pl.program_id(2)

        av = a_ref[...]
        if square_lhs:
            av = (av * av) if av.dtype == BF16 else (av.astype(F32) * av.astype(F32))
        av = av.astype(BF16)

        def finish(r, cols):
            if relu:
                r = jnp.maximum(r, 0.0)
            if scale_ref is not None:
                r = r * (2.0 * scale_ref[:, cols].astype(F32))
            if res_ref is not None:
                r = r + res_ref[:, cols].astype(F32)
            o_ref[:, cols] = r.astype(out_dtype)

        if nk == 1:
            for lo in range(0, tn, chunk):
                cols = slice(lo, lo + chunk)
                bv = (b_ref[cols, :] if trans_b else b_ref[:, cols]).astype(BF16)
                finish(lax.dot_general(av, bv, dims, preferred_element_type=F32), cols)
        else:
            acc_ref = refs[pos + 1]
            part = lax.dot_general(av, b_ref[...].astype(BF16), dims, preferred_element_type=F32)

            @pl.when(kk == 0)
            def _():
                acc_ref[...] = part

            @pl.when(jnp.logical_and(kk > 0, kk < nk - 1))
            def _():
                acc_ref[...] += part

            @pl.when(kk == nk - 1)
            def _():
                finish(acc_ref[...] + part, slice(None))

    return _only(_call(
        body, name=name, out_shape=[out_shape], grid=(m // tm, n // tn, nk),
        in_specs=in_specs, out_specs=[o_spec], operands=operands,
        scratch_shapes=[pltpu.VMEM((tm, tn), F32)] if nk > 1 else [],
        semantics=("parallel", "parallel", "arbitrary"), phases=phases))


def _rmsnorm_fwd(x, g, *, name, tr=512, phases=()):
    s, d = x.shape
    tr = _tile(s, tr, 8)

    def body(x_ref, g_ref, o_ref):
        xv = x_ref[...]
        r = lax.rsqrt(jnp.mean(xv * xv, axis=-1, keepdims=True) + EPS)
        o_ref[...] = ((xv * r) * g_ref[...]).astype(BF16)

    return _only(_call(
        body, name=name, out_shape=[jax.ShapeDtypeStruct((s, d), BF16)], grid=(s // tr,),
        in_specs=[pl.BlockSpec((tr, d), lambda i: (i, 0)), pl.BlockSpec((1, d), lambda i: (0, 0))],
        out_specs=[pl.BlockSpec((tr, d), lambda i: (i, 0))], operands=[x, g],
        semantics=("parallel",), phases=phases))


def _rms_bwd_rows(dy, xv, g):
    d = xv.shape[-1]
    r = lax.rsqrt(jnp.mean(xv * xv, axis=-1, keepdims=True) + EPS)
    gdy = dy * g
    dot = jnp.sum(gdy * xv, axis=-1, keepdims=True)
    dx = gdy * r - xv * (r * r * r) * (dot / d)
    return dx, dy * (xv * r)


def _rmsnorm_bwd(dy, x, res, g, *, name, tr=256, rounded_copy=False, phases=()):
    s, d = x.shape
    tr = _tile(s, tr, 8)

    def body(dy_ref, x_ref, res_ref, g_ref, dx_ref, dg_ref, *dxb_ref):
        @pl.when(pl.program_id(0) == 0)
        def _():
            dg_ref[...] = jnp.zeros_like(dg_ref)

        dx, dg_rows = _rms_bwd_rows(dy_ref[...].astype(F32), x_ref[...], g_ref[...])
        out = res_ref[...] + dx
        dx_ref[...] = out
        if rounded_copy:
            dxb_ref[0][...] = out.astype(BF16)
        dg_ref[...] += jnp.sum(dg_rows, axis=0, keepdims=True)

    row = pl.BlockSpec((tr, d), lambda i: (i, 0))
    vec = pl.BlockSpec((1, d), lambda i: (0, 0))
    extra = [jax.ShapeDtypeStruct((s, d), BF16)] if rounded_copy else []
    return _call(
        body, name=name,
        out_shape=[jax.ShapeDtypeStruct((s, d), F32), jax.ShapeDtypeStruct((1, d), F32)] + extra,
        grid=(s // tr,), in_specs=[row, row, row, vec], out_specs=[row, vec] + [row] * len(extra),
        operands=[dy, x, res, g], semantics=("arbitrary",), phases=phases)


def _loss_and_final_bwd(x2, target, g, *, name, tr=256):
    s, d = x2.shape
    tr = _tile(s, tr, 8)

    def body(x_ref, t_ref, g_ref, dx_ref, dxb_ref, dg_ref, loss_ref):
        @pl.when(pl.program_id(0) == 0)
        def _():
            dg_ref[...] = jnp.zeros_like(dg_ref)
            loss_ref[...] = jnp.zeros_like(loss_ref)

        xv, gv = x_ref[...], g_ref[...]
        r = lax.rsqrt(jnp.mean(xv * xv, axis=-1, keepdims=True) + EPS)
        err = (xv * r) * gv - t_ref[...]
        row_loss = jnp.mean(err * err, axis=-1, keepdims=True)
        loss_ref[...] += 0.5 * jnp.sum(row_loss, axis=0, keepdims=True)
        dx, dg_rows = _rms_bwd_rows(err / d, xv, gv)
        dx_ref[...] = dx
        dxb_ref[...] = dx.astype(BF16)
        dg_ref[...] += jnp.sum(dg_rows, axis=0, keepdims=True)

    row = pl.BlockSpec((tr, d), lambda i: (i, 0))
    vec = pl.BlockSpec((1, d), lambda i: (0, 0))
    one = pl.BlockSpec((1, 1), lambda i: (0, 0))
    return pl.pallas_call(
        body, name=name,
        out_shape=(jax.ShapeDtypeStruct((s, d), F32), jax.ShapeDtypeStruct((s, d), BF16),
                   jax.ShapeDtypeStruct((1, d), F32), jax.ShapeDtypeStruct((1, 1), F32)),
        grid=(s // tr,), in_specs=[row, row, vec], out_specs=(row, row, vec, one),
        compiler_params=_params("arbitrary"),
    )(x2, target, g)


def _tri_ones(n, lower):
    r = lax.broadcasted_iota(jnp.int32, (n, n), 0)
    c = lax.broadcasted_iota(jnp.int32, (n, n), 1)
    return jnp.where((c <= r) if lower else (c >= r), 1.0, 0.0).astype(F32)


def _forget_fwd(h, w_f, b_f, *, name, tr=256):
    s, d = h.shape
    tr = _tile(s, tr, 8)

    def body(h_ref, w_ref, b_ref, zb_ref, f_ref, carry):
        @pl.when(pl.program_id(0) == 0)
        def _():
            carry[...] = jnp.zeros_like(carry)

        zb = lax.dot_general(h_ref[...], w_ref[...], (((1,), (1,)), ((), ())), preferred_element_type=F32) + b_ref[...]
        zb_ref[...] = zb
        log_f = jnp.minimum(zb, 0.0) - jnp.log(1.0 + jnp.exp(-jnp.abs(zb)))
        run = jnp.dot(_tri_ones(tr, True), log_f, preferred_element_type=F32,
                      precision=lax.Precision.HIGHEST) + carry[...]
        f_ref[...] = run
        carry[...] = run[tr - 1:tr, :]

    row = pl.BlockSpec((tr, LANES), lambda i: (i, 0))
    return pl.pallas_call(
        body, name=name,
        out_shape=(jax.ShapeDtypeStruct((s, LANES), F32), jax.ShapeDtypeStruct((s, LANES), F32)),
        grid=(s // tr,),
        in_specs=[pl.BlockSpec((tr, d), lambda i: (i, 0)), pl.BlockSpec((LANES, d), lambda i: (0, 0)),
                  pl.BlockSpec((1, LANES), lambda i: (0, 0))],
        out_specs=(row, row), scratch_shapes=[pltpu.VMEM((1, LANES), F32)],
        compiler_params=_params("arbitrary"),
    )(h, w_f, b_f)


def _forget_bwd(d_f, zb, *, name, tr=256):
    s = zb.shape[0]
    tr = _tile(s, tr, 8)
    nb = s // tr

    def body(df_ref, zb_ref, dz_ref, db_ref, carry):
        @pl.when(pl.program_id(0) == 0)
        def _():
            carry[...] = jnp.zeros_like(carry)
            db_ref[...] = jnp.zeros_like(db_ref)

        run = jnp.dot(_tri_ones(tr, False), df_ref[...], preferred_element_type=F32,
                      precision=lax.Precision.HIGHEST) + carry[...]
        carry[...] = run[0:1, :]
        dz = run / (1.0 + jnp.exp(zb_ref[...]))
        dz_ref[...] = dz.astype(BF16)
        db_ref[...] += jnp.sum(dz, axis=0, keepdims=True)

    row = pl.BlockSpec((tr, LANES), lambda i: (nb - 1 - i, 0))
    return pl.pallas_call(
        body, name=name,
        out_shape=(jax.ShapeDtypeStruct((s, LANES), BF16), jax.ShapeDtypeStruct((1, LANES), F32)),
        grid=(nb,), in_specs=[row, row], out_specs=(row, pl.BlockSpec((1, LANES), lambda i: (0, 0))),
        scratch_shapes=[pltpu.VMEM((1, LANES), F32)],
        compiler_params=_params("arbitrary"),
    )(d_f, zb)


def _pairs(nblk, by_kv):
    if by_kv:
        pr = [(i, j) for j in range(nblk) for i in range(j, nblk)]
    else:
        pr = [(i, j) for i in range(nblk) for j in range(i + 1)]
    return (jnp.asarray(np.array([p[0] for p in pr], np.int32)), jnp.asarray(np.array([p[1] for p in pr], np.int32)))


def _causal_mask(rows, keys):
    r = lax.broadcasted_iota(jnp.int32, (rows[1] - rows[0], keys[1] - keys[0]), 0) + rows[0]
    c = lax.broadcasted_iota(jnp.int32, (rows[1] - rows[0], keys[1] - keys[0]), 1) + keys[0]
    return c <= r


def _diagonal_pieces(tb):
    half = tb // 2
    if half % LANES:
        return [((0, tb), (0, tb))]
    return [((0, half), (0, half)), ((half, tb), (0, tb))]


LOG2E = math.log2(math.e)
QK_TO_LOG2 = LOG2E / math.sqrt(HEAD_DIM)


def _attn_logits2(q, k, fk_row):
    sc = lax.dot_general(q, k, (((1,), (1,)), ((), ())), preferred_element_type=F32)
    return sc * QK_TO_LOG2 - fk_row * LOG2E


def _attn_fwd(z, f_row, n_heads, *, name, tb=1024, phases=()):
    s = z.shape[0]
    tb = _tile(s, tb, 128)
    nblk = s // tb
    qi, kj = _pairs(nblk, by_kv=False)

    def body(qi_ref, kj_ref, q_ref, k_ref, v_ref, fk_ref, o_ref, lse_ref, m_sc, l_sc, acc_sc):
        p = pl.program_id(1)
        i, j = qi_ref[p], kj_ref[p]

        @pl.when(j == 0)
        def _():
            m_sc[...] = jnp.full_like(m_sc, NEG_BIG)
            l_sc[...] = jnp.zeros_like(l_sc)
            acc_sc[...] = jnp.zeros_like(acc_sc)

        def update(rows, keys, masked):
            rs, ks = slice(*rows), slice(*keys)
            s2 = _attn_logits2(q_ref[rs, :], k_ref[ks, :], fk_ref[:, ks])
            if masked:
                s2 = jnp.where(_causal_mask(rows, keys), s2, NEG_BIG)
            m_old = m_sc[rs, :]
            m_new = jnp.maximum(m_old, jnp.max(s2, axis=-1, keepdims=True))
            alpha = jnp.exp2(m_old - m_new)
            pv = jnp.exp2(s2 - jnp.tile(m_new, (1, (keys[1] - keys[0]) // LANES)))
            l_sc[rs, :] = alpha * l_sc[rs, :] + jnp.sum(pv, axis=-1, keepdims=True)
            acc_sc[rs, :] = alpha * acc_sc[rs, :] + jnp.dot(pv.astype(BF16), v_ref[ks, :], preferred_element_type=F32)
            m_sc[rs, :] = m_new

        @pl.when(j < i)
        def _():
            update((0, tb), (0, tb), False)

        @pl.when(j == i)
        def _():
            for rows, keys in _diagonal_pieces(tb):
                update(rows, keys, True)
            o_ref[...] = (acc_sc[...] / l_sc[...]).astype(BF16)
            lse_ref[...] = m_sc[...] + jnp.log2(l_sc[...])

    h = n_heads
    return _call(
        body, name=name, n_prefetch=2, grid=(h, int(qi.shape[0])),
        in_specs=[
            pl.BlockSpec((tb, HEAD_DIM), lambda hh, p, qi_r, kj_r: (qi_r[p], hh)),
            pl.BlockSpec((tb, HEAD_DIM), lambda hh, p, qi_r, kj_r: (kj_r[p], h + hh)),
            pl.BlockSpec((tb, HEAD_DIM), lambda hh, p, qi_r, kj_r: (kj_r[p], 2 * h + hh)),
            pl.BlockSpec((None, 1, tb), lambda hh, p, qi_r, kj_r: (hh, 0, kj_r[p])),
        ],
        out_specs=[
            pl.BlockSpec((tb, HEAD_DIM), lambda hh, p, qi_r, kj_r: (qi_r[p], hh)),
            pl.BlockSpec((None, tb, LANES), lambda hh, p, qi_r, kj_r: (hh, qi_r[p], 0)),
        ],
        scratch_shapes=[pltpu.VMEM((tb, LANES), F32), pltpu.VMEM((tb, LANES), F32), pltpu.VMEM((tb, HEAD_DIM), F32)],
        out_shape=[jax.ShapeDtypeStruct((s, h * HEAD_DIM), BF16), jax.ShapeDtypeStruct((h, s, LANES), F32)],
        operands=[qi, kj, z, z, z, f_row], semantics=("parallel", "arbitrary"), phases=phases)


def _attn_bwd(z, o, d_o, lse2, f_row, n_heads, *, name, tb=1024, phases=()):
    s = z.shape[0]
    tb = _tile(s, tb, 128)
    nblk = s // tb
    qi, kj = _pairs(nblk, by_kv=True)
    n_pairs = int(qi.shape[0])
    scale = 1.0 / math.sqrt(HEAD_DIM)
    h = n_heads

    def body(qi_ref, kj_ref, q_ref, k_ref, v_ref, o_ref, do_ref, lse_ref, fk_ref,
             dq_ref, dk_ref, dv_ref, df_ref, dfq_ref, dq_sc, dk_sc, dv_sc, df_sc, dfq_sc):
        p = pl.program_id(1)
        i, j = qi_ref[p], kj_ref[p]

        @pl.when(p == 0)
        def _():
            dq_sc[...] = jnp.zeros_like(dq_sc)
            dfq_sc[...] = jnp.zeros_like(dfq_sc)

        @pl.when(i == j)
        def _():
            dk_sc[...] = jnp.zeros_like(dk_sc)
            dv_sc[...] = jnp.zeros_like(dv_sc)
            df_sc[...] = jnp.zeros_like(df_sc)

        def update(rows, keys, masked):
            rs, ks, n_rows = slice(*rows), slice(*keys), rows[1] - rows[0]
            q, k, v, do = q_ref[rs, :], k_ref[ks, :], v_ref[ks, :], do_ref[rs, :]
            delta = jnp.sum(do.astype(F32) * o_ref[rs, :].astype(F32), axis=-1, keepdims=True)
            pv = jnp.exp2(_attn_logits2(q, k, fk_ref[:, ks]) - jnp.tile(lse_ref[rs, :], (1, (keys[1] - keys[0]) // LANES)))
            if masked:
                pv = jnp.where(_causal_mask(rows, keys), pv, 0.0)
            dp = lax.dot_general(do, v, (((1,), (1,)), ((), ())), preferred_element_type=F32)
            ds = pv * (dp - delta)
            ds_b = ds.astype(BF16)
            dv_sc[ks, :] += lax.dot_general(pv.astype(BF16), do, (((0,), (0,)), ((), ())), preferred_element_type=F32)
            dk_sc[ks, :] += lax.dot_general(ds_b, q, (((0,), (0,)), ((), ())), preferred_element_type=F32)
            at = pl.ds(pl.multiple_of(i * tb + rows[0], LANES), n_rows)
            dq_sc[at, :] += jnp.dot(ds_b, k, preferred_element_type=F32)
            df_sc[:, ks] -= jnp.sum(ds, axis=0, keepdims=True)
            dfq_sc[at, :] += jnp.broadcast_to(jnp.sum(ds, axis=1, keepdims=True), (n_rows, LANES))

        @pl.when(i > j)
        def _():
            update((0, tb), (0, tb), False)

        @pl.when(i == j)
        def _():
            for rows, keys in _diagonal_pieces(tb):
                update(rows, keys, True)

        @pl.when(i == nblk - 1)
        def _():
            dk_ref[...] = (dk_sc[...] * scale).astype(BF16)
            dv_ref[...] = dv_sc[...].astype(BF16)
            df_ref[...] = df_sc[...]

        @pl.when(p == n_pairs - 1)
        def _():
            dq_ref[...] = (dq_sc[...] * scale).astype(BF16)
            dfq_ref[...] = jnp.transpose(dfq_sc[...])[0:1, :]

    qblk = lambda off: pl.BlockSpec((tb, HEAD_DIM), lambda hh, p, qi_r, kj_r: (qi_r[p], off + hh))
    kblk = lambda off: pl.BlockSpec((tb, HEAD_DIM), lambda hh, p, qi_r, kj_r: (kj_r[p], off + hh))
    qrep = pl.BlockSpec((None, tb, LANES), lambda hh, p, qi_r, kj_r: (hh, qi_r[p], 0))
    krow = pl.BlockSpec((None, 1, tb), lambda hh, p, qi_r, kj_r: (hh, 0, kj_r[p]))
    act = jax.ShapeDtypeStruct((s, h * HEAD_DIM), BF16)
    return _call(
        body, name=name, n_prefetch=2, grid=(h, n_pairs),
        in_specs=[qblk(0), kblk(h), kblk(2 * h), qblk(0), qblk(0), qrep, krow],
        out_specs=[
            pl.BlockSpec((s, HEAD_DIM), lambda hh, p, qi_r, kj_r: (0, hh)),
            kblk(0), kblk(0), krow,
            pl.BlockSpec((None, 1, s), lambda hh, p, qi_r, kj_r: (hh, 0, 0)),
        ],
        scratch_shapes=[pltpu.VMEM((s, HEAD_DIM), F32), pltpu.VMEM((tb, HEAD_DIM), F32),
                        pltpu.VMEM((tb, HEAD_DIM), F32), pltpu.VMEM((1, tb), F32), pltpu.VMEM((s, LANES), F32)],
        out_shape=[act, act, act, jax.ShapeDtypeStruct((h, 1, s), F32), jax.ShapeDtypeStruct((h, 1, s), F32)],
        operands=[qi, kj, z, z, z, o, d_o, lse2, f_row], semantics=("parallel", "arbitrary"), phases=phases)


GELU_C = math.sqrt(2.0 / math.pi)
GELU_A = 0.044715


def _gelu(x):
    return 0.5 * x * (1.0 + jnp.tanh(GELU_C * (x + GELU_A * (x * x * x))))


def _gelu_and_grad(x):
    t = jnp.tanh(GELU_C * (x + GELU_A * (x * x * x)))
    y = 0.5 * x * (1.0 + t)
    dy = 0.5 * (1.0 + t) + 0.5 * x * (1.0 - t * t) * (GELU_C * (1.0 + 3.0 * GELU_A * (x * x)))
    return y, dy


def _layernorm_parts(g):
    mu = jnp.mean(g, axis=-1, keepdims=True)
    xc = g - mu
    rs = lax.rsqrt(jnp.mean(xc * xc, axis=-1, keepdims=True) + EPS)
    return xc * rs, rs


def _spatial_mix(w_ref, bcol_ref, vv_b, n_heads, n_chunks):
    tril = _causal_mask((0, CHUNK), (0, CHUNK))
    cols = []
    for hh in range(n_heads):
        wc = jnp.where(tril, w_ref[hh], 0.0).astype(BF16)
        lanes = slice(hh * HEAD_DIM, (hh + 1) * HEAD_DIM)
        rows = [jnp.dot(wc, vv_b[c * CHUNK:(c + 1) * CHUNK, lanes], preferred_element_type=F32)
                + bcol_ref[:, hh:hh + 1] for c in range(n_chunks)]
        cols.append(jnp.concatenate(rows, axis=0))
    return jnp.concatenate(cols, axis=1)


def _mix_fwd(z, o, ln_g, ln_b, w_s, b_col, attn_g, gm_g, n_heads, *, name, tr=256):
    s = z.shape[0]
    dg = n_heads * HEAD_DIM
    tr = _tile(s, tr, CHUNK)
    n_chunks = tr // CHUNK

    def body(zu_ref, zv_ref, o_ref, lg_ref, lb_ref, w_ref, bcol_ref, ag_ref, gg_ref, out_ref):
        u = _gelu(zu_ref[...].astype(F32))
        xhat, _ = _layernorm_parts(_gelu(zv_ref[...].astype(F32)))
        vv = xhat * lg_ref[...] + lb_ref[...]
        gm = u * _spatial_mix(w_ref, bcol_ref, vv.astype(BF16), n_heads, n_chunks)
        rg = lax.rsqrt(jnp.mean(gm * gm, axis=-1, keepdims=True) + EPS)
        ov = o_ref[...].astype(F32)
        ra = lax.rsqrt(jnp.mean(ov * ov, axis=-1, keepdims=True) + EPS)
        out_ref[:, :dg] = ((ov * ra) * ag_ref[...]).astype(BF16)
        out_ref[:, dg:] = ((gm * rg) * gg_ref[...]).astype(BF16)

    vec = pl.BlockSpec((1, dg), lambda i: (0, 0))
    return pl.pallas_call(
        body, name=name, out_shape=jax.ShapeDtypeStruct((s, 2 * dg), BF16), grid=(s // tr,),
        in_specs=[pl.BlockSpec((tr, dg), lambda i: (i, 3)), pl.BlockSpec((tr, dg), lambda i: (i, 4)),
                  pl.BlockSpec((tr, dg), lambda i: (i, 0)), vec, vec,
                  pl.BlockSpec((n_heads, CHUNK, CHUNK), lambda i: (0, 0, 0)),
                  pl.BlockSpec((CHUNK, n_heads), lambda i: (0, 0)), vec, vec],
        out_specs=pl.BlockSpec((tr, 2 * dg), lambda i: (i, 0)),
        compiler_params=_params("parallel"),
    )(z, z, o, ln_g, ln_b, w_s, b_col, attn_g, gm_g)


def _mix_bwd(z, o, d_merged, ln_g, ln_b, w_s, b_col, attn_g, gm_g, n_heads, *, name, tr=256):
    s = z.shape[0]
    dg = n_heads * HEAD_DIM
    tr = _tile(s, tr, CHUNK)
    n_chunks = tr // CHUNK

    def body(zu_ref, zv_ref, o_ref, dm_ref, lg_ref, lb_ref, w_ref, bcol_ref, ag_ref, gg_ref,
             do_ref, dzu_ref, dzv_ref, dw_ref, dbcol_ref, dlg_ref, dlb_ref, dag_ref, dgg_ref):
        @pl.when(pl.program_id(0) == 0)
        def _():
            for ref in (dw_ref, dbcol_ref, dlg_ref, dlb_ref, dag_ref, dgg_ref):
                ref[...] = jnp.zeros_like(ref)

        d_o, dag_rows = _rms_bwd_rows(dm_ref[:, :dg], o_ref[...].astype(F32), ag_ref[...])
        do_ref[...] = d_o.astype(BF16)
        dag_ref[...] += jnp.sum(dag_rows, axis=0, keepdims=True)

        u, du_dz = _gelu_and_grad(zu_ref[...].astype(F32))
        gv, dgv_dz = _gelu_and_grad(zv_ref[...].astype(F32))
        xhat, rs = _layernorm_parts(gv)
        lg = lg_ref[...]
        vv_b = (xhat * lg + lb_ref[...]).astype(BF16)
        mix = _spatial_mix(w_ref, bcol_ref, vv_b, n_heads, n_chunks)
        gm = u * mix
        d_gm, dgg_rows = _rms_bwd_rows(dm_ref[:, dg:], gm, gg_ref[...])
        dgg_ref[...] += jnp.sum(dgg_rows, axis=0, keepdims=True)
        dzu_ref[...] = ((d_gm * mix) * du_dz).astype(BF16)
        d_mix = d_gm * u
        d_mix_b = d_mix.astype(BF16)

        tril = _causal_mask((0, CHUNK), (0, CHUNK))
        lane = lax.broadcasted_iota(jnp.int32, (CHUNK, n_heads), 1)
        cols = []
        db = jnp.zeros((CHUNK, n_heads), F32)
        for hh in range(n_heads):
            wc = jnp.where(tril, w_ref[hh], 0.0).astype(BF16)
            lanes = slice(hh * HEAD_DIM, (hh + 1) * HEAD_DIM)
            dw = jnp.zeros((CHUNK, CHUNK), F32)
            dmix_sum = jnp.zeros((CHUNK, HEAD_DIM), F32)
            rows = []
            for c in range(n_chunks):
                rws = slice(c * CHUNK, (c + 1) * CHUNK)
                dmb = d_mix_b[rws, lanes]
                dw += lax.dot_general(dmb, vv_b[rws, lanes], (((1,), (1,)), ((), ())), preferred_element_type=F32)
                rows.append(lax.dot_general(wc, dmb, (((0,), (0,)), ((), ())), preferred_element_type=F32))
                dmix_sum += d_mix[rws, lanes]
            dw_ref[hh] += jnp.where(tril, dw, 0.0)
            db += jnp.where(lane == hh, jnp.sum(dmix_sum, axis=-1, keepdims=True), 0.0)
            cols.append(jnp.concatenate(rows, axis=0))
        dbcol_ref[...] += db
        d_vv = jnp.concatenate(cols, axis=1)

        dlg_ref[...] += jnp.sum(d_vv * xhat, axis=0, keepdims=True)
        dlb_ref[...] += jnp.sum(d_vv, axis=0, keepdims=True)
        d_xhat = d_vv * lg
        d_gv = rs * (d_xhat - jnp.mean(d_xhat, axis=-1, keepdims=True)
                     - xhat * jnp.mean(d_xhat * xhat, axis=-1, keepdims=True))
        dzv_ref[...] = (d_gv * dgv_dz).astype(BF16)

    vec = pl.BlockSpec((1, dg), lambda i: (0, 0))
    wspec = pl.BlockSpec((n_heads, CHUNK, CHUNK), lambda i: (0, 0, 0))
    bspec = pl.BlockSpec((CHUNK, n_heads), lambda i: (0, 0))
    rowb = pl.BlockSpec((tr, dg), lambda i: (i, 0))
    act = jax.ShapeDtypeStruct((s, dg), BF16)
    vshape = jax.ShapeDtypeStruct((1, dg), F32)
    return pl.pallas_call(
        body, name=name,
        out_shape=(act, act, act, jax.ShapeDtypeStruct((n_heads, CHUNK, CHUNK), F32),
                   jax.ShapeDtypeStruct((CHUNK, n_heads), F32), vshape, vshape, vshape, vshape),
        grid=(s // tr,),
        in_specs=[pl.BlockSpec((tr, dg), lambda i: (i, 3)), pl.BlockSpec((tr, dg), lambda i: (i, 4)),
                  rowb, pl.BlockSpec((tr, 2 * dg), lambda i: (i, 0)), vec, vec, wspec, bspec, vec, vec],
        out_specs=(rowb, rowb, rowb, wspec, bspec, vec, vec, vec, vec),
        compiler_params=_params("arbitrary"),
    )(z, z, o, d_merged, ln_g, ln_b, w_s, b_col, attn_g, gm_g)


def _place():
    x, y, c = lax.axis_index("x"), lax.axis_index("y"), lax.axis_index("c")
    other_chips = [(1 - x, y), (x, 1 - y), (1 - x, 1 - y)]
    return x, y, c, other_chips


def _remote(src, dst, send_sem, recv_sem, to):
    return pltpu.make_async_remote_copy(src_ref=src, dst_ref=dst, send_sem=send_sem, recv_sem=recv_sem,
                                        device_id=to, device_id_type=MESH)


def _cast_into_slot(w, place, *, name, phases=()):
    rows, cols = w.shape
    tr, tc = _rc_tile(rows, cols)

    def body(place_ref, w_ref, o_ref):
        o_ref[...] = w_ref[...].astype(BF16)

    return _only(_call(
        body, name=name, n_prefetch=1, grid=(rows // tr, cols // tc),
        in_specs=[pl.BlockSpec((tr, tc), lambda i, j, pr: (i, j))],
        out_specs=[pl.BlockSpec((None, tr, tc), lambda i, j, pr: (pr[0], i, j))],
        out_shape=[jax.ShapeDtypeStruct((N_CHIPS, rows, cols), BF16)], operands=[place, w],
        semantics=("parallel", "parallel"), phases=phases))


def _casts_and_norm(weights, place, x, g, *, name, rows=256, phases=()):
    cols = x.shape[1]
    jobs = [w.shape[0] // rows for w in weights] + [x.shape[0] // rows]
    assert all(w.shape[1] == cols and w.shape[0] % rows == 0 for w in weights) and x.shape[0] % rows == 0
    first = [sum(jobs[:k]) for k in range(len(jobs))]

    def strip(k):
        return lambda t: jnp.clip(t - first[k], 0, jobs[k] - 1)

    def body(place_ref, *refs):
        n = len(weights)
        w_refs, x_ref, g_ref, outs = refs[:n], refs[n], refs[n + 1], refs[n + 2:]
        t = pl.program_id(0)
        for k in range(n):
            @pl.when(jnp.logical_and(t >= first[k], t < first[k] + jobs[k]))
            def _(k=k):
                outs[k][...] = w_refs[k][...].astype(BF16)

        @pl.when(t >= first[n])
        def _():
            xv = x_ref[...]
            r = lax.rsqrt(jnp.mean(xv * xv, axis=-1, keepdims=True) + EPS)
            outs[n][...] = ((xv * r) * g_ref[...]).astype(BF16)

    in_specs = [pl.BlockSpec((rows, cols), lambda t, pr, k=k: (strip(k)(t), 0)) for k in range(len(weights))]
    in_specs += [pl.BlockSpec((rows, cols), lambda t, pr: (strip(len(weights))(t), 0)),
                 pl.BlockSpec((1, cols), lambda t, pr: (0, 0))]
    out_specs = [pl.BlockSpec((None, rows, cols), lambda t, pr, k=k: (pr[0], strip(k)(t), 0)) for k in range(len(weights))]
    out_specs.append(pl.BlockSpec((rows, cols), lambda t, pr: (strip(len(weights))(t), 0)))
    out_shape = [jax.ShapeDtypeStruct((N_CHIPS,) + w.shape, BF16) for w in weights] + [jax.ShapeDtypeStruct(x.shape, BF16)]
    return _call(body, name=name, n_prefetch=1, grid=(sum(jobs),), in_specs=in_specs, out_specs=out_specs,
                 out_shape=out_shape, operands=[place, *weights, x, g], semantics=("arbitrary",), phases=phases)


def _exchange(phases, *, name):
    comm_in = [a for ph in phases for a in ph.arrays]
    comm_out = [jax.ShapeDtypeStruct(s.shape, s.dtype) for ph in phases for s in (ph.arrays if ph.in_place else ph.out_shapes)]
    aliases, at_in, at_out = {}, 0, 0
    for ph in phases:
        if ph.in_place:
            aliases.update({at_in + r: at_out + r for r in range(len(ph.arrays))})
        at_in, at_out = at_in + len(ph.arrays), at_out + ph.n_out
    n_sems = sum(ph.n_sems for ph in phases)

    def body(*refs):
        cin, cout = refs[:len(comm_in)], refs[len(comm_in):len(comm_in) + len(comm_out)]
        send_sems, recv_sems = refs[len(comm_in) + len(comm_out):]
        _run_phases(phases, ("start", "finish"), cin, cout, send_sems, recv_sems)

    return pl.pallas_call(
        body, name=name, out_shape=tuple(comm_out), in_specs=[ANY] * len(comm_in), out_specs=tuple([ANY] * len(comm_out)),
        input_output_aliases=aliases,
        scratch_shapes=[pltpu.SemaphoreType.DMA((n_sems,)), pltpu.SemaphoreType.DMA((n_sems,))],
    )(*comm_in)


GATHER_PARTS = 4


def _gather(bufs, stage, part=(0, GATHER_PARTS)):
    n = 3 * len(bufs)
    lo, hi = part

    def copies(outs, send, recv, d2d, incoming):
        x, y, c, chips = _place()
        for t, buf in enumerate(outs):
            half = buf.shape[2] // 2
            piece = half // GATHER_PARTS
            for k, (cx, cy) in enumerate(chips):
                i = 3 * t + k + (n if (d2d and stage == "both") else 0)
                cols = pl.ds(((1 - c) if (d2d and incoming) else c) * half + lo * piece, (hi - lo) * piece)
                blk = buf.at[(2 * cx + cy) if (d2d or incoming) else (2 * x + y), :, cols]
                yield _remote(blk, blk, send(i), recv(i), (x, y, 1 - c) if d2d else (cx, cy, c))

    def start(ins, outs, send, recv):
        for cp in copies(outs, send, recv, stage == "d2d", False):
            cp.start()

    def finish(ins, outs, send, recv):
        if stage == "both":
            for arrival, onward in zip(copies(outs, send, recv, False, True), copies(outs, send, recv, True, False)):
                arrival.wait_recv()
                onward.start()
        for cp in copies(outs, send, recv, stage != "ici", True):
            cp.wait_recv()
        for d2d in ((False, True) if stage == "both" else (stage == "d2d",)):
            for cp in copies(outs, send, recv, d2d, False):
                cp.wait_send()

    return _Phase(bufs, [], True, (2 if stage == "both" else 1) * n, start, finish)


def _merge(first, second):
    n_first = first.n_sems

    def later(sem):
        return lambda i: sem(n_first + i)

    def start(ins, outs, send, recv):
        first.start(ins, outs, send, recv)
        second.start(ins, outs, later(send), later(recv))

    def finish(ins, outs, send, recv):
        first.finish(ins, outs, send, recv)
        second.finish(ins, outs, later(send), later(recv))

    return _Phase(first.arrays, [], True, n_first + second.n_sems, start, finish)


def _gather_by_parts(bufs, lo, hi):
    phase = _gather(bufs, "both", (lo, lo + 1))
    for part in range(lo + 1, hi):
        phase = _merge(phase, _gather(bufs, "both", (part, part + 1)))
    return phase


def _swap_halves(grads):
    def copies(ins, outs, send, recv):
        x, y, c, _ = _place()
        for t, g in enumerate(ins):
            half = g.shape[2] // 2
            yield _remote(g.at[:, :, pl.ds((1 - c) * half, half)], outs[t], send(t), recv(t), (x, y, 1 - c))

    def start(ins, outs, send, recv):
        for cp in copies(ins, outs, send, recv):
            cp.start()

    def finish(ins, outs, send, recv):
        for cp in copies(ins, outs, send, recv):
            cp.wait()

    shapes = [jax.ShapeDtypeStruct((a.shape[0], a.shape[1], a.shape[2] // 2), a.dtype) for a in grads]
    return _Phase(grads, shapes, False, len(grads), start, finish)


def _add_halves(grad, received, place, *, name):
    ns, rows, half = received.shape
    tr, tc = _rc_tile(rows, half, pref_rows=1024)
    per = half // tc

    def body(place_ref, g_ref, r_ref, o_ref):
        o_ref[...] = (g_ref[...].astype(F32) + r_ref[...].astype(F32)).astype(BF16)

    grid_spec = pltpu.PrefetchScalarGridSpec(
        num_scalar_prefetch=1, grid=(ns, rows // tr, per),
        in_specs=[pl.BlockSpec((None, tr, tc), lambda s, i, j, pr: (s, i, pr[1] * per + j)),
                  pl.BlockSpec((None, tr, tc), lambda s, i, j, pr: (s, i, j))],
        out_specs=pl.BlockSpec((None, tr, tc), lambda s, i, j, pr: (s, i, j)),
    )
    return pl.pallas_call(
        body, name=name, grid_spec=grid_spec, out_shape=jax.ShapeDtypeStruct(received.shape, BF16),
        compiler_params=_params("parallel", "parallel", "parallel"),
    )(place, grad, received)


def _send_partials(parts, piece=(0, 1)):
    k_th, n_pieces = piece

    def cols(part):
        width = part.shape[2] // n_pieces
        return pl.ds(k_th * width, width)

    def start(ins, outs, send, recv):
        x, y, c, chips = _place()
        for t, part in enumerate(ins):
            for k, (cx, cy) in enumerate(chips):
                _remote(part.at[2 * cx + cy, :, cols(part)], outs[t].at[2 * x + y],
                        send(3 * t + k), recv(3 * t + k), (cx, cy, c)).start()

    def finish(ins, outs, send, recv):
        x, y, c, chips = _place()
        for t, part in enumerate(ins):
            for k, (cx, cy) in enumerate(chips):
                slot = outs[t].at[2 * cx + cy]
                _remote(slot, slot, send(3 * t + k), recv(3 * t + k), (cx, cy, c)).wait_recv()
        for t, part in enumerate(ins):
            for k, (cx, cy) in enumerate(chips):
                sent = part.at[2 * cx + cy, :, cols(part)]
                _remote(sent, sent, send(3 * t + k), recv(3 * t + k), (cx, cy, c)).wait_send()

    shapes = [jax.ShapeDtypeStruct(a.shape[:2] + (a.shape[2] // n_pieces,), a.dtype) for a in parts]
    return _Phase(parts, shapes, False, 3 * len(parts), start, finish)


def _sum_chips(parts, slots, place, *, name, piece=(0, 1), into=None):
    ns, rows, width = slots.shape
    k_th, n_pieces = piece
    half = width * n_pieces
    tr, tc = _rc_tile(rows, width, pref_rows=512)
    per = width // tc

    def body(place_ref, p_ref, s_ref, *rest):
        acc = p_ref[...].astype(F32)
        for k in range(ns):
            acc = acc + jnp.where(place_ref[0] == k, 0.0, s_ref[k].astype(F32))
        rest[-1][...] = acc

    grid_spec = pltpu.PrefetchScalarGridSpec(
        num_scalar_prefetch=1, grid=(rows // tr, per),
        in_specs=[pl.BlockSpec((None, tr, tc), lambda i, j, pr: (pr[0], i, k_th * per + j)),
                  pl.BlockSpec((ns, tr, tc), lambda i, j, pr: (0, i, j))] + ([ANY] if into is not None else []),
        out_specs=pl.BlockSpec((tr, tc), lambda i, j, pr: (i, (pr[1] * n_pieces + k_th) * per + j)),
    )
    return pl.pallas_call(
        body, name=name, grid_spec=grid_spec, out_shape=jax.ShapeDtypeStruct((rows, 2 * half), F32),
        input_output_aliases={3: 0} if into is not None else {},
        compiler_params=_params("parallel", "parallel"),
    )(place, parts, slots, *([into] if into is not None else []))


def _join_halves(bufs):
    def copies(outs, send, recv, incoming):
        x, y, c, _ = _place()
        for t, buf in enumerate(outs):
            half = buf.shape[1] // 2
            cols = buf.at[:, pl.ds(((1 - c) if incoming else c) * half, half)]
            yield _remote(cols, cols, send(t), recv(t), (x, y, 1 - c))

    def start(ins, outs, send, recv):
        for cp in copies(outs, send, recv, False):
            cp.start()

    def finish(ins, outs, send, recv):
        for cp in copies(outs, send, recv, True):
            cp.wait_recv()
        for cp in copies(outs, send, recv, False):
            cp.wait_send()

    return _Phase(bufs, [], True, len(bufs), start, finish)


def _gather_small(buf):
    def slot(out, px, py, pc):
        return out.at[4 * px + 2 * py + pc]

    def start(ins, outs, send, recv):
        x, y, c, chips = _place()
        mine = slot(outs[0], x, y, c)
        _remote(ins[0], mine, send(0), recv(0), (x, y, 1 - c)).start()
        for k, (cx, cy) in enumerate(chips):
            _remote(ins[0], mine, send(1 + k), recv(1 + k), (cx, cy, c)).start()

    def finish(ins, outs, send, recv):
        x, y, c, chips = _place()
        sibling = (x, y, 1 - c)
        for k, (cx, cy) in enumerate(chips):
            arrived = slot(outs[0], cx, cy, c)
            _remote(arrived, arrived, send(1 + k), recv(1 + k), sibling).wait_recv()
            _remote(arrived, arrived, send(4 + k), recv(4 + k), sibling).start()
        theirs = slot(outs[0], x, y, 1 - c)
        _remote(theirs, theirs, send(0), recv(0), sibling).wait_recv()
        for k, (cx, cy) in enumerate(chips):
            passed = slot(outs[0], cx, cy, 1 - c)
            _remote(passed, passed, send(4 + k), recv(4 + k), sibling).wait_recv()
        for i in range(7):
            _remote(ins[0], ins[0], send(i), recv(i), sibling).wait_send()

    return _Phase([buf], [jax.ShapeDtypeStruct((N_DEV,) + buf.shape, buf.dtype)], False, 7, start, finish)


def _adamw_math(w, g, m, v):
    m = ADAM_B1 * m + (1.0 - ADAM_B1) * g
    v = ADAM_B2 * v + (1.0 - ADAM_B2) * (g * g)
    m_hat = m / (1.0 - ADAM_B1 ** ADAM_STEP)
    v_hat = v / (1.0 - ADAM_B2 ** ADAM_STEP)
    delta = -ADAM_LR * (m_hat / (jnp.sqrt(v_hat) + ADAM_EPS) + ADAM_WD * w)
    return delta, m, v


def _adamw(w, g, m, v, *, name):
    rows, cols = w.shape
    tr, tc = _rc_tile(rows, cols)

    def body(w_ref, g_ref, m_ref, v_ref, go_ref, d_ref, mo_ref, vo_ref):
        g = g_ref[...]
        go_ref[...] = g
        d_ref[...], mo_ref[...], vo_ref[...] = _adamw_math(w_ref[...], g, m_ref[...], v_ref[...])

    blk = pl.BlockSpec((tr, tc), lambda i, j: (i, j))
    shape = jax.ShapeDtypeStruct((rows, cols), F32)
    return pl.pallas_call(
        body, name=name, out_shape=(shape, shape, shape, shape), grid=(rows // tr, cols // tc),
        in_specs=[blk] * 4, out_specs=(blk, blk, blk, blk), compiler_params=_params("parallel", "parallel"),
    )(w, g, m, v)


def _adamw_small(gathered, own, place, w, m, v, *, name):
    nd = gathered.shape[0]

    def body(place_ref, gs_ref, own_ref, w_ref, m_ref, v_ref, g_ref, d_ref, mo_ref, vo_ref):
        me = 2 * place_ref[0] + place_ref[1]
        g = jnp.zeros(own_ref.shape, F32)
        for k in range(nd):
            g = g + jnp.where(me == k, own_ref[...], gs_ref[k])
        g_ref[...] = g
        d_ref[...], mo_ref[...], vo_ref[...] = _adamw_math(w_ref[...], g, m_ref[...], v_ref[...])

    whole = pl.BlockSpec(w.shape, lambda i, pr: (0, 0))
    grid_spec = pltpu.PrefetchScalarGridSpec(
        num_scalar_prefetch=1, grid=(1,),
        in_specs=[pl.BlockSpec(gathered.shape, lambda i, pr: (0, 0, 0)), whole, whole, whole, whole],
        out_specs=(whole, whole, whole, whole))
    shape = jax.ShapeDtypeStruct(w.shape, F32)
    return pl.pallas_call(body, name=name, grid_spec=grid_spec, out_shape=(shape, shape, shape, shape),
                          compiler_params=_params("arbitrary"))(place, gathered, own, w, m, v)


def _pack(parts):
    flat = jnp.concatenate([p.reshape(-1).astype(F32) for p in parts])
    rows = -(-flat.shape[0] // (8 * LANES)) * 8
    return jnp.pad(flat, (0, rows * LANES - flat.shape[0])).reshape(rows, LANES)


def _unpack(buf, shapes):
    flat = buf.reshape(-1)
    out, pos = [], 0
    for shp in shapes:
        size = int(np.prod(shp))
        out.append(flat[pos:pos + size].reshape(shp))
        pos += size
    return out


ROW_BLOCK = 256


def _realign_rows(sources, segments, out_shape, *, name):
    n_slots, rows, cols = out_shape
    n_src = len(sources)
    per_slot = -(-rows // ROW_BLOCK)
    table = np.zeros((6, n_slots * per_slot, n_src), np.int32)
    for so in range(n_slots):
        for first, last, src, src_slot, src_row in segments[so]:
            for blk in range(first // ROW_BLOCK, (last - 1) // ROW_BLOCK + 1):
                lo, hi = max(first, blk * ROW_BLOCK), min(last, (blk + 1) * ROW_BLOCK)
                base = src_row + (blk * ROW_BLOCK - first)
                m0 = (base + lo - blk * ROW_BLOCK) // ROW_BLOCK
                at = so * per_slot + blk
                assert table[4, at, src] == 0, "two segments of one block share a source operand"
                table[:, at, src] = (src_slot, m0, base - m0 * ROW_BLOCK, lo - blk * ROW_BLOCK, hi - blk * ROW_BLOCK,
                                     min(2 * ROW_BLOCK, sources[src].shape[1] - m0 * ROW_BLOCK))
    last_block = [-(-a.shape[1] // ROW_BLOCK) - 1 for a in sources]

    def body(slot_ref, blk_ref, off_ref, lo_ref, hi_ref, valid_ref, *refs):
        o_ref, acc = refs[2 * n_src], refs[2 * n_src + 1]
        at = (pl.program_id(0) * per_slot + pl.program_id(1)) * n_src
        acc[...] = jnp.zeros_like(acc)
        for p in range(n_src):
            @pl.when(hi_ref[at + p] > lo_ref[at + p])
            def _():
                two = jnp.concatenate([refs[2 * p][...], refs[2 * p + 1][...]], axis=0)
                src_row = lax.broadcasted_iota(jnp.int32, two.shape, 0)
                two = jnp.where(src_row < valid_ref[at + p], two, jnp.zeros_like(two))
                r = lax.broadcasted_iota(jnp.int32, (ROW_BLOCK, 2 * ROW_BLOCK), 0)
                c = lax.broadcasted_iota(jnp.int32, (ROW_BLOCK, 2 * ROW_BLOCK), 1)
                place = (c == r + off_ref[at + p]) & (r >= lo_ref[at + p]) & (r < hi_ref[at + p])
                acc[...] += jnp.dot(place.astype(two.dtype), two, preferred_element_type=F32)
        o_ref[...] = acc[...].astype(o_ref.dtype)

    def src_spec(p, second):
        def index(so, i, slot_r, blk_r, off_r, lo_r, hi_r, valid_r):
            at = (so * per_slot + i) * n_src + p
            return slot_r[at], jnp.minimum(blk_r[at] + second, last_block[p]), 0
        return pl.BlockSpec((None, ROW_BLOCK, cols), index)

    grid_spec = pltpu.PrefetchScalarGridSpec(
        num_scalar_prefetch=6, grid=(n_slots, per_slot),
        in_specs=[src_spec(p, second) for p in range(n_src) for second in (0, 1)],
        out_specs=pl.BlockSpec((None, ROW_BLOCK, cols), lambda so, i, *_: (so, i, 0)),
        scratch_shapes=[pltpu.VMEM((ROW_BLOCK, cols), F32)],
    )
    flat = [jnp.asarray(table[k].reshape(-1)) for k in range(6)]
    return pl.pallas_call(
        body, name=name, grid_spec=grid_spec, out_shape=jax.ShapeDtypeStruct(out_shape, sources[0].dtype),
        compiler_params=_params("parallel", "arbitrary"),
    )(*flat, *[a for a in sources for _ in (0, 1)])


def _shard_rows(g, lo, hi):
    rs = g.shape[1]
    pieces = []
    for j in range(g.shape[0]):
        a, b = max(lo, j * rs), min(hi, (j + 1) * rs)
        if a < b:
            pieces.append(g[j, a - j * rs:b - j * rs])
    return pieces


def kernel(x, norm_mix_g, w_in, b_f, gmlp_ln_g, gmlp_ln_b, w_s, b_s, attn_out_g, gmlp_out_g, w_out, norm_ffn_g, w_ff1, w_ff2, norm_final_g, loss_target, m_norm_mix_g, m_w_in, m_b_f, m_gmlp_ln_g, m_gmlp_ln_b, m_w_s, m_b_s, m_attn_out_g, m_gmlp_out_g, m_w_out, m_norm_ffn_g, m_w_ff1, m_w_ff2, m_norm_final_g, v_norm_mix_g, v_w_in, v_b_f, v_gmlp_ln_g, v_gmlp_ln_b, v_w_s, v_b_s, v_attn_out_g, v_gmlp_out_g, v_w_out, v_norm_ffn_g, v_w_ff1, v_w_ff2, v_norm_final_g):
    seq, d_model = x.shape[1], x.shape[2]
    d_attn = d_model // 2
    n_heads = d_attn // HEAD_DIM
    qkv = 3 * d_attn
    shard_cols = w_in.shape[2]
    assert N_CHIPS * shard_cols == qkv + n_heads + 2 * d_attn
    xs = x.reshape(seq, d_model)
    target = loss_target.reshape(seq, d_model)

    place = jnp.stack([2 * lax.axis_index("x") + lax.axis_index("y"), lax.axis_index("c")]).astype(jnp.int32)
    names = ["w_in", "w_out", "w_ff1", "w_ff2"]
    wt_in, mt_in, vt_in = w_in[0].T, m_w_in[0].T, v_w_in[0].T
    b_in, _ = _cast_into_slot(wt_in, place, name="cast_w_in")
    (b_out, b_ff1, b_ff2, h), (g_in,) = _casts_and_norm(
        [w_out[0], w_ff1[0], w_ff2[0]], place, xs, norm_mix_g, name="casts_and_norm_mix",
        phases=[_gather_by_parts([b_in], 0, GATHER_PARTS)])
    n_cols = N_CHIPS * shard_cols
    gate_slot, gate_row = divmod(qkv, shard_cols)
    assert gate_row + n_heads <= shard_cols
    pieces = []
    for j in range(N_CHIPS):
        if j == gate_slot:
            pieces += [(j, 0, gate_row), (j, gate_row + n_heads, shard_cols - gate_row - n_heads)]
        else:
            pieces.append((j, 0, shard_cols))
    fwd_segments, at = [[]], 0
    for order, (j, src_row, size) in enumerate(pieces):
        fwd_segments[0].append((at, at + size, order % 3, j, src_row))
        at += size
    wt_main = _realign_rows([g_in] * 3, fwd_segments, (1, n_cols - n_heads, d_model), name="w_in_rows")[0]
    wt_f = jnp.pad(jnp.concatenate(_shard_rows(g_in, qkv, qkv + n_heads), axis=0), ((0, LANES - n_heads), (0, 0)))
    b_f_pad = jnp.pad(b_f, ((0, 0), (0, LANES - n_heads)))
    b_col = b_s[0].T

    first, rest = (0, 1), (1, GATHER_PARTS)
    z, (b_out, b_ff1) = _matmul(h, wt_main, name="in_proj", out_dtype=BF16, trans_b=True, tm=2048,
                                phases=[_gather([b_out], "ici"), _gather([b_ff1], "ici", first)])
    zb, f_cum = _forget_fwd(h, wt_f, b_f_pad, name="forget_fwd")
    f_row = f_cum[:, :n_heads].T[:, None, :]
    (o, lse2), (b_ff1, b_out) = _attn_fwd(z, f_row, n_heads, name="attn_fwd",
                                          phases=[_gather([b_ff1], "ici", rest), _gather([b_out], "d2d")])
    merged = _mix_fwd(z, o, gmlp_ln_g, gmlp_ln_b, w_s[0], b_col, attn_out_g, gmlp_out_g, n_heads, name="mix_fwd")
    w_out_full = b_out.reshape(2 * d_attn, d_model)
    x1, (b_ff1, b_ff2) = _matmul(merged, w_out_full, name="out_proj", out_dtype=F32, residual=xs,
                                 phases=[_gather([b_ff1], "d2d"), _gather([b_ff2], "ici", first)])
    h2, _ = _rmsnorm_fwd(x1, norm_ffn_g, name="norm_ffn")
    a, (b_ff2,) = _matmul(h2, b_ff1, name="ff1", out_dtype=BF16, relu=True, b_sharded=True, tm=2048,
                          phases=[_merge(_gather([b_ff2], "d2d", first), _gather_by_parts([b_ff2], 1, GATHER_PARTS))])
    w_ff2_full = b_ff2.reshape(N_CHIPS * b_ff2.shape[1], d_model)
    x2, _ = _matmul(a, w_ff2_full, name="ff2", out_dtype=F32, square_lhs=True, residual=x1)
    dx2, dx2_b, dg_final, loss = _loss_and_final_bwd(x2, target, norm_final_g.reshape(1, d_model), name="loss_head")

    def pair_sum(g, r, nm):
        return _add_halves(g, r, place, name="grads_pair_sum_" + nm)

    def chip_sum(p, q, nm, **piece):
        return _sum_chips(p, q, place, name="grads_chip_sum_" + nm, **piece)

    dw_ff2, _ = _matmul(a, dx2_b, name="ff2_dw", out_dtype=BF16, trans_a=True, square_lhs=True)
    dw_ff2 = dw_ff2.reshape(N_CHIPS, -1, d_model)
    da, (r_ff2,) = _matmul(dx2_b, w_ff2_full, name="ff2_dlhs", out_dtype=BF16, trans_b=True, scale2_by=a, tm=2048,
                           phases=[_swap_halves([dw_ff2])])
    ps_ff2 = pair_sum(dw_ff2, r_ff2, "w_ff2")
    dh2, (q_ff2a,) = _matmul(da, b_ff1, name="ff1_dlhs", out_dtype=F32, trans_b=True, b_sharded=True,
                             phases=[_send_partials([ps_ff2], (0, 2))])
    dw_ff1, (q_ff2b,) = _matmul(h2, da, name="ff1_dw", out_dtype=BF16, trans_a=True, out_sharded=True, tk=seq,
                                phases=[_send_partials([ps_ff2], (1, 2))])
    g_ff2 = chip_sum(ps_ff2, q_ff2a, "w_ff2_a", piece=(0, 2))
    g_ff2 = chip_sum(ps_ff2, q_ff2b, "w_ff2_b", piece=(1, 2), into=g_ff2)
    (dx1, dg_ffn, dx1_b), (g_ff2,) = _rmsnorm_bwd(dh2, x1, dx2, norm_ffn_g, name="norm_ffn_bwd", rounded_copy=True,
                                                   phases=[_join_halves([g_ff2])])
    dw_out, _ = _matmul(merged, dx1_b, name="out_proj_dw", out_dtype=BF16, trans_a=True, tk=seq)
    dw_out = dw_out.reshape(N_CHIPS, -1, d_model)
    d_merged, (r_ff1, r_out) = _matmul(dx1_b, w_out_full, name="out_proj_dlhs", out_dtype=F32, trans_b=True,
                                       phases=[_swap_halves([dw_ff1, dw_out])])
    ps_ff1, ps_out = pair_sum(dw_ff1, r_ff1, "w_ff1"), pair_sum(dw_out, r_out, "w_out")
    d_o, dzu, dzv, dw_s, db_col, dlg, dlb, dag, dgg = _mix_bwd(
        z, o, d_merged, gmlp_ln_g, gmlp_ln_b, w_s[0], b_col, attn_out_g, gmlp_out_g, n_heads, name="mix_bwd")
    (dq, dk, dv, d_f_key, d_f_query), (q_ff1, q_out) = _attn_bwd(
        z, o, d_o, lse2, f_row, n_heads, name="attn_bwd", phases=[_send_partials([ps_ff1, ps_out])])
    g_ff1, g_out = chip_sum(ps_ff1, q_ff1, "w_ff1"), chip_sum(ps_out, q_out, "w_out")
    d_f = d_f_key.reshape(n_heads, seq) + d_f_query.reshape(n_heads, seq)
    d_f_pad = jnp.pad(d_f.T, ((0, 0), (0, LANES - n_heads)))
    dzf, db_f = _forget_bwd(d_f_pad, zb, name="forget_bwd")
    dz = jnp.concatenate([dq, dk, dv, dzu, dzv], axis=1)
    early_g = _pack([db_f[:, :n_heads], dlg, dlb, dw_s, db_col.T, dag, dgg, dg_ffn, dg_final])
    dwt_main, (g_ff1, g_out, early_all) = _matmul(dz, h, name="in_proj_dw", out_dtype=BF16, trans_a=True, tk=seq,
                                                  phases=[_join_halves([g_ff1, g_out]), _gather_small(early_g)])
    dwt_f, _ = _matmul(dzf, h, name="gate_dw", out_dtype=BF16, trans_a=True)
    bwd_segments = []
    for j in range(N_CHIPS):
        first = j * shard_cols
        if j < gate_slot:
            bwd_segments.append([(0, shard_cols, 0, 0, first)])
        elif j > gate_slot:
            bwd_segments.append([(0, shard_cols, 0, 0, first - n_heads)])
        else:
            bwd_segments.append([(0, gate_row, 0, 0, first), (gate_row, gate_row + n_heads, 1, 0, 0),
                                 (gate_row + n_heads, shard_cols, 2, 0, qkv)])
    dw_in = _realign_rows([dwt_main[None], dwt_f[None], dwt_main[None]], bwd_segments,
                          (N_CHIPS, shard_cols, d_model), name="dw_in_rows")
    dh_gate, (r_in,) = _matmul(dzf, wt_f, name="gate_dlhs", out_dtype=F32, phases=[_swap_halves([dw_in])])
    ps_in = pair_sum(dw_in, r_in, "w_in")
    dh, (q_in,) = _matmul(dz, wt_main, name="in_proj_dlhs", out_dtype=F32, residual=dh_gate, tk=2560,
                          phases=[_send_partials([ps_in])])
    g_in_sum = chip_sum(ps_in, q_in, "w_in")
    (grad_x, dg_mix), _ = _rmsnorm_bwd(dh, xs, dx1, norm_mix_g, name="norm_mix_bwd")
    late_g = _pack([dg_mix])
    g_in_sum, late_all = _exchange([_join_halves([g_in_sum]), _gather_small(late_g)], name="grads_join_w_in")

    big = {}
    for nm, g, w, m, v in zip(names, (g_in_sum, g_out, g_ff1, g_ff2), (wt_in, w_out[0], w_ff1[0], w_ff2[0]),
                              (mt_in, m_w_out[0], m_w_ff1[0], m_w_ff2[0]), (vt_in, v_w_out[0], v_w_ff1[0], v_w_ff2[0])):
        big[nm] = tuple((t.T if nm == "w_in" else t)[None] for t in _adamw(w, g, m, v, name="adamw_" + nm))

    small_params = dict(
        norm_mix_g=(norm_mix_g, m_norm_mix_g, v_norm_mix_g), b_f=(b_f, m_b_f, v_b_f),
        gmlp_ln_g=(gmlp_ln_g, m_gmlp_ln_g, v_gmlp_ln_g), gmlp_ln_b=(gmlp_ln_b, m_gmlp_ln_b, v_gmlp_ln_b),
        w_s=(w_s, m_w_s, v_w_s), b_s=(b_s, m_b_s, v_b_s), attn_out_g=(attn_out_g, m_attn_out_g, v_attn_out_g),
        gmlp_out_g=(gmlp_out_g, m_gmlp_out_g, v_gmlp_out_g), norm_ffn_g=(norm_ffn_g, m_norm_ffn_g, v_norm_ffn_g),
        norm_final_g=(norm_final_g, m_norm_final_g, v_norm_final_g))

    def small_step(group, grads_all, grads_own, label):
        w, m, v = ([small_params[nm][k] for nm in group] for k in range(3))
        packed = _adamw_small(grads_all, grads_own, place, _pack(w), _pack(m), _pack(v), name="adamw_small_" + label)
        parts = [_unpack(p, [a.shape for a in w]) for p in packed]
        return {nm: tuple(part[i] for part in parts) for i, nm in enumerate(group)}

    early = ["b_f", "gmlp_ln_g", "gmlp_ln_b", "w_s", "b_s", "attn_out_g", "gmlp_out_g", "norm_ffn_g", "norm_final_g"]
    small = {**small_step(early, early_all, early_g, "early"), **small_step(["norm_mix_g"], late_all, late_g, "late")}

    order = ["norm_mix_g", "w_in", "b_f", "gmlp_ln_g", "gmlp_ln_b", "w_s", "b_s", "attn_out_g", "gmlp_out_g", "w_out",
             "norm_ffn_g", "w_ff1", "w_ff2", "norm_final_g"]
    result = {**small, **big}
    total_loss = lax.psum(loss[0, 0], ("x", "y", "c"))
    outs = [total_loss, grad_x.reshape(x.shape)]
    for part in range(4):
        outs += [result[nm][part] for nm in order]
    return tuple(outs)
```

```python
import functools
import math

import numpy as np
import jax
import jax.numpy as jnp
from jax import lax
from jax.experimental import pallas as pl
from jax.experimental.pallas import tpu as pltpu

HEAD_DIM = 128
CHUNK = 128
EPS = 1e-6
LANES = 128
MXU_COLUMNS = 256
N_CHIPS = 4
N_DEV = 8
VMEM_LIMIT_BYTES = 56 * 1024 * 1024

ADAM_LR = 0.001
ADAM_B1 = 0.9
ADAM_B2 = 0.999
ADAM_EPS = 1e-08
ADAM_WD = 0.01
ADAM_STEP = 10

BF16 = jnp.bfloat16
F32 = jnp.float32
MESH = pl.DeviceIdType.MESH
ANY = pl.BlockSpec(memory_space=pl.ANY)
NEG_BIG = -1e30


def _params(*sem):
    return pltpu.CompilerParams(dimension_semantics=tuple(sem), vmem_limit_bytes=VMEM_LIMIT_BYTES)


def _tile(n, pref, unit):
    t = (min(pref, n) // unit) * unit
    while t >= unit:
        if n % t == 0:
            return t
        t -= unit
    return n


def _rc_tile(rows, cols, pref_rows=256, pref_cols=256):
    if rows % 16 == 0:
        return _tile(rows, pref_rows, 16), cols
    return rows, _tile(cols, pref_cols, LANES)


class _Phase:
    def __init__(self, arrays, out_shapes, in_place, n_sems, start, finish):
        self.arrays, self.out_shapes, self.in_place = list(arrays), list(out_shapes), in_place
        self.n_sems, self.start, self.finish = n_sems, start, finish

    @property
    def n_out(self):
        return len(self.arrays) if self.in_place else len(self.out_shapes)


def _run_phases(phases, steps, comm_in, comm_out, send_sems, recv_sems):
    at_in = at_out = at_sem = 0
    for ph in phases:
        for step in steps:
            getattr(ph, step)(comm_in[at_in:at_in + len(ph.arrays)], comm_out[at_out:at_out + ph.n_out],
                              lambda i, base=at_sem: send_sems.at[base + i], lambda i, base=at_sem: recv_sems.at[base + i])
        at_in, at_out, at_sem = at_in + len(ph.arrays), at_out + ph.n_out, at_sem + ph.n_sems


def _call(body, *, name, grid, in_specs, out_specs, out_shape, operands, semantics, scratch_shapes=(),
          n_prefetch=0, phases=()):
    in_specs, out_specs, out_shape = list(in_specs), list(out_specs), list(out_shape)
    scratch_shapes = list(scratch_shapes)
    n_in, n_out, n_scr = len(operands) - n_prefetch, len(out_shape), len(scratch_shapes)
    comm_in = [a for ph in phases for a in ph.arrays]
    comm_out = [jax.ShapeDtypeStruct(s.shape, s.dtype) for ph in phases
                for s in (ph.arrays if ph.in_place else ph.out_shapes)]
    aliases, at_in, at_out = {}, n_prefetch + n_in, n_out
    for ph in phases:
        if ph.in_place:
            aliases.update({at_in + r: at_out + r for r in range(len(ph.arrays))})
        at_in, at_out = at_in + len(ph.arrays), at_out + ph.n_out
    n_sems = sum(ph.n_sems for ph in phases)

    def hosted(*refs):
        pre, rest = refs[:n_prefetch], refs[n_prefetch:]
        ins, rest = rest[:n_in], rest[n_in:]
        cin, rest = rest[:len(comm_in)], rest[len(comm_in):]
        outs, rest = rest[:n_out], rest[n_out:]
        cout, rest = rest[:len(comm_out)], rest[len(comm_out):]
        scr = rest[:n_scr]
        if phases:
            send_sems, recv_sems = rest[n_scr:]
            ids = [pl.program_id(ax) for ax in range(len(grid))]
            first = functools.reduce(jnp.logical_and, [i == 0 for i in ids])
            last = functools.reduce(jnp.logical_and, [i == g - 1 for i, g in zip(ids, grid)])

            @pl.when(first)
            def _():
                _run_phases(phases, ("start",), cin, cout, send_sems, recv_sems)

        body(*pre, *ins, *outs, *scr)
        if phases:
            @pl.when(last)
            def _():
                _run_phases(phases, ("finish",), cin, cout, send_sems, recv_sems)

    all_in = in_specs + [ANY] * len(comm_in)
    all_out = out_specs + [ANY] * len(comm_out)
    all_scr = scratch_shapes + ([pltpu.SemaphoreType.DMA((n_sems,)), pltpu.SemaphoreType.DMA((n_sems,))] if phases else [])
    if phases:
        semantics = ("arbitrary",) * len(grid)
    kwargs = dict(name=name, out_shape=tuple(out_shape + comm_out), compiler_params=_params(*semantics),
                  input_output_aliases=aliases)
    if n_prefetch:
        kwargs["grid_spec"] = pltpu.PrefetchScalarGridSpec(
            num_scalar_prefetch=n_prefetch, grid=grid, in_specs=all_in, out_specs=tuple(all_out), scratch_shapes=all_scr)
    else:
        kwargs.update(grid=grid, in_specs=all_in, out_specs=tuple(all_out), scratch_shapes=all_scr)
    res = pl.pallas_call(hosted, **kwargs)(*operands, *comm_in)
    return tuple(res[:n_out]), tuple(res[n_out:])


def _only(results):
    outs, comm = results
    return outs[0] if len(outs) == 1 else outs, comm


def _matmul(a, b, *, name, out_dtype, trans_a=False, trans_b=False, tm=1024, tn=1024, tk=2048,
            square_lhs=False, relu=False, residual=None, scale2_by=None,
            b_sharded=False, out_sharded=False, phases=()):
    m, k = (a.shape[1], a.shape[0]) if trans_a else a.shape
    if b_sharded:
        if trans_b:
            n, ks = b.shape[1], b.shape[2]
            assert N_CHIPS * ks == k
        else:
            ns = b.shape[2]
            n = N_CHIPS * ns
            assert b.shape[1] == k
    else:
        n = b.shape[0] if trans_b else b.shape[1]
        assert (b.shape[1] if trans_b else b.shape[0]) == k
    tm = _tile(m, tm, 128)
    tn = _tile(n // N_CHIPS if (out_sharded or (b_sharded and not trans_b)) else n, tn, 128)
    tk = _tile(k // N_CHIPS if (b_sharded and trans_b) else k, tk, 128)
    nk = k // tk

    if trans_a:
        a_spec = pl.BlockSpec((tk, tm), lambda i, j, kk: (kk, i))
    else:
        a_spec = pl.BlockSpec((tm, tk), lambda i, j, kk: (i, kk))
    if b_sharded and trans_b:
        per = ks // tk
        assert per * tk == ks
        b_spec = pl.BlockSpec((None, tn, tk), lambda i, j, kk: (kk // per, j, kk % per))
    elif b_sharded:
        per = ns // tn
        assert per * tn == ns
        b_spec = pl.BlockSpec((None, tk, tn), lambda i, j, kk: (j // per, kk, j % per))
    elif trans_b:
        b_spec = pl.BlockSpec((tn, tk), lambda i, j, kk: (j, kk))
    else:
        b_spec = pl.BlockSpec((tk, tn), lambda i, j, kk: (kk, j))
    if out_sharded:
        ns_out = n // N_CHIPS
        per_o = ns_out // tn
        assert per_o * tn == ns_out
        out_shape = jax.ShapeDtypeStruct((N_CHIPS, m, ns_out), out_dtype)
        o_spec = pl.BlockSpec((None, tm, tn), lambda i, j, kk: (j // per_o, i, j % per_o))
    else:
        out_shape = jax.ShapeDtypeStruct((m, n), out_dtype)
        o_spec = pl.BlockSpec((tm, tn), lambda i, j, kk: (i, j))
    mn_spec = pl.BlockSpec((tm, tn), lambda i, j, kk: (i, j))

    operands, in_specs = [a, b], [a_spec, b_spec]
    if scale2_by is not None:
        operands.append(scale2_by)
        in_specs.append(mn_spec)
    if residual is not None:
        operands.append(residual)
        in_specs.append(mn_spec)
    dims = (((0 if trans_a else 1,), (1 if trans_b else 0,)), ((), ()))
    chunk = MXU_COLUMNS if tn % MXU_COLUMNS == 0 else tn

    def body(*refs):
        a_ref, b_ref = refs[0], refs[1]
        pos = 2
        scale_ref = res_ref = None
        if scale2_by is not None:
            scale_ref = refs[pos]
            pos += 1
        if residual is not None:
            res_ref = refs[pos]
            pos += 1
        o_ref = refs[pos]
        kk = pl.program_id(2)

        av = a_ref[...]
        if square_lhs:
            av = av.astype(F32)
            av = av * av
        av = av.astype(BF16)

        def finish(r, cols):
            if relu:
                r = jnp.maximum(r, 0.0)
            if scale_ref is not None:
                r = r * (2.0 * scale_ref[:, cols].astype(F32))
            if res_ref is not None:
                r = r + res_ref[:, cols].astype(F32)
            o_ref[:, cols] = r.astype(out_dtype)

        if nk == 1:
            for lo in range(0, tn, chunk):
                cols = slice(lo, lo + chunk)
                bv = (b_ref[cols, :] if trans_b else b_ref[:, cols]).astype(BF16)
                finish(lax.dot_general(av, bv, dims, preferred_element_type=F32), cols)
        else:
            acc_ref = refs[pos + 1]
            part = lax.dot_general(av, b_ref[...].astype(BF16), dims, preferred_element_type=F32)

            @pl.when(kk == 0)
            def _():
                acc_ref[...] = part

            @pl.when(jnp.logical_and(kk > 0, kk < nk - 1))
            def _():
                acc_ref[...] += part

            @pl.when(kk == nk - 1)
            def _():
                finish(acc_ref[...] + part, slice(None))

    return _only(_call(
        body, name=name, out_shape=[out_shape], grid=(m // tm, n // tn, nk),
        in_specs=in_specs, out_specs=[o_spec], operands=operands,
        scratch_shapes=[pltpu.VMEM((tm, tn), F32)] if nk > 1 else [],
        semantics=("parallel", "parallel", "arbitrary"), phases=phases))


def _rmsnorm_fwd(x, g, *, name, tr=512, phases=()):
    s, d = x.shape
    tr = _tile(s, tr, 8)

    def body(x_ref, g_ref, o_ref):
        xv = x_ref[...]
        r = lax.rsqrt(jnp.mean(xv * xv, axis=-1, keepdims=True) + EPS)
        o_ref[...] = ((xv * r) * g_ref[...]).astype(BF16)

    return _only(_call(
        body, name=name, out_shape=[jax.ShapeDtypeStruct((s, d), BF16)], grid=(s // tr,),
        in_specs=[pl.BlockSpec((tr, d), lambda i: (i, 0)), pl.BlockSpec((1, d), lambda i: (0, 0))],
        out_specs=[pl.BlockSpec((tr, d), lambda i: (i, 0))], operands=[x, g],
        semantics=("parallel",), phases=phases))


def _rms_bwd_rows(dy, xv, g):
    d = xv.shape[-1]
    r = lax.rsqrt(jnp.mean(xv * xv, axis=-1, keepdims=True) + EPS)
    gdy = dy * g
    dot = jnp.sum(gdy * xv, axis=-1, keepdims=True)
    dx = gdy * r - xv * (r * r * r) * (dot / d)
    return dx, dy * (xv * r)


def _rmsnorm_bwd(dy, x, res, g, *, name, tr=512, rounded_copy=False, phases=()):
    s, d = x.shape
    tr = _tile(s, tr, 8)

    def body(dy_ref, x_ref, res_ref, g_ref, dx_ref, dg_ref, *dxb_ref):
        @pl.when(pl.program_id(0) == 0)
        def _():
            dg_ref[...] = jnp.zeros_like(dg_ref)

        dx, dg_rows = _rms_bwd_rows(dy_ref[...].astype(F32), x_ref[...], g_ref[...])
        out = res_ref[...] + dx
        dx_ref[...] = out
        if rounded_copy:
            dxb_ref[0][...] = out.astype(BF16)
        dg_ref[...] += jnp.sum(dg_rows, axis=0, keepdims=True)

    row = pl.BlockSpec((tr, d), lambda i: (i, 0))
    vec = pl.BlockSpec((1, d), lambda i: (0, 0))
    extra = [jax.ShapeDtypeStruct((s, d), BF16)] if rounded_copy else []
    return _call(
        body, name=name,
        out_shape=[jax.ShapeDtypeStruct((s, d), F32), jax.ShapeDtypeStruct((1, d), F32)] + extra,
        grid=(s // tr,), in_specs=[row, row, row, vec], out_specs=[row, vec] + [row] * len(extra),
        operands=[dy, x, res, g], semantics=("arbitrary",), phases=phases)


def _loss_and_final_bwd(x2, target, g, *, name, tr=512):
    s, d = x2.shape
    tr = _tile(s, tr, 8)

    def body(x_ref, t_ref, g_ref, dx_ref, dxb_ref, dg_ref, loss_ref):
        @pl.when(pl.program_id(0) == 0)
        def _():
            dg_ref[...] = jnp.zeros_like(dg_ref)
            loss_ref[...] = jnp.zeros_like(loss_ref)

        xv, gv = x_ref[...], g_ref[...]
        r = lax.rsqrt(jnp.mean(xv * xv, axis=-1, keepdims=True) + EPS)
        err = (xv * r) * gv - t_ref[...]
        row_loss = jnp.mean(err * err, axis=-1, keepdims=True)
        loss_ref[...] += 0.5 * jnp.sum(row_loss, axis=0, keepdims=True)
        dx, dg_rows = _rms_bwd_rows(err / d, xv, gv)
        dx_ref[...] = dx
        dxb_ref[...] = dx.astype(BF16)
        dg_ref[...] += jnp.sum(dg_rows, axis=0, keepdims=True)

    row = pl.BlockSpec((tr, d), lambda i: (i, 0))
    vec = pl.BlockSpec((1, d), lambda i: (0, 0))
    one = pl.BlockSpec((1, 1), lambda i: (0, 0))
    return pl.pallas_call(
        body, name=name,
        out_shape=(jax.ShapeDtypeStruct((s, d), F32), jax.ShapeDtypeStruct((s, d), BF16),
                   jax.ShapeDtypeStruct((1, d), F32), jax.ShapeDtypeStruct((1, 1), F32)),
        grid=(s // tr,), in_specs=[row, row, vec], out_specs=(row, row, vec, one),
        compiler_params=_params("arbitrary"),
    )(x2, target, g)


def _tri_ones(n, lower):
    r = lax.broadcasted_iota(jnp.int32, (n, n), 0)
    c = lax.broadcasted_iota(jnp.int32, (n, n), 1)
    return jnp.where((c <= r) if lower else (c >= r), 1.0, 0.0).astype(F32)


def _forget_fwd(h, w_f, b_f, *, name, tr=256):
    s, d = h.shape
    tr = _tile(s, tr, 8)

    def body(h_ref, w_ref, b_ref, zb_ref, f_ref, carry):
        @pl.when(pl.program_id(0) == 0)
        def _():
            carry[...] = jnp.zeros_like(carry)

        zb = lax.dot_general(h_ref[...], w_ref[...], (((1,), (1,)), ((), ())), preferred_element_type=F32) + b_ref[...]
        zb_ref[...] = zb
        log_f = jnp.minimum(zb, 0.0) - jnp.log(1.0 + jnp.exp(-jnp.abs(zb)))
        run = jnp.dot(_tri_ones(tr, True), log_f, preferred_element_type=F32,
                      precision=lax.Precision.HIGHEST) + carry[...]
        f_ref[...] = run
        carry[...] = run[tr - 1:tr, :]

    row = pl.BlockSpec((tr, LANES), lambda i: (i, 0))
    return pl.pallas_call(
        body, name=name,
        out_shape=(jax.ShapeDtypeStruct((s, LANES), F32), jax.ShapeDtypeStruct((s, LANES), F32)),
        grid=(s // tr,),
        in_specs=[pl.BlockSpec((tr, d), lambda i: (i, 0)), pl.BlockSpec((LANES, d), lambda i: (0, 0)),
                  pl.BlockSpec((1, LANES), lambda i: (0, 0))],
        out_specs=(row, row), scratch_shapes=[pltpu.VMEM((1, LANES), F32)],
        compiler_params=_params("arbitrary"),
    )(h, w_f, b_f)


def _forget_bwd(d_f, zb, *, name, tr=256):
    s = zb.shape[0]
    tr = _tile(s, tr, 8)
    nb = s // tr

    def body(df_ref, zb_ref, dz_ref, db_ref, carry):
        @pl.when(pl.program_id(0) == 0)
        def _():
            carry[...] = jnp.zeros_like(carry)
            db_ref[...] = jnp.zeros_like(db_ref)

        run = jnp.dot(_tri_ones(tr, False), df_ref[...], preferred_element_type=F32,
                      precision=lax.Precision.HIGHEST) + carry[...]
        carry[...] = run[0:1, :]
        dz = run / (1.0 + jnp.exp(zb_ref[...]))
        dz_ref[...] = dz.astype(BF16)
        db_ref[...] += jnp.sum(dz, axis=0, keepdims=True)

    row = pl.BlockSpec((tr, LANES), lambda i: (nb - 1 - i, 0))
    return pl.pallas_call(
        body, name=name,
        out_shape=(jax.ShapeDtypeStruct((s, LANES), BF16), jax.ShapeDtypeStruct((1, LANES), F32)),
        grid=(nb,), in_specs=[row, row], out_specs=(row, pl.BlockSpec((1, LANES), lambda i: (0, 0))),
        scratch_shapes=[pltpu.VMEM((1, LANES), F32)],
        compiler_params=_params("arbitrary"),
    )(d_f, zb)


def _pairs(nblk, by_kv):
    if by_kv:
        pr = [(i, j) for j in range(nblk) for i in range(j, nblk)]
    else:
        pr = [(i, j) for i in range(nblk) for j in range(i + 1)]
    return (jnp.asarray(np.array([p[0] for p in pr], np.int32)), jnp.asarray(np.array([p[1] for p in pr], np.int32)))


def _causal_mask(rows, keys):
    r = lax.broadcasted_iota(jnp.int32, (rows[1] - rows[0], keys[1] - keys[0]), 0) + rows[0]
    c = lax.broadcasted_iota(jnp.int32, (rows[1] - rows[0], keys[1] - keys[0]), 1) + keys[0]
    return c <= r


def _diagonal_pieces(tb):
    half = tb // 2
    if half % LANES:
        return [((0, tb), (0, tb))]
    return [((0, half), (0, half)), ((half, tb), (0, tb))]


LOG2E = math.log2(math.e)
QK_TO_LOG2 = LOG2E / math.sqrt(HEAD_DIM)


def _attn_logits2(q, k, fk_row):
    sc = lax.dot_general(q, k, (((1,), (1,)), ((), ())), preferred_element_type=F32)
    return sc * QK_TO_LOG2 - fk_row * LOG2E


def _attn_fwd(z, f_row, n_heads, *, name, tb=1024, phases=()):
    s = z.shape[0]
    tb = _tile(s, tb, 128)
    nblk = s // tb
    qi, kj = _pairs(nblk, by_kv=False)

    def body(qi_ref, kj_ref, q_ref, k_ref, v_ref, fk_ref, o_ref, lse_ref, m_sc, l_sc, acc_sc):
        p = pl.program_id(1)
        i, j = qi_ref[p], kj_ref[p]

        @pl.when(j == 0)
        def _():
            m_sc[...] = jnp.full_like(m_sc, NEG_BIG)
            l_sc[...] = jnp.zeros_like(l_sc)
            acc_sc[...] = jnp.zeros_like(acc_sc)

        def update(rows, keys, masked):
            rs, ks = slice(*rows), slice(*keys)
            s2 = _attn_logits2(q_ref[rs, :], k_ref[ks, :], fk_ref[:, ks])
            if masked:
                s2 = jnp.where(_causal_mask(rows, keys), s2, NEG_BIG)
            m_old = m_sc[rs, :]
            m_new = jnp.maximum(m_old, jnp.max(s2, axis=-1, keepdims=True))
            alpha = jnp.exp2(m_old - m_new)
            pv = jnp.exp2(s2 - jnp.tile(m_new, (1, (keys[1] - keys[0]) // LANES)))
            l_sc[rs, :] = alpha * l_sc[rs, :] + jnp.sum(pv, axis=-1, keepdims=True)
            acc_sc[rs, :] = alpha * acc_sc[rs, :] + jnp.dot(pv.astype(BF16), v_ref[ks, :], preferred_element_type=F32)
            m_sc[rs, :] = m_new

        @pl.when(j < i)
        def _():
            update((0, tb), (0, tb), False)

        @pl.when(j == i)
        def _():
            for rows, keys in _diagonal_pieces(tb):
                update(rows, keys, True)
            o_ref[...] = (acc_sc[...] / l_sc[...]).astype(BF16)
            lse_ref[...] = m_sc[...] + jnp.log2(l_sc[...])

    h = n_heads
    return _call(
        body, name=name, n_prefetch=2, grid=(h, int(qi.shape[0])),
        in_specs=[
            pl.BlockSpec((tb, HEAD_DIM), lambda hh, p, qi_r, kj_r: (qi_r[p], hh)),
            pl.BlockSpec((tb, HEAD_DIM), lambda hh, p, qi_r, kj_r: (kj_r[p], h + hh)),
            pl.BlockSpec((tb, HEAD_DIM), lambda hh, p, qi_r, kj_r: (kj_r[p], 2 * h + hh)),
            pl.BlockSpec((None, 1, tb), lambda hh, p, qi_r, kj_r: (hh, 0, kj_r[p])),
        ],
        out_specs=[
            pl.BlockSpec((tb, HEAD_DIM), lambda hh, p, qi_r, kj_r: (qi_r[p], hh)),
            pl.BlockSpec((None, tb, LANES), lambda hh, p, qi_r, kj_r: (hh, qi_r[p], 0)),
        ],
        scratch_shapes=[pltpu.VMEM((tb, LANES), F32), pltpu.VMEM((tb, LANES), F32), pltpu.VMEM((tb, HEAD_DIM), F32)],
        out_shape=[jax.ShapeDtypeStruct((s, h * HEAD_DIM), BF16), jax.ShapeDtypeStruct((h, s, LANES), F32)],
        operands=[qi, kj, z, z, z, f_row], semantics=("parallel", "arbitrary"), phases=phases)


def _attn_bwd(z, o, d_o, lse2, f_row, n_heads, *, name, tb=1024, phases=()):
    s = z.shape[0]
    tb = _tile(s, tb, 128)
    nblk = s // tb
    qi, kj = _pairs(nblk, by_kv=True)
    n_pairs = int(qi.shape[0])
    scale = 1.0 / math.sqrt(HEAD_DIM)
    h = n_heads

    def body(qi_ref, kj_ref, q_ref, k_ref, v_ref, o_ref, do_ref, lse_ref, fk_ref,
             dq_ref, dk_ref, dv_ref, df_ref, dfq_ref, dq_sc, dk_sc, dv_sc, df_sc, dfq_sc):
        p = pl.program_id(1)
        i, j = qi_ref[p], kj_ref[p]

        @pl.when(p == 0)
        def _():
            dq_sc[...] = jnp.zeros_like(dq_sc)
            dfq_sc[...] = jnp.zeros_like(dfq_sc)

        @pl.when(i == j)
        def _():
            dk_sc[...] = jnp.zeros_like(dk_sc)
            dv_sc[...] = jnp.zeros_like(dv_sc)
            df_sc[...] = jnp.zeros_like(df_sc)

        def update(rows, keys, masked):
            rs, ks, n_rows = slice(*rows), slice(*keys), rows[1] - rows[0]
            q, k, v, do = q_ref[rs, :], k_ref[ks, :], v_ref[ks, :], do_ref[rs, :]
            delta = jnp.sum(do.astype(F32) * o_ref[rs, :].astype(F32), axis=-1, keepdims=True)
            pv = jnp.exp2(_attn_logits2(q, k, fk_ref[:, ks]) - jnp.tile(lse_ref[rs, :], (1, (keys[1] - keys[0]) // LANES)))
            if masked:
                pv = jnp.where(_causal_mask(rows, keys), pv, 0.0)
            dp = lax.dot_general(do, v, (((1,), (1,)), ((), ())), preferred_element_type=F32)
            ds = pv * (dp - delta)
            ds_b = ds.astype(BF16)
            dv_sc[ks, :] += lax.dot_general(pv.astype(BF16), do, (((0,), (0,)), ((), ())), preferred_element_type=F32)
            dk_sc[ks, :] += lax.dot_general(ds_b, q, (((0,), (0,)), ((), ())), preferred_element_type=F32)
            at = pl.ds(pl.multiple_of(i * tb + rows[0], LANES), n_rows)
            dq_sc[at, :] += jnp.dot(ds_b, k, preferred_element_type=F32)
            df_sc[:, ks] -= jnp.sum(ds, axis=0, keepdims=True)
            dfq_sc[at, :] += jnp.broadcast_to(jnp.sum(ds, axis=1, keepdims=True), (n_rows, LANES))

        @pl.when(i > j)
        def _():
            update((0, tb), (0, tb), False)

        @pl.when(i == j)
        def _():
            for rows, keys in _diagonal_pieces(tb):
                update(rows, keys, True)

        @pl.when(i == nblk - 1)
        def _():
            dk_ref[...] = (dk_sc[...] * scale).astype(BF16)
            dv_ref[...] = dv_sc[...].astype(BF16)
            df_ref[...] = df_sc[...]

        @pl.when(p == n_pairs - 1)
        def _():
            dq_ref[...] = (dq_sc[...] * scale).astype(BF16)
            dfq_ref[...] = jnp.transpose(dfq_sc[...])[0:1, :]

    qblk = lambda off: pl.BlockSpec((tb, HEAD_DIM), lambda hh, p, qi_r, kj_r: (qi_r[p], off + hh))
    kblk = lambda off: pl.BlockSpec((tb, HEAD_DIM), lambda hh, p, qi_r, kj_r: (kj_r[p], off + hh))
    qrep = pl.BlockSpec((None, tb, LANES), lambda hh, p, qi_r, kj_r: (hh, qi_r[p], 0))
    krow = pl.BlockSpec((None, 1, tb), lambda hh, p, qi_r, kj_r: (hh, 0, kj_r[p]))
    act = jax.ShapeDtypeStruct((s, h * HEAD_DIM), BF16)
    return _call(
        body, name=name, n_prefetch=2, grid=(h, n_pairs),
        in_specs=[qblk(0), kblk(h), kblk(2 * h), qblk(0), qblk(0), qrep, krow],
        out_specs=[
            pl.BlockSpec((s, HEAD_DIM), lambda hh, p, qi_r, kj_r: (0, hh)),
            kblk(0), kblk(0), krow,
            pl.BlockSpec((None, 1, s), lambda hh, p, qi_r, kj_r: (hh, 0, 0)),
        ],
        scratch_shapes=[pltpu.VMEM((s, HEAD_DIM), F32), pltpu.VMEM((tb, HEAD_DIM), F32),
                        pltpu.VMEM((tb, HEAD_DIM), F32), pltpu.VMEM((1, tb), F32), pltpu.VMEM((s, LANES), F32)],
        out_shape=[act, act, act, jax.ShapeDtypeStruct((h, 1, s), F32), jax.ShapeDtypeStruct((h, 1, s), F32)],
        operands=[qi, kj, z, z, z, o, d_o, lse2, f_row], semantics=("parallel", "arbitrary"), phases=phases)


GELU_C = math.sqrt(2.0 / math.pi)
GELU_A = 0.044715


def _gelu(x):
    return 0.5 * x * (1.0 + jnp.tanh(GELU_C * (x + GELU_A * (x * x * x))))


def _gelu_and_grad(x):
    t = jnp.tanh(GELU_C * (x + GELU_A * (x * x * x)))
    y = 0.5 * x * (1.0 + t)
    dy = 0.5 * (1.0 + t) + 0.5 * x * (1.0 - t * t) * (GELU_C * (1.0 + 3.0 * GELU_A * (x * x)))
    return y, dy


def _layernorm_parts(g):
    mu = jnp.mean(g, axis=-1, keepdims=True)
    xc = g - mu
    rs = lax.rsqrt(jnp.mean(xc * xc, axis=-1, keepdims=True) + EPS)
    return xc * rs, rs


def _spatial_mix(w_ref, bcol_ref, vv_b, n_heads, n_chunks):
    tril = _causal_mask((0, CHUNK), (0, CHUNK))
    cols = []
    for hh in range(n_heads):
        wc = jnp.where(tril, w_ref[hh], 0.0).astype(BF16)
        lanes = slice(hh * HEAD_DIM, (hh + 1) * HEAD_DIM)
        rows = [jnp.dot(wc, vv_b[c * CHUNK:(c + 1) * CHUNK, lanes], preferred_element_type=F32)
                + bcol_ref[:, hh:hh + 1] for c in range(n_chunks)]
        cols.append(jnp.concatenate(rows, axis=0))
    return jnp.concatenate(cols, axis=1)


def _mix_fwd(z, o, ln_g, ln_b, w_s, b_col, attn_g, gm_g, n_heads, *, name, tr=256):
    s = z.shape[0]
    dg = n_heads * HEAD_DIM
    tr = _tile(s, tr, CHUNK)
    n_chunks = tr // CHUNK

    def body(zu_ref, zv_ref, o_ref, lg_ref, lb_ref, w_ref, bcol_ref, ag_ref, gg_ref, out_ref):
        u = _gelu(zu_ref[...].astype(F32))
        xhat, _ = _layernorm_parts(_gelu(zv_ref[...].astype(F32)))
        vv = xhat * lg_ref[...] + lb_ref[...]
        gm = u * _spatial_mix(w_ref, bcol_ref, vv.astype(BF16), n_heads, n_chunks)
        rg = lax.rsqrt(jnp.mean(gm * gm, axis=-1, keepdims=True) + EPS)
        ov = o_ref[...].astype(F32)
        ra = lax.rsqrt(jnp.mean(ov * ov, axis=-1, keepdims=True) + EPS)
        out_ref[:, :dg] = ((ov * ra) * ag_ref[...]).astype(BF16)
        out_ref[:, dg:] = ((gm * rg) * gg_ref[...]).astype(BF16)

    vec = pl.BlockSpec((1, dg), lambda i: (0, 0))
    return pl.pallas_call(
        body, name=name, out_shape=jax.ShapeDtypeStruct((s, 2 * dg), BF16), grid=(s // tr,),
        in_specs=[pl.BlockSpec((tr, dg), lambda i: (i, 3)), pl.BlockSpec((tr, dg), lambda i: (i, 4)),
                  pl.BlockSpec((tr, dg), lambda i: (i, 0)), vec, vec,
                  pl.BlockSpec((n_heads, CHUNK, CHUNK), lambda i: (0, 0, 0)),
                  pl.BlockSpec((CHUNK, n_heads), lambda i: (0, 0)), vec, vec],
        out_specs=pl.BlockSpec((tr, 2 * dg), lambda i: (i, 0)),
        compiler_params=_params("parallel"),
    )(z, z, o, ln_g, ln_b, w_s, b_col, attn_g, gm_g)


def _mix_bwd(z, o, d_merged, ln_g, ln_b, w_s, b_col, attn_g, gm_g, n_heads, *, name, tr=256):
    s = z.shape[0]
    dg = n_heads * HEAD_DIM
    tr = _tile(s, tr, CHUNK)
    n_chunks = tr // CHUNK

    def body(zu_ref, zv_ref, o_ref, dm_ref, lg_ref, lb_ref, w_ref, bcol_ref, ag_ref, gg_ref,
             do_ref, dzu_ref, dzv_ref, dw_ref, dbcol_ref, dlg_ref, dlb_ref, dag_ref, dgg_ref):
        @pl.when(pl.program_id(0) == 0)
        def _():
            for ref in (dw_ref, dbcol_ref, dlg_ref, dlb_ref, dag_ref, dgg_ref):
                ref[...] = jnp.zeros_like(ref)

        d_o, dag_rows = _rms_bwd_rows(dm_ref[:, :dg], o_ref[...].astype(F32), ag_ref[...])
        do_ref[...] = d_o.astype(BF16)
        dag_ref[...] += jnp.sum(dag_rows, axis=0, keepdims=True)

        u, du_dz = _gelu_and_grad(zu_ref[...].astype(F32))
        gv, dgv_dz = _gelu_and_grad(zv_ref[...].astype(F32))
        xhat, rs = _layernorm_parts(gv)
        lg = lg_ref[...]
        vv_b = (xhat * lg + lb_ref[...]).astype(BF16)
        mix = _spatial_mix(w_ref, bcol_ref, vv_b, n_heads, n_chunks)
        gm = u * mix
        d_gm, dgg_rows = _rms_bwd_rows(dm_ref[:, dg:], gm, gg_ref[...])
        dgg_ref[...] += jnp.sum(dgg_rows, axis=0, keepdims=True)
        dzu_ref[...] = ((d_gm * mix) * du_dz).astype(BF16)
        d_mix = d_gm * u
        d_mix_b = d_mix.astype(BF16)

        tril = _causal_mask((0, CHUNK), (0, CHUNK))
        lane = lax.broadcasted_iota(jnp.int32, (CHUNK, n_heads), 1)
        cols = []
        db = jnp.zeros((CHUNK, n_heads), F32)
        for hh in range(n_heads):
            wc = jnp.where(tril, w_ref[hh], 0.0).astype(BF16)
            lanes = slice(hh * HEAD_DIM, (hh + 1) * HEAD_DIM)
            dw = jnp.zeros((CHUNK, CHUNK), F32)
            dmix_sum = jnp.zeros((CHUNK, HEAD_DIM), F32)
            rows = []
            for c in range(n_chunks):
                rws = slice(c * CHUNK, (c + 1) * CHUNK)
                dmb = d_mix_b[rws, lanes]
                dw += lax.dot_general(dmb, vv_b[rws, lanes], (((1,), (1,)), ((), ())), preferred_element_type=F32)
                rows.append(lax.dot_general(wc, dmb, (((0,), (0,)), ((), ())), preferred_element_type=F32))
                dmix_sum += d_mix[rws, lanes]
            dw_ref[hh] += jnp.where(tril, dw, 0.0)
            db += jnp.where(lane == hh, jnp.sum(dmix_sum, axis=-1, keepdims=True), 0.0)
            cols.append(jnp.concatenate(rows, axis=0))
        dbcol_ref[...] += db
        d_vv = jnp.concatenate(cols, axis=1)

        dlg_ref[...] += jnp.sum(d_vv * xhat, axis=0, keepdims=True)
        dlb_ref[...] += jnp.sum(d_vv, axis=0, keepdims=True)
        d_xhat = d_vv * lg
        d_gv = rs * (d_xhat - jnp.mean(d_xhat, axis=-1, keepdims=True)
                     - xhat * jnp.mean(d_xhat * xhat, axis=-1, keepdims=True))
        dzv_ref[...] = (d_gv * dgv_dz).astype(BF16)

    vec = pl.BlockSpec((1, dg), lambda i: (0, 0))
    wspec = pl.BlockSpec((n_heads, CHUNK, CHUNK), lambda i: (0, 0, 0))
    bspec = pl.BlockSpec((CHUNK, n_heads), lambda i: (0, 0))
    rowb = pl.BlockSpec((tr, dg), lambda i: (i, 0))
    act = jax.ShapeDtypeStruct((s, dg), BF16)
    vshape = jax.ShapeDtypeStruct((1, dg), F32)
    return pl.pallas_call(
        body, name=name,
        out_shape=(act, act, act, jax.ShapeDtypeStruct((n_heads, CHUNK, CHUNK), F32),
                   jax.ShapeDtypeStruct((CHUNK, n_heads), F32), vshape, vshape, vshape, vshape),
        grid=(s // tr,),
        in_specs=[pl.BlockSpec((tr, dg), lambda i: (i, 3)), pl.BlockSpec((tr, dg), lambda i: (i, 4)),
                  rowb, pl.BlockSpec((tr, 2 * dg), lambda i: (i, 0)), vec, vec, wspec, bspec, vec, vec],
        out_specs=(rowb, rowb, rowb, wspec, bspec, vec, vec, vec, vec),
        compiler_params=_params("arbitrary"),
    )(z, z, o, d_merged, ln_g, ln_b, w_s, b_col, attn_g, gm_g)


def _place():
    x, y, c = lax.axis_index("x"), lax.axis_index("y"), lax.axis_index("c")
    other_chips = [(1 - x, y), (x, 1 - y), (1 - x, 1 - y)]
    return x, y, c, other_chips


def _remote(src, dst, send_sem, recv_sem, to):
    return pltpu.make_async_remote_copy(src_ref=src, dst_ref=dst, send_sem=send_sem, recv_sem=recv_sem,
                                        device_id=to, device_id_type=MESH)


def _cast_into_slot(w, place, *, name, phases=()):
    rows, cols = w.shape
    tr, tc = _rc_tile(rows, cols)

    def body(place_ref, w_ref, o_ref):
        o_ref[...] = w_ref[...].astype(BF16)

    return _only(_call(
        body, name=name, n_prefetch=1, grid=(rows // tr, cols // tc),
        in_specs=[pl.BlockSpec((tr, tc), lambda i, j, pr: (i, j))],
        out_specs=[pl.BlockSpec((None, tr, tc), lambda i, j, pr: (pr[0], i, j))],
        out_shape=[jax.ShapeDtypeStruct((N_CHIPS, rows, cols), BF16)], operands=[place, w],
        semantics=("parallel", "parallel"), phases=phases))


def _casts_and_norm(weights, place, x, g, *, name, rows=256, phases=()):
    cols = x.shape[1]
    jobs = [w.shape[0] // rows for w in weights] + [x.shape[0] // rows]
    assert all(w.shape[1] == cols and w.shape[0] % rows == 0 for w in weights) and x.shape[0] % rows == 0
    first = [sum(jobs[:k]) for k in range(len(jobs))]

    def strip(k):
        return lambda t: jnp.clip(t - first[k], 0, jobs[k] - 1)

    def body(place_ref, *refs):
        n = len(weights)
        w_refs, x_ref, g_ref, outs = refs[:n], refs[n], refs[n + 1], refs[n + 2:]
        t = pl.program_id(0)
        for k in range(n):
            @pl.when(jnp.logical_and(t >= first[k], t < first[k] + jobs[k]))
            def _(k=k):
                outs[k][...] = w_refs[k][...].astype(BF16)

        @pl.when(t >= first[n])
        def _():
            xv = x_ref[...]
            r = lax.rsqrt(jnp.mean(xv * xv, axis=-1, keepdims=True) + EPS)
            outs[n][...] = ((xv * r) * g_ref[...]).astype(BF16)

    in_specs = [pl.BlockSpec((rows, cols), lambda t, pr, k=k: (strip(k)(t), 0)) for k in range(len(weights))]
    in_specs += [pl.BlockSpec((rows, cols), lambda t, pr: (strip(len(weights))(t), 0)),
                 pl.BlockSpec((1, cols), lambda t, pr: (0, 0))]
    out_specs = [pl.BlockSpec((None, rows, cols), lambda t, pr, k=k: (pr[0], strip(k)(t), 0)) for k in range(len(weights))]
    out_specs.append(pl.BlockSpec((rows, cols), lambda t, pr: (strip(len(weights))(t), 0)))
    out_shape = [jax.ShapeDtypeStruct((N_CHIPS,) + w.shape, BF16) for w in weights] + [jax.ShapeDtypeStruct(x.shape, BF16)]
    return _call(body, name=name, n_prefetch=1, grid=(sum(jobs),), in_specs=in_specs, out_specs=out_specs,
                 out_shape=out_shape, operands=[place, *weights, x, g], semantics=("arbitrary",), phases=phases)


def _exchange(phases, *, name):
    comm_in = [a for ph in phases for a in ph.arrays]
    comm_out = [jax.ShapeDtypeStruct(s.shape, s.dtype) for ph in phases for s in (ph.arrays if ph.in_place else ph.out_shapes)]
    aliases, at_in, at_out = {}, 0, 0
    for ph in phases:
        if ph.in_place:
            aliases.update({at_in + r: at_out + r for r in range(len(ph.arrays))})
        at_in, at_out = at_in + len(ph.arrays), at_out + ph.n_out
    n_sems = sum(ph.n_sems for ph in phases)

    def body(*refs):
        cin, cout = refs[:len(comm_in)], refs[len(comm_in):len(comm_in) + len(comm_out)]
        send_sems, recv_sems = refs[len(comm_in) + len(comm_out):]
        _run_phases(phases, ("start", "finish"), cin, cout, send_sems, recv_sems)

    return pl.pallas_call(
        body, name=name, out_shape=tuple(comm_out), in_specs=[ANY] * len(comm_in), out_specs=tuple([ANY] * len(comm_out)),
        input_output_aliases=aliases,
        scratch_shapes=[pltpu.SemaphoreType.DMA((n_sems,)), pltpu.SemaphoreType.DMA((n_sems,))],
    )(*comm_in)


GATHER_PARTS = 4


def _gather(bufs, stage, part=(0, GATHER_PARTS)):
    n = 3 * len(bufs)
    lo, hi = part

    def copies(outs, send, recv, d2d, incoming):
        x, y, c, chips = _place()
        for t, buf in enumerate(outs):
            half = buf.shape[2] // 2
            piece = half // GATHER_PARTS
            for k, (cx, cy) in enumerate(chips):
                i = 3 * t + k + (n if (d2d and stage == "both") else 0)
                cols = pl.ds(((1 - c) if (d2d and incoming) else c) * half + lo * piece, (hi - lo) * piece)
                blk = buf.at[(2 * cx + cy) if (d2d or incoming) else (2 * x + y), :, cols]
                yield _remote(blk, blk, send(i), recv(i), (x, y, 1 - c) if d2d else (cx, cy, c))

    def start(ins, outs, send, recv):
        for cp in copies(outs, send, recv, stage == "d2d", False):
            cp.start()

    def finish(ins, outs, send, recv):
        if stage == "both":
            for arrival, onward in zip(copies(outs, send, recv, False, True), copies(outs, send, recv, True, False)):
                arrival.wait_recv()
                onward.start()
        for cp in copies(outs, send, recv, stage != "ici", True):
            cp.wait_recv()
        for d2d in ((False, True) if stage == "both" else (stage == "d2d",)):
            for cp in copies(outs, send, recv, d2d, False):
                cp.wait_send()

    return _Phase(bufs, [], True, (2 if stage == "both" else 1) * n, start, finish)


def _merge(first, second):
    n_first = first.n_sems

    def later(sem):
        return lambda i: sem(n_first + i)

    def start(ins, outs, send, recv):
        first.start(ins, outs, send, recv)
        second.start(ins, outs, later(send), later(recv))

    def finish(ins, outs, send, recv):
        first.finish(ins, outs, send, recv)
        second.finish(ins, outs, later(send), later(recv))

    return _Phase(first.arrays, [], True, n_first + second.n_sems, start, finish)


def _gather_by_parts(bufs, lo, hi):
    phase = _gather(bufs, "both", (lo, lo + 1))
    for part in range(lo + 1, hi):
        phase = _merge(phase, _gather(bufs, "both", (part, part + 1)))
    return phase


def _swap_halves(grads):
    def copies(ins, outs, send, recv):
        x, y, c, _ = _place()
        for t, g in enumerate(ins):
            half = g.shape[2] // 2
            yield _remote(g.at[:, :, pl.ds((1 - c) * half, half)], outs[t], send(t), recv(t), (x, y, 1 - c))

    def start(ins, outs, send, recv):
        for cp in copies(ins, outs, send, recv):
            cp.start()

    def finish(ins, outs, send, recv):
        for cp in copies(ins, outs, send, recv):
            cp.wait()

    shapes = [jax.ShapeDtypeStruct((a.shape[0], a.shape[1], a.shape[2] // 2), a.dtype) for a in grads]
    return _Phase(grads, shapes, False, len(grads), start, finish)


def _add_halves(grad, received, place, *, name):
    ns, rows, half = received.shape
    tr, tc = _rc_tile(rows, half, pref_rows=1024)
    per = half // tc

    def body(place_ref, g_ref, r_ref, o_ref):
        o_ref[...] = (g_ref[...].astype(F32) + r_ref[...].astype(F32)).astype(BF16)

    grid_spec = pltpu.PrefetchScalarGridSpec(
        num_scalar_prefetch=1, grid=(ns, rows // tr, per),
        in_specs=[pl.BlockSpec((None, tr, tc), lambda s, i, j, pr: (s, i, pr[1] * per + j)),
                  pl.BlockSpec((None, tr, tc), lambda s, i, j, pr: (s, i, j))],
        out_specs=pl.BlockSpec((None, tr, tc), lambda s, i, j, pr: (s, i, j)),
    )
    return pl.pallas_call(
        body, name=name, grid_spec=grid_spec, out_shape=jax.ShapeDtypeStruct(received.shape, BF16),
        compiler_params=_params("parallel", "parallel", "parallel"),
    )(place, grad, received)


def _send_partials(parts, piece=(0, 1)):
    k_th, n_pieces = piece

    def cols(part):
        width = part.shape[2] // n_pieces
        return pl.ds(k_th * width, width)

    def start(ins, outs, send, recv):
        x, y, c, chips = _place()
        for t, part in enumerate(ins):
            for k, (cx, cy) in enumerate(chips):
                _remote(part.at[2 * cx + cy, :, cols(part)], outs[t].at[2 * x + y],
                        send(3 * t + k), recv(3 * t + k), (cx, cy, c)).start()

    def finish(ins, outs, send, recv):
        x, y, c, chips = _place()
        for t, part in enumerate(ins):
            for k, (cx, cy) in enumerate(chips):
                slot = outs[t].at[2 * cx + cy]
                _remote(slot, slot, send(3 * t + k), recv(3 * t + k), (cx, cy, c)).wait_recv()
        for t, part in enumerate(ins):
            for k, (cx, cy) in enumerate(chips):
                sent = part.at[2 * cx + cy, :, cols(part)]
                _remote(sent, sent, send(3 * t + k), recv(3 * t + k), (cx, cy, c)).wait_send()

    shapes = [jax.ShapeDtypeStruct(a.shape[:2] + (a.shape[2] // n_pieces,), a.dtype) for a in parts]
    return _Phase(parts, shapes, False, 3 * len(parts), start, finish)


def _sum_chips(parts, slots, place, *, name, piece=(0, 1), into=None):
    ns, rows, width = slots.shape
    k_th, n_pieces = piece
    half = width * n_pieces
    tr, tc = _rc_tile(rows, width, pref_rows=512)
    per = width // tc

    def body(place_ref, p_ref, s_ref, *rest):
        acc = p_ref[...].astype(F32)
        for k in range(ns):
            acc = acc + jnp.where(place_ref[0] == k, 0.0, s_ref[k].astype(F32))
        rest[-1][...] = acc

    grid_spec = pltpu.PrefetchScalarGridSpec(
        num_scalar_prefetch=1, grid=(rows // tr, per),
        in_specs=[pl.BlockSpec((None, tr, tc), lambda i, j, pr: (pr[0], i, k_th * per + j)),
                  pl.BlockSpec((ns, tr, tc), lambda i, j, pr: (0, i, j))] + ([ANY] if into is not None else []),
        out_specs=pl.BlockSpec((tr, tc), lambda i, j, pr: (i, (pr[1] * n_pieces + k_th) * per + j)),
    )
    return pl.pallas_call(
        body, name=name, grid_spec=grid_spec, out_shape=jax.ShapeDtypeStruct((rows, 2 * half), F32),
        input_output_aliases={3: 0} if into is not None else {},
        compiler_params=_params("parallel", "parallel"),
    )(place, parts, slots, *([into] if into is not None else []))


def _join_halves(bufs):
    def copies(outs, send, recv, incoming):
        x, y, c, _ = _place()
        for t, buf in enumerate(outs):
            half = buf.shape[1] // 2
            cols = buf.at[:, pl.ds(((1 - c) if incoming else c) * half, half)]
            yield _remote(cols, cols, send(t), recv(t), (x, y, 1 - c))

    def start(ins, outs, send, recv):
        for cp in copies(outs, send, recv, False):
            cp.start()

    def finish(ins, outs, send, recv):
        for cp in copies(outs, send, recv, True):
            cp.wait_recv()
        for cp in copies(outs, send, recv, False):
            cp.wait_send()

    return _Phase(bufs, [], True, len(bufs), start, finish)


def _gather_small(buf):
    def slot(out, px, py, pc):
        return out.at[4 * px + 2 * py + pc]

    def start(ins, outs, send, recv):
        x, y, c, chips = _place()
        mine = slot(outs[0], x, y, c)
        _remote(ins[0], mine, send(0), recv(0), (x, y, 1 - c)).start()
        for k, (cx, cy) in enumerate(chips):
            _remote(ins[0], mine, send(1 + k), recv(1 + k), (cx, cy, c)).start()

    def finish(ins, outs, send, recv):
        x, y, c, chips = _place()
        sibling = (x, y, 1 - c)
        for k, (cx, cy) in enumerate(chips):
            arrived = slot(outs[0], cx, cy, c)
            _remote(arrived, arrived, send(1 + k), recv(1 + k), sibling).wait_recv()
            _remote(arrived, arrived, send(4 + k), recv(4 + k), sibling).start()
        theirs = slot(outs[0], x, y, 1 - c)
        _remote(theirs, theirs, send(0), recv(0), sibling).wait_recv()
        for k, (cx, cy) in enumerate(chips):
            passed = slot(outs[0], cx, cy, 1 - c)
            _remote(passed, passed, send(4 + k), recv(4 + k), sibling).wait_recv()
        for i in range(7):
            _remote(ins[0], ins[0], send(i), recv(i), sibling).wait_send()

    return _Phase([buf], [jax.ShapeDtypeStruct((N_DEV,) + buf.shape, buf.dtype)], False, 7, start, finish)


def _adamw_math(w, g, m, v):
    m = ADAM_B1 * m + (1.0 - ADAM_B1) * g
    v = ADAM_B2 * v + (1.0 - ADAM_B2) * (g * g)
    m_hat = m / (1.0 - ADAM_B1 ** ADAM_STEP)
    v_hat = v / (1.0 - ADAM_B2 ** ADAM_STEP)
    delta = -ADAM_LR * (m_hat / (jnp.sqrt(v_hat) + ADAM_EPS) + ADAM_WD * w)
    return delta, m, v


def _adamw(w, g, m, v, *, name):
    rows, cols = w.shape
    tr, tc = _rc_tile(rows, cols)

    def body(w_ref, g_ref, m_ref, v_ref, go_ref, d_ref, mo_ref, vo_ref):
        g = g_ref[...]
        go_ref[...] = g
        d_ref[...], mo_ref[...], vo_ref[...] = _adamw_math(w_ref[...], g, m_ref[...], v_ref[...])

    blk = pl.BlockSpec((tr, tc), lambda i, j: (i, j))
    shape = jax.ShapeDtypeStruct((rows, cols), F32)
    return pl.pallas_call(
        body, name=name, out_shape=(shape, shape, shape, shape), grid=(rows // tr, cols // tc),
        in_specs=[blk] * 4, out_specs=(blk, blk, blk, blk), compiler_params=_params("parallel", "parallel"),
    )(w, g, m, v)


def _adamw_small(gathered, own, place, w, m, v, *, name):
    nd = gathered.shape[0]

    def body(place_ref, gs_ref, own_ref, w_ref, m_ref, v_ref, g_ref, d_ref, mo_ref, vo_ref):
        me = 2 * place_ref[0] + place_ref[1]
        g = jnp.zeros(own_ref.shape, F32)
        for k in range(nd):
            g = g + jnp.where(me == k, own_ref[...], gs_ref[k])
        g_ref[...] = g
        d_ref[...], mo_ref[...], vo_ref[...] = _adamw_math(w_ref[...], g, m_ref[...], v_ref[...])

    whole = pl.BlockSpec(w.shape, lambda i, pr: (0, 0))
    grid_spec = pltpu.PrefetchScalarGridSpec(
        num_scalar_prefetch=1, grid=(1,),
        in_specs=[pl.BlockSpec(gathered.shape, lambda i, pr: (0, 0, 0)), whole, whole, whole, whole],
        out_specs=(whole, whole, whole, whole))
    shape = jax.ShapeDtypeStruct(w.shape, F32)
    return pl.pallas_call(body, name=name, grid_spec=grid_spec, out_shape=(shape, shape, shape, shape),
                          compiler_params=_params("arbitrary"))(place, gathered, own, w, m, v)


def _pack(parts):
    flat = jnp.concatenate([p.reshape(-1).astype(F32) for p in parts])
    rows = -(-flat.shape[0] // (8 * LANES)) * 8
    return jnp.pad(flat, (0, rows * LANES - flat.shape[0])).reshape(rows, LANES)


def _unpack(buf, shapes):
    flat = buf.reshape(-1)
    out, pos = [], 0
    for shp in shapes:
        size = int(np.prod(shp))
        out.append(flat[pos:pos + size].reshape(shp))
        pos += size
    return out


ROW_BLOCK = 256


def _realign_rows(sources, segments, out_shape, *, name):
    n_slots, rows, cols = out_shape
    n_src = len(sources)
    per_slot = -(-rows // ROW_BLOCK)
    table = np.zeros((6, n_slots * per_slot, n_src), np.int32)
    for so in range(n_slots):
        for first, last, src, src_slot, src_row in segments[so]:
            for blk in range(first // ROW_BLOCK, (last - 1) // ROW_BLOCK + 1):
                lo, hi = max(first, blk * ROW_BLOCK), min(last, (blk + 1) * ROW_BLOCK)
                base = src_row + (blk * ROW_BLOCK - first)
                m0 = (base + lo - blk * ROW_BLOCK) // ROW_BLOCK
                at = so * per_slot + blk
                assert table[4, at, src] == 0, "two segments of one block share a source operand"
                table[:, at, src] = (src_slot, m0, base - m0 * ROW_BLOCK, lo - blk * ROW_BLOCK, hi - blk * ROW_BLOCK,
                                     min(2 * ROW_BLOCK, sources[src].shape[1] - m0 * ROW_BLOCK))
    last_block = [-(-a.shape[1] // ROW_BLOCK) - 1 for a in sources]

    def body(slot_ref, blk_ref, off_ref, lo_ref, hi_ref, valid_ref, *refs):
        o_ref, acc = refs[2 * n_src], refs[2 * n_src + 1]
        at = (pl.program_id(0) * per_slot + pl.program_id(1)) * n_src
        acc[...] = jnp.zeros_like(acc)
        for p in range(n_src):
            @pl.when(hi_ref[at + p] > lo_ref[at + p])
            def _():
                two = jnp.concatenate([refs[2 * p][...], refs[2 * p + 1][...]], axis=0)
                src_row = lax.broadcasted_iota(jnp.int32, two.shape, 0)
                two = jnp.where(src_row < valid_ref[at + p], two, jnp.zeros_like(two))
                r = lax.broadcasted_iota(jnp.int32, (ROW_BLOCK, 2 * ROW_BLOCK), 0)
                c = lax.broadcasted_iota(jnp.int32, (ROW_BLOCK, 2 * ROW_BLOCK), 1)
                place = (c == r + off_ref[at + p]) & (r >= lo_ref[at + p]) & (r < hi_ref[at + p])
                acc[...] += jnp.dot(place.astype(two.dtype), two, preferred_element_type=F32)
        o_ref[...] = acc[...].astype(o_ref.dtype)

    def src_spec(p, second):
        def index(so, i, slot_r, blk_r, off_r, lo_r, hi_r, valid_r):
            at = (so * per_slot + i) * n_src + p
            return slot_r[at], jnp.minimum(blk_r[at] + second, last_block[p]), 0
        return pl.BlockSpec((None, ROW_BLOCK, cols), index)

    grid_spec = pltpu.PrefetchScalarGridSpec(
        num_scalar_prefetch=6, grid=(n_slots, per_slot),
        in_specs=[src_spec(p, second) for p in range(n_src) for second in (0, 1)],
        out_specs=pl.BlockSpec((None, ROW_BLOCK, cols), lambda so, i, *_: (so, i, 0)),
        scratch_shapes=[pltpu.VMEM((ROW_BLOCK, cols), F32)],
    )
    flat = [jnp.asarray(table[k].reshape(-1)) for k in range(6)]
    return pl.pallas_call(
        body, name=name, grid_spec=grid_spec, out_shape=jax.ShapeDtypeStruct(out_shape, sources[0].dtype),
        compiler_params=_params("parallel", "arbitrary"),
    )(*flat, *[a for a in sources for _ in (0, 1)])


def _shard_rows(g, lo, hi):
    rs = g.shape[1]
    pieces = []
    for j in range(g.shape[0]):
        a, b = max(lo, j * rs), min(hi, (j + 1) * rs)
        if a < b:
            pieces.append(g[j, a - j * rs:b - j * rs])
    return pieces


def kernel(x, norm_mix_g, w_in, b_f, gmlp_ln_g, gmlp_ln_b, w_s, b_s, attn_out_g, gmlp_out_g, w_out, norm_ffn_g, w_ff1, w_ff2, norm_final_g, loss_target, m_norm_mix_g, m_w_in, m_b_f, m_gmlp_ln_g, m_gmlp_ln_b, m_w_s, m_b_s, m_attn_out_g, m_gmlp_out_g, m_w_out, m_norm_ffn_g, m_w_ff1, m_w_ff2, m_norm_final_g, v_norm_mix_g, v_w_in, v_b_f, v_gmlp_ln_g, v_gmlp_ln_b, v_w_s, v_b_s, v_attn_out_g, v_gmlp_out_g, v_w_out, v_norm_ffn_g, v_w_ff1, v_w_ff2, v_norm_final_g):
    seq, d_model = x.shape[1], x.shape[2]
    d_attn = d_model // 2
    n_heads = d_attn // HEAD_DIM
    qkv = 3 * d_attn
    shard_cols = w_in.shape[2]
    assert N_CHIPS * shard_cols == qkv + n_heads + 2 * d_attn
    xs = x.reshape(seq, d_model)
    target = loss_target.reshape(seq, d_model)

    place = jnp.stack([2 * lax.axis_index("x") + lax.axis_index("y"), lax.axis_index("c")]).astype(jnp.int32)
    names = ["w_in", "w_out", "w_ff1", "w_ff2"]
    wt_in, mt_in, vt_in = w_in[0].T, m_w_in[0].T, v_w_in[0].T
    b_in, _ = _cast_into_slot(wt_in, place, name="cast_w_in")
    (b_out, b_ff1, b_ff2, h), (g_in,) = _casts_and_norm(
        [w_out[0], w_ff1[0], w_ff2[0]], place, xs, norm_mix_g, name="casts_and_norm_mix",
        phases=[_gather_by_parts([b_in], 0, GATHER_PARTS)])
    n_cols = N_CHIPS * shard_cols
    gate_slot, gate_row = divmod(qkv, shard_cols)
    assert gate_row + n_heads <= shard_cols
    pieces = []
    for j in range(N_CHIPS):
        if j == gate_slot:
            pieces += [(j, 0, gate_row), (j, gate_row + n_heads, shard_cols - gate_row - n_heads)]
        else:
            pieces.append((j, 0, shard_cols))
    fwd_segments, at = [[]], 0
    for order, (j, src_row, size) in enumerate(pieces):
        fwd_segments[0].append((at, at + size, order % 3, j, src_row))
        at += size
    wt_main = _realign_rows([g_in] * 3, fwd_segments, (1, n_cols - n_heads, d_model), name="w_in_rows")[0]
    wt_f = jnp.pad(jnp.concatenate(_shard_rows(g_in, qkv, qkv + n_heads), axis=0), ((0, LANES - n_heads), (0, 0)))
    b_f_pad = jnp.pad(b_f, ((0, 0), (0, LANES - n_heads)))
    b_col = b_s[0].T

    first, rest = (0, 1), (1, GATHER_PARTS)
    z, (b_out, b_ff1) = _matmul(h, wt_main, name="in_proj", out_dtype=BF16, trans_b=True, tm=2048,
                                phases=[_gather([b_out], "ici"), _gather([b_ff1], "ici", first)])
    zb, f_cum = _forget_fwd(h, wt_f, b_f_pad, name="forget_fwd")
    f_row = f_cum[:, :n_heads].T[:, None, :]
    (o, lse2), (b_ff1, b_out) = _attn_fwd(z, f_row, n_heads, name="attn_fwd",
                                          phases=[_gather([b_ff1], "ici", rest), _gather([b_out], "d2d")])
    merged = _mix_fwd(z, o, gmlp_ln_g, gmlp_ln_b, w_s[0], b_col, attn_out_g, gmlp_out_g, n_heads, name="mix_fwd")
    w_out_full = b_out.reshape(2 * d_attn, d_model)
    x1, (b_ff1, b_ff2) = _matmul(merged, w_out_full, name="out_proj", out_dtype=F32, residual=xs,
                                 phases=[_gather([b_ff1], "d2d"), _gather([b_ff2], "ici", first)])
    h2, _ = _rmsnorm_fwd(x1, norm_ffn_g, name="norm_ffn")
    a, (b_ff2,) = _matmul(h2, b_ff1, name="ff1", out_dtype=BF16, relu=True, b_sharded=True, tm=2048,
                          phases=[_merge(_gather([b_ff2], "d2d", first), _gather_by_parts([b_ff2], 1, GATHER_PARTS))])
    w_ff2_full = b_ff2.reshape(N_CHIPS * b_ff2.shape[1], d_model)
    x2, _ = _matmul(a, w_ff2_full, name="ff2", out_dtype=F32, square_lhs=True, residual=x1)
    dx2, dx2_b, dg_final, loss = _loss_and_final_bwd(x2, target, norm_final_g.reshape(1, d_model), name="loss_head")

    def pair_sum(g, r, nm):
        return _add_halves(g, r, place, name="grads_pair_sum_" + nm)

    def chip_sum(p, q, nm, **piece):
        return _sum_chips(p, q, place, name="grads_chip_sum_" + nm, **piece)

    dw_ff2, _ = _matmul(a, dx2_b, name="ff2_dw", out_dtype=BF16, trans_a=True, square_lhs=True)
    dw_ff2 = dw_ff2.reshape(N_CHIPS, -1, d_model)
    da, (r_ff2,) = _matmul(dx2_b, w_ff2_full, name="ff2_dlhs", out_dtype=BF16, trans_b=True, scale2_by=a, tm=2048,
                           phases=[_swap_halves([dw_ff2])])
    ps_ff2 = pair_sum(dw_ff2, r_ff2, "w_ff2")
    dh2, (q_ff2a,) = _matmul(da, b_ff1, name="ff1_dlhs", out_dtype=F32, trans_b=True, b_sharded=True,
                             phases=[_send_partials([ps_ff2], (0, 2))])
    dw_ff1, (q_ff2b,) = _matmul(h2, da, name="ff1_dw", out_dtype=BF16, trans_a=True, out_sharded=True, tk=seq,
                                phases=[_send_partials([ps_ff2], (1, 2))])
    g_ff2 = chip_sum(ps_ff2, q_ff2a, "w_ff2_a", piece=(0, 2))
    g_ff2 = chip_sum(ps_ff2, q_ff2b, "w_ff2_b", piece=(1, 2), into=g_ff2)
    (dx1, dg_ffn, dx1_b), (g_ff2,) = _rmsnorm_bwd(dh2, x1, dx2, norm_ffn_g, name="norm_ffn_bwd", rounded_copy=True,
                                                   phases=[_join_halves([g_ff2])])
    dw_out, _ = _matmul(merged, dx1_b, name="out_proj_dw", out_dtype=BF16, trans_a=True, tk=seq)
    dw_out = dw_out.reshape(N_CHIPS, -1, d_model)
    d_merged, (r_ff1, r_out) = _matmul(dx1_b, w_out_full, name="out_proj_dlhs", out_dtype=F32, trans_b=True,
                                       phases=[_swap_halves([dw_ff1, dw_out])])
    ps_ff1, ps_out = pair_sum(dw_ff1, r_ff1, "w_ff1"), pair_sum(dw_out, r_out, "w_out")
    d_o, dzu, dzv, dw_s, db_col, dlg, dlb, dag, dgg = _mix_bwd(
        z, o, d_merged, gmlp_ln_g, gmlp_ln_b, w_s[0], b_col, attn_out_g, gmlp_out_g, n_heads, name="mix_bwd")
    (dq, dk, dv, d_f_key, d_f_query), (q_ff1, q_out) = _attn_bwd(
        z, o, d_o, lse2, f_row, n_heads, name="attn_bwd", phases=[_send_partials([ps_ff1, ps_out])])
    g_ff1, g_out = chip_sum(ps_ff1, q_ff1, "w_ff1"), chip_sum(ps_out, q_out, "w_out")
    d_f = d_f_key.reshape(n_heads, seq) + d_f_query.reshape(n_heads, seq)
    d_f_pad = jnp.pad(d_f.T, ((0, 0), (0, LANES - n_heads)))
    dzf, db_f = _forget_bwd(d_f_pad, zb, name="forget_bwd")
    dz = jnp.concatenate([dq, dk, dv, dzu, dzv], axis=1)
    early_g = _pack([db_f[:, :n_heads], dlg, dlb, dw_s, db_col.T, dag, dgg, dg_ffn, dg_final])
    dwt_main, (g_ff1, g_out, early_all) = _matmul(dz, h, name="in_proj_dw", out_dtype=BF16, trans_a=True, tk=seq,
                                                  phases=[_join_halves([g_ff1, g_out]), _gather_small(early_g)])
    dwt_f, _ = _matmul(dzf, h, name="gate_dw", out_dtype=BF16, trans_a=True)
    bwd_segments = []
    for j in range(N_CHIPS):
        first = j * shard_cols
        if j < gate_slot:
            bwd_segments.append([(0, shard_cols, 0, 0, first)])
        elif j > gate_slot:
            bwd_segments.append([(0, shard_cols, 0, 0, first - n_heads)])
        else:
            bwd_segments.append([(0, gate_row, 0, 0, first), (gate_row, gate_row + n_heads, 1, 0, 0),
                                 (gate_row + n_heads, shard_cols, 2, 0, qkv)])
    dw_in = _realign_rows([dwt_main[None], dwt_f[None], dwt_main[None]], bwd_segments,
                          (N_CHIPS, shard_cols, d_model), name="dw_in_rows")
    dh_gate, (r_in,) = _matmul(dzf, wt_f, name="gate_dlhs", out_dtype=F32, phases=[_swap_halves([dw_in])])
    ps_in = pair_sum(dw_in, r_in, "w_in")
    dh, (q_in,) = _matmul(dz, wt_main, name="in_proj_dlhs", out_dtype=F32, residual=dh_gate, tk=2560,
                          phases=[_send_partials([ps_in])])
    g_in_sum = chip_sum(ps_in, q_in, "w_in")
    (grad_x, dg_mix), _ = _rmsnorm_bwd(dh, xs, dx1, norm_mix_g, name="norm_mix_bwd")
    late_g = _pack([dg_mix])
    g_in_sum, late_all = _exchange([_join_halves([g_in_sum]), _gather_small(late_g)], name="grads_join_w_in")

    big = {}
    for nm, g, w, m, v in zip(names, (g_in_sum, g_out, g_ff1, g_ff2), (wt_in, w_out[0], w_ff1[0], w_ff2[0]),
                              (mt_in, m_w_out[0], m_w_ff1[0], m_w_ff2[0]), (vt_in, v_w_out[0], v_w_ff1[0], v_w_ff2[0])):
        big[nm] = tuple((t.T if nm == "w_in" else t)[None] for t in _adamw(w, g, m, v, name="adamw_" + nm))

    small_params = dict(
        norm_mix_g=(norm_mix_g, m_norm_mix_g, v_norm_mix_g), b_f=(b_f, m_b_f, v_b_f),
        gmlp_ln_g=(gmlp_ln_g, m_gmlp_ln_g, v_gmlp_ln_g), gmlp_ln_b=(gmlp_ln_b, m_gmlp_ln_b, v_gmlp_ln_b),
        w_s=(w_s, m_w_s, v_w_s), b_s=(b_s, m_b_s, v_b_s), attn_out_g=(attn_out_g, m_attn_out_g, v_attn_out_g),
        gmlp_out_g=(gmlp_out_g, m_gmlp_out_g, v_gmlp_out_g), norm_ffn_g=(norm_ffn_g, m_norm_ffn_g, v_norm_ffn_g),
        norm_final_g=(norm_final_g, m_norm_final_g, v_norm_final_g))

    def small_step(group, grads_all, grads_own, label):
        w, m, v = ([small_params[nm][k] for nm in group] for k in range(3))
        packed = _adamw_small(grads_all, grads_own, place, _pack(w), _pack(m), _pack(v), name="adamw_small_" + label)
        parts = [_unpack(p, [a.shape for a in w]) for p in packed]
        return {nm: tuple(part[i] for part in parts) for i, nm in enumerate(group)}

    early = ["b_f", "gmlp_ln_g", "gmlp_ln_b", "w_s", "b_s", "attn_out_g", "gmlp_out_g", "norm_ffn_g", "norm_final_g"]
    small = {**small_step(early, early_all, early_g, "early"), **small_step(["norm_mix_g"], late_all, late_g, "late")}

    order = ["norm_mix_g", "w_in", "b_f", "gmlp_ln_g", "gmlp_ln_b", "w_s", "b_s", "attn_out_g", "gmlp_out_g", "w_out",
             "norm_ffn_g", "w_ff1", "w_ff2", "norm_final_g"]
    result = {**small, **big}
    total_loss = lax.psum(loss[0, 0], ("x", "y", "c"))
    outs = [total_loss, grad_x.reshape(x.shape)]
    for part in range(4):
        outs += [result[nm][part] for nm in order]
    return tuple(outs)
```

```python
import functools
import math

import numpy as np
import jax
import jax.numpy as jnp
from jax import lax
from jax.experimental import pallas as pl
from jax.experimental.pallas import tpu as pltpu

HEAD_DIM = 128
CHUNK = 128
EPS = 1e-6
LANES = 128
MXU_COLUMNS = 256
N_CHIPS = 4
N_DEV = 8
VMEM_LIMIT_BYTES = 56 * 1024 * 1024

ADAM_LR = 0.001
ADAM_B1 = 0.9
ADAM_B2 = 0.999
ADAM_EPS = 1e-08
ADAM_WD = 0.01
ADAM_STEP = 10

BF16 = jnp.bfloat16
F32 = jnp.float32
MESH = pl.DeviceIdType.MESH
ANY = pl.BlockSpec(memory_space=pl.ANY)
NEG_BIG = -1e30


def _params(*sem):
    return pltpu.CompilerParams(dimension_semantics=tuple(sem), vmem_limit_bytes=VMEM_LIMIT_BYTES)


def _tile(n, pref, unit):
    t = (min(pref, n) // unit) * unit
    while t >= unit:
        if n % t == 0:
            return t
        t -= unit
    return n


def _rc_tile(rows, cols, pref_rows=256, pref_cols=256):
    if rows % 16 == 0:
        return _tile(rows, pref_rows, 16), cols
    return rows, _tile(cols, pref_cols, LANES)


class _Phase:
    def __init__(self, arrays, out_shapes, in_place, n_sems, start, finish):
        self.arrays, self.out_shapes, self.in_place = list(arrays), list(out_shapes), in_place
        self.n_sems, self.start, self.finish = n_sems, start, finish

    @property
    def n_out(self):
        return len(self.arrays) if self.in_place else len(self.out_shapes)


def _run_phases(phases, steps, comm_in, comm_out, send_sems, recv_sems):
    at_in = at_out = at_sem = 0
    for ph in phases:
        for step in steps:
            getattr(ph, step)(comm_in[at_in:at_in + len(ph.arrays)], comm_out[at_out:at_out + ph.n_out],
                              lambda i, base=at_sem: send_sems.at[base + i], lambda i, base=at_sem: recv_sems.at[base + i])
        at_in, at_out, at_sem = at_in + len(ph.arrays), at_out + ph.n_out, at_sem + ph.n_sems


def _call(body, *, name, grid, in_specs, out_specs, out_shape, operands, semantics, scratch_shapes=(),
          n_prefetch=0, phases=()):
    in_specs, out_specs, out_shape = list(in_specs), list(out_specs), list(out_shape)
    scratch_shapes = list(scratch_shapes)
    n_in, n_out, n_scr = len(operands) - n_prefetch, len(out_shape), len(scratch_shapes)
    comm_in = [a for ph in phases for a in ph.arrays]
    comm_out = [jax.ShapeDtypeStruct(s.shape, s.dtype) for ph in phases
                for s in (ph.arrays if ph.in_place else ph.out_shapes)]
    aliases, at_in, at_out = {}, n_prefetch + n_in, n_out
    for ph in phases:
        if ph.in_place:
            aliases.update({at_in + r: at_out + r for r in range(len(ph.arrays))})
        at_in, at_out = at_in + len(ph.arrays), at_out + ph.n_out
    n_sems = sum(ph.n_sems for ph in phases)

    def hosted(*refs):
        pre, rest = refs[:n_prefetch], refs[n_prefetch:]
        ins, rest = rest[:n_in], rest[n_in:]
        cin, rest = rest[:len(comm_in)], rest[len(comm_in):]
        outs, rest = rest[:n_out], rest[n_out:]
        cout, rest = rest[:len(comm_out)], rest[len(comm_out):]
        scr = rest[:n_scr]
        if phases:
            send_sems, recv_sems = rest[n_scr:]
            ids = [pl.program_id(ax) for ax in range(len(grid))]
            first = functools.reduce(jnp.logical_and, [i == 0 for i in ids])
            last = functools.reduce(jnp.logical_and, [i == g - 1 for i, g in zip(ids, grid)])

            @pl.when(first)
            def _():
                _run_phases(phases, ("start",), cin, cout, send_sems, recv_sems)

        body(*pre, *ins, *outs, *scr)
        if phases:
            @pl.when(last)
            def _():
                _run_phases(phases, ("finish",), cin, cout, send_sems, recv_sems)

    all_in = in_specs + [ANY] * len(comm_in)
    all_out = out_specs + [ANY] * len(comm_out)
    all_scr = scratch_shapes + ([pltpu.SemaphoreType.DMA((n_sems,)), pltpu.SemaphoreType.DMA((n_sems,))] if phases else [])
    if phases:
        semantics = ("arbitrary",) * len(grid)
    kwargs = dict(name=name, out_shape=tuple(out_shape + comm_out), compiler_params=_params(*semantics),
                  input_output_aliases=aliases)
    if n_prefetch:
        kwargs["grid_spec"] = pltpu.PrefetchScalarGridSpec(
            num_scalar_prefetch=n_prefetch, grid=grid, in_specs=all_in, out_specs=tuple(all_out), scratch_shapes=all_scr)
    else:
        kwargs.update(grid=grid, in_specs=all_in, out_specs=tuple(all_out), scratch_shapes=all_scr)
    res = pl.pallas_call(hosted, **kwargs)(*operands, *comm_in)
    return tuple(res[:n_out]), tuple(res[n_out:])


def _only(results):
    outs, comm = results
    return outs[0] if len(outs) == 1 else outs, comm


def _matmul(a, b, *, name, out_dtype, trans_a=False, trans_b=False, tm=1024, tn=1024, tk=2048,
            square_lhs=False, relu=False, residual=None, scale2_by=None,
            b_sharded=False, out_sharded=False, phases=()):
    m, k = (a.shape[1], a.shape[0]) if trans_a else a.shape
    if b_sharded:
        if trans_b:
            n, ks = b.shape[1], b.shape[2]
            assert N_CHIPS * ks == k
        else:
            ns = b.shape[2]
            n = N_CHIPS * ns
            assert b.shape[1] == k
    else:
        n = b.shape[0] if trans_b else b.shape[1]
        assert (b.shape[1] if trans_b else b.shape[0]) == k
    tm = _tile(m, tm, 128)
    tn = _tile(n // N_CHIPS if (out_sharded or (b_sharded and not trans_b)) else n, tn, 128)
    tk = _tile(k // N_CHIPS if (b_sharded and trans_b) else k, tk, 128)
    nk = k // tk

    if trans_a:
        a_spec = pl.BlockSpec((tk, tm), lambda i, j, kk: (kk, i))
    else:
        a_spec = pl.BlockSpec((tm, tk), lambda i, j, kk: (i, kk))
    if b_sharded and trans_b:
        per = ks // tk
        assert per * tk == ks
        b_spec = pl.BlockSpec((None, tn, tk), lambda i, j, kk: (kk // per, j, kk % per))
    elif b_sharded:
        per = ns // tn
        assert per * tn == ns
        b_spec = pl.BlockSpec((None, tk, tn), lambda i, j, kk: (j // per, kk, j % per))
    elif trans_b:
        b_spec = pl.BlockSpec((tn, tk), lambda i, j, kk: (j, kk))
    else:
        b_spec = pl.BlockSpec((tk, tn), lambda i, j, kk: (kk, j))
    if out_sharded:
        ns_out = n // N_CHIPS
        per_o = ns_out // tn
        assert per_o * tn == ns_out
        out_shape = jax.ShapeDtypeStruct((N_CHIPS, m, ns_out), out_dtype)
        o_spec = pl.BlockSpec((None, tm, tn), lambda i, j, kk: (j // per_o, i, j % per_o))
    else:
        out_shape = jax.ShapeDtypeStruct((m, n), out_dtype)
        o_spec = pl.BlockSpec((tm, tn), lambda i, j, kk: (i, j))
    mn_spec = pl.BlockSpec((tm, tn), lambda i, j, kk: (i, j))

    operands, in_specs = [a, b], [a_spec, b_spec]
    if scale2_by is not None:
        operands.append(scale2_by)
        in_specs.append(mn_spec)
    if residual is not None:
        operands.append(residual)
        in_specs.append(mn_spec)
    dims = (((0 if trans_a else 1,), (1 if trans_b else 0,)), ((), ()))
    chunk = MXU_COLUMNS if tn % MXU_COLUMNS == 0 else tn

    def body(*refs):
        a_ref, b_ref = refs[0], refs[1]
        pos = 2
        scale_ref = res_ref = None
        if scale2_by is not None:
            scale_ref = refs[pos]
            pos += 1
        if residual is not None:
            res_ref = refs[pos]
            pos += 1
        o_ref = refs[pos]
        kk = pl.program_id(2)

        av = a_ref[...]
        if square_lhs:
            av = av.astype(F32)
            av = av * av
        av = av.astype(BF16)

        def finish(r, cols):
            if relu:
                r = jnp.maximum(r, 0.0)
            if scale_ref is not None:
                r = r * (2.0 * scale_ref[:, cols].astype(F32))
            if res_ref is not None:
                r = r + res_ref[:, cols].astype(F32)
            o_ref[:, cols] = r.astype(out_dtype)

        if nk == 1:
            for lo in range(0, tn, chunk):
                cols = slice(lo, lo + chunk)
                bv = (b_ref[cols, :] if trans_b else b_ref[:, cols]).astype(BF16)
                finish(lax.dot_general(av, bv, dims, preferred_element_type=F32), cols)
        else:
            acc_ref = refs[pos + 1]
            part = lax.dot_general(av, b_ref[...].astype(BF16), dims, preferred_element_type=F32)

            @pl.when(kk == 0)
            def _():
                acc_ref[...] = part

            @pl.when(jnp.logical_and(kk > 0, kk < nk - 1))
            def _():
                acc_ref[...] += part

            @pl.when(kk == nk - 1)
            def _():
                finish(acc_ref[...] + part, slice(None))

    return _only(_call(
        body, name=name, out_shape=[out_shape], grid=(m // tm, n // tn, nk),
        in_specs=in_specs, out_specs=[o_spec], operands=operands,
        scratch_shapes=[pltpu.VMEM((tm, tn), F32)] if nk > 1 else [],
        semantics=("parallel", "parallel", "arbitrary"), phases=phases))


def _rmsnorm_fwd(x, g, *, name, tr=512, phases=()):
    s, d = x.shape
    tr = _tile(s, tr, 8)

    def body(x_ref, g_ref, o_ref):
        xv = x_ref[...]
        r = lax.rsqrt(jnp.mean(xv * xv, axis=-1, keepdims=True) + EPS)
        o_ref[...] = ((xv * r) * g_ref[...]).astype(BF16)

    return _only(_call(
        body, name=name, out_shape=[jax.ShapeDtypeStruct((s, d), BF16)], grid=(s // tr,),
        in_specs=[pl.BlockSpec((tr, d), lambda i: (i, 0)), pl.BlockSpec((1, d), lambda i: (0, 0))],
        out_specs=[pl.BlockSpec((tr, d), lambda i: (i, 0))], operands=[x, g],
        semantics=("parallel",), phases=phases))


def _rms_bwd_rows(dy, xv, g):
    d = xv.shape[-1]
    r = lax.rsqrt(jnp.mean(xv * xv, axis=-1, keepdims=True) + EPS)
    gdy = dy * g
    dot = jnp.sum(gdy * xv, axis=-1, keepdims=True)
    dx = gdy * r - xv * (r * r * r) * (dot / d)
    return dx, dy * (xv * r)


def _rmsnorm_bwd(dy, x, res, g, *, name, tr=256, rounded_copy=False, phases=()):
    s, d = x.shape
    tr = _tile(s, tr, 8)

    def body(dy_ref, x_ref, res_ref, g_ref, dx_ref, dg_ref, *dxb_ref):
        @pl.when(pl.program_id(0) == 0)
        def _():
            dg_ref[...] = jnp.zeros_like(dg_ref)

        dx, dg_rows = _rms_bwd_rows(dy_ref[...].astype(F32), x_ref[...], g_ref[...])
        out = res_ref[...] + dx
        dx_ref[...] = out
        if rounded_copy:
            dxb_ref[0][...] = out.astype(BF16)
        dg_ref[...] += jnp.sum(dg_rows, axis=0, keepdims=True)

    row = pl.BlockSpec((tr, d), lambda i: (i, 0))
    vec = pl.BlockSpec((1, d), lambda i: (0, 0))
    extra = [jax.ShapeDtypeStruct((s, d), BF16)] if rounded_copy else []
    return _call(
        body, name=name,
        out_shape=[jax.ShapeDtypeStruct((s, d), F32), jax.ShapeDtypeStruct((1, d), F32)] + extra,
        grid=(s // tr,), in_specs=[row, row, row, vec], out_specs=[row, vec] + [row] * len(extra),
        operands=[dy, x, res, g], semantics=("arbitrary",), phases=phases)


def _loss_and_final_bwd(x2, target, g, *, name, tr=256):
    s, d = x2.shape
    tr = _tile(s, tr, 8)

    def body(x_ref, t_ref, g_ref, dx_ref, dxb_ref, dg_ref, loss_ref):
        @pl.when(pl.program_id(0) == 0)
        def _():
            dg_ref[...] = jnp.zeros_like(dg_ref)
            loss_ref[...] = jnp.zeros_like(loss_ref)

        xv, gv = x_ref[...], g_ref[...]
        r = lax.rsqrt(jnp.mean(xv * xv, axis=-1, keepdims=True) + EPS)
        err = (xv * r) * gv - t_ref[...]
        row_loss = jnp.mean(err * err, axis=-1, keepdims=True)
        loss_ref[...] += 0.5 * jnp.sum(row_loss, axis=0, keepdims=True)
        dx, dg_rows = _rms_bwd_rows(err / d, xv, gv)
        dx_ref[...] = dx
        dxb_ref[...] = dx.astype(BF16)
        dg_ref[...] += jnp.sum(dg_rows, axis=0, keepdims=True)

    row = pl.BlockSpec((tr, d), lambda i: (i, 0))
    vec = pl.BlockSpec((1, d), lambda i: (0, 0))
    one = pl.BlockSpec((1, 1), lambda i: (0, 0))
    return pl.pallas_call(
        body, name=name,
        out_shape=(jax.ShapeDtypeStruct((s, d), F32), jax.ShapeDtypeStruct((s, d), BF16),
                   jax.ShapeDtypeStruct((1, d), F32), jax.ShapeDtypeStruct((1, 1), F32)),
        grid=(s // tr,), in_specs=[row, row, vec], out_specs=(row, row, vec, one),
        compiler_params=_params("arbitrary"),
    )(x2, target, g)


def _tri_ones(n, lower):
    r = lax.broadcasted_iota(jnp.int32, (n, n), 0)
    c = lax.broadcasted_iota(jnp.int32, (n, n), 1)
    return jnp.where((c <= r) if lower else (c >= r), 1.0, 0.0).astype(F32)


def _forget_fwd(h, w_f, b_f, *, name, tr=256):
    s, d = h.shape
    tr = _tile(s, tr, 8)

    def body(h_ref, w_ref, b_ref, zb_ref, f_ref, carry):
        @pl.when(pl.program_id(0) == 0)
        def _():
            carry[...] = jnp.zeros_like(carry)

        zb = lax.dot_general(h_ref[...], w_ref[...], (((1,), (1,)), ((), ())), preferred_element_type=F32) + b_ref[...]
        zb_ref[...] = zb
        log_f = jnp.minimum(zb, 0.0) - jnp.log(1.0 + jnp.exp(-jnp.abs(zb)))
        run = jnp.dot(_tri_ones(tr, True), log_f, preferred_element_type=F32,
                      precision=lax.Precision.HIGHEST) + carry[...]
        f_ref[...] = run
        carry[...] = run[tr - 1:tr, :]

    row = pl.BlockSpec((tr, LANES), lambda i: (i, 0))
    return pl.pallas_call(
        body, name=name,
        out_shape=(jax.ShapeDtypeStruct((s, LANES), F32), jax.ShapeDtypeStruct((s, LANES), F32)),
        grid=(s // tr,),
        in_specs=[pl.BlockSpec((tr, d), lambda i: (i, 0)), pl.BlockSpec((LANES, d), lambda i: (0, 0)),
                  pl.BlockSpec((1, LANES), lambda i: (0, 0))],
        out_specs=(row, row), scratch_shapes=[pltpu.VMEM((1, LANES), F32)],
        compiler_params=_params("arbitrary"),
    )(h, w_f, b_f)


def _forget_bwd(d_f, zb, *, name, tr=256):
    s = zb.shape[0]
    tr = _tile(s, tr, 8)
    nb = s // tr

    def body(df_ref, zb_ref, dz_ref, db_ref, carry):
        @pl.when(pl.program_id(0) == 0)
        def _():
            carry[...] = jnp.zeros_like(carry)
            db_ref[...] = jnp.zeros_like(db_ref)

        run = jnp.dot(_tri_ones(tr, False), df_ref[...], preferred_element_type=F32,
                      precision=lax.Precision.HIGHEST) + carry[...]
        carry[...] = run[0:1, :]
        dz = run / (1.0 + jnp.exp(zb_ref[...]))
        dz_ref[...] = dz.astype(BF16)
        db_ref[...] += jnp.sum(dz, axis=0, keepdims=True)

    row = pl.BlockSpec((tr, LANES), lambda i: (nb - 1 - i, 0))
    return pl.pallas_call(
        body, name=name,
        out_shape=(jax.ShapeDtypeStruct((s, LANES), BF16), jax.ShapeDtypeStruct((1, LANES), F32)),
        grid=(nb,), in_specs=[row, row], out_specs=(row, pl.BlockSpec((1, LANES), lambda i: (0, 0))),
        scratch_shapes=[pltpu.VMEM((1, LANES), F32)],
        compiler_params=_params("arbitrary"),
    )(d_f, zb)


def _pairs(nblk, by_kv):
    if by_kv:
        pr = [(i, j) for j in range(nblk) for i in range(j, nblk)]
    else:
        pr = [(i, j) for i in range(nblk) for j in range(i + 1)]
    return (jnp.asarray(np.array([p[0] for p in pr], np.int32)), jnp.asarray(np.array([p[1] for p in pr], np.int32)))


def _causal_mask(rows, keys):
    r = lax.broadcasted_iota(jnp.int32, (rows[1] - rows[0], keys[1] - keys[0]), 0) + rows[0]
    c = lax.broadcasted_iota(jnp.int32, (rows[1] - rows[0], keys[1] - keys[0]), 1) + keys[0]
    return c <= r


def _diagonal_pieces(tb):
    half = tb // 2
    if half % LANES:
        return [((0, tb), (0, tb))]
    return [((0, half), (0, half)), ((half, tb), (0, tb))]


LOG2E = math.log2(math.e)
QK_TO_LOG2 = LOG2E / math.sqrt(HEAD_DIM)


def _attn_logits2(q, k, fk_row):
    sc = lax.dot_general(q, k, (((1,), (1,)), ((), ())), preferred_element_type=F32)
    return sc * QK_TO_LOG2 - fk_row * LOG2E


def _attn_fwd(z, f_row, n_heads, *, name, tb=1024, per_step=2, phases=()):
    s = z.shape[0]
    tb = _tile(s, tb, 128)
    nblk = s // tb
    qi, kj = _pairs(nblk, by_kv=False)

    def body(qi_ref, kj_ref, q_ref, k_ref, v_ref, fk_ref, o_ref, lse_ref, m_sc, l_sc, acc_sc):
        p = pl.program_id(1)
        i, j = qi_ref[p], kj_ref[p]

        @pl.when(j == 0)
        def _():
            m_sc[...] = jnp.full_like(m_sc, NEG_BIG)
            l_sc[...] = jnp.zeros_like(l_sc)
            acc_sc[...] = jnp.zeros_like(acc_sc)

        def update(head, rows, keys, masked):
            rs, ks, lanes = slice(*rows), slice(*keys), slice(head * HEAD_DIM, (head + 1) * HEAD_DIM)
            s2 = _attn_logits2(q_ref[rs, lanes], k_ref[ks, lanes], fk_ref[head, :, ks])
            if masked:
                s2 = jnp.where(_causal_mask(rows, keys), s2, NEG_BIG)
            m_old = m_sc[head, rs, :]
            m_new = jnp.maximum(m_old, jnp.max(s2, axis=-1, keepdims=True))
            alpha = jnp.exp2(m_old - m_new)
            pv = jnp.exp2(s2 - jnp.tile(m_new, (1, (keys[1] - keys[0]) // LANES)))
            l_sc[head, rs, :] = alpha * l_sc[head, rs, :] + jnp.sum(pv, axis=-1, keepdims=True)
            acc_sc[head, rs, :] = alpha * acc_sc[head, rs, :] + jnp.dot(pv.astype(BF16), v_ref[ks, lanes],
                                                                        preferred_element_type=F32)
            m_sc[head, rs, :] = m_new

        @pl.when(j < i)
        def _():
            for head in range(per_step):
                update(head, (0, tb), (0, tb), False)

        @pl.when(j == i)
        def _():
            for head in range(per_step):
                for rows, keys in _diagonal_pieces(tb):
                    update(head, rows, keys, True)
                o_ref[:, head * HEAD_DIM:(head + 1) * HEAD_DIM] = (acc_sc[head] / l_sc[head]).astype(BF16)
                lse_ref[head] = m_sc[head] + jnp.log2(l_sc[head])

    h = n_heads // per_step
    width = per_step * HEAD_DIM
    return _call(
        body, name=name, n_prefetch=2, grid=(h, int(qi.shape[0])),
        in_specs=[
            pl.BlockSpec((tb, width), lambda hh, p, qi_r, kj_r: (qi_r[p], hh)),
            pl.BlockSpec((tb, width), lambda hh, p, qi_r, kj_r: (kj_r[p], h + hh)),
            pl.BlockSpec((tb, width), lambda hh, p, qi_r, kj_r: (kj_r[p], 2 * h + hh)),
            pl.BlockSpec((per_step, 1, tb), lambda hh, p, qi_r, kj_r: (hh, 0, kj_r[p])),
        ],
        out_specs=[
            pl.BlockSpec((tb, width), lambda hh, p, qi_r, kj_r: (qi_r[p], hh)),
            pl.BlockSpec((per_step, tb, LANES), lambda hh, p, qi_r, kj_r: (hh, qi_r[p], 0)),
        ],
        scratch_shapes=[pltpu.VMEM((per_step, tb, LANES), F32), pltpu.VMEM((per_step, tb, LANES), F32),
                        pltpu.VMEM((per_step, tb, HEAD_DIM), F32)],
        out_shape=[jax.ShapeDtypeStruct((s, n_heads * HEAD_DIM), BF16), jax.ShapeDtypeStruct((n_heads, s, LANES), F32)],
        operands=[qi, kj, z, z, z, f_row], semantics=("parallel", "arbitrary"), phases=phases)


def _attn_bwd(z, o, d_o, lse2, f_row, n_heads, *, name, tb=1024, phases=()):
    s = z.shape[0]
    tb = _tile(s, tb, 128)
    nblk = s // tb
    qi, kj = _pairs(nblk, by_kv=True)
    n_pairs = int(qi.shape[0])
    scale = 1.0 / math.sqrt(HEAD_DIM)
    h = n_heads

    def body(qi_ref, kj_ref, q_ref, k_ref, v_ref, o_ref, do_ref, lse_ref, fk_ref,
             dq_ref, dk_ref, dv_ref, df_ref, dfq_ref, dq_sc, dk_sc, dv_sc, df_sc, dfq_sc):
        p = pl.program_id(1)
        i, j = qi_ref[p], kj_ref[p]

        @pl.when(p == 0)
        def _():
            dq_sc[...] = jnp.zeros_like(dq_sc)
            dfq_sc[...] = jnp.zeros_like(dfq_sc)

        @pl.when(i == j)
        def _():
            dk_sc[...] = jnp.zeros_like(dk_sc)
            dv_sc[...] = jnp.zeros_like(dv_sc)
            df_sc[...] = jnp.zeros_like(df_sc)

        def update(rows, keys, masked):
            rs, ks, n_rows = slice(*rows), slice(*keys), rows[1] - rows[0]
            q, k, v, do = q_ref[rs, :], k_ref[ks, :], v_ref[ks, :], do_ref[rs, :]
            delta = jnp.sum(do.astype(F32) * o_ref[rs, :].astype(F32), axis=-1, keepdims=True)
            pv = jnp.exp2(_attn_logits2(q, k, fk_ref[:, ks]) - jnp.tile(lse_ref[rs, :], (1, (keys[1] - keys[0]) // LANES)))
            if masked:
                pv = jnp.where(_causal_mask(rows, keys), pv, 0.0)
            dp = lax.dot_general(do, v, (((1,), (1,)), ((), ())), preferred_element_type=F32)
            ds = pv * (dp - delta)
            ds_b = ds.astype(BF16)
            dv_sc[ks, :] += lax.dot_general(pv.astype(BF16), do, (((0,), (0,)), ((), ())), preferred_element_type=F32)
            dk_sc[ks, :] += lax.dot_general(ds_b, q, (((0,), (0,)), ((), ())), preferred_element_type=F32)
            at = pl.ds(pl.multiple_of(i * tb + rows[0], LANES), n_rows)
            dq_sc[at, :] += jnp.dot(ds_b, k, preferred_element_type=F32)
            df_sc[:, ks] -= jnp.sum(ds, axis=0, keepdims=True)
            dfq_sc[at, :] += jnp.broadcast_to(jnp.sum(ds, axis=1, keepdims=True), (n_rows, LANES))

        @pl.when(i > j)
        def _():
            update((0, tb), (0, tb), False)

        @pl.when(i == j)
        def _():
            for rows, keys in _diagonal_pieces(tb):
                update(rows, keys, True)

        @pl.when(i == nblk - 1)
        def _():
            dk_ref[...] = (dk_sc[...] * scale).astype(BF16)
            dv_ref[...] = dv_sc[...].astype(BF16)
            df_ref[...] = df_sc[...]

        @pl.when(p == n_pairs - 1)
        def _():
            dq_ref[...] = (dq_sc[...] * scale).astype(BF16)
            dfq_ref[...] = jnp.transpose(dfq_sc[...])[0:1, :]

    qblk = lambda off: pl.BlockSpec((tb, HEAD_DIM), lambda hh, p, qi_r, kj_r: (qi_r[p], off + hh))
    kblk = lambda off: pl.BlockSpec((tb, HEAD_DIM), lambda hh, p, qi_r, kj_r: (kj_r[p], off + hh))
    qrep = pl.BlockSpec((None, tb, LANES), lambda hh, p, qi_r, kj_r: (hh, qi_r[p], 0))
    krow = pl.BlockSpec((None, 1, tb), lambda hh, p, qi_r, kj_r: (hh, 0, kj_r[p]))
    act = jax.ShapeDtypeStruct((s, h * HEAD_DIM), BF16)
    return _call(
        body, name=name, n_prefetch=2, grid=(h, n_pairs),
        in_specs=[qblk(0), kblk(h), kblk(2 * h), qblk(0), qblk(0), qrep, krow],
        out_specs=[
            pl.BlockSpec((s, HEAD_DIM), lambda hh, p, qi_r, kj_r: (0, hh)),
            kblk(0), kblk(0), krow,
            pl.BlockSpec((None, 1, s), lambda hh, p, qi_r, kj_r: (hh, 0, 0)),
        ],
        scratch_shapes=[pltpu.VMEM((s, HEAD_DIM), F32), pltpu.VMEM((tb, HEAD_DIM), F32),
                        pltpu.VMEM((tb, HEAD_DIM), F32), pltpu.VMEM((1, tb), F32), pltpu.VMEM((s, LANES), F32)],
        out_shape=[act, act, act, jax.ShapeDtypeStruct((h, 1, s), F32), jax.ShapeDtypeStruct((h, 1, s), F32)],
        operands=[qi, kj, z, z, z, o, d_o, lse2, f_row], semantics=("parallel", "arbitrary"), phases=phases)


GELU_C = math.sqrt(2.0 / math.pi)
GELU_A = 0.044715


def _gelu(x):
    return 0.5 * x * (1.0 + jnp.tanh(GELU_C * (x + GELU_A * (x * x * x))))


def _gelu_and_grad(x):
    t = jnp.tanh(GELU_C * (x + GELU_A * (x * x * x)))
    y = 0.5 * x * (1.0 + t)
    dy = 0.5 * (1.0 + t) + 0.5 * x * (1.0 - t * t) * (GELU_C * (1.0 + 3.0 * GELU_A * (x * x)))
    return y, dy


def _layernorm_parts(g):
    mu = jnp.mean(g, axis=-1, keepdims=True)
    xc = g - mu
    rs = lax.rsqrt(jnp.mean(xc * xc, axis=-1, keepdims=True) + EPS)
    return xc * rs, rs


def _spatial_mix(w_ref, bcol_ref, vv_b, n_heads, n_chunks):
    tril = _causal_mask((0, CHUNK), (0, CHUNK))
    cols = []
    for hh in range(n_heads):
        wc = jnp.where(tril, w_ref[hh], 0.0).astype(BF16)
        lanes = slice(hh * HEAD_DIM, (hh + 1) * HEAD_DIM)
        rows = [jnp.dot(wc, vv_b[c * CHUNK:(c + 1) * CHUNK, lanes], preferred_element_type=F32)
                + bcol_ref[:, hh:hh + 1] for c in range(n_chunks)]
        cols.append(jnp.concatenate(rows, axis=0))
    return jnp.concatenate(cols, axis=1)


def _mix_fwd(z, o, ln_g, ln_b, w_s, b_col, attn_g, gm_g, n_heads, *, name, tr=256):
    s = z.shape[0]
    dg = n_heads * HEAD_DIM
    tr = _tile(s, tr, CHUNK)
    n_chunks = tr // CHUNK

    def body(zu_ref, zv_ref, o_ref, lg_ref, lb_ref, w_ref, bcol_ref, ag_ref, gg_ref, out_ref):
        u = _gelu(zu_ref[...].astype(F32))
        xhat, _ = _layernorm_parts(_gelu(zv_ref[...].astype(F32)))
        vv = xhat * lg_ref[...] + lb_ref[...]
        gm = u * _spatial_mix(w_ref, bcol_ref, vv.astype(BF16), n_heads, n_chunks)
        rg = lax.rsqrt(jnp.mean(gm * gm, axis=-1, keepdims=True) + EPS)
        ov = o_ref[...].astype(F32)
        ra = lax.rsqrt(jnp.mean(ov * ov, axis=-1, keepdims=True) + EPS)
        out_ref[:, :dg] = ((ov * ra) * ag_ref[...]).astype(BF16)
        out_ref[:, dg:] = ((gm * rg) * gg_ref[...]).astype(BF16)

    vec = pl.BlockSpec((1, dg), lambda i: (0, 0))
    return pl.pallas_call(
        body, name=name, out_shape=jax.ShapeDtypeStruct((s, 2 * dg), BF16), grid=(s // tr,),
        in_specs=[pl.BlockSpec((tr, dg), lambda i: (i, 3)), pl.BlockSpec((tr, dg), lambda i: (i, 4)),
                  pl.BlockSpec((tr, dg), lambda i: (i, 0)), vec, vec,
                  pl.BlockSpec((n_heads, CHUNK, CHUNK), lambda i: (0, 0, 0)),
                  pl.BlockSpec((CHUNK, n_heads), lambda i: (0, 0)), vec, vec],
        out_specs=pl.BlockSpec((tr, 2 * dg), lambda i: (i, 0)),
        compiler_params=_params("parallel"),
    )(z, z, o, ln_g, ln_b, w_s, b_col, attn_g, gm_g)


def _mix_bwd(z, o, d_merged, ln_g, ln_b, w_s, b_col, attn_g, gm_g, n_heads, *, name, tr=256):
    s = z.shape[0]
    dg = n_heads * HEAD_DIM
    tr = _tile(s, tr, CHUNK)
    n_chunks = tr // CHUNK

    def body(zu_ref, zv_ref, o_ref, dm_ref, lg_ref, lb_ref, w_ref, bcol_ref, ag_ref, gg_ref,
             do_ref, dzu_ref, dzv_ref, dw_ref, dbcol_ref, dlg_ref, dlb_ref, dag_ref, dgg_ref):
        @pl.when(pl.program_id(0) == 0)
        def _():
            for ref in (dw_ref, dbcol_ref, dlg_ref, dlb_ref, dag_ref, dgg_ref):
                ref[...] = jnp.zeros_like(ref)

        d_o, dag_rows = _rms_bwd_rows(dm_ref[:, :dg], o_ref[...].astype(F32), ag_ref[...])
        do_ref[...] = d_o.astype(BF16)
        dag_ref[...] += jnp.sum(dag_rows, axis=0, keepdims=True)

        u, du_dz = _gelu_and_grad(zu_ref[...].astype(F32))
        gv, dgv_dz = _gelu_and_grad(zv_ref[...].astype(F32))
        xhat, rs = _layernorm_parts(gv)
        lg = lg_ref[...]
        vv_b = (xhat * lg + lb_ref[...]).astype(BF16)
        mix = _spatial_mix(w_ref, bcol_ref, vv_b, n_heads, n_chunks)
        gm = u * mix
        d_gm, dgg_rows = _rms_bwd_rows(dm_ref[:, dg:], gm, gg_ref[...])
        dgg_ref[...] += jnp.sum(dgg_rows, axis=0, keepdims=True)
        dzu_ref[...] = ((d_gm * mix) * du_dz).astype(BF16)
        d_mix = d_gm * u
        d_mix_b = d_mix.astype(BF16)

        tril = _causal_mask((0, CHUNK), (0, CHUNK))
        lane = lax.broadcasted_iota(jnp.int32, (CHUNK, n_heads), 1)
        cols = []
        db = jnp.zeros((CHUNK, n_heads), F32)
        for hh in range(n_heads):
            wc = jnp.where(tril, w_ref[hh], 0.0).astype(BF16)
            lanes = slice(hh * HEAD_DIM, (hh + 1) * HEAD_DIM)
            dw = jnp.zeros((CHUNK, CHUNK), F32)
            dmix_sum = jnp.zeros((CHUNK, HEAD_DIM), F32)
            rows = []
            for c in range(n_chunks):
                rws = slice(c * CHUNK, (c + 1) * CHUNK)
                dmb = d_mix_b[rws, lanes]
                dw += lax.dot_general(dmb, vv_b[rws, lanes], (((1,), (1,)), ((), ())), preferred_element_type=F32)
                rows.append(lax.dot_general(wc, dmb, (((0,), (0,)), ((), ())), preferred_element_type=F32))
                dmix_sum += d_mix[rws, lanes]
            dw_ref[hh] += jnp.where(tril, dw, 0.0)
            db += jnp.where(lane == hh, jnp.sum(dmix_sum, axis=-1, keepdims=True), 0.0)
            cols.append(jnp.concatenate(rows, axis=0))
        dbcol_ref[...] += db
        d_vv = jnp.concatenate(cols, axis=1)

        dlg_ref[...] += jnp.sum(d_vv * xhat, axis=0, keepdims=True)
        dlb_ref[...] += jnp.sum(d_vv, axis=0, keepdims=True)
        d_xhat = d_vv * lg
        d_gv = rs * (d_xhat - jnp.mean(d_xhat, axis=-1, keepdims=True)
                     - xhat * jnp.mean(d_xhat * xhat, axis=-1, keepdims=True))
        dzv_ref[...] = (d_gv * dgv_dz).astype(BF16)

    vec = pl.BlockSpec((1, dg), lambda i: (0, 0))
    wspec = pl.BlockSpec((n_heads, CHUNK, CHUNK), lambda i: (0, 0, 0))
    bspec = pl.BlockSpec((CHUNK, n_heads), lambda i: (0, 0))
    rowb = pl.BlockSpec((tr, dg), lambda i: (i, 0))
    act = jax.ShapeDtypeStruct((s, dg), BF16)
    vshape = jax.ShapeDtypeStruct((1, dg), F32)
    return pl.pallas_call(
        body, name=name,
        out_shape=(act, act, act, jax.ShapeDtypeStruct((n_heads, CHUNK, CHUNK), F32),
                   jax.ShapeDtypeStruct((CHUNK, n_heads), F32), vshape, vshape, vshape, vshape),
        grid=(s // tr,),
        in_specs=[pl.BlockSpec((tr, dg), lambda i: (i, 3)), pl.BlockSpec((tr, dg), lambda i: (i, 4)),
                  rowb, pl.BlockSpec((tr, 2 * dg), lambda i: (i, 0)), vec, vec, wspec, bspec, vec, vec],
        out_specs=(rowb, rowb, rowb, wspec, bspec, vec, vec, vec, vec),
        compiler_params=_params("arbitrary"),
    )(z, z, o, d_merged, ln_g, ln_b, w_s, b_col, attn_g, gm_g)


def _place():
    x, y, c = lax.axis_index("x"), lax.axis_index("y"), lax.axis_index("c")
    other_chips = [(1 - x, y), (x, 1 - y), (1 - x, 1 - y)]
    return x, y, c, other_chips


def _remote(src, dst, send_sem, recv_sem, to):
    return pltpu.make_async_remote_copy(src_ref=src, dst_ref=dst, send_sem=send_sem, recv_sem=recv_sem,
                                        device_id=to, device_id_type=MESH)


def _cast_into_slot(w, place, *, name, phases=()):
    rows, cols = w.shape
    tr, tc = _rc_tile(rows, cols)

    def body(place_ref, w_ref, o_ref):
        o_ref[...] = w_ref[...].astype(BF16)

    return _only(_call(
        body, name=name, n_prefetch=1, grid=(rows // tr, cols // tc),
        in_specs=[pl.BlockSpec((tr, tc), lambda i, j, pr: (i, j))],
        out_specs=[pl.BlockSpec((None, tr, tc), lambda i, j, pr: (pr[0], i, j))],
        out_shape=[jax.ShapeDtypeStruct((N_CHIPS, rows, cols), BF16)], operands=[place, w],
        semantics=("parallel", "parallel"), phases=phases))


def _casts_and_norm(weights, place, x, g, *, name, rows=256, phases=()):
    cols = x.shape[1]
    jobs = [w.shape[0] // rows for w in weights] + [x.shape[0] // rows]
    assert all(w.shape[1] == cols and w.shape[0] % rows == 0 for w in weights) and x.shape[0] % rows == 0
    first = [sum(jobs[:k]) for k in range(len(jobs))]

    def strip(k):
        return lambda t: jnp.clip(t - first[k], 0, jobs[k] - 1)

    def body(place_ref, *refs):
        n = len(weights)
        w_refs, x_ref, g_ref, outs = refs[:n], refs[n], refs[n + 1], refs[n + 2:]
        t = pl.program_id(0)
        for k in range(n):
            @pl.when(jnp.logical_and(t >= first[k], t < first[k] + jobs[k]))
            def _(k=k):
                outs[k][...] = w_refs[k][...].astype(BF16)

        @pl.when(t >= first[n])
        def _():
            xv = x_ref[...]
            r = lax.rsqrt(jnp.mean(xv * xv, axis=-1, keepdims=True) + EPS)
            outs[n][...] = ((xv * r) * g_ref[...]).astype(BF16)

    in_specs = [pl.BlockSpec((rows, cols), lambda t, pr, k=k: (strip(k)(t), 0)) for k in range(len(weights))]
    in_specs += [pl.BlockSpec((rows, cols), lambda t, pr: (strip(len(weights))(t), 0)),
                 pl.BlockSpec((1, cols), lambda t, pr: (0, 0))]
    out_specs = [pl.BlockSpec((None, rows, cols), lambda t, pr, k=k: (pr[0], strip(k)(t), 0)) for k in range(len(weights))]
    out_specs.append(pl.BlockSpec((rows, cols), lambda t, pr: (strip(len(weights))(t), 0)))
    out_shape = [jax.ShapeDtypeStruct((N_CHIPS,) + w.shape, BF16) for w in weights] + [jax.ShapeDtypeStruct(x.shape, BF16)]
    return _call(body, name=name, n_prefetch=1, grid=(sum(jobs),), in_specs=in_specs, out_specs=out_specs,
                 out_shape=out_shape, operands=[place, *weights, x, g], semantics=("arbitrary",), phases=phases)


def _exchange(phases, *, name):
    comm_in = [a for ph in phases for a in ph.arrays]
    comm_out = [jax.ShapeDtypeStruct(s.shape, s.dtype) for ph in phases for s in (ph.arrays if ph.in_place else ph.out_shapes)]
    aliases, at_in, at_out = {}, 0, 0
    for ph in phases:
        if ph.in_place:
            aliases.update({at_in + r: at_out + r for r in range(len(ph.arrays))})
        at_in, at_out = at_in + len(ph.arrays), at_out + ph.n_out
    n_sems = sum(ph.n_sems for ph in phases)

    def body(*refs):
        cin, cout = refs[:len(comm_in)], refs[len(comm_in):len(comm_in) + len(comm_out)]
        send_sems, recv_sems = refs[len(comm_in) + len(comm_out):]
        _run_phases(phases, ("start", "finish"), cin, cout, send_sems, recv_sems)

    return pl.pallas_call(
        body, name=name, out_shape=tuple(comm_out), in_specs=[ANY] * len(comm_in), out_specs=tuple([ANY] * len(comm_out)),
        input_output_aliases=aliases,
        scratch_shapes=[pltpu.SemaphoreType.DMA((n_sems,)), pltpu.SemaphoreType.DMA((n_sems,))],
    )(*comm_in)


GATHER_PARTS = 4


def _gather(bufs, stage, part=(0, GATHER_PARTS)):
    n = 3 * len(bufs)
    lo, hi = part

    def copies(outs, send, recv, d2d, incoming):
        x, y, c, chips = _place()
        for t, buf in enumerate(outs):
            half = buf.shape[2] // 2
            piece = half // GATHER_PARTS
            for k, (cx, cy) in enumerate(chips):
                i = 3 * t + k + (n if (d2d and stage == "both") else 0)
                cols = pl.ds(((1 - c) if (d2d and incoming) else c) * half + lo * piece, (hi - lo) * piece)
                blk = buf.at[(2 * cx + cy) if (d2d or incoming) else (2 * x + y), :, cols]
                yield _remote(blk, blk, send(i), recv(i), (x, y, 1 - c) if d2d else (cx, cy, c))

    def start(ins, outs, send, recv):
        for cp in copies(outs, send, recv, stage == "d2d", False):
            cp.start()

    def finish(ins, outs, send, recv):
        if stage == "both":
            for arrival, onward in zip(copies(outs, send, recv, False, True), copies(outs, send, recv, True, False)):
                arrival.wait_recv()
                onward.start()
        for cp in copies(outs, send, recv, stage != "ici", True):
            cp.wait_recv()
        for d2d in ((False, True) if stage == "both" else (stage == "d2d",)):
            for cp in copies(outs, send, recv, d2d, False):
                cp.wait_send()

    return _Phase(bufs, [], True, (2 if stage == "both" else 1) * n, start, finish)


def _merge(first, second):
    n_first = first.n_sems

    def later(sem):
        return lambda i: sem(n_first + i)

    def start(ins, outs, send, recv):
        first.start(ins, outs, send, recv)
        second.start(ins, outs, later(send), later(recv))

    def finish(ins, outs, send, recv):
        first.finish(ins, outs, send, recv)
        second.finish(ins, outs, later(send), later(recv))

    return _Phase(first.arrays, [], True, n_first + second.n_sems, start, finish)


def _gather_by_parts(bufs, lo, hi):
    phase = _gather(bufs, "both", (lo, lo + 1))
    for part in range(lo + 1, hi):
        phase = _merge(phase, _gather(bufs, "both", (part, part + 1)))
    return phase


def _swap_halves(grads):
    def copies(ins, outs, send, recv):
        x, y, c, _ = _place()
        for t, g in enumerate(ins):
            half = g.shape[2] // 2
            yield _remote(g.at[:, :, pl.ds((1 - c) * half, half)], outs[t], send(t), recv(t), (x, y, 1 - c))

    def start(ins, outs, send, recv):
        for cp in copies(ins, outs, send, recv):
            cp.start()

    def finish(ins, outs, send, recv):
        for cp in copies(ins, outs, send, recv):
            cp.wait()

    shapes = [jax.ShapeDtypeStruct((a.shape[0], a.shape[1], a.shape[2] // 2), a.dtype) for a in grads]
    return _Phase(grads, shapes, False, len(grads), start, finish)


def _add_halves(grad, received, place, *, name):
    ns, rows, half = received.shape
    tr, tc = _rc_tile(rows, half, pref_rows=1024)
    per = half // tc

    def body(place_ref, g_ref, r_ref, o_ref):
        o_ref[...] = (g_ref[...].astype(F32) + r_ref[...].astype(F32)).astype(BF16)

    grid_spec = pltpu.PrefetchScalarGridSpec(
        num_scalar_prefetch=1, grid=(ns, rows // tr, per),
        in_specs=[pl.BlockSpec((None, tr, tc), lambda s, i, j, pr: (s, i, pr[1] * per + j)),
                  pl.BlockSpec((None, tr, tc), lambda s, i, j, pr: (s, i, j))],
        out_specs=pl.BlockSpec((None, tr, tc), lambda s, i, j, pr: (s, i, j)),
    )
    return pl.pallas_call(
        body, name=name, grid_spec=grid_spec, out_shape=jax.ShapeDtypeStruct(received.shape, BF16),
        compiler_params=_params("parallel", "parallel", "parallel"),
    )(place, grad, received)


def _send_partials(parts, piece=(0, 1)):
    k_th, n_pieces = piece

    def cols(part):
        width = part.shape[2] // n_pieces
        return pl.ds(k_th * width, width)

    def start(ins, outs, send, recv):
        x, y, c, chips = _place()
        for t, part in enumerate(ins):
            for k, (cx, cy) in enumerate(chips):
                _remote(part.at[2 * cx + cy, :, cols(part)], outs[t].at[2 * x + y],
                        send(3 * t + k), recv(3 * t + k), (cx, cy, c)).start()

    def finish(ins, outs, send, recv):
        x, y, c, chips = _place()
        for t, part in enumerate(ins):
            for k, (cx, cy) in enumerate(chips):
                slot = outs[t].at[2 * cx + cy]
                _remote(slot, slot, send(3 * t + k), recv(3 * t + k), (cx, cy, c)).wait_recv()
        for t, part in enumerate(ins):
            for k, (cx, cy) in enumerate(chips):
                sent = part.at[2 * cx + cy, :, cols(part)]
                _remote(sent, sent, send(3 * t + k), recv(3 * t + k), (cx, cy, c)).wait_send()

    shapes = [jax.ShapeDtypeStruct(a.shape[:2] + (a.shape[2] // n_pieces,), a.dtype) for a in parts]
    return _Phase(parts, shapes, False, 3 * len(parts), start, finish)


def _sum_chips(parts, slots, place, *, name, piece=(0, 1), into=None):
    ns, rows, width = slots.shape
    k_th, n_pieces = piece
    half = width * n_pieces
    tr, tc = _rc_tile(rows, width, pref_rows=512)
    per = width // tc

    def body(place_ref, p_ref, s_ref, *rest):
        acc = p_ref[...].astype(F32)
        for k in range(ns):
            acc = acc + jnp.where(place_ref[0] == k, 0.0, s_ref[k].astype(F32))
        rest[-1][...] = acc

    grid_spec = pltpu.PrefetchScalarGridSpec(
        num_scalar_prefetch=1, grid=(rows // tr, per),
        in_specs=[pl.BlockSpec((None, tr, tc), lambda i, j, pr: (pr[0], i, k_th * per + j)),
                  pl.BlockSpec((ns, tr, tc), lambda i, j, pr: (0, i, j))] + ([ANY] if into is not None else []),
        out_specs=pl.BlockSpec((tr, tc), lambda i, j, pr: (i, (pr[1] * n_pieces + k_th) * per + j)),
    )
    return pl.pallas_call(
        body, name=name, grid_spec=grid_spec, out_shape=jax.ShapeDtypeStruct((rows, 2 * half), F32),
        input_output_aliases={3: 0} if into is not None else {},
        compiler_params=_params("parallel", "parallel"),
    )(place, parts, slots, *([into] if into is not None else []))


def _join_halves(bufs):
    def copies(outs, send, recv, incoming):
        x, y, c, _ = _place()
        for t, buf in enumerate(outs):
            half = buf.shape[1] // 2
            cols = buf.at[:, pl.ds(((1 - c) if incoming else c) * half, half)]
            yield _remote(cols, cols, send(t), recv(t), (x, y, 1 - c))

    def start(ins, outs, send, recv):
        for cp in copies(outs, send, recv, False):
            cp.start()

    def finish(ins, outs, send, recv):
        for cp in copies(outs, send, recv, True):
            cp.wait_recv()
        for cp in copies(outs, send, recv, False):
            cp.wait_send()

    return _Phase(bufs, [], True, len(bufs), start, finish)


def _gather_small(buf):
    def slot(out, px, py, pc):
        return out.at[4 * px + 2 * py + pc]

    def start(ins, outs, send, recv):
        x, y, c, chips = _place()
        mine = slot(outs[0], x, y, c)
        _remote(ins[0], mine, send(0), recv(0), (x, y, 1 - c)).start()
        for k, (cx, cy) in enumerate(chips):
            _remote(ins[0], mine, send(1 + k), recv(1 + k), (cx, cy, c)).start()

    def finish(ins, outs, send, recv):
        x, y, c, chips = _place()
        sibling = (x, y, 1 - c)
        for k, (cx, cy) in enumerate(chips):
            arrived = slot(outs[0], cx, cy, c)
            _remote(arrived, arrived, send(1 + k), recv(1 + k), sibling).wait_recv()
            _remote(arrived, arrived, send(4 + k), recv(4 + k), sibling).start()
        theirs = slot(outs[0], x, y, 1 - c)
        _remote(theirs, theirs, send(0), recv(0), sibling).wait_recv()
        for k, (cx, cy) in enumerate(chips):
            passed = slot(outs[0], cx, cy, 1 - c)
            _remote(passed, passed, send(4 + k), recv(4 + k), sibling).wait_recv()
        for i in range(7):
            _remote(ins[0], ins[0], send(i), recv(i), sibling).wait_send()

    return _Phase([buf], [jax.ShapeDtypeStruct((N_DEV,) + buf.shape, buf.dtype)], False, 7, start, finish)


def _adamw_math(w, g, m, v):
    m = ADAM_B1 * m + (1.0 - ADAM_B1) * g
    v = ADAM_B2 * v + (1.0 - ADAM_B2) * (g * g)
    m_hat = m / (1.0 - ADAM_B1 ** ADAM_STEP)
    v_hat = v / (1.0 - ADAM_B2 ** ADAM_STEP)
    delta = -ADAM_LR * (m_hat / (jnp.sqrt(v_hat) + ADAM_EPS) + ADAM_WD * w)
    return delta, m, v


def _adamw(w, g, m, v, *, name):
    rows, cols = w.shape
    tr, tc = _rc_tile(rows, cols)

    def body(w_ref, g_ref, m_ref, v_ref, go_ref, d_ref, mo_ref, vo_ref):
        g = g_ref[...]
        go_ref[...] = g
        d_ref[...], mo_ref[...], vo_ref[...] = _adamw_math(w_ref[...], g, m_ref[...], v_ref[...])

    blk = pl.BlockSpec((tr, tc), lambda i, j: (i, j))
    shape = jax.ShapeDtypeStruct((rows, cols), F32)
    return pl.pallas_call(
        body, name=name, out_shape=(shape, shape, shape, shape), grid=(rows // tr, cols // tc),
        in_specs=[blk] * 4, out_specs=(blk, blk, blk, blk), compiler_params=_params("parallel", "parallel"),
    )(w, g, m, v)


def _adamw_small(gathered, own, place, w, m, v, *, name):
    nd = gathered.shape[0]

    def body(place_ref, gs_ref, own_ref, w_ref, m_ref, v_ref, g_ref, d_ref, mo_ref, vo_ref):
        me = 2 * place_ref[0] + place_ref[1]
        g = jnp.zeros(own_ref.shape, F32)
        for k in range(nd):
            g = g + jnp.where(me == k, own_ref[...], gs_ref[k])
        g_ref[...] = g
        d_ref[...], mo_ref[...], vo_ref[...] = _adamw_math(w_ref[...], g, m_ref[...], v_ref[...])

    whole = pl.BlockSpec(w.shape, lambda i, pr: (0, 0))
    grid_spec = pltpu.PrefetchScalarGridSpec(
        num_scalar_prefetch=1, grid=(1,),
        in_specs=[pl.BlockSpec(gathered.shape, lambda i, pr: (0, 0, 0)), whole, whole, whole, whole],
        out_specs=(whole, whole, whole, whole))
    shape = jax.ShapeDtypeStruct(w.shape, F32)
    return pl.pallas_call(body, name=name, grid_spec=grid_spec, out_shape=(shape, shape, shape, shape),
                          compiler_params=_params("arbitrary"))(place, gathered, own, w, m, v)


def _pack(parts):
    flat = jnp.concatenate([p.reshape(-1).astype(F32) for p in parts])
    rows = -(-flat.shape[0] // (8 * LANES)) * 8
    return jnp.pad(flat, (0, rows * LANES - flat.shape[0])).reshape(rows, LANES)


def _unpack(buf, shapes):
    flat = buf.reshape(-1)
    out, pos = [], 0
    for shp in shapes:
        size = int(np.prod(shp))
        out.append(flat[pos:pos + size].reshape(shp))
        pos += size
    return out


ROW_BLOCK = 256


def _realign_rows(sources, segments, out_shape, *, name):
    n_slots, rows, cols = out_shape
    n_src = len(sources)
    per_slot = -(-rows // ROW_BLOCK)
    table = np.zeros((6, n_slots * per_slot, n_src), np.int32)
    for so in range(n_slots):
        for first, last, src, src_slot, src_row in segments[so]:
            for blk in range(first // ROW_BLOCK, (last - 1) // ROW_BLOCK + 1):
                lo, hi = max(first, blk * ROW_BLOCK), min(last, (blk + 1) * ROW_BLOCK)
                base = src_row + (blk * ROW_BLOCK - first)
                m0 = (base + lo - blk * ROW_BLOCK) // ROW_BLOCK
                at = so * per_slot + blk
                assert table[4, at, src] == 0, "two segments of one block share a source operand"
                table[:, at, src] = (src_slot, m0, base - m0 * ROW_BLOCK, lo - blk * ROW_BLOCK, hi - blk * ROW_BLOCK,
                                     min(2 * ROW_BLOCK, sources[src].shape[1] - m0 * ROW_BLOCK))
    last_block = [-(-a.shape[1] // ROW_BLOCK) - 1 for a in sources]

    def body(slot_ref, blk_ref, off_ref, lo_ref, hi_ref, valid_ref, *refs):
        o_ref, acc = refs[2 * n_src], refs[2 * n_src + 1]
        at = (pl.program_id(0) * per_slot + pl.program_id(1)) * n_src
        acc[...] = jnp.zeros_like(acc)
        for p in range(n_src):
            @pl.when(hi_ref[at + p] > lo_ref[at + p])
            def _():
                two = jnp.concatenate([refs[2 * p][...], refs[2 * p + 1][...]], axis=0)
                src_row = lax.broadcasted_iota(jnp.int32, two.shape, 0)
                two = jnp.where(src_row < valid_ref[at + p], two, jnp.zeros_like(two))
                r = lax.broadcasted_iota(jnp.int32, (ROW_BLOCK, 2 * ROW_BLOCK), 0)
                c = lax.broadcasted_iota(jnp.int32, (ROW_BLOCK, 2 * ROW_BLOCK), 1)
                place = (c == r + off_ref[at + p]) & (r >= lo_ref[at + p]) & (r < hi_ref[at + p])
                acc[...] += jnp.dot(place.astype(two.dtype), two, preferred_element_type=F32)
        o_ref[...] = acc[...].astype(o_ref.dtype)

    def src_spec(p, second):
        def index(so, i, slot_r, blk_r, off_r, lo_r, hi_r, valid_r):
            at = (so * per_slot + i) * n_src + p
            return slot_r[at], jnp.minimum(blk_r[at] + second, last_block[p]), 0
        return pl.BlockSpec((None, ROW_BLOCK, cols), index)

    grid_spec = pltpu.PrefetchScalarGridSpec(
        num_scalar_prefetch=6, grid=(n_slots, per_slot),
        in_specs=[src_spec(p, second) for p in range(n_src) for second in (0, 1)],
        out_specs=pl.BlockSpec((None, ROW_BLOCK, cols), lambda so, i, *_: (so, i, 0)),
        scratch_shapes=[pltpu.VMEM((ROW_BLOCK, cols), F32)],
    )
    flat = [jnp.asarray(table[k].reshape(-1)) for k in range(6)]
    return pl.pallas_call(
        body, name=name, grid_spec=grid_spec, out_shape=jax.ShapeDtypeStruct(out_shape, sources[0].dtype),
        compiler_params=_params("parallel", "arbitrary"),
    )(*flat, *[a for a in sources for _ in (0, 1)])


def _shard_rows(g, lo, hi):
    rs = g.shape[1]
    pieces = []
    for j in range(g.shape[0]):
        a, b = max(lo, j * rs), min(hi, (j + 1) * rs)
        if a < b:
            pieces.append(g[j, a - j * rs:b - j * rs])
    return pieces


def kernel(x, norm_mix_g, w_in, b_f, gmlp_ln_g, gmlp_ln_b, w_s, b_s, attn_out_g, gmlp_out_g, w_out, norm_ffn_g, w_ff1, w_ff2, norm_final_g, loss_target, m_norm_mix_g, m_w_in, m_b_f, m_gmlp_ln_g, m_gmlp_ln_b, m_w_s, m_b_s, m_attn_out_g, m_gmlp_out_g, m_w_out, m_norm_ffn_g, m_w_ff1, m_w_ff2, m_norm_final_g, v_norm_mix_g, v_w_in, v_b_f, v_gmlp_ln_g, v_gmlp_ln_b, v_w_s, v_b_s, v_attn_out_g, v_gmlp_out_g, v_w_out, v_norm_ffn_g, v_w_ff1, v_w_ff2, v_norm_final_g):
    seq, d_model = x.shape[1], x.shape[2]
    d_attn = d_model // 2
    n_heads = d_attn // HEAD_DIM
    qkv = 3 * d_attn
    shard_cols = w_in.shape[2]
    assert N_CHIPS * shard_cols == qkv + n_heads + 2 * d_attn
    xs = x.reshape(seq, d_model)
    target = loss_target.reshape(seq, d_model)

    place = jnp.stack([2 * lax.axis_index("x") + lax.axis_index("y"), lax.axis_index("c")]).astype(jnp.int32)
    names = ["w_in", "w_out", "w_ff1", "w_ff2"]
    wt_in, mt_in, vt_in = w_in[0].T, m_w_in[0].T, v_w_in[0].T
    b_in, _ = _cast_into_slot(wt_in, place, name="cast_w_in")
    (b_out, b_ff1, b_ff2, h), (g_in,) = _casts_and_norm(
        [w_out[0], w_ff1[0], w_ff2[0]], place, xs, norm_mix_g, name="casts_and_norm_mix",
        phases=[_gather_by_parts([b_in], 0, GATHER_PARTS)])
    n_cols = N_CHIPS * shard_cols
    gate_slot, gate_row = divmod(qkv, shard_cols)
    assert gate_row + n_heads <= shard_cols
    pieces = []
    for j in range(N_CHIPS):
        if j == gate_slot:
            pieces += [(j, 0, gate_row), (j, gate_row + n_heads, shard_cols - gate_row - n_heads)]
        else:
            pieces.append((j, 0, shard_cols))
    fwd_segments, at = [[]], 0
    for order, (j, src_row, size) in enumerate(pieces):
        fwd_segments[0].append((at, at + size, order % 3, j, src_row))
        at += size
    wt_main = _realign_rows([g_in] * 3, fwd_segments, (1, n_cols - n_heads, d_model), name="w_in_rows")[0]
    wt_f = jnp.pad(jnp.concatenate(_shard_rows(g_in, qkv, qkv + n_heads), axis=0), ((0, LANES - n_heads), (0, 0)))
    b_f_pad = jnp.pad(b_f, ((0, 0), (0, LANES - n_heads)))
    b_col = b_s[0].T

    first, rest = (0, 1), (1, GATHER_PARTS)
    z, (b_out, b_ff1) = _matmul(h, wt_main, name="in_proj", out_dtype=BF16, trans_b=True, tm=2048,
                                phases=[_gather([b_out], "ici"), _gather([b_ff1], "ici", first)])
    zb, f_cum = _forget_fwd(h, wt_f, b_f_pad, name="forget_fwd")
    f_row = f_cum[:, :n_heads].T[:, None, :]
    (o, lse2), (b_ff1, b_out) = _attn_fwd(z, f_row, n_heads, name="attn_fwd",
                                          phases=[_gather([b_ff1], "ici", rest), _gather([b_out], "d2d")])
    merged = _mix_fwd(z, o, gmlp_ln_g, gmlp_ln_b, w_s[0], b_col, attn_out_g, gmlp_out_g, n_heads, name="mix_fwd")
    w_out_full = b_out.reshape(2 * d_attn, d_model)
    x1, (b_ff1, b_ff2) = _matmul(merged, w_out_full, name="out_proj", out_dtype=F32, residual=xs,
                                 phases=[_gather([b_ff1], "d2d"), _gather([b_ff2], "ici", first)])
    h2, _ = _rmsnorm_fwd(x1, norm_ffn_g, name="norm_ffn")
    a, (b_ff2,) = _matmul(h2, b_ff1, name="ff1", out_dtype=BF16, relu=True, b_sharded=True, tm=2048,
                          phases=[_merge(_gather([b_ff2], "d2d", first), _gather_by_parts([b_ff2], 1, GATHER_PARTS))])
    w_ff2_full = b_ff2.reshape(N_CHIPS * b_ff2.shape[1], d_model)
    x2, _ = _matmul(a, w_ff2_full, name="ff2", out_dtype=F32, square_lhs=True, residual=x1)
    dx2, dx2_b, dg_final, loss = _loss_and_final_bwd(x2, target, norm_final_g.reshape(1, d_model), name="loss_head")

    def pair_sum(g, r, nm):
        return _add_halves(g, r, place, name="grads_pair_sum_" + nm)

    def chip_sum(p, q, nm, **piece):
        return _sum_chips(p, q, place, name="grads_chip_sum_" + nm, **piece)

    dw_ff2, _ = _matmul(a, dx2_b, name="ff2_dw", out_dtype=BF16, trans_a=True, square_lhs=True)
    dw_ff2 = dw_ff2.reshape(N_CHIPS, -1, d_model)
    da, (r_ff2,) = _matmul(dx2_b, w_ff2_full, name="ff2_dlhs", out_dtype=BF16, trans_b=True, scale2_by=a, tm=2048,
                           phases=[_swap_halves([dw_ff2])])
    ps_ff2 = pair_sum(dw_ff2, r_ff2, "w_ff2")
    dh2, (q_ff2a,) = _matmul(da, b_ff1, name="ff1_dlhs", out_dtype=F32, trans_b=True, b_sharded=True,
                             phases=[_send_partials([ps_ff2], (0, 2))])
    dw_ff1, (q_ff2b,) = _matmul(h2, da, name="ff1_dw", out_dtype=BF16, trans_a=True, out_sharded=True, tk=seq,
                                phases=[_send_partials([ps_ff2], (1, 2))])
    g_ff2 = chip_sum(ps_ff2, q_ff2a, "w_ff2_a", piece=(0, 2))
    g_ff2 = chip_sum(ps_ff2, q_ff2b, "w_ff2_b", piece=(1, 2), into=g_ff2)
    (dx1, dg_ffn, dx1_b), (g_ff2,) = _rmsnorm_bwd(dh2, x1, dx2, norm_ffn_g, name="norm_ffn_bwd", rounded_copy=True,
                                                   phases=[_join_halves([g_ff2])])
    dw_out, _ = _matmul(merged, dx1_b, name="out_proj_dw", out_dtype=BF16, trans_a=True, tk=seq)
    dw_out = dw_out.reshape(N_CHIPS, -1, d_model)
    d_merged, (r_ff1, r_out) = _matmul(dx1_b, w_out_full, name="out_proj_dlhs", out_dtype=F32, trans_b=True,
                                       phases=[_swap_halves([dw_ff1, dw_out])])
    ps_ff1, ps_out = pair_sum(dw_ff1, r_ff1, "w_ff1"), pair_sum(dw_out, r_out, "w_out")
    d_o, dzu, dzv, dw_s, db_col, dlg, dlb, dag, dgg = _mix_bwd(
        z, o, d_merged, gmlp_ln_g, gmlp_ln_b, w_s[0], b_col, attn_out_g, gmlp_out_g, n_heads, name="mix_bwd")
    (dq, dk, dv, d_f_key, d_f_query), (q_ff1, q_out) = _attn_bwd(
        z, o, d_o, lse2, f_row, n_heads, name="attn_bwd", phases=[_send_partials([ps_ff1, ps_out])])
    g_ff1, g_out = chip_sum(ps_ff1, q_ff1, "w_ff1"), chip_sum(ps_out, q_out, "w_out")
    d_f = d_f_key.reshape(n_heads, seq) + d_f_query.reshape(n_heads, seq)
    d_f_pad = jnp.pad(d_f.T, ((0, 0), (0, LANES - n_heads)))
    dzf, db_f = _forget_bwd(d_f_pad, zb, name="forget_bwd")
    dz = jnp.concatenate([dq, dk, dv, dzu, dzv], axis=1)
    early_g = _pack([db_f[:, :n_heads], dlg, dlb, dw_s, db_col.T, dag, dgg, dg_ffn, dg_final])
    dwt_main, (g_ff1, g_out, early_all) = _matmul(dz, h, name="in_proj_dw", out_dtype=BF16, trans_a=True, tk=seq,
                                                  phases=[_join_halves([g_ff1, g_out]), _gather_small(early_g)])
    dwt_f, _ = _matmul(dzf, h, name="gate_dw", out_dtype=BF16, trans_a=True)
    bwd_segments = []
    for j in range(N_CHIPS):
        first = j * shard_cols
        if j < gate_slot:
            bwd_segments.append([(0, shard_cols, 0, 0, first)])
        elif j > gate_slot:
            bwd_segments.append([(0, shard_cols, 0, 0, first - n_heads)])
        else:
            bwd_segments.append([(0, gate_row, 0, 0, first), (gate_row, gate_row + n_heads, 1, 0, 0),
                                 (gate_row + n_heads, shard_cols, 2, 0, qkv)])
    dw_in = _realign_rows([dwt_main[None], dwt_f[None], dwt_main[None]], bwd_segments,
                          (N_CHIPS, shard_cols, d_model), name="dw_in_rows")
    dh_gate, (r_in,) = _matmul(dzf, wt_f, name="gate_dlhs", out_dtype=F32, phases=[_swap_halves([dw_in])])
    ps_in = pair_sum(dw_in, r_in, "w_in")
    dh, (q_in,) = _matmul(dz, wt_main, name="in_proj_dlhs", out_dtype=F32, residual=dh_gate, tk=2560,
                          phases=[_send_partials([ps_in])])
    g_in_sum = chip_sum(ps_in, q_in, "w_in")
    (grad_x, dg_mix), _ = _rmsnorm_bwd(dh, xs, dx1, norm_mix_g, name="norm_mix_bwd")
    late_g = _pack([dg_mix])
    g_in_sum, late_all = _exchange([_join_halves([g_in_sum]), _gather_small(late_g)], name="grads_join_w_in")

    big = {}
    for nm, g, w, m, v in zip(names, (g_in_sum, g_out, g_ff1, g_ff2), (wt_in, w_out[0], w_ff1[0], w_ff2[0]),
                              (mt_in, m_w_out[0], m_w_ff1[0], m_w_ff2[0]), (vt_in, v_w_out[0], v_w_ff1[0], v_w_ff2[0])):
        big[nm] = tuple((t.T if nm == "w_in" else t)[None] for t in _adamw(w, g, m, v, name="adamw_" + nm))

    small_params = dict(
        norm_mix_g=(norm_mix_g, m_norm_mix_g, v_norm_mix_g), b_f=(b_f, m_b_f, v_b_f),
        gmlp_ln_g=(gmlp_ln_g, m_gmlp_ln_g, v_gmlp_ln_g), gmlp_ln_b=(gmlp_ln_b, m_gmlp_ln_b, v_gmlp_ln_b),
        w_s=(w_s, m_w_s, v_w_s), b_s=(b_s, m_b_s, v_b_s), attn_out_g=(attn_out_g, m_attn_out_g, v_attn_out_g),
        gmlp_out_g=(gmlp_out_g, m_gmlp_out_g, v_gmlp_out_g), norm_ffn_g=(norm_ffn_g, m_norm_ffn_g, v_norm_ffn_g),
        norm_final_g=(norm_final_g, m_norm_final_g, v_norm_final_g))

    def small_step(group, grads_all, grads_own, label):
        w, m, v = ([small_params[nm][k] for nm in group] for k in range(3))
        packed = _adamw_small(grads_all, grads_own, place, _pack(w), _pack(m), _pack(v), name="adamw_small_" + label)
        parts = [_unpack(p, [a.shape for a in w]) for p in packed]
        return {nm: tuple(part[i] for part in parts) for i, nm in enumerate(group)}

    early = ["b_f", "gmlp_ln_g", "gmlp_ln_b", "w_s", "b_s", "attn_out_g", "gmlp_out_g", "norm_ffn_g", "norm_final_g"]
    small = {**small_step(early, early_all, early_g, "early"), **small_step(["norm_mix_g"], late_all, late_g, "late")}

    order = ["norm_mix_g", "w_in", "b_f", "gmlp_ln_g", "gmlp_ln_b", "w_s", "b_s", "attn_out_g", "gmlp_out_g", "w_out",
             "norm_ffn_g", "w_ff1", "w_ff2", "norm_final_g"]
    result = {**small, **big}
    total_loss = lax.psum(loss[0, 0], ("x", "y", "c"))
    outs = [total_loss, grad_x.reshape(x.shape)]
    for part in range(4):
        outs += [result[nm][part] for nm in order]
    return tuple(outs)
```

```python
import functools
import math

import numpy as np
import jax
import jax.numpy as jnp
from jax import lax
from jax.experimental import pallas as pl
from jax.experimental.pallas import tpu as pltpu

HEAD_DIM = 128
CHUNK = 128
EPS = 1e-6
LANES = 128
MXU_COLUMNS = 256
N_CHIPS = 4
N_DEV = 8
VMEM_LIMIT_BYTES = 56 * 1024 * 1024

ADAM_LR = 0.001
ADAM_B1 = 0.9
ADAM_B2 = 0.999
ADAM_EPS = 1e-08
ADAM_WD = 0.01
ADAM_STEP = 10

BF16 = jnp.bfloat16
F32 = jnp.float32
MESH = pl.DeviceIdType.MESH
ANY = pl.BlockSpec(memory_space=pl.ANY)
NEG_BIG = -1e30


def _params(*sem):
    return pltpu.CompilerParams(dimension_semantics=tuple(sem), vmem_limit_bytes=VMEM_LIMIT_BYTES)


def _tile(n, pref, unit):
    t = (min(pref, n) // unit) * unit
    while t >= unit:
        if n % t == 0:
            return t
        t -= unit
    return n


def _rc_tile(rows, cols, pref_rows=256, pref_cols=256):
    if rows % 16 == 0:
        return _tile(rows, pref_rows, 16), cols
    return rows, _tile(cols, pref_cols, LANES)


class _Phase:
    def __init__(self, arrays, out_shapes, in_place, n_sems, start, finish):
        self.arrays, self.out_shapes, self.in_place = list(arrays), list(out_shapes), in_place
        self.n_sems, self.start, self.finish = n_sems, start, finish

    @property
    def n_out(self):
        return len(self.arrays) if self.in_place else len(self.out_shapes)


def _run_phases(phases, steps, comm_in, comm_out, send_sems, recv_sems):
    at_in = at_out = at_sem = 0
    for ph in phases:
        for step in steps:
            getattr(ph, step)(comm_in[at_in:at_in + len(ph.arrays)], comm_out[at_out:at_out + ph.n_out],
                              lambda i, base=at_sem: send_sems.at[base + i], lambda i, base=at_sem: recv_sems.at[base + i])
        at_in, at_out, at_sem = at_in + len(ph.arrays), at_out + ph.n_out, at_sem + ph.n_sems


def _call(body, *, name, grid, in_specs, out_specs, out_shape, operands, semantics, scratch_shapes=(),
          n_prefetch=0, phases=()):
    in_specs, out_specs, out_shape = list(in_specs), list(out_specs), list(out_shape)
    scratch_shapes = list(scratch_shapes)
    n_in, n_out, n_scr = len(operands) - n_prefetch, len(out_shape), len(scratch_shapes)
    comm_in = [a for ph in phases for a in ph.arrays]
    comm_out = [jax.ShapeDtypeStruct(s.shape, s.dtype) for ph in phases
                for s in (ph.arrays if ph.in_place else ph.out_shapes)]
    aliases, at_in, at_out = {}, n_prefetch + n_in, n_out
    for ph in phases:
        if ph.in_place:
            aliases.update({at_in + r: at_out + r for r in range(len(ph.arrays))})
        at_in, at_out = at_in + len(ph.arrays), at_out + ph.n_out
    n_sems = sum(ph.n_sems for ph in phases)

    def hosted(*refs):
        pre, rest = refs[:n_prefetch], refs[n_prefetch:]
        ins, rest = rest[:n_in], rest[n_in:]
        cin, rest = rest[:len(comm_in)], rest[len(comm_in):]
        outs, rest = rest[:n_out], rest[n_out:]
        cout, rest = rest[:len(comm_out)], rest[len(comm_out):]
        scr = rest[:n_scr]
        if phases:
            send_sems, recv_sems = rest[n_scr:]
            ids = [pl.program_id(ax) for ax in range(len(grid))]
            first = functools.reduce(jnp.logical_and, [i == 0 for i in ids])
            last = functools.reduce(jnp.logical_and, [i == g - 1 for i, g in zip(ids, grid)])

            @pl.when(first)
            def _():
                _run_phases(phases, ("start",), cin, cout, send_sems, recv_sems)

        body(*pre, *ins, *outs, *scr)
        if phases:
            @pl.when(last)
            def _():
                _run_phases(phases, ("finish",), cin, cout, send_sems, recv_sems)

    all_in = in_specs + [ANY] * len(comm_in)
    all_out = out_specs + [ANY] * len(comm_out)
    all_scr = scratch_shapes + ([pltpu.SemaphoreType.DMA((n_sems,)), pltpu.SemaphoreType.DMA((n_sems,))] if phases else [])
    if phases:
        semantics = ("arbitrary",) * len(grid)
    kwargs = dict(name=name, out_shape=tuple(out_shape + comm_out), compiler_params=_params(*semantics),
                  input_output_aliases=aliases)
    if n_prefetch:
        kwargs["grid_spec"] = pltpu.PrefetchScalarGridSpec(
            num_scalar_prefetch=n_prefetch, grid=grid, in_specs=all_in, out_specs=tuple(all_out), scratch_shapes=all_scr)
    else:
        kwargs.update(grid=grid, in_specs=all_in, out_specs=tuple(all_out), scratch_shapes=all_scr)
    res = pl.pallas_call(hosted, **kwargs)(*operands, *comm_in)
    return tuple(res[:n_out]), tuple(res[n_out:])


def _only(results):
    outs, comm = results
    return outs[0] if len(outs) == 1 else outs, comm


def _matmul(a, b, *, name, out_dtype, trans_a=False, trans_b=False, tm=1024, tn=1024, tk=2048,
            square_lhs=False, relu=False, residual=None, scale2_by=None,
            b_sharded=False, out_sharded=False, phases=()):
    m, k = (a.shape[1], a.shape[0]) if trans_a else a.shape
    if b_sharded:
        if trans_b:
            n, ks = b.shape[1], b.shape[2]
            assert N_CHIPS * ks == k
        else:
            ns = b.shape[2]
            n = N_CHIPS * ns
            assert b.shape[1] == k
    else:
        n = b.shape[0] if trans_b else b.shape[1]
        assert (b.shape[1] if trans_b else b.shape[0]) == k
    tm = _tile(m, tm, 128)
    tn = _tile(n // N_CHIPS if (out_sharded or (b_sharded and not trans_b)) else n, tn, 128)
    tk = _tile(k // N_CHIPS if (b_sharded and trans_b) else k, tk, 128)
    nk = k // tk

    if trans_a:
        a_spec = pl.BlockSpec((tk, tm), lambda i, j, kk: (kk, i))
    else:
        a_spec = pl.BlockSpec((tm, tk), lambda i, j, kk: (i, kk))
    if b_sharded and trans_b:
        per = ks // tk
        assert per * tk == ks
        b_spec = pl.BlockSpec((None, tn, tk), lambda i, j, kk: (kk // per, j, kk % per))
    elif b_sharded:
        per = ns // tn
        assert per * tn == ns
        b_spec = pl.BlockSpec((None, tk, tn), lambda i, j, kk: (j // per, kk, j % per))
    elif trans_b:
        b_spec = pl.BlockSpec((tn, tk), lambda i, j, kk: (j, kk))
    else:
        b_spec = pl.BlockSpec((tk, tn), lambda i, j, kk: (kk, j))
    if out_sharded:
        ns_out = n // N_CHIPS
        per_o = ns_out // tn
        assert per_o * tn == ns_out
        out_shape = jax.ShapeDtypeStruct((N_CHIPS, m, ns_out), out_dtype)
        o_spec = pl.BlockSpec((None, tm, tn), lambda i, j, kk: (j // per_o, i, j % per_o))
    else:
        out_shape = jax.ShapeDtypeStruct((m, n), out_dtype)
        o_spec = pl.BlockSpec((tm, tn), lambda i, j, kk: (i, j))
    mn_spec = pl.BlockSpec((tm, tn), lambda i, j, kk: (i, j))

    operands, in_specs = [a, b], [a_spec, b_spec]
    if scale2_by is not None:
        operands.append(scale2_by)
        in_specs.append(mn_spec)
    if residual is not None:
        operands.append(residual)
        in_specs.append(mn_spec)
    dims = (((0 if trans_a else 1,), (1 if trans_b else 0,)), ((), ()))
    chunk = MXU_COLUMNS if tn % MXU_COLUMNS == 0 else tn

    def body(*refs):
        a_ref, b_ref = refs[0], refs[1]
        pos = 2
        scale_ref = res_ref = None
        if scale2_by is not None:
            scale_ref = refs[pos]
            pos += 1
        if residual is not None:
            res_ref = refs[pos]
            pos += 1
        o_ref = refs[pos]
        kk = pl.program_id(2)

        av = a_ref[...]
        if square_lhs:
            av = av.astype(F32)
            av = av * av
        av = av.astype(BF16)

        def finish(r, cols):
            if relu:
                r = jnp.maximum(r, 0.0)
            if scale_ref is not None:
                r = r * (2.0 * scale_ref[:, cols].astype(F32))
            if res_ref is not None:
                r = r + res_ref[:, cols].astype(F32)
            o_ref[:, cols] = r.astype(out_dtype)

        if nk == 1:
            for lo in range(0, tn, chunk):
                cols = slice(lo, lo + chunk)
                bv = (b_ref[cols, :] if trans_b else b_ref[:, cols]).astype(BF16)
                finish(lax.dot_general(av, bv, dims, preferred_element_type=F32), cols)
        else:
            acc_ref = refs[pos + 1]
            part = lax.dot_general(av, b_ref[...].astype(BF16), dims, preferred_element_type=F32)

            @pl.when(kk == 0)
            def _():
                acc_ref[...] = part

            @pl.when(jnp.logical_and(kk > 0, kk < nk - 1))
            def _():
                acc_ref[...] += part

            @pl.when(kk == nk - 1)
            def _():
                finish(acc_ref[...] + part, slice(None))

    return _only(_call(
        body, name=name, out_shape=[out_shape], grid=(m // tm, n // tn, nk),
        in_specs=in_specs, out_specs=[o_spec], operands=operands,
        scratch_shapes=[pltpu.VMEM((tm, tn), F32)] if nk > 1 else [],
        semantics=("parallel", "parallel", "arbitrary"), phases=phases))


def _rmsnorm_fwd(x, g, *, name, tr=512, phases=()):
    s, d = x.shape
    tr = _tile(s, tr, 8)

    def body(x_ref, g_ref, o_ref):
        xv = x_ref[...]
        r = lax.rsqrt(jnp.mean(xv * xv, axis=-1, keepdims=True) + EPS)
        o_ref[...] = ((xv * r) * g_ref[...]).astype(BF16)

    return _only(_call(
        body, name=name, out_shape=[jax.ShapeDtypeStruct((s, d), BF16)], grid=(s // tr,),
        in_specs=[pl.BlockSpec((tr, d), lambda i: (i, 0)), pl.BlockSpec((1, d), lambda i: (0, 0))],
        out_specs=[pl.BlockSpec((tr, d), lambda i: (i, 0))], operands=[x, g],
        semantics=("parallel",), phases=phases))


def _rms_bwd_rows(dy, xv, g):
    d = xv.shape[-1]
    r = lax.rsqrt(jnp.mean(xv * xv, axis=-1, keepdims=True) + EPS)
    gdy = dy * g
    dot = jnp.sum(gdy * xv, axis=-1, keepdims=True)
    dx = gdy * r - xv * (r * r * r) * (dot / d)
    return dx, dy * (xv * r)


def _rmsnorm_bwd(dy, x, res, g, *, name, tr=256, rounded_copy=False, phases=()):
    s, d = x.shape
    tr = _tile(s, tr, 8)

    def body(dy_ref, x_ref, res_ref, g_ref, dx_ref, dg_ref, *dxb_ref):
        @pl.when(pl.program_id(0) == 0)
        def _():
            dg_ref[...] = jnp.zeros_like(dg_ref)

        dx, dg_rows = _rms_bwd_rows(dy_ref[...].astype(F32), x_ref[...], g_ref[...])
        out = res_ref[...] + dx
        dx_ref[...] = out
        if rounded_copy:
            dxb_ref[0][...] = out.astype(BF16)
        dg_ref[...] += jnp.sum(dg_rows, axis=0, keepdims=True)

    row = pl.BlockSpec((tr, d), lambda i: (i, 0))
    vec = pl.BlockSpec((1, d), lambda i: (0, 0))
    extra = [jax.ShapeDtypeStruct((s, d), BF16)] if rounded_copy else []
    return _call(
        body, name=name,
        out_shape=[jax.ShapeDtypeStruct((s, d), F32), jax.ShapeDtypeStruct((1, d), F32)] + extra,
        grid=(s // tr,), in_specs=[row, row, row, vec], out_specs=[row, vec] + [row] * len(extra),
        operands=[dy, x, res, g], semantics=("arbitrary",), phases=phases)


def _loss_and_final_bwd(x2, target, g, *, name, tr=256):
    s, d = x2.shape
    tr = _tile(s, tr, 8)

    def body(x_ref, t_ref, g_ref, dx_ref, dxb_ref, dg_ref, loss_ref):
        @pl.when(pl.program_id(0) == 0)
        def _():
            dg_ref[...] = jnp.zeros_like(dg_ref)
            loss_ref[...] = jnp.zeros_like(loss_ref)

        xv, gv = x_ref[...], g_ref[...]
        r = lax.rsqrt(jnp.mean(xv * xv, axis=-1, keepdims=True) + EPS)
        err = (xv * r) * gv - t_ref[...]
        row_loss = jnp.mean(err * err, axis=-1, keepdims=True)
        loss_ref[...] += 0.5 * jnp.sum(row_loss, axis=0, keepdims=True)
        dx, dg_rows = _rms_bwd_rows(err / d, xv, gv)
        dx_ref[...] = dx
        dxb_ref[...] = dx.astype(BF16)
        dg_ref[...] += jnp.sum(dg_rows, axis=0, keepdims=True)

    row = pl.BlockSpec((tr, d), lambda i: (i, 0))
    vec = pl.BlockSpec((1, d), lambda i: (0, 0))
    one = pl.BlockSpec((1, 1), lambda i: (0, 0))
    return pl.pallas_call(
        body, name=name,
        out_shape=(jax.ShapeDtypeStruct((s, d), F32), jax.ShapeDtypeStruct((s, d), BF16),
                   jax.ShapeDtypeStruct((1, d), F32), jax.ShapeDtypeStruct((1, 1), F32)),
        grid=(s // tr,), in_specs=[row, row, vec], out_specs=(row, row, vec, one),
        compiler_params=_params("arbitrary"),
    )(x2, target, g)


def _tri_ones(n, lower):
    r = lax.broadcasted_iota(jnp.int32, (n, n), 0)
    c = lax.broadcasted_iota(jnp.int32, (n, n), 1)
    return jnp.where((c <= r) if lower else (c >= r), 1.0, 0.0).astype(F32)


def _forget_fwd(h, w_f, b_f, *, name, tr=256):
    s, d = h.shape
    tr = _tile(s, tr, 8)

    def body(h_ref, w_ref, b_ref, zb_ref, f_ref, carry):
        @pl.when(pl.program_id(0) == 0)
        def _():
            carry[...] = jnp.zeros_like(carry)

        zb = lax.dot_general(h_ref[...], w_ref[...], (((1,), (1,)), ((), ())), preferred_element_type=F32) + b_ref[...]
        zb_ref[...] = zb
        log_f = jnp.minimum(zb, 0.0) - jnp.log(1.0 + jnp.exp(-jnp.abs(zb)))
        run = jnp.dot(_tri_ones(tr, True), log_f, preferred_element_type=F32,
                      precision=lax.Precision.HIGHEST) + carry[...]
        f_ref[...] = run
        carry[...] = run[tr - 1:tr, :]

    row = pl.BlockSpec((tr, LANES), lambda i: (i, 0))
    return pl.pallas_call(
        body, name=name,
        out_shape=(jax.ShapeDtypeStruct((s, LANES), F32), jax.ShapeDtypeStruct((s, LANES), F32)),
        grid=(s // tr,),
        in_specs=[pl.BlockSpec((tr, d), lambda i: (i, 0)), pl.BlockSpec((LANES, d), lambda i: (0, 0)),
                  pl.BlockSpec((1, LANES), lambda i: (0, 0))],
        out_specs=(row, row), scratch_shapes=[pltpu.VMEM((1, LANES), F32)],
        compiler_params=_params("arbitrary"),
    )(h, w_f, b_f)


def _forget_bwd(d_f, zb, *, name, tr=256):
    s = zb.shape[0]
    tr = _tile(s, tr, 8)
    nb = s // tr

    def body(df_ref, zb_ref, dz_ref, db_ref, carry):
        @pl.when(pl.program_id(0) == 0)
        def _():
            carry[...] = jnp.zeros_like(carry)
            db_ref[...] = jnp.zeros_like(db_ref)

        run = jnp.dot(_tri_ones(tr, False), df_ref[...], preferred_element_type=F32,
                      precision=lax.Precision.HIGHEST) + carry[...]
        carry[...] = run[0:1, :]
        dz = run / (1.0 + jnp.exp(zb_ref[...]))
        dz_ref[...] = dz.astype(BF16)
        db_ref[...] += jnp.sum(dz, axis=0, keepdims=True)

    row = pl.BlockSpec((tr, LANES), lambda i: (nb - 1 - i, 0))
    return pl.pallas_call(
        body, name=name,
        out_shape=(jax.ShapeDtypeStruct((s, LANES), BF16), jax.ShapeDtypeStruct((1, LANES), F32)),
        grid=(nb,), in_specs=[row, row], out_specs=(row, pl.BlockSpec((1, LANES), lambda i: (0, 0))),
        scratch_shapes=[pltpu.VMEM((1, LANES), F32)],
        compiler_params=_params("arbitrary"),
    )(d_f, zb)


def _pairs(nblk, by_kv):
    if by_kv:
        pr = [(i, j) for j in range(nblk) for i in range(j, nblk)]
    else:
        pr = [(i, j) for i in range(nblk) for j in range(i + 1)]
    return (jnp.asarray(np.array([p[0] for p in pr], np.int32)), jnp.asarray(np.array([p[1] for p in pr], np.int32)))


def _causal_mask(rows, keys):
    r = lax.broadcasted_iota(jnp.int32, (rows[1] - rows[0], keys[1] - keys[0]), 0) + rows[0]
    c = lax.broadcasted_iota(jnp.int32, (rows[1] - rows[0], keys[1] - keys[0]), 1) + keys[0]
    return c <= r


def _diagonal_pieces(tb):
    half = tb // 2
    if half % LANES:
        return [((0, tb), (0, tb))]
    return [((0, half), (0, half)), ((half, tb), (0, tb))]


LOG2E = math.log2(math.e)
QK_TO_LOG2 = LOG2E / math.sqrt(HEAD_DIM)


def _attn_logits2(q, k, fk_row):
    sc = lax.dot_general(q, k, (((1,), (1,)), ((), ())), preferred_element_type=F32)
    return sc * QK_TO_LOG2 - fk_row * LOG2E


def _attn_fwd(z, f_row, n_heads, *, name, tb=1024, per_step=2, phases=()):
    s = z.shape[0]
    tb = _tile(s, tb, 128)
    nblk = s // tb
    qi, kj = _pairs(nblk, by_kv=False)

    def body(qi_ref, kj_ref, q_ref, k_ref, v_ref, fk_ref, o_ref, lse_ref, m_sc, l_sc, acc_sc):
        p = pl.program_id(1)
        i, j = qi_ref[p], kj_ref[p]

        @pl.when(j == 0)
        def _():
            m_sc[...] = jnp.full_like(m_sc, NEG_BIG)
            l_sc[...] = jnp.zeros_like(l_sc)
            acc_sc[...] = jnp.zeros_like(acc_sc)

        def update(head, rows, keys, masked):
            rs, ks, lanes = slice(*rows), slice(*keys), slice(head * HEAD_DIM, (head + 1) * HEAD_DIM)
            s2 = _attn_logits2(q_ref[rs, lanes], k_ref[ks, lanes], fk_ref[head, :, ks])
            if masked:
                s2 = jnp.where(_causal_mask(rows, keys), s2, NEG_BIG)
            m_old = m_sc[head, rs, :]
            m_new = jnp.maximum(m_old, jnp.max(s2, axis=-1, keepdims=True))
            alpha = jnp.exp2(m_old - m_new)
            pv = jnp.exp2(s2 - jnp.tile(m_new, (1, (keys[1] - keys[0]) // LANES)))
            l_sc[head, rs, :] = alpha * l_sc[head, rs, :] + jnp.sum(pv, axis=-1, keepdims=True)
            acc_sc[head, rs, :] = alpha * acc_sc[head, rs, :] + jnp.dot(pv.astype(BF16), v_ref[ks, lanes],
                                                                        preferred_element_type=F32)
            m_sc[head, rs, :] = m_new

        @pl.when(j < i)
        def _():
            for head in range(per_step):
                update(head, (0, tb), (0, tb), False)

        @pl.when(j == i)
        def _():
            for head in range(per_step):
                for rows, keys in _diagonal_pieces(tb):
                    update(head, rows, keys, True)
                o_ref[:, head * HEAD_DIM:(head + 1) * HEAD_DIM] = (acc_sc[head] / l_sc[head]).astype(BF16)
                lse_ref[head] = m_sc[head] + jnp.log2(l_sc[head])

    h = n_heads // per_step
    width = per_step * HEAD_DIM
    return _call(
        body, name=name, n_prefetch=2, grid=(h, int(qi.shape[0])),
        in_specs=[
            pl.BlockSpec((tb, width), lambda hh, p, qi_r, kj_r: (qi_r[p], hh)),
            pl.BlockSpec((tb, width), lambda hh, p, qi_r, kj_r: (kj_r[p], h + hh)),
            pl.BlockSpec((tb, width), lambda hh, p, qi_r, kj_r: (kj_r[p], 2 * h + hh)),
            pl.BlockSpec((per_step, 1, tb), lambda hh, p, qi_r, kj_r: (hh, 0, kj_r[p])),
        ],
        out_specs=[
            pl.BlockSpec((tb, width), lambda hh, p, qi_r, kj_r: (qi_r[p], hh)),
            pl.BlockSpec((per_step, tb, LANES), lambda hh, p, qi_r, kj_r: (hh, qi_r[p], 0)),
        ],
        scratch_shapes=[pltpu.VMEM((per_step, tb, LANES), F32), pltpu.VMEM((per_step, tb, LANES), F32),
                        pltpu.VMEM((per_step, tb, HEAD_DIM), F32)],
        out_shape=[jax.ShapeDtypeStruct((s, n_heads * HEAD_DIM), BF16), jax.ShapeDtypeStruct((n_heads, s, LANES), F32)],
        operands=[qi, kj, z, z, z, f_row], semantics=("parallel", "arbitrary"), phases=phases)


def _attn_bwd(z, o, d_o, lse2, f_row, n_heads, *, name, tb=1024, per_step=2, phases=()):
    s = z.shape[0]
    tb = _tile(s, tb, 128)
    nblk = s // tb
    qi, kj = _pairs(nblk, by_kv=True)
    n_pairs = int(qi.shape[0])
    scale = 1.0 / math.sqrt(HEAD_DIM)
    h = n_heads // per_step
    width = per_step * HEAD_DIM

    def body(qi_ref, kj_ref, q_ref, k_ref, v_ref, o_ref, do_ref, lse_ref, fk_ref,
             dq_ref, dk_ref, dv_ref, df_ref, dfq_ref, dq_sc, dk_sc, dv_sc, df_sc, dfq_sc):
        p = pl.program_id(1)
        i, j = qi_ref[p], kj_ref[p]

        @pl.when(p == 0)
        def _():
            dq_sc[...] = jnp.zeros_like(dq_sc)
            dfq_sc[...] = jnp.zeros_like(dfq_sc)

        @pl.when(i == j)
        def _():
            dk_sc[...] = jnp.zeros_like(dk_sc)
            dv_sc[...] = jnp.zeros_like(dv_sc)
            df_sc[...] = jnp.zeros_like(df_sc)

        def update(head, rows, keys, masked):
            rs, ks, n_rows = slice(*rows), slice(*keys), rows[1] - rows[0]
            lanes = slice(head * HEAD_DIM, (head + 1) * HEAD_DIM)
            q, k, v, do = q_ref[rs, lanes], k_ref[ks, lanes], v_ref[ks, lanes], do_ref[rs, lanes]
            delta = jnp.sum(do.astype(F32) * o_ref[rs, lanes].astype(F32), axis=-1, keepdims=True)
            pv = jnp.exp2(_attn_logits2(q, k, fk_ref[head, :, ks])
                          - jnp.tile(lse_ref[head, rs, :], (1, (keys[1] - keys[0]) // LANES)))
            if masked:
                pv = jnp.where(_causal_mask(rows, keys), pv, 0.0)
            dp = lax.dot_general(do, v, (((1,), (1,)), ((), ())), preferred_element_type=F32)
            ds = pv * (dp - delta)
            ds_b = ds.astype(BF16)
            dv_sc[head, ks, :] += lax.dot_general(pv.astype(BF16), do, (((0,), (0,)), ((), ())), preferred_element_type=F32)
            dk_sc[head, ks, :] += lax.dot_general(ds_b, q, (((0,), (0,)), ((), ())), preferred_element_type=F32)
            at = pl.ds(pl.multiple_of(i * tb + rows[0], LANES), n_rows)
            dq_sc[head, at, :] += jnp.dot(ds_b, k, preferred_element_type=F32)
            df_sc[head, :, ks] -= jnp.sum(ds, axis=0, keepdims=True)
            dfq_sc[head, at, :] += jnp.broadcast_to(jnp.sum(ds, axis=1, keepdims=True), (n_rows, LANES))

        @pl.when(i > j)
        def _():
            for head in range(per_step):
                update(head, (0, tb), (0, tb), False)

        @pl.when(i == j)
        def _():
            for head in range(per_step):
                for rows, keys in _diagonal_pieces(tb):
                    update(head, rows, keys, True)

        @pl.when(i == nblk - 1)
        def _():
            for head in range(per_step):
                lanes = slice(head * HEAD_DIM, (head + 1) * HEAD_DIM)
                dk_ref[:, lanes] = (dk_sc[head] * scale).astype(BF16)
                dv_ref[:, lanes] = dv_sc[head].astype(BF16)
            df_ref[...] = df_sc[...]

        @pl.when(p == n_pairs - 1)
        def _():
            for head in range(per_step):
                dq_ref[:, head * HEAD_DIM:(head + 1) * HEAD_DIM] = (dq_sc[head] * scale).astype(BF16)
                dfq_ref[head] = jnp.transpose(dfq_sc[head])[0:1, :]

    qblk = lambda off: pl.BlockSpec((tb, width), lambda hh, p, qi_r, kj_r: (qi_r[p], off + hh))
    kblk = lambda off: pl.BlockSpec((tb, width), lambda hh, p, qi_r, kj_r: (kj_r[p], off + hh))
    qrep = pl.BlockSpec((per_step, tb, LANES), lambda hh, p, qi_r, kj_r: (hh, qi_r[p], 0))
    krow = pl.BlockSpec((per_step, 1, tb), lambda hh, p, qi_r, kj_r: (hh, 0, kj_r[p]))
    act = jax.ShapeDtypeStruct((s, n_heads * HEAD_DIM), BF16)
    return _call(
        body, name=name, n_prefetch=2, grid=(h, n_pairs),
        in_specs=[qblk(0), kblk(h), kblk(2 * h), qblk(0), qblk(0), qrep, krow],
        out_specs=[
            pl.BlockSpec((s, width), lambda hh, p, qi_r, kj_r: (0, hh)),
            kblk(0), kblk(0), krow,
            pl.BlockSpec((per_step, 1, s), lambda hh, p, qi_r, kj_r: (hh, 0, 0)),
        ],
        scratch_shapes=[pltpu.VMEM((per_step, s, HEAD_DIM), F32), pltpu.VMEM((per_step, tb, HEAD_DIM), F32),
                        pltpu.VMEM((per_step, tb, HEAD_DIM), F32), pltpu.VMEM((per_step, 1, tb), F32),
                        pltpu.VMEM((per_step, s, LANES), F32)],
        out_shape=[act, act, act, jax.ShapeDtypeStruct((n_heads, 1, s), F32), jax.ShapeDtypeStruct((n_heads, 1, s), F32)],
        operands=[qi, kj, z, z, z, o, d_o, lse2, f_row], semantics=("parallel", "arbitrary"), phases=phases)


GELU_C = math.sqrt(2.0 / math.pi)
GELU_A = 0.044715


def _gelu(x):
    return 0.5 * x * (1.0 + jnp.tanh(GELU_C * (x + GELU_A * (x * x * x))))


def _gelu_and_grad(x):
    t = jnp.tanh(GELU_C * (x + GELU_A * (x * x * x)))
    y = 0.5 * x * (1.0 + t)
    dy = 0.5 * (1.0 + t) + 0.5 * x * (1.0 - t * t) * (GELU_C * (1.0 + 3.0 * GELU_A * (x * x)))
    return y, dy


def _layernorm_parts(g):
    mu = jnp.mean(g, axis=-1, keepdims=True)
    xc = g - mu
    rs = lax.rsqrt(jnp.mean(xc * xc, axis=-1, keepdims=True) + EPS)
    return xc * rs, rs


def _spatial_mix(w_ref, bcol_ref, vv_b, n_heads, n_chunks):
    tril = _causal_mask((0, CHUNK), (0, CHUNK))
    cols = []
    for hh in range(n_heads):
        wc = jnp.where(tril, w_ref[hh], 0.0).astype(BF16)
        lanes = slice(hh * HEAD_DIM, (hh + 1) * HEAD_DIM)
        rows = [jnp.dot(wc, vv_b[c * CHUNK:(c + 1) * CHUNK, lanes], preferred_element_type=F32)
                + bcol_ref[:, hh:hh + 1] for c in range(n_chunks)]
        cols.append(jnp.concatenate(rows, axis=0))
    return jnp.concatenate(cols, axis=1)


def _mix_fwd(z, o, ln_g, ln_b, w_s, b_col, attn_g, gm_g, n_heads, *, name, tr=256):
    s = z.shape[0]
    dg = n_heads * HEAD_DIM
    tr = _tile(s, tr, CHUNK)
    n_chunks = tr // CHUNK

    def body(zu_ref, zv_ref, o_ref, lg_ref, lb_ref, w_ref, bcol_ref, ag_ref, gg_ref, out_ref):
        u = _gelu(zu_ref[...].astype(F32))
        xhat, _ = _layernorm_parts(_gelu(zv_ref[...].astype(F32)))
        vv = xhat * lg_ref[...] + lb_ref[...]
        gm = u * _spatial_mix(w_ref, bcol_ref, vv.astype(BF16), n_heads, n_chunks)
        rg = lax.rsqrt(jnp.mean(gm * gm, axis=-1, keepdims=True) + EPS)
        ov = o_ref[...].astype(F32)
        ra = lax.rsqrt(jnp.mean(ov * ov, axis=-1, keepdims=True) + EPS)
        out_ref[:, :dg] = ((ov * ra) * ag_ref[...]).astype(BF16)
        out_ref[:, dg:] = ((gm * rg) * gg_ref[...]).astype(BF16)

    vec = pl.BlockSpec((1, dg), lambda i: (0, 0))
    return pl.pallas_call(
        body, name=name, out_shape=jax.ShapeDtypeStruct((s, 2 * dg), BF16), grid=(s // tr,),
        in_specs=[pl.BlockSpec((tr, dg), lambda i: (i, 3)), pl.BlockSpec((tr, dg), lambda i: (i, 4)),
                  pl.BlockSpec((tr, dg), lambda i: (i, 0)), vec, vec,
                  pl.BlockSpec((n_heads, CHUNK, CHUNK), lambda i: (0, 0, 0)),
                  pl.BlockSpec((CHUNK, n_heads), lambda i: (0, 0)), vec, vec],
        out_specs=pl.BlockSpec((tr, 2 * dg), lambda i: (i, 0)),
        compiler_params=_params("parallel"),
    )(z, z, o, ln_g, ln_b, w_s, b_col, attn_g, gm_g)


def _mix_bwd(z, o, d_merged, ln_g, ln_b, w_s, b_col, attn_g, gm_g, n_heads, *, name, tr=256):
    s = z.shape[0]
    dg = n_heads * HEAD_DIM
    tr = _tile(s, tr, CHUNK)
    n_chunks = tr // CHUNK

    def body(zu_ref, zv_ref, o_ref, dm_ref, lg_ref, lb_ref, w_ref, bcol_ref, ag_ref, gg_ref,
             do_ref, dzu_ref, dzv_ref, dw_ref, dbcol_ref, dlg_ref, dlb_ref, dag_ref, dgg_ref):
        @pl.when(pl.program_id(0) == 0)
        def _():
            for ref in (dw_ref, dbcol_ref, dlg_ref, dlb_ref, dag_ref, dgg_ref):
                ref[...] = jnp.zeros_like(ref)

        d_o, dag_rows = _rms_bwd_rows(dm_ref[:, :dg], o_ref[...].astype(F32), ag_ref[...])
        do_ref[...] = d_o.astype(BF16)
        dag_ref[...] += jnp.sum(dag_rows, axis=0, keepdims=True)

        u, du_dz = _gelu_and_grad(zu_ref[...].astype(F32))
        gv, dgv_dz = _gelu_and_grad(zv_ref[...].astype(F32))
        xhat, rs = _layernorm_parts(gv)
        lg = lg_ref[...]
        vv_b = (xhat * lg + lb_ref[...]).astype(BF16)
        mix = _spatial_mix(w_ref, bcol_ref, vv_b, n_heads, n_chunks)
        gm = u * mix
        d_gm, dgg_rows = _rms_bwd_rows(dm_ref[:, dg:], gm, gg_ref[...])
        dgg_ref[...] += jnp.sum(dgg_rows, axis=0, keepdims=True)
        dzu_ref[...] = ((d_gm * mix) * du_dz).astype(BF16)
        d_mix = d_gm * u
        d_mix_b = d_mix.astype(BF16)

        tril = _causal_mask((0, CHUNK), (0, CHUNK))
        lane = lax.broadcasted_iota(jnp.int32, (CHUNK, n_heads), 1)
        cols = []
        db = jnp.zeros((CHUNK, n_heads), F32)
        for hh in range(n_heads):
            wc = jnp.where(tril, w_ref[hh], 0.0).astype(BF16)
            lanes = slice(hh * HEAD_DIM, (hh + 1) * HEAD_DIM)
            dw = jnp.zeros((CHUNK, CHUNK), F32)
            dmix_sum = jnp.zeros((CHUNK, HEAD_DIM), F32)
            rows = []
            for c in range(n_chunks):
                rws = slice(c * CHUNK, (c + 1) * CHUNK)
                dmb = d_mix_b[rws, lanes]
                dw += lax.dot_general(dmb, vv_b[rws, lanes], (((1,), (1,)), ((), ())), preferred_element_type=F32)
                rows.append(lax.dot_general(wc, dmb, (((0,), (0,)), ((), ())), preferred_element_type=F32))
                dmix_sum += d_mix[rws, lanes]
            dw_ref[hh] += jnp.where(tril, dw, 0.0)
            db += jnp.where(lane == hh, jnp.sum(dmix_sum, axis=-1, keepdims=True), 0.0)
            cols.append(jnp.concatenate(rows, axis=0))
        dbcol_ref[...] += db
        d_vv = jnp.concatenate(cols, axis=1)

        dlg_ref[...] += jnp.sum(d_vv * xhat, axis=0, keepdims=True)
        dlb_ref[...] += jnp.sum(d_vv, axis=0, keepdims=True)
        d_xhat = d_vv * lg
        d_gv = rs * (d_xhat - jnp.mean(d_xhat, axis=-1, keepdims=True)
                     - xhat * jnp.mean(d_xhat * xhat, axis=-1, keepdims=True))
        dzv_ref[...] = (d_gv * dgv_dz).astype(BF16)

    vec = pl.BlockSpec((1, dg), lambda i: (0, 0))
    wspec = pl.BlockSpec((n_heads, CHUNK, CHUNK), lambda i: (0, 0, 0))
    bspec = pl.BlockSpec((CHUNK, n_heads), lambda i: (0, 0))
    rowb = pl.BlockSpec((tr, dg), lambda i: (i, 0))
    act = jax.ShapeDtypeStruct((s, dg), BF16)
    vshape = jax.ShapeDtypeStruct((1, dg), F32)
    return pl.pallas_call(
        body, name=name,
        out_shape=(act, act, act, jax.ShapeDtypeStruct((n_heads, CHUNK, CHUNK), F32),
                   jax.ShapeDtypeStruct((CHUNK, n_heads), F32), vshape, vshape, vshape, vshape),
        grid=(s // tr,),
        in_specs=[pl.BlockSpec((tr, dg), lambda i: (i, 3)), pl.BlockSpec((tr, dg), lambda i: (i, 4)),
                  rowb, pl.BlockSpec((tr, 2 * dg), lambda i: (i, 0)), vec, vec, wspec, bspec, vec, vec],
        out_specs=(rowb, rowb, rowb, wspec, bspec, vec, vec, vec, vec),
        compiler_params=_params("arbitrary"),
    )(z, z, o, d_merged, ln_g, ln_b, w_s, b_col, attn_g, gm_g)


def _place():
    x, y, c = lax.axis_index("x"), lax.axis_index("y"), lax.axis_index("c")
    other_chips = [(1 - x, y), (x, 1 - y), (1 - x, 1 - y)]
    return x, y, c, other_chips


def _remote(src, dst, send_sem, recv_sem, to):
    return pltpu.make_async_remote_copy(src_ref=src, dst_ref=dst, send_sem=send_sem, recv_sem=recv_sem,
                                        device_id=to, device_id_type=MESH)


def _cast_into_slot(w, place, *, name, phases=()):
    rows, cols = w.shape
    tr, tc = _rc_tile(rows, cols)

    def body(place_ref, w_ref, o_ref):
        o_ref[...] = w_ref[...].astype(BF16)

    return _only(_call(
        body, name=name, n_prefetch=1, grid=(rows // tr, cols // tc),
        in_specs=[pl.BlockSpec((tr, tc), lambda i, j, pr: (i, j))],
        out_specs=[pl.BlockSpec((None, tr, tc), lambda i, j, pr: (pr[0], i, j))],
        out_shape=[jax.ShapeDtypeStruct((N_CHIPS, rows, cols), BF16)], operands=[place, w],
        semantics=("parallel", "parallel"), phases=phases))


def _casts_and_norm(weights, place, x, g, *, name, rows=256, phases=()):
    cols = x.shape[1]
    jobs = [w.shape[0] // rows for w in weights] + [x.shape[0] // rows]
    assert all(w.shape[1] == cols and w.shape[0] % rows == 0 for w in weights) and x.shape[0] % rows == 0
    first = [sum(jobs[:k]) for k in range(len(jobs))]

    def strip(k):
        return lambda t: jnp.clip(t - first[k], 0, jobs[k] - 1)

    def body(place_ref, *refs):
        n = len(weights)
        w_refs, x_ref, g_ref, outs = refs[:n], refs[n], refs[n + 1], refs[n + 2:]
        t = pl.program_id(0)
        for k in range(n):
            @pl.when(jnp.logical_and(t >= first[k], t < first[k] + jobs[k]))
            def _(k=k):
                outs[k][...] = w_refs[k][...].astype(BF16)

        @pl.when(t >= first[n])
        def _():
            xv = x_ref[...]
            r = lax.rsqrt(jnp.mean(xv * xv, axis=-1, keepdims=True) + EPS)
            outs[n][...] = ((xv * r) * g_ref[...]).astype(BF16)

    in_specs = [pl.BlockSpec((rows, cols), lambda t, pr, k=k: (strip(k)(t), 0)) for k in range(len(weights))]
    in_specs += [pl.BlockSpec((rows, cols), lambda t, pr: (strip(len(weights))(t), 0)),
                 pl.BlockSpec((1, cols), lambda t, pr: (0, 0))]
    out_specs = [pl.BlockSpec((None, rows, cols), lambda t, pr, k=k: (pr[0], strip(k)(t), 0)) for k in range(len(weights))]
    out_specs.append(pl.BlockSpec((rows, cols), lambda t, pr: (strip(len(weights))(t), 0)))
    out_shape = [jax.ShapeDtypeStruct((N_CHIPS,) + w.shape, BF16) for w in weights] + [jax.ShapeDtypeStruct(x.shape, BF16)]
    return _call(body, name=name, n_prefetch=1, grid=(sum(jobs),), in_specs=in_specs, out_specs=out_specs,
                 out_shape=out_shape, operands=[place, *weights, x, g], semantics=("arbitrary",), phases=phases)


def _exchange(phases, *, name):
    comm_in = [a for ph in phases for a in ph.arrays]
    comm_out = [jax.ShapeDtypeStruct(s.shape, s.dtype) for ph in phases for s in (ph.arrays if ph.in_place else ph.out_shapes)]
    aliases, at_in, at_out = {}, 0, 0
    for ph in phases:
        if ph.in_place:
            aliases.update({at_in + r: at_out + r for r in range(len(ph.arrays))})
        at_in, at_out = at_in + len(ph.arrays), at_out + ph.n_out
    n_sems = sum(ph.n_sems for ph in phases)

    def body(*refs):
        cin, cout = refs[:len(comm_in)], refs[len(comm_in):len(comm_in) + len(comm_out)]
        send_sems, recv_sems = refs[len(comm_in) + len(comm_out):]
        _run_phases(phases, ("start", "finish"), cin, cout, send_sems, recv_sems)

    return pl.pallas_call(
        body, name=name, out_shape=tuple(comm_out), in_specs=[ANY] * len(comm_in), out_specs=tuple([ANY] * len(comm_out)),
        input_output_aliases=aliases,
        scratch_shapes=[pltpu.SemaphoreType.DMA((n_sems,)), pltpu.SemaphoreType.DMA((n_sems,))],
    )(*comm_in)


GATHER_PARTS = 4


def _gather(bufs, stage, part=(0, GATHER_PARTS)):
    n = 3 * len(bufs)
    lo, hi = part

    def copies(outs, send, recv, d2d, incoming):
        x, y, c, chips = _place()
        for t, buf in enumerate(outs):
            half = buf.shape[2] // 2
            piece = half // GATHER_PARTS
            for k, (cx, cy) in enumerate(chips):
                i = 3 * t + k + (n if (d2d and stage == "both") else 0)
                cols = pl.ds(((1 - c) if (d2d and incoming) else c) * half + lo * piece, (hi - lo) * piece)
                blk = buf.at[(2 * cx + cy) if (d2d or incoming) else (2 * x + y), :, cols]
                yield _remote(blk, blk, send(i), recv(i), (x, y, 1 - c) if d2d else (cx, cy, c))

    def start(ins, outs, send, recv):
        for cp in copies(outs, send, recv, stage == "d2d", False):
            cp.start()

    def finish(ins, outs, send, recv):
        if stage == "both":
            for arrival, onward in zip(copies(outs, send, recv, False, True), copies(outs, send, recv, True, False)):
                arrival.wait_recv()
                onward.start()
        for cp in copies(outs, send, recv, stage != "ici", True):
            cp.wait_recv()
        for d2d in ((False, True) if stage == "both" else (stage == "d2d",)):
            for cp in copies(outs, send, recv, d2d, False):
                cp.wait_send()

    return _Phase(bufs, [], True, (2 if stage == "both" else 1) * n, start, finish)


def _merge(first, second):
    n_first = first.n_sems

    def later(sem):
        return lambda i: sem(n_first + i)

    def start(ins, outs, send, recv):
        first.start(ins, outs, send, recv)
        second.start(ins, outs, later(send), later(recv))

    def finish(ins, outs, send, recv):
        first.finish(ins, outs, send, recv)
        second.finish(ins, outs, later(send), later(recv))

    return _Phase(first.arrays, [], True, n_first + second.n_sems, start, finish)


def _gather_by_parts(bufs, lo, hi):
    phase = _gather(bufs, "both", (lo, lo + 1))
    for part in range(lo + 1, hi):
        phase = _merge(phase, _gather(bufs, "both", (part, part + 1)))
    return phase


def _swap_halves(grads):
    def copies(ins, outs, send, recv):
        x, y, c, _ = _place()
        for t, g in enumerate(ins):
            half = g.shape[2] // 2
            yield _remote(g.at[:, :, pl.ds((1 - c) * half, half)], outs[t], send(t), recv(t), (x, y, 1 - c))

    def start(ins, outs, send, recv):
        for cp in copies(ins, outs, send, recv):
            cp.start()

    def finish(ins, outs, send, recv):
        for cp in copies(ins, outs, send, recv):
            cp.wait()

    shapes = [jax.ShapeDtypeStruct((a.shape[0], a.shape[1], a.shape[2] // 2), a.dtype) for a in grads]
    return _Phase(grads, shapes, False, len(grads), start, finish)


def _add_halves(grad, received, place, *, name):
    ns, rows, half = received.shape
    tr, tc = _rc_tile(rows, half, pref_rows=1024)
    per = half // tc

    def body(place_ref, g_ref, r_ref, o_ref):
        o_ref[...] = (g_ref[...].astype(F32) + r_ref[...].astype(F32)).astype(BF16)

    grid_spec = pltpu.PrefetchScalarGridSpec(
        num_scalar_prefetch=1, grid=(ns, rows // tr, per),
        in_specs=[pl.BlockSpec((None, tr, tc), lambda s, i, j, pr: (s, i, pr[1] * per + j)),
                  pl.BlockSpec((None, tr, tc), lambda s, i, j, pr: (s, i, j))],
        out_specs=pl.BlockSpec((None, tr, tc), lambda s, i, j, pr: (s, i, j)),
    )
    return pl.pallas_call(
        body, name=name, grid_spec=grid_spec, out_shape=jax.ShapeDtypeStruct(received.shape, BF16),
        compiler_params=_params("parallel", "parallel", "parallel"),
    )(place, grad, received)


def _send_partials(parts, piece=(0, 1)):
    k_th, n_pieces = piece

    def cols(part):
        width = part.shape[2] // n_pieces
        return pl.ds(k_th * width, width)

    def start(ins, outs, send, recv):
        x, y, c, chips = _place()
        for t, part in enumerate(ins):
            for k, (cx, cy) in enumerate(chips):
                _remote(part.at[2 * cx + cy, :, cols(part)], outs[t].at[2 * x + y],
                        send(3 * t + k), recv(3 * t + k), (cx, cy, c)).start()

    def finish(ins, outs, send, recv):
        x, y, c, chips = _place()
        for t, part in enumerate(ins):
            for k, (cx, cy) in enumerate(chips):
                slot = outs[t].at[2 * cx + cy]
                _remote(slot, slot, send(3 * t + k), recv(3 * t + k), (cx, cy, c)).wait_recv()
        for t, part in enumerate(ins):
            for k, (cx, cy) in enumerate(chips):
                sent = part.at[2 * cx + cy, :, cols(part)]
                _remote(sent, sent, send(3 * t + k), recv(3 * t + k), (cx, cy, c)).wait_send()

    shapes = [jax.ShapeDtypeStruct(a.shape[:2] + (a.shape[2] // n_pieces,), a.dtype) for a in parts]
    return _Phase(parts, shapes, False, 3 * len(parts), start, finish)


def _sum_chips(parts, slots, place, *, name, piece=(0, 1), into=None):
    ns, rows, width = slots.shape
    k_th, n_pieces = piece
    half = width * n_pieces
    tr, tc = _rc_tile(rows, width, pref_rows=512)
    per = width // tc

    def body(place_ref, p_ref, s_ref, *rest):
        acc = p_ref[...].astype(F32)
        for k in range(ns):
            acc = acc + jnp.where(place_ref[0] == k, 0.0, s_ref[k].astype(F32))
        rest[-1][...] = acc

    grid_spec = pltpu.PrefetchScalarGridSpec(
        num_scalar_prefetch=1, grid=(rows // tr, per),
        in_specs=[pl.BlockSpec((None, tr, tc), lambda i, j, pr: (pr[0], i, k_th * per + j)),
                  pl.BlockSpec((ns, tr, tc), lambda i, j, pr: (0, i, j))] + ([ANY] if into is not None else []),
        out_specs=pl.BlockSpec((tr, tc), lambda i, j, pr: (i, (pr[1] * n_pieces + k_th) * per + j)),
    )
    return pl.pallas_call(
        body, name=name, grid_spec=grid_spec, out_shape=jax.ShapeDtypeStruct((rows, 2 * half), F32),
        input_output_aliases={3: 0} if into is not None else {},
        compiler_params=_params("parallel", "parallel"),
    )(place, parts, slots, *([into] if into is not None else []))


def _join_halves(bufs):
    def copies(outs, send, recv, incoming):
        x, y, c, _ = _place()
        for t, buf in enumerate(outs):
            half = buf.shape[1] // 2
            cols = buf.at[:, pl.ds(((1 - c) if incoming else c) * half, half)]
            yield _remote(cols, cols, send(t), recv(t), (x, y, 1 - c))

    def start(ins, outs, send, recv):
        for cp in copies(outs, send, recv, False):
            cp.start()

    def finish(ins, outs, send, recv):
        for cp in copies(outs, send, recv, True):
            cp.wait_recv()
        for cp in copies(outs, send, recv, False):
            cp.wait_send()

    return _Phase(bufs, [], True, len(bufs), start, finish)


def _gather_small(buf):
    def slot(out, px, py, pc):
        return out.at[4 * px + 2 * py + pc]

    def start(ins, outs, send, recv):
        x, y, c, chips = _place()
        mine = slot(outs[0], x, y, c)
        _remote(ins[0], mine, send(0), recv(0), (x, y, 1 - c)).start()
        for k, (cx, cy) in enumerate(chips):
            _remote(ins[0], mine, send(1 + k), recv(1 + k), (cx, cy, c)).start()

    def finish(ins, outs, send, recv):
        x, y, c, chips = _place()
        sibling = (x, y, 1 - c)
        for k, (cx, cy) in enumerate(chips):
            arrived = slot(outs[0], cx, cy, c)
            _remote(arrived, arrived, send(1 + k), recv(1 + k), sibling).wait_recv()
            _remote(arrived, arrived, send(4 + k), recv(4 + k), sibling).start()
        theirs = slot(outs[0], x, y, 1 - c)
        _remote(theirs, theirs, send(0), recv(0), sibling).wait_recv()
        for k, (cx, cy) in enumerate(chips):
            passed = slot(outs[0], cx, cy, 1 - c)
            _remote(passed, passed, send(4 + k), recv(4 + k), sibling).wait_recv()
        for i in range(7):
            _remote(ins[0], ins[0], send(i), recv(i), sibling).wait_send()

    return _Phase([buf], [jax.ShapeDtypeStruct((N_DEV,) + buf.shape, buf.dtype)], False, 7, start, finish)


def _adamw_math(w, g, m, v):
    m = ADAM_B1 * m + (1.0 - ADAM_B1) * g
    v = ADAM_B2 * v + (1.0 - ADAM_B2) * (g * g)
    m_hat = m / (1.0 - ADAM_B1 ** ADAM_STEP)
    v_hat = v / (1.0 - ADAM_B2 ** ADAM_STEP)
    delta = -ADAM_LR * (m_hat / (jnp.sqrt(v_hat) + ADAM_EPS) + ADAM_WD * w)
    return delta, m, v


def _adamw(w, g, m, v, *, name):
    rows, cols = w.shape
    tr, tc = _rc_tile(rows, cols)

    def body(w_ref, g_ref, m_ref, v_ref, go_ref, d_ref, mo_ref, vo_ref):
        g = g_ref[...]
        go_ref[...] = g
        d_ref[...], mo_ref[...], vo_ref[...] = _adamw_math(w_ref[...], g, m_ref[...], v_ref[...])

    blk = pl.BlockSpec((tr, tc), lambda i, j: (i, j))
    shape = jax.ShapeDtypeStruct((rows, cols), F32)
    return pl.pallas_call(
        body, name=name, out_shape=(shape, shape, shape, shape), grid=(rows // tr, cols // tc),
        in_specs=[blk] * 4, out_specs=(blk, blk, blk, blk), compiler_params=_params("parallel", "parallel"),
    )(w, g, m, v)


def _adamw_small(gathered, own, place, w, m, v, *, name):
    nd = gathered.shape[0]

    def body(place_ref, gs_ref, own_ref, w_ref, m_ref, v_ref, g_ref, d_ref, mo_ref, vo_ref):
        me = 2 * place_ref[0] + place_ref[1]
        g = jnp.zeros(own_ref.shape, F32)
        for k in range(nd):
            g = g + jnp.where(me == k, own_ref[...], gs_ref[k])
        g_ref[...] = g
        d_ref[...], mo_ref[...], vo_ref[...] = _adamw_math(w_ref[...], g, m_ref[...], v_ref[...])

    whole = pl.BlockSpec(w.shape, lambda i, pr: (0, 0))
    grid_spec = pltpu.PrefetchScalarGridSpec(
        num_scalar_prefetch=1, grid=(1,),
        in_specs=[pl.BlockSpec(gathered.shape, lambda i, pr: (0, 0, 0)), whole, whole, whole, whole],
        out_specs=(whole, whole, whole, whole))
    shape = jax.ShapeDtypeStruct(w.shape, F32)
    return pl.pallas_call(body, name=name, grid_spec=grid_spec, out_shape=(shape, shape, shape, shape),
                          compiler_params=_params("arbitrary"))(place, gathered, own, w, m, v)


def _pack(parts):
    flat = jnp.concatenate([p.reshape(-1).astype(F32) for p in parts])
    rows = -(-flat.shape[0] // (8 * LANES)) * 8
    return jnp.pad(flat, (0, rows * LANES - flat.shape[0])).reshape(rows, LANES)


def _unpack(buf, shapes):
    flat = buf.reshape(-1)
    out, pos = [], 0
    for shp in shapes:
        size = int(np.prod(shp))
        out.append(flat[pos:pos + size].reshape(shp))
        pos += size
    return out


ROW_BLOCK = 256


def _realign_rows(sources, segments, out_shape, *, name):
    n_slots, rows, cols = out_shape
    n_src = len(sources)
    per_slot = -(-rows // ROW_BLOCK)
    table = np.zeros((6, n_slots * per_slot, n_src), np.int32)
    for so in range(n_slots):
        for first, last, src, src_slot, src_row in segments[so]:
            for blk in range(first // ROW_BLOCK, (last - 1) // ROW_BLOCK + 1):
                lo, hi = max(first, blk * ROW_BLOCK), min(last, (blk + 1) * ROW_BLOCK)
                base = src_row + (blk * ROW_BLOCK - first)
                m0 = (base + lo - blk * ROW_BLOCK) // ROW_BLOCK
                at = so * per_slot + blk
                assert table[4, at, src] == 0, "two segments of one block share a source operand"
                table[:, at, src] = (src_slot, m0, base - m0 * ROW_BLOCK, lo - blk * ROW_BLOCK, hi - blk * ROW_BLOCK,
                                     min(2 * ROW_BLOCK, sources[src].shape[1] - m0 * ROW_BLOCK))
    last_block = [-(-a.shape[1] // ROW_BLOCK) - 1 for a in sources]

    def body(slot_ref, blk_ref, off_ref, lo_ref, hi_ref, valid_ref, *refs):
        o_ref, acc = refs[2 * n_src], refs[2 * n_src + 1]
        at = (pl.program_id(0) * per_slot + pl.program_id(1)) * n_src
        acc[...] = jnp.zeros_like(acc)
        for p in range(n_src):
            @pl.when(hi_ref[at + p] > lo_ref[at + p])
            def _():
                two = jnp.concatenate([refs[2 * p][...], refs[2 * p + 1][...]], axis=0)
                src_row = lax.broadcasted_iota(jnp.int32, two.shape, 0)
                two = jnp.where(src_row < valid_ref[at + p], two, jnp.zeros_like(two))
                r = lax.broadcasted_iota(jnp.int32, (ROW_BLOCK, 2 * ROW_BLOCK), 0)
                c = lax.broadcasted_iota(jnp.int32, (ROW_BLOCK, 2 * ROW_BLOCK), 1)
                place = (c == r + off_ref[at + p]) & (r >= lo_ref[at + p]) & (r < hi_ref[at + p])
                acc[...] += jnp.dot(place.astype(two.dtype), two, preferred_element_type=F32)
        o_ref[...] = acc[...].astype(o_ref.dtype)

    def src_spec(p, second):
        def index(so, i, slot_r, blk_r, off_r, lo_r, hi_r, valid_r):
            at = (so * per_slot + i) * n_src + p
            return slot_r[at], jnp.minimum(blk_r[at] + second, last_block[p]), 0
        return pl.BlockSpec((None, ROW_BLOCK, cols), index)

    grid_spec = pltpu.PrefetchScalarGridSpec(
        num_scalar_prefetch=6, grid=(n_slots, per_slot),
        in_specs=[src_spec(p, second) for p in range(n_src) for second in (0, 1)],
        out_specs=pl.BlockSpec((None, ROW_BLOCK, cols), lambda so, i, *_: (so, i, 0)),
        scratch_shapes=[pltpu.VMEM((ROW_BLOCK, cols), F32)],
    )
    flat = [jnp.asarray(table[k].reshape(-1)) for k in range(6)]
    return pl.pallas_call(
        body, name=name, grid_spec=grid_spec, out_shape=jax.ShapeDtypeStruct(out_shape, sources[0].dtype),
        compiler_params=_params("parallel", "arbitrary"),
    )(*flat, *[a for a in sources for _ in (0, 1)])


def _shard_rows(g, lo, hi):
    rs = g.shape[1]
    pieces = []
    for j in range(g.shape[0]):
        a, b = max(lo, j * rs), min(hi, (j + 1) * rs)
        if a < b:
            pieces.append(g[j, a - j * rs:b - j * rs])
    return pieces


def kernel(x, norm_mix_g, w_in, b_f, gmlp_ln_g, gmlp_ln_b, w_s, b_s, attn_out_g, gmlp_out_g, w_out, norm_ffn_g, w_ff1, w_ff2, norm_final_g, loss_target, m_norm_mix_g, m_w_in, m_b_f, m_gmlp_ln_g, m_gmlp_ln_b, m_w_s, m_b_s, m_attn_out_g, m_gmlp_out_g, m_w_out, m_norm_ffn_g, m_w_ff1, m_w_ff2, m_norm_final_g, v_norm_mix_g, v_w_in, v_b_f, v_gmlp_ln_g, v_gmlp_ln_b, v_w_s, v_b_s, v_attn_out_g, v_gmlp_out_g, v_w_out, v_norm_ffn_g, v_w_ff1, v_w_ff2, v_norm_final_g):
    seq, d_model = x.shape[1], x.shape[2]
    d_attn = d_model // 2
    n_heads = d_attn // HEAD_DIM
    qkv = 3 * d_attn
    shard_cols = w_in.shape[2]
    assert N_CHIPS * shard_cols == qkv + n_heads + 2 * d_attn
    xs = x.reshape(seq, d_model)
    target = loss_target.reshape(seq, d_model)

    place = jnp.stack([2 * lax.axis_index("x") + lax.axis_index("y"), lax.axis_index("c")]).astype(jnp.int32)
    names = ["w_in", "w_out", "w_ff1", "w_ff2"]
    wt_in, mt_in, vt_in = w_in[0].T, m_w_in[0].T, v_w_in[0].T
    b_in, _ = _cast_into_slot(wt_in, place, name="cast_w_in")
    (b_out, b_ff1, b_ff2, h), (g_in,) = _casts_and_norm(
        [w_out[0], w_ff1[0], w_ff2[0]], place, xs, norm_mix_g, name="casts_and_norm_mix",
        phases=[_gather_by_parts([b_in], 0, GATHER_PARTS)])
    n_cols = N_CHIPS * shard_cols
    gate_slot, gate_row = divmod(qkv, shard_cols)
    assert gate_row + n_heads <= shard_cols
    pieces = []
    for j in range(N_CHIPS):
        if j == gate_slot:
            pieces += [(j, 0, gate_row), (j, gate_row + n_heads, shard_cols - gate_row - n_heads)]
        else:
            pieces.append((j, 0, shard_cols))
    fwd_segments, at = [[]], 0
    for order, (j, src_row, size) in enumerate(pieces):
        fwd_segments[0].append((at, at + size, order % 3, j, src_row))
        at += size
    wt_main = _realign_rows([g_in] * 3, fwd_segments, (1, n_cols - n_heads, d_model), name="w_in_rows")[0]
    wt_f = jnp.pad(jnp.concatenate(_shard_rows(g_in, qkv, qkv + n_heads), axis=0), ((0, LANES - n_heads), (0, 0)))
    b_f_pad = jnp.pad(b_f, ((0, 0), (0, LANES - n_heads)))
    b_col = b_s[0].T

    first, rest = (0, 1), (1, GATHER_PARTS)
    z, (b_out, b_ff1) = _matmul(h, wt_main, name="in_proj", out_dtype=BF16, trans_b=True, tm=2048,
                                phases=[_gather([b_out], "ici"), _gather([b_ff1], "ici", first)])
    zb, f_cum = _forget_fwd(h, wt_f, b_f_pad, name="forget_fwd")
    f_row = f_cum[:, :n_heads].T[:, None, :]
    (o, lse2), (b_ff1, b_out) = _attn_fwd(z, f_row, n_heads, name="attn_fwd",
                                          phases=[_gather([b_ff1], "ici", rest), _gather([b_out], "d2d")])
    merged = _mix_fwd(z, o, gmlp_ln_g, gmlp_ln_b, w_s[0], b_col, attn_out_g, gmlp_out_g, n_heads, name="mix_fwd")
    w_out_full = b_out.reshape(2 * d_attn, d_model)
    x1, (b_ff1, b_ff2) = _matmul(merged, w_out_full, name="out_proj", out_dtype=F32, residual=xs,
                                 phases=[_gather([b_ff1], "d2d"), _gather([b_ff2], "ici", first)])
    h2, _ = _rmsnorm_fwd(x1, norm_ffn_g, name="norm_ffn")
    a, (b_ff2,) = _matmul(h2, b_ff1, name="ff1", out_dtype=BF16, relu=True, b_sharded=True, tm=2048,
                          phases=[_merge(_gather([b_ff2], "d2d", first), _gather_by_parts([b_ff2], 1, GATHER_PARTS))])
    w_ff2_full = b_ff2.reshape(N_CHIPS * b_ff2.shape[1], d_model)
    x2, _ = _matmul(a, w_ff2_full, name="ff2", out_dtype=F32, square_lhs=True, residual=x1)
    dx2, dx2_b, dg_final, loss = _loss_and_final_bwd(x2, target, norm_final_g.reshape(1, d_model), name="loss_head")

    def pair_sum(g, r, nm):
        return _add_halves(g, r, place, name="grads_pair_sum_" + nm)

    def chip_sum(p, q, nm, **piece):
        return _sum_chips(p, q, place, name="grads_chip_sum_" + nm, **piece)

    dw_ff2, _ = _matmul(a, dx2_b, name="ff2_dw", out_dtype=BF16, trans_a=True, square_lhs=True)
    dw_ff2 = dw_ff2.reshape(N_CHIPS, -1, d_model)
    da, (r_ff2,) = _matmul(dx2_b, w_ff2_full, name="ff2_dlhs", out_dtype=BF16, trans_b=True, scale2_by=a, tm=2048,
                           phases=[_swap_halves([dw_ff2])])
    ps_ff2 = pair_sum(dw_ff2, r_ff2, "w_ff2")
    dh2, (q_ff2a,) = _matmul(da, b_ff1, name="ff1_dlhs", out_dtype=F32, trans_b=True, b_sharded=True,
                             phases=[_send_partials([ps_ff2], (0, 2))])
    dw_ff1, (q_ff2b,) = _matmul(h2, da, name="ff1_dw", out_dtype=BF16, trans_a=True, out_sharded=True, tk=seq,
                                phases=[_send_partials([ps_ff2], (1, 2))])
    g_ff2 = chip_sum(ps_ff2, q_ff2a, "w_ff2_a", piece=(0, 2))
    g_ff2 = chip_sum(ps_ff2, q_ff2b, "w_ff2_b", piece=(1, 2), into=g_ff2)
    (dx1, dg_ffn, dx1_b), (g_ff2,) = _rmsnorm_bwd(dh2, x1, dx2, norm_ffn_g, name="norm_ffn_bwd", rounded_copy=True,
                                                   phases=[_join_halves([g_ff2])])
    dw_out, _ = _matmul(merged, dx1_b, name="out_proj_dw", out_dtype=BF16, trans_a=True, tk=seq)
    dw_out = dw_out.reshape(N_CHIPS, -1, d_model)
    d_merged, (r_ff1, r_out) = _matmul(dx1_b, w_out_full, name="out_proj_dlhs", out_dtype=F32, trans_b=True,
                                       phases=[_swap_halves([dw_ff1, dw_out])])
    ps_ff1, ps_out = pair_sum(dw_ff1, r_ff1, "w_ff1"), pair_sum(dw_out, r_out, "w_out")
    d_o, dzu, dzv, dw_s, db_col, dlg, dlb, dag, dgg = _mix_bwd(
        z, o, d_merged, gmlp_ln_g, gmlp_ln_b, w_s[0], b_col, attn_out_g, gmlp_out_g, n_heads, name="mix_bwd")
    (dq, dk, dv, d_f_key, d_f_query), (q_ff1, q_out) = _attn_bwd(
        z, o, d_o, lse2, f_row, n_heads, name="attn_bwd", phases=[_send_partials([ps_ff1, ps_out])])
    g_ff1, g_out = chip_sum(ps_ff1, q_ff1, "w_ff1"), chip_sum(ps_out, q_out, "w_out")
    d_f = d_f_key.reshape(n_heads, seq) + d_f_query.reshape(n_heads, seq)
    d_f_pad = jnp.pad(d_f.T, ((0, 0), (0, LANES - n_heads)))
    dzf, db_f = _forget_bwd(d_f_pad, zb, name="forget_bwd")
    dz = jnp.concatenate([dq, dk, dv, dzu, dzv], axis=1)
    early_g = _pack([db_f[:, :n_heads], dlg, dlb, dw_s, db_col.T, dag, dgg, dg_ffn, dg_final])
    dwt_main, (g_ff1, g_out, early_all) = _matmul(dz, h, name="in_proj_dw", out_dtype=BF16, trans_a=True, tk=seq,
                                                  phases=[_join_halves([g_ff1, g_out]), _gather_small(early_g)])
    dwt_f, _ = _matmul(dzf, h, name="gate_dw", out_dtype=BF16, trans_a=True)
    bwd_segments = []
    for j in range(N_CHIPS):
        first = j * shard_cols
        if j < gate_slot:
            bwd_segments.append([(0, shard_cols, 0, 0, first)])
        elif j > gate_slot:
            bwd_segments.append([(0, shard_cols, 0, 0, first - n_heads)])
        else:
            bwd_segments.append([(0, gate_row, 0, 0, first), (gate_row, gate_row + n_heads, 1, 0, 0),
                                 (gate_row + n_heads, shard_cols, 2, 0, qkv)])
    dw_in = _realign_rows([dwt_main[None], dwt_f[None], dwt_main[None]], bwd_segments,
                          (N_CHIPS, shard_cols, d_model), name="dw_in_rows")
    dh_gate, (r_in,) = _matmul(dzf, wt_f, name="gate_dlhs", out_dtype=F32, phases=[_swap_halves([dw_in])])
    ps_in = pair_sum(dw_in, r_in, "w_in")
    dh, (q_in,) = _matmul(dz, wt_main, name="in_proj_dlhs", out_dtype=F32, residual=dh_gate, tk=2560,
                          phases=[_send_partials([ps_in])])
    g_in_sum = chip_sum(ps_in, q_in, "w_in")
    (grad_x, dg_mix), _ = _rmsnorm_bwd(dh, xs, dx1, norm_mix_g, name="norm_mix_bwd")
    late_g = _pack([dg_mix])
    g_in_sum, late_all = _exchange([_join_halves([g_in_sum]), _gather_small(late_g)], name="grads_join_w_in")

    big = {}
    for nm, g, w, m, v in zip(names, (g_in_sum, g_out, g_ff1, g_ff2), (wt_in, w_out[0], w_ff1[0], w_ff2[0]),
                              (mt_in, m_w_out[0], m_w_ff1[0], m_w_ff2[0]), (vt_in, v_w_out[0], v_w_ff1[0], v_w_ff2[0])):
        big[nm] = tuple((t.T if nm == "w_in" else t)[None] for t in _adamw(w, g, m, v, name="adamw_" + nm))

    small_params = dict(
        norm_mix_g=(norm_mix_g, m_norm_mix_g, v_norm_mix_g), b_f=(b_f, m_b_f, v_b_f),
        gmlp_ln_g=(gmlp_ln_g, m_gmlp_ln_g, v_gmlp_ln_g), gmlp_ln_b=(gmlp_ln_b, m_gmlp_ln_b, v_gmlp_ln_b),
        w_s=(w_s, m_w_s, v_w_s), b_s=(b_s, m_b_s, v_b_s), attn_out_g=(attn_out_g, m_attn_out_g, v_attn_out_g),
        gmlp_out_g=(gmlp_out_g, m_gmlp_out_g, v_gmlp_out_g), norm_ffn_g=(norm_ffn_g, m_norm_ffn_g, v_norm_ffn_g),
        norm_final_g=(norm_final_g, m_norm_final_g, v_norm_final_g))

    def small_step(group, grads_all, grads_own, label):
        w, m, v = ([small_params[nm][k] for nm in group] for k in range(3))
        packed = _adamw_small(grads_all, grads_own, place, _pack(w), _pack(m), _pack(v), name="adamw_small_" + label)
        parts = [_unpack(p, [a.shape for a in w]) for p in packed]
        return {nm: tuple(part[i] for part in parts) for i, nm in enumerate(group)}

    early = ["b_f", "gmlp_ln_g", "gmlp_ln_b", "w_s", "b_s", "attn_out_g", "gmlp_out_g", "norm_ffn_g", "norm_final_g"]
    small = {**small_step(early, early_all, early_g, "early"), **small_step(["norm_mix_g"], late_all, late_g, "late")}

    order = ["norm_mix_g", "w_in", "b_f", "gmlp_ln_g", "gmlp_ln_b", "w_s", "b_s", "attn_out_g", "gmlp_out_g", "w_out",
             "norm_ffn_g", "w_ff1", "w_ff2", "norm_final_g"]
    result = {**small, **big}
    total_loss = lax.psum(loss[0, 0], ("x", "y", "c"))
    outs = [total_loss, grad_x.reshape(x.shape)]
    for part in range(4):
        outs += [result[nm][part] for nm in order]
    return tuple(outs)
```

```python
import functools
import math

import numpy as np
import jax
import jax.numpy as jnp
from jax import lax
from jax.experimental import pallas as pl
from jax.experimental.pallas import tpu as pltpu

HEAD_DIM = 128
CHUNK = 128
EPS = 1e-6
LANES = 128
MXU_COLUMNS = 256
N_CHIPS = 4
N_DEV = 8
VMEM_LIMIT_BYTES = 56 * 1024 * 1024

ADAM_LR = 0.001
ADAM_B1 = 0.9
ADAM_B2 = 0.999
ADAM_EPS = 1e-08
ADAM_WD = 0.01
ADAM_STEP = 10

BF16 = jnp.bfloat16
F32 = jnp.float32
MESH = pl.DeviceIdType.MESH
ANY = pl.BlockSpec(memory_space=pl.ANY)
NEG_BIG = -1e30


def _params(*sem):
    return pltpu.CompilerParams(dimension_semantics=tuple(sem), vmem_limit_bytes=VMEM_LIMIT_BYTES)


def _tile(n, pref, unit):
    t = (min(pref, n) // unit) * unit
    while t >= unit:
        if n % t == 0:
            return t
        t -= unit
    return n


def _rc_tile(rows, cols, pref_rows=256, pref_cols=256):
    if rows % 16 == 0:
        return _tile(rows, pref_rows, 16), cols
    return rows, _tile(cols, pref_cols, LANES)


class _Phase:
    def __init__(self, arrays, out_shapes, in_place, n_sems, start, finish):
        self.arrays, self.out_shapes, self.in_place = list(arrays), list(out_shapes), in_place
        self.n_sems, self.start, self.finish = n_sems, start, finish

    @property
    def n_out(self):
        return len(self.arrays) if self.in_place else len(self.out_shapes)


def _run_phases(phases, steps, comm_in, comm_out, send_sems, recv_sems):
    at_in = at_out = at_sem = 0
    for ph in phases:
        for step in steps:
            getattr(ph, step)(comm_in[at_in:at_in + len(ph.arrays)], comm_out[at_out:at_out + ph.n_out],
                              lambda i, base=at_sem: send_sems.at[base + i], lambda i, base=at_sem: recv_sems.at[base + i])
        at_in, at_out, at_sem = at_in + len(ph.arrays), at_out + ph.n_out, at_sem + ph.n_sems


def _call(body, *, name, grid, in_specs, out_specs, out_shape, operands, semantics, scratch_shapes=(),
          n_prefetch=0, phases=()):
    in_specs, out_specs, out_shape = list(in_specs), list(out_specs), list(out_shape)
    scratch_shapes = list(scratch_shapes)
    n_in, n_out, n_scr = len(operands) - n_prefetch, len(out_shape), len(scratch_shapes)
    comm_in = [a for ph in phases for a in ph.arrays]
    comm_out = [jax.ShapeDtypeStruct(s.shape, s.dtype) for ph in phases
                for s in (ph.arrays if ph.in_place else ph.out_shapes)]
    aliases, at_in, at_out = {}, n_prefetch + n_in, n_out
    for ph in phases:
        if ph.in_place:
            aliases.update({at_in + r: at_out + r for r in range(len(ph.arrays))})
        at_in, at_out = at_in + len(ph.arrays), at_out + ph.n_out
    n_sems = sum(ph.n_sems for ph in phases)

    def hosted(*refs):
        pre, rest = refs[:n_prefetch], refs[n_prefetch:]
        ins, rest = rest[:n_in], rest[n_in:]
        cin, rest = rest[:len(comm_in)], rest[len(comm_in):]
        outs, rest = rest[:n_out], rest[n_out:]
        cout, rest = rest[:len(comm_out)], rest[len(comm_out):]
        scr = rest[:n_scr]
        if phases:
            send_sems, recv_sems = rest[n_scr:]
            ids = [pl.program_id(ax) for ax in range(len(grid))]
            first = functools.reduce(jnp.logical_and, [i == 0 for i in ids])
            last = functools.reduce(jnp.logical_and, [i == g - 1 for i, g in zip(ids, grid)])

            @pl.when(first)
            def _():
                _run_phases(phases, ("start",), cin, cout, send_sems, recv_sems)

        body(*pre, *ins, *outs, *scr)
        if phases:
            @pl.when(last)
            def _():
                _run_phases(phases, ("finish",), cin, cout, send_sems, recv_sems)

    all_in = in_specs + [ANY] * len(comm_in)
    all_out = out_specs + [ANY] * len(comm_out)
    all_scr = scratch_shapes + ([pltpu.SemaphoreType.DMA((n_sems,)), pltpu.SemaphoreType.DMA((n_sems,))] if phases else [])
    if phases:
        semantics = ("arbitrary",) * len(grid)
    kwargs = dict(name=name, out_shape=tuple(out_shape + comm_out), compiler_params=_params(*semantics),
                  input_output_aliases=aliases)
    if n_prefetch:
        kwargs["grid_spec"] = pltpu.PrefetchScalarGridSpec(
            num_scalar_prefetch=n_prefetch, grid=grid, in_specs=all_in, out_specs=tuple(all_out), scratch_shapes=all_scr)
    else:
        kwargs.update(grid=grid, in_specs=all_in, out_specs=tuple(all_out), scratch_shapes=all_scr)
    res = pl.pallas_call(hosted, **kwargs)(*operands, *comm_in)
    return tuple(res[:n_out]), tuple(res[n_out:])


def _only(results):
    outs, comm = results
    return outs[0] if len(outs) == 1 else outs, comm


def _matmul(a, b, *, name, out_dtype, trans_a=False, trans_b=False, tm=1024, tn=1024, tk=2048,
            square_lhs=False, relu=False, residual=None, scale2_by=None,
            b_sharded=False, out_sharded=False, phases=()):
    m, k = (a.shape[1], a.shape[0]) if trans_a else a.shape
    if b_sharded:
        if trans_b:
            n, ks = b.shape[1], b.shape[2]
            assert N_CHIPS * ks == k
        else:
            ns = b.shape[2]
            n = N_CHIPS * ns
            assert b.shape[1] == k
    else:
        n = b.shape[0] if trans_b else b.shape[1]
        assert (b.shape[1] if trans_b else b.shape[0]) == k
    tm = _tile(m, tm, 128)
    tn = _tile(n // N_CHIPS if (out_sharded or (b_sharded and not trans_b)) else n, tn, 128)
    tk = _tile(k // N_CHIPS if (b_sharded and trans_b) else k, tk, 128)
    nk = k // tk

    if trans_a:
        a_spec = pl.BlockSpec((tk, tm), lambda i, j, kk: (kk, i))
    else:
        a_spec = pl.BlockSpec((tm, tk), lambda i, j, kk: (i, kk))
    if b_sharded and trans_b:
        per = ks // tk
        assert per * tk == ks
        b_spec = pl.BlockSpec((None, tn, tk), lambda i, j, kk: (kk // per, j, kk % per))
    elif b_sharded:
        per = ns // tn
        assert per * tn == ns
        b_spec = pl.BlockSpec((None, tk, tn), lambda i, j, kk: (j // per, kk, j % per))
    elif trans_b:
        b_spec = pl.BlockSpec((tn, tk), lambda i, j, kk: (j, kk))
    else:
        b_spec = pl.BlockSpec((tk, tn), lambda i, j, kk: (kk, j))
    if out_sharded:
        ns_out = n // N_CHIPS
        per_o = ns_out // tn
        assert per_o * tn == ns_out
        out_shape = jax.ShapeDtypeStruct((N_CHIPS, m, ns_out), out_dtype)
        o_spec = pl.BlockSpec((None, tm, tn), lambda i, j, kk: (j // per_o, i, j % per_o))
    else:
        out_shape = jax.ShapeDtypeStruct((m, n), out_dtype)
        o_spec = pl.BlockSpec((tm, tn), lambda i, j, kk: (i, j))
    mn_spec = pl.BlockSpec((tm, tn), lambda i, j, kk: (i, j))

    operands, in_specs = [a, b], [a_spec, b_spec]
    if scale2_by is not None:
        operands.append(scale2_by)
        in_specs.append(mn_spec)
    if residual is not None:
        operands.append(residual)
        in_specs.append(mn_spec)
    dims = (((0 if trans_a else 1,), (1 if trans_b else 0,)), ((), ()))
    chunk = MXU_COLUMNS if tn % MXU_COLUMNS == 0 else tn

    def body(*refs):
        a_ref, b_ref = refs[0], refs[1]
        pos = 2
        scale_ref = res_ref = None
        if scale2_by is not None:
            scale_ref = refs[pos]
            pos += 1
        if residual is not None:
            res_ref = refs[pos]
            pos += 1
        o_ref = refs[pos]
        kk = pl.program_id(2)

        av = a_ref[...]
        if square_lhs:
            av = av.astype(F32)
            av = av * av
        av = av.astype(BF16)

        def finish(r, cols):
            if relu:
                r = jnp.maximum(r, 0.0)
            if scale_ref is not None:
                r = r * (2.0 * scale_ref[:, cols].astype(F32))
            if res_ref is not None:
                r = r + res_ref[:, cols].astype(F32)
            o_ref[:, cols] = r.astype(out_dtype)

        if nk == 1:
            for lo in range(0, tn, chunk):
                cols = slice(lo, lo + chunk)
                bv = (b_ref[cols, :] if trans_b else b_ref[:, cols]).astype(BF16)
                finish(lax.dot_general(av, bv, dims, preferred_element_type=F32), cols)
        else:
            acc_ref = refs[pos + 1]
            part = lax.dot_general(av, b_ref[...].astype(BF16), dims, preferred_element_type=F32)

            @pl.when(kk == 0)
            def _():
                acc_ref[...] = part

            @pl.when(jnp.logical_and(kk > 0, kk < nk - 1))
            def _():
                acc_ref[...] += part

            @pl.when(kk == nk - 1)
            def _():
                finish(acc_ref[...] + part, slice(None))

    return _only(_call(
        body, name=name, out_shape=[out_shape], grid=(m // tm, n // tn, nk),
        in_specs=in_specs, out_specs=[o_spec], operands=operands,
        scratch_shapes=[pltpu.VMEM((tm, tn), F32)] if nk > 1 else [],
        semantics=("parallel", "parallel", "arbitrary"), phases=phases))


def _rmsnorm_fwd(x, g, *, name, tr=512, phases=()):
    s, d = x.shape
    tr = _tile(s, tr, 8)

    def body(x_ref, g_ref, o_ref):
        xv = x_ref[...]
        r = lax.rsqrt(jnp.mean(xv * xv, axis=-1, keepdims=True) + EPS)
        o_ref[...] = ((xv * r) * g_ref[...]).astype(BF16)

    return _only(_call(
        body, name=name, out_shape=[jax.ShapeDtypeStruct((s, d), BF16)], grid=(s // tr,),
        in_specs=[pl.BlockSpec((tr, d), lambda i: (i, 0)), pl.BlockSpec((1, d), lambda i: (0, 0))],
        out_specs=[pl.BlockSpec((tr, d), lambda i: (i, 0))], operands=[x, g],
        semantics=("parallel",), phases=phases))


def _rms_bwd_rows(dy, xv, g):
    d = xv.shape[-1]
    r = lax.rsqrt(jnp.mean(xv * xv, axis=-1, keepdims=True) + EPS)
    gdy = dy * g
    dot = jnp.sum(gdy * xv, axis=-1, keepdims=True)
    dx = gdy * r - xv * (r * r * r) * (dot / d)
    return dx, dy * (xv * r)


def _rmsnorm_bwd(dy, x, res, g, *, name, tr=256, rounded_copy=False, phases=()):
    s, d = x.shape
    tr = _tile(s, tr, 8)

    def body(dy_ref, x_ref, res_ref, g_ref, dx_ref, dg_ref, *dxb_ref):
        @pl.when(pl.program_id(0) == 0)
        def _():
            dg_ref[...] = jnp.zeros_like(dg_ref)

        dx, dg_rows = _rms_bwd_rows(dy_ref[...].astype(F32), x_ref[...], g_ref[...])
        out = res_ref[...] + dx
        dx_ref[...] = out
        if rounded_copy:
            dxb_ref[0][...] = out.astype(BF16)
        dg_ref[...] += jnp.sum(dg_rows, axis=0, keepdims=True)

    row = pl.BlockSpec((tr, d), lambda i: (i, 0))
    vec = pl.BlockSpec((1, d), lambda i: (0, 0))
    extra = [jax.ShapeDtypeStruct((s, d), BF16)] if rounded_copy else []
    return _call(
        body, name=name,
        out_shape=[jax.ShapeDtypeStruct((s, d), F32), jax.ShapeDtypeStruct((1, d), F32)] + extra,
        grid=(s // tr,), in_specs=[row, row, row, vec], out_specs=[row, vec] + [row] * len(extra),
        operands=[dy, x, res, g], semantics=("arbitrary",), phases=phases)


def _loss_and_final_bwd(x2, target, g, *, name, tr=256):
    s, d = x2.shape
    tr = _tile(s, tr, 8)

    def body(x_ref, t_ref, g_ref, dx_ref, dxb_ref, dg_ref, loss_ref):
        @pl.when(pl.program_id(0) == 0)
        def _():
            dg_ref[...] = jnp.zeros_like(dg_ref)
            loss_ref[...] = jnp.zeros_like(loss_ref)

        xv, gv = x_ref[...], g_ref[...]
        r = lax.rsqrt(jnp.mean(xv * xv, axis=-1, keepdims=True) + EPS)
        err = (xv * r) * gv - t_ref[...]
        row_loss = jnp.mean(err * err, axis=-1, keepdims=True)
        loss_ref[...] += 0.5 * jnp.sum(row_loss, axis=0, keepdims=True)
        dx, dg_rows = _rms_bwd_rows(err / d, xv, gv)
        dx_ref[...] = dx
        dxb_ref[...] = dx.astype(BF16)
        dg_ref[...] += jnp.sum(dg_rows, axis=0, keepdims=True)

    row = pl.BlockSpec((tr, d), lambda i: (i, 0))
    vec = pl.BlockSpec((1, d), lambda i: (0, 0))
    one = pl.BlockSpec((1, 1), lambda i: (0, 0))
    return pl.pallas_call(
        body, name=name,
        out_shape=(jax.ShapeDtypeStruct((s, d), F32), jax.ShapeDtypeStruct((s, d), BF16),
                   jax.ShapeDtypeStruct((1, d), F32), jax.ShapeDtypeStruct((1, 1), F32)),
        grid=(s // tr,), in_specs=[row, row, vec], out_specs=(row, row, vec, one),
        compiler_params=_params("arbitrary"),
    )(x2, target, g)


def _tri_ones(n, lower):
    r = lax.broadcasted_iota(jnp.int32, (n, n), 0)
    c = lax.broadcasted_iota(jnp.int32, (n, n), 1)
    return jnp.where((c <= r) if lower else (c >= r), 1.0, 0.0).astype(F32)


def _forget_fwd(h, w_f, b_f, *, name, tr=256):
    s, d = h.shape
    tr = _tile(s, tr, 8)

    def body(h_ref, w_ref, b_ref, zb_ref, f_ref, carry):
        @pl.when(pl.program_id(0) == 0)
        def _():
            carry[...] = jnp.zeros_like(carry)

        zb = lax.dot_general(h_ref[...], w_ref[...], (((1,), (1,)), ((), ())), preferred_element_type=F32) + b_ref[...]
        zb_ref[...] = zb
        log_f = jnp.minimum(zb, 0.0) - jnp.log(1.0 + jnp.exp(-jnp.abs(zb)))
        run = jnp.dot(_tri_ones(tr, True), log_f, preferred_element_type=F32,
                      precision=lax.Precision.HIGHEST) + carry[...]
        f_ref[...] = run
        carry[...] = run[tr - 1:tr, :]

    row = pl.BlockSpec((tr, LANES), lambda i: (i, 0))
    return pl.pallas_call(
        body, name=name,
        out_shape=(jax.ShapeDtypeStruct((s, LANES), F32), jax.ShapeDtypeStruct((s, LANES), F32)),
        grid=(s // tr,),
        in_specs=[pl.BlockSpec((tr, d), lambda i: (i, 0)), pl.BlockSpec((LANES, d), lambda i: (0, 0)),
                  pl.BlockSpec((1, LANES), lambda i: (0, 0))],
        out_specs=(row, row), scratch_shapes=[pltpu.VMEM((1, LANES), F32)],
        compiler_params=_params("arbitrary"),
    )(h, w_f, b_f)


def _forget_bwd(d_f, zb, *, name, tr=256):
    s = zb.shape[0]
    tr = _tile(s, tr, 8)
    nb = s // tr

    def body(df_ref, zb_ref, dz_ref, db_ref, carry):
        @pl.when(pl.program_id(0) == 0)
        def _():
            carry[...] = jnp.zeros_like(carry)
            db_ref[...] = jnp.zeros_like(db_ref)

        run = jnp.dot(_tri_ones(tr, False), df_ref[...], preferred_element_type=F32,
                      precision=lax.Precision.HIGHEST) + carry[...]
        carry[...] = run[0:1, :]
        dz = run / (1.0 + jnp.exp(zb_ref[...]))
        dz_ref[...] = dz.astype(BF16)
        db_ref[...] += jnp.sum(dz, axis=0, keepdims=True)

    row = pl.BlockSpec((tr, LANES), lambda i: (nb - 1 - i, 0))
    return pl.pallas_call(
        body, name=name,
        out_shape=(jax.ShapeDtypeStruct((s, LANES), BF16), jax.ShapeDtypeStruct((1, LANES), F32)),
        grid=(nb,), in_specs=[row, row], out_specs=(row, pl.BlockSpec((1, LANES), lambda i: (0, 0))),
        scratch_shapes=[pltpu.VMEM((1, LANES), F32)],
        compiler_params=_params("arbitrary"),
    )(d_f, zb)


def _pairs(nblk, by_kv):
    if by_kv:
        pr = [(i, j) for j in range(nblk) for i in range(j, nblk)]
    else:
        pr = [(i, j) for i in range(nblk) for j in range(i + 1)]
    return (jnp.asarray(np.array([p[0] for p in pr], np.int32)), jnp.asarray(np.array([p[1] for p in pr], np.int32)))


def _causal_mask(rows, keys):
    r = lax.broadcasted_iota(jnp.int32, (rows[1] - rows[0], keys[1] - keys[0]), 0) + rows[0]
    c = lax.broadcasted_iota(jnp.int32, (rows[1] - rows[0], keys[1] - keys[0]), 1) + keys[0]
    return c <= r


def _diagonal_pieces(tb):
    half = tb // 2
    if half % LANES:
        return [((0, tb), (0, tb))]
    return [((0, half), (0, half)), ((half, tb), (0, tb))]


LOG2E = math.log2(math.e)
QK_TO_LOG2 = LOG2E / math.sqrt(HEAD_DIM)


def _attn_logits2(q, k, fk_row):
    sc = lax.dot_general(q, k, (((1,), (1,)), ((), ())), preferred_element_type=F32)
    return sc * QK_TO_LOG2 - fk_row * LOG2E


def _attn_fwd(z, f_row, n_heads, *, name, tb=1024, per_step=4, phases=()):
    s = z.shape[0]
    tb = _tile(s, tb, 128)
    nblk = s // tb
    qi, kj = _pairs(nblk, by_kv=False)

    def body(qi_ref, kj_ref, q_ref, k_ref, v_ref, fk_ref, o_ref, lse_ref, m_sc, l_sc, acc_sc):
        p = pl.program_id(1)
        i, j = qi_ref[p], kj_ref[p]

        @pl.when(j == 0)
        def _():
            m_sc[...] = jnp.full_like(m_sc, NEG_BIG)
            l_sc[...] = jnp.zeros_like(l_sc)
            acc_sc[...] = jnp.zeros_like(acc_sc)

        def update(head, rows, keys, masked):
            rs, ks, lanes = slice(*rows), slice(*keys), slice(head * HEAD_DIM, (head + 1) * HEAD_DIM)
            s2 = _attn_logits2(q_ref[rs, lanes], k_ref[ks, lanes], fk_ref[head, :, ks])
            if masked:
                s2 = jnp.where(_causal_mask(rows, keys), s2, NEG_BIG)
            m_old = m_sc[head, rs, :]
            m_new = jnp.maximum(m_old, jnp.max(s2, axis=-1, keepdims=True))
            alpha = jnp.exp2(m_old - m_new)
            pv = jnp.exp2(s2 - jnp.tile(m_new, (1, (keys[1] - keys[0]) // LANES)))
            l_sc[head, rs, :] = alpha * l_sc[head, rs, :] + jnp.sum(pv, axis=-1, keepdims=True)
            acc_sc[head, rs, :] = alpha * acc_sc[head, rs, :] + jnp.dot(pv.astype(BF16), v_ref[ks, lanes],
                                                                        preferred_element_type=F32)
            m_sc[head, rs, :] = m_new

        @pl.when(j < i)
        def _():
            for head in range(per_step):
                update(head, (0, tb), (0, tb), False)

        @pl.when(j == i)
        def _():
            for head in range(per_step):
                for rows, keys in _diagonal_pieces(tb):
                    update(head, rows, keys, True)
                o_ref[:, head * HEAD_DIM:(head + 1) * HEAD_DIM] = (acc_sc[head] / l_sc[head]).astype(BF16)
                lse_ref[head] = m_sc[head] + jnp.log2(l_sc[head])

    h = n_heads // per_step
    width = per_step * HEAD_DIM
    return _call(
        body, name=name, n_prefetch=2, grid=(h, int(qi.shape[0])),
        in_specs=[
            pl.BlockSpec((tb, width), lambda hh, p, qi_r, kj_r: (qi_r[p], hh)),
            pl.BlockSpec((tb, width), lambda hh, p, qi_r, kj_r: (kj_r[p], h + hh)),
            pl.BlockSpec((tb, width), lambda hh, p, qi_r, kj_r: (kj_r[p], 2 * h + hh)),
            pl.BlockSpec((per_step, 1, tb), lambda hh, p, qi_r, kj_r: (hh, 0, kj_r[p])),
        ],
        out_specs=[
            pl.BlockSpec((tb, width), lambda hh, p, qi_r, kj_r: (qi_r[p], hh)),
            pl.BlockSpec((per_step, tb, LANES), lambda hh, p, qi_r, kj_r: (hh, qi_r[p], 0)),
        ],
        scratch_shapes=[pltpu.VMEM((per_step, tb, LANES), F32), pltpu.VMEM((per_step, tb, LANES), F32),
                        pltpu.VMEM((per_step, tb, HEAD_DIM), F32)],
        out_shape=[jax.ShapeDtypeStruct((s, n_heads * HEAD_DIM), BF16), jax.ShapeDtypeStruct((n_heads, s, LANES), F32)],
        operands=[qi, kj, z, z, z, f_row], semantics=("parallel", "arbitrary"), phases=phases)


def _attn_bwd(z, o, d_o, lse2, f_row, n_heads, *, name, tb=1024, per_step=2, phases=()):
    s = z.shape[0]
    tb = _tile(s, tb, 128)
    nblk = s // tb
    qi, kj = _pairs(nblk, by_kv=True)
    n_pairs = int(qi.shape[0])
    scale = 1.0 / math.sqrt(HEAD_DIM)
    h = n_heads // per_step
    width = per_step * HEAD_DIM

    def body(qi_ref, kj_ref, q_ref, k_ref, v_ref, o_ref, do_ref, lse_ref, fk_ref,
             dq_ref, dk_ref, dv_ref, df_ref, dfq_ref, dq_sc, dk_sc, dv_sc, df_sc, dfq_sc):
        p = pl.program_id(1)
        i, j = qi_ref[p], kj_ref[p]

        @pl.when(p == 0)
        def _():
            dq_sc[...] = jnp.zeros_like(dq_sc)
            dfq_sc[...] = jnp.zeros_like(dfq_sc)

        @pl.when(i == j)
        def _():
            dk_sc[...] = jnp.zeros_like(dk_sc)
            dv_sc[...] = jnp.zeros_like(dv_sc)
            df_sc[...] = jnp.zeros_like(df_sc)

        def update(head, rows, keys, masked):
            rs, ks, n_rows = slice(*rows), slice(*keys), rows[1] - rows[0]
            lanes = slice(head * HEAD_DIM, (head + 1) * HEAD_DIM)
            q, k, v, do = q_ref[rs, lanes], k_ref[ks, lanes], v_ref[ks, lanes], do_ref[rs, lanes]
            delta = jnp.sum(do.astype(F32) * o_ref[rs, lanes].astype(F32), axis=-1, keepdims=True)
            pv = jnp.exp2(_attn_logits2(q, k, fk_ref[head, :, ks])
                          - jnp.tile(lse_ref[head, rs, :], (1, (keys[1] - keys[0]) // LANES)))
            if masked:
                pv = jnp.where(_causal_mask(rows, keys), pv, 0.0)
            dp = lax.dot_general(do, v, (((1,), (1,)), ((), ())), preferred_element_type=F32)
            ds = pv * (dp - delta)
            ds_b = ds.astype(BF16)
            dv_sc[head, ks, :] += lax.dot_general(pv.astype(BF16), do, (((0,), (0,)), ((), ())), preferred_element_type=F32)
            dk_sc[head, ks, :] += lax.dot_general(ds_b, q, (((0,), (0,)), ((), ())), preferred_element_type=F32)
            at = pl.ds(pl.multiple_of(i * tb + rows[0], LANES), n_rows)
            dq_sc[head, at, :] += jnp.dot(ds_b, k, preferred_element_type=F32)
            df_sc[head, :, ks] -= jnp.sum(ds, axis=0, keepdims=True)
            dfq_sc[head, at, :] += jnp.broadcast_to(jnp.sum(ds, axis=1, keepdims=True), (n_rows, LANES))

        @pl.when(i > j)
        def _():
            for head in range(per_step):
                update(head, (0, tb), (0, tb), False)

        @pl.when(i == j)
        def _():
            for head in range(per_step):
                for rows, keys in _diagonal_pieces(tb):
                    update(head, rows, keys, True)

        @pl.when(i == nblk - 1)
        def _():
            for head in range(per_step):
                lanes = slice(head * HEAD_DIM, (head + 1) * HEAD_DIM)
                dk_ref[:, lanes] = (dk_sc[head] * scale).astype(BF16)
                dv_ref[:, lanes] = dv_sc[head].astype(BF16)
            df_ref[...] = df_sc[...]

        @pl.when(p == n_pairs - 1)
        def _():
            for head in range(per_step):
                dq_ref[:, head * HEAD_DIM:(head + 1) * HEAD_DIM] = (dq_sc[head] * scale).astype(BF16)
                dfq_ref[head] = jnp.transpose(dfq_sc[head])[0:1, :]

    qblk = lambda off: pl.BlockSpec((tb, width), lambda hh, p, qi_r, kj_r: (qi_r[p], off + hh))
    kblk = lambda off: pl.BlockSpec((tb, width), lambda hh, p, qi_r, kj_r: (kj_r[p], off + hh))
    qrep = pl.BlockSpec((per_step, tb, LANES), lambda hh, p, qi_r, kj_r: (hh, qi_r[p], 0))
    krow = pl.BlockSpec((per_step, 1, tb), lambda hh, p, qi_r, kj_r: (hh, 0, kj_r[p]))
    act = jax.ShapeDtypeStruct((s, n_heads * HEAD_DIM), BF16)
    return _call(
        body, name=name, n_prefetch=2, grid=(h, n_pairs),
        in_specs=[qblk(0), kblk(h), kblk(2 * h), qblk(0), qblk(0), qrep, krow],
        out_specs=[
            pl.BlockSpec((s, width), lambda hh, p, qi_r, kj_r: (0, hh)),
            kblk(0), kblk(0), krow,
            pl.BlockSpec((per_step, 1, s), lambda hh, p, qi_r, kj_r: (hh, 0, 0)),
        ],
        scratch_shapes=[pltpu.VMEM((per_step, s, HEAD_DIM), F32), pltpu.VMEM((per_step, tb, HEAD_DIM), F32),
                        pltpu.VMEM((per_step, tb, HEAD_DIM), F32), pltpu.VMEM((per_step, 1, tb), F32),
                        pltpu.VMEM((per_step, s, LANES), F32)],
        out_shape=[act, act, act, jax.ShapeDtypeStruct((n_heads, 1, s), F32), jax.ShapeDtypeStruct((n_heads, 1, s), F32)],
        operands=[qi, kj, z, z, z, o, d_o, lse2, f_row], semantics=("parallel", "arbitrary"), phases=phases)


GELU_C = math.sqrt(2.0 / math.pi)
GELU_A = 0.044715


def _gelu(x):
    return 0.5 * x * (1.0 + jnp.tanh(GELU_C * (x + GELU_A * (x * x * x))))


def _gelu_and_grad(x):
    t = jnp.tanh(GELU_C * (x + GELU_A * (x * x * x)))
    y = 0.5 * x * (1.0 + t)
    dy = 0.5 * (1.0 + t) + 0.5 * x * (1.0 - t * t) * (GELU_C * (1.0 + 3.0 * GELU_A * (x * x)))
    return y, dy


def _layernorm_parts(g):
    mu = jnp.mean(g, axis=-1, keepdims=True)
    xc = g - mu
    rs = lax.rsqrt(jnp.mean(xc * xc, axis=-1, keepdims=True) + EPS)
    return xc * rs, rs


def _spatial_mix(w_ref, bcol_ref, vv_b, n_heads, n_chunks):
    tril = _causal_mask((0, CHUNK), (0, CHUNK))
    cols = []
    for hh in range(n_heads):
        wc = jnp.where(tril, w_ref[hh], 0.0).astype(BF16)
        lanes = slice(hh * HEAD_DIM, (hh + 1) * HEAD_DIM)
        rows = [jnp.dot(wc, vv_b[c * CHUNK:(c + 1) * CHUNK, lanes], preferred_element_type=F32)
                + bcol_ref[:, hh:hh + 1] for c in range(n_chunks)]
        cols.append(jnp.concatenate(rows, axis=0))
    return jnp.concatenate(cols, axis=1)


def _mix_fwd(z, o, ln_g, ln_b, w_s, b_col, attn_g, gm_g, n_heads, *, name, tr=256):
    s = z.shape[0]
    dg = n_heads * HEAD_DIM
    tr = _tile(s, tr, CHUNK)
    n_chunks = tr // CHUNK

    def body(zu_ref, zv_ref, o_ref, lg_ref, lb_ref, w_ref, bcol_ref, ag_ref, gg_ref, out_ref):
        u = _gelu(zu_ref[...].astype(F32))
        xhat, _ = _layernorm_parts(_gelu(zv_ref[...].astype(F32)))
        vv = xhat * lg_ref[...] + lb_ref[...]
        gm = u * _spatial_mix(w_ref, bcol_ref, vv.astype(BF16), n_heads, n_chunks)
        rg = lax.rsqrt(jnp.mean(gm * gm, axis=-1, keepdims=True) + EPS)
        ov = o_ref[...].astype(F32)
        ra = lax.rsqrt(jnp.mean(ov * ov, axis=-1, keepdims=True) + EPS)
        out_ref[:, :dg] = ((ov * ra) * ag_ref[...]).astype(BF16)
        out_ref[:, dg:] = ((gm * rg) * gg_ref[...]).astype(BF16)

    vec = pl.BlockSpec((1, dg), lambda i: (0, 0))
    return pl.pallas_call(
        body, name=name, out_shape=jax.ShapeDtypeStruct((s, 2 * dg), BF16), grid=(s // tr,),
        in_specs=[pl.BlockSpec((tr, dg), lambda i: (i, 3)), pl.BlockSpec((tr, dg), lambda i: (i, 4)),
                  pl.BlockSpec((tr, dg), lambda i: (i, 0)), vec, vec,
                  pl.BlockSpec((n_heads, CHUNK, CHUNK), lambda i: (0, 0, 0)),
                  pl.BlockSpec((CHUNK, n_heads), lambda i: (0, 0)), vec, vec],
        out_specs=pl.BlockSpec((tr, 2 * dg), lambda i: (i, 0)),
        compiler_params=_params("parallel"),
    )(z, z, o, ln_g, ln_b, w_s, b_col, attn_g, gm_g)


def _mix_bwd(z, o, d_merged, ln_g, ln_b, w_s, b_col, attn_g, gm_g, n_heads, *, name, tr=256):
    s = z.shape[0]
    dg = n_heads * HEAD_DIM
    tr = _tile(s, tr, CHUNK)
    n_chunks = tr // CHUNK

    def body(zu_ref, zv_ref, o_ref, dm_ref, lg_ref, lb_ref, w_ref, bcol_ref, ag_ref, gg_ref,
             do_ref, dzu_ref, dzv_ref, dw_ref, dbcol_ref, dlg_ref, dlb_ref, dag_ref, dgg_ref):
        @pl.when(pl.program_id(0) == 0)
        def _():
            for ref in (dw_ref, dbcol_ref, dlg_ref, dlb_ref, dag_ref, dgg_ref):
                ref[...] = jnp.zeros_like(ref)

        d_o, dag_rows = _rms_bwd_rows(dm_ref[:, :dg], o_ref[...].astype(F32), ag_ref[...])
        do_ref[...] = d_o.astype(BF16)
        dag_ref[...] += jnp.sum(dag_rows, axis=0, keepdims=True)

        u, du_dz = _gelu_and_grad(zu_ref[...].astype(F32))
        gv, dgv_dz = _gelu_and_grad(zv_ref[...].astype(F32))
        xhat, rs = _layernorm_parts(gv)
        lg = lg_ref[...]
        vv_b = (xhat * lg + lb_ref[...]).astype(BF16)
        mix = _spatial_mix(w_ref, bcol_ref, vv_b, n_heads, n_chunks)
        gm = u * mix
        d_gm, dgg_rows = _rms_bwd_rows(dm_ref[:, dg:], gm, gg_ref[...])
        dgg_ref[...] += jnp.sum(dgg_rows, axis=0, keepdims=True)
        dzu_ref[...] = ((d_gm * mix) * du_dz).astype(BF16)
        d_mix = d_gm * u
        d_mix_b = d_mix.astype(BF16)

        tril = _causal_mask((0, CHUNK), (0, CHUNK))
        lane = lax.broadcasted_iota(jnp.int32, (CHUNK, n_heads), 1)
        cols = []
        db = jnp.zeros((CHUNK, n_heads), F32)
        for hh in range(n_heads):
            wc = jnp.where(tril, w_ref[hh], 0.0).astype(BF16)
            lanes = slice(hh * HEAD_DIM, (hh + 1) * HEAD_DIM)
            dw = jnp.zeros((CHUNK, CHUNK), F32)
            dmix_sum = jnp.zeros((CHUNK, HEAD_DIM), F32)
            rows = []
            for c in range(n_chunks):
                rws = slice(c * CHUNK, (c + 1) * CHUNK)
                dmb = d_mix_b[rws, lanes]
                dw += lax.dot_general(dmb, vv_b[rws, lanes], (((1,), (1,)), ((), ())), preferred_element_type=F32)
                rows.append(lax.dot_general(wc, dmb, (((0,), (0,)), ((), ())), preferred_element_type=F32))
                dmix_sum += d_mix[rws, lanes]
            dw_ref[hh] += jnp.where(tril, dw, 0.0)
            db += jnp.where(lane == hh, jnp.sum(dmix_sum, axis=-1, keepdims=True), 0.0)
            cols.append(jnp.concatenate(rows, axis=0))
        dbcol_ref[...] += db
        d_vv = jnp.concatenate(cols, axis=1)

        dlg_ref[...] += jnp.sum(d_vv * xhat, axis=0, keepdims=True)
        dlb_ref[...] += jnp.sum(d_vv, axis=0, keepdims=True)
        d_xhat = d_vv * lg
        d_gv = rs * (d_xhat - jnp.mean(d_xhat, axis=-1, keepdims=True)
                     - xhat * jnp.mean(d_xhat * xhat, axis=-1, keepdims=True))
        dzv_ref[...] = (d_gv * dgv_dz).astype(BF16)

    vec = pl.BlockSpec((1, dg), lambda i: (0, 0))
    wspec = pl.BlockSpec((n_heads, CHUNK, CHUNK), lambda i: (0, 0, 0))
    bspec = pl.BlockSpec((CHUNK, n_heads), lambda i: (0, 0))
    rowb = pl.BlockSpec((tr, dg), lambda i: (i, 0))
    act = jax.ShapeDtypeStruct((s, dg), BF16)
    vshape = jax.ShapeDtypeStruct((1, dg), F32)
    return pl.pallas_call(
        body, name=name,
        out_shape=(act, act, act, jax.ShapeDtypeStruct((n_heads, CHUNK, CHUNK), F32),
                   jax.ShapeDtypeStruct((CHUNK, n_heads), F32), vshape, vshape, vshape, vshape),
        grid=(s // tr,),
        in_specs=[pl.BlockSpec((tr, dg), lambda i: (i, 3)), pl.BlockSpec((tr, dg), lambda i: (i, 4)),
                  rowb, pl.BlockSpec((tr, 2 * dg), lambda i: (i, 0)), vec, vec, wspec, bspec, vec, vec],
        out_specs=(rowb, rowb, rowb, wspec, bspec, vec, vec, vec, vec),
        compiler_params=_params("arbitrary"),
    )(z, z, o, d_merged, ln_g, ln_b, w_s, b_col, attn_g, gm_g)


def _place():
    x, y, c = lax.axis_index("x"), lax.axis_index("y"), lax.axis_index("c")
    other_chips = [(1 - x, y), (x, 1 - y), (1 - x, 1 - y)]
    return x, y, c, other_chips


def _remote(src, dst, send_sem, recv_sem, to):
    return pltpu.make_async_remote_copy(src_ref=src, dst_ref=dst, send_sem=send_sem, recv_sem=recv_sem,
                                        device_id=to, device_id_type=MESH)


def _cast_into_slot(w, place, *, name, phases=()):
    rows, cols = w.shape
    tr, tc = _rc_tile(rows, cols)

    def body(place_ref, w_ref, o_ref):
        o_ref[...] = w_ref[...].astype(BF16)

    return _only(_call(
        body, name=name, n_prefetch=1, grid=(rows // tr, cols // tc),
        in_specs=[pl.BlockSpec((tr, tc), lambda i, j, pr: (i, j))],
        out_specs=[pl.BlockSpec((None, tr, tc), lambda i, j, pr: (pr[0], i, j))],
        out_shape=[jax.ShapeDtypeStruct((N_CHIPS, rows, cols), BF16)], operands=[place, w],
        semantics=("parallel", "parallel"), phases=phases))


def _casts_and_norm(weights, place, x, g, *, name, rows=256, phases=()):
    cols = x.shape[1]
    jobs = [w.shape[0] // rows for w in weights] + [x.shape[0] // rows]
    assert all(w.shape[1] == cols and w.shape[0] % rows == 0 for w in weights) and x.shape[0] % rows == 0
    first = [sum(jobs[:k]) for k in range(len(jobs))]

    def strip(k):
        return lambda t: jnp.clip(t - first[k], 0, jobs[k] - 1)

    def body(place_ref, *refs):
        n = len(weights)
        w_refs, x_ref, g_ref, outs = refs[:n], refs[n], refs[n + 1], refs[n + 2:]
        t = pl.program_id(0)
        for k in range(n):
            @pl.when(jnp.logical_and(t >= first[k], t < first[k] + jobs[k]))
            def _(k=k):
                outs[k][...] = w_refs[k][...].astype(BF16)

        @pl.when(t >= first[n])
        def _():
            xv = x_ref[...]
            r = lax.rsqrt(jnp.mean(xv * xv, axis=-1, keepdims=True) + EPS)
            outs[n][...] = ((xv * r) * g_ref[...]).astype(BF16)

    in_specs = [pl.BlockSpec((rows, cols), lambda t, pr, k=k: (strip(k)(t), 0)) for k in range(len(weights))]
    in_specs += [pl.BlockSpec((rows, cols), lambda t, pr: (strip(len(weights))(t), 0)),
                 pl.BlockSpec((1, cols), lambda t, pr: (0, 0))]
    out_specs = [pl.BlockSpec((None, rows, cols), lambda t, pr, k=k: (pr[0], strip(k)(t), 0)) for k in range(len(weights))]
    out_specs.append(pl.BlockSpec((rows, cols), lambda t, pr: (strip(len(weights))(t), 0)))
    out_shape = [jax.ShapeDtypeStruct((N_CHIPS,) + w.shape, BF16) for w in weights] + [jax.ShapeDtypeStruct(x.shape, BF16)]
    return _call(body, name=name, n_prefetch=1, grid=(sum(jobs),), in_specs=in_specs, out_specs=out_specs,
                 out_shape=out_shape, operands=[place, *weights, x, g], semantics=("arbitrary",), phases=phases)


def _exchange(phases, *, name):
    comm_in = [a for ph in phases for a in ph.arrays]
    comm_out = [jax.ShapeDtypeStruct(s.shape, s.dtype) for ph in phases for s in (ph.arrays if ph.in_place else ph.out_shapes)]
    aliases, at_in, at_out = {}, 0, 0
    for ph in phases:
        if ph.in_place:
            aliases.update({at_in + r: at_out + r for r in range(len(ph.arrays))})
        at_in, at_out = at_in + len(ph.arrays), at_out + ph.n_out
    n_sems = sum(ph.n_sems for ph in phases)

    def body(*refs):
        cin, cout = refs[:len(comm_in)], refs[len(comm_in):len(comm_in) + len(comm_out)]
        send_sems, recv_sems = refs[len(comm_in) + len(comm_out):]
        _run_phases(phases, ("start", "finish"), cin, cout, send_sems, recv_sems)

    return pl.pallas_call(
        body, name=name, out_shape=tuple(comm_out), in_specs=[ANY] * len(comm_in), out_specs=tuple([ANY] * len(comm_out)),
        input_output_aliases=aliases,
        scratch_shapes=[pltpu.SemaphoreType.DMA((n_sems,)), pltpu.SemaphoreType.DMA((n_sems,))],
    )(*comm_in)


GATHER_PARTS = 4


def _gather(bufs, stage, part=(0, GATHER_PARTS)):
    n = 3 * len(bufs)
    lo, hi = part

    def copies(outs, send, recv, d2d, incoming):
        x, y, c, chips = _place()
        for t, buf in enumerate(outs):
            half = buf.shape[2] // 2
            piece = half // GATHER_PARTS
            for k, (cx, cy) in enumerate(chips):
                i = 3 * t + k + (n if (d2d and stage == "both") else 0)
                cols = pl.ds(((1 - c) if (d2d and incoming) else c) * half + lo * piece, (hi - lo) * piece)
                blk = buf.at[(2 * cx + cy) if (d2d or incoming) else (2 * x + y), :, cols]
                yield _remote(blk, blk, send(i), recv(i), (x, y, 1 - c) if d2d else (cx, cy, c))

    def start(ins, outs, send, recv):
        for cp in copies(outs, send, recv, stage == "d2d", False):
            cp.start()

    def finish(ins, outs, send, recv):
        if stage == "both":
            for arrival, onward in zip(copies(outs, send, recv, False, True), copies(outs, send, recv, True, False)):
                arrival.wait_recv()
                onward.start()
        for cp in copies(outs, send, recv, stage != "ici", True):
            cp.wait_recv()
        for d2d in ((False, True) if stage == "both" else (stage == "d2d",)):
            for cp in copies(outs, send, recv, d2d, False):
                cp.wait_send()

    return _Phase(bufs, [], True, (2 if stage == "both" else 1) * n, start, finish)


def _merge(first, second):
    n_first = first.n_sems

    def later(sem):
        return lambda i: sem(n_first + i)

    def start(ins, outs, send, recv):
        first.start(ins, outs, send, recv)
        second.start(ins, outs, later(send), later(recv))

    def finish(ins, outs, send, recv):
        first.finish(ins, outs, send, recv)
        second.finish(ins, outs, later(send), later(recv))

    return _Phase(first.arrays, [], True, n_first + second.n_sems, start, finish)


def _gather_by_parts(bufs, lo, hi):
    phase = _gather(bufs, "both", (lo, lo + 1))
    for part in range(lo + 1, hi):
        phase = _merge(phase, _gather(bufs, "both", (part, part + 1)))
    return phase


def _swap_halves(grads):
    def copies(ins, outs, send, recv):
        x, y, c, _ = _place()
        for t, g in enumerate(ins):
            half = g.shape[2] // 2
            yield _remote(g.at[:, :, pl.ds((1 - c) * half, half)], outs[t], send(t), recv(t), (x, y, 1 - c))

    def start(ins, outs, send, recv):
        for cp in copies(ins, outs, send, recv):
            cp.start()

    def finish(ins, outs, send, recv):
        for cp in copies(ins, outs, send, recv):
            cp.wait()

    shapes = [jax.ShapeDtypeStruct((a.shape[0], a.shape[1], a.shape[2] // 2), a.dtype) for a in grads]
    return _Phase(grads, shapes, False, len(grads), start, finish)


def _add_halves(grad, received, place, *, name):
    ns, rows, half = received.shape
    tr, tc = _rc_tile(rows, half, pref_rows=1024)
    per = half // tc

    def body(place_ref, g_ref, r_ref, o_ref):
        o_ref[...] = (g_ref[...].astype(F32) + r_ref[...].astype(F32)).astype(BF16)

    grid_spec = pltpu.PrefetchScalarGridSpec(
        num_scalar_prefetch=1, grid=(ns, rows // tr, per),
        in_specs=[pl.BlockSpec((None, tr, tc), lambda s, i, j, pr: (s, i, pr[1] * per + j)),
                  pl.BlockSpec((None, tr, tc), lambda s, i, j, pr: (s, i, j))],
        out_specs=pl.BlockSpec((None, tr, tc), lambda s, i, j, pr: (s, i, j)),
    )
    return pl.pallas_call(
        body, name=name, grid_spec=grid_spec, out_shape=jax.ShapeDtypeStruct(received.shape, BF16),
        compiler_params=_params("parallel", "parallel", "parallel"),
    )(place, grad, received)


def _send_partials(parts, piece=(0, 1)):
    k_th, n_pieces = piece

    def cols(part):
        width = part.shape[2] // n_pieces
        return pl.ds(k_th * width, width)

    def start(ins, outs, send, recv):
        x, y, c, chips = _place()
        for t, part in enumerate(ins):
            for k, (cx, cy) in enumerate(chips):
                _remote(part.at[2 * cx + cy, :, cols(part)], outs[t].at[2 * x + y],
                        send(3 * t + k), recv(3 * t + k), (cx, cy, c)).start()

    def finish(ins, outs, send, recv):
        x, y, c, chips = _place()
        for t, part in enumerate(ins):
            for k, (cx, cy) in enumerate(chips):
                slot = outs[t].at[2 * cx + cy]
                _remote(slot, slot, send(3 * t + k), recv(3 * t + k), (cx, cy, c)).wait_recv()
        for t, part in enumerate(ins):
            for k, (cx, cy) in enumerate(chips):
                sent = part.at[2 * cx + cy, :, cols(part)]
                _remote(sent, sent, send(3 * t + k), recv(3 * t + k), (cx, cy, c)).wait_send()

    shapes = [jax.ShapeDtypeStruct(a.shape[:2] + (a.shape[2] // n_pieces,), a.dtype) for a in parts]
    return _Phase(parts, shapes, False, 3 * len(parts), start, finish)


def _sum_chips(parts, slots, place, *, name, piece=(0, 1), into=None):
    ns, rows, width = slots.shape
    k_th, n_pieces = piece
    half = width * n_pieces
    tr, tc = _rc_tile(rows, width, pref_rows=512)
    per = width // tc

    def body(place_ref, p_ref, s_ref, *rest):
        acc = p_ref[...].astype(F32)
        for k in range(ns):
            acc = acc + jnp.where(place_ref[0] == k, 0.0, s_ref[k].astype(F32))
        rest[-1][...] = acc

    grid_spec = pltpu.PrefetchScalarGridSpec(
        num_scalar_prefetch=1, grid=(rows // tr, per),
        in_specs=[pl.BlockSpec((None, tr, tc), lambda i, j, pr: (pr[0], i, k_th * per + j)),
                  pl.BlockSpec((ns, tr, tc), lambda i, j, pr: (0, i, j))] + ([ANY] if into is not None else []),
        out_specs=pl.BlockSpec((tr, tc), lambda i, j, pr: (i, (pr[1] * n_pieces + k_th) * per + j)),
    )
    return pl.pallas_call(
        body, name=name, grid_spec=grid_spec, out_shape=jax.ShapeDtypeStruct((rows, 2 * half), F32),
        input_output_aliases={3: 0} if into is not None else {},
        compiler_params=_params("parallel", "parallel"),
    )(place, parts, slots, *([into] if into is not None else []))


def _join_halves(bufs):
    def copies(outs, send, recv, incoming):
        x, y, c, _ = _place()
        for t, buf in enumerate(outs):
            half = buf.shape[1] // 2
            cols = buf.at[:, pl.ds(((1 - c) if incoming else c) * half, half)]
            yield _remote(cols, cols, send(t), recv(t), (x, y, 1 - c))

    def start(ins, outs, send, recv):
        for cp in copies(outs, send, recv, False):
            cp.start()

    def finish(ins, outs, send, recv):
        for cp in copies(outs, send, recv, True):
            cp.wait_recv()
        for cp in copies(outs, send, recv, False):
            cp.wait_send()

    return _Phase(bufs, [], True, len(bufs), start, finish)


def _gather_small(buf):
    def slot(out, px, py, pc):
        return out.at[4 * px + 2 * py + pc]

    def start(ins, outs, send, recv):
        x, y, c, chips = _place()
        mine = slot(outs[0], x, y, c)
        _remote(ins[0], mine, send(0), recv(0), (x, y, 1 - c)).start()
        for k, (cx, cy) in enumerate(chips):
            _remote(ins[0], mine, send(1 + k), recv(1 + k), (cx, cy, c)).start()

    def finish(ins, outs, send, recv):
        x, y, c, chips = _place()
        sibling = (x, y, 1 - c)
        for k, (cx, cy) in enumerate(chips):
            arrived = slot(outs[0], cx, cy, c)
            _remote(arrived, arrived, send(1 + k), recv(1 + k), sibling).wait_recv()
            _remote(arrived, arrived, send(4 + k), recv(4 + k), sibling).start()
        theirs = slot(outs[0], x, y, 1 - c)
        _remote(theirs, theirs, send(0), recv(0), sibling).wait_recv()
        for k, (cx, cy) in enumerate(chips):
            passed = slot(outs[0], cx, cy, 1 - c)
            _remote(passed, passed, send(4 + k), recv(4 + k), sibling).wait_recv()
        for i in range(7):
            _remote(ins[0], ins[0], send(i), recv(i), sibling).wait_send()

    return _Phase([buf], [jax.ShapeDtypeStruct((N_DEV,) + buf.shape, buf.dtype)], False, 7, start, finish)


def _adamw_math(w, g, m, v):
    m = ADAM_B1 * m + (1.0 - ADAM_B1) * g
    v = ADAM_B2 * v + (1.0 - ADAM_B2) * (g * g)
    m_hat = m / (1.0 - ADAM_B1 ** ADAM_STEP)
    v_hat = v / (1.0 - ADAM_B2 ** ADAM_STEP)
    delta = -ADAM_LR * (m_hat / (jnp.sqrt(v_hat) + ADAM_EPS) + ADAM_WD * w)
    return delta, m, v


def _adamw(w, g, m, v, *, name):
    rows, cols = w.shape
    tr, tc = _rc_tile(rows, cols)

    def body(w_ref, g_ref, m_ref, v_ref, go_ref, d_ref, mo_ref, vo_ref):
        g = g_ref[...]
        go_ref[...] = g
        d_ref[...], mo_ref[...], vo_ref[...] = _adamw_math(w_ref[...], g, m_ref[...], v_ref[...])

    blk = pl.BlockSpec((tr, tc), lambda i, j: (i, j))
    shape = jax.ShapeDtypeStruct((rows, cols), F32)
    return pl.pallas_call(
        body, name=name, out_shape=(shape, shape, shape, shape), grid=(rows // tr, cols // tc),
        in_specs=[blk] * 4, out_specs=(blk, blk, blk, blk), compiler_params=_params("parallel", "parallel"),
    )(w, g, m, v)


def _adamw_small(gathered, own, place, w, m, v, *, name):
    nd = gathered.shape[0]

    def body(place_ref, gs_ref, own_ref, w_ref, m_ref, v_ref, g_ref, d_ref, mo_ref, vo_ref):
        me = 2 * place_ref[0] + place_ref[1]
        g = jnp.zeros(own_ref.shape, F32)
        for k in range(nd):
            g = g + jnp.where(me == k, own_ref[...], gs_ref[k])
        g_ref[...] = g
        d_ref[...], mo_ref[...], vo_ref[...] = _adamw_math(w_ref[...], g, m_ref[...], v_ref[...])

    whole = pl.BlockSpec(w.shape, lambda i, pr: (0, 0))
    grid_spec = pltpu.PrefetchScalarGridSpec(
        num_scalar_prefetch=1, grid=(1,),
        in_specs=[pl.BlockSpec(gathered.shape, lambda i, pr: (0, 0, 0)), whole, whole, whole, whole],
        out_specs=(whole, whole, whole, whole))
    shape = jax.ShapeDtypeStruct(w.shape, F32)
    return pl.pallas_call(body, name=name, grid_spec=grid_spec, out_shape=(shape, shape, shape, shape),
                          compiler_params=_params("arbitrary"))(place, gathered, own, w, m, v)


def _pack(parts):
    flat = jnp.concatenate([p.reshape(-1).astype(F32) for p in parts])
    rows = -(-flat.shape[0] // (8 * LANES)) * 8
    return jnp.pad(flat, (0, rows * LANES - flat.shape[0])).reshape(rows, LANES)


def _unpack(buf, shapes):
    flat = buf.reshape(-1)
    out, pos = [], 0
    for shp in shapes:
        size = int(np.prod(shp))
        out.append(flat[pos:pos + size].reshape(shp))
        pos += size
    return out


ROW_BLOCK = 256


def _realign_rows(sources, segments, out_shape, *, name):
    n_slots, rows, cols = out_shape
    n_src = len(sources)
    per_slot = -(-rows // ROW_BLOCK)
    table = np.zeros((6, n_slots * per_slot, n_src), np.int32)
    for so in range(n_slots):
        for first, last, src, src_slot, src_row in segments[so]:
            for blk in range(first // ROW_BLOCK, (last - 1) // ROW_BLOCK + 1):
                lo, hi = max(first, blk * ROW_BLOCK), min(last, (blk + 1) * ROW_BLOCK)
                base = src_row + (blk * ROW_BLOCK - first)
                m0 = (base + lo - blk * ROW_BLOCK) // ROW_BLOCK
                at = so * per_slot + blk
                assert table[4, at, src] == 0, "two segments of one block share a source operand"
                table[:, at, src] = (src_slot, m0, base - m0 * ROW_BLOCK, lo - blk * ROW_BLOCK, hi - blk * ROW_BLOCK,
                                     min(2 * ROW_BLOCK, sources[src].shape[1] - m0 * ROW_BLOCK))
    last_block = [-(-a.shape[1] // ROW_BLOCK) - 1 for a in sources]

    def body(slot_ref, blk_ref, off_ref, lo_ref, hi_ref, valid_ref, *refs):
        o_ref, acc = refs[2 * n_src], refs[2 * n_src + 1]
        at = (pl.program_id(0) * per_slot + pl.program_id(1)) * n_src
        acc[...] = jnp.zeros_like(acc)
        for p in range(n_src):
            @pl.when(hi_ref[at + p] > lo_ref[at + p])
            def _():
                two = jnp.concatenate([refs[2 * p][...], refs[2 * p + 1][...]], axis=0)
                src_row = lax.broadcasted_iota(jnp.int32, two.shape, 0)
                two = jnp.where(src_row < valid_ref[at + p], two, jnp.zeros_like(two))
                r = lax.broadcasted_iota(jnp.int32, (ROW_BLOCK, 2 * ROW_BLOCK), 0)
                c = lax.broadcasted_iota(jnp.int32, (ROW_BLOCK, 2 * ROW_BLOCK), 1)
                place = (c == r + off_ref[at + p]) & (r >= lo_ref[at + p]) & (r < hi_ref[at + p])
                acc[...] += jnp.dot(place.astype(two.dtype), two, preferred_element_type=F32)
        o_ref[...] = acc[...].astype(o_ref.dtype)

    def src_spec(p, second):
        def index(so, i, slot_r, blk_r, off_r, lo_r, hi_r, valid_r):
            at = (so * per_slot + i) * n_src + p
            return slot_r[at], jnp.minimum(blk_r[at] + second, last_block[p]), 0
        return pl.BlockSpec((None, ROW_BLOCK, cols), index)

    grid_spec = pltpu.PrefetchScalarGridSpec(
        num_scalar_prefetch=6, grid=(n_slots, per_slot),
        in_specs=[src_spec(p, second) for p in range(n_src) for second in (0, 1)],
        out_specs=pl.BlockSpec((None, ROW_BLOCK, cols), lambda so, i, *_: (so, i, 0)),
        scratch_shapes=[pltpu.VMEM((ROW_BLOCK, cols), F32)],
    )
    flat = [jnp.asarray(table[k].reshape(-1)) for k in range(6)]
    return pl.pallas_call(
        body, name=name, grid_spec=grid_spec, out_shape=jax.ShapeDtypeStruct(out_shape, sources[0].dtype),
        compiler_params=_params("parallel", "arbitrary"),
    )(*flat, *[a for a in sources for _ in (0, 1)])


def _shard_rows(g, lo, hi):
    rs = g.shape[1]
    pieces = []
    for j in range(g.shape[0]):
        a, b = max(lo, j * rs), min(hi, (j + 1) * rs)
        if a < b:
            pieces.append(g[j, a - j * rs:b - j * rs])
    return pieces


def kernel(x, norm_mix_g, w_in, b_f, gmlp_ln_g, gmlp_ln_b, w_s, b_s, attn_out_g, gmlp_out_g, w_out, norm_ffn_g, w_ff1, w_ff2, norm_final_g, loss_target, m_norm_mix_g, m_w_in, m_b_f, m_gmlp_ln_g, m_gmlp_ln_b, m_w_s, m_b_s, m_attn_out_g, m_gmlp_out_g, m_w_out, m_norm_ffn_g, m_w_ff1, m_w_ff2, m_norm_final_g, v_norm_mix_g, v_w_in, v_b_f, v_gmlp_ln_g, v_gmlp_ln_b, v_w_s, v_b_s, v_attn_out_g, v_gmlp_out_g, v_w_out, v_norm_ffn_g, v_w_ff1, v_w_ff2, v_norm_final_g):
    seq, d_model = x.shape[1], x.shape[2]
    d_attn = d_model // 2
    n_heads = d_attn // HEAD_DIM
    qkv = 3 * d_attn
    shard_cols = w_in.shape[2]
    assert N_CHIPS * shard_cols == qkv + n_heads + 2 * d_attn
    xs = x.reshape(seq, d_model)
    target = loss_target.reshape(seq, d_model)

    place = jnp.stack([2 * lax.axis_index("x") + lax.axis_index("y"), lax.axis_index("c")]).astype(jnp.int32)
    names = ["w_in", "w_out", "w_ff1", "w_ff2"]
    wt_in, mt_in, vt_in = w_in[0].T, m_w_in[0].T, v_w_in[0].T
    b_in, _ = _cast_into_slot(wt_in, place, name="cast_w_in")
    (b_out, b_ff1, b_ff2, h), (g_in,) = _casts_and_norm(
        [w_out[0], w_ff1[0], w_ff2[0]], place, xs, norm_mix_g, name="casts_and_norm_mix",
        phases=[_gather_by_parts([b_in], 0, GATHER_PARTS)])
    n_cols = N_CHIPS * shard_cols
    gate_slot, gate_row = divmod(qkv, shard_cols)
    assert gate_row + n_heads <= shard_cols
    pieces = []
    for j in range(N_CHIPS):
        if j == gate_slot:
            pieces += [(j, 0, gate_row), (j, gate_row + n_heads, shard_cols - gate_row - n_heads)]
        else:
            pieces.append((j, 0, shard_cols))
    fwd_segments, at = [[]], 0
    for order, (j, src_row, size) in enumerate(pieces):
        fwd_segments[0].append((at, at + size, order % 3, j, src_row))
        at += size
    wt_main = _realign_rows([g_in] * 3, fwd_segments, (1, n_cols - n_heads, d_model), name="w_in_rows")[0]
    wt_f = jnp.pad(jnp.concatenate(_shard_rows(g_in, qkv, qkv + n_heads), axis=0), ((0, LANES - n_heads), (0, 0)))
    b_f_pad = jnp.pad(b_f, ((0, 0), (0, LANES - n_heads)))
    b_col = b_s[0].T

    first, rest = (0, 1), (1, GATHER_PARTS)
    z, (b_out, b_ff1) = _matmul(h, wt_main, name="in_proj", out_dtype=BF16, trans_b=True, tm=2048,
                                phases=[_gather([b_out], "ici"), _gather([b_ff1], "ici", first)])
    zb, f_cum = _forget_fwd(h, wt_f, b_f_pad, name="forget_fwd")
    f_row = f_cum[:, :n_heads].T[:, None, :]
    (o, lse2), (b_ff1, b_out) = _attn_fwd(z, f_row, n_heads, name="attn_fwd",
                                          phases=[_gather([b_ff1], "ici", rest), _gather([b_out], "d2d")])
    merged = _mix_fwd(z, o, gmlp_ln_g, gmlp_ln_b, w_s[0], b_col, attn_out_g, gmlp_out_g, n_heads, name="mix_fwd")
    w_out_full = b_out.reshape(2 * d_attn, d_model)
    x1, (b_ff1, b_ff2) = _matmul(merged, w_out_full, name="out_proj", out_dtype=F32, residual=xs,
                                 phases=[_gather([b_ff1], "d2d"), _gather([b_ff2], "ici", first)])
    h2, _ = _rmsnorm_fwd(x1, norm_ffn_g, name="norm_ffn")
    a, (b_ff2,) = _matmul(h2, b_ff1, name="ff1", out_dtype=BF16, relu=True, b_sharded=True, tm=2048,
                          phases=[_merge(_gather([b_ff2], "d2d", first), _gather_by_parts([b_ff2], 1, GATHER_PARTS))])
    w_ff2_full = b_ff2.reshape(N_CHIPS * b_ff2.shape[1], d_model)
    x2, _ = _matmul(a, w_ff2_full, name="ff2", out_dtype=F32, square_lhs=True, residual=x1)
    dx2, dx2_b, dg_final, loss = _loss_and_final_bwd(x2, target, norm_final_g.reshape(1, d_model), name="loss_head")

    def pair_sum(g, r, nm):
        return _add_halves(g, r, place, name="grads_pair_sum_" + nm)

    def chip_sum(p, q, nm, **piece):
        return _sum_chips(p, q, place, name="grads_chip_sum_" + nm, **piece)

    dw_ff2, _ = _matmul(a, dx2_b, name="ff2_dw", out_dtype=BF16, trans_a=True, square_lhs=True)
    dw_ff2 = dw_ff2.reshape(N_CHIPS, -1, d_model)
    da, (r_ff2,) = _matmul(dx2_b, w_ff2_full, name="ff2_dlhs", out_dtype=BF16, trans_b=True, scale2_by=a, tm=2048,
                           phases=[_swap_halves([dw_ff2])])
    ps_ff2 = pair_sum(dw_ff2, r_ff2, "w_ff2")
    dh2, (q_ff2a,) = _matmul(da, b_ff1, name="ff1_dlhs", out_dtype=F32, trans_b=True, b_sharded=True,
                             phases=[_send_partials([ps_ff2], (0, 2))])
    dw_ff1, (q_ff2b,) = _matmul(h2, da, name="ff1_dw", out_dtype=BF16, trans_a=True, out_sharded=True, tk=seq,
                                phases=[_send_partials([ps_ff2], (1, 2))])
    g_ff2 = chip_sum(ps_ff2, q_ff2a, "w_ff2_a", piece=(0, 2))
    g_ff2 = chip_sum(ps_ff2, q_ff2b, "w_ff2_b", piece=(1, 2), into=g_ff2)
    (dx1, dg_ffn, dx1_b), (g_ff2,) = _rmsnorm_bwd(dh2, x1, dx2, norm_ffn_g, name="norm_ffn_bwd", rounded_copy=True,
                                                   phases=[_join_halves([g_ff2])])
    dw_out, _ = _matmul(merged, dx1_b, name="out_proj_dw", out_dtype=BF16, trans_a=True, tk=seq)
    dw_out = dw_out.reshape(N_CHIPS, -1, d_model)
    d_merged, (r_ff1, r_out) = _matmul(dx1_b, w_out_full, name="out_proj_dlhs", out_dtype=F32, trans_b=True,
                                       phases=[_swap_halves([dw_ff1, dw_out])])
    ps_ff1, ps_out = pair_sum(dw_ff1, r_ff1, "w_ff1"), pair_sum(dw_out, r_out, "w_out")
    d_o, dzu, dzv, dw_s, db_col, dlg, dlb, dag, dgg = _mix_bwd(
        z, o, d_merged, gmlp_ln_g, gmlp_ln_b, w_s[0], b_col, attn_out_g, gmlp_out_g, n_heads, name="mix_bwd")
    (dq, dk, dv, d_f_key, d_f_query), (q_ff1, q_out) = _attn_bwd(
        z, o, d_o, lse2, f_row, n_heads, name="attn_bwd", phases=[_send_partials([ps_ff1, ps_out])])
    g_ff1, g_out = chip_sum(ps_ff1, q_ff1, "w_ff1"), chip_sum(ps_out, q_out, "w_out")
    d_f = d_f_key.reshape(n_heads, seq) + d_f_query.reshape(n_heads, seq)
    d_f_pad = jnp.pad(d_f.T, ((0, 0), (0, LANES - n_heads)))
    dzf, db_f = _forget_bwd(d_f_pad, zb, name="forget_bwd")
    dz = jnp.concatenate([dq, dk, dv, dzu, dzv], axis=1)
    early_g = _pack([db_f[:, :n_heads], dlg, dlb, dw_s, db_col.T, dag, dgg, dg_ffn, dg_final])
    dwt_main, (g_ff1, g_out, early_all) = _matmul(dz, h, name="in_proj_dw", out_dtype=BF16, trans_a=True, tk=seq,
                                                  phases=[_join_halves([g_ff1, g_out]), _gather_small(early_g)])
    dwt_f, _ = _matmul(dzf, h, name="gate_dw", out_dtype=BF16, trans_a=True)
    bwd_segments = []
    for j in range(N_CHIPS):
        first = j * shard_cols
        if j < gate_slot:
            bwd_segments.append([(0, shard_cols, 0, 0, first)])
        elif j > gate_slot:
            bwd_segments.append([(0, shard_cols, 0, 0, first - n_heads)])
        else:
            bwd_segments.append([(0, gate_row, 0, 0, first), (gate_row, gate_row + n_heads, 1, 0, 0),
                                 (gate_row + n_heads, shard_cols, 2, 0, qkv)])
    dw_in = _realign_rows([dwt_main[None], dwt_f[None], dwt_main[None]], bwd_segments,
                          (N_CHIPS, shard_cols, d_model), name="dw_in_rows")
    dh_gate, (r_in,) = _matmul(dzf, wt_f, name="gate_dlhs", out_dtype=F32, phases=[_swap_halves([dw_in])])
    ps_in = pair_sum(dw_in, r_in, "w_in")
    dh, (q_in,) = _matmul(dz, wt_main, name="in_proj_dlhs", out_dtype=F32, residual=dh_gate, tk=2560,
                          phases=[_send_partials([ps_in])])
    g_in_sum = chip_sum(ps_in, q_in, "w_in")
    (grad_x, dg_mix), _ = _rmsnorm_bwd(dh, xs, dx1, norm_mix_g, name="norm_mix_bwd")
    late_g = _pack([dg_mix])
    g_in_sum, late_all = _exchange([_join_halves([g_in_sum]), _gather_small(late_g)], name="grads_join_w_in")

    big = {}
    for nm, g, w, m, v in zip(names, (g_in_sum, g_out, g_ff1, g_ff2), (wt_in, w_out[0], w_ff1[0], w_ff2[0]),
                              (mt_in, m_w_out[0], m_w_ff1[0], m_w_ff2[0]), (vt_in, v_w_out[0], v_w_ff1[0], v_w_ff2[0])):
        big[nm] = tuple((t.T if nm == "w_in" else t)[None] for t in _adamw(w, g, m, v, name="adamw_" + nm))

    small_params = dict(
        norm_mix_g=(norm_mix_g, m_norm_mix_g, v_norm_mix_g), b_f=(b_f, m_b_f, v_b_f),
        gmlp_ln_g=(gmlp_ln_g, m_gmlp_ln_g, v_gmlp_ln_g), gmlp_ln_b=(gmlp_ln_b, m_gmlp_ln_b, v_gmlp_ln_b),
        w_s=(w_s, m_w_s, v_w_s), b_s=(b_s, m_b_s, v_b_s), attn_out_g=(attn_out_g, m_attn_out_g, v_attn_out_g),
        gmlp_out_g=(gmlp_out_g, m_gmlp_out_g, v_gmlp_out_g), norm_ffn_g=(norm_ffn_g, m_norm_ffn_g, v_norm_ffn_g),
        norm_final_g=(norm_final_g, m_norm_final_g, v_norm_final_g))

    def small_step(group, grads_all, grads_own, label):
        w, m, v = ([small_params[nm][k] for nm in group] for k in range(3))
        packed = _adamw_small(grads_all, grads_own, place, _pack(w), _pack(m), _pack(v), name="adamw_small_" + label)
        parts = [_unpack(p, [a.shape for a in w]) for p in packed]
        return {nm: tuple(part[i] for part in parts) for i, nm in enumerate(group)}

    early = ["b_f", "gmlp_ln_g", "gmlp_ln_b", "w_s", "b_s", "attn_out_g", "gmlp_out_g", "norm_ffn_g", "norm_final_g"]
    small = {**small_step(early, early_all, early_g, "early"), **small_step(["norm_mix_g"], late_all, late_g, "late")}

    order = ["norm_mix_g", "w_in", "b_f", "gmlp_ln_g", "gmlp_ln_b", "w_s", "b_s", "attn_out_g", "gmlp_out_g", "w_out",
             "norm_ffn_g", "w_ff1", "w_ff2", "norm_final_g"]
    result = {**small, **big}
    total_loss = lax.psum(loss[0, 0], ("x", "y", "c"))
    outs = [total_loss, grad_x.reshape(x.shape)]
    for part in range(4):
        outs += [result[nm][part] for nm in order]
    return tuple(outs)
```
